```python
import math
import jax, jax.numpy as jnp
from jax import lax
import numpy as np

D_MODEL = 2048
BATCH = 8
SEQ = 8192
DEPTH = 2

D_MIX = D_MODEL
BR = D_MIX // 4
CONV_A_WIDTH = 3
ATT_HEADS = 8
ATT_HEAD_DIM = BR // ATT_HEADS
DILATIONS = ((128, 1), (512, 4), (2048, 16))
BLK = 128
REL_BUCKETS = 32
REL_MAX_DIST = 2048
LRU_HEADS = 8
LRU_HEAD_DIM = BR // LRU_HEADS
CONV_C_WIDTH = 4
LRU_C = 8.0
S5_CH = 16
S5_GROUPS = BR // S5_CH
S5_STATE = 64
N_IN = 4 * BR + 4 * BR + 2 * BR + 2 * BR
ALPHA = (2 * DEPTH) ** 0.25
BETA = (8 * DEPTH) ** -0.25
LN_EPS = 1e-5

kernel_name = "hybrid_parallel_conv_dilattn_rglru_s5"


def layer_norm(x, g, b):
    xf = x.astype(jnp.float32)
    mu = jnp.mean(xf, axis=-1, keepdims=True)
    var = jnp.mean(jnp.square(xf - mu), axis=-1, keepdims=True)
    return ((xf - mu) * lax.rsqrt(var + LN_EPS)).astype(x.dtype) * g + b


def causal_dwconv(x, w):
    K = w.shape[0]
    S = x.shape[1]
    xp = jnp.pad(x, ((0, 0), (K - 1, 0), (0, 0)))
    y = xp[:, :S] * w[0]
    for j in range(1, K):
        y = y + xp[:, j:j + S] * w[j]
    return y


def t5_bucket(dist):
    max_exact = REL_BUCKETS // 2
    nf = jnp.maximum(dist, 1).astype(jnp.float32)
    large = max_exact + (jnp.log(nf / max_exact) / math.log(REL_MAX_DIST / max_exact)
                         * (REL_BUCKETS - max_exact)).astype(jnp.int32)
    large = jnp.minimum(large, REL_BUCKETS - 1)
    return jnp.where(dist < max_exact, dist, large)


def dilated_group(q, k, v, rel_bias, window, dil):
    Bsz, S, H, hd = q.shape
    span = window // dil
    assert span <= BLK
    unit = dil * BLK
    S_pad = -(-S // unit) * unit
    nb = S_pad // unit
    pad = ((0, 0), (0, S_pad - S), (0, 0), (0, 0))

    def split(t):
        return jnp.pad(t, pad).reshape(Bsz, nb, BLK, dil, H, hd)

    def with_prev(t):
        prev = jnp.pad(t, ((0, 0), (1, 0), (0, 0), (0, 0), (0, 0), (0, 0)))[:, :-1]
        return jnp.concatenate([prev, t], axis=2)

    qb = split(q)
    kk = with_prev(split(k))
    vv = with_prev(split(v))
    s = jnp.einsum('bnirhd,bnjrhd->bnrhij', qb, kk).astype(jnp.float32)

    i = jnp.arange(BLK)[:, None]
    j = jnp.arange(2 * BLK)[None, :]
    delta = i + BLK - j
    bucket = t5_bucket(jnp.clip(delta, 0, span) * dil)
    bias = jnp.transpose(rel_bias[bucket], (2, 0, 1)).astype(jnp.float32)
    valid = (delta >= 0) & (delta <= span)
    has_prev = (jnp.arange(nb)[:, None, None] > 0) | (j >= BLK)[None]
    mask = valid[None] & has_prev

    s = jnp.where(mask[None, :, None, None], s + bias, -1e30)
    m = jnp.max(s, axis=-1, keepdims=True)
    p = jnp.exp(s - m)
    l = jnp.sum(p, axis=-1, keepdims=True)
    o = jnp.einsum('bnrhij,bnjrhd->bnirhd', (p / l).astype(v.dtype), vv)
    lse = (m + jnp.log(l))[..., 0]
    lse = jnp.transpose(lse, (0, 1, 4, 2, 3)).reshape(Bsz, S_pad, H)[:, :S]
    o = o.reshape(Bsz, S_pad, H, hd)[:, :S]
    return o, lse


def dilated_attention(q, k, v, rel_bias):
    outs, lses = [], []
    for window, dil in DILATIONS:
        o, lse = dilated_group(q, k, v, rel_bias, window, dil)
        outs.append(o)
        lses.append(lse)
    w = jax.nn.softmax(jnp.stack(lses, axis=0), axis=0)
    return jnp.einsum('gbsh,gbshd->bshd', w.astype(q.dtype), jnp.stack(outs, axis=0))


def linear_combine(e1, e2):
    a1, b1 = e1
    a2, b2 = e2
    return a1 * a2, a2 * b1 + b2


def complex_combine(e1, e2):
    ar1, ai1, br1, bi1 = e1
    ar2, ai2, br2, bi2 = e2
    return (ar2 * ar1 - ai2 * ai1,
            ar2 * ai1 + ai2 * ar1,
            ar2 * br1 - ai2 * bi1 + br2,
            ar2 * bi1 + ai2 * br1 + bi2)


def rglru_branch(xb, conv_w, conv_b, wa, ba, wx, bx, lam):
    Bsz, S, _ = xb.shape
    xc = causal_dwconv(xb, conv_w) + conv_b
    xh = xc.reshape(Bsz, S, LRU_HEADS, LRU_HEAD_DIM)
    r = jax.nn.sigmoid(jnp.einsum('bshi,hij->bshj', xh, wa).reshape(Bsz, S, BR) + ba)
    ig = jax.nn.sigmoid(jnp.einsum('bshi,hij->bshj', xh, wx).reshape(Bsz, S, BR) + bx)
    log_a = -LRU_C * r * jax.nn.softplus(-lam)
    a = jnp.exp(log_a)
    mult = jnp.sqrt(-jnp.expm1(2.0 * log_a))
    _, h = lax.associative_scan(linear_combine, (a, mult * ig * xc), axis=1)
    return h


def s5_branch(u, lam_re, lam_im, log_dt, b_re, b_im, c_re, c_im, d_skip, w_glu, b_glu):
    Bsz, S, _ = u.shape
    ug = u.reshape(Bsz, S, S5_GROUPS, S5_CH)
    dt = jnp.exp(log_dt)[:, None]
    mag = jnp.exp(lam_re * dt)
    ab_re = mag * jnp.cos(lam_im * dt)
    ab_im = mag * jnp.sin(lam_im * dt)
    den = lam_re * lam_re + lam_im * lam_im
    f_re = ((ab_re - 1.0) * lam_re + ab_im * lam_im) / den
    f_im = (ab_im * lam_re - (ab_re - 1.0) * lam_im) / den
    bb_re = f_re[..., None] * b_re - f_im[..., None] * b_im
    bb_im = f_re[..., None] * b_im + f_im[..., None] * b_re
    bu_re = jnp.einsum('bsgc,gpc->bsgp', ug, bb_re)
    bu_im = jnp.einsum('bsgc,gpc->bsgp', ug, bb_im)
    a_re = jnp.broadcast_to(ab_re, bu_re.shape)
    a_im = jnp.broadcast_to(ab_im, bu_im.shape)
    _, _, x_re, x_im = lax.associative_scan(complex_combine, (a_re, a_im, bu_re, bu_im), axis=1)
    y = jnp.einsum('gcp,bsgp->bsgc', c_re, x_re) - jnp.einsum('gcp,bsgp->bsgc', c_im, x_im)
    y = y.reshape(Bsz, S, BR) + d_skip * u
    y = jax.nn.gelu(y)
    return y * jax.nn.sigmoid(y @ w_glu + b_glu)


def _fwd_setup_inputs(seed: int = 0) -> dict:
    key = jax.random.key(seed)
    ks = jax.random.split(key, 32)

    def nrm(k, shape, scale):
        return scale * jax.random.normal(k, shape, jnp.float32)

    HD = LRU_HEAD_DIM
    a_c = jax.random.uniform(ks[13], (DEPTH, BR), jnp.float32, minval=0.9, maxval=0.999)
    a0 = a_c ** (1.0 / LRU_C)
    lru_lambda = jnp.log(a0) - jnp.log1p(-a0)
    n_idx = jnp.arange(S5_STATE, dtype=jnp.float32)
    return {
        "x": nrm(ks[0], (BATCH, SEQ, D_MODEL), 1.0),
        "c": nrm(ks[1], (BATCH, D_MODEL), 1.0),
        "rel_bias": nrm(ks[2], (REL_BUCKETS, ATT_HEADS), 0.5),
        "w_ada": nrm(ks[3], (DEPTH, D_MODEL, 3 * D_MODEL), 0.1 * D_MODEL ** -0.5),
        "b_ada": nrm(ks[4], (DEPTH, 3 * D_MODEL), 0.01),
        "w_in": nrm(ks[5], (DEPTH, D_MODEL, N_IN), D_MODEL ** -0.5),
        "conv_a": nrm(ks[6], (DEPTH, CONV_A_WIDTH, BR), CONV_A_WIDTH ** -0.5),
        "conv_c": nrm(ks[7], (DEPTH, CONV_C_WIDTH, BR), CONV_C_WIDTH ** -0.5),
        "conv_c_b": nrm(ks[8], (DEPTH, BR), 0.01),
        "lru_wa": nrm(ks[9], (DEPTH, LRU_HEADS, HD, HD), HD ** -0.5),
        "lru_ba": nrm(ks[10], (DEPTH, BR), 0.01),
        "lru_wx": nrm(ks[11], (DEPTH, LRU_HEADS, HD, HD), HD ** -0.5),
        "lru_bx": nrm(ks[12], (DEPTH, BR), 0.01),
        "lru_lambda": lru_lambda,
        "s5_lam_re": -0.5 + nrm(ks[14], (DEPTH, S5_GROUPS, S5_STATE), 0.01),
        "s5_lam_im": jnp.pi * n_idx + nrm(ks[15], (DEPTH, S5_GROUPS, S5_STATE), 0.01),
        "s5_log_dt": jax.random.uniform(ks[16], (DEPTH, S5_GROUPS), jnp.float32,
                                        minval=math.log(1e-3), maxval=math.log(1e-1)),
        "s5_b_re": nrm(ks[17], (DEPTH, S5_GROUPS, S5_STATE, S5_CH), (2 * S5_CH) ** -0.5),
        "s5_b_im": nrm(ks[18], (DEPTH, S5_GROUPS, S5_STATE, S5_CH), (2 * S5_CH) ** -0.5),
        "s5_c_re": nrm(ks[19], (DEPTH, S5_GROUPS, S5_CH, S5_STATE), (2 * S5_STATE) ** -0.5),
        "s5_c_im": nrm(ks[20], (DEPTH, S5_GROUPS, S5_CH, S5_STATE), (2 * S5_STATE) ** -0.5),
        "s5_d": nrm(ks[21], (DEPTH, BR), 1.0),
        "s5_w_glu": nrm(ks[22], (DEPTH, BR, BR), BR ** -0.5),
        "s5_b_glu": nrm(ks[23], (DEPTH, BR), 0.01),
        "w_out": nrm(ks[24], (DEPTH, D_MIX, D_MODEL), BETA * D_MIX ** -0.5),
        "ln_g": 1.0 + nrm(ks[25], (DEPTH, D_MODEL), 0.01),
        "ln_b": nrm(ks[26], (DEPTH, D_MODEL), 0.01),
    }


def _fwd_reference(x, c, rel_bias, w_ada, b_ada, w_in, conv_a, conv_c, conv_c_b, lru_wa, lru_ba,
              lru_wx, lru_bx, lru_lambda, s5_lam_re, s5_lam_im, s5_log_dt, s5_b_re, s5_b_im,
              s5_c_re, s5_c_im, s5_d, s5_w_glu, s5_b_glu, w_out, ln_g, ln_b):
    Bsz, S, _ = x.shape
    cond = jax.nn.silu(c)
    for l in range(DEPTH):
        ada = cond @ w_ada[l] + b_ada[l]
        shift, scale, gate = jnp.split(ada, 3, axis=-1)
        h = x * (1.0 + scale[:, None]) + shift[:, None]
        proj = h @ w_in[l]
        pa, pb, pc, pd = jnp.split(proj, [4 * BR, 8 * BR, 10 * BR], axis=-1)

        a_b, a_c, a_x, a_g = jnp.split(pa, 4, axis=-1)
        y_a = a_b * causal_dwconv(a_c * a_x, conv_a[l]) * jax.nn.silu(a_g)

        q, k, v, b_g = jnp.split(pb, 4, axis=-1)
        q = q.reshape(Bsz, S, ATT_HEADS, ATT_HEAD_DIM) * (ATT_HEAD_DIM ** -0.5)
        k = k.reshape(Bsz, S, ATT_HEADS, ATT_HEAD_DIM)
        v = v.reshape(Bsz, S, ATT_HEADS, ATT_HEAD_DIM)
        y_b = dilated_attention(q, k, v, rel_bias).reshape(Bsz, S, BR) * jax.nn.silu(b_g)

        c_x, c_g = jnp.split(pc, 2, axis=-1)
        y_c = rglru_branch(c_x, conv_c[l], conv_c_b[l], lru_wa[l], lru_ba[l], lru_wx[l],
                           lru_bx[l], lru_lambda[l]) * jax.nn.silu(c_g)

        d_u, d_g = jnp.split(pd, 2, axis=-1)
        y_d = s5_branch(d_u, s5_lam_re[l], s5_lam_im[l], s5_log_dt[l], s5_b_re[l], s5_b_im[l],
                        s5_c_re[l], s5_c_im[l], s5_d[l], s5_w_glu[l], s5_b_glu[l]) * jax.nn.silu(d_g)

        y = jnp.concatenate([y_a, y_b, y_c, y_d], axis=-1) @ w_out[l]
        x = layer_norm(ALPHA * x + (1.0 + gate[:, None]) * y, ln_g[l], ln_b[l])
    return x


import jax as _jax
import jax.numpy as _jnp

TWIN_FORMAT = 'train_step'
FWD_PARAMS = ['x', 'c', 'rel_bias', 'w_ada', 'b_ada', 'w_in', 'conv_a', 'conv_c', 'conv_c_b', 'lru_wa', 'lru_ba', 'lru_wx', 'lru_bx', 'lru_lambda', 's5_lam_re', 's5_lam_im', 's5_log_dt', 's5_b_re', 's5_b_im', 's5_c_re', 's5_c_im', 's5_d', 's5_w_glu', 's5_b_glu', 'w_out', 'ln_g', 'ln_b']
TWIN_WEIGHTS = ['rel_bias', 'w_ada', 'b_ada', 'w_in', 'conv_a', 'conv_c', 'conv_c_b', 'lru_wa', 'lru_ba', 'lru_wx', 'lru_bx', 'lru_lambda', 's5_lam_re', 's5_lam_im', 's5_log_dt', 's5_b_re', 's5_b_im', 's5_c_re', 's5_c_im', 's5_d', 's5_w_glu', 's5_b_glu', 'w_out', 'ln_g', 'ln_b']
TWIN_DIFF_INPUT = 'x'
TWIN_INPUTS = ['x', 'c', 'rel_bias', 'w_ada', 'b_ada', 'w_in', 'conv_a', 'conv_c', 'conv_c_b', 'lru_wa', 'lru_ba', 'lru_wx', 'lru_bx', 'lru_lambda', 's5_lam_re', 's5_lam_im', 's5_log_dt', 's5_b_re', 's5_b_im', 's5_c_re', 's5_c_im', 's5_d', 's5_w_glu', 's5_b_glu', 'w_out', 'ln_g', 'ln_b', 'loss_target', 'm_rel_bias', 'm_w_ada', 'm_b_ada', 'm_w_in', 'm_conv_a', 'm_conv_c', 'm_conv_c_b', 'm_lru_wa', 'm_lru_ba', 'm_lru_wx', 'm_lru_bx', 'm_lru_lambda', 'm_s5_lam_re', 'm_s5_lam_im', 'm_s5_log_dt', 'm_s5_b_re', 'm_s5_b_im', 'm_s5_c_re', 'm_s5_c_im', 'm_s5_d', 'm_s5_w_glu', 'm_s5_b_glu', 'm_w_out', 'm_ln_g', 'm_ln_b', 'v_rel_bias', 'v_w_ada', 'v_b_ada', 'v_w_in', 'v_conv_a', 'v_conv_c', 'v_conv_c_b', 'v_lru_wa', 'v_lru_ba', 'v_lru_wx', 'v_lru_bx', 'v_lru_lambda', 'v_s5_lam_re', 'v_s5_lam_im', 'v_s5_log_dt', 'v_s5_b_re', 'v_s5_b_im', 'v_s5_c_re', 'v_s5_c_im', 'v_s5_d', 'v_s5_w_glu', 'v_s5_b_glu', 'v_w_out', 'v_ln_g', 'v_ln_b']
TWIN_OUTPUTS = ['loss', 'grad_x', 'grad_rel_bias', 'grad_w_ada', 'grad_b_ada', 'grad_w_in', 'grad_conv_a', 'grad_conv_c', 'grad_conv_c_b', 'grad_lru_wa', 'grad_lru_ba', 'grad_lru_wx', 'grad_lru_bx', 'grad_lru_lambda', 'grad_s5_lam_re', 'grad_s5_lam_im', 'grad_s5_log_dt', 'grad_s5_b_re', 'grad_s5_b_im', 'grad_s5_c_re', 'grad_s5_c_im', 'grad_s5_d', 'grad_s5_w_glu', 'grad_s5_b_glu', 'grad_w_out', 'grad_ln_g', 'grad_ln_b', 'delta_rel_bias', 'delta_w_ada', 'delta_b_ada', 'delta_w_in', 'delta_conv_a', 'delta_conv_c', 'delta_conv_c_b', 'delta_lru_wa', 'delta_lru_ba', 'delta_lru_wx', 'delta_lru_bx', 'delta_lru_lambda', 'delta_s5_lam_re', 'delta_s5_lam_im', 'delta_s5_log_dt', 'delta_s5_b_re', 'delta_s5_b_im', 'delta_s5_c_re', 'delta_s5_c_im', 'delta_s5_d', 'delta_s5_w_glu', 'delta_s5_b_glu', 'delta_w_out', 'delta_ln_g', 'delta_ln_b', 'new_m_rel_bias', 'new_m_w_ada', 'new_m_b_ada', 'new_m_w_in', 'new_m_conv_a', 'new_m_conv_c', 'new_m_conv_c_b', 'new_m_lru_wa', 'new_m_lru_ba', 'new_m_lru_wx', 'new_m_lru_bx', 'new_m_lru_lambda', 'new_m_s5_lam_re', 'new_m_s5_lam_im', 'new_m_s5_log_dt', 'new_m_s5_b_re', 'new_m_s5_b_im', 'new_m_s5_c_re', 'new_m_s5_c_im', 'new_m_s5_d', 'new_m_s5_w_glu', 'new_m_s5_b_glu', 'new_m_w_out', 'new_m_ln_g', 'new_m_ln_b', 'new_v_rel_bias', 'new_v_w_ada', 'new_v_b_ada', 'new_v_w_in', 'new_v_conv_a', 'new_v_conv_c', 'new_v_conv_c_b', 'new_v_lru_wa', 'new_v_lru_ba', 'new_v_lru_wx', 'new_v_lru_bx', 'new_v_lru_lambda', 'new_v_s5_lam_re', 'new_v_s5_lam_im', 'new_v_s5_log_dt', 'new_v_s5_b_re', 'new_v_s5_b_im', 'new_v_s5_c_re', 'new_v_s5_c_im', 'new_v_s5_d', 'new_v_s5_w_glu', 'new_v_s5_b_glu', 'new_v_w_out', 'new_v_ln_g', 'new_v_ln_b']
TWIN_LEAF_KINDS = {'loss': 'loss', 'grad_x': 'grad_x', 'grad_rel_bias': 'grad_w', 'grad_w_ada': 'grad_w', 'grad_b_ada': 'grad_w', 'grad_w_in': 'grad_w', 'grad_conv_a': 'grad_w', 'grad_conv_c': 'grad_w', 'grad_conv_c_b': 'grad_w', 'grad_lru_wa': 'grad_w', 'grad_lru_ba': 'grad_w', 'grad_lru_wx': 'grad_w', 'grad_lru_bx': 'grad_w', 'grad_lru_lambda': 'grad_w', 'grad_s5_lam_re': 'grad_w', 'grad_s5_lam_im': 'grad_w', 'grad_s5_log_dt': 'grad_w', 'grad_s5_b_re': 'grad_w', 'grad_s5_b_im': 'grad_w', 'grad_s5_c_re': 'grad_w', 'grad_s5_c_im': 'grad_w', 'grad_s5_d': 'grad_w', 'grad_s5_w_glu': 'grad_w', 'grad_s5_b_glu': 'grad_w', 'grad_w_out': 'grad_w', 'grad_ln_g': 'grad_w', 'grad_ln_b': 'grad_w', 'delta_rel_bias': 'delta_w', 'delta_w_ada': 'delta_w', 'delta_b_ada': 'delta_w', 'delta_w_in': 'delta_w', 'delta_conv_a': 'delta_w', 'delta_conv_c': 'delta_w', 'delta_conv_c_b': 'delta_w', 'delta_lru_wa': 'delta_w', 'delta_lru_ba': 'delta_w', 'delta_lru_wx': 'delta_w', 'delta_lru_bx': 'delta_w', 'delta_lru_lambda': 'delta_w', 'delta_s5_lam_re': 'delta_w', 'delta_s5_lam_im': 'delta_w', 'delta_s5_log_dt': 'delta_w', 'delta_s5_b_re': 'delta_w', 'delta_s5_b_im': 'delta_w', 'delta_s5_c_re': 'delta_w', 'delta_s5_c_im': 'delta_w', 'delta_s5_d': 'delta_w', 'delta_s5_w_glu': 'delta_w', 'delta_s5_b_glu': 'delta_w', 'delta_w_out': 'delta_w', 'delta_ln_g': 'delta_w', 'delta_ln_b': 'delta_w', 'new_m_rel_bias': 'new_m', 'new_m_w_ada': 'new_m', 'new_m_b_ada': 'new_m', 'new_m_w_in': 'new_m', 'new_m_conv_a': 'new_m', 'new_m_conv_c': 'new_m', 'new_m_conv_c_b': 'new_m', 'new_m_lru_wa': 'new_m', 'new_m_lru_ba': 'new_m', 'new_m_lru_wx': 'new_m', 'new_m_lru_bx': 'new_m', 'new_m_lru_lambda': 'new_m', 'new_m_s5_lam_re': 'new_m', 'new_m_s5_lam_im': 'new_m', 'new_m_s5_log_dt': 'new_m', 'new_m_s5_b_re': 'new_m', 'new_m_s5_b_im': 'new_m', 'new_m_s5_c_re': 'new_m', 'new_m_s5_c_im': 'new_m', 'new_m_s5_d': 'new_m', 'new_m_s5_w_glu': 'new_m', 'new_m_s5_b_glu': 'new_m', 'new_m_w_out': 'new_m', 'new_m_ln_g': 'new_m', 'new_m_ln_b': 'new_m', 'new_v_rel_bias': 'new_v', 'new_v_w_ada': 'new_v', 'new_v_b_ada': 'new_v', 'new_v_w_in': 'new_v', 'new_v_conv_a': 'new_v', 'new_v_conv_c': 'new_v', 'new_v_conv_c_b': 'new_v', 'new_v_lru_wa': 'new_v', 'new_v_lru_ba': 'new_v', 'new_v_lru_wx': 'new_v', 'new_v_lru_bx': 'new_v', 'new_v_lru_lambda': 'new_v', 'new_v_s5_lam_re': 'new_v', 'new_v_s5_lam_im': 'new_v', 'new_v_s5_log_dt': 'new_v', 'new_v_s5_b_re': 'new_v', 'new_v_s5_b_im': 'new_v', 'new_v_s5_c_re': 'new_v', 'new_v_s5_c_im': 'new_v', 'new_v_s5_d': 'new_v', 'new_v_s5_w_glu': 'new_v', 'new_v_s5_b_glu': 'new_v', 'new_v_w_out': 'new_v', 'new_v_ln_g': 'new_v', 'new_v_ln_b': 'new_v'}


def _forward(args):
    return _fwd_reference(*[args[k] for k in FWD_PARAMS])


def _output_shape():
    def fwd():
        inp = _fwd_setup_inputs(0)
        return _fwd_reference(*[inp[k] for k in FWD_PARAMS])
    out = _jax.eval_shape(fwd)
    return out.shape, out.dtype

N_MICROBATCH = 1
ADAM_LR = 0.001
ADAM_B1 = 0.9
ADAM_B2 = 0.999
ADAM_EPS = 1e-08
ADAM_WD = 0.01
ADAM_STEP = 10
PER_EXAMPLE_BATCH_AXIS = {'x': 0, 'c': 0, 'loss_target': 0}
SHARED_INPUTS = []
_WEIGHT_DTYPES = {'rel_bias': _jnp.float32, 'w_ada': _jnp.float32, 'b_ada': _jnp.float32, 'w_in': _jnp.float32, 'conv_a': _jnp.float32, 'conv_c': _jnp.float32, 'conv_c_b': _jnp.float32, 'lru_wa': _jnp.float32, 'lru_ba': _jnp.float32, 'lru_wx': _jnp.float32, 'lru_bx': _jnp.float32, 'lru_lambda': _jnp.float32, 's5_lam_re': _jnp.float32, 's5_lam_im': _jnp.float32, 's5_log_dt': _jnp.float32, 's5_b_re': _jnp.float32, 's5_b_im': _jnp.float32, 's5_c_re': _jnp.float32, 's5_c_im': _jnp.float32, 's5_d': _jnp.float32, 's5_w_glu': _jnp.float32, 's5_b_glu': _jnp.float32, 'w_out': _jnp.float32, 'ln_g': _jnp.float32, 'ln_b': _jnp.float32}
MOMENT_SCALE = {'rel_bias': 8.693832e-03, 'w_ada': 3.808782e-02, 'b_ada': 6.736346e-02, 'w_in': 1.836760e-02, 'conv_a': 2.698432e-02, 'conv_c': 2.574919e-02, 'conv_c_b': 2.603645e-01, 'lru_wa': 1.065101e-02, 'lru_ba': 6.817195e-03, 'lru_wx': 1.941800e-02, 'lru_bx': 9.008004e-03, 'lru_lambda': 1.283553e-02, 's5_lam_re': 4.890123e-04, 's5_lam_im': 5.565730e-04, 's5_log_dt': 3.535235e-01, 's5_b_re': 3.421474e-04, 's5_b_im': 3.426845e-04, 's5_c_re': 6.910302e-04, 's5_c_im': 6.796573e-04, 's5_d': 1.006699e-02, 's5_w_glu': 2.840522e-03, 's5_b_glu': 4.210621e-03, 'w_out': 3.738554e-02, 'ln_g': 2.260534e+01, 'ln_b': 5.124808e-01}


def _to_microbatches(a, axis):
    t = _jnp.moveaxis(a, axis, 0)
    t = t.reshape((N_MICROBATCH, t.shape[0] // N_MICROBATCH) + t.shape[1:])
    return _jnp.moveaxis(t, 1, axis + 1)


def setup_inputs(seed: int = 0) -> dict:
    inp = _fwd_setup_inputs(seed)
    key = _jax.random.fold_in(_jax.random.key(seed), 7919)
    shape, _ = _output_shape()
    out = dict(inp)
    out["loss_target"] = _jax.random.normal(_jax.random.fold_in(key, 0), shape, _jnp.float32)
    for i, name in enumerate(TWIN_WEIGHTS):
        w = inp[name].astype(_jnp.float32)
        if MOMENT_SCALE is None:
            s = _jnp.sqrt(_jnp.mean(_jnp.square(w)) + 1e-30)
        else:
            s = MOMENT_SCALE[name]
        km, kv = _jax.random.split(_jax.random.fold_in(key, i + 1))
        out[name] = w
        out["m_" + name] = s * _jax.random.normal(km, w.shape, _jnp.float32)
        out["v_" + name] = (s * s) * _jax.random.uniform(kv, w.shape, _jnp.float32, 0.5, 1.5)
    if N_MICROBATCH > 1:
        for name, axis in PER_EXAMPLE_BATCH_AXIS.items():
            out[name] = _to_microbatches(out[name], axis)
    return {'x': out['x'], 'c': out['c'], 'rel_bias': out['rel_bias'], 'w_ada': out['w_ada'], 'b_ada': out['b_ada'], 'w_in': out['w_in'], 'conv_a': out['conv_a'], 'conv_c': out['conv_c'], 'conv_c_b': out['conv_c_b'], 'lru_wa': out['lru_wa'], 'lru_ba': out['lru_ba'], 'lru_wx': out['lru_wx'], 'lru_bx': out['lru_bx'], 'lru_lambda': out['lru_lambda'], 's5_lam_re': out['s5_lam_re'], 's5_lam_im': out['s5_lam_im'], 's5_log_dt': out['s5_log_dt'], 's5_b_re': out['s5_b_re'], 's5_b_im': out['s5_b_im'], 's5_c_re': out['s5_c_re'], 's5_c_im': out['s5_c_im'], 's5_d': out['s5_d'], 's5_w_glu': out['s5_w_glu'], 's5_b_glu': out['s5_b_glu'], 'w_out': out['w_out'], 'ln_g': out['ln_g'], 'ln_b': out['ln_b'], 'loss_target': out['loss_target'], 'm_rel_bias': out['m_rel_bias'], 'm_w_ada': out['m_w_ada'], 'm_b_ada': out['m_b_ada'], 'm_w_in': out['m_w_in'], 'm_conv_a': out['m_conv_a'], 'm_conv_c': out['m_conv_c'], 'm_conv_c_b': out['m_conv_c_b'], 'm_lru_wa': out['m_lru_wa'], 'm_lru_ba': out['m_lru_ba'], 'm_lru_wx': out['m_lru_wx'], 'm_lru_bx': out['m_lru_bx'], 'm_lru_lambda': out['m_lru_lambda'], 'm_s5_lam_re': out['m_s5_lam_re'], 'm_s5_lam_im': out['m_s5_lam_im'], 'm_s5_log_dt': out['m_s5_log_dt'], 'm_s5_b_re': out['m_s5_b_re'], 'm_s5_b_im': out['m_s5_b_im'], 'm_s5_c_re': out['m_s5_c_re'], 'm_s5_c_im': out['m_s5_c_im'], 'm_s5_d': out['m_s5_d'], 'm_s5_w_glu': out['m_s5_w_glu'], 'm_s5_b_glu': out['m_s5_b_glu'], 'm_w_out': out['m_w_out'], 'm_ln_g': out['m_ln_g'], 'm_ln_b': out['m_ln_b'], 'v_rel_bias': out['v_rel_bias'], 'v_w_ada': out['v_w_ada'], 'v_b_ada': out['v_b_ada'], 'v_w_in': out['v_w_in'], 'v_conv_a': out['v_conv_a'], 'v_conv_c': out['v_conv_c'], 'v_conv_c_b': out['v_conv_c_b'], 'v_lru_wa': out['v_lru_wa'], 'v_lru_ba': out['v_lru_ba'], 'v_lru_wx': out['v_lru_wx'], 'v_lru_bx': out['v_lru_bx'], 'v_lru_lambda': out['v_lru_lambda'], 'v_s5_lam_re': out['v_s5_lam_re'], 'v_s5_lam_im': out['v_s5_lam_im'], 'v_s5_log_dt': out['v_s5_log_dt'], 'v_s5_b_re': out['v_s5_b_re'], 'v_s5_b_im': out['v_s5_b_im'], 'v_s5_c_re': out['v_s5_c_re'], 'v_s5_c_im': out['v_s5_c_im'], 'v_s5_d': out['v_s5_d'], 'v_s5_w_glu': out['v_s5_w_glu'], 'v_s5_b_glu': out['v_s5_b_glu'], 'v_w_out': out['v_w_out'], 'v_ln_g': out['v_ln_g'], 'v_ln_b': out['v_ln_b']}


def _loss(weights, diff, rest, loss_target):
    with _jax.named_scope("forward"):
        args = {**rest, TWIN_DIFF_INPUT: diff, **{k: w.astype(_WEIGHT_DTYPES[k]) for k, w in weights.items()}}
        y = _forward(args)
    with _jax.named_scope("loss_head"):
        err = _jnp.square(y.astype(_jnp.float32) - loss_target)
        return 0.5 * _jnp.sum(_jnp.mean(err, axis=-1)) if err.ndim else 0.5 * err


def _adamw(w, g, m, v):
    m = ADAM_B1 * m + (1.0 - ADAM_B1) * g
    v = ADAM_B2 * v + (1.0 - ADAM_B2) * _jnp.square(g)
    m_hat = m / (1.0 - ADAM_B1 ** ADAM_STEP)
    v_hat = v / (1.0 - ADAM_B2 ** ADAM_STEP)
    delta = -ADAM_LR * (m_hat / (_jnp.sqrt(v_hat) + ADAM_EPS) + ADAM_WD * w)
    return delta, m, v


def reference(x, c, rel_bias, w_ada, b_ada, w_in, conv_a, conv_c, conv_c_b, lru_wa, lru_ba, lru_wx, lru_bx, lru_lambda, s5_lam_re, s5_lam_im, s5_log_dt, s5_b_re, s5_b_im, s5_c_re, s5_c_im, s5_d, s5_w_glu, s5_b_glu, w_out, ln_g, ln_b, loss_target, m_rel_bias, m_w_ada, m_b_ada, m_w_in, m_conv_a, m_conv_c, m_conv_c_b, m_lru_wa, m_lru_ba, m_lru_wx, m_lru_bx, m_lru_lambda, m_s5_lam_re, m_s5_lam_im, m_s5_log_dt, m_s5_b_re, m_s5_b_im, m_s5_c_re, m_s5_c_im, m_s5_d, m_s5_w_glu, m_s5_b_glu, m_w_out, m_ln_g, m_ln_b, v_rel_bias, v_w_ada, v_b_ada, v_w_in, v_conv_a, v_conv_c, v_conv_c_b, v_lru_wa, v_lru_ba, v_lru_wx, v_lru_bx, v_lru_lambda, v_s5_lam_re, v_s5_lam_im, v_s5_log_dt, v_s5_b_re, v_s5_b_im, v_s5_c_re, v_s5_c_im, v_s5_d, v_s5_w_glu, v_s5_b_glu, v_w_out, v_ln_g, v_ln_b):
    given = dict(x=x, c=c, rel_bias=rel_bias, w_ada=w_ada, b_ada=b_ada, w_in=w_in, conv_a=conv_a, conv_c=conv_c, conv_c_b=conv_c_b, lru_wa=lru_wa, lru_ba=lru_ba, lru_wx=lru_wx, lru_bx=lru_bx, lru_lambda=lru_lambda, s5_lam_re=s5_lam_re, s5_lam_im=s5_lam_im, s5_log_dt=s5_log_dt, s5_b_re=s5_b_re, s5_b_im=s5_b_im, s5_c_re=s5_c_re, s5_c_im=s5_c_im, s5_d=s5_d, s5_w_glu=s5_w_glu, s5_b_glu=s5_b_glu, w_out=w_out, ln_g=ln_g, ln_b=ln_b, loss_target=loss_target, m_rel_bias=m_rel_bias, m_w_ada=m_w_ada, m_b_ada=m_b_ada, m_w_in=m_w_in, m_conv_a=m_conv_a, m_conv_c=m_conv_c, m_conv_c_b=m_conv_c_b, m_lru_wa=m_lru_wa, m_lru_ba=m_lru_ba, m_lru_wx=m_lru_wx, m_lru_bx=m_lru_bx, m_lru_lambda=m_lru_lambda, m_s5_lam_re=m_s5_lam_re, m_s5_lam_im=m_s5_lam_im, m_s5_log_dt=m_s5_log_dt, m_s5_b_re=m_s5_b_re, m_s5_b_im=m_s5_b_im, m_s5_c_re=m_s5_c_re, m_s5_c_im=m_s5_c_im, m_s5_d=m_s5_d, m_s5_w_glu=m_s5_w_glu, m_s5_b_glu=m_s5_b_glu, m_w_out=m_w_out, m_ln_g=m_ln_g, m_ln_b=m_ln_b, v_rel_bias=v_rel_bias, v_w_ada=v_w_ada, v_b_ada=v_b_ada, v_w_in=v_w_in, v_conv_a=v_conv_a, v_conv_c=v_conv_c, v_conv_c_b=v_conv_c_b, v_lru_wa=v_lru_wa, v_lru_ba=v_lru_ba, v_lru_wx=v_lru_wx, v_lru_bx=v_lru_bx, v_lru_lambda=v_lru_lambda, v_s5_lam_re=v_s5_lam_re, v_s5_lam_im=v_s5_lam_im, v_s5_log_dt=v_s5_log_dt, v_s5_b_re=v_s5_b_re, v_s5_b_im=v_s5_b_im, v_s5_c_re=v_s5_c_re, v_s5_c_im=v_s5_c_im, v_s5_d=v_s5_d, v_s5_w_glu=v_s5_w_glu, v_s5_b_glu=v_s5_b_glu, v_w_out=v_w_out, v_ln_g=v_ln_g, v_ln_b=v_ln_b)
    weights = {n: given[n] for n in TWIN_WEIGHTS}
    shared = {n: given[n] for n in SHARED_INPUTS}
    per_example = {n: given[n] for n in ['x', 'c']}
    grad_fn = _jax.value_and_grad(_loss, argnums=(0, 1))

    def one_microbatch(ex, loss_target):
        ex = dict(ex)
        diff = ex.pop(TWIN_DIFF_INPUT)
        return grad_fn(weights, diff, {**shared, **ex}, loss_target)

    if N_MICROBATCH == 1:
        loss, (grad_w, grad_x) = one_microbatch(per_example, given["loss_target"])
    else:
        def body(carry, xs):
            loss_sum, grad_sum = carry
            l_k, (gw_k, gx_k) = one_microbatch(xs[0], xs[1])
            with _jax.named_scope("update"):
                return (loss_sum + l_k, _jax.tree.map(_jnp.add, grad_sum, gw_k)), gx_k

        init = (_jnp.zeros((), _jnp.float32), _jax.tree.map(_jnp.zeros_like, weights))
        (loss, grad_w), grad_x = _jax.lax.scan(body, init, (per_example, given["loss_target"]))
    with _jax.named_scope("update"):
        delta_w, new_m, new_v = {}, {}, {}
        for n in TWIN_WEIGHTS:
            delta_w[n], new_m[n], new_v[n] = _adamw(weights[n], grad_w[n], given["m_" + n], given["v_" + n])
    return (loss, grad_x, *[grad_w[n] for n in TWIN_WEIGHTS], *[delta_w[n] for n in TWIN_WEIGHTS],
            *[new_m[n] for n in TWIN_WEIGHTS], *[new_v[n] for n in TWIN_WEIGHTS])
```

```python
import functools
import math

import jax
import jax.numpy as jnp
from jax import lax
from jax.experimental import pallas as pl
from jax.experimental.pallas import tpu as pltpu

F32 = jnp.float32
BF16 = jnp.bfloat16

NDEV = 8
DEPTH = 2
BLK = 128
DILATIONS = ((128, 1), (512, 4), (2048, 16))
REL_BUCKETS = 32
REL_MAX_DIST = 2048
LRU_C = 8.0
S5_CH = 16
S5_STATE = 64
ALPHA = (2 * DEPTH) ** 0.25
LN_EPS = 1e-5
ADAM_LR, ADAM_B1, ADAM_B2, ADAM_EPS, ADAM_WD, ADAM_STEP = 0.001, 0.9, 0.999, 1e-08, 0.01, 10
NEG = -1e30
VMEM_LIMIT = 56 * 1024 * 1024


def _call(body, **kw):
    return pl.pallas_call(body, **kw)


def _params(*sem):
    return pltpu.CompilerParams(dimension_semantics=sem, vmem_limit_bytes=VMEM_LIMIT)


def _sigmoid(x):
    return 1.0 / (1.0 + jnp.exp(-x))


def _silu(x):
    return x * _sigmoid(x)


def _dsilu(x):
    s = _sigmoid(x)
    return s * (1.0 + x * (1.0 - s))


_GC = math.sqrt(2.0 / math.pi)


def _gelu(x):
    return 0.5 * x * (1.0 + jnp.tanh(_GC * (x + 0.044715 * x * x * x)))


def _dgelu(x):
    t = jnp.tanh(_GC * (x + 0.044715 * x * x * x))
    return 0.5 * (1.0 + t) + 0.5 * x * (1.0 - t * t) * _GC * (1.0 + 3.0 * 0.044715 * x * x)


def _expm1(x):
    series = x * (1.0 + x * (1.0 / 2) * (1.0 + x * (1.0 / 3) * (1.0 + x * (1.0 / 4) * (1.0 + x * (1.0 / 5) * (1.0 + x * (1.0 / 6))))))
    return jnp.where(jnp.abs(x) < 0.25, series, jnp.exp(x) - 1.0)


def _dot(a, b):
    return jnp.dot(a, b, preferred_element_type=F32)


def _dot_nt(a, b):
    return lax.dot_general(a, b, (((1,), (1,)), ((), ())), preferred_element_type=F32)


def _dot_tn(a, b):
    return lax.dot_general(a, b, (((0,), (0,)), ((), ())), preferred_element_type=F32)


def _colsum(v):
    return jnp.sum(v, axis=0, keepdims=True)


def _exchange(x, gather, name):
    blk = x.shape if gather else x.shape[1:]

    def body(x_ref, o_ref, send_sems, recv_sems, local_sem):
        ix, iy, ic = lax.axis_index("x"), lax.axis_index("y"), lax.axis_index("c")
        me = 4 * ix + 2 * iy + ic

        def src(dev):
            return x_ref if gather else x_ref.at[dev]

        local = pltpu.make_async_copy(src(me), o_ref.at[me], local_sem)
        local.start()
        copies = []
        for k in range(1, NDEV):
            px = 1 - ix if k & 4 else ix
            py = 1 - iy if k & 2 else iy
            pc = 1 - ic if k & 1 else ic
            peer = 4 * px + 2 * py + pc
            cp = pltpu.make_async_remote_copy(
                src_ref=src(peer), dst_ref=o_ref.at[me], send_sem=send_sems.at[k - 1], recv_sem=recv_sems.at[k - 1],
                device_id=(px, py, pc), device_id_type=pl.DeviceIdType.MESH)
            cp.start()
            copies.append(cp)
        for cp in copies:
            cp.wait()
        local.wait()

    return _call(
        body, name=name,
        out_shape=jax.ShapeDtypeStruct((NDEV,) + tuple(blk), x.dtype),
        in_specs=[pl.BlockSpec(memory_space=pl.ANY)],
        out_specs=pl.BlockSpec(memory_space=pl.ANY),
        scratch_shapes=[pltpu.SemaphoreType.DMA((NDEV - 1,)), pltpu.SemaphoreType.DMA((NDEV - 1,)), pltpu.SemaphoreType.DMA],
    )(x)


def _pack(arrs):
    flat = jnp.concatenate([a.reshape(-1).astype(F32) for a in arrs])
    n = flat.shape[0]
    npad = -(-n // 1024) * 1024
    return jnp.pad(flat, (0, npad - n)).reshape(npad // 128, 128)


def _unpack(packed, shapes):
    flat = packed.reshape(-1)
    out, off = [], 0
    for s in shapes:
        n = math.prod(s)
        out.append(flat[off:off + n].reshape(s))
        off += n
    return out


def _sum8(parts, name):
    _, R, C = parts.shape
    TR = R
    for cand in (512, 256, 128, 64, 32, 16, 8):
        if R % cand == 0:
            TR = cand
            break

    def body(p_ref, o_ref):
        acc = p_ref[0]
        for d in range(1, NDEV):
            acc = acc + p_ref[d]
        o_ref[...] = acc

    return _call(
        body, name=name, grid=(R // TR,),
        in_specs=[pl.BlockSpec((NDEV, TR, C), lambda i: (0, i, 0))],
        out_specs=pl.BlockSpec((TR, C), lambda i: (i, 0)),
        out_shape=jax.ShapeDtypeStruct((R, C), F32),
        compiler_params=_params("arbitrary"),
    )(parts)


def _adamw(g_parts, w, m, v, name):
    R, C = w.shape
    nparts = g_parts.shape[0] if g_parts.ndim == 3 else 0
    TR = R
    for cand in (256, 128, 64, 32, 16, 8):
        if R % cand == 0:
            TR = cand
            break

    def body(g_ref, w_ref, m_ref, v_ref, go_ref, d_ref, mo_ref, vo_ref):
        if nparts:
            g = g_ref[0]
            for d in range(1, nparts):
                g = g + g_ref[d]
        else:
            g = g_ref[...]
        m2 = ADAM_B1 * m_ref[...] + (1.0 - ADAM_B1) * g
        v2 = ADAM_B2 * v_ref[...] + (1.0 - ADAM_B2) * (g * g)
        m_hat = m2 / (1.0 - ADAM_B1 ** ADAM_STEP)
        v_hat = v2 / (1.0 - ADAM_B2 ** ADAM_STEP)
        go_ref[...] = g
        d_ref[...] = -ADAM_LR * (m_hat / (jnp.sqrt(v_hat) + ADAM_EPS) + ADAM_WD * w_ref[...])
        mo_ref[...] = m2
        vo_ref[...] = v2

    spec = pl.BlockSpec((TR, C), lambda i: (i, 0))
    gspec = pl.BlockSpec((nparts, TR, C), lambda i: (0, i, 0)) if nparts else spec
    return _call(
        body, name=name, grid=(R // TR,),
        in_specs=[gspec, spec, spec, spec], out_specs=[spec] * 4,
        out_shape=[jax.ShapeDtypeStruct((R, C), F32)] * 4,
        compiler_params=_params("arbitrary"),
    )(g_parts, w, m, v)


def _ada_cols(c_all, w_ada, b_cols):
    L, D, NS = w_ada.shape

    def body(c_ref, w_ref, b_ref, o_ref):
        cond = _silu(c_ref[...]).astype(BF16)
        o_ref[0] = _dot(cond, w_ref[0].astype(BF16)) + b_ref[0]

    return _call(
        body, name="ada_cols", grid=(L,),
        in_specs=[pl.BlockSpec((NDEV, D), lambda l: (0, 0)), pl.BlockSpec((1, D, NS), lambda l: (l, 0, 0)),
                  pl.BlockSpec((1, 1, NS), lambda l: (l, 0, 0))],
        out_specs=pl.BlockSpec((1, NDEV, NS), lambda l: (l, 0, 0)),
        out_shape=jax.ShapeDtypeStruct((L, NDEV, NS), F32),
        compiler_params=_params("arbitrary"),
    )(c_all, w_ada, b_cols)


def _ada_grads(c_all, dada_all, dada_cols):
    _, D = c_all.shape
    L, _, NS = dada_cols.shape
    D3 = dada_all.shape[2]

    def body(c_ref, da_ref, dc_ref, gw_ref, gb_ref):
        cond = _silu(c_ref[...]).astype(BF16)
        gw_ref[0] = _dot_tn(cond, dc_ref[0].astype(BF16))
        acc = da_ref[0]
        for d in range(1, NDEV):
            acc = acc + da_ref[d]
        gb_ref[...] = acc

    return _call(
        body, name="ada_grads", grid=(L,),
        in_specs=[pl.BlockSpec((NDEV, D), lambda l: (0, 0)), pl.BlockSpec((NDEV, L, D3), lambda l: (0, 0, 0)),
                  pl.BlockSpec((1, NDEV, NS), lambda l: (l, 0, 0))],
        out_specs=[pl.BlockSpec((1, D, NS), lambda l: (l, 0, 0)), pl.BlockSpec((L, D3), lambda l: (0, 0))],
        out_shape=[jax.ShapeDtypeStruct((L, D, NS), F32), jax.ShapeDtypeStruct((L, D3), F32)],
        compiler_params=_params("arbitrary"),
    )(c_all, dada_all, dada_cols)


def _tile(S, want):
    return want if S % want == 0 else S


def _proj(x, mod, wg):
    S, D = x.shape
    NS = wg.shape[2]
    TS = _tile(S, 512)

    def body(x_ref, mod_ref, w_ref, o_ref, h_ref):
        @pl.when(pl.program_id(1) == 0)
        def _():
            h_ref[...] = (x_ref[...] * mod_ref[0:1, :] + mod_ref[1:2, :]).astype(BF16)
        o_ref[...] = _dot(h_ref[...], w_ref[0])

    return _call(
        body, name="proj", grid=(S // TS, NDEV),
        in_specs=[pl.BlockSpec((TS, D), lambda i, j: (i, 0)), pl.BlockSpec((8, D), lambda i, j: (0, 0)),
                  pl.BlockSpec((1, D, NS), lambda i, j: (j, 0, 0))],
        out_specs=[pl.BlockSpec((TS, NS), lambda i, j: (i, j)), pl.BlockSpec((TS, D), lambda i, j: (i, 0))],
        out_shape=[jax.ShapeDtypeStruct((S, NDEV * NS), F32), jax.ShapeDtypeStruct((S, D), BF16)],
        compiler_params=_params("arbitrary", "arbitrary"),
    )(x, mod, wg)


def _chunk(TS, BR, k):
    return pl.BlockSpec((TS, BR), lambda i: (i, k))


def _halo_prev(TS, BR, k):
    return pl.BlockSpec((8, BR), lambda i: (jnp.maximum(i * (TS // 8) - 1, 0), k))


def _conv_a_fwd(proj, w8, BR):
    S = proj.shape[0]
    TS = _tile(S, 512)

    def body(ab_ref, ac_ref, ax_ref, ag_ref, hc_ref, hx_ref, w_ref, o_ref, ext):
        i = pl.program_id(0)
        u = ac_ref[...] * ax_ref[...]
        ext[0:8, :] = jnp.where(i > 0, hc_ref[...] * hx_ref[...], 0.0)
        ext[8:8 + TS, :] = u
        conv = ext[pl.ds(6, TS), :] * w_ref[0:1, :] + ext[pl.ds(7, TS), :] * w_ref[1:2, :] + u * w_ref[2:3, :]
        o_ref[...] = ab_ref[...] * conv * _silu(ag_ref[...])

    return _call(
        body, name="conv_a_fwd", grid=(S // TS,),
        in_specs=[_chunk(TS, BR, 0), _chunk(TS, BR, 1), _chunk(TS, BR, 2), _chunk(TS, BR, 3),
                  _halo_prev(TS, BR, 1), _halo_prev(TS, BR, 2), pl.BlockSpec((8, BR), lambda i: (0, 0))],
        out_specs=pl.BlockSpec((TS, BR), lambda i: (i, 0)),
        out_shape=jax.ShapeDtypeStruct((S, BR), F32),
        scratch_shapes=[pltpu.VMEM((TS + 8, BR), F32)],
        compiler_params=_params("arbitrary"),
    )(proj, proj, proj, proj, proj, proj, w8)


def _attn_masks():
    i = lax.broadcasted_iota(jnp.int32, (BLK, BLK), 0)
    j = lax.broadcasted_iota(jnp.int32, (BLK, BLK), 1)
    return j >= i, j <= i


def _attn_fwd(proj, bias, dil, BR):
    S = proj.shape[0]
    HD = BR // 8
    nb = S // (BLK * dil)
    p3 = proj.reshape(nb, BLK, dil * 12 * BR)
    scale = HD ** -0.5

    def body(q_ref, kp_ref, kc_ref, vp_ref, vc_ref, b_ref, o_ref, l_ref):
        n = pl.program_id(1)
        mask_p, mask_c = _attn_masks()
        mask_p = jnp.logical_and(mask_p, n > 0)
        q = q_ref[0] * scale
        kp, kc, vp, vc = kp_ref[0], kc_ref[0], vp_ref[0], vc_ref[0]
        for h in range(8):
            sl = slice(h * HD, (h + 1) * HD)
            qh = q[:, sl].astype(BF16)
            sp = jnp.where(mask_p, _dot_nt(qh, kp[:, sl].astype(BF16)) + b_ref[h, :, 0:BLK], NEG)
            sc = jnp.where(mask_c, _dot_nt(qh, kc[:, sl].astype(BF16)) + b_ref[h, :, BLK:2 * BLK], NEG)
            m = jnp.maximum(jnp.max(sp, axis=-1, keepdims=True), jnp.max(sc, axis=-1, keepdims=True))
            pp = jnp.exp(sp - m)
            pc = jnp.exp(sc - m)
            l = jnp.sum(pp, axis=-1, keepdims=True) + jnp.sum(pc, axis=-1, keepdims=True)
            inv = 1.0 / l
            oh = _dot((pp * inv).astype(BF16), vp[:, sl].astype(BF16)) + _dot((pc * inv).astype(BF16), vc[:, sl].astype(BF16))
            o_ref[0, :, sl] = oh
            l_ref[0, :, sl] = jnp.broadcast_to(m + jnp.log(l), (BLK, HD))

    def cur(c):
        return pl.BlockSpec((1, BLK, BR), lambda r, n: (n, 0, r * 12 + c))

    def prev(c):
        return pl.BlockSpec((1, BLK, BR), lambda r, n: (jnp.maximum(n - 1, 0), 0, r * 12 + c))

    ospec = pl.BlockSpec((1, BLK, BR), lambda r, n: (n, 0, r))
    o, lse = _call(
        body, name="attn_fwd_d%d" % dil, grid=(dil, nb),
        in_specs=[cur(4), prev(5), cur(5), prev(6), cur(6), pl.BlockSpec((8, BLK, 2 * BLK), lambda r, n: (0, 0, 0))],
        out_specs=[ospec, ospec],
        out_shape=[jax.ShapeDtypeStruct((nb, BLK, dil * BR), F32)] * 2,
        compiler_params=_params("arbitrary", "arbitrary"),
    )(p3, p3, p3, p3, p3, bias)
    return o.reshape(S, BR), lse.reshape(S, BR)


def _mix_weights(l1, l2, l3):
    m = jnp.maximum(jnp.maximum(l1, l2), l3)
    e1, e2, e3 = jnp.exp(l1 - m), jnp.exp(l2 - m), jnp.exp(l3 - m)
    inv = 1.0 / (e1 + e2 + e3)
    return e1 * inv, e2 * inv, e3 * inv


def _attn_mix_fwd(proj, os_, ls_, BR):
    S = proj.shape[0]
    TS = _tile(S, 512)

    def body(o1, o2, o3, l1, l2, l3, g_ref, y_ref):
        w1, w2, w3 = _mix_weights(l1[...], l2[...], l3[...])
        y_ref[...] = (w1 * o1[...] + w2 * o2[...] + w3 * o3[...]) * _silu(g_ref[...])

    spec = pl.BlockSpec((TS, BR), lambda i: (i, 0))
    return _call(
        body, name="attn_mix_fwd", grid=(S // TS,),
        in_specs=[spec] * 6 + [_chunk(TS, BR, 7)], out_specs=spec,
        out_shape=jax.ShapeDtypeStruct((S, BR), F32),
        compiler_params=_params("arbitrary"),
    )(*os_, *ls_, proj)


def _lru_gates(xc, wa, wx, ba, bx, sp):
    xb = xc.astype(BF16)
    r = _sigmoid(_dot(xb, wa) + ba)
    ig = _sigmoid(_dot(xb, wx) + bx)
    la = -LRU_C * r * sp
    a = jnp.exp(la)
    mult = jnp.sqrt(-_expm1(2.0 * la))
    return r, ig, a, mult


def _lru_fwd(proj, rows, wa, wx, BR):
    S = proj.shape[0]
    TS = _tile(S, 256)
    PAD = TS // 2

    def body(cx_ref, hx_ref, cg_ref, r_ref, wa_ref, wx_ref, y_ref, h_ref, ext, sa, sb, carry):
        i = pl.program_id(0)

        @pl.when(i == 0)
        def _():
            sa[0:PAD, :] = jnp.ones((PAD, BR), F32)
            sb[0:PAD, :] = jnp.zeros((PAD, BR), F32)
            carry[...] = jnp.zeros_like(carry)

        cx = cx_ref[...]
        ext[0:8, :] = jnp.where(i > 0, hx_ref[...], 0.0)
        ext[8:8 + TS, :] = cx
        xc = (ext[pl.ds(5, TS), :] * r_ref[0:1, :] + ext[pl.ds(6, TS), :] * r_ref[1:2, :]
              + ext[pl.ds(7, TS), :] * r_ref[2:3, :] + cx * r_ref[3:4, :] + r_ref[4:5, :])
        _, ig, a, mult = _lru_gates(xc, wa_ref[...], wx_ref[...], r_ref[5:6, :], r_ref[6:7, :], r_ref[7:8, :])
        sa[PAD:PAD + TS, :] = a
        sb[PAD:PAD + TS, :] = mult * ig * xc
        d = 1
        while d < TS:
            A, B = sa[PAD:PAD + TS, :], sb[PAD:PAD + TS, :]
            As, Bs = sa[pl.ds(PAD - d, TS), :], sb[pl.ds(PAD - d, TS), :]
            sb[PAD:PAD + TS, :] = A * Bs + B
            sa[PAD:PAD + TS, :] = A * As
            d *= 2
        h = sb[PAD:PAD + TS, :] + sa[PAD:PAD + TS, :] * carry[0:1, :]
        carry[0:1, :] = h[TS - 1:TS, :]
        h_ref[...] = h
        y_ref[...] = h * _silu(cg_ref[...])

    full = pl.BlockSpec((BR, BR), lambda i: (0, 0))
    spec = pl.BlockSpec((TS, BR), lambda i: (i, 0))
    return _call(
        body, name="lru_fwd", grid=(S // TS,),
        in_specs=[_chunk(TS, BR, 8), _halo_prev(TS, BR, 8), _chunk(TS, BR, 9), pl.BlockSpec((8, BR), lambda i: (0, 0)), full, full],
        out_specs=[spec, spec],
        out_shape=[jax.ShapeDtypeStruct((S, BR), F32)] * 2,
        scratch_shapes=[pltpu.VMEM((TS + 8, BR), F32), pltpu.VMEM((PAD + TS, BR), F32), pltpu.VMEM((PAD + TS, BR), F32),
                        pltpu.VMEM((8, BR), F32)],
        compiler_params=_params("arbitrary"),
    )(proj, proj, proj, rows, wa, wx)


def _s5_fwd(proj, pw_re, pw_im, bbre, bbim, cre, cim, rows, wg, BR):
    S = proj.shape[0]
    TS, W = pw_re.shape
    PAD = TS // 2
    CB = 8 * S5_CH
    SB = 8 * S5_STATE
    nblk = BR // CB

    def body(u_ref, dg_ref, pr_ref, pi_ref, bbre_ref, bbim_ref, cre_ref, cim_ref, r_ref, wg_ref,
             y_ref, xre_ref, xim_ref, sre, sim, ypre_s, carry):
        i = pl.program_id(0)

        @pl.when(i == 0)
        def _():
            sre[0:PAD, :] = jnp.zeros((PAD, SB), F32)
            sim[0:PAD, :] = jnp.zeros((PAD, SB), F32)
            carry[...] = jnp.zeros_like(carry)

        for m in range(nblk):
            cs, ws = slice(m * CB, (m + 1) * CB), slice(m * SB, (m + 1) * SB)
            ub = u_ref[:, cs].astype(BF16)
            sre[PAD:PAD + TS, :] = _dot(ub, bbre_ref[cs, :])
            sim[PAD:PAD + TS, :] = _dot(ub, bbim_ref[cs, :])
            d = 1
            while d < TS:
                ar, ai = pr_ref[d - 1:d, ws], pi_ref[d - 1:d, ws]
                xr, xi = sre[PAD:PAD + TS, :], sim[PAD:PAD + TS, :]
                sr, si = sre[pl.ds(PAD - d, TS), :], sim[pl.ds(PAD - d, TS), :]
                sre[PAD:PAD + TS, :] = xr + ar * sr - ai * si
                sim[PAD:PAD + TS, :] = xi + ar * si + ai * sr
                d *= 2
            cr, ci = carry[0:1, ws], carry[1:2, ws]
            pr, pi = pr_ref[:, ws], pi_ref[:, ws]
            xre = sre[PAD:PAD + TS, :] + pr * cr - pi * ci
            xim = sim[PAD:PAD + TS, :] + pr * ci + pi * cr
            carry[0:1, ws] = xre[TS - 1:TS, :]
            carry[1:2, ws] = xim[TS - 1:TS, :]
            xre_ref[:, ws] = xre
            xim_ref[:, ws] = xim
            ypre_s[:, cs] = _dot(xre.astype(BF16), cre_ref[ws, :]) - _dot(xim.astype(BF16), cim_ref[ws, :])
        dg = dg_ref[...]
        yg = _gelu(ypre_s[...] + r_ref[0:1, :] * u_ref[...])
        s = _sigmoid(_dot(yg.astype(BF16), wg_ref[...]) + r_ref[1:2, :])
        y_ref[...] = yg * s * _silu(dg)

    def const(shape):
        return pl.BlockSpec(shape, lambda i: (0, 0))

    return _call(
        body, name="s5_fwd", grid=(S // TS,),
        in_specs=[_chunk(TS, BR, 10), _chunk(TS, BR, 11), const((TS, W)), const((TS, W)), const((BR, SB)), const((BR, SB)),
                  const((W, CB)), const((W, CB)), const((8, BR)), const((BR, BR))],
        out_specs=[pl.BlockSpec((TS, BR), lambda i: (i, 0)), pl.BlockSpec((TS, W), lambda i: (i, 0)), pl.BlockSpec((TS, W), lambda i: (i, 0))],
        out_shape=[jax.ShapeDtypeStruct((S, BR), F32), jax.ShapeDtypeStruct((S, W), F32), jax.ShapeDtypeStruct((S, W), F32)],
        scratch_shapes=[pltpu.VMEM((PAD + TS, SB), F32), pltpu.VMEM((PAD + TS, SB), F32), pltpu.VMEM((TS, BR), F32), pltpu.VMEM((8, W), F32)],
        compiler_params=_params("arbitrary"),
    )(proj, proj, pw_re, pw_im, bbre, bbim, cre, cim, rows, wg)


def _out_ln(ys, w_out, x, rows):
    S, D = x.shape
    BR = D // 4
    TS = _tile(S, 256)

    def body(ya, yb, yc, yd, w_ref, x_ref, r_ref, o_ref, y_ref, cat_ref):
        cat = jnp.concatenate([ya[...], yb[...], yc[...], yd[...]], axis=1).astype(BF16)
        y = _dot(cat, w_ref[...])
        z = ALPHA * x_ref[...] + r_ref[0:1, :] * y
        mu = jnp.mean(z, axis=-1, keepdims=True)
        zc = z - mu
        var = jnp.mean(zc * zc, axis=-1, keepdims=True)
        o_ref[...] = zc * lax.rsqrt(var + LN_EPS) * r_ref[1:2, :] + r_ref[2:3, :]
        y_ref[...] = y
        cat_ref[...] = cat

    yspec = pl.BlockSpec((TS, BR), lambda i: (i, 0))
    spec = pl.BlockSpec((TS, D), lambda i: (i, 0))
    return _call(
        body, name="out_ln", grid=(S // TS,),
        in_specs=[yspec] * 4 + [pl.BlockSpec((D, D), lambda i: (0, 0)), spec, pl.BlockSpec((8, D), lambda i: (0, 0))],
        out_specs=[spec, spec, spec],
        out_shape=[jax.ShapeDtypeStruct((S, D), F32), jax.ShapeDtypeStruct((S, D), F32), jax.ShapeDtypeStruct((S, D), BF16)],
        compiler_params=_params("arbitrary"),
    )(*ys, w_out, x, rows)


def _loss_grad(x, target):
    S, D = x.shape
    TS = _tile(S, 512)

    def body(x_ref, t_ref, g_ref, l_ref):
        @pl.when(pl.program_id(0) == 0)
        def _():
            l_ref[...] = jnp.zeros_like(l_ref)
        diff = x_ref[...] - t_ref[...]
        g_ref[...] = diff * (1.0 / D)
        l_ref[...] += jnp.sum(_colsum(diff * diff), axis=1, keepdims=True) * (0.5 / D)

    spec = pl.BlockSpec((TS, D), lambda i: (i, 0))
    g, l = _call(
        body, name="loss_grad", grid=(S // TS,),
        in_specs=[spec, spec], out_specs=[spec, pl.BlockSpec((1, 1), lambda i: (0, 0))],
        out_shape=[jax.ShapeDtypeStruct((S, D), F32), jax.ShapeDtypeStruct((1, 1), F32)],
        compiler_params=_params("arbitrary"),
    )(x, target)
    return g, l[0, 0]


def _ln_bwd(dout, x, y, rows, w_out):
    S, D = x.shape
    TS = _tile(S, 256)

    def body(do_ref, x_ref, y_ref, r_ref, w_ref, dxr_ref, dyb_ref, dcat_ref, acc_ref):
        @pl.when(pl.program_id(0) == 0)
        def _():
            acc_ref[...] = jnp.zeros_like(acc_ref)
        g1, lg = r_ref[0:1, :], r_ref[1:2, :]
        yv = y_ref[...]
        z = ALPHA * x_ref[...] + g1 * yv
        mu = jnp.mean(z, axis=-1, keepdims=True)
        zc = z - mu
        var = jnp.mean(zc * zc, axis=-1, keepdims=True)
        rstd = lax.rsqrt(var + LN_EPS)
        xhat = zc * rstd
        do = do_ref[...]
        dxh = do * lg
        dz = rstd * (dxh - jnp.mean(dxh, axis=-1, keepdims=True) - xhat * jnp.mean(dxh * xhat, axis=-1, keepdims=True))
        dxr_ref[...] = ALPHA * dz
        dyb = (g1 * dz).astype(BF16)
        dyb_ref[...] = dyb
        dcat_ref[...] = _dot_nt(dyb, w_ref[...])
        acc_ref[0:1, :] += _colsum(do * xhat)
        acc_ref[1:2, :] += _colsum(do)
        acc_ref[2:3, :] += _colsum(dz * yv)

    spec = pl.BlockSpec((TS, D), lambda i: (i, 0))
    rspec = pl.BlockSpec((8, D), lambda i: (0, 0))
    return _call(
        body, name="ln_bwd", grid=(S // TS,),
        in_specs=[spec, spec, spec, rspec, pl.BlockSpec((D, D), lambda i: (0, 0))],
        out_specs=[spec, spec, spec, rspec],
        out_shape=[jax.ShapeDtypeStruct((S, D), F32), jax.ShapeDtypeStruct((S, D), BF16), jax.ShapeDtypeStruct((S, D), F32),
                   jax.ShapeDtypeStruct((8, D), F32)],
        compiler_params=_params("arbitrary"),
    )(dout, x, y, rows, w_out)


def _matmul_tn_rows(a, b, TM, name):
    S, M = a.shape
    N = b.shape[1]
    TK = _tile(S, 512)

    def body(a_ref, b_ref, o_ref):
        @pl.when(pl.program_id(1) == 0)
        def _():
            o_ref[...] = jnp.zeros_like(o_ref)
        o_ref[...] += _dot_tn(a_ref[...], b_ref[...])

    return _call(
        body, name=name, grid=(M // TM, S // TK),
        in_specs=[pl.BlockSpec((TK, TM), lambda i, k: (k, i)), pl.BlockSpec((TK, N), lambda i, k: (k, 0))],
        out_specs=pl.BlockSpec((TM, N), lambda i, k: (i, 0)),
        out_shape=jax.ShapeDtypeStruct((M, N), F32),
        compiler_params=_params("arbitrary", "arbitrary"),
    )(a, b)


def _matmul_tn_cols(a, b, TN, name):
    S, M = a.shape
    N = b.shape[1]
    TK = _tile(S, 512)

    def body(a_ref, b_ref, o_ref):
        @pl.when(pl.program_id(1) == 0)
        def _():
            o_ref[...] = jnp.zeros_like(o_ref)
        o_ref[0] += _dot_tn(a_ref[...], b_ref[...])

    return _call(
        body, name=name, grid=(N // TN, S // TK),
        in_specs=[pl.BlockSpec((TK, M), lambda j, k: (k, 0)), pl.BlockSpec((TK, TN), lambda j, k: (k, j))],
        out_specs=pl.BlockSpec((1, M, TN), lambda j, k: (j, 0, 0)),
        out_shape=jax.ShapeDtypeStruct((N // TN, M, TN), F32),
        compiler_params=_params("arbitrary", "arbitrary"),
    )(a, b)


def _dh(dproj, wg, dxres, x, mod):
    S, D = x.shape
    NS = wg.shape[2]
    TS = _tile(S, 512)

    def body(dp_ref, w_ref, dxr_ref, x_ref, mod_ref, dx_ref, sum_ref, acc):
        i, j = pl.program_id(0), pl.program_id(1)

        @pl.when(jnp.logical_and(i == 0, j == 0))
        def _():
            sum_ref[...] = jnp.zeros_like(sum_ref)

        @pl.when(j == 0)
        def _():
            acc[...] = jnp.zeros_like(acc)

        acc[...] += _dot_nt(dp_ref[...], w_ref[0])

        @pl.when(j == NDEV - 1)
        def _():
            dh = acc[...]
            dx_ref[...] = dxr_ref[...] + dh * mod_ref[0:1, :]
            sum_ref[0:1, :] += _colsum(dh * x_ref[...])
            sum_ref[1:2, :] += _colsum(dh)

    spec = pl.BlockSpec((TS, D), lambda i, j: (i, 0))
    rspec = pl.BlockSpec((8, D), lambda i, j: (0, 0))
    return _call(
        body, name="dh", grid=(S // TS, NDEV),
        in_specs=[pl.BlockSpec((TS, NS), lambda i, j: (i, j)), pl.BlockSpec((1, D, NS), lambda i, j: (j, 0, 0)), spec, spec, rspec],
        out_specs=[spec, rspec],
        out_shape=[jax.ShapeDtypeStruct((S, D), F32), jax.ShapeDtypeStruct((8, D), F32)],
        scratch_shapes=[pltpu.VMEM((TS, D), F32)],
        compiler_params=_params("arbitrary", "arbitrary"),
    )(dproj, wg, dxres, x, mod)


def _halo_next(S, TS, BR, k):
    last = S // 8 - 1
    return pl.BlockSpec((8, BR), lambda i: (jnp.minimum((i + 1) * (TS // 8), last), k))


def _conv_a_bwd(proj, dcat, w8, BR):
    S = proj.shape[0]
    TS = _tile(S, 512)
    nt = S // TS

    def body(ab_ref, ac_ref, ax_ref, ag_ref, dy_ref, hc_ref, hx_ref, nab_ref, nag_ref, ndy_ref, w_ref, dp_ref, dw_ref, ext, dext):
        i = pl.program_id(0)

        @pl.when(i == 0)
        def _():
            dw_ref[...] = jnp.zeros_like(dw_ref)

        w0, w1, w2 = w_ref[0:1, :], w_ref[1:2, :], w_ref[2:3, :]
        ab, ac, ax, ag, dy = ab_ref[...], ac_ref[...], ax_ref[...], ag_ref[...], dy_ref[...]
        u = ac * ax
        ext[0:8, :] = jnp.where(i > 0, hc_ref[...] * hx_ref[...], 0.0)
        ext[8:8 + TS, :] = u
        u1, u2 = ext[pl.ds(7, TS), :], ext[pl.ds(6, TS), :]
        conv = u2 * w0 + u1 * w1 + u * w2
        sg = _silu(ag)
        dconv = dy * ab * sg
        dext[0:TS, :] = dconv
        dext[TS:TS + 8, :] = jnp.where(i < nt - 1, ndy_ref[...] * nab_ref[...] * _silu(nag_ref[...]), 0.0)
        du = w2 * dconv + w1 * dext[pl.ds(1, TS), :] + w0 * dext[pl.ds(2, TS), :]
        dp_ref[:, 0:BR] = (dy * conv * sg).astype(BF16)
        dp_ref[:, BR:2 * BR] = (du * ax).astype(BF16)
        dp_ref[:, 2 * BR:3 * BR] = (du * ac).astype(BF16)
        dp_ref[:, 3 * BR:4 * BR] = (dy * ab * conv * _dsilu(ag)).astype(BF16)
        dw_ref[0:1, :] += _colsum(dconv * u2)
        dw_ref[1:2, :] += _colsum(dconv * u1)
        dw_ref[2:3, :] += _colsum(dconv * u)

    rspec = pl.BlockSpec((8, BR), lambda i: (0, 0))
    return _call(
        body, name="conv_a_bwd", grid=(nt,),
        in_specs=[_chunk(TS, BR, 0), _chunk(TS, BR, 1), _chunk(TS, BR, 2), _chunk(TS, BR, 3), _chunk(TS, BR, 0),
                  _halo_prev(TS, BR, 1), _halo_prev(TS, BR, 2), _halo_next(S, TS, BR, 0), _halo_next(S, TS, BR, 3),
                  _halo_next(S, TS, BR, 0), rspec],
        out_specs=[pl.BlockSpec((TS, 4 * BR), lambda i: (i, 0)), rspec],
        out_shape=[jax.ShapeDtypeStruct((S, 4 * BR), BF16), jax.ShapeDtypeStruct((8, BR), F32)],
        scratch_shapes=[pltpu.VMEM((TS + 8, BR), F32), pltpu.VMEM((TS + 8, BR), F32)],
        compiler_params=_params("arbitrary"),
    )(proj, proj, proj, proj, dcat, proj, proj, proj, proj, dcat, w8)


def _attn_mix_bwd(proj, dcat, os_, ls_, head_ones, BR):
    S = proj.shape[0]
    TS = _tile(S, 512)

    def body(dy_ref, g_ref, o1, o2, o3, l1, l2, l3, ones_ref, d1, d2, d3, e1, e2, e3, dg_ref):
        w1, w2, w3 = _mix_weights(l1[...], l2[...], l3[...])
        mix = w1 * o1[...] + w2 * o2[...] + w3 * o3[...]
        g = g_ref[...]
        dy = dy_ref[...]
        dmix = dy * _silu(g)
        dg_ref[...] = dy * mix * _dsilu(g)
        t = jnp.dot(dmix * mix, ones_ref[...], preferred_element_type=F32, precision=lax.Precision.HIGHEST)
        d1[...] = w1 * dmix
        d2[...] = w2 * dmix
        d3[...] = w3 * dmix
        e1[...] = -w1 * t
        e2[...] = -w2 * t
        e3[...] = -w3 * t

    spec = pl.BlockSpec((TS, BR), lambda i: (i, 0))
    outs = _call(
        body, name="attn_mix_bwd", grid=(S // TS,),
        in_specs=[_chunk(TS, BR, 1), _chunk(TS, BR, 7)] + [spec] * 6 + [pl.BlockSpec((BR, BR), lambda i: (0, 0))],
        out_specs=[spec] * 7,
        out_shape=[jax.ShapeDtypeStruct((S, BR), F32)] * 7,
        compiler_params=_params("arbitrary"),
    )(dcat, proj, *os_, *ls_, head_ones)
    return outs[0:3], outs[3:6], outs[6]


def _attn_bwd(proj, do, e, lse, bias, dil, BR):
    S = proj.shape[0]
    HD = BR // 8
    nb = S // (BLK * dil)
    p3 = proj.reshape(nb, BLK, dil * 12 * BR)
    do3, e3, l3 = (t.reshape(nb, BLK, dil * BR) for t in (do, e, lse))
    scale = HD ** -0.5

    def body(qc_ref, qn_ref, kp_ref, kc_ref, vp_ref, vc_ref, doc_ref, don_ref, ec_ref, en_ref, lc_ref, ln_ref, b_ref,
             d_ref, db_ref):
        r, n = pl.program_id(0), pl.program_id(1)

        @pl.when(jnp.logical_and(r == 0, n == 0))
        def _():
            db_ref[...] = jnp.zeros_like(db_ref)

        mask_p, mask_c = _attn_masks()
        mask_a = jnp.logical_and(mask_p, n > 0)
        mask_n = jnp.logical_and(mask_p, n < nb - 1)
        qc, qn = qc_ref[0] * scale, qn_ref[0] * scale
        kp, kc, vp, vc = kp_ref[0], kc_ref[0], vp_ref[0], vc_ref[0]
        doc, don, ec, en, lc, ln = doc_ref[0], don_ref[0], ec_ref[0], en_ref[0], lc_ref[0], ln_ref[0]
        for h in range(8):
            sl = slice(h * HD, (h + 1) * HD)
            one = slice(h * HD, h * HD + 1)
            qh, qnh = qc[:, sl].astype(BF16), qn[:, sl].astype(BF16)
            kph, kch = kp[:, sl].astype(BF16), kc[:, sl].astype(BF16)
            vph, vch = vp[:, sl].astype(BF16), vc[:, sl].astype(BF16)
            dob, donb = doc[:, sl].astype(BF16), don[:, sl].astype(BF16)
            lse_c, lse_n, e_c, e_n = lc[:, one], ln[:, one], ec[:, one], en[:, one]
            pa = jnp.exp(jnp.where(mask_a, _dot_nt(qh, kph) + b_ref[h, :, 0:BLK], NEG) - lse_c)
            pb = jnp.exp(jnp.where(mask_c, _dot_nt(qh, kch) + b_ref[h, :, BLK:2 * BLK], NEG) - lse_c)
            pc = jnp.exp(jnp.where(mask_n, _dot_nt(qnh, kch) + b_ref[h, :, 0:BLK], NEG) - lse_n)
            dsa = pa * (_dot_nt(dob, vph) + e_c)
            dsb = pb * (_dot_nt(dob, vch) + e_c)
            dsc = pc * (_dot_nt(donb, vch) + e_n)
            dsab, dsbb, dscb = dsa.astype(BF16), dsb.astype(BF16), dsc.astype(BF16)
            d_ref[0, :, h * HD:(h + 1) * HD] = (_dot(dsab, kph) + _dot(dsbb, kch)) * scale
            d_ref[0, :, BR + h * HD:BR + (h + 1) * HD] = _dot_tn(dsbb, qh) + _dot_tn(dscb, qnh)
            d_ref[0, :, 2 * BR + h * HD:2 * BR + (h + 1) * HD] = _dot_tn(pb.astype(BF16), dob) + _dot_tn(pc.astype(BF16), donb)
            db_ref[h, :, 0:BLK] += dsa
            db_ref[h, :, BLK:2 * BLK] += dsb

    def cur(c, w):
        return pl.BlockSpec((1, BLK, BR), lambda r, n: (n, 0, r * w + c))

    def prev(c, w):
        return pl.BlockSpec((1, BLK, BR), lambda r, n: (jnp.maximum(n - 1, 0), 0, r * w + c))

    def nxt(c, w):
        return pl.BlockSpec((1, BLK, BR), lambda r, n: (jnp.minimum(n + 1, nb - 1), 0, r * w + c))

    bspec = pl.BlockSpec((8, BLK, 2 * BLK), lambda r, n: (0, 0, 0))
    d, db = _call(
        body, name="attn_bwd_d%d" % dil, grid=(dil, nb),
        in_specs=[cur(4, 12), nxt(4, 12), prev(5, 12), cur(5, 12), prev(6, 12), cur(6, 12),
                  cur(0, 1), nxt(0, 1), cur(0, 1), nxt(0, 1), cur(0, 1), nxt(0, 1), bspec],
        out_specs=[pl.BlockSpec((1, BLK, 3 * BR), lambda r, n: (n, 0, r)), bspec],
        out_shape=[jax.ShapeDtypeStruct((nb, BLK, dil * 3 * BR), F32), jax.ShapeDtypeStruct((8, BLK, 2 * BLK), F32)],
        compiler_params=_params("arbitrary", "arbitrary"),
    )(p3, p3, p3, p3, p3, p3, do3, do3, e3, e3, l3, l3, bias)
    return d.reshape(S, 3 * BR), db


def _attn_dsum(ds, dbg, BR):
    S = dbg.shape[0]
    TS = _tile(S, 512)

    def body(a_ref, b_ref, c_ref, g_ref, o_ref):
        o_ref[:, 0:3 * BR] = (a_ref[...] + b_ref[...] + c_ref[...]).astype(BF16)
        o_ref[:, 3 * BR:4 * BR] = g_ref[...].astype(BF16)

    spec3 = pl.BlockSpec((TS, 3 * BR), lambda i: (i, 0))
    return _call(
        body, name="attn_dsum", grid=(S // TS,),
        in_specs=[spec3, spec3, spec3, pl.BlockSpec((TS, BR), lambda i: (i, 0))],
        out_specs=pl.BlockSpec((TS, 4 * BR), lambda i: (i, 0)),
        out_shape=jax.ShapeDtypeStruct((S, 4 * BR), BF16),
        compiler_params=_params("arbitrary"),
    )(*ds, dbg)


def _lru_bwd(proj, h, dcat, rows, wa, wx, BR):
    S = proj.shape[0]
    TS = _tile(S, 256)
    PAD = TS // 2
    nt = S // TS

    def body(cx_ref, hx_ref, cg_ref, h_ref, hh_ref, dy_ref, r_ref, wa_ref, wx_ref,
             dp_ref, sum_ref, dwa_ref, dwx_ref, ext, hext, aext, dext, sa, sb, carry):
        i = pl.program_id(0)
        ti = nt - 1 - i

        @pl.when(i == 0)
        def _():
            sa[TS:TS + PAD, :] = jnp.ones((PAD, BR), F32)
            sb[TS:TS + PAD, :] = jnp.zeros((PAD, BR), F32)
            carry[...] = jnp.zeros_like(carry)
            dext[TS:TS + 8, :] = jnp.zeros((8, BR), F32)
            sum_ref[...] = jnp.zeros_like(sum_ref)
            dwa_ref[...] = jnp.zeros_like(dwa_ref)
            dwx_ref[...] = jnp.zeros_like(dwx_ref)

        w0, w1, w2, w3 = r_ref[0:1, :], r_ref[1:2, :], r_ref[2:3, :], r_ref[3:4, :]
        sp = r_ref[7:8, :]
        cx = cx_ref[...]
        ext[0:8, :] = jnp.where(ti > 0, hx_ref[...], 0.0)
        ext[8:8 + TS, :] = cx
        x3, x2, x1 = ext[pl.ds(5, TS), :], ext[pl.ds(6, TS), :], ext[pl.ds(7, TS), :]
        xc = x3 * w0 + x2 * w1 + x1 * w2 + cx * w3 + r_ref[4:5, :]
        wa_, wx_ = wa_ref[...], wx_ref[...]
        r, ig, a, mult = _lru_gates(xc, wa_, wx_, r_ref[5:6, :], r_ref[6:7, :], sp)
        cg, dy, hv = cg_ref[...], dy_ref[...], h_ref[...]
        dp_ref[:, BR:2 * BR] = (dy * hv * _dsilu(cg)).astype(BF16)
        aext[0:TS, :] = a
        aext[TS:TS + 8, :] = jnp.broadcast_to(carry[0:1, :], (8, BR))
        sa[0:TS, :] = aext[pl.ds(1, TS), :]
        sb[0:TS, :] = dy * _silu(cg)
        d = 1
        while d < TS:
            A, B = sa[0:TS, :], sb[0:TS, :]
            As, Bs = sa[pl.ds(d, TS), :], sb[pl.ds(d, TS), :]
            sb[0:TS, :] = B + A * Bs
            sa[0:TS, :] = A * As
            d *= 2
        gh = sb[0:TS, :] + sa[0:TS, :] * carry[1:2, :]
        carry[0:1, :] = a[0:1, :]
        carry[1:2, :] = gh[0:1, :]
        hext[0:8, :] = jnp.where(ti > 0, hh_ref[...], 0.0)
        hext[8:8 + TS, :] = hv
        da = gh * hext[pl.ds(7, TS), :]
        dmult = gh * ig * xc
        dig = gh * mult * xc
        dxc = gh * mult * ig
        dla = da * a - dmult * a * a / mult
        dpr = dla * (-LRU_C) * sp * r * (1.0 - r)
        dpi = dig * ig * (1.0 - ig)
        xb, dprb, dpib = xc.astype(BF16), dpr.astype(BF16), dpi.astype(BF16)
        dwa_ref[...] += _dot_tn(xb, dprb)
        dwx_ref[...] += _dot_tn(xb, dpib)
        dxc = dxc + _dot_nt(dprb, wa_) + _dot_nt(dpib, wx_)
        sum_ref[0:1, :] += _colsum(dxc * x3)
        sum_ref[1:2, :] += _colsum(dxc * x2)
        sum_ref[2:3, :] += _colsum(dxc * x1)
        sum_ref[3:4, :] += _colsum(dxc * cx)
        sum_ref[4:5, :] += _colsum(dxc)
        sum_ref[5:6, :] += _colsum(dpr)
        sum_ref[6:7, :] += _colsum(dpi)
        sum_ref[7:8, :] += _colsum(dla * (-LRU_C) * r)
        dext[0:TS, :] = dxc
        dcx = w3 * dxc + w2 * dext[pl.ds(1, TS), :] + w1 * dext[pl.ds(2, TS), :] + w0 * dext[pl.ds(3, TS), :]
        dext[TS:TS + 8, :] = dxc[0:8, :]
        dp_ref[:, 0:BR] = dcx.astype(BF16)

    def rev(k):
        return pl.BlockSpec((TS, BR), lambda i: (nt - 1 - i, k))

    def rev_halo(k):
        return pl.BlockSpec((8, BR), lambda i: (jnp.maximum((nt - 1 - i) * (TS // 8) - 1, 0), k))

    full = pl.BlockSpec((BR, BR), lambda i: (0, 0))
    rspec = pl.BlockSpec((8, BR), lambda i: (0, 0))
    return _call(
        body, name="lru_bwd", grid=(nt,),
        in_specs=[rev(8), rev_halo(8), rev(9), rev(0), rev_halo(0), rev(2), rspec, full, full],
        out_specs=[pl.BlockSpec((TS, 2 * BR), lambda i: (nt - 1 - i, 0)), rspec, full, full],
        out_shape=[jax.ShapeDtypeStruct((S, 2 * BR), BF16), jax.ShapeDtypeStruct((8, BR), F32),
                   jax.ShapeDtypeStruct((BR, BR), F32), jax.ShapeDtypeStruct((BR, BR), F32)],
        scratch_shapes=[pltpu.VMEM((TS + 8, BR), F32)] * 4 + [pltpu.VMEM((TS + PAD, BR), F32)] * 2 + [pltpu.VMEM((8, BR), F32)],
        compiler_params=_params("arbitrary"),
    )(proj, proj, proj, h, h, dcat, rows, wa, wx)


def _s5_bwd(proj, dcat, xre_all, xim_all, qw_re, qw_im, bbre, bbim, cre, cim, rows, wg, BR):
    S = proj.shape[0]
    TS, W = qw_re.shape
    PAD = TS // 2
    nt = S // TS
    CB = 8 * S5_CH
    SB = 8 * S5_STATE
    nblk = BR // CB

    def body(u_ref, dg_ref, dy_ref, xre_ref, xim_ref, hre_ref, him_ref, qr_ref, qi_ref, bbre_ref, bbim_ref, cre_ref, cim_ref,
             r_ref, wg_ref, dp_ref, sum_ref, dwg_ref, da_ref, dbbre_ref, dbbim_ref, dcre_ref, dcim_ref,
             sre, sim, ere, eim, ypre_s, dyp_s, carry):
        i = pl.program_id(0)
        ti = nt - 1 - i

        @pl.when(i == 0)
        def _():
            sre[TS:TS + PAD, :] = jnp.zeros((PAD, SB), F32)
            sim[TS:TS + PAD, :] = jnp.zeros((PAD, SB), F32)
            carry[...] = jnp.zeros_like(carry)
            for ref in (sum_ref, dwg_ref, da_ref, dbbre_ref, dbbim_ref, dcre_ref, dcim_ref):
                ref[...] = jnp.zeros_like(ref)

        u, dg, dy = u_ref[...], dg_ref[...], dy_ref[...]
        for m in range(nblk):
            cs, ws = slice(m * CB, (m + 1) * CB), slice(m * SB, (m + 1) * SB)
            ypre_s[:, cs] = (_dot(xre_ref[:, ws].astype(BF16), cre_ref[ws, :]) - _dot(xim_ref[:, ws].astype(BF16), cim_ref[ws, :]))
        ypre = ypre_s[...] + r_ref[0:1, :] * u
        yg = _gelu(ypre)
        wg_ = wg_ref[...]
        s = _sigmoid(_dot(yg.astype(BF16), wg_) + r_ref[1:2, :])
        dgl = dy * _silu(dg)
        dp_ref[:, BR:2 * BR] = (dy * yg * s * _dsilu(dg)).astype(BF16)
        dps = dgl * yg * s * (1.0 - s)
        dpsb = dps.astype(BF16)
        sum_ref[1:2, :] += _colsum(dps)
        dwg_ref[...] += _dot_tn(yg.astype(BF16), dpsb)
        dyp = (dgl * s + _dot_nt(dpsb, wg_)) * _dgelu(ypre)
        sum_ref[0:1, :] += _colsum(dyp * u)
        dyp_s[...] = dyp
        for m in range(nblk):
            cs, ws = slice(m * CB, (m + 1) * CB), slice(m * SB, (m + 1) * SB)
            dypb = dyp_s[:, cs].astype(BF16)
            xre, xim = xre_ref[:, ws], xim_ref[:, ws]
            dcre_ref[ws, :] += _dot_tn(xre.astype(BF16), dypb)
            dcim_ref[ws, :] -= _dot_tn(xim.astype(BF16), dypb)
            sre[0:TS, :] = _dot_nt(dypb, cre_ref[ws, :])
            sim[0:TS, :] = -_dot_nt(dypb, cim_ref[ws, :])
            d = 1
            while d < TS:
                ar, ai = qr_ref[TS - d:TS - d + 1, ws], -qi_ref[TS - d:TS - d + 1, ws]
                xr, xi = sre[0:TS, :], sim[0:TS, :]
                sr, si = sre[pl.ds(d, TS), :], sim[pl.ds(d, TS), :]
                sre[0:TS, :] = xr + ar * sr - ai * si
                sim[0:TS, :] = xi + ar * si + ai * sr
                d *= 2
            cr, ci = carry[0:1, ws], carry[1:2, ws]
            qr, qi = qr_ref[:, ws], -qi_ref[:, ws]
            gre = sre[0:TS, :] + qr * cr - qi * ci
            gim = sim[0:TS, :] + qr * ci + qi * cr
            carry[0:1, ws] = gre[0:1, :]
            carry[1:2, ws] = gim[0:1, :]
            ere[0:8, :] = jnp.where(ti > 0, hre_ref[:, ws], 0.0)
            eim[0:8, :] = jnp.where(ti > 0, him_ref[:, ws], 0.0)
            ere[8:8 + TS, :] = xre
            eim[8:8 + TS, :] = xim
            xpr, xpi = ere[pl.ds(7, TS), :], eim[pl.ds(7, TS), :]
            da_ref[0:1, ws] += _colsum(gre * xpr + gim * xpi)
            da_ref[1:2, ws] += _colsum(gim * xpr - gre * xpi)
            greb, gimb = gre.astype(BF16), gim.astype(BF16)
            ub = u_ref[:, cs].astype(BF16)
            dbbre_ref[cs, :] += _dot_tn(ub, greb)
            dbbim_ref[cs, :] += _dot_tn(ub, gimb)
            du = _dot_nt(greb, bbre_ref[cs, :]) + _dot_nt(gimb, bbim_ref[cs, :]) + r_ref[0:1, cs] * dyp_s[:, cs]
            dp_ref[:, cs] = du.astype(BF16)

    def rev(width, k):
        return pl.BlockSpec((TS, width), lambda i: (nt - 1 - i, k))

    def rev_halo(width):
        return pl.BlockSpec((8, width), lambda i: (jnp.maximum((nt - 1 - i) * (TS // 8) - 1, 0), 0))

    def const(shape):
        return pl.BlockSpec(shape, lambda i: (0, 0))

    return _call(
        body, name="s5_bwd", grid=(nt,),
        in_specs=[rev(BR, 10), rev(BR, 11), rev(BR, 3), rev(W, 0), rev(W, 0), rev_halo(W), rev_halo(W),
                  const((TS, W)), const((TS, W)), const((BR, SB)), const((BR, SB)), const((W, CB)), const((W, CB)),
                  const((8, BR)), const((BR, BR))],
        out_specs=[pl.BlockSpec((TS, 2 * BR), lambda i: (nt - 1 - i, 0)), const((8, BR)), const((BR, BR)), const((8, W)),
                   const((BR, SB)), const((BR, SB)), const((W, CB)), const((W, CB))],
        out_shape=[jax.ShapeDtypeStruct((S, 2 * BR), BF16), jax.ShapeDtypeStruct((8, BR), F32), jax.ShapeDtypeStruct((BR, BR), F32),
                   jax.ShapeDtypeStruct((8, W), F32), jax.ShapeDtypeStruct((BR, SB), F32), jax.ShapeDtypeStruct((BR, SB), F32),
                   jax.ShapeDtypeStruct((W, CB), F32), jax.ShapeDtypeStruct((W, CB), F32)],
        scratch_shapes=[pltpu.VMEM((TS + PAD, SB), F32)] * 2 + [pltpu.VMEM((TS + 8, SB), F32)] * 2
        + [pltpu.VMEM((TS, BR), F32)] * 2 + [pltpu.VMEM((8, W), F32)],
        compiler_params=_params("arbitrary"),
    )(proj, proj, dcat, xre_all, xim_all, xre_all, xim_all, qw_re, qw_im, bbre, bbim, cre, cim, rows, wg)


_WEIGHTS = ['rel_bias', 'w_ada', 'b_ada', 'w_in', 'conv_a', 'conv_c', 'conv_c_b', 'lru_wa', 'lru_ba', 'lru_wx', 'lru_bx',
            'lru_lambda', 's5_lam_re', 's5_lam_im', 's5_log_dt', 's5_b_re', 's5_b_im', 's5_c_re', 's5_c_im', 's5_d',
            's5_w_glu', 's5_b_glu', 'w_out', 'ln_g', 'ln_b']
_LAYER_SMALL = ['conv_a', 'conv_c', 'conv_c_b', 'lru_wa', 'lru_ba', 'lru_wx', 'lru_bx', 'lru_lambda', 's5_lam_re', 's5_lam_im',
                's5_log_dt', 's5_b_re', 's5_b_im', 's5_c_re', 's5_c_im', 's5_d', 's5_b_glu']
_SMALL = ['rel_bias'] + _LAYER_SMALL + ['ln_g', 'ln_b']


def _t5_bucket(dist):
    max_exact = REL_BUCKETS // 2
    nf = jnp.maximum(dist, 1).astype(F32)
    large = max_exact + (jnp.log(nf / max_exact) / math.log(REL_MAX_DIST / max_exact)
                         * (REL_BUCKETS - max_exact)).astype(jnp.int32)
    large = jnp.minimum(large, REL_BUCKETS - 1)
    return jnp.where(dist < max_exact, dist, large)


def _bias_tables(rel_bias):
    i = jnp.arange(BLK)[:, None]
    j = jnp.arange(2 * BLK)[None, :]
    delta = i + BLK - j
    out = []
    for window, dil in DILATIONS:
        bucket = _t5_bucket(jnp.clip(delta, 0, window // dil) * dil)
        onehot = (bucket[:, :, None] == jnp.arange(REL_BUCKETS)[None, None, :]).astype(F32)
        out.append(jnp.einsum('ijb,bh->hij', onehot, rel_bias, precision=lax.Precision.HIGHEST))
    return jnp.stack(out)


def _prep_layer(q):
    eye8 = jnp.eye(8, dtype=F32)
    G = q['s5_lam_re'].shape[0]
    nblk = G // 8

    def block_diag(w):
        hd = w.shape[1]
        return (w[:, :, None, :] * eye8[:, None, :, None]).reshape(8 * hd, 8 * hd)

    def compact_b(bb):
        t = jnp.transpose(bb.reshape(nblk, 8, S5_STATE, S5_CH), (0, 1, 3, 2))
        return (t[:, :, :, None, :] * eye8[None, :, None, :, None]).reshape(G * S5_CH, 8 * S5_STATE)

    def compact_c(cc):
        t = jnp.transpose(cc.reshape(nblk, 8, S5_CH, S5_STATE), (0, 1, 3, 2))
        return (t[:, :, :, None, :] * eye8[None, :, None, :, None]).reshape(G * S5_STATE, 8 * S5_CH)

    lam_re, lam_im = q['s5_lam_re'], q['s5_lam_im']
    dt = jnp.exp(q['s5_log_dt'])[:, None]
    mag = jnp.exp(lam_re * dt)
    ab_re = mag * jnp.cos(lam_im * dt)
    ab_im = mag * jnp.sin(lam_im * dt)
    den = lam_re * lam_re + lam_im * lam_im
    f_re = ((ab_re - 1.0) * lam_re + ab_im * lam_im) / den
    f_im = (ab_im * lam_re - (ab_re - 1.0) * lam_im) / den
    b_re, b_im = q['s5_b_re'], q['s5_b_im']
    bb_re = f_re[..., None] * b_re - f_im[..., None] * b_im
    bb_im = f_re[..., None] * b_im + f_im[..., None] * b_re
    return dict(
        conv_a=q['conv_a'], conv_c=q['conv_c'], conv_c_b=q['conv_c_b'], lru_ba=q['lru_ba'], lru_bx=q['lru_bx'],
        sp=jax.nn.softplus(-q['lru_lambda']), wa=block_diag(q['lru_wa']), wx=block_diag(q['lru_wx']),
        ar=ab_re.reshape(-1), ai=ab_im.reshape(-1), bbre=compact_b(bb_re), bbim=compact_b(bb_im),
        cre=compact_c(q['s5_c_re']), cim=compact_c(q['s5_c_im']), s5_d=q['s5_d'], s5_b_glu=q['s5_b_glu'])


def _powers(ar, ai, T):
    pr, pi = ar[None, :], ai[None, :]
    while pr.shape[0] < T:
        lr, li = pr[-1:], pi[-1:]
        pr, pi = jnp.concatenate([pr, pr * lr - pi * li]), jnp.concatenate([pi, pr * li + pi * lr])
    return pr[:T], pi[:T]


def _rows8(vecs, width):
    rows = [v.reshape(1, width).astype(F32) for v in vecs]
    return jnp.concatenate(rows + [jnp.zeros((8 - len(rows), width), F32)], axis=0)


def kernel(x, c, rel_bias, w_ada, b_ada, w_in, conv_a, conv_c, conv_c_b, lru_wa, lru_ba, lru_wx, lru_bx, lru_lambda, s5_lam_re, s5_lam_im, s5_log_dt, s5_b_re, s5_b_im, s5_c_re, s5_c_im, s5_d, s5_w_glu, s5_b_glu, w_out, ln_g, ln_b, loss_target, m_rel_bias, m_w_ada, m_b_ada, m_w_in, m_conv_a, m_conv_c, m_conv_c_b, m_lru_wa, m_lru_ba, m_lru_wx, m_lru_bx, m_lru_lambda, m_s5_lam_re, m_s5_lam_im, m_s5_log_dt, m_s5_b_re, m_s5_b_im, m_s5_c_re, m_s5_c_im, m_s5_d, m_s5_w_glu, m_s5_b_glu, m_w_out, m_ln_g, m_ln_b, v_rel_bias, v_w_ada, v_b_ada, v_w_in, v_conv_a, v_conv_c, v_conv_c_b, v_lru_wa, v_lru_ba, v_lru_wx, v_lru_bx, v_lru_lambda, v_s5_lam_re, v_s5_lam_im, v_s5_log_dt, v_s5_b_re, v_s5_b_im, v_s5_c_re, v_s5_c_im, v_s5_d, v_s5_w_glu, v_s5_b_glu, v_w_out, v_ln_g, v_ln_b):
    a = dict(locals())
    me = 4 * lax.axis_index("x") + 2 * lax.axis_index("y") + lax.axis_index("c")
    x0, target = x[0], loss_target[0]
    S, D = x0.shape
    BR = D // 4
    NS = w_in.shape[2]
    SH = BR // NDEV
    assert S % (BLK * DILATIONS[-1][1]) == 0 and BR % (8 * S5_CH) == 0
    TS5 = _tile(S, 256)

    small = _exchange(_pack([c, conv_a, conv_c]), True, "gather_small").reshape(NDEV, -1)
    c_all = small[:, :D]
    na, nc = DEPTH * 3 * SH, DEPTH * 4 * SH
    conv_a_full = jnp.transpose(small[:, D:D + na].reshape(NDEV, DEPTH, 3, SH), (1, 2, 0, 3)).reshape(DEPTH, 3, BR)
    conv_c_full = jnp.transpose(small[:, D + na:D + na + nc].reshape(NDEV, DEPTH, 4, SH), (1, 2, 0, 3)).reshape(DEPTH, 4, BR)
    wg_in = _exchange(w_in.astype(BF16), True, "gather_w_in")
    wg_out = _exchange(w_out.astype(BF16), True, "gather_w_out")
    wg_glu = _exchange(s5_w_glu.astype(BF16), True, "gather_w_glu")

    b_cols = lax.dynamic_slice(b_ada, (0, me * NS), (DEPTH, NS)).reshape(DEPTH, 1, NS)
    ada_all = _exchange(_ada_cols(c_all, w_ada, b_cols), True, "gather_ada")
    ada_me = lax.dynamic_index_in_dim(ada_all, me, axis=2, keepdims=False)
    ada_me = jnp.transpose(ada_me, (1, 0, 2)).reshape(DEPTH, 3 * D)
    shift, scale, gate = ada_me[:, :D], ada_me[:, D:2 * D], ada_me[:, 2 * D:]

    bias_tabs, bias_pull = jax.vjp(_bias_tables, rel_bias)
    HD = BR // 8
    head_ones = jnp.kron(jnp.eye(8, dtype=F32), jnp.ones((HD, HD), F32))

    saved = []
    xl = x0
    for l in range(DEPTH):
        q = {n: a[n][l] for n in _LAYER_SMALL}
        q['conv_a'], q['conv_c'] = conv_a_full[l], conv_c_full[l]
        kp, pull = jax.vjp(_prep_layer, q)
        pw_re, pw_im = _powers(lax.stop_gradient(kp['ar']), lax.stop_gradient(kp['ai']), TS5)
        w_in_l = wg_in[:, l]
        w_out_l = wg_out[:, l].reshape(D, D)
        w_glu_l = wg_glu[:, l].reshape(BR, BR)
        mod = _rows8([1.0 + scale[l], shift[l]], D)
        arows = _rows8(list(kp['conv_a']), BR)
        lrows = _rows8(list(kp['conv_c']) + [kp['conv_c_b'], kp['lru_ba'], kp['lru_bx'], kp['sp']], BR)
        srows = _rows8([kp['s5_d'], kp['s5_b_glu']], BR)
        orows = _rows8([1.0 + gate[l], ln_g[l], ln_b[l]], D)
        wa, wx = kp['wa'].astype(BF16), kp['wx'].astype(BF16)
        s5w = [kp[n].astype(BF16) for n in ('bbre', 'bbim', 'cre', 'cim')]

        proj, hbf = _proj(xl, mod, w_in_l)
        ya = _conv_a_fwd(proj, arows, BR)
        os_, ls_ = [], []
        for g, (_, dil) in enumerate(DILATIONS):
            o, lse = _attn_fwd(proj, bias_tabs[g], dil, BR)
            os_.append(o)
            ls_.append(lse)
        yb = _attn_mix_fwd(proj, os_, ls_, BR)
        yc, hs = _lru_fwd(proj, lrows, wa, wx, BR)
        yd, xre, xim = _s5_fwd(proj, pw_re, pw_im, *s5w, srows, w_glu_l, BR)
        xn, y, cat = _out_ln([ya, yb, yc, yd], w_out_l, xl, orows)
        saved.append(dict(x=xl, proj=proj, hbf=hbf, os=os_, ls=ls_, hs=hs, xre=xre, xim=xim, y=y, cat=cat, pull=pull,
                          mod=mod, arows=arows, lrows=lrows, srows=srows, orows=orows, wa=wa, wx=wx, s5w=s5w,
                          qw=(pw_re[::-1], pw_im[::-1]), w_in=w_in_l, w_out=w_out_l, w_glu=w_glu_l))
        xl = xn

    dout, loss_local = _loss_grad(xl, target)
    loss = lax.psum(loss_local, ("x", "y", "c"))

    dbias = jnp.zeros_like(bias_tabs)
    lgrads, dada, dw_in_parts, dw_out_parts, dw_glu_parts = [None] * DEPTH, [None] * DEPTH, [None] * DEPTH, [None] * DEPTH, [None] * DEPTH
    for l in reversed(range(DEPTH)):
        sv = saved[l]
        proj = sv['proj']
        dxres, dyb, dcat, lnsum = _ln_bwd(dout, sv['x'], sv['y'], sv['orows'], sv['w_out'])
        dw_out_parts[l] = _matmul_tn_rows(sv['cat'], dyb, D // NDEV, "dw_out").reshape(NDEV, D // NDEV, D)
        dpa, asum = _conv_a_bwd(proj, dcat, sv['arows'], BR)
        dos, es, dbg = _attn_mix_bwd(proj, dcat, sv['os'], sv['ls'], head_ones, BR)
        dqkv, dbs = [], []
        for g, (_, dil) in enumerate(DILATIONS):
            d_, db_ = _attn_bwd(proj, dos[g], es[g], sv['ls'][g], bias_tabs[g], dil, BR)
            dqkv.append(d_)
            dbs.append(db_)
        dbias = dbias + jnp.stack(dbs)
        dpb = _attn_dsum(dqkv, dbg, BR)
        dpc, lsum, dwa, dwx = _lru_bwd(proj, sv['hs'], dcat, sv['lrows'], sv['wa'], sv['wx'], BR)
        dpd, ssum, dwg, da_, dbbre, dbbim, dcre, dcim = _s5_bwd(
            proj, dcat, sv['xre'], sv['xim'], *sv['qw'], *sv['s5w'], sv['srows'], sv['w_glu'], BR)
        dw_glu_parts[l] = dwg.reshape(NDEV, BR // NDEV, BR)
        dproj = jnp.concatenate([dpa, dpb, dpc, dpd], axis=1)
        dout, msum = _dh(dproj, sv['w_in'], dxres, sv['x'], sv['mod'])
        dw_in_parts[l] = _matmul_tn_cols(sv['hbf'], dproj, NS, "dw_in")
        dada[l] = jnp.concatenate([msum[1], msum[0], lnsum[2]])
        dkp = dict(conv_a=asum[0:3], conv_c=lsum[0:4], conv_c_b=lsum[4], lru_ba=lsum[5], lru_bx=lsum[6], sp=lsum[7],
                   wa=dwa, wx=dwx, ar=da_[0], ai=da_[1], bbre=dbbre, bbim=dbbim, cre=dcre, cim=dcim,
                   s5_d=ssum[0], s5_b_glu=ssum[1])
        lgrads[l] = dict(sv['pull'](dkp)[0], ln_g=lnsum[0], ln_b=lnsum[1])
    grad_x = dout[None]

    out = {}

    def put(name, res):
        out[name] = res

    def stacked(fn):
        res = [fn(l) for l in range(DEPTH)]
        return tuple(jnp.stack([r[k] for r in res]) for k in range(4))

    put('w_in', stacked(lambda l: _adamw(_exchange(dw_in_parts[l], False, "scatter_w_in"), w_in[l], m_w_in[l], v_w_in[l], "adamw_w_in")))
    put('w_out', stacked(lambda l: _adamw(_exchange(dw_out_parts[l], False, "scatter_w_out"), w_out[l], m_w_out[l], v_w_out[l], "adamw_w_out")))
    put('s5_w_glu', stacked(lambda l: _adamw(_exchange(dw_glu_parts[l], False, "scatter_w_glu"), s5_w_glu[l], m_s5_w_glu[l], v_s5_w_glu[l], "adamw_w_glu")))

    dada_all = _exchange(jnp.stack(dada), True, "gather_dada")
    dada_cols = jnp.transpose(lax.dynamic_slice(dada_all, (0, 0, me * NS), (NDEV, DEPTH, NS)), (1, 0, 2))
    gw_ada, gb_ada = _ada_grads(c_all, dada_all, dada_cols)
    res = _adamw(gw_ada.reshape(DEPTH * D, NS), w_ada.reshape(DEPTH * D, NS), m_w_ada.reshape(DEPTH * D, NS),
                 v_w_ada.reshape(DEPTH * D, NS), "adamw_w_ada")
    put('w_ada', tuple(r.reshape(DEPTH, D, NS) for r in res))

    local = {'rel_bias': bias_pull(dbias)[0]}
    for n in _LAYER_SMALL + ['ln_g', 'ln_b']:
        local[n] = jnp.stack([lgrads[l][n] for l in range(DEPTH)])
    shapes = [local[n].shape for n in _SMALL]
    summed = _sum8(_exchange(_pack([local[n] for n in _SMALL]), True, "gather_small_grads"), "sum_small_grads")
    gs = dict(zip(_SMALL, _unpack(summed, shapes)))
    gs['conv_a'] = lax.dynamic_slice_in_dim(gs['conv_a'], me * SH, SH, axis=2)
    gs['conv_c'] = lax.dynamic_slice_in_dim(gs['conv_c'], me * SH, SH, axis=2)
    gs['b_ada'] = gb_ada
    names = _SMALL + ['b_ada']
    shapes = [a[n].shape for n in names]
    res = _adamw(_pack([gs[n] for n in names]), _pack([a[n] for n in names]), _pack([a['m_' + n] for n in names]),
                 _pack([a['v_' + n] for n in names]), "adamw_small")
    unpacked = [_unpack(r, shapes) for r in res]
    for k, n in enumerate(names):
        put(n, tuple(unpacked[j][k] for j in range(4)))

    return (loss, grad_x, *[out[n][0] for n in _WEIGHTS], *[out[n][1] for n in _WEIGHTS],
            *[out[n][2] for n in _WEIGHTS], *[out[n][3] for n in _WEIGHTS])
```

```python
import functools
import math

import jax
import jax.numpy as jnp
from jax import lax
from jax.experimental import pallas as pl
from jax.experimental.pallas import tpu as pltpu

F32 = jnp.float32
BF16 = jnp.bfloat16

NDEV = 8
DEPTH = 2
BLK = 128
DILATIONS = ((128, 1), (512, 4), (2048, 16))
REL_BUCKETS = 32
REL_MAX_DIST = 2048
LRU_C = 8.0
S5_CH = 16
S5_STATE = 64
ALPHA = (2 * DEPTH) ** 0.25
LN_EPS = 1e-5
ADAM_LR, ADAM_B1, ADAM_B2, ADAM_EPS, ADAM_WD, ADAM_STEP = 0.001, 0.9, 0.999, 1e-08, 0.01, 10
NEG = -1e30
VMEM_LIMIT = 56 * 1024 * 1024


def _call(body, **kw):
    return pl.pallas_call(body, **kw)


def _params(*sem):
    return pltpu.CompilerParams(dimension_semantics=sem, vmem_limit_bytes=VMEM_LIMIT)


def _sigmoid(x):
    return 1.0 / (1.0 + jnp.exp(-x))


def _silu(x):
    return x * _sigmoid(x)


def _dsilu(x):
    s = _sigmoid(x)
    return s * (1.0 + x * (1.0 - s))


_GC = math.sqrt(2.0 / math.pi)


def _gelu(x):
    return 0.5 * x * (1.0 + jnp.tanh(_GC * (x + 0.044715 * x * x * x)))


def _dgelu(x):
    t = jnp.tanh(_GC * (x + 0.044715 * x * x * x))
    return 0.5 * (1.0 + t) + 0.5 * x * (1.0 - t * t) * _GC * (1.0 + 3.0 * 0.044715 * x * x)


def _expm1(x):
    series = x * (1.0 + x * (1.0 / 2) * (1.0 + x * (1.0 / 3) * (1.0 + x * (1.0 / 4) * (1.0 + x * (1.0 / 5) * (1.0 + x * (1.0 / 6))))))
    return jnp.where(jnp.abs(x) < 0.25, series, jnp.exp(x) - 1.0)


def _dot(a, b):
    return jnp.dot(a, b, preferred_element_type=F32)


def _dot_nt(a, b):
    return lax.dot_general(a, b, (((1,), (1,)), ((), ())), preferred_element_type=F32)


def _dot_tn(a, b):
    return lax.dot_general(a, b, (((0,), (0,)), ((), ())), preferred_element_type=F32)


def _colsum(v):
    return jnp.sum(v, axis=0, keepdims=True)


def _exchange(x, gather, name):
    blk = x.shape if gather else x.shape[1:]

    def body(x_ref, o_ref, send_sems, recv_sems, local_sem):
        ix, iy, ic = lax.axis_index("x"), lax.axis_index("y"), lax.axis_index("c")
        me = 4 * ix + 2 * iy + ic

        def src(dev):
            return x_ref if gather else x_ref.at[dev]

        local = pltpu.make_async_copy(src(me), o_ref.at[me], local_sem)
        local.start()
        copies = []
        for k in range(1, NDEV):
            px = 1 - ix if k & 4 else ix
            py = 1 - iy if k & 2 else iy
            pc = 1 - ic if k & 1 else ic
            peer = 4 * px + 2 * py + pc
            cp = pltpu.make_async_remote_copy(
                src_ref=src(peer), dst_ref=o_ref.at[me], send_sem=send_sems.at[k - 1], recv_sem=recv_sems.at[k - 1],
                device_id=(px, py, pc), device_id_type=pl.DeviceIdType.MESH)
            cp.start()
            copies.append(cp)
        for cp in copies:
            cp.wait()
        local.wait()

    return _call(
        body, name=name,
        out_shape=jax.ShapeDtypeStruct((NDEV,) + tuple(blk), x.dtype),
        in_specs=[pl.BlockSpec(memory_space=pl.ANY)],
        out_specs=pl.BlockSpec(memory_space=pl.ANY),
        scratch_shapes=[pltpu.SemaphoreType.DMA((NDEV - 1,)), pltpu.SemaphoreType.DMA((NDEV - 1,)), pltpu.SemaphoreType.DMA],
    )(x)


def _pack(arrs):
    flat = jnp.concatenate([a.reshape(-1).astype(F32) for a in arrs])
    n = flat.shape[0]
    quantum = 1024 if n <= 65536 else 65536
    npad = -(-n // quantum) * quantum
    return jnp.pad(flat, (0, npad - n)).reshape(npad // 128, 128)


def _unpack(packed, shapes):
    flat = packed.reshape(-1)
    out, off = [], 0
    for s in shapes:
        n = math.prod(s)
        out.append(flat[off:off + n].reshape(s))
        off += n
    return out


def _sum8(parts, name):
    _, R, C = parts.shape
    TR = R
    for cand in (512, 256, 128, 64, 32, 16, 8):
        if R % cand == 0:
            TR = cand
            break

    def body(p_ref, o_ref):
        acc = p_ref[0]
        for d in range(1, NDEV):
            acc = acc + p_ref[d]
        o_ref[...] = acc

    return _call(
        body, name=name, grid=(R // TR,),
        in_specs=[pl.BlockSpec((NDEV, TR, C), lambda i: (0, i, 0))],
        out_specs=pl.BlockSpec((TR, C), lambda i: (i, 0)),
        out_shape=jax.ShapeDtypeStruct((R, C), F32),
        compiler_params=_params("arbitrary"),
    )(parts)


def _adamw(g_parts, w, m, v, name):
    R, C = w.shape
    nparts = g_parts.shape[0] if g_parts.ndim == 3 else 0
    TR = R
    for cand in (256, 128, 64, 32, 16, 8):
        if R % cand == 0:
            TR = cand
            break

    def body(g_ref, w_ref, m_ref, v_ref, go_ref, d_ref, mo_ref, vo_ref):
        if nparts:
            g = g_ref[0].astype(F32)
            for d in range(1, nparts):
                g = g + g_ref[d].astype(F32)
        else:
            g = g_ref[...]
        m2 = ADAM_B1 * m_ref[...] + (1.0 - ADAM_B1) * g
        v2 = ADAM_B2 * v_ref[...] + (1.0 - ADAM_B2) * (g * g)
        m_hat = m2 / (1.0 - ADAM_B1 ** ADAM_STEP)
        v_hat = v2 / (1.0 - ADAM_B2 ** ADAM_STEP)
        go_ref[...] = g
        d_ref[...] = -ADAM_LR * (m_hat / (jnp.sqrt(v_hat) + ADAM_EPS) + ADAM_WD * w_ref[...])
        mo_ref[...] = m2
        vo_ref[...] = v2

    spec = pl.BlockSpec((TR, C), lambda i: (i, 0))
    gspec = pl.BlockSpec((nparts, TR, C), lambda i: (0, i, 0)) if nparts else spec
    return _call(
        body, name=name, grid=(R // TR,),
        in_specs=[gspec, spec, spec, spec], out_specs=[spec] * 4,
        out_shape=[jax.ShapeDtypeStruct((R, C), F32)] * 4,
        compiler_params=_params("arbitrary"),
    )(g_parts, w, m, v)


def _ada_cols(c_all, w_ada, b_cols):
    L, D, NS = w_ada.shape

    def body(c_ref, w_ref, b_ref, o_ref):
        cond = _silu(c_ref[...]).astype(BF16)
        o_ref[0] = _dot(cond, w_ref[0].astype(BF16)) + b_ref[0]

    return _call(
        body, name="ada_cols", grid=(L,),
        in_specs=[pl.BlockSpec((NDEV, D), lambda l: (0, 0)), pl.BlockSpec((1, D, NS), lambda l: (l, 0, 0)),
                  pl.BlockSpec((1, 1, NS), lambda l: (l, 0, 0))],
        out_specs=pl.BlockSpec((1, NDEV, NS), lambda l: (l, 0, 0)),
        out_shape=jax.ShapeDtypeStruct((L, NDEV, NS), F32),
        compiler_params=_params("arbitrary"),
    )(c_all, w_ada, b_cols)


def _ada_grads(c_all, dada_all, dada_cols):
    _, D = c_all.shape
    L, _, NS = dada_cols.shape
    D3 = dada_all.shape[2]

    def body(c_ref, da_ref, dc_ref, gw_ref, gb_ref):
        cond = _silu(c_ref[...]).astype(BF16)
        gw_ref[0] = _dot_tn(cond, dc_ref[0].astype(BF16))
        acc = da_ref[0]
        for d in range(1, NDEV):
            acc = acc + da_ref[d]
        gb_ref[...] = acc

    return _call(
        body, name="ada_grads", grid=(L,),
        in_specs=[pl.BlockSpec((NDEV, D), lambda l: (0, 0)), pl.BlockSpec((NDEV, L, D3), lambda l: (0, 0, 0)),
                  pl.BlockSpec((1, NDEV, NS), lambda l: (l, 0, 0))],
        out_specs=[pl.BlockSpec((1, D, NS), lambda l: (l, 0, 0)), pl.BlockSpec((L, D3), lambda l: (0, 0))],
        out_shape=[jax.ShapeDtypeStruct((L, D, NS), F32), jax.ShapeDtypeStruct((L, D3), F32)],
        compiler_params=_params("arbitrary"),
    )(c_all, dada_all, dada_cols)


def _tile(S, want):
    return want if S % want == 0 else S


def _proj(x, mod, wg, l):
    S, D = x.shape
    NS = wg.shape[3]
    TS = _tile(S, 512)

    def body(x_ref, mod_ref, w_ref, o_ref, h_ref):
        @pl.when(pl.program_id(1) == 0)
        def _():
            h_ref[...] = (x_ref[...] * mod_ref[0:1, :] + mod_ref[1:2, :]).astype(BF16)
        o_ref[...] = _dot(h_ref[...], w_ref[0, 0])

    return _call(
        body, name="proj", grid=(S // TS, NDEV),
        in_specs=[pl.BlockSpec((TS, D), lambda i, j: (i, 0)), pl.BlockSpec((8, D), lambda i, j: (0, 0)),
                  pl.BlockSpec((1, 1, D, NS), lambda i, j: (j, l, 0, 0))],
        out_specs=[pl.BlockSpec((TS, NS), lambda i, j: (i, j)), pl.BlockSpec((TS, D), lambda i, j: (i, 0))],
        out_shape=[jax.ShapeDtypeStruct((S, NDEV * NS), F32), jax.ShapeDtypeStruct((S, D), BF16)],
        compiler_params=_params("arbitrary", "arbitrary"),
    )(x, mod, wg)


def _chunk(TS, BR, k):
    return pl.BlockSpec((TS, BR), lambda i: (i, k))


def _halo_prev(TS, BR, k):
    return pl.BlockSpec((8, BR), lambda i: (jnp.maximum(i * (TS // 8) - 1, 0), k))


def _conv_a_fwd(proj, w8, BR):
    S = proj.shape[0]
    TS = _tile(S, 512)

    def body(ab_ref, ac_ref, ax_ref, ag_ref, hc_ref, hx_ref, w_ref, o_ref, ext):
        i = pl.program_id(0)
        u = ac_ref[...] * ax_ref[...]
        ext[0:8, :] = jnp.where(i > 0, hc_ref[...] * hx_ref[...], 0.0)
        ext[8:8 + TS, :] = u
        conv = ext[pl.ds(6, TS), :] * w_ref[0:1, :] + ext[pl.ds(7, TS), :] * w_ref[1:2, :] + u * w_ref[2:3, :]
        o_ref[...] = ab_ref[...] * conv * _silu(ag_ref[...])

    return _call(
        body, name="conv_a_fwd", grid=(S // TS,),
        in_specs=[_chunk(TS, BR, 0), _chunk(TS, BR, 1), _chunk(TS, BR, 2), _chunk(TS, BR, 3),
                  _halo_prev(TS, BR, 1), _halo_prev(TS, BR, 2), pl.BlockSpec((8, BR), lambda i: (0, 0))],
        out_specs=pl.BlockSpec((TS, BR), lambda i: (i, 0)),
        out_shape=jax.ShapeDtypeStruct((S, BR), F32),
        scratch_shapes=[pltpu.VMEM((TS + 8, BR), F32)],
        compiler_params=_params("arbitrary"),
    )(proj, proj, proj, proj, proj, proj, w8)


def _attn_masks():
    i = lax.broadcasted_iota(jnp.int32, (BLK, BLK), 0)
    j = lax.broadcasted_iota(jnp.int32, (BLK, BLK), 1)
    return j >= i, j <= i


def _attn_fwd(proj, bias, dil, BR):
    S = proj.shape[0]
    HD = BR // 8
    nb = S // (BLK * dil)
    p3 = proj.reshape(nb, BLK, dil * 12 * BR)
    scale = HD ** -0.5

    def body(q_ref, kp_ref, kc_ref, vp_ref, vc_ref, b_ref, o_ref, l_ref):
        n = pl.program_id(1)
        mask_p, mask_c = _attn_masks()
        mask_p = jnp.logical_and(mask_p, n > 0)
        q = q_ref[0] * scale
        kp, kc, vp, vc = kp_ref[0], kc_ref[0], vp_ref[0], vc_ref[0]
        for h in range(8):
            sl = slice(h * HD, (h + 1) * HD)
            qh = q[:, sl].astype(BF16)
            sp = jnp.where(mask_p, _dot_nt(qh, kp[:, sl].astype(BF16)) + b_ref[h, :, 0:BLK], NEG)
            sc = jnp.where(mask_c, _dot_nt(qh, kc[:, sl].astype(BF16)) + b_ref[h, :, BLK:2 * BLK], NEG)
            m = jnp.maximum(jnp.max(sp, axis=-1, keepdims=True), jnp.max(sc, axis=-1, keepdims=True))
            pp = jnp.exp(sp - m)
            pc = jnp.exp(sc - m)
            l = jnp.sum(pp, axis=-1, keepdims=True) + jnp.sum(pc, axis=-1, keepdims=True)
            inv = 1.0 / l
            oh = _dot((pp * inv).astype(BF16), vp[:, sl].astype(BF16)) + _dot((pc * inv).astype(BF16), vc[:, sl].astype(BF16))
            o_ref[0, :, sl] = oh
            l_ref[0, :, sl] = jnp.broadcast_to(m + jnp.log(l), (BLK, HD))

    def cur(c):
        return pl.BlockSpec((1, BLK, BR), lambda r, n: (n, 0, r * 12 + c))

    def prev(c):
        return pl.BlockSpec((1, BLK, BR), lambda r, n: (jnp.maximum(n - 1, 0), 0, r * 12 + c))

    ospec = pl.BlockSpec((1, BLK, BR), lambda r, n: (n, 0, r))
    o, lse = _call(
        body, name="attn_fwd_d%d" % dil, grid=(dil, nb),
        in_specs=[cur(4), prev(5), cur(5), prev(6), cur(6), pl.BlockSpec((8, BLK, 2 * BLK), lambda r, n: (0, 0, 0))],
        out_specs=[ospec, ospec],
        out_shape=[jax.ShapeDtypeStruct((nb, BLK, dil * BR), F32)] * 2,
        compiler_params=_params("arbitrary", "arbitrary"),
    )(p3, p3, p3, p3, p3, bias)
    return o.reshape(S, BR), lse.reshape(S, BR)


def _mix_weights(l1, l2, l3):
    m = jnp.maximum(jnp.maximum(l1, l2), l3)
    e1, e2, e3 = jnp.exp(l1 - m), jnp.exp(l2 - m), jnp.exp(l3 - m)
    inv = 1.0 / (e1 + e2 + e3)
    return e1 * inv, e2 * inv, e3 * inv


def _attn_mix_fwd(proj, os_, ls_, BR):
    S = proj.shape[0]
    TS = _tile(S, 512)

    def body(o1, o2, o3, l1, l2, l3, g_ref, y_ref):
        w1, w2, w3 = _mix_weights(l1[...], l2[...], l3[...])
        y_ref[...] = (w1 * o1[...] + w2 * o2[...] + w3 * o3[...]) * _silu(g_ref[...])

    spec = pl.BlockSpec((TS, BR), lambda i: (i, 0))
    return _call(
        body, name="attn_mix_fwd", grid=(S // TS,),
        in_specs=[spec] * 6 + [_chunk(TS, BR, 7)], out_specs=spec,
        out_shape=jax.ShapeDtypeStruct((S, BR), F32),
        compiler_params=_params("arbitrary"),
    )(*os_, *ls_, proj)


def _lru_gates(xc, wa, wx, ba, bx, sp):
    xb = xc.astype(BF16)
    r = _sigmoid(_dot(xb, wa) + ba)
    ig = _sigmoid(_dot(xb, wx) + bx)
    la = -LRU_C * r * sp
    a = jnp.exp(la)
    mult = jnp.sqrt(-_expm1(2.0 * la))
    return r, ig, a, mult


def _lru_fwd(proj, rows, wa, wx, BR):
    S = proj.shape[0]
    TS = _tile(S, 256)
    PAD = TS // 2

    def body(cx_ref, hx_ref, cg_ref, r_ref, wa_ref, wx_ref, y_ref, h_ref, ext, sa, sb, carry):
        i = pl.program_id(0)

        @pl.when(i == 0)
        def _():
            sa[0:PAD, :] = jnp.ones((PAD, BR), F32)
            sb[0:PAD, :] = jnp.zeros((PAD, BR), F32)
            carry[...] = jnp.zeros_like(carry)

        cx = cx_ref[...]
        ext[0:8, :] = jnp.where(i > 0, hx_ref[...], 0.0)
        ext[8:8 + TS, :] = cx
        xc = (ext[pl.ds(5, TS), :] * r_ref[0:1, :] + ext[pl.ds(6, TS), :] * r_ref[1:2, :]
              + ext[pl.ds(7, TS), :] * r_ref[2:3, :] + cx * r_ref[3:4, :] + r_ref[4:5, :])
        _, ig, a, mult = _lru_gates(xc, wa_ref[...], wx_ref[...], r_ref[5:6, :], r_ref[6:7, :], r_ref[7:8, :])
        sa[PAD:PAD + TS, :] = a
        sb[PAD:PAD + TS, :] = mult * ig * xc
        d = 1
        while d < TS:
            A, B = sa[PAD:PAD + TS, :], sb[PAD:PAD + TS, :]
            As, Bs = sa[pl.ds(PAD - d, TS), :], sb[pl.ds(PAD - d, TS), :]
            sb[PAD:PAD + TS, :] = A * Bs + B
            sa[PAD:PAD + TS, :] = A * As
            d *= 2
        h = sb[PAD:PAD + TS, :] + sa[PAD:PAD + TS, :] * carry[0:1, :]
        carry[0:1, :] = h[TS - 1:TS, :]
        h_ref[...] = h
        y_ref[...] = h * _silu(cg_ref[...])

    full = pl.BlockSpec((BR, BR), lambda i: (0, 0))
    spec = pl.BlockSpec((TS, BR), lambda i: (i, 0))
    return _call(
        body, name="lru_fwd", grid=(S // TS,),
        in_specs=[_chunk(TS, BR, 8), _halo_prev(TS, BR, 8), _chunk(TS, BR, 9), pl.BlockSpec((8, BR), lambda i: (0, 0)), full, full],
        out_specs=[spec, spec],
        out_shape=[jax.ShapeDtypeStruct((S, BR), F32)] * 2,
        scratch_shapes=[pltpu.VMEM((TS + 8, BR), F32), pltpu.VMEM((PAD + TS, BR), F32), pltpu.VMEM((PAD + TS, BR), F32),
                        pltpu.VMEM((8, BR), F32)],
        compiler_params=_params("arbitrary"),
    )(proj, proj, proj, rows, wa, wx)


def _s5_fwd(proj, pw_re, pw_im, bbre, bbim, cre, cim, rows, wg, BR):
    S = proj.shape[0]
    TS, W = pw_re.shape
    PAD = TS // 2
    CB = 8 * S5_CH
    SB = 8 * S5_STATE
    nblk = BR // CB

    def body(u_ref, dg_ref, pr_ref, pi_ref, bbre_ref, bbim_ref, cre_ref, cim_ref, r_ref, wg_ref,
             y_ref, xre_ref, xim_ref, sre, sim, ypre_s, carry):
        i = pl.program_id(0)

        @pl.when(i == 0)
        def _():
            sre[0:PAD, :] = jnp.zeros((PAD, SB), F32)
            sim[0:PAD, :] = jnp.zeros((PAD, SB), F32)
            carry[...] = jnp.zeros_like(carry)

        for m in range(nblk):
            cs, ws = slice(m * CB, (m + 1) * CB), slice(m * SB, (m + 1) * SB)
            ub = u_ref[:, cs].astype(BF16)
            sre[PAD:PAD + TS, :] = _dot(ub, bbre_ref[cs, :])
            sim[PAD:PAD + TS, :] = _dot(ub, bbim_ref[cs, :])
            d = 1
            while d < TS:
                ar, ai = pr_ref[d - 1:d, ws], pi_ref[d - 1:d, ws]
                xr, xi = sre[PAD:PAD + TS, :], sim[PAD:PAD + TS, :]
                sr, si = sre[pl.ds(PAD - d, TS), :], sim[pl.ds(PAD - d, TS), :]
                sre[PAD:PAD + TS, :] = xr + ar * sr - ai * si
                sim[PAD:PAD + TS, :] = xi + ar * si + ai * sr
                d *= 2
            cr, ci = carry[0:1, ws], carry[1:2, ws]
            pr, pi = pr_ref[:, ws], pi_ref[:, ws]
            xre = sre[PAD:PAD + TS, :] + pr * cr - pi * ci
            xim = sim[PAD:PAD + TS, :] + pr * ci + pi * cr
            carry[0:1, ws] = xre[TS - 1:TS, :]
            carry[1:2, ws] = xim[TS - 1:TS, :]
            xre_ref[:, ws] = xre
            xim_ref[:, ws] = xim
            ypre_s[:, cs] = _dot(xre.astype(BF16), cre_ref[ws, :]) - _dot(xim.astype(BF16), cim_ref[ws, :])
        dg = dg_ref[...]
        yg = _gelu(ypre_s[...] + r_ref[0:1, :] * u_ref[...])
        s = _sigmoid(_dot(yg.astype(BF16), wg_ref[...]) + r_ref[1:2, :])
        y_ref[...] = yg * s * _silu(dg)

    def const(shape):
        return pl.BlockSpec(shape, lambda i: (0, 0))

    return _call(
        body, name="s5_fwd", grid=(S // TS,),
        in_specs=[_chunk(TS, BR, 10), _chunk(TS, BR, 11), const((TS, W)), const((TS, W)), const((BR, SB)), const((BR, SB)),
                  const((W, CB)), const((W, CB)), const((8, BR)), const((BR, BR))],
        out_specs=[pl.BlockSpec((TS, BR), lambda i: (i, 0)), pl.BlockSpec((TS, W), lambda i: (i, 0)), pl.BlockSpec((TS, W), lambda i: (i, 0))],
        out_shape=[jax.ShapeDtypeStruct((S, BR), F32), jax.ShapeDtypeStruct((S, W), F32), jax.ShapeDtypeStruct((S, W), F32)],
        scratch_shapes=[pltpu.VMEM((PAD + TS, SB), F32), pltpu.VMEM((PAD + TS, SB), F32), pltpu.VMEM((TS, BR), F32), pltpu.VMEM((8, W), F32)],
        compiler_params=_params("arbitrary"),
    )(proj, proj, pw_re, pw_im, bbre, bbim, cre, cim, rows, wg)


def _out_ln(ys, w_out, x, rows):
    S, D = x.shape
    BR = D // 4
    TS = _tile(S, 256)

    def body(ya, yb, yc, yd, w_ref, x_ref, r_ref, o_ref, y_ref, cat_ref):
        cat = jnp.concatenate([ya[...], yb[...], yc[...], yd[...]], axis=1).astype(BF16)
        y = _dot(cat, w_ref[...])
        z = ALPHA * x_ref[...] + r_ref[0:1, :] * y
        mu = jnp.mean(z, axis=-1, keepdims=True)
        zc = z - mu
        var = jnp.mean(zc * zc, axis=-1, keepdims=True)
        o_ref[...] = zc * lax.rsqrt(var + LN_EPS) * r_ref[1:2, :] + r_ref[2:3, :]
        y_ref[...] = y
        cat_ref[...] = cat

    yspec = pl.BlockSpec((TS, BR), lambda i: (i, 0))
    spec = pl.BlockSpec((TS, D), lambda i: (i, 0))
    return _call(
        body, name="out_ln", grid=(S // TS,),
        in_specs=[yspec] * 4 + [pl.BlockSpec((D, D), lambda i: (0, 0)), spec, pl.BlockSpec((8, D), lambda i: (0, 0))],
        out_specs=[spec, spec, spec],
        out_shape=[jax.ShapeDtypeStruct((S, D), F32), jax.ShapeDtypeStruct((S, D), F32), jax.ShapeDtypeStruct((S, D), BF16)],
        compiler_params=_params("arbitrary"),
    )(*ys, w_out, x, rows)


def _loss_grad(x, target):
    S, D = x.shape
    TS = _tile(S, 512)

    def body(x_ref, t_ref, g_ref, l_ref):
        @pl.when(pl.program_id(0) == 0)
        def _():
            l_ref[...] = jnp.zeros_like(l_ref)
        diff = x_ref[...] - t_ref[...]
        g_ref[...] = diff * (1.0 / D)
        l_ref[...] += jnp.sum(_colsum(diff * diff), axis=1, keepdims=True) * (0.5 / D)

    spec = pl.BlockSpec((TS, D), lambda i: (i, 0))
    g, l = _call(
        body, name="loss_grad", grid=(S // TS,),
        in_specs=[spec, spec], out_specs=[spec, pl.BlockSpec((1, 1), lambda i: (0, 0))],
        out_shape=[jax.ShapeDtypeStruct((S, D), F32), jax.ShapeDtypeStruct((1, 1), F32)],
        compiler_params=_params("arbitrary"),
    )(x, target)
    return g, l[0, 0]


def _ln_bwd(dout, x, y, rows, w_out):
    S, D = x.shape
    TS = _tile(S, 256)

    def body(do_ref, x_ref, y_ref, r_ref, w_ref, dxr_ref, dyb_ref, dcat_ref, acc_ref):
        @pl.when(pl.program_id(0) == 0)
        def _():
            acc_ref[...] = jnp.zeros_like(acc_ref)
        g1, lg = r_ref[0:1, :], r_ref[1:2, :]
        yv = y_ref[...]
        z = ALPHA * x_ref[...] + g1 * yv
        mu = jnp.mean(z, axis=-1, keepdims=True)
        zc = z - mu
        var = jnp.mean(zc * zc, axis=-1, keepdims=True)
        rstd = lax.rsqrt(var + LN_EPS)
        xhat = zc * rstd
        do = do_ref[...]
        dxh = do * lg
        dz = rstd * (dxh - jnp.mean(dxh, axis=-1, keepdims=True) - xhat * jnp.mean(dxh * xhat, axis=-1, keepdims=True))
        dxr_ref[...] = ALPHA * dz
        dyb = (g1 * dz).astype(BF16)
        dyb_ref[...] = dyb
        dcat_ref[...] = _dot_nt(dyb, w_ref[...])
        acc_ref[0:1, :] += _colsum(do * xhat)
        acc_ref[1:2, :] += _colsum(do)
        acc_ref[2:3, :] += _colsum(dz * yv)

    spec = pl.BlockSpec((TS, D), lambda i: (i, 0))
    rspec = pl.BlockSpec((8, D), lambda i: (0, 0))
    return _call(
        body, name="ln_bwd", grid=(S // TS,),
        in_specs=[spec, spec, spec, rspec, pl.BlockSpec((D, D), lambda i: (0, 0))],
        out_specs=[spec, spec, spec, rspec],
        out_shape=[jax.ShapeDtypeStruct((S, D), F32), jax.ShapeDtypeStruct((S, D), BF16), jax.ShapeDtypeStruct((S, D), F32),
                   jax.ShapeDtypeStruct((8, D), F32)],
        compiler_params=_params("arbitrary"),
    )(dout, x, y, rows, w_out)


def _matmul_tn_rows(a, b, TM, name):
    S, M = a.shape
    N = b.shape[1]
    TK = _tile(S, 512)
    nk = S // TK

    def body(a_ref, b_ref, o_ref, acc):
        k = pl.program_id(1)

        @pl.when(k == 0)
        def _():
            acc[...] = jnp.zeros_like(acc)
        acc[...] += _dot_tn(a_ref[...], b_ref[...])

        @pl.when(k == nk - 1)
        def _():
            o_ref[...] = acc[...].astype(BF16)

    return _call(
        body, name=name, grid=(M // TM, nk),
        in_specs=[pl.BlockSpec((TK, TM), lambda i, k: (k, i)), pl.BlockSpec((TK, N), lambda i, k: (k, 0))],
        out_specs=pl.BlockSpec((TM, N), lambda i, k: (i, 0)),
        out_shape=jax.ShapeDtypeStruct((M, N), BF16),
        scratch_shapes=[pltpu.VMEM((TM, N), F32)],
        compiler_params=_params("arbitrary", "arbitrary"),
    )(a, b)


def _matmul_tn_cols(a, b, TN, name):
    S, M = a.shape
    N = b.shape[1]
    TK = _tile(S, 512)
    nk = S // TK

    def body(a_ref, b_ref, o_ref, acc):
        k = pl.program_id(1)

        @pl.when(k == 0)
        def _():
            acc[...] = jnp.zeros_like(acc)
        acc[...] += _dot_tn(a_ref[...], b_ref[...])

        @pl.when(k == nk - 1)
        def _():
            o_ref[0] = acc[...].astype(BF16)

    return _call(
        body, name=name, grid=(N // TN, nk),
        in_specs=[pl.BlockSpec((TK, M), lambda j, k: (k, 0)), pl.BlockSpec((TK, TN), lambda j, k: (k, j))],
        out_specs=pl.BlockSpec((1, M, TN), lambda j, k: (j, 0, 0)),
        out_shape=jax.ShapeDtypeStruct((N // TN, M, TN), BF16),
        scratch_shapes=[pltpu.VMEM((M, TN), F32)],
        compiler_params=_params("arbitrary", "arbitrary"),
    )(a, b)


def _dh(dproj, wg, l, dxres, x, mod):
    S, D = x.shape
    NS = wg.shape[3]
    TS = _tile(S, 512)

    def body(dp_ref, w_ref, dxr_ref, x_ref, mod_ref, dx_ref, sum_ref, acc):
        i, j = pl.program_id(0), pl.program_id(1)

        @pl.when(jnp.logical_and(i == 0, j == 0))
        def _():
            sum_ref[...] = jnp.zeros_like(sum_ref)

        @pl.when(j == 0)
        def _():
            acc[...] = jnp.zeros_like(acc)

        acc[...] += _dot_nt(dp_ref[...], w_ref[0, 0])

        @pl.when(j == NDEV - 1)
        def _():
            dh = acc[...]
            dx_ref[...] = dxr_ref[...] + dh * mod_ref[0:1, :]
            sum_ref[0:1, :] += _colsum(dh * x_ref[...])
            sum_ref[1:2, :] += _colsum(dh)

    spec = pl.BlockSpec((TS, D), lambda i, j: (i, 0))
    rspec = pl.BlockSpec((8, D), lambda i, j: (0, 0))
    return _call(
        body, name="dh", grid=(S // TS, NDEV),
        in_specs=[pl.BlockSpec((TS, NS), lambda i, j: (i, j)), pl.BlockSpec((1, 1, D, NS), lambda i, j: (j, l, 0, 0)), spec, spec, rspec],
        out_specs=[spec, rspec],
        out_shape=[jax.ShapeDtypeStruct((S, D), F32), jax.ShapeDtypeStruct((8, D), F32)],
        scratch_shapes=[pltpu.VMEM((TS, D), F32)],
        compiler_params=_params("arbitrary", "arbitrary"),
    )(dproj, wg, dxres, x, mod)


def _halo_next(S, TS, BR, k):
    last = S // 8 - 1
    return pl.BlockSpec((8, BR), lambda i: (jnp.minimum((i + 1) * (TS // 8), last), k))


def _conv_a_bwd(proj, dcat, w8, BR):
    S = proj.shape[0]
    TS = _tile(S, 512)
    nt = S // TS

    def body(ab_ref, ac_ref, ax_ref, ag_ref, dy_ref, hc_ref, hx_ref, nab_ref, nag_ref, ndy_ref, w_ref, dp_ref, dw_ref, ext, dext):
        i = pl.program_id(0)

        @pl.when(i == 0)
        def _():
            dw_ref[...] = jnp.zeros_like(dw_ref)

        w0, w1, w2 = w_ref[0:1, :], w_ref[1:2, :], w_ref[2:3, :]
        ab, ac, ax, ag, dy = ab_ref[...], ac_ref[...], ax_ref[...], ag_ref[...], dy_ref[...]
        u = ac * ax
        ext[0:8, :] = jnp.where(i > 0, hc_ref[...] * hx_ref[...], 0.0)
        ext[8:8 + TS, :] = u
        u1, u2 = ext[pl.ds(7, TS), :], ext[pl.ds(6, TS), :]
        conv = u2 * w0 + u1 * w1 + u * w2
        sg = _silu(ag)
        dconv = dy * ab * sg
        dext[0:TS, :] = dconv
        dext[TS:TS + 8, :] = jnp.where(i < nt - 1, ndy_ref[...] * nab_ref[...] * _silu(nag_ref[...]), 0.0)
        du = w2 * dconv + w1 * dext[pl.ds(1, TS), :] + w0 * dext[pl.ds(2, TS), :]
        dp_ref[:, 0:BR] = (dy * conv * sg).astype(BF16)
        dp_ref[:, BR:2 * BR] = (du * ax).astype(BF16)
        dp_ref[:, 2 * BR:3 * BR] = (du * ac).astype(BF16)
        dp_ref[:, 3 * BR:4 * BR] = (dy * ab * conv * _dsilu(ag)).astype(BF16)
        dw_ref[0:1, :] += _colsum(dconv * u2)
        dw_ref[1:2, :] += _colsum(dconv * u1)
        dw_ref[2:3, :] += _colsum(dconv * u)

    rspec = pl.BlockSpec((8, BR), lambda i: (0, 0))
    return _call(
        body, name="conv_a_bwd", grid=(nt,),
        in_specs=[_chunk(TS, BR, 0), _chunk(TS, BR, 1), _chunk(TS, BR, 2), _chunk(TS, BR, 3), _chunk(TS, BR, 0),
                  _halo_prev(TS, BR, 1), _halo_prev(TS, BR, 2), _halo_next(S, TS, BR, 0), _halo_next(S, TS, BR, 3),
                  _halo_next(S, TS, BR, 0), rspec],
        out_specs=[pl.BlockSpec((TS, 4 * BR), lambda i: (i, 0)), rspec],
        out_shape=[jax.ShapeDtypeStruct((S, 12 * BR), BF16), jax.ShapeDtypeStruct((8, BR), F32)],
        scratch_shapes=[pltpu.VMEM((TS + 8, BR), F32), pltpu.VMEM((TS + 8, BR), F32)],
        compiler_params=_params("arbitrary"),
    )(proj, proj, proj, proj, dcat, proj, proj, proj, proj, dcat, w8)


def _attn_mix_bwd(proj, dcat, os_, ls_, head_ones, BR):
    S = proj.shape[0]
    TS = _tile(S, 512)

    def body(dy_ref, g_ref, o1, o2, o3, l1, l2, l3, ones_ref, d1, d2, d3, e1, e2, e3, dg_ref):
        w1, w2, w3 = _mix_weights(l1[...], l2[...], l3[...])
        mix = w1 * o1[...] + w2 * o2[...] + w3 * o3[...]
        g = g_ref[...]
        dy = dy_ref[...]
        dmix = dy * _silu(g)
        dg_ref[...] = dy * mix * _dsilu(g)
        t = jnp.dot(dmix * mix, ones_ref[...], preferred_element_type=F32, precision=lax.Precision.HIGHEST)
        d1[...] = w1 * dmix
        d2[...] = w2 * dmix
        d3[...] = w3 * dmix
        e1[...] = -w1 * t
        e2[...] = -w2 * t
        e3[...] = -w3 * t

    spec = pl.BlockSpec((TS, BR), lambda i: (i, 0))
    outs = _call(
        body, name="attn_mix_bwd", grid=(S // TS,),
        in_specs=[_chunk(TS, BR, 1), _chunk(TS, BR, 7)] + [spec] * 6 + [pl.BlockSpec((BR, BR), lambda i: (0, 0))],
        out_specs=[spec] * 7,
        out_shape=[jax.ShapeDtypeStruct((S, BR), F32)] * 7,
        compiler_params=_params("arbitrary"),
    )(dcat, proj, *os_, *ls_, head_ones)
    return outs[0:3], outs[3:6], outs[6]


def _attn_bwd(proj, do, e, lse, bias, dil, BR):
    S = proj.shape[0]
    HD = BR // 8
    nb = S // (BLK * dil)
    p3 = proj.reshape(nb, BLK, dil * 12 * BR)
    do3, e3, l3 = (t.reshape(nb, BLK, dil * BR) for t in (do, e, lse))
    scale = HD ** -0.5

    def body(qc_ref, qn_ref, kp_ref, kc_ref, vp_ref, vc_ref, doc_ref, don_ref, ec_ref, en_ref, lc_ref, ln_ref, b_ref,
             d_ref, db_ref):
        r, n = pl.program_id(0), pl.program_id(1)

        @pl.when(jnp.logical_and(r == 0, n == 0))
        def _():
            db_ref[...] = jnp.zeros_like(db_ref)

        mask_p, mask_c = _attn_masks()
        mask_a = jnp.logical_and(mask_p, n > 0)
        mask_n = jnp.logical_and(mask_p, n < nb - 1)
        qc, qn = qc_ref[0] * scale, qn_ref[0] * scale
        kp, kc, vp, vc = kp_ref[0], kc_ref[0], vp_ref[0], vc_ref[0]
        doc, don, ec, en, lc, ln = doc_ref[0], don_ref[0], ec_ref[0], en_ref[0], lc_ref[0], ln_ref[0]
        for h in range(8):
            sl = slice(h * HD, (h + 1) * HD)
            one = slice(h * HD, h * HD + 1)
            qh, qnh = qc[:, sl].astype(BF16), qn[:, sl].astype(BF16)
            kph, kch = kp[:, sl].astype(BF16), kc[:, sl].astype(BF16)
            vph, vch = vp[:, sl].astype(BF16), vc[:, sl].astype(BF16)
            dob, donb = doc[:, sl].astype(BF16), don[:, sl].astype(BF16)
            lse_c, lse_n, e_c, e_n = lc[:, one], ln[:, one], ec[:, one], en[:, one]
            pa = jnp.exp(jnp.where(mask_a, _dot_nt(qh, kph) + b_ref[h, :, 0:BLK], NEG) - lse_c)
            pb = jnp.exp(jnp.where(mask_c, _dot_nt(qh, kch) + b_ref[h, :, BLK:2 * BLK], NEG) - lse_c)
            pc = jnp.exp(jnp.where(mask_n, _dot_nt(qnh, kch) + b_ref[h, :, 0:BLK], NEG) - lse_n)
            dsa = pa * (_dot_nt(dob, vph) + e_c)
            dsb = pb * (_dot_nt(dob, vch) + e_c)
            dsc = pc * (_dot_nt(donb, vch) + e_n)
            dsab, dsbb, dscb = dsa.astype(BF16), dsb.astype(BF16), dsc.astype(BF16)
            d_ref[0, :, h * HD:(h + 1) * HD] = (_dot(dsab, kph) + _dot(dsbb, kch)) * scale
            d_ref[0, :, BR + h * HD:BR + (h + 1) * HD] = _dot_tn(dsbb, qh) + _dot_tn(dscb, qnh)
            d_ref[0, :, 2 * BR + h * HD:2 * BR + (h + 1) * HD] = _dot_tn(pb.astype(BF16), dob) + _dot_tn(pc.astype(BF16), donb)
            db_ref[h, :, 0:BLK] += dsa
            db_ref[h, :, BLK:2 * BLK] += dsb

    def cur(c, w):
        return pl.BlockSpec((1, BLK, BR), lambda r, n: (n, 0, r * w + c))

    def prev(c, w):
        return pl.BlockSpec((1, BLK, BR), lambda r, n: (jnp.maximum(n - 1, 0), 0, r * w + c))

    def nxt(c, w):
        return pl.BlockSpec((1, BLK, BR), lambda r, n: (jnp.minimum(n + 1, nb - 1), 0, r * w + c))

    bspec = pl.BlockSpec((8, BLK, 2 * BLK), lambda r, n: (0, 0, 0))
    d, db = _call(
        body, name="attn_bwd_d%d" % dil, grid=(dil, nb),
        in_specs=[cur(4, 12), nxt(4, 12), prev(5, 12), cur(5, 12), prev(6, 12), cur(6, 12),
                  cur(0, 1), nxt(0, 1), cur(0, 1), nxt(0, 1), cur(0, 1), nxt(0, 1), bspec],
        out_specs=[pl.BlockSpec((1, BLK, 3 * BR), lambda r, n: (n, 0, r)), bspec],
        out_shape=[jax.ShapeDtypeStruct((nb, BLK, dil * 3 * BR), F32), jax.ShapeDtypeStruct((8, BLK, 2 * BLK), F32)],
        compiler_params=_params("arbitrary", "arbitrary"),
    )(p3, p3, p3, p3, p3, p3, do3, do3, e3, e3, l3, l3, bias)
    return d.reshape(S, 3 * BR), db


def _attn_dsum(ds, dbg, dproj, BR):
    S = dbg.shape[0]
    TS = _tile(S, 512)

    def body(a_ref, b_ref, c_ref, g_ref, _, o_ref):
        o_ref[:, 0:3 * BR] = (a_ref[...] + b_ref[...] + c_ref[...]).astype(BF16)
        o_ref[:, 3 * BR:4 * BR] = g_ref[...].astype(BF16)

    spec3 = pl.BlockSpec((TS, 3 * BR), lambda i: (i, 0))
    return _call(
        body, name="attn_dsum", grid=(S // TS,),
        in_specs=[spec3, spec3, spec3, pl.BlockSpec((TS, BR), lambda i: (i, 0)), pl.BlockSpec(memory_space=pl.ANY)],
        out_specs=pl.BlockSpec((TS, 4 * BR), lambda i: (i, 1)),
        out_shape=jax.ShapeDtypeStruct(dproj.shape, BF16),
        input_output_aliases={4: 0},
        compiler_params=_params("arbitrary"),
    )(*ds, dbg, dproj)


def _lru_bwd(proj, h, dcat, rows, wa, wx, dproj, BR):
    S = proj.shape[0]
    TS = _tile(S, 256)
    PAD = TS // 2
    nt = S // TS

    def body(cx_ref, hx_ref, cg_ref, h_ref, hh_ref, dy_ref, r_ref, wa_ref, wx_ref, _,
             dp_ref, sum_ref, dwa_ref, dwx_ref, ext, hext, aext, dext, sa, sb, carry):
        i = pl.program_id(0)
        ti = nt - 1 - i

        @pl.when(i == 0)
        def _():
            sa[TS:TS + PAD, :] = jnp.ones((PAD, BR), F32)
            sb[TS:TS + PAD, :] = jnp.zeros((PAD, BR), F32)
            carry[...] = jnp.zeros_like(carry)
            dext[TS:TS + 8, :] = jnp.zeros((8, BR), F32)
            sum_ref[...] = jnp.zeros_like(sum_ref)
            dwa_ref[...] = jnp.zeros_like(dwa_ref)
            dwx_ref[...] = jnp.zeros_like(dwx_ref)

        w0, w1, w2, w3 = r_ref[0:1, :], r_ref[1:2, :], r_ref[2:3, :], r_ref[3:4, :]
        sp = r_ref[7:8, :]
        cx = cx_ref[...]
        ext[0:8, :] = jnp.where(ti > 0, hx_ref[...], 0.0)
        ext[8:8 + TS, :] = cx
        x3, x2, x1 = ext[pl.ds(5, TS), :], ext[pl.ds(6, TS), :], ext[pl.ds(7, TS), :]
        xc = x3 * w0 + x2 * w1 + x1 * w2 + cx * w3 + r_ref[4:5, :]
        wa_, wx_ = wa_ref[...], wx_ref[...]
        r, ig, a, mult = _lru_gates(xc, wa_, wx_, r_ref[5:6, :], r_ref[6:7, :], sp)
        cg, dy, hv = cg_ref[...], dy_ref[...], h_ref[...]
        dp_ref[:, BR:2 * BR] = (dy * hv * _dsilu(cg)).astype(BF16)
        aext[0:TS, :] = a
        aext[TS:TS + 8, :] = jnp.broadcast_to(carry[0:1, :], (8, BR))
        sa[0:TS, :] = aext[pl.ds(1, TS), :]
        sb[0:TS, :] = dy * _silu(cg)
        d = 1
        while d < TS:
            A, B = sa[0:TS, :], sb[0:TS, :]
            As, Bs = sa[pl.ds(d, TS), :], sb[pl.ds(d, TS), :]
            sb[0:TS, :] = B + A * Bs
            sa[0:TS, :] = A * As
            d *= 2
        gh = sb[0:TS, :] + sa[0:TS, :] * carry[1:2, :]
        carry[0:1, :] = a[0:1, :]
        carry[1:2, :] = gh[0:1, :]
        hext[0:8, :] = jnp.where(ti > 0, hh_ref[...], 0.0)
        hext[8:8 + TS, :] = hv
        da = gh * hext[pl.ds(7, TS), :]
        dmult = gh * ig * xc
        dig = gh * mult * xc
        dxc = gh * mult * ig
        dla = da * a - dmult * a * a / mult
        dpr = dla * (-LRU_C) * sp * r * (1.0 - r)
        dpi = dig * ig * (1.0 - ig)
        xb, dprb, dpib = xc.astype(BF16), dpr.astype(BF16), dpi.astype(BF16)
        dwa_ref[...] += _dot_tn(xb, dprb)
        dwx_ref[...] += _dot_tn(xb, dpib)
        dxc = dxc + _dot_nt(dprb, wa_) + _dot_nt(dpib, wx_)
        sum_ref[0:1, :] += _colsum(dxc * x3)
        sum_ref[1:2, :] += _colsum(dxc * x2)
        sum_ref[2:3, :] += _colsum(dxc * x1)
        sum_ref[3:4, :] += _colsum(dxc * cx)
        sum_ref[4:5, :] += _colsum(dxc)
        sum_ref[5:6, :] += _colsum(dpr)
        sum_ref[6:7, :] += _colsum(dpi)
        sum_ref[7:8, :] += _colsum(dla * (-LRU_C) * r)
        dext[0:TS, :] = dxc
        dcx = w3 * dxc + w2 * dext[pl.ds(1, TS), :] + w1 * dext[pl.ds(2, TS), :] + w0 * dext[pl.ds(3, TS), :]
        dext[TS:TS + 8, :] = dxc[0:8, :]
        dp_ref[:, 0:BR] = dcx.astype(BF16)

    def rev(k):
        return pl.BlockSpec((TS, BR), lambda i: (nt - 1 - i, k))

    def rev_halo(k):
        return pl.BlockSpec((8, BR), lambda i: (jnp.maximum((nt - 1 - i) * (TS // 8) - 1, 0), k))

    full = pl.BlockSpec((BR, BR), lambda i: (0, 0))
    rspec = pl.BlockSpec((8, BR), lambda i: (0, 0))
    return _call(
        body, name="lru_bwd", grid=(nt,),
        in_specs=[rev(8), rev_halo(8), rev(9), rev(0), rev_halo(0), rev(2), rspec, full, full, pl.BlockSpec(memory_space=pl.ANY)],
        out_specs=[pl.BlockSpec((TS, 2 * BR), lambda i: (nt - 1 - i, 4)), rspec, full, full],
        input_output_aliases={9: 0},
        out_shape=[jax.ShapeDtypeStruct(dproj.shape, BF16), jax.ShapeDtypeStruct((8, BR), F32),
                   jax.ShapeDtypeStruct((BR, BR), F32), jax.ShapeDtypeStruct((BR, BR), F32)],
        scratch_shapes=[pltpu.VMEM((TS + 8, BR), F32)] * 4 + [pltpu.VMEM((TS + PAD, BR), F32)] * 2 + [pltpu.VMEM((8, BR), F32)],
        compiler_params=_params("arbitrary"),
    )(proj, proj, proj, h, h, dcat, rows, wa, wx, dproj)


def _s5_bwd(proj, dcat, xre_all, xim_all, qw_re, qw_im, bbre, bbim, cre, cim, rows, wg, dproj, BR):
    S = proj.shape[0]
    TS, W = qw_re.shape
    PAD = TS // 2
    nt = S // TS
    CB = 8 * S5_CH
    SB = 8 * S5_STATE
    nblk = BR // CB

    def body(u_ref, dg_ref, dy_ref, xre_ref, xim_ref, hre_ref, him_ref, qr_ref, qi_ref, bbre_ref, bbim_ref, cre_ref, cim_ref,
             r_ref, wg_ref, _, dp_ref, sum_ref, dwg_ref, da_ref, dbbre_ref, dbbim_ref, dcre_ref, dcim_ref,
             sre, sim, ere, eim, ypre_s, dyp_s, carry):
        i = pl.program_id(0)
        ti = nt - 1 - i

        @pl.when(i == 0)
        def _():
            sre[TS:TS + PAD, :] = jnp.zeros((PAD, SB), F32)
            sim[TS:TS + PAD, :] = jnp.zeros((PAD, SB), F32)
            carry[...] = jnp.zeros_like(carry)
            for ref in (sum_ref, dwg_ref, da_ref, dbbre_ref, dbbim_ref, dcre_ref, dcim_ref):
                ref[...] = jnp.zeros_like(ref)

        u, dg, dy = u_ref[...], dg_ref[...], dy_ref[...]
        for m in range(nblk):
            cs, ws = slice(m * CB, (m + 1) * CB), slice(m * SB, (m + 1) * SB)
            ypre_s[:, cs] = (_dot(xre_ref[:, ws].astype(BF16), cre_ref[ws, :]) - _dot(xim_ref[:, ws].astype(BF16), cim_ref[ws, :]))
        ypre = ypre_s[...] + r_ref[0:1, :] * u
        yg = _gelu(ypre)
        wg_ = wg_ref[...]
        s = _sigmoid(_dot(yg.astype(BF16), wg_) + r_ref[1:2, :])
        dgl = dy * _silu(dg)
        dp_ref[:, BR:2 * BR] = (dy * yg * s * _dsilu(dg)).astype(BF16)
        dps = dgl * yg * s * (1.0 - s)
        dpsb = dps.astype(BF16)
        sum_ref[1:2, :] += _colsum(dps)
        dwg_ref[...] += _dot_tn(yg.astype(BF16), dpsb)
        dyp = (dgl * s + _dot_nt(dpsb, wg_)) * _dgelu(ypre)
        sum_ref[0:1, :] += _colsum(dyp * u)
        dyp_s[...] = dyp
        for m in range(nblk):
            cs, ws = slice(m * CB, (m + 1) * CB), slice(m * SB, (m + 1) * SB)
            dypb = dyp_s[:, cs].astype(BF16)
            xre, xim = xre_ref[:, ws], xim_ref[:, ws]
            dcre_ref[ws, :] += _dot_tn(xre.astype(BF16), dypb)
            dcim_ref[ws, :] -= _dot_tn(xim.astype(BF16), dypb)
            sre[0:TS, :] = _dot_nt(dypb, cre_ref[ws, :])
            sim[0:TS, :] = -_dot_nt(dypb, cim_ref[ws, :])
            d = 1
            while d < TS:
                ar, ai = qr_ref[TS - d:TS - d + 1, ws], -qi_ref[TS - d:TS - d + 1, ws]
                xr, xi = sre[0:TS, :], sim[0:TS, :]
                sr, si = sre[pl.ds(d, TS), :], sim[pl.ds(d, TS), :]
                sre[0:TS, :] = xr + ar * sr - ai * si
                sim[0:TS, :] = xi + ar * si + ai * sr
                d *= 2
            cr, ci = carry[0:1, ws], carry[1:2, ws]
            qr, qi = qr_ref[:, ws], -qi_ref[:, ws]
            gre = sre[0:TS, :] + qr * cr - qi * ci
            gim = sim[0:TS, :] + qr * ci + qi * cr
            carry[0:1, ws] = gre[0:1, :]
            carry[1:2, ws] = gim[0:1, :]
            ere[0:8, :] = jnp.where(ti > 0, hre_ref[:, ws], 0.0)
            eim[0:8, :] = jnp.where(ti > 0, him_ref[:, ws], 0.0)
            ere[8:8 + TS, :] = xre
            eim[8:8 + TS, :] = xim
            xpr, xpi = ere[pl.ds(7, TS), :], eim[pl.ds(7, TS), :]
            da_ref[0:1, ws] += _colsum(gre * xpr + gim * xpi)
            da_ref[1:2, ws] += _colsum(gim * xpr - gre * xpi)
            greb, gimb = gre.astype(BF16), gim.astype(BF16)
            ub = u_ref[:, cs].astype(BF16)
            dbbre_ref[cs, :] += _dot_tn(ub, greb)
            dbbim_ref[cs, :] += _dot_tn(ub, gimb)
            du = _dot_nt(greb, bbre_ref[cs, :]) + _dot_nt(gimb, bbim_ref[cs, :]) + r_ref[0:1, cs] * dyp_s[:, cs]
            dp_ref[:, cs] = du.astype(BF16)

    def rev(width, k):
        return pl.BlockSpec((TS, width), lambda i: (nt - 1 - i, k))

    def rev_halo(width):
        return pl.BlockSpec((8, width), lambda i: (jnp.maximum((nt - 1 - i) * (TS // 8) - 1, 0), 0))

    def const(shape):
        return pl.BlockSpec(shape, lambda i: (0, 0))

    return _call(
        body, name="s5_bwd", grid=(nt,),
        in_specs=[rev(BR, 10), rev(BR, 11), rev(BR, 3), rev(W, 0), rev(W, 0), rev_halo(W), rev_halo(W),
                  const((TS, W)), const((TS, W)), const((BR, SB)), const((BR, SB)), const((W, CB)), const((W, CB)),
                  const((8, BR)), const((BR, BR)), pl.BlockSpec(memory_space=pl.ANY)],
        input_output_aliases={15: 0},
        out_specs=[pl.BlockSpec((TS, 2 * BR), lambda i: (nt - 1 - i, 5)), const((8, BR)), const((BR, BR)), const((8, W)),
                   const((BR, SB)), const((BR, SB)), const((W, CB)), const((W, CB))],
        out_shape=[jax.ShapeDtypeStruct(dproj.shape, BF16), jax.ShapeDtypeStruct((8, BR), F32), jax.ShapeDtypeStruct((BR, BR), F32),
                   jax.ShapeDtypeStruct((8, W), F32), jax.ShapeDtypeStruct((BR, SB), F32), jax.ShapeDtypeStruct((BR, SB), F32),
                   jax.ShapeDtypeStruct((W, CB), F32), jax.ShapeDtypeStruct((W, CB), F32)],
        scratch_shapes=[pltpu.VMEM((TS + PAD, SB), F32)] * 2 + [pltpu.VMEM((TS + 8, SB), F32)] * 2
        + [pltpu.VMEM((TS, BR), F32)] * 2 + [pltpu.VMEM((8, W), F32)],
        compiler_params=_params("arbitrary"),
    )(proj, proj, dcat, xre_all, xim_all, xre_all, xim_all, qw_re, qw_im, bbre, bbim, cre, cim, rows, wg, dproj)


_WEIGHTS = ['rel_bias', 'w_ada', 'b_ada', 'w_in', 'conv_a', 'conv_c', 'conv_c_b', 'lru_wa', 'lru_ba', 'lru_wx', 'lru_bx',
            'lru_lambda', 's5_lam_re', 's5_lam_im', 's5_log_dt', 's5_b_re', 's5_b_im', 's5_c_re', 's5_c_im', 's5_d',
            's5_w_glu', 's5_b_glu', 'w_out', 'ln_g', 'ln_b']
_LAYER_SMALL = ['conv_a', 'conv_c', 'conv_c_b', 'lru_wa', 'lru_ba', 'lru_wx', 'lru_bx', 'lru_lambda', 's5_lam_re', 's5_lam_im',
                's5_log_dt', 's5_b_re', 's5_b_im', 's5_c_re', 's5_c_im', 's5_d', 's5_b_glu']
_SMALL = ['rel_bias'] + _LAYER_SMALL + ['ln_g', 'ln_b']


def _t5_bucket(dist):
    max_exact = REL_BUCKETS // 2
    nf = jnp.maximum(dist, 1).astype(F32)
    large = max_exact + (jnp.log(nf / max_exact) / math.log(REL_MAX_DIST / max_exact)
                         * (REL_BUCKETS - max_exact)).astype(jnp.int32)
    large = jnp.minimum(large, REL_BUCKETS - 1)
    return jnp.where(dist < max_exact, dist, large)


def _bias_tables(rel_bias):
    i = jnp.arange(BLK)[:, None]
    j = jnp.arange(2 * BLK)[None, :]
    delta = i + BLK - j
    out = []
    for window, dil in DILATIONS:
        bucket = _t5_bucket(jnp.clip(delta, 0, window // dil) * dil)
        onehot = (bucket[:, :, None] == jnp.arange(REL_BUCKETS)[None, None, :]).astype(F32)
        out.append(jnp.einsum('ijb,bh->hij', onehot, rel_bias, precision=lax.Precision.HIGHEST))
    return jnp.stack(out)


def _prep_layer(q):
    eye8 = jnp.eye(8, dtype=F32)
    G = q['s5_lam_re'].shape[0]
    nblk = G // 8

    def block_diag(w):
        hd = w.shape[1]
        return (w[:, :, None, :] * eye8[:, None, :, None]).reshape(8 * hd, 8 * hd)

    def compact_b(bb):
        t = jnp.transpose(bb.reshape(nblk, 8, S5_STATE, S5_CH), (0, 1, 3, 2))
        return (t[:, :, :, None, :] * eye8[None, :, None, :, None]).reshape(G * S5_CH, 8 * S5_STATE)

    def compact_c(cc):
        t = jnp.transpose(cc.reshape(nblk, 8, S5_CH, S5_STATE), (0, 1, 3, 2))
        return (t[:, :, :, None, :] * eye8[None, :, None, :, None]).reshape(G * S5_STATE, 8 * S5_CH)

    lam_re, lam_im = q['s5_lam_re'], q['s5_lam_im']
    dt = jnp.exp(q['s5_log_dt'])[:, None]
    mag = jnp.exp(lam_re * dt)
    ab_re = mag * jnp.cos(lam_im * dt)
    ab_im = mag * jnp.sin(lam_im * dt)
    den = lam_re * lam_re + lam_im * lam_im
    f_re = ((ab_re - 1.0) * lam_re + ab_im * lam_im) / den
    f_im = (ab_im * lam_re - (ab_re - 1.0) * lam_im) / den
    b_re, b_im = q['s5_b_re'], q['s5_b_im']
    bb_re = f_re[..., None] * b_re - f_im[..., None] * b_im
    bb_im = f_re[..., None] * b_im + f_im[..., None] * b_re
    return dict(
        conv_a=q['conv_a'], conv_c=q['conv_c'], conv_c_b=q['conv_c_b'], lru_ba=q['lru_ba'], lru_bx=q['lru_bx'],
        sp=jax.nn.softplus(-q['lru_lambda']), wa=block_diag(q['lru_wa']), wx=block_diag(q['lru_wx']),
        ar=ab_re.reshape(-1), ai=ab_im.reshape(-1), bbre=compact_b(bb_re), bbim=compact_b(bb_im),
        cre=compact_c(q['s5_c_re']), cim=compact_c(q['s5_c_im']), s5_d=q['s5_d'], s5_b_glu=q['s5_b_glu'])


def _powers(ar, ai, T):
    pr, pi = ar[None, :], ai[None, :]
    while pr.shape[0] < T:
        lr, li = pr[-1:], pi[-1:]
        pr, pi = jnp.concatenate([pr, pr * lr - pi * li]), jnp.concatenate([pi, pr * li + pi * lr])
    return pr[:T], pi[:T]


def _rows8(vecs, width):
    rows = [v.reshape(1, width).astype(F32) for v in vecs]
    return jnp.concatenate(rows + [jnp.zeros((8 - len(rows), width), F32)], axis=0)


def kernel(x, c, rel_bias, w_ada, b_ada, w_in, conv_a, conv_c, conv_c_b, lru_wa, lru_ba, lru_wx, lru_bx, lru_lambda, s5_lam_re, s5_lam_im, s5_log_dt, s5_b_re, s5_b_im, s5_c_re, s5_c_im, s5_d, s5_w_glu, s5_b_glu, w_out, ln_g, ln_b, loss_target, m_rel_bias, m_w_ada, m_b_ada, m_w_in, m_conv_a, m_conv_c, m_conv_c_b, m_lru_wa, m_lru_ba, m_lru_wx, m_lru_bx, m_lru_lambda, m_s5_lam_re, m_s5_lam_im, m_s5_log_dt, m_s5_b_re, m_s5_b_im, m_s5_c_re, m_s5_c_im, m_s5_d, m_s5_w_glu, m_s5_b_glu, m_w_out, m_ln_g, m_ln_b, v_rel_bias, v_w_ada, v_b_ada, v_w_in, v_conv_a, v_conv_c, v_conv_c_b, v_lru_wa, v_lru_ba, v_lru_wx, v_lru_bx, v_lru_lambda, v_s5_lam_re, v_s5_lam_im, v_s5_log_dt, v_s5_b_re, v_s5_b_im, v_s5_c_re, v_s5_c_im, v_s5_d, v_s5_w_glu, v_s5_b_glu, v_w_out, v_ln_g, v_ln_b):
    a = dict(locals())
    me = 4 * lax.axis_index("x") + 2 * lax.axis_index("y") + lax.axis_index("c")
    x0, target = x[0], loss_target[0]
    S, D = x0.shape
    BR = D // 4
    NS = w_in.shape[2]
    SH = BR // NDEV
    assert S % (BLK * DILATIONS[-1][1]) == 0 and BR % (8 * S5_CH) == 0
    TS5 = _tile(S, 256)

    small = _exchange(_pack([c, conv_a, conv_c]), True, "gather_small").reshape(NDEV, -1)
    c_all = small[:, :D]
    na, nc = DEPTH * 3 * SH, DEPTH * 4 * SH
    conv_a_full = jnp.transpose(small[:, D:D + na].reshape(NDEV, DEPTH, 3, SH), (1, 2, 0, 3)).reshape(DEPTH, 3, BR)
    conv_c_full = jnp.transpose(small[:, D + na:D + na + nc].reshape(NDEV, DEPTH, 4, SH), (1, 2, 0, 3)).reshape(DEPTH, 4, BR)
    wg_in = _exchange(w_in.astype(BF16), True, "gather_w_in")
    wg_out = _exchange(w_out.astype(BF16), True, "gather_w_out")
    wg_glu = _exchange(s5_w_glu.astype(BF16), True, "gather_w_glu")

    b_cols = lax.dynamic_slice(b_ada, (0, me * NS), (DEPTH, NS)).reshape(DEPTH, 1, NS)
    ada_all = _exchange(_ada_cols(c_all, w_ada, b_cols), True, "gather_ada")
    ada_me = lax.dynamic_index_in_dim(ada_all, me, axis=2, keepdims=False)
    ada_me = jnp.transpose(ada_me, (1, 0, 2)).reshape(DEPTH, 3 * D)
    shift, scale, gate = ada_me[:, :D], ada_me[:, D:2 * D], ada_me[:, 2 * D:]

    bias_tabs, bias_pull = jax.vjp(_bias_tables, rel_bias)
    HD = BR // 8
    head_ones = jnp.kron(jnp.eye(8, dtype=F32), jnp.ones((HD, HD), F32))

    saved = []
    xl = x0
    for l in range(DEPTH):
        q = {n: a[n][l] for n in _LAYER_SMALL}
        q['conv_a'], q['conv_c'] = conv_a_full[l], conv_c_full[l]
        kp, pull = jax.vjp(_prep_layer, q)
        pw_re, pw_im = _powers(lax.stop_gradient(kp['ar']), lax.stop_gradient(kp['ai']), TS5)
        w_out_l = wg_out[:, l].reshape(D, D)
        w_glu_l = wg_glu[:, l].reshape(BR, BR)
        mod = _rows8([1.0 + scale[l], shift[l]], D)
        arows = _rows8(list(kp['conv_a']), BR)
        lrows = _rows8(list(kp['conv_c']) + [kp['conv_c_b'], kp['lru_ba'], kp['lru_bx'], kp['sp']], BR)
        srows = _rows8([kp['s5_d'], kp['s5_b_glu']], BR)
        orows = _rows8([1.0 + gate[l], ln_g[l], ln_b[l]], D)
        wa, wx = kp['wa'].astype(BF16), kp['wx'].astype(BF16)
        s5w = [kp[n].astype(BF16) for n in ('bbre', 'bbim', 'cre', 'cim')]

        proj, hbf = _proj(xl, mod, wg_in, l)
        ya = _conv_a_fwd(proj, arows, BR)
        os_, ls_ = [], []
        for g, (_, dil) in enumerate(DILATIONS):
            o, lse = _attn_fwd(proj, bias_tabs[g], dil, BR)
            os_.append(o)
            ls_.append(lse)
        yb = _attn_mix_fwd(proj, os_, ls_, BR)
        yc, hs = _lru_fwd(proj, lrows, wa, wx, BR)
        yd, xre, xim = _s5_fwd(proj, pw_re, pw_im, *s5w, srows, w_glu_l, BR)
        xn, y, cat = _out_ln([ya, yb, yc, yd], w_out_l, xl, orows)
        saved.append(dict(x=xl, proj=proj, hbf=hbf, os=os_, ls=ls_, hs=hs, xre=xre, xim=xim, y=y, cat=cat, pull=pull,
                          mod=mod, arows=arows, lrows=lrows, srows=srows, orows=orows, wa=wa, wx=wx, s5w=s5w,
                          qw=(pw_re[::-1], pw_im[::-1]), w_out=w_out_l, w_glu=w_glu_l))
        xl = xn

    dout, loss_local = _loss_grad(xl, target)
    loss = lax.psum(loss_local, ("x", "y", "c"))

    dbias = jnp.zeros_like(bias_tabs)
    lgrads, dada, dw_in_parts, dw_out_parts, dw_glu_parts = [None] * DEPTH, [None] * DEPTH, [None] * DEPTH, [None] * DEPTH, [None] * DEPTH
    for l in reversed(range(DEPTH)):
        sv = saved[l]
        proj = sv['proj']
        dxres, dyb, dcat, lnsum = _ln_bwd(dout, sv['x'], sv['y'], sv['orows'], sv['w_out'])
        dw_out_parts[l] = _matmul_tn_rows(sv['cat'], dyb, D // NDEV, "dw_out").reshape(NDEV, D // NDEV, D)
        dproj, asum = _conv_a_bwd(proj, dcat, sv['arows'], BR)
        dos, es, dbg = _attn_mix_bwd(proj, dcat, sv['os'], sv['ls'], head_ones, BR)
        dqkv, dbs = [], []
        for g, (_, dil) in enumerate(DILATIONS):
            d_, db_ = _attn_bwd(proj, dos[g], es[g], sv['ls'][g], bias_tabs[g], dil, BR)
            dqkv.append(d_)
            dbs.append(db_)
        dbias = dbias + jnp.stack(dbs)
        dproj = _attn_dsum(dqkv, dbg, dproj, BR)
        dproj, lsum, dwa, dwx = _lru_bwd(proj, sv['hs'], dcat, sv['lrows'], sv['wa'], sv['wx'], dproj, BR)
        dproj, ssum, dwg, da_, dbbre, dbbim, dcre, dcim = _s5_bwd(
            proj, dcat, sv['xre'], sv['xim'], *sv['qw'], *sv['s5w'], sv['srows'], sv['w_glu'], dproj, BR)
        dw_glu_parts[l] = dwg.reshape(NDEV, BR // NDEV, BR)
        dout, msum = _dh(dproj, wg_in, l, dxres, sv['x'], sv['mod'])
        dw_in_parts[l] = _matmul_tn_cols(sv['hbf'], dproj, NS, "dw_in")
        dada[l] = jnp.concatenate([msum[1], msum[0], lnsum[2]])
        dkp = dict(conv_a=asum[0:3], conv_c=lsum[0:4], conv_c_b=lsum[4], lru_ba=lsum[5], lru_bx=lsum[6], sp=lsum[7],
                   wa=dwa, wx=dwx, ar=da_[0], ai=da_[1], bbre=dbbre, bbim=dbbim, cre=dcre, cim=dcim,
                   s5_d=ssum[0], s5_b_glu=ssum[1])
        lgrads[l] = dict(sv['pull'](dkp)[0], ln_g=lnsum[0], ln_b=lnsum[1])
    grad_x = dout[None]

    out = {}

    def put(name, res):
        out[name] = res

    def stacked(fn):
        res = [fn(l) for l in range(DEPTH)]
        return tuple(jnp.stack([r[k] for r in res]) for k in range(4))

    put('w_in', stacked(lambda l: _adamw(_exchange(dw_in_parts[l], False, "scatter_w_in"), w_in[l], m_w_in[l], v_w_in[l], "adamw_w_in")))
    put('w_out', stacked(lambda l: _adamw(_exchange(dw_out_parts[l], False, "scatter_w_out"), w_out[l], m_w_out[l], v_w_out[l], "adamw_w_out")))
    put('s5_w_glu', stacked(lambda l: _adamw(_exchange(dw_glu_parts[l], False, "scatter_w_glu"), s5_w_glu[l], m_s5_w_glu[l], v_s5_w_glu[l], "adamw_w_glu")))

    dada_all = _exchange(jnp.stack(dada), True, "gather_dada")
    dada_cols = jnp.transpose(lax.dynamic_slice(dada_all, (0, 0, me * NS), (NDEV, DEPTH, NS)), (1, 0, 2))
    gw_ada, gb_ada = _ada_grads(c_all, dada_all, dada_cols)
    res = _adamw(gw_ada.reshape(DEPTH * D, NS), w_ada.reshape(DEPTH * D, NS), m_w_ada.reshape(DEPTH * D, NS),
                 v_w_ada.reshape(DEPTH * D, NS), "adamw_w_ada")
    put('w_ada', tuple(r.reshape(DEPTH, D, NS) for r in res))

    local = {'rel_bias': bias_pull(dbias)[0]}
    for n in _LAYER_SMALL + ['ln_g', 'ln_b']:
        local[n] = jnp.stack([lgrads[l][n] for l in range(DEPTH)])
    shapes = [local[n].shape for n in _SMALL]
    summed = _sum8(_exchange(_pack([local[n] for n in _SMALL]), True, "gather_small_grads"), "sum_small_grads")
    gs = dict(zip(_SMALL, _unpack(summed, shapes)))
    gs['conv_a'] = lax.dynamic_slice_in_dim(gs['conv_a'], me * SH, SH, axis=2)
    gs['conv_c'] = lax.dynamic_slice_in_dim(gs['conv_c'], me * SH, SH, axis=2)
    gs['b_ada'] = gb_ada
    names = _SMALL + ['b_ada']
    shapes = [a[n].shape for n in names]
    res = _adamw(_pack([gs[n] for n in names]), _pack([a[n] for n in names]), _pack([a['m_' + n] for n in names]),
                 _pack([a['v_' + n] for n in names]), "adamw_small")
    unpacked = [_unpack(r, shapes) for r in res]
    for k, n in enumerate(names):
        put(n, tuple(unpacked[j][k] for j in range(4)))

    return (loss, grad_x, *[out[n][0] for n in _WEIGHTS], *[out[n][1] for n in _WEIGHTS],
            *[out[n][2] for n in _WEIGHTS], *[out[n][3] for n in _WEIGHTS])
```

```python
import functools
import math

import jax
import jax.numpy as jnp
from jax import lax
from jax.experimental import pallas as pl
from jax.experimental.pallas import tpu as pltpu

F32 = jnp.float32
BF16 = jnp.bfloat16

NDEV = 8
DEPTH = 2
BLK = 128
DILATIONS = ((128, 1), (512, 4), (2048, 16))
REL_BUCKETS = 32
REL_MAX_DIST = 2048
LRU_C = 8.0
S5_CH = 16
S5_STATE = 64
ALPHA = (2 * DEPTH) ** 0.25
LN_EPS = 1e-5
ADAM_LR, ADAM_B1, ADAM_B2, ADAM_EPS, ADAM_WD, ADAM_STEP = 0.001, 0.9, 0.999, 1e-08, 0.01, 10
NEG = -1e30
VMEM_LIMIT = 56 * 1024 * 1024


def _call(body, **kw):
    return pl.pallas_call(body, **kw)


def _params(*sem):
    return pltpu.CompilerParams(dimension_semantics=sem, vmem_limit_bytes=VMEM_LIMIT)


def _sigmoid(x):
    return 1.0 / (1.0 + jnp.exp(-x))


def _silu(x):
    return x * _sigmoid(x)


def _dsilu(x):
    s = _sigmoid(x)
    return s * (1.0 + x * (1.0 - s))


_GC = math.sqrt(2.0 / math.pi)


def _gelu(x):
    return 0.5 * x * (1.0 + jnp.tanh(_GC * (x + 0.044715 * x * x * x)))


def _dgelu(x):
    t = jnp.tanh(_GC * (x + 0.044715 * x * x * x))
    return 0.5 * (1.0 + t) + 0.5 * x * (1.0 - t * t) * _GC * (1.0 + 3.0 * 0.044715 * x * x)


def _expm1(x):
    series = x * (1.0 + x * (1.0 / 2) * (1.0 + x * (1.0 / 3) * (1.0 + x * (1.0 / 4) * (1.0 + x * (1.0 / 5) * (1.0 + x * (1.0 / 6))))))
    return jnp.where(jnp.abs(x) < 0.25, series, jnp.exp(x) - 1.0)


def _dot(a, b):
    return jnp.dot(a, b, preferred_element_type=F32)


def _dot_nt(a, b):
    return lax.dot_general(a, b, (((1,), (1,)), ((), ())), preferred_element_type=F32)


def _dot_tn(a, b):
    return lax.dot_general(a, b, (((0,), (0,)), ((), ())), preferred_element_type=F32)


def _colsum(v):
    return jnp.sum(v, axis=0, keepdims=True)


def _exchange(x, gather, name):
    blk = x.shape if gather else x.shape[1:]

    def body(x_ref, o_ref, send_sems, recv_sems, local_sem):
        ix, iy, ic = lax.axis_index("x"), lax.axis_index("y"), lax.axis_index("c")
        me = 4 * ix + 2 * iy + ic

        def src(dev):
            return x_ref if gather else x_ref.at[dev]

        local = pltpu.make_async_copy(src(me), o_ref.at[me], local_sem)
        local.start()
        copies = []
        for k in range(1, NDEV):
            px = 1 - ix if k & 4 else ix
            py = 1 - iy if k & 2 else iy
            pc = 1 - ic if k & 1 else ic
            peer = 4 * px + 2 * py + pc
            cp = pltpu.make_async_remote_copy(
                src_ref=src(peer), dst_ref=o_ref.at[me], send_sem=send_sems.at[k - 1], recv_sem=recv_sems.at[k - 1],
                device_id=(px, py, pc), device_id_type=pl.DeviceIdType.MESH)
            cp.start()
            copies.append(cp)
        for cp in copies:
            cp.wait()
        local.wait()

    return _call(
        body, name=name,
        out_shape=jax.ShapeDtypeStruct((NDEV,) + tuple(blk), x.dtype),
        in_specs=[pl.BlockSpec(memory_space=pl.ANY)],
        out_specs=pl.BlockSpec(memory_space=pl.ANY),
        scratch_shapes=[pltpu.SemaphoreType.DMA((NDEV - 1,)), pltpu.SemaphoreType.DMA((NDEV - 1,)), pltpu.SemaphoreType.DMA],
    )(x)


def _pack(arrs):
    flat = jnp.concatenate([a.reshape(-1).astype(F32) for a in arrs])
    n = flat.shape[0]
    quantum = 1024 if n <= 65536 else 65536
    npad = -(-n // quantum) * quantum
    return jnp.pad(flat, (0, npad - n)).reshape(npad // 128, 128)


def _unpack(packed, shapes):
    flat = packed.reshape(-1)
    out, off = [], 0
    for s in shapes:
        n = math.prod(s)
        out.append(flat[off:off + n].reshape(s))
        off += n
    return out


def _sum8(parts, name):
    _, R, C = parts.shape
    TR = R
    for cand in (512, 256, 128, 64, 32, 16, 8):
        if R % cand == 0:
            TR = cand
            break

    def body(p_ref, o_ref):
        acc = p_ref[0]
        for d in range(1, NDEV):
            acc = acc + p_ref[d]
        o_ref[...] = acc

    return _call(
        body, name=name, grid=(R // TR,),
        in_specs=[pl.BlockSpec((NDEV, TR, C), lambda i: (0, i, 0))],
        out_specs=pl.BlockSpec((TR, C), lambda i: (i, 0)),
        out_shape=jax.ShapeDtypeStruct((R, C), F32),
        compiler_params=_params("arbitrary"),
    )(parts)


def _adamw(g_parts, w, m, v, name):
    R, C = w.shape
    nparts = g_parts.shape[0] if g_parts.ndim == 3 else 0
    TR = R
    for cand in (256, 128, 64, 32, 16, 8):
        if R % cand == 0:
            TR = cand
            break

    def body(g_ref, w_ref, m_ref, v_ref, go_ref, d_ref, mo_ref, vo_ref):
        if nparts:
            g = g_ref[0].astype(F32)
            for d in range(1, nparts):
                g = g + g_ref[d].astype(F32)
        else:
            g = g_ref[...]
        m2 = ADAM_B1 * m_ref[...] + (1.0 - ADAM_B1) * g
        v2 = ADAM_B2 * v_ref[...] + (1.0 - ADAM_B2) * (g * g)
        m_hat = m2 / (1.0 - ADAM_B1 ** ADAM_STEP)
        v_hat = v2 / (1.0 - ADAM_B2 ** ADAM_STEP)
        go_ref[...] = g
        d_ref[...] = -ADAM_LR * (m_hat / (jnp.sqrt(v_hat) + ADAM_EPS) + ADAM_WD * w_ref[...])
        mo_ref[...] = m2
        vo_ref[...] = v2

    spec = pl.BlockSpec((TR, C), lambda i: (i, 0))
    gspec = pl.BlockSpec((nparts, TR, C), lambda i: (0, i, 0)) if nparts else spec
    return _call(
        body, name=name, grid=(R // TR,),
        in_specs=[gspec, spec, spec, spec], out_specs=[spec] * 4,
        out_shape=[jax.ShapeDtypeStruct((R, C), F32)] * 4,
        compiler_params=_params("arbitrary"),
    )(g_parts, w, m, v)


def _ada_cols(c_all, w_ada, b_cols):
    L, D, NS = w_ada.shape

    def body(c_ref, w_ref, b_ref, o_ref):
        cond = _silu(c_ref[...]).astype(BF16)
        o_ref[0] = _dot(cond, w_ref[0].astype(BF16)) + b_ref[0]

    return _call(
        body, name="ada_cols", grid=(L,),
        in_specs=[pl.BlockSpec((NDEV, D), lambda l: (0, 0)), pl.BlockSpec((1, D, NS), lambda l: (l, 0, 0)),
                  pl.BlockSpec((1, 1, NS), lambda l: (l, 0, 0))],
        out_specs=pl.BlockSpec((1, NDEV, NS), lambda l: (l, 0, 0)),
        out_shape=jax.ShapeDtypeStruct((L, NDEV, NS), F32),
        compiler_params=_params("arbitrary"),
    )(c_all, w_ada, b_cols)


def _ada_grads(c_all, dada_all, dada_cols):
    _, D = c_all.shape
    L, _, NS = dada_cols.shape
    D3 = dada_all.shape[2]

    def body(c_ref, da_ref, dc_ref, gw_ref, gb_ref):
        cond = _silu(c_ref[...]).astype(BF16)
        gw_ref[0] = _dot_tn(cond, dc_ref[0].astype(BF16))
        acc = da_ref[0]
        for d in range(1, NDEV):
            acc = acc + da_ref[d]
        gb_ref[...] = acc

    return _call(
        body, name="ada_grads", grid=(L,),
        in_specs=[pl.BlockSpec((NDEV, D), lambda l: (0, 0)), pl.BlockSpec((NDEV, L, D3), lambda l: (0, 0, 0)),
                  pl.BlockSpec((1, NDEV, NS), lambda l: (l, 0, 0))],
        out_specs=[pl.BlockSpec((1, D, NS), lambda l: (l, 0, 0)), pl.BlockSpec((L, D3), lambda l: (0, 0))],
        out_shape=[jax.ShapeDtypeStruct((L, D, NS), F32), jax.ShapeDtypeStruct((L, D3), F32)],
        compiler_params=_params("arbitrary"),
    )(c_all, dada_all, dada_cols)


def _tile(S, want):
    return want if S % want == 0 else S


def _proj(x, mod, wg, l):
    S, D = x.shape
    NS = wg.shape[3]
    TS = _tile(S, 512)

    def body(x_ref, mod_ref, w_ref, o_ref, h_ref):
        @pl.when(pl.program_id(1) == 0)
        def _():
            h_ref[...] = (x_ref[...] * mod_ref[0:1, :] + mod_ref[1:2, :]).astype(BF16)
        o_ref[...] = _dot(h_ref[...], w_ref[0, 0])

    return _call(
        body, name="proj", grid=(S // TS, NDEV),
        in_specs=[pl.BlockSpec((TS, D), lambda i, j: (i, 0)), pl.BlockSpec((8, D), lambda i, j: (0, 0)),
                  pl.BlockSpec((1, 1, D, NS), lambda i, j: (j, l, 0, 0))],
        out_specs=[pl.BlockSpec((TS, NS), lambda i, j: (i, j)), pl.BlockSpec((TS, D), lambda i, j: (i, 0))],
        out_shape=[jax.ShapeDtypeStruct((S, NDEV * NS), F32), jax.ShapeDtypeStruct((S, D), BF16)],
        compiler_params=_params("arbitrary", "arbitrary"),
    )(x, mod, wg)


def _chunk(TS, BR, k):
    return pl.BlockSpec((TS, BR), lambda i: (i, k))


def _halo_prev(TS, BR, k):
    return pl.BlockSpec((8, BR), lambda i: (jnp.maximum(i * (TS // 8) - 1, 0), k))


def _conv_a_fwd(proj, w8, BR):
    S = proj.shape[0]
    TS = _tile(S, 512)

    def body(ab_ref, ac_ref, ax_ref, ag_ref, hc_ref, hx_ref, w_ref, o_ref, ext):
        i = pl.program_id(0)
        u = ac_ref[...] * ax_ref[...]
        ext[0:8, :] = jnp.where(i > 0, hc_ref[...] * hx_ref[...], 0.0)
        ext[8:8 + TS, :] = u
        conv = ext[pl.ds(6, TS), :] * w_ref[0:1, :] + ext[pl.ds(7, TS), :] * w_ref[1:2, :] + u * w_ref[2:3, :]
        o_ref[...] = ab_ref[...] * conv * _silu(ag_ref[...])

    return _call(
        body, name="conv_a_fwd", grid=(S // TS,),
        in_specs=[_chunk(TS, BR, 0), _chunk(TS, BR, 1), _chunk(TS, BR, 2), _chunk(TS, BR, 3),
                  _halo_prev(TS, BR, 1), _halo_prev(TS, BR, 2), pl.BlockSpec((8, BR), lambda i: (0, 0))],
        out_specs=pl.BlockSpec((TS, BR), lambda i: (i, 0)),
        out_shape=jax.ShapeDtypeStruct((S, BR), F32),
        scratch_shapes=[pltpu.VMEM((TS + 8, BR), F32)],
        compiler_params=_params("arbitrary"),
    )(proj, proj, proj, proj, proj, proj, w8)


ATT_UNIT = 2048
ATT_LANES = 128


def _attn_mask():
    i = lax.broadcasted_iota(jnp.int32, (BLK, 2 * BLK), 0)
    j = lax.broadcasted_iota(jnp.int32, (BLK, 2 * BLK), 1)
    return jnp.logical_and(j >= i, j <= i + BLK), j >= BLK


def _attn_rows(n, r, dil):
    return pl.ds(n * BLK * dil + r, BLK, stride=dil) if dil > 1 else pl.ds(n * BLK, BLK)


def _attn_geometry(S, BR, dil):
    HD = BR // 8
    U = ATT_UNIT
    assert S % U == 0 and BR % ATT_LANES == 0 and ATT_LANES % HD == 0 and U % (BLK * dil) == 0
    return HD, U, S // U, U // (BLK * dil), ATT_LANES // HD, BR // ATT_LANES


def _attn_fwd(proj, bias, dil, BR):
    S = proj.shape[0]
    HD, U, NU, nbu, hpb, HBK = _attn_geometry(S, BR, dil)
    scale = HD ** -0.5
    combos = [(n, r) for n in range(nbu) for r in range(dil)]

    def body(q_ref, kc_ref, kp_ref, vc_ref, vp_ref, b_ref, o_ref, l_ref, sbuf, pbuf):
        m = pl.program_id(1)
        band, is_cur = _attn_mask()
        first = jnp.logical_and(band, jnp.logical_or(is_cur, m > 0))

        def keys(ref_c, ref_p, n, r):
            prev = ref_c[_attn_rows(n - 1, r, dil), :] if n > 0 else ref_p[_attn_rows(nbu - 1, r, dil), :]
            return jnp.concatenate([prev, ref_c[_attn_rows(n, r, dil), :]], axis=0).astype(BF16)

        for c, (n, r) in enumerate(combos):
            q = (q_ref[_attn_rows(n, r, dil), :] * scale).astype(BF16)
            kk = keys(kc_ref, kp_ref, n, r)
            for h in range(hpb):
                sl = slice(h * HD, (h + 1) * HD)
                sbuf[c * hpb + h] = _dot_nt(q[:, sl], kk[:, sl])
        for c, (n, r) in enumerate(combos):
            lses = []
            for h in range(hpb):
                s = jnp.where(band if n > 0 else first, sbuf[c * hpb + h] + b_ref[h], NEG)
                mx = jnp.max(s, axis=-1, keepdims=True)
                p = jnp.exp(s - mx)
                l = jnp.sum(p, axis=-1, keepdims=True)
                pbuf[c * hpb + h] = (p * (1.0 / l)).astype(BF16)
                lses.append(jnp.broadcast_to(mx + jnp.log(l), (BLK, HD)))
            l_ref[_attn_rows(n, r, dil), :] = jnp.concatenate(lses, axis=1)
        for c, (n, r) in enumerate(combos):
            vv = keys(vc_ref, vp_ref, n, r)
            o_ref[_attn_rows(n, r, dil), :] = jnp.concatenate(
                [_dot(pbuf[c * hpb + h], vv[:, h * HD:(h + 1) * HD]) for h in range(hpb)], axis=1)

    def cur(c):
        return pl.BlockSpec((U, ATT_LANES), lambda hb, m: (m, c * HBK + hb))

    def prev(c):
        return pl.BlockSpec((U, ATT_LANES), lambda hb, m: (jnp.maximum(m - 1, 0), c * HBK + hb))

    ospec = pl.BlockSpec((U, ATT_LANES), lambda hb, m: (m, hb))
    nhb = len(combos) * hpb
    return _call(
        body, name="attn_fwd_d%d" % dil, grid=(HBK, NU),
        in_specs=[cur(4), cur(5), prev(5), cur(6), prev(6), pl.BlockSpec((hpb, BLK, 2 * BLK), lambda hb, m: (hb, 0, 0))],
        out_specs=[ospec, ospec],
        out_shape=[jax.ShapeDtypeStruct((S, BR), F32)] * 2,
        scratch_shapes=[pltpu.VMEM((nhb, BLK, 2 * BLK), F32), pltpu.VMEM((nhb, BLK, 2 * BLK), BF16)],
        compiler_params=_params("arbitrary", "arbitrary"),
    )(proj, proj, proj, proj, proj, bias)


def _mix_weights(l1, l2, l3):
    m = jnp.maximum(jnp.maximum(l1, l2), l3)
    e1, e2, e3 = jnp.exp(l1 - m), jnp.exp(l2 - m), jnp.exp(l3 - m)
    inv = 1.0 / (e1 + e2 + e3)
    return e1 * inv, e2 * inv, e3 * inv


def _attn_mix_fwd(proj, os_, ls_, BR):
    S = proj.shape[0]
    TS = _tile(S, 512)

    def body(o1, o2, o3, l1, l2, l3, g_ref, y_ref):
        w1, w2, w3 = _mix_weights(l1[...], l2[...], l3[...])
        y_ref[...] = (w1 * o1[...] + w2 * o2[...] + w3 * o3[...]) * _silu(g_ref[...])

    spec = pl.BlockSpec((TS, BR), lambda i: (i, 0))
    return _call(
        body, name="attn_mix_fwd", grid=(S // TS,),
        in_specs=[spec] * 6 + [_chunk(TS, BR, 7)], out_specs=spec,
        out_shape=jax.ShapeDtypeStruct((S, BR), F32),
        compiler_params=_params("arbitrary"),
    )(*os_, *ls_, proj)


def _lru_gates(xc, wa, wx, ba, bx, sp):
    xb = xc.astype(BF16)
    r = _sigmoid(_dot(xb, wa) + ba)
    ig = _sigmoid(_dot(xb, wx) + bx)
    la = -LRU_C * r * sp
    a = jnp.exp(la)
    mult = jnp.sqrt(-_expm1(2.0 * la))
    return r, ig, a, mult


def _lru_fwd(proj, rows, wa, wx, BR):
    S = proj.shape[0]
    TS = _tile(S, 256)
    PAD = TS // 2

    def body(cx_ref, hx_ref, cg_ref, r_ref, wa_ref, wx_ref, y_ref, h_ref, ext, sa, sb, carry):
        i = pl.program_id(0)

        @pl.when(i == 0)
        def _():
            sa[0:PAD, :] = jnp.ones((PAD, BR), F32)
            sb[0:PAD, :] = jnp.zeros((PAD, BR), F32)
            carry[...] = jnp.zeros_like(carry)

        cx = cx_ref[...]
        ext[0:8, :] = jnp.where(i > 0, hx_ref[...], 0.0)
        ext[8:8 + TS, :] = cx
        xc = (ext[pl.ds(5, TS), :] * r_ref[0:1, :] + ext[pl.ds(6, TS), :] * r_ref[1:2, :]
              + ext[pl.ds(7, TS), :] * r_ref[2:3, :] + cx * r_ref[3:4, :] + r_ref[4:5, :])
        _, ig, a, mult = _lru_gates(xc, wa_ref[...], wx_ref[...], r_ref[5:6, :], r_ref[6:7, :], r_ref[7:8, :])
        sa[PAD:PAD + TS, :] = a
        sb[PAD:PAD + TS, :] = mult * ig * xc
        d = 1
        while d < TS:
            A, B = sa[PAD:PAD + TS, :], sb[PAD:PAD + TS, :]
            As, Bs = sa[pl.ds(PAD - d, TS), :], sb[pl.ds(PAD - d, TS), :]
            sb[PAD:PAD + TS, :] = A * Bs + B
            sa[PAD:PAD + TS, :] = A * As
            d *= 2
        h = sb[PAD:PAD + TS, :] + sa[PAD:PAD + TS, :] * carry[0:1, :]
        carry[0:1, :] = h[TS - 1:TS, :]
        h_ref[...] = h
        y_ref[...] = h * _silu(cg_ref[...])

    full = pl.BlockSpec((BR, BR), lambda i: (0, 0))
    spec = pl.BlockSpec((TS, BR), lambda i: (i, 0))
    return _call(
        body, name="lru_fwd", grid=(S // TS,),
        in_specs=[_chunk(TS, BR, 8), _halo_prev(TS, BR, 8), _chunk(TS, BR, 9), pl.BlockSpec((8, BR), lambda i: (0, 0)), full, full],
        out_specs=[spec, spec],
        out_shape=[jax.ShapeDtypeStruct((S, BR), F32)] * 2,
        scratch_shapes=[pltpu.VMEM((TS + 8, BR), F32), pltpu.VMEM((PAD + TS, BR), F32), pltpu.VMEM((PAD + TS, BR), F32),
                        pltpu.VMEM((8, BR), F32)],
        compiler_params=_params("arbitrary"),
    )(proj, proj, proj, rows, wa, wx)


def _s5_fwd(proj, pw_re, pw_im, bbre, bbim, cre, cim, rows, wg, BR):
    S = proj.shape[0]
    TS, W = pw_re.shape
    PAD = TS // 2
    CB = 8 * S5_CH
    SB = 8 * S5_STATE
    nblk = BR // CB

    def body(u_ref, dg_ref, pr_ref, pi_ref, bbre_ref, bbim_ref, cre_ref, cim_ref, r_ref, wg_ref,
             y_ref, xre_ref, xim_ref, sre, sim, ypre_s, carry):
        i = pl.program_id(0)

        @pl.when(i == 0)
        def _():
            sre[0:PAD, :] = jnp.zeros((PAD, SB), F32)
            sim[0:PAD, :] = jnp.zeros((PAD, SB), F32)
            carry[...] = jnp.zeros_like(carry)

        for m in range(nblk):
            cs, ws = slice(m * CB, (m + 1) * CB), slice(m * SB, (m + 1) * SB)
            ub = u_ref[:, cs].astype(BF16)
            sre[PAD:PAD + TS, :] = _dot(ub, bbre_ref[cs, :])
            sim[PAD:PAD + TS, :] = _dot(ub, bbim_ref[cs, :])
            d = 1
            while d < TS:
                ar, ai = pr_ref[d - 1:d, ws], pi_ref[d - 1:d, ws]
                xr, xi = sre[PAD:PAD + TS, :], sim[PAD:PAD + TS, :]
                sr, si = sre[pl.ds(PAD - d, TS), :], sim[pl.ds(PAD - d, TS), :]
                sre[PAD:PAD + TS, :] = xr + ar * sr - ai * si
                sim[PAD:PAD + TS, :] = xi + ar * si + ai * sr
                d *= 2
            cr, ci = carry[0:1, ws], carry[1:2, ws]
            pr, pi = pr_ref[:, ws], pi_ref[:, ws]
            xre = sre[PAD:PAD + TS, :] + pr * cr - pi * ci
            xim = sim[PAD:PAD + TS, :] + pr * ci + pi * cr
            carry[0:1, ws] = xre[TS - 1:TS, :]
            carry[1:2, ws] = xim[TS - 1:TS, :]
            xre_ref[:, ws] = xre
            xim_ref[:, ws] = xim
            ypre_s[:, cs] = _dot(xre.astype(BF16), cre_ref[ws, :]) - _dot(xim.astype(BF16), cim_ref[ws, :])
        dg = dg_ref[...]
        yg = _gelu(ypre_s[...] + r_ref[0:1, :] * u_ref[...])
        s = _sigmoid(_dot(yg.astype(BF16), wg_ref[...]) + r_ref[1:2, :])
        y_ref[...] = yg * s * _silu(dg)

    def const(shape):
        return pl.BlockSpec(shape, lambda i: (0, 0))

    return _call(
        body, name="s5_fwd", grid=(S // TS,),
        in_specs=[_chunk(TS, BR, 10), _chunk(TS, BR, 11), const((TS, W)), const((TS, W)), const((BR, SB)), const((BR, SB)),
                  const((W, CB)), const((W, CB)), const((8, BR)), const((BR, BR))],
        out_specs=[pl.BlockSpec((TS, BR), lambda i: (i, 0)), pl.BlockSpec((TS, W), lambda i: (i, 0)), pl.BlockSpec((TS, W), lambda i: (i, 0))],
        out_shape=[jax.ShapeDtypeStruct((S, BR), F32), jax.ShapeDtypeStruct((S, W), F32), jax.ShapeDtypeStruct((S, W), F32)],
        scratch_shapes=[pltpu.VMEM((PAD + TS, SB), F32), pltpu.VMEM((PAD + TS, SB), F32), pltpu.VMEM((TS, BR), F32), pltpu.VMEM((8, W), F32)],
        compiler_params=_params("arbitrary"),
    )(proj, proj, pw_re, pw_im, bbre, bbim, cre, cim, rows, wg)


def _out_ln(ys, w_out, x, rows):
    S, D = x.shape
    BR = D // 4
    TS = _tile(S, 256)

    def body(ya, yb, yc, yd, w_ref, x_ref, r_ref, o_ref, y_ref, cat_ref):
        cat = jnp.concatenate([ya[...], yb[...], yc[...], yd[...]], axis=1).astype(BF16)
        y = _dot(cat, w_ref[...])
        z = ALPHA * x_ref[...] + r_ref[0:1, :] * y
        mu = jnp.mean(z, axis=-1, keepdims=True)
        zc = z - mu
        var = jnp.mean(zc * zc, axis=-1, keepdims=True)
        o_ref[...] = zc * lax.rsqrt(var + LN_EPS) * r_ref[1:2, :] + r_ref[2:3, :]
        y_ref[...] = y
        cat_ref[...] = cat

    yspec = pl.BlockSpec((TS, BR), lambda i: (i, 0))
    spec = pl.BlockSpec((TS, D), lambda i: (i, 0))
    return _call(
        body, name="out_ln", grid=(S // TS,),
        in_specs=[yspec] * 4 + [pl.BlockSpec((D, D), lambda i: (0, 0)), spec, pl.BlockSpec((8, D), lambda i: (0, 0))],
        out_specs=[spec, spec, spec],
        out_shape=[jax.ShapeDtypeStruct((S, D), F32), jax.ShapeDtypeStruct((S, D), F32), jax.ShapeDtypeStruct((S, D), BF16)],
        compiler_params=_params("arbitrary"),
    )(*ys, w_out, x, rows)


def _loss_grad(x, target):
    S, D = x.shape
    TS = _tile(S, 512)

    def body(x_ref, t_ref, g_ref, l_ref):
        @pl.when(pl.program_id(0) == 0)
        def _():
            l_ref[...] = jnp.zeros_like(l_ref)
        diff = x_ref[...] - t_ref[...]
        g_ref[...] = diff * (1.0 / D)
        l_ref[...] += jnp.sum(_colsum(diff * diff), axis=1, keepdims=True) * (0.5 / D)

    spec = pl.BlockSpec((TS, D), lambda i: (i, 0))
    g, l = _call(
        body, name="loss_grad", grid=(S // TS,),
        in_specs=[spec, spec], out_specs=[spec, pl.BlockSpec((1, 1), lambda i: (0, 0))],
        out_shape=[jax.ShapeDtypeStruct((S, D), F32), jax.ShapeDtypeStruct((1, 1), F32)],
        compiler_params=_params("arbitrary"),
    )(x, target)
    return g, l[0, 0]


def _ln_bwd(dout, x, y, rows, w_out):
    S, D = x.shape
    TS = _tile(S, 256)

    def body(do_ref, x_ref, y_ref, r_ref, w_ref, dxr_ref, dyb_ref, dcat_ref, acc_ref):
        @pl.when(pl.program_id(0) == 0)
        def _():
            acc_ref[...] = jnp.zeros_like(acc_ref)
        g1, lg = r_ref[0:1, :], r_ref[1:2, :]
        yv = y_ref[...]
        z = ALPHA * x_ref[...] + g1 * yv
        mu = jnp.mean(z, axis=-1, keepdims=True)
        zc = z - mu
        var = jnp.mean(zc * zc, axis=-1, keepdims=True)
        rstd = lax.rsqrt(var + LN_EPS)
        xhat = zc * rstd
        do = do_ref[...]
        dxh = do * lg
        dz = rstd * (dxh - jnp.mean(dxh, axis=-1, keepdims=True) - xhat * jnp.mean(dxh * xhat, axis=-1, keepdims=True))
        dxr_ref[...] = ALPHA * dz
        dyb = (g1 * dz).astype(BF16)
        dyb_ref[...] = dyb
        dcat_ref[...] = _dot_nt(dyb, w_ref[...])
        acc_ref[0:1, :] += _colsum(do * xhat)
        acc_ref[1:2, :] += _colsum(do)
        acc_ref[2:3, :] += _colsum(dz * yv)

    spec = pl.BlockSpec((TS, D), lambda i: (i, 0))
    rspec = pl.BlockSpec((8, D), lambda i: (0, 0))
    return _call(
        body, name="ln_bwd", grid=(S // TS,),
        in_specs=[spec, spec, spec, rspec, pl.BlockSpec((D, D), lambda i: (0, 0))],
        out_specs=[spec, spec, spec, rspec],
        out_shape=[jax.ShapeDtypeStruct((S, D), F32), jax.ShapeDtypeStruct((S, D), BF16), jax.ShapeDtypeStruct((S, D), F32),
                   jax.ShapeDtypeStruct((8, D), F32)],
        compiler_params=_params("arbitrary"),
    )(dout, x, y, rows, w_out)


def _matmul_tn_rows(a, b, TM, name):
    S, M = a.shape
    N = b.shape[1]
    TK = _tile(S, 512)
    nk = S // TK

    def body(a_ref, b_ref, o_ref, acc):
        k = pl.program_id(1)

        @pl.when(k == 0)
        def _():
            acc[...] = jnp.zeros_like(acc)
        acc[...] += _dot_tn(a_ref[...], b_ref[...])

        @pl.when(k == nk - 1)
        def _():
            o_ref[...] = acc[...].astype(BF16)

    return _call(
        body, name=name, grid=(M // TM, nk),
        in_specs=[pl.BlockSpec((TK, TM), lambda i, k: (k, i)), pl.BlockSpec((TK, N), lambda i, k: (k, 0))],
        out_specs=pl.BlockSpec((TM, N), lambda i, k: (i, 0)),
        out_shape=jax.ShapeDtypeStruct((M, N), BF16),
        scratch_shapes=[pltpu.VMEM((TM, N), F32)],
        compiler_params=_params("arbitrary", "arbitrary"),
    )(a, b)


def _matmul_tn_cols(a, b, TN, name):
    S, M = a.shape
    N = b.shape[1]
    TK = _tile(S, 512)
    nk = S // TK

    def body(a_ref, b_ref, o_ref, acc):
        k = pl.program_id(1)

        @pl.when(k == 0)
        def _():
            acc[...] = jnp.zeros_like(acc)
        acc[...] += _dot_tn(a_ref[...], b_ref[...])

        @pl.when(k == nk - 1)
        def _():
            o_ref[0] = acc[...].astype(BF16)

    return _call(
        body, name=name, grid=(N // TN, nk),
        in_specs=[pl.BlockSpec((TK, M), lambda j, k: (k, 0)), pl.BlockSpec((TK, TN), lambda j, k: (k, j))],
        out_specs=pl.BlockSpec((1, M, TN), lambda j, k: (j, 0, 0)),
        out_shape=jax.ShapeDtypeStruct((N // TN, M, TN), BF16),
        scratch_shapes=[pltpu.VMEM((M, TN), F32)],
        compiler_params=_params("arbitrary", "arbitrary"),
    )(a, b)


def _dh(dproj, wg, l, dxres, x, mod):
    S, D = x.shape
    NS = wg.shape[3]
    TS = _tile(S, 512)

    def body(dp_ref, w_ref, dxr_ref, x_ref, mod_ref, dx_ref, sum_ref, acc):
        i, j = pl.program_id(0), pl.program_id(1)

        @pl.when(jnp.logical_and(i == 0, j == 0))
        def _():
            sum_ref[...] = jnp.zeros_like(sum_ref)

        @pl.when(j == 0)
        def _():
            acc[...] = jnp.zeros_like(acc)

        acc[...] += _dot_nt(dp_ref[...], w_ref[0, 0])

        @pl.when(j == NDEV - 1)
        def _():
            dh = acc[...]
            dx_ref[...] = dxr_ref[...] + dh * mod_ref[0:1, :]
            sum_ref[0:1, :] += _colsum(dh * x_ref[...])
            sum_ref[1:2, :] += _colsum(dh)

    spec = pl.BlockSpec((TS, D), lambda i, j: (i, 0))
    rspec = pl.BlockSpec((8, D), lambda i, j: (0, 0))
    return _call(
        body, name="dh", grid=(S // TS, NDEV),
        in_specs=[pl.BlockSpec((TS, NS), lambda i, j: (i, j)), pl.BlockSpec((1, 1, D, NS), lambda i, j: (j, l, 0, 0)), spec, spec, rspec],
        out_specs=[spec, rspec],
        out_shape=[jax.ShapeDtypeStruct((S, D), F32), jax.ShapeDtypeStruct((8, D), F32)],
        scratch_shapes=[pltpu.VMEM((TS, D), F32)],
        compiler_params=_params("arbitrary", "arbitrary"),
    )(dproj, wg, dxres, x, mod)


def _halo_next(S, TS, BR, k):
    last = S // 8 - 1
    return pl.BlockSpec((8, BR), lambda i: (jnp.minimum((i + 1) * (TS // 8), last), k))


def _conv_a_bwd(proj, dcat, w8, BR):
    S = proj.shape[0]
    TS = _tile(S, 512)
    nt = S // TS

    def body(ab_ref, ac_ref, ax_ref, ag_ref, dy_ref, hc_ref, hx_ref, nab_ref, nag_ref, ndy_ref, w_ref, dp_ref, dw_ref, ext, dext):
        i = pl.program_id(0)

        @pl.when(i == 0)
        def _():
            dw_ref[...] = jnp.zeros_like(dw_ref)

        w0, w1, w2 = w_ref[0:1, :], w_ref[1:2, :], w_ref[2:3, :]
        ab, ac, ax, ag, dy = ab_ref[...], ac_ref[...], ax_ref[...], ag_ref[...], dy_ref[...]
        u = ac * ax
        ext[0:8, :] = jnp.where(i > 0, hc_ref[...] * hx_ref[...], 0.0)
        ext[8:8 + TS, :] = u
        u1, u2 = ext[pl.ds(7, TS), :], ext[pl.ds(6, TS), :]
        conv = u2 * w0 + u1 * w1 + u * w2
        sg = _silu(ag)
        dconv = dy * ab * sg
        dext[0:TS, :] = dconv
        dext[TS:TS + 8, :] = jnp.where(i < nt - 1, ndy_ref[...] * nab_ref[...] * _silu(nag_ref[...]), 0.0)
        du = w2 * dconv + w1 * dext[pl.ds(1, TS), :] + w0 * dext[pl.ds(2, TS), :]
        dp_ref[:, 0:BR] = (dy * conv * sg).astype(BF16)
        dp_ref[:, BR:2 * BR] = (du * ax).astype(BF16)
        dp_ref[:, 2 * BR:3 * BR] = (du * ac).astype(BF16)
        dp_ref[:, 3 * BR:4 * BR] = (dy * ab * conv * _dsilu(ag)).astype(BF16)
        dw_ref[0:1, :] += _colsum(dconv * u2)
        dw_ref[1:2, :] += _colsum(dconv * u1)
        dw_ref[2:3, :] += _colsum(dconv * u)

    rspec = pl.BlockSpec((8, BR), lambda i: (0, 0))
    return _call(
        body, name="conv_a_bwd", grid=(nt,),
        in_specs=[_chunk(TS, BR, 0), _chunk(TS, BR, 1), _chunk(TS, BR, 2), _chunk(TS, BR, 3), _chunk(TS, BR, 0),
                  _halo_prev(TS, BR, 1), _halo_prev(TS, BR, 2), _halo_next(S, TS, BR, 0), _halo_next(S, TS, BR, 3),
                  _halo_next(S, TS, BR, 0), rspec],
        out_specs=[pl.BlockSpec((TS, 4 * BR), lambda i: (i, 0)), rspec],
        out_shape=[jax.ShapeDtypeStruct((S, 12 * BR), BF16), jax.ShapeDtypeStruct((8, BR), F32)],
        scratch_shapes=[pltpu.VMEM((TS + 8, BR), F32), pltpu.VMEM((TS + 8, BR), F32)],
        compiler_params=_params("arbitrary"),
    )(proj, proj, proj, proj, dcat, proj, proj, proj, proj, dcat, w8)


def _attn_mix_bwd(proj, dcat, os_, ls_, head_ones, BR):
    S = proj.shape[0]
    TS = _tile(S, 512)

    def body(dy_ref, g_ref, o1, o2, o3, l1, l2, l3, ones_ref, d1, d2, d3, e1, e2, e3, dg_ref):
        w1, w2, w3 = _mix_weights(l1[...], l2[...], l3[...])
        mix = w1 * o1[...] + w2 * o2[...] + w3 * o3[...]
        g = g_ref[...]
        dy = dy_ref[...]
        dmix = dy * _silu(g)
        dg_ref[...] = dy * mix * _dsilu(g)
        t = jnp.dot(dmix * mix, ones_ref[...], preferred_element_type=F32, precision=lax.Precision.HIGHEST)
        d1[...] = w1 * dmix
        d2[...] = w2 * dmix
        d3[...] = w3 * dmix
        e1[...] = -w1 * t
        e2[...] = -w2 * t
        e3[...] = -w3 * t

    spec = pl.BlockSpec((TS, BR), lambda i: (i, 0))
    outs = _call(
        body, name="attn_mix_bwd", grid=(S // TS,),
        in_specs=[_chunk(TS, BR, 1), _chunk(TS, BR, 7)] + [spec] * 6 + [pl.BlockSpec((BR, BR), lambda i: (0, 0))],
        out_specs=[spec] * 7,
        out_shape=[jax.ShapeDtypeStruct((S, BR), F32)] * 7,
        compiler_params=_params("arbitrary"),
    )(dcat, proj, *os_, *ls_, head_ones)
    return outs[0:3], outs[3:6], outs[6]


def _attn_bwd(proj, do, e, lse, bias, dil, BR):
    S = proj.shape[0]
    HD, U, NU, nbu, hpb, HBK = _attn_geometry(S, BR, dil)
    scale = HD ** -0.5
    combos = [(n, r) for n in range(nbu) for r in range(dil)]

    def body(q_ref, kc_ref, kp_ref, vc_ref, vp_ref, do_ref, e_ref, l_ref, b_ref, dq_ref, dk_ref, dv_ref, db_ref,
             sbuf, dpbuf, pbuf, dsbuf, dkacc, dvacc, nxk, nxv, cark, carv):
        j = pl.program_id(1)
        mu = NU - 1 - j
        band, is_cur = _attn_mask()
        first = jnp.logical_and(band, jnp.logical_or(is_cur, mu > 0))

        @pl.when(j == 0)
        def _():
            cark[...] = jnp.zeros_like(cark)
            carv[...] = jnp.zeros_like(carv)
            db_ref[...] = jnp.zeros_like(db_ref)

        for ref in (dkacc, dvacc, nxk, nxv):
            ref[...] = jnp.zeros_like(ref)

        def keys(ref_c, ref_p, n, r):
            prev = ref_c[_attn_rows(n - 1, r, dil), :] if n > 0 else ref_p[_attn_rows(nbu - 1, r, dil), :]
            return jnp.concatenate([prev, ref_c[_attn_rows(n, r, dil), :]], axis=0).astype(BF16)

        for c, (n, r) in enumerate(combos):
            rows = _attn_rows(n, r, dil)
            q = (q_ref[rows, :] * scale).astype(BF16)
            dob = do_ref[rows, :].astype(BF16)
            kk, vv = keys(kc_ref, kp_ref, n, r), keys(vc_ref, vp_ref, n, r)
            for h in range(hpb):
                sl = slice(h * HD, (h + 1) * HD)
                sbuf[c * hpb + h] = _dot_nt(q[:, sl], kk[:, sl])
                dpbuf[c * hpb + h] = _dot_nt(dob[:, sl], vv[:, sl])
        dbs = [None] * hpb
        for c, (n, r) in enumerate(combos):
            rows = _attn_rows(n, r, dil)
            lse, ev = l_ref[rows, :], e_ref[rows, :]
            for h in range(hpb):
                one = slice(h * HD, h * HD + 1)
                s = jnp.where(band if n > 0 else first, sbuf[c * hpb + h] + b_ref[h], NEG)
                p = jnp.exp(s - lse[:, one])
                ds = p * (dpbuf[c * hpb + h] + ev[:, one])
                pbuf[c * hpb + h] = p.astype(BF16)
                dsbuf[c * hpb + h] = ds.astype(BF16)
                dbs[h] = ds if dbs[h] is None else dbs[h] + ds
        for h in range(hpb):
            db_ref[h] += dbs[h]
        for c, (n, r) in enumerate(combos):
            rows = _attn_rows(n, r, dil)
            q = (q_ref[rows, :] * scale).astype(BF16)
            dob = do_ref[rows, :].astype(BF16)
            kk = keys(kc_ref, kp_ref, n, r)
            dq, dkk, dvv = [], [], []
            for h in range(hpb):
                sl = slice(h * HD, (h + 1) * HD)
                dsb = dsbuf[c * hpb + h]
                dq.append(_dot(dsb, kk[:, sl]) * scale)
                dkk.append(_dot_tn(dsb, q[:, sl]))
                dvv.append(_dot_tn(pbuf[c * hpb + h], dob[:, sl]))
            dq_ref[rows, :] = jnp.concatenate(dq, axis=1)
            dkk, dvv = jnp.concatenate(dkk, axis=1), jnp.concatenate(dvv, axis=1)
            dkacc[rows, :] += dkk[BLK:2 * BLK, :]
            dvacc[rows, :] += dvv[BLK:2 * BLK, :]
            if n > 0:
                prow = _attn_rows(n - 1, r, dil)
                dkacc[prow, :] += dkk[0:BLK, :]
                dvacc[prow, :] += dvv[0:BLK, :]
            else:
                prow = _attn_rows(nbu - 1, r, dil)
                nxk[prow, :] += dkk[0:BLK, :]
                nxv[prow, :] += dvv[0:BLK, :]
        dk_ref[...] = dkacc[...] + cark[...]
        dv_ref[...] = dvacc[...] + carv[...]
        cark[...] = nxk[...]
        carv[...] = nxv[...]

    def cur(c):
        return pl.BlockSpec((U, ATT_LANES), lambda hb, j: (NU - 1 - j, c * HBK + hb))

    def prev(c):
        return pl.BlockSpec((U, ATT_LANES), lambda hb, j: (jnp.maximum(NU - 2 - j, 0), c * HBK + hb))

    own = pl.BlockSpec((U, ATT_LANES), lambda hb, j: (NU - 1 - j, hb))
    bspec = pl.BlockSpec((hpb, BLK, 2 * BLK), lambda hb, j: (hb, 0, 0))
    nhb = len(combos) * hpb
    unit = pltpu.VMEM((U, ATT_LANES), F32)
    return _call(
        body, name="attn_bwd_d%d" % dil, grid=(HBK, NU),
        in_specs=[cur(4), cur(5), prev(5), cur(6), prev(6), own, own, own, bspec],
        out_specs=[own, own, own, bspec],
        out_shape=[jax.ShapeDtypeStruct((S, BR), F32)] * 3 + [jax.ShapeDtypeStruct((8, BLK, 2 * BLK), F32)],
        scratch_shapes=[pltpu.VMEM((nhb, BLK, 2 * BLK), F32)] * 2 + [pltpu.VMEM((nhb, BLK, 2 * BLK), BF16)] * 2 + [unit] * 6,
        compiler_params=_params("arbitrary", "arbitrary"),
    )(proj, proj, proj, proj, proj, do, e, lse, bias)


def _attn_dsum(ds, dbg, dproj, BR):
    S = dbg.shape[0]
    TS = _tile(S, 512)

    def body(*refs):
        o_ref = refs[11]
        for k in range(3):
            o_ref[:, k * BR:(k + 1) * BR] = (refs[k][...] + refs[3 + k][...] + refs[6 + k][...]).astype(BF16)
        o_ref[:, 3 * BR:4 * BR] = refs[9][...].astype(BF16)

    spec = pl.BlockSpec((TS, BR), lambda i: (i, 0))
    return _call(
        body, name="attn_dsum", grid=(S // TS,),
        in_specs=[spec] * 10 + [pl.BlockSpec(memory_space=pl.ANY)],
        out_specs=pl.BlockSpec((TS, 4 * BR), lambda i: (i, 1)),
        out_shape=jax.ShapeDtypeStruct(dproj.shape, BF16),
        input_output_aliases={10: 0},
        compiler_params=_params("arbitrary"),
    )(*[t for trip in ds for t in trip], dbg, dproj)


def _lru_bwd(proj, h, dcat, rows, wa, wx, dproj, BR):
    S = proj.shape[0]
    TS = _tile(S, 256)
    PAD = TS // 2
    nt = S // TS

    def body(cx_ref, hx_ref, cg_ref, h_ref, hh_ref, dy_ref, r_ref, wa_ref, wx_ref, _,
             dp_ref, sum_ref, dwa_ref, dwx_ref, ext, hext, aext, dext, sa, sb, carry):
        i = pl.program_id(0)
        ti = nt - 1 - i

        @pl.when(i == 0)
        def _():
            sa[TS:TS + PAD, :] = jnp.ones((PAD, BR), F32)
            sb[TS:TS + PAD, :] = jnp.zeros((PAD, BR), F32)
            carry[...] = jnp.zeros_like(carry)
            dext[TS:TS + 8, :] = jnp.zeros((8, BR), F32)
            sum_ref[...] = jnp.zeros_like(sum_ref)
            dwa_ref[...] = jnp.zeros_like(dwa_ref)
            dwx_ref[...] = jnp.zeros_like(dwx_ref)

        w0, w1, w2, w3 = r_ref[0:1, :], r_ref[1:2, :], r_ref[2:3, :], r_ref[3:4, :]
        sp = r_ref[7:8, :]
        cx = cx_ref[...]
        ext[0:8, :] = jnp.where(ti > 0, hx_ref[...], 0.0)
        ext[8:8 + TS, :] = cx
        x3, x2, x1 = ext[pl.ds(5, TS), :], ext[pl.ds(6, TS), :], ext[pl.ds(7, TS), :]
        xc = x3 * w0 + x2 * w1 + x1 * w2 + cx * w3 + r_ref[4:5, :]
        wa_, wx_ = wa_ref[...], wx_ref[...]
        r, ig, a, mult = _lru_gates(xc, wa_, wx_, r_ref[5:6, :], r_ref[6:7, :], sp)
        cg, dy, hv = cg_ref[...], dy_ref[...], h_ref[...]
        dp_ref[:, BR:2 * BR] = (dy * hv * _dsilu(cg)).astype(BF16)
        aext[0:TS, :] = a
        aext[TS:TS + 8, :] = jnp.broadcast_to(carry[0:1, :], (8, BR))
        sa[0:TS, :] = aext[pl.ds(1, TS), :]
        sb[0:TS, :] = dy * _silu(cg)
        d = 1
        while d < TS:
            A, B = sa[0:TS, :], sb[0:TS, :]
            As, Bs = sa[pl.ds(d, TS), :], sb[pl.ds(d, TS), :]
            sb[0:TS, :] = B + A * Bs
            sa[0:TS, :] = A * As
            d *= 2
        gh = sb[0:TS, :] + sa[0:TS, :] * carry[1:2, :]
        carry[0:1, :] = a[0:1, :]
        carry[1:2, :] = gh[0:1, :]
        hext[0:8, :] = jnp.where(ti > 0, hh_ref[...], 0.0)
        hext[8:8 + TS, :] = hv
        da = gh * hext[pl.ds(7, TS), :]
        dmult = gh * ig * xc
        dig = gh * mult * xc
        dxc = gh * mult * ig
        dla = da * a - dmult * a * a / mult
        dpr = dla * (-LRU_C) * sp * r * (1.0 - r)
        dpi = dig * ig * (1.0 - ig)
        xb, dprb, dpib = xc.astype(BF16), dpr.astype(BF16), dpi.astype(BF16)
        dwa_ref[...] += _dot_tn(xb, dprb)
        dwx_ref[...] += _dot_tn(xb, dpib)
        dxc = dxc + _dot_nt(dprb, wa_) + _dot_nt(dpib, wx_)
        sum_ref[0:1, :] += _colsum(dxc * x3)
        sum_ref[1:2, :] += _colsum(dxc * x2)
        sum_ref[2:3, :] += _colsum(dxc * x1)
        sum_ref[3:4, :] += _colsum(dxc * cx)
        sum_ref[4:5, :] += _colsum(dxc)
        sum_ref[5:6, :] += _colsum(dpr)
        sum_ref[6:7, :] += _colsum(dpi)
        sum_ref[7:8, :] += _colsum(dla * (-LRU_C) * r)
        dext[0:TS, :] = dxc
        dcx = w3 * dxc + w2 * dext[pl.ds(1, TS), :] + w1 * dext[pl.ds(2, TS), :] + w0 * dext[pl.ds(3, TS), :]
        dext[TS:TS + 8, :] = dxc[0:8, :]
        dp_ref[:, 0:BR] = dcx.astype(BF16)

    def rev(k):
        return pl.BlockSpec((TS, BR), lambda i: (nt - 1 - i, k))

    def rev_halo(k):
        return pl.BlockSpec((8, BR), lambda i: (jnp.maximum((nt - 1 - i) * (TS // 8) - 1, 0), k))

    full = pl.BlockSpec((BR, BR), lambda i: (0, 0))
    rspec = pl.BlockSpec((8, BR), lambda i: (0, 0))
    return _call(
        body, name="lru_bwd", grid=(nt,),
        in_specs=[rev(8), rev_halo(8), rev(9), rev(0), rev_halo(0), rev(2), rspec, full, full, pl.BlockSpec(memory_space=pl.ANY)],
        out_specs=[pl.BlockSpec((TS, 2 * BR), lambda i: (nt - 1 - i, 4)), rspec, full, full],
        input_output_aliases={9: 0},
        out_shape=[jax.ShapeDtypeStruct(dproj.shape, BF16), jax.ShapeDtypeStruct((8, BR), F32),
                   jax.ShapeDtypeStruct((BR, BR), F32), jax.ShapeDtypeStruct((BR, BR), F32)],
        scratch_shapes=[pltpu.VMEM((TS + 8, BR), F32)] * 4 + [pltpu.VMEM((TS + PAD, BR), F32)] * 2 + [pltpu.VMEM((8, BR), F32)],
        compiler_params=_params("arbitrary"),
    )(proj, proj, proj, h, h, dcat, rows, wa, wx, dproj)


def _s5_bwd(proj, dcat, xre_all, xim_all, qw_re, qw_im, bbre, bbim, cre, cim, rows, wg, dproj, BR):
    S = proj.shape[0]
    TS, W = qw_re.shape
    PAD = TS // 2
    nt = S // TS
    CB = 8 * S5_CH
    SB = 8 * S5_STATE
    nblk = BR // CB

    def body(u_ref, dg_ref, dy_ref, xre_ref, xim_ref, hre_ref, him_ref, qr_ref, qi_ref, bbre_ref, bbim_ref, cre_ref, cim_ref,
             r_ref, wg_ref, _, dp_ref, sum_ref, dwg_ref, da_ref, dbbre_ref, dbbim_ref, dcre_ref, dcim_ref,
             sre, sim, ere, eim, ypre_s, dyp_s, carry):
        i = pl.program_id(0)
        ti = nt - 1 - i

        @pl.when(i == 0)
        def _():
            sre[TS:TS + PAD, :] = jnp.zeros((PAD, SB), F32)
            sim[TS:TS + PAD, :] = jnp.zeros((PAD, SB), F32)
            carry[...] = jnp.zeros_like(carry)
            for ref in (sum_ref, dwg_ref, da_ref, dbbre_ref, dbbim_ref, dcre_ref, dcim_ref):
                ref[...] = jnp.zeros_like(ref)

        u, dg, dy = u_ref[...], dg_ref[...], dy_ref[...]
        for m in range(nblk):
            cs, ws = slice(m * CB, (m + 1) * CB), slice(m * SB, (m + 1) * SB)
            ypre_s[:, cs] = (_dot(xre_ref[:, ws].astype(BF16), cre_ref[ws, :]) - _dot(xim_ref[:, ws].astype(BF16), cim_ref[ws, :]))
        ypre = ypre_s[...] + r_ref[0:1, :] * u
        yg = _gelu(ypre)
        wg_ = wg_ref[...]
        s = _sigmoid(_dot(yg.astype(BF16), wg_) + r_ref[1:2, :])
        dgl = dy * _silu(dg)
        dp_ref[:, BR:2 * BR] = (dy * yg * s * _dsilu(dg)).astype(BF16)
        dps = dgl * yg * s * (1.0 - s)
        dpsb = dps.astype(BF16)
        sum_ref[1:2, :] += _colsum(dps)
        dwg_ref[...] += _dot_tn(yg.astype(BF16), dpsb)
        dyp = (dgl * s + _dot_nt(dpsb, wg_)) * _dgelu(ypre)
        sum_ref[0:1, :] += _colsum(dyp * u)
        dyp_s[...] = dyp
        for m in range(nblk):
            cs, ws = slice(m * CB, (m + 1) * CB), slice(m * SB, (m + 1) * SB)
            dypb = dyp_s[:, cs].astype(BF16)
            xre, xim = xre_ref[:, ws], xim_ref[:, ws]
            dcre_ref[ws, :] += _dot_tn(xre.astype(BF16), dypb)
            dcim_ref[ws, :] -= _dot_tn(xim.astype(BF16), dypb)
            sre[0:TS, :] = _dot_nt(dypb, cre_ref[ws, :])
            sim[0:TS, :] = -_dot_nt(dypb, cim_ref[ws, :])
            d = 1
            while d < TS:
                ar, ai = qr_ref[TS - d:TS - d + 1, ws], -qi_ref[TS - d:TS - d + 1, ws]
                xr, xi = sre[0:TS, :], sim[0:TS, :]
                sr, si = sre[pl.ds(d, TS), :], sim[pl.ds(d, TS), :]
                sre[0:TS, :] = xr + ar * sr - ai * si
                sim[0:TS, :] = xi + ar * si + ai * sr
                d *= 2
            cr, ci = carry[0:1, ws], carry[1:2, ws]
            qr, qi = qr_ref[:, ws], -qi_ref[:, ws]
            gre = sre[0:TS, :] + qr * cr - qi * ci
            gim = sim[0:TS, :] + qr * ci + qi * cr
            carry[0:1, ws] = gre[0:1, :]
            carry[1:2, ws] = gim[0:1, :]
            ere[0:8, :] = jnp.where(ti > 0, hre_ref[:, ws], 0.0)
            eim[0:8, :] = jnp.where(ti > 0, him_ref[:, ws], 0.0)
            ere[8:8 + TS, :] = xre
            eim[8:8 + TS, :] = xim
            xpr, xpi = ere[pl.ds(7, TS), :], eim[pl.ds(7, TS), :]
            da_ref[0:1, ws] += _colsum(gre * xpr + gim * xpi)
            da_ref[1:2, ws] += _colsum(gim * xpr - gre * xpi)
            greb, gimb = gre.astype(BF16), gim.astype(BF16)
            ub = u_ref[:, cs].astype(BF16)
            dbbre_ref[cs, :] += _dot_tn(ub, greb)
            dbbim_ref[cs, :] += _dot_tn(ub, gimb)
            du = _dot_nt(greb, bbre_ref[cs, :]) + _dot_nt(gimb, bbim_ref[cs, :]) + r_ref[0:1, cs] * dyp_s[:, cs]
            dp_ref[:, cs] = du.astype(BF16)

    def rev(width, k):
        return pl.BlockSpec((TS, width), lambda i: (nt - 1 - i, k))

    def rev_halo(width):
        return pl.BlockSpec((8, width), lambda i: (jnp.maximum((nt - 1 - i) * (TS // 8) - 1, 0), 0))

    def const(shape):
        return pl.BlockSpec(shape, lambda i: (0, 0))

    return _call(
        body, name="s5_bwd", grid=(nt,),
        in_specs=[rev(BR, 10), rev(BR, 11), rev(BR, 3), rev(W, 0), rev(W, 0), rev_halo(W), rev_halo(W),
                  const((TS, W)), const((TS, W)), const((BR, SB)), const((BR, SB)), const((W, CB)), const((W, CB)),
                  const((8, BR)), const((BR, BR)), pl.BlockSpec(memory_space=pl.ANY)],
        input_output_aliases={15: 0},
        out_specs=[pl.BlockSpec((TS, 2 * BR), lambda i: (nt - 1 - i, 5)), const((8, BR)), const((BR, BR)), const((8, W)),
                   const((BR, SB)), const((BR, SB)), const((W, CB)), const((W, CB))],
        out_shape=[jax.ShapeDtypeStruct(dproj.shape, BF16), jax.ShapeDtypeStruct((8, BR), F32), jax.ShapeDtypeStruct((BR, BR), F32),
                   jax.ShapeDtypeStruct((8, W), F32), jax.ShapeDtypeStruct((BR, SB), F32), jax.ShapeDtypeStruct((BR, SB), F32),
                   jax.ShapeDtypeStruct((W, CB), F32), jax.ShapeDtypeStruct((W, CB), F32)],
        scratch_shapes=[pltpu.VMEM((TS + PAD, SB), F32)] * 2 + [pltpu.VMEM((TS + 8, SB), F32)] * 2
        + [pltpu.VMEM((TS, BR), F32)] * 2 + [pltpu.VMEM((8, W), F32)],
        compiler_params=_params("arbitrary"),
    )(proj, proj, dcat, xre_all, xim_all, xre_all, xim_all, qw_re, qw_im, bbre, bbim, cre, cim, rows, wg, dproj)


_WEIGHTS = ['rel_bias', 'w_ada', 'b_ada', 'w_in', 'conv_a', 'conv_c', 'conv_c_b', 'lru_wa', 'lru_ba', 'lru_wx', 'lru_bx',
            'lru_lambda', 's5_lam_re', 's5_lam_im', 's5_log_dt', 's5_b_re', 's5_b_im', 's5_c_re', 's5_c_im', 's5_d',
            's5_w_glu', 's5_b_glu', 'w_out', 'ln_g', 'ln_b']
_LAYER_SMALL = ['conv_a', 'conv_c', 'conv_c_b', 'lru_wa', 'lru_ba', 'lru_wx', 'lru_bx', 'lru_lambda', 's5_lam_re', 's5_lam_im',
                's5_log_dt', 's5_b_re', 's5_b_im', 's5_c_re', 's5_c_im', 's5_d', 's5_b_glu']
_SMALL = ['rel_bias'] + _LAYER_SMALL + ['ln_g', 'ln_b']


def _t5_bucket(dist):
    max_exact = REL_BUCKETS // 2
    nf = jnp.maximum(dist, 1).astype(F32)
    large = max_exact + (jnp.log(nf / max_exact) / math.log(REL_MAX_DIST / max_exact)
                         * (REL_BUCKETS - max_exact)).astype(jnp.int32)
    large = jnp.minimum(large, REL_BUCKETS - 1)
    return jnp.where(dist < max_exact, dist, large)


def _bias_tables(rel_bias):
    i = jnp.arange(BLK)[:, None]
    j = jnp.arange(2 * BLK)[None, :]
    delta = i + BLK - j
    out = []
    for window, dil in DILATIONS:
        bucket = _t5_bucket(jnp.clip(delta, 0, window // dil) * dil)
        onehot = (bucket[:, :, None] == jnp.arange(REL_BUCKETS)[None, None, :]).astype(F32)
        out.append(jnp.einsum('ijb,bh->hij', onehot, rel_bias, precision=lax.Precision.HIGHEST))
    return jnp.stack(out)


def _prep_layer(q):
    eye8 = jnp.eye(8, dtype=F32)
    G = q['s5_lam_re'].shape[0]
    nblk = G // 8

    def block_diag(w):
        hd = w.shape[1]
        return (w[:, :, None, :] * eye8[:, None, :, None]).reshape(8 * hd, 8 * hd)

    def compact_b(bb):
        t = jnp.transpose(bb.reshape(nblk, 8, S5_STATE, S5_CH), (0, 1, 3, 2))
        return (t[:, :, :, None, :] * eye8[None, :, None, :, None]).reshape(G * S5_CH, 8 * S5_STATE)

    def compact_c(cc):
        t = jnp.transpose(cc.reshape(nblk, 8, S5_CH, S5_STATE), (0, 1, 3, 2))
        return (t[:, :, :, None, :] * eye8[None, :, None, :, None]).reshape(G * S5_STATE, 8 * S5_CH)

    lam_re, lam_im = q['s5_lam_re'], q['s5_lam_im']
    dt = jnp.exp(q['s5_log_dt'])[:, None]
    mag = jnp.exp(lam_re * dt)
    ab_re = mag * jnp.cos(lam_im * dt)
    ab_im = mag * jnp.sin(lam_im * dt)
    den = lam_re * lam_re + lam_im * lam_im
    f_re = ((ab_re - 1.0) * lam_re + ab_im * lam_im) / den
    f_im = (ab_im * lam_re - (ab_re - 1.0) * lam_im) / den
    b_re, b_im = q['s5_b_re'], q['s5_b_im']
    bb_re = f_re[..., None] * b_re - f_im[..., None] * b_im
    bb_im = f_re[..., None] * b_im + f_im[..., None] * b_re
    return dict(
        conv_a=q['conv_a'], conv_c=q['conv_c'], conv_c_b=q['conv_c_b'], lru_ba=q['lru_ba'], lru_bx=q['lru_bx'],
        sp=jax.nn.softplus(-q['lru_lambda']), wa=block_diag(q['lru_wa']), wx=block_diag(q['lru_wx']),
        ar=ab_re.reshape(-1), ai=ab_im.reshape(-1), bbre=compact_b(bb_re), bbim=compact_b(bb_im),
        cre=compact_c(q['s5_c_re']), cim=compact_c(q['s5_c_im']), s5_d=q['s5_d'], s5_b_glu=q['s5_b_glu'])


def _powers(ar, ai, T):
    pr, pi = ar[None, :], ai[None, :]
    while pr.shape[0] < T:
        lr, li = pr[-1:], pi[-1:]
        pr, pi = jnp.concatenate([pr, pr * lr - pi * li]), jnp.concatenate([pi, pr * li + pi * lr])
    return pr[:T], pi[:T]


def _rows8(vecs, width):
    rows = [v.reshape(1, width).astype(F32) for v in vecs]
    return jnp.concatenate(rows + [jnp.zeros((8 - len(rows), width), F32)], axis=0)


def kernel(x, c, rel_bias, w_ada, b_ada, w_in, conv_a, conv_c, conv_c_b, lru_wa, lru_ba, lru_wx, lru_bx, lru_lambda, s5_lam_re, s5_lam_im, s5_log_dt, s5_b_re, s5_b_im, s5_c_re, s5_c_im, s5_d, s5_w_glu, s5_b_glu, w_out, ln_g, ln_b, loss_target, m_rel_bias, m_w_ada, m_b_ada, m_w_in, m_conv_a, m_conv_c, m_conv_c_b, m_lru_wa, m_lru_ba, m_lru_wx, m_lru_bx, m_lru_lambda, m_s5_lam_re, m_s5_lam_im, m_s5_log_dt, m_s5_b_re, m_s5_b_im, m_s5_c_re, m_s5_c_im, m_s5_d, m_s5_w_glu, m_s5_b_glu, m_w_out, m_ln_g, m_ln_b, v_rel_bias, v_w_ada, v_b_ada, v_w_in, v_conv_a, v_conv_c, v_conv_c_b, v_lru_wa, v_lru_ba, v_lru_wx, v_lru_bx, v_lru_lambda, v_s5_lam_re, v_s5_lam_im, v_s5_log_dt, v_s5_b_re, v_s5_b_im, v_s5_c_re, v_s5_c_im, v_s5_d, v_s5_w_glu, v_s5_b_glu, v_w_out, v_ln_g, v_ln_b):
    a = dict(locals())
    me = 4 * lax.axis_index("x") + 2 * lax.axis_index("y") + lax.axis_index("c")
    x0, target = x[0], loss_target[0]
    S, D = x0.shape
    BR = D // 4
    NS = w_in.shape[2]
    SH = BR // NDEV
    assert S % (BLK * DILATIONS[-1][1]) == 0 and BR % (8 * S5_CH) == 0
    TS5 = _tile(S, 256)

    small = _exchange(_pack([c, conv_a, conv_c]), True, "gather_small").reshape(NDEV, -1)
    c_all = small[:, :D]
    na, nc = DEPTH * 3 * SH, DEPTH * 4 * SH
    conv_a_full = jnp.transpose(small[:, D:D + na].reshape(NDEV, DEPTH, 3, SH), (1, 2, 0, 3)).reshape(DEPTH, 3, BR)
    conv_c_full = jnp.transpose(small[:, D + na:D + na + nc].reshape(NDEV, DEPTH, 4, SH), (1, 2, 0, 3)).reshape(DEPTH, 4, BR)
    wg_in = _exchange(w_in.astype(BF16), True, "gather_w_in")
    wg_out = _exchange(w_out.astype(BF16), True, "gather_w_out")
    wg_glu = _exchange(s5_w_glu.astype(BF16), True, "gather_w_glu")

    b_cols = lax.dynamic_slice(b_ada, (0, me * NS), (DEPTH, NS)).reshape(DEPTH, 1, NS)
    ada_all = _exchange(_ada_cols(c_all, w_ada, b_cols), True, "gather_ada")
    ada_me = lax.dynamic_index_in_dim(ada_all, me, axis=2, keepdims=False)
    ada_me = jnp.transpose(ada_me, (1, 0, 2)).reshape(DEPTH, 3 * D)
    shift, scale, gate = ada_me[:, :D], ada_me[:, D:2 * D], ada_me[:, 2 * D:]

    bias_tabs, bias_pull = jax.vjp(_bias_tables, rel_bias)
    HD = BR // 8
    head_ones = jnp.kron(jnp.eye(8, dtype=F32), jnp.ones((HD, HD), F32))

    saved = []
    xl = x0
    for l in range(DEPTH):
        q = {n: a[n][l] for n in _LAYER_SMALL}
        q['conv_a'], q['conv_c'] = conv_a_full[l], conv_c_full[l]
        kp, pull = jax.vjp(_prep_layer, q)
        pw_re, pw_im = _powers(lax.stop_gradient(kp['ar']), lax.stop_gradient(kp['ai']), TS5)
        w_out_l = wg_out[:, l].reshape(D, D)
        w_glu_l = wg_glu[:, l].reshape(BR, BR)
        mod = _rows8([1.0 + scale[l], shift[l]], D)
        arows = _rows8(list(kp['conv_a']), BR)
        lrows = _rows8(list(kp['conv_c']) + [kp['conv_c_b'], kp['lru_ba'], kp['lru_bx'], kp['sp']], BR)
        srows = _rows8([kp['s5_d'], kp['s5_b_glu']], BR)
        orows = _rows8([1.0 + gate[l], ln_g[l], ln_b[l]], D)
        wa, wx = kp['wa'].astype(BF16), kp['wx'].astype(BF16)
        s5w = [kp[n].astype(BF16) for n in ('bbre', 'bbim', 'cre', 'cim')]

        proj, hbf = _proj(xl, mod, wg_in, l)
        ya = _conv_a_fwd(proj, arows, BR)
        os_, ls_ = [], []
        for g, (_, dil) in enumerate(DILATIONS):
            o, lse = _attn_fwd(proj, bias_tabs[g], dil, BR)
            os_.append(o)
            ls_.append(lse)
        yb = _attn_mix_fwd(proj, os_, ls_, BR)
        yc, hs = _lru_fwd(proj, lrows, wa, wx, BR)
        yd, xre, xim = _s5_fwd(proj, pw_re, pw_im, *s5w, srows, w_glu_l, BR)
        xn, y, cat = _out_ln([ya, yb, yc, yd], w_out_l, xl, orows)
        saved.append(dict(x=xl, proj=proj, hbf=hbf, os=os_, ls=ls_, hs=hs, xre=xre, xim=xim, y=y, cat=cat, pull=pull,
                          mod=mod, arows=arows, lrows=lrows, srows=srows, orows=orows, wa=wa, wx=wx, s5w=s5w,
                          qw=(pw_re[::-1], pw_im[::-1]), w_out=w_out_l, w_glu=w_glu_l))
        xl = xn

    dout, loss_local = _loss_grad(xl, target)
    loss = lax.psum(loss_local, ("x", "y", "c"))

    dbias = jnp.zeros_like(bias_tabs)
    lgrads, dada, dw_in_parts, dw_out_parts, dw_glu_parts = [None] * DEPTH, [None] * DEPTH, [None] * DEPTH, [None] * DEPTH, [None] * DEPTH
    for l in reversed(range(DEPTH)):
        sv = saved[l]
        proj = sv['proj']
        dxres, dyb, dcat, lnsum = _ln_bwd(dout, sv['x'], sv['y'], sv['orows'], sv['w_out'])
        dw_out_parts[l] = _matmul_tn_rows(sv['cat'], dyb, D // NDEV, "dw_out").reshape(NDEV, D // NDEV, D)
        dproj, asum = _conv_a_bwd(proj, dcat, sv['arows'], BR)
        dos, es, dbg = _attn_mix_bwd(proj, dcat, sv['os'], sv['ls'], head_ones, BR)
        dqkv, dbs = [], []
        for g, (_, dil) in enumerate(DILATIONS):
            dq_, dk_, dv_, db_ = _attn_bwd(proj, dos[g], es[g], sv['ls'][g], bias_tabs[g], dil, BR)
            dqkv.append((dq_, dk_, dv_))
            dbs.append(db_)
        dbias = dbias + jnp.stack(dbs)
        dproj = _attn_dsum(dqkv, dbg, dproj, BR)
        dproj, lsum, dwa, dwx = _lru_bwd(proj, sv['hs'], dcat, sv['lrows'], sv['wa'], sv['wx'], dproj, BR)
        dproj, ssum, dwg, da_, dbbre, dbbim, dcre, dcim = _s5_bwd(
            proj, dcat, sv['xre'], sv['xim'], *sv['qw'], *sv['s5w'], sv['srows'], sv['w_glu'], dproj, BR)
        dw_glu_parts[l] = dwg.reshape(NDEV, BR // NDEV, BR)
        dout, msum = _dh(dproj, wg_in, l, dxres, sv['x'], sv['mod'])
        dw_in_parts[l] = _matmul_tn_cols(sv['hbf'], dproj, NS, "dw_in")
        dada[l] = jnp.concatenate([msum[1], msum[0], lnsum[2]])
        dkp = dict(conv_a=asum[0:3], conv_c=lsum[0:4], conv_c_b=lsum[4], lru_ba=lsum[5], lru_bx=lsum[6], sp=lsum[7],
                   wa=dwa, wx=dwx, ar=da_[0], ai=da_[1], bbre=dbbre, bbim=dbbim, cre=dcre, cim=dcim,
                   s5_d=ssum[0], s5_b_glu=ssum[1])
        lgrads[l] = dict(sv['pull'](dkp)[0], ln_g=lnsum[0], ln_b=lnsum[1])
    grad_x = dout[None]

    out = {}

    def put(name, res):
        out[name] = res

    def stacked(fn):
        res = [fn(l) for l in range(DEPTH)]
        return tuple(jnp.stack([r[k] for r in res]) for k in range(4))

    put('w_in', stacked(lambda l: _adamw(_exchange(dw_in_parts[l], False, "scatter_w_in"), w_in[l], m_w_in[l], v_w_in[l], "adamw_w_in")))
    put('w_out', stacked(lambda l: _adamw(_exchange(dw_out_parts[l], False, "scatter_w_out"), w_out[l], m_w_out[l], v_w_out[l], "adamw_w_out")))
    put('s5_w_glu', stacked(lambda l: _adamw(_exchange(dw_glu_parts[l], False, "scatter_w_glu"), s5_w_glu[l], m_s5_w_glu[l], v_s5_w_glu[l], "adamw_w_glu")))

    dada_all = _exchange(jnp.stack(dada), True, "gather_dada")
    dada_cols = jnp.transpose(lax.dynamic_slice(dada_all, (0, 0, me * NS), (NDEV, DEPTH, NS)), (1, 0, 2))
    gw_ada, gb_ada = _ada_grads(c_all, dada_all, dada_cols)
    res = _adamw(gw_ada.reshape(DEPTH * D, NS), w_ada.reshape(DEPTH * D, NS), m_w_ada.reshape(DEPTH * D, NS),
                 v_w_ada.reshape(DEPTH * D, NS), "adamw_w_ada")
    put('w_ada', tuple(r.reshape(DEPTH, D, NS) for r in res))

    local = {'rel_bias': bias_pull(dbias)[0]}
    for n in _LAYER_SMALL + ['ln_g', 'ln_b']:
        local[n] = jnp.stack([lgrads[l][n] for l in range(DEPTH)])
    shapes = [local[n].shape for n in _SMALL]
    summed = _sum8(_exchange(_pack([local[n] for n in _SMALL]), True, "gather_small_grads"), "sum_small_grads")
    gs = dict(zip(_SMALL, _unpack(summed, shapes)))
    gs['conv_a'] = lax.dynamic_slice_in_dim(gs['conv_a'], me * SH, SH, axis=2)
    gs['conv_c'] = lax.dynamic_slice_in_dim(gs['conv_c'], me * SH, SH, axis=2)
    gs['b_ada'] = gb_ada
    names = _SMALL + ['b_ada']
    shapes = [a[n].shape for n in names]
    res = _adamw(_pack([gs[n] for n in names]), _pack([a[n] for n in names]), _pack([a['m_' + n] for n in names]),
                 _pack([a['v_' + n] for n in names]), "adamw_small")
    unpacked = [_unpack(r, shapes) for r in res]
    for k, n in enumerate(names):
        put(n, tuple(unpacked[j][k] for j in range(4)))

    return (loss, grad_x, *[out[n][0] for n in _WEIGHTS], *[out[n][1] for n in _WEIGHTS],
            *[out[n][2] for n in _WEIGHTS], *[out[n][3] for n in _WEIGHTS])
```

```python
import functools
import math

import jax
import jax.numpy as jnp
from jax import lax
from jax.experimental import pallas as pl
from jax.experimental.pallas import tpu as pltpu

F32 = jnp.float32
BF16 = jnp.bfloat16

NDEV = 8
DEPTH = 2
BLK = 128
DILATIONS = ((128, 1), (512, 4), (2048, 16))
REL_BUCKETS = 32
REL_MAX_DIST = 2048
LRU_C = 8.0
S5_CH = 16
S5_STATE = 64
ALPHA = (2 * DEPTH) ** 0.25
LN_EPS = 1e-5
ADAM_LR, ADAM_B1, ADAM_B2, ADAM_EPS, ADAM_WD, ADAM_STEP = 0.001, 0.9, 0.999, 1e-08, 0.01, 10
NEG = -1e30
VMEM_LIMIT = 56 * 1024 * 1024


def _call(body, **kw):
    return pl.pallas_call(body, **kw)


def _params(*sem):
    return pltpu.CompilerParams(dimension_semantics=sem, vmem_limit_bytes=VMEM_LIMIT)


def _sigmoid(x):
    return 1.0 / (1.0 + jnp.exp(-x))


def _silu(x):
    return x * _sigmoid(x)


def _dsilu(x):
    s = _sigmoid(x)
    return s * (1.0 + x * (1.0 - s))


_GC = math.sqrt(2.0 / math.pi)


def _gelu(x):
    return 0.5 * x * (1.0 + jnp.tanh(_GC * (x + 0.044715 * x * x * x)))


def _dgelu(x):
    t = jnp.tanh(_GC * (x + 0.044715 * x * x * x))
    return 0.5 * (1.0 + t) + 0.5 * x * (1.0 - t * t) * _GC * (1.0 + 3.0 * 0.044715 * x * x)


def _expm1(x):
    series = x * (1.0 + x * (1.0 / 2) * (1.0 + x * (1.0 / 3) * (1.0 + x * (1.0 / 4) * (1.0 + x * (1.0 / 5) * (1.0 + x * (1.0 / 6))))))
    return jnp.where(jnp.abs(x) < 0.25, series, jnp.exp(x) - 1.0)


def _dot(a, b):
    return jnp.dot(a, b, preferred_element_type=F32)


def _dot_nt(a, b):
    return lax.dot_general(a, b, (((1,), (1,)), ((), ())), preferred_element_type=F32)


def _dot_tn(a, b):
    return lax.dot_general(a, b, (((0,), (0,)), ((), ())), preferred_element_type=F32)


def _colsum(v):
    return jnp.sum(v, axis=0, keepdims=True)


def _exchange(x, gather, name):
    blk = x.shape if gather else x.shape[1:]

    def body(x_ref, o_ref, send_sems, recv_sems, local_sem):
        ix, iy, ic = lax.axis_index("x"), lax.axis_index("y"), lax.axis_index("c")
        me = 4 * ix + 2 * iy + ic

        def src(dev):
            return x_ref if gather else x_ref.at[dev]

        local = pltpu.make_async_copy(src(me), o_ref.at[me], local_sem)
        local.start()
        copies = []
        for k in range(1, NDEV):
            px = 1 - ix if k & 4 else ix
            py = 1 - iy if k & 2 else iy
            pc = 1 - ic if k & 1 else ic
            peer = 4 * px + 2 * py + pc
            cp = pltpu.make_async_remote_copy(
                src_ref=src(peer), dst_ref=o_ref.at[me], send_sem=send_sems.at[k - 1], recv_sem=recv_sems.at[k - 1],
                device_id=(px, py, pc), device_id_type=pl.DeviceIdType.MESH)
            cp.start()
            copies.append(cp)
        for cp in copies:
            cp.wait()
        local.wait()

    return _call(
        body, name=name,
        out_shape=jax.ShapeDtypeStruct((NDEV,) + tuple(blk), x.dtype),
        in_specs=[pl.BlockSpec(memory_space=pl.ANY)],
        out_specs=pl.BlockSpec(memory_space=pl.ANY),
        scratch_shapes=[pltpu.SemaphoreType.DMA((NDEV - 1,)), pltpu.SemaphoreType.DMA((NDEV - 1,)), pltpu.SemaphoreType.DMA],
    )(x)


def _pack(arrs):
    flat = jnp.concatenate([a.reshape(-1).astype(F32) for a in arrs])
    n = flat.shape[0]
    quantum = 1024 if n <= 65536 else 65536
    npad = -(-n // quantum) * quantum
    return jnp.pad(flat, (0, npad - n)).reshape(npad // 128, 128)


def _unpack(packed, shapes):
    flat = packed.reshape(-1)
    out, off = [], 0
    for s in shapes:
        n = math.prod(s)
        out.append(flat[off:off + n].reshape(s))
        off += n
    return out


def _sum8(parts, name):
    _, R, C = parts.shape
    TR = R
    for cand in (512, 256, 128, 64, 32, 16, 8):
        if R % cand == 0:
            TR = cand
            break

    def body(p_ref, o_ref):
        acc = p_ref[0]
        for d in range(1, NDEV):
            acc = acc + p_ref[d]
        o_ref[...] = acc

    return _call(
        body, name=name, grid=(R // TR,),
        in_specs=[pl.BlockSpec((NDEV, TR, C), lambda i: (0, i, 0))],
        out_specs=pl.BlockSpec((TR, C), lambda i: (i, 0)),
        out_shape=jax.ShapeDtypeStruct((R, C), F32),
        compiler_params=_params("arbitrary"),
    )(parts)


def _adamw(g_parts, w, m, v, name):
    R, C = w.shape
    nparts = g_parts.shape[0] if g_parts.ndim == 3 else 0
    TR = R
    for cand in (256, 128, 64, 32, 16, 8):
        if R % cand == 0:
            TR = cand
            break

    def body(g_ref, w_ref, m_ref, v_ref, go_ref, d_ref, mo_ref, vo_ref):
        if nparts:
            g = g_ref[0].astype(F32)
            for d in range(1, nparts):
                g = g + g_ref[d].astype(F32)
        else:
            g = g_ref[...]
        m2 = ADAM_B1 * m_ref[...] + (1.0 - ADAM_B1) * g
        v2 = ADAM_B2 * v_ref[...] + (1.0 - ADAM_B2) * (g * g)
        m_hat = m2 / (1.0 - ADAM_B1 ** ADAM_STEP)
        v_hat = v2 / (1.0 - ADAM_B2 ** ADAM_STEP)
        go_ref[...] = g
        d_ref[...] = -ADAM_LR * (m_hat / (jnp.sqrt(v_hat) + ADAM_EPS) + ADAM_WD * w_ref[...])
        mo_ref[...] = m2
        vo_ref[...] = v2

    spec = pl.BlockSpec((TR, C), lambda i: (i, 0))
    gspec = pl.BlockSpec((nparts, TR, C), lambda i: (0, i, 0)) if nparts else spec
    return _call(
        body, name=name, grid=(R // TR,),
        in_specs=[gspec, spec, spec, spec], out_specs=[spec] * 4,
        out_shape=[jax.ShapeDtypeStruct((R, C), F32)] * 4,
        compiler_params=_params("arbitrary"),
    )(g_parts, w, m, v)


def _ada_cols(c_all, w_ada, b_cols):
    L, D, NS = w_ada.shape

    def body(c_ref, w_ref, b_ref, o_ref):
        cond = _silu(c_ref[...]).astype(BF16)
        o_ref[0] = _dot(cond, w_ref[0].astype(BF16)) + b_ref[0]

    return _call(
        body, name="ada_cols", grid=(L,),
        in_specs=[pl.BlockSpec((NDEV, D), lambda l: (0, 0)), pl.BlockSpec((1, D, NS), lambda l: (l, 0, 0)),
                  pl.BlockSpec((1, 1, NS), lambda l: (l, 0, 0))],
        out_specs=pl.BlockSpec((1, NDEV, NS), lambda l: (l, 0, 0)),
        out_shape=jax.ShapeDtypeStruct((L, NDEV, NS), F32),
        compiler_params=_params("arbitrary"),
    )(c_all, w_ada, b_cols)


def _ada_grads(c_all, dada_all, dada_cols):
    _, D = c_all.shape
    L, _, NS = dada_cols.shape
    D3 = dada_all.shape[2]

    def body(c_ref, da_ref, dc_ref, gw_ref, gb_ref):
        cond = _silu(c_ref[...]).astype(BF16)
        gw_ref[0] = _dot_tn(cond, dc_ref[0].astype(BF16))
        acc = da_ref[0]
        for d in range(1, NDEV):
            acc = acc + da_ref[d]
        gb_ref[...] = acc

    return _call(
        body, name="ada_grads", grid=(L,),
        in_specs=[pl.BlockSpec((NDEV, D), lambda l: (0, 0)), pl.BlockSpec((NDEV, L, D3), lambda l: (0, 0, 0)),
                  pl.BlockSpec((1, NDEV, NS), lambda l: (l, 0, 0))],
        out_specs=[pl.BlockSpec((1, D, NS), lambda l: (l, 0, 0)), pl.BlockSpec((L, D3), lambda l: (0, 0))],
        out_shape=[jax.ShapeDtypeStruct((L, D, NS), F32), jax.ShapeDtypeStruct((L, D3), F32)],
        compiler_params=_params("arbitrary"),
    )(c_all, dada_all, dada_cols)


def _tile(S, want):
    return want if S % want == 0 else S


def _modulate(x, mod):
    S, D = x.shape
    TS = _tile(S, 512)

    def body(x_ref, mod_ref, h_ref):
        h_ref[...] = (x_ref[...] * mod_ref[0:1, :] + mod_ref[1:2, :]).astype(BF16)

    spec = pl.BlockSpec((TS, D), lambda i: (i, 0))
    return _call(
        body, name="modulate", grid=(S // TS,),
        in_specs=[spec, pl.BlockSpec((8, D), lambda i: (0, 0))], out_specs=spec,
        out_shape=jax.ShapeDtypeStruct((S, D), BF16),
        compiler_params=_params("arbitrary"),
    )(x, mod)


def _peer(k):
    ix, iy, ic = lax.axis_index("x"), lax.axis_index("y"), lax.axis_index("c")
    bx, by, bc = (k >> 2) & 1, (k >> 1) & 1, k & 1
    px, py, pc = ix + bx - 2 * ix * bx, iy + by - 2 * iy * by, ic + bc - 2 * ic * bc
    return (px, py, pc), 4 * px + 2 * py + pc


def _proj_gather(h, w_shard, me1):
    S, D = h.shape
    NS = w_shard.shape[1]
    TS = _tile(S, 512)
    nt = S // TS

    def body(me_ref, h_ref, w_ref, o_ref, wg_ref, wbuf, send_sems, recv_sems, local_sem, load_sems):
        k, i = pl.program_id(0), pl.program_id(1)
        me = me_ref[0]

        def push(kk):
            peer, _ = _peer(kk)
            return pltpu.make_async_remote_copy(
                src_ref=w_ref, dst_ref=wg_ref.at[me], send_sem=send_sems.at[kk], recv_sem=recv_sems.at[kk],
                device_id=peer, device_id_type=pl.DeviceIdType.MESH)

        own = pltpu.make_async_copy(w_ref, wg_ref.at[me], local_sem)

        @pl.when(jnp.logical_and(k == 0, i == 0))
        def _():
            own.start()
            for kk in range(1, NDEV):
                push(kk).start()
            first = pltpu.make_async_copy(w_ref, wbuf.at[0], load_sems.at[0])
            first.start()
            first.wait()

        @pl.when(jnp.logical_and(k > 0, i == 0))
        def _():
            push(k).wait_recv()
            _, blk = _peer(k)
            load = pltpu.make_async_copy(wg_ref.at[blk], wbuf.at[k % 2], load_sems.at[k % 2])
            load.start()
            load.wait()

        o_ref[...] = _dot(h_ref[...], wbuf[k % 2])

        @pl.when(jnp.logical_and(k == NDEV - 1, i == nt - 1))
        def _():
            for kk in range(1, NDEV):
                push(kk).wait_send()
            own.wait()

    def col(k, i, me_ref):
        m = me_ref[0]
        return i, (m | k) - (m & k)

    grid_spec = pltpu.PrefetchScalarGridSpec(
        num_scalar_prefetch=1, grid=(NDEV, nt),
        in_specs=[pl.BlockSpec((TS, D), lambda k, i, me_ref: (i, 0)), pl.BlockSpec(memory_space=pl.ANY)],
        out_specs=[pl.BlockSpec((TS, NS), col), pl.BlockSpec(memory_space=pl.ANY)],
        scratch_shapes=[pltpu.VMEM((2, D, NS), BF16), pltpu.SemaphoreType.DMA((NDEV,)), pltpu.SemaphoreType.DMA((NDEV,)),
                        pltpu.SemaphoreType.DMA, pltpu.SemaphoreType.DMA((2,))])
    return _call(
        body, name="proj_gather", grid_spec=grid_spec,
        out_shape=[jax.ShapeDtypeStruct((S, NDEV * NS), F32), jax.ShapeDtypeStruct((NDEV, D, NS), BF16)],
        compiler_params=_params("arbitrary", "arbitrary"),
    )(me1, h, w_shard)


def _dw_scatter(a, b, blk, by_rows, me1, name):
    S = a.shape[0]
    TK = _tile(S, 512)
    nk = S // TK
    shape = (blk, b.shape[1]) if by_rows else (a.shape[1], blk)

    def body(me_ref, a_ref, b_ref, recv_ref, acc, stage, send_sems, recv_sems, local_sem):
        k, kk = pl.program_id(0), pl.program_id(1)
        me = me_ref[0]

        def push(step):
            peer, _ = _peer(step)
            return pltpu.make_async_remote_copy(
                src_ref=stage.at[step % 2], dst_ref=recv_ref.at[me], send_sem=send_sems.at[step], recv_sem=recv_sems.at[step],
                device_id=peer, device_id_type=pl.DeviceIdType.MESH)

        own = pltpu.make_async_copy(stage.at[0], recv_ref.at[me], local_sem)

        @pl.when(kk == 0)
        def _():
            acc[...] = jnp.zeros_like(acc)
        acc[...] += _dot_tn(a_ref[...], b_ref[...])

        @pl.when(kk == nk - 1)
        def _():
            @pl.when(k == 2)
            def _():
                own.wait()

            @pl.when(k > 2)
            def _():
                push(k - 2).wait_send()

            stage[k % 2] = acc[...].astype(BF16)

            @pl.when(k == 0)
            def _():
                own.start()

            @pl.when(k > 0)
            def _():
                push(k).start()

            @pl.when(k == NDEV - 1)
            def _():
                push(NDEV - 2).wait_send()
                push(NDEV - 1).wait_send()
                for step in range(1, NDEV):
                    push(step).wait_recv()

    def blk_of(k, me_ref):
        m = me_ref[0]
        return (m | k) - (m & k)

    if by_rows:
        in_specs = [pl.BlockSpec((TK, blk), lambda k, kk, me_ref: (kk, blk_of(k, me_ref))),
                    pl.BlockSpec((TK, b.shape[1]), lambda k, kk, me_ref: (kk, 0))]
    else:
        in_specs = [pl.BlockSpec((TK, a.shape[1]), lambda k, kk, me_ref: (kk, 0)),
                    pl.BlockSpec((TK, blk), lambda k, kk, me_ref: (kk, blk_of(k, me_ref)))]
    grid_spec = pltpu.PrefetchScalarGridSpec(
        num_scalar_prefetch=1, grid=(NDEV, nk), in_specs=in_specs,
        out_specs=pl.BlockSpec(memory_space=pl.ANY),
        scratch_shapes=[pltpu.VMEM(shape, F32), pltpu.VMEM((2,) + shape, BF16), pltpu.SemaphoreType.DMA((NDEV,)),
                        pltpu.SemaphoreType.DMA((NDEV,)), pltpu.SemaphoreType.DMA])
    return _call(
        body, name=name, grid_spec=grid_spec,
        out_shape=jax.ShapeDtypeStruct((NDEV,) + shape, BF16),
        compiler_params=_params("arbitrary", "arbitrary"),
    )(me1, a, b)


def _chunk(TS, BR, k):
    return pl.BlockSpec((TS, BR), lambda i: (i, k))


def _halo_prev(TS, BR, k):
    return pl.BlockSpec((8, BR), lambda i: (jnp.maximum(i * (TS // 8) - 1, 0), k))


def _conv_a_fwd(proj, w8, BR):
    S = proj.shape[0]
    TS = _tile(S, 512)

    def body(ab_ref, ac_ref, ax_ref, ag_ref, hc_ref, hx_ref, w_ref, o_ref, ext):
        i = pl.program_id(0)
        u = ac_ref[...] * ax_ref[...]
        ext[0:8, :] = jnp.where(i > 0, hc_ref[...] * hx_ref[...], 0.0)
        ext[8:8 + TS, :] = u
        conv = ext[pl.ds(6, TS), :] * w_ref[0:1, :] + ext[pl.ds(7, TS), :] * w_ref[1:2, :] + u * w_ref[2:3, :]
        o_ref[...] = ab_ref[...] * conv * _silu(ag_ref[...])

    return _call(
        body, name="conv_a_fwd", grid=(S // TS,),
        in_specs=[_chunk(TS, BR, 0), _chunk(TS, BR, 1), _chunk(TS, BR, 2), _chunk(TS, BR, 3),
                  _halo_prev(TS, BR, 1), _halo_prev(TS, BR, 2), pl.BlockSpec((8, BR), lambda i: (0, 0))],
        out_specs=pl.BlockSpec((TS, BR), lambda i: (i, 0)),
        out_shape=jax.ShapeDtypeStruct((S, BR), F32),
        scratch_shapes=[pltpu.VMEM((TS + 8, BR), F32)],
        compiler_params=_params("arbitrary"),
    )(proj, proj, proj, proj, proj, proj, w8)


ATT_UNIT = 2048
ATT_LANES = 128


def _attn_mask():
    i = lax.broadcasted_iota(jnp.int32, (BLK, 2 * BLK), 0)
    j = lax.broadcasted_iota(jnp.int32, (BLK, 2 * BLK), 1)
    return jnp.logical_and(j >= i, j <= i + BLK), j >= BLK


def _attn_rows(n, r, dil):
    return pl.ds(n * BLK * dil + r, BLK, stride=dil) if dil > 1 else pl.ds(n * BLK, BLK)


def _attn_geometry(S, BR, dil):
    HD = BR // 8
    U = ATT_UNIT
    assert S % U == 0 and BR % ATT_LANES == 0 and ATT_LANES % HD == 0 and U % (BLK * dil) == 0
    return HD, U, S // U, U // (BLK * dil), ATT_LANES // HD, BR // ATT_LANES


def _attn_fwd(proj, bias, dil, BR):
    S = proj.shape[0]
    HD, U, NU, nbu, hpb, HBK = _attn_geometry(S, BR, dil)
    scale = HD ** -0.5
    combos = [(n, r) for n in range(nbu) for r in range(dil)]

    def body(q_ref, kc_ref, kp_ref, vc_ref, vp_ref, b_ref, o_ref, l_ref, sbuf, pbuf):
        m = pl.program_id(1)
        band, is_cur = _attn_mask()
        first = jnp.logical_and(band, jnp.logical_or(is_cur, m > 0))

        def keys(ref_c, ref_p, n, r):
            prev = ref_c[_attn_rows(n - 1, r, dil), :] if n > 0 else ref_p[_attn_rows(nbu - 1, r, dil), :]
            return jnp.concatenate([prev, ref_c[_attn_rows(n, r, dil), :]], axis=0).astype(BF16)

        for c, (n, r) in enumerate(combos):
            q = (q_ref[_attn_rows(n, r, dil), :] * scale).astype(BF16)
            kk = keys(kc_ref, kp_ref, n, r)
            for h in range(hpb):
                sl = slice(h * HD, (h + 1) * HD)
                sbuf[c * hpb + h] = _dot_nt(q[:, sl], kk[:, sl])
        for c, (n, r) in enumerate(combos):
            lses = []
            for h in range(hpb):
                s = jnp.where(band if n > 0 else first, sbuf[c * hpb + h] + b_ref[h], NEG)
                mx = jnp.max(s, axis=-1, keepdims=True)
                p = jnp.exp(s - mx)
                l = jnp.sum(p, axis=-1, keepdims=True)
                pbuf[c * hpb + h] = (p * (1.0 / l)).astype(BF16)
                lses.append(jnp.broadcast_to(mx + jnp.log(l), (BLK, HD)))
            l_ref[_attn_rows(n, r, dil), :] = jnp.concatenate(lses, axis=1)
        for c, (n, r) in enumerate(combos):
            vv = keys(vc_ref, vp_ref, n, r)
            o_ref[_attn_rows(n, r, dil), :] = jnp.concatenate(
                [_dot(pbuf[c * hpb + h], vv[:, h * HD:(h + 1) * HD]) for h in range(hpb)], axis=1)

    def cur(c):
        return pl.BlockSpec((U, ATT_LANES), lambda hb, m: (m, c * HBK + hb))

    def prev(c):
        return pl.BlockSpec((U, ATT_LANES), lambda hb, m: (jnp.maximum(m - 1, 0), c * HBK + hb))

    ospec = pl.BlockSpec((U, ATT_LANES), lambda hb, m: (m, hb))
    nhb = len(combos) * hpb
    return _call(
        body, name="attn_fwd_d%d" % dil, grid=(HBK, NU),
        in_specs=[cur(4), cur(5), prev(5), cur(6), prev(6), pl.BlockSpec((hpb, BLK, 2 * BLK), lambda hb, m: (hb, 0, 0))],
        out_specs=[ospec, ospec],
        out_shape=[jax.ShapeDtypeStruct((S, BR), F32)] * 2,
        scratch_shapes=[pltpu.VMEM((nhb, BLK, 2 * BLK), F32), pltpu.VMEM((nhb, BLK, 2 * BLK), BF16)],
        compiler_params=_params("arbitrary", "arbitrary"),
    )(proj, proj, proj, proj, proj, bias)


def _mix_weights(l1, l2, l3):
    m = jnp.maximum(jnp.maximum(l1, l2), l3)
    e1, e2, e3 = jnp.exp(l1 - m), jnp.exp(l2 - m), jnp.exp(l3 - m)
    inv = 1.0 / (e1 + e2 + e3)
    return e1 * inv, e2 * inv, e3 * inv


def _attn_mix_fwd(proj, os_, ls_, BR):
    S = proj.shape[0]
    TS = _tile(S, 512)

    def body(o1, o2, o3, l1, l2, l3, g_ref, y_ref):
        w1, w2, w3 = _mix_weights(l1[...], l2[...], l3[...])
        y_ref[...] = (w1 * o1[...] + w2 * o2[...] + w3 * o3[...]) * _silu(g_ref[...])

    spec = pl.BlockSpec((TS, BR), lambda i: (i, 0))
    return _call(
        body, name="attn_mix_fwd", grid=(S // TS,),
        in_specs=[spec] * 6 + [_chunk(TS, BR, 7)], out_specs=spec,
        out_shape=jax.ShapeDtypeStruct((S, BR), F32),
        compiler_params=_params("arbitrary"),
    )(*os_, *ls_, proj)


def _lru_gates(xc, wa, wx, ba, bx, sp):
    xb = xc.astype(BF16)
    r = _sigmoid(_dot(xb, wa) + ba)
    ig = _sigmoid(_dot(xb, wx) + bx)
    la = -LRU_C * r * sp
    a = jnp.exp(la)
    mult = jnp.sqrt(-_expm1(2.0 * la))
    return r, ig, a, mult


def _lru_fwd(proj, rows, wa, wx, BR):
    S = proj.shape[0]
    TS = _tile(S, 256)
    PAD = TS // 2

    def body(cx_ref, hx_ref, cg_ref, r_ref, wa_ref, wx_ref, y_ref, h_ref, ext, sa, sb, carry):
        i = pl.program_id(0)

        @pl.when(i == 0)
        def _():
            sa[0:PAD, :] = jnp.ones((PAD, BR), F32)
            sb[0:PAD, :] = jnp.zeros((PAD, BR), F32)
            carry[...] = jnp.zeros_like(carry)

        cx = cx_ref[...]
        ext[0:8, :] = jnp.where(i > 0, hx_ref[...], 0.0)
        ext[8:8 + TS, :] = cx
        xc = (ext[pl.ds(5, TS), :] * r_ref[0:1, :] + ext[pl.ds(6, TS), :] * r_ref[1:2, :]
              + ext[pl.ds(7, TS), :] * r_ref[2:3, :] + cx * r_ref[3:4, :] + r_ref[4:5, :])
        _, ig, a, mult = _lru_gates(xc, wa_ref[...], wx_ref[...], r_ref[5:6, :], r_ref[6:7, :], r_ref[7:8, :])
        sa[PAD:PAD + TS, :] = a
        sb[PAD:PAD + TS, :] = mult * ig * xc
        d = 1
        while d < TS:
            A, B = sa[PAD:PAD + TS, :], sb[PAD:PAD + TS, :]
            As, Bs = sa[pl.ds(PAD - d, TS), :], sb[pl.ds(PAD - d, TS), :]
            sb[PAD:PAD + TS, :] = A * Bs + B
            sa[PAD:PAD + TS, :] = A * As
            d *= 2
        h = sb[PAD:PAD + TS, :] + sa[PAD:PAD + TS, :] * carry[0:1, :]
        carry[0:1, :] = h[TS - 1:TS, :]
        h_ref[...] = h
        y_ref[...] = h * _silu(cg_ref[...])

    full = pl.BlockSpec((BR, BR), lambda i: (0, 0))
    spec = pl.BlockSpec((TS, BR), lambda i: (i, 0))
    return _call(
        body, name="lru_fwd", grid=(S // TS,),
        in_specs=[_chunk(TS, BR, 8), _halo_prev(TS, BR, 8), _chunk(TS, BR, 9), pl.BlockSpec((8, BR), lambda i: (0, 0)), full, full],
        out_specs=[spec, spec],
        out_shape=[jax.ShapeDtypeStruct((S, BR), F32)] * 2,
        scratch_shapes=[pltpu.VMEM((TS + 8, BR), F32), pltpu.VMEM((PAD + TS, BR), F32), pltpu.VMEM((PAD + TS, BR), F32),
                        pltpu.VMEM((8, BR), F32)],
        compiler_params=_params("arbitrary"),
    )(proj, proj, proj, rows, wa, wx)


def _s5_fwd(proj, pw_re, pw_im, bbre, bbim, cre, cim, rows, wg, BR):
    S = proj.shape[0]
    TS, W = pw_re.shape
    PAD = TS // 2
    CB = 8 * S5_CH
    SB = 8 * S5_STATE
    nblk = BR // CB

    def body(u_ref, dg_ref, pr_ref, pi_ref, bbre_ref, bbim_ref, cre_ref, cim_ref, r_ref, wg_ref,
             y_ref, xre_ref, xim_ref, sre, sim, ypre_s, carry):
        i = pl.program_id(0)

        @pl.when(i == 0)
        def _():
            sre[0:PAD, :] = jnp.zeros((PAD, SB), F32)
            sim[0:PAD, :] = jnp.zeros((PAD, SB), F32)
            carry[...] = jnp.zeros_like(carry)

        for m in range(nblk):
            cs, ws = slice(m * CB, (m + 1) * CB), slice(m * SB, (m + 1) * SB)
            ub = u_ref[:, cs].astype(BF16)
            sre[PAD:PAD + TS, :] = _dot(ub, bbre_ref[cs, :])
            sim[PAD:PAD + TS, :] = _dot(ub, bbim_ref[cs, :])
            d = 1
            while d < TS:
                ar, ai = pr_ref[d - 1:d, ws], pi_ref[d - 1:d, ws]
                xr, xi = sre[PAD:PAD + TS, :], sim[PAD:PAD + TS, :]
                sr, si = sre[pl.ds(PAD - d, TS), :], sim[pl.ds(PAD - d, TS), :]
                sre[PAD:PAD + TS, :] = xr + ar * sr - ai * si
                sim[PAD:PAD + TS, :] = xi + ar * si + ai * sr
                d *= 2
            cr, ci = carry[0:1, ws], carry[1:2, ws]
            pr, pi = pr_ref[:, ws], pi_ref[:, ws]
            xre = sre[PAD:PAD + TS, :] + pr * cr - pi * ci
            xim = sim[PAD:PAD + TS, :] + pr * ci + pi * cr
            carry[0:1, ws] = xre[TS - 1:TS, :]
            carry[1:2, ws] = xim[TS - 1:TS, :]
            xre_ref[:, ws] = xre
            xim_ref[:, ws] = xim
            ypre_s[:, cs] = _dot(xre.astype(BF16), cre_ref[ws, :]) - _dot(xim.astype(BF16), cim_ref[ws, :])
        dg = dg_ref[...]
        yg = _gelu(ypre_s[...] + r_ref[0:1, :] * u_ref[...])
        s = _sigmoid(_dot(yg.astype(BF16), wg_ref[...]) + r_ref[1:2, :])
        y_ref[...] = yg * s * _silu(dg)

    def const(shape):
        return pl.BlockSpec(shape, lambda i: (0, 0))

    return _call(
        body, name="s5_fwd", grid=(S // TS,),
        in_specs=[_chunk(TS, BR, 10), _chunk(TS, BR, 11), const((TS, W)), const((TS, W)), const((BR, SB)), const((BR, SB)),
                  const((W, CB)), const((W, CB)), const((8, BR)), const((BR, BR))],
        out_specs=[pl.BlockSpec((TS, BR), lambda i: (i, 0)), pl.BlockSpec((TS, W), lambda i: (i, 0)), pl.BlockSpec((TS, W), lambda i: (i, 0))],
        out_shape=[jax.ShapeDtypeStruct((S, BR), F32), jax.ShapeDtypeStruct((S, W), F32), jax.ShapeDtypeStruct((S, W), F32)],
        scratch_shapes=[pltpu.VMEM((PAD + TS, SB), F32), pltpu.VMEM((PAD + TS, SB), F32), pltpu.VMEM((TS, BR), F32), pltpu.VMEM((8, W), F32)],
        compiler_params=_params("arbitrary"),
    )(proj, proj, pw_re, pw_im, bbre, bbim, cre, cim, rows, wg)


def _out_ln(ys, w_out, x, rows):
    S, D = x.shape
    BR = D // 4
    TS = _tile(S, 256)

    def body(ya, yb, yc, yd, w_ref, x_ref, r_ref, o_ref, y_ref, cat_ref):
        cat = jnp.concatenate([ya[...], yb[...], yc[...], yd[...]], axis=1).astype(BF16)
        y = _dot(cat, w_ref[...])
        z = ALPHA * x_ref[...] + r_ref[0:1, :] * y
        mu = jnp.mean(z, axis=-1, keepdims=True)
        zc = z - mu
        var = jnp.mean(zc * zc, axis=-1, keepdims=True)
        o_ref[...] = zc * lax.rsqrt(var + LN_EPS) * r_ref[1:2, :] + r_ref[2:3, :]
        y_ref[...] = y
        cat_ref[...] = cat

    yspec = pl.BlockSpec((TS, BR), lambda i: (i, 0))
    spec = pl.BlockSpec((TS, D), lambda i: (i, 0))
    return _call(
        body, name="out_ln", grid=(S // TS,),
        in_specs=[yspec] * 4 + [pl.BlockSpec((D, D), lambda i: (0, 0)), spec, pl.BlockSpec((8, D), lambda i: (0, 0))],
        out_specs=[spec, spec, spec],
        out_shape=[jax.ShapeDtypeStruct((S, D), F32), jax.ShapeDtypeStruct((S, D), F32), jax.ShapeDtypeStruct((S, D), BF16)],
        compiler_params=_params("arbitrary"),
    )(*ys, w_out, x, rows)


def _loss_grad(x, target):
    S, D = x.shape
    TS = _tile(S, 512)

    def body(x_ref, t_ref, g_ref, l_ref):
        @pl.when(pl.program_id(0) == 0)
        def _():
            l_ref[...] = jnp.zeros_like(l_ref)
        diff = x_ref[...] - t_ref[...]
        g_ref[...] = diff * (1.0 / D)
        l_ref[...] += jnp.sum(_colsum(diff * diff), axis=1, keepdims=True) * (0.5 / D)

    spec = pl.BlockSpec((TS, D), lambda i: (i, 0))
    g, l = _call(
        body, name="loss_grad", grid=(S // TS,),
        in_specs=[spec, spec], out_specs=[spec, pl.BlockSpec((1, 1), lambda i: (0, 0))],
        out_shape=[jax.ShapeDtypeStruct((S, D), F32), jax.ShapeDtypeStruct((1, 1), F32)],
        compiler_params=_params("arbitrary"),
    )(x, target)
    return g, l[0, 0]


def _ln_bwd(dout, x, y, rows, w_out):
    S, D = x.shape
    TS = _tile(S, 256)

    def body(do_ref, x_ref, y_ref, r_ref, w_ref, dxr_ref, dyb_ref, dcat_ref, acc_ref):
        @pl.when(pl.program_id(0) == 0)
        def _():
            acc_ref[...] = jnp.zeros_like(acc_ref)
        g1, lg = r_ref[0:1, :], r_ref[1:2, :]
        yv = y_ref[...]
        z = ALPHA * x_ref[...] + g1 * yv
        mu = jnp.mean(z, axis=-1, keepdims=True)
        zc = z - mu
        var = jnp.mean(zc * zc, axis=-1, keepdims=True)
        rstd = lax.rsqrt(var + LN_EPS)
        xhat = zc * rstd
        do = do_ref[...]
        dxh = do * lg
        dz = rstd * (dxh - jnp.mean(dxh, axis=-1, keepdims=True) - xhat * jnp.mean(dxh * xhat, axis=-1, keepdims=True))
        dxr_ref[...] = ALPHA * dz
        dyb = (g1 * dz).astype(BF16)
        dyb_ref[...] = dyb
        dcat_ref[...] = _dot_nt(dyb, w_ref[...])
        acc_ref[0:1, :] += _colsum(do * xhat)
        acc_ref[1:2, :] += _colsum(do)
        acc_ref[2:3, :] += _colsum(dz * yv)

    spec = pl.BlockSpec((TS, D), lambda i: (i, 0))
    rspec = pl.BlockSpec((8, D), lambda i: (0, 0))
    return _call(
        body, name="ln_bwd", grid=(S // TS,),
        in_specs=[spec, spec, spec, rspec, pl.BlockSpec((D, D), lambda i: (0, 0))],
        out_specs=[spec, spec, spec, rspec],
        out_shape=[jax.ShapeDtypeStruct((S, D), F32), jax.ShapeDtypeStruct((S, D), BF16), jax.ShapeDtypeStruct((S, D), F32),
                   jax.ShapeDtypeStruct((8, D), F32)],
        compiler_params=_params("arbitrary"),
    )(dout, x, y, rows, w_out)


def _dh(dproj, wg, dxres, x, mod):
    S, D = x.shape
    NS = wg.shape[2]
    TS = _tile(S, 512)

    def body(dp_ref, w_ref, dxr_ref, x_ref, mod_ref, dx_ref, sum_ref, acc):
        i, j = pl.program_id(0), pl.program_id(1)

        @pl.when(jnp.logical_and(i == 0, j == 0))
        def _():
            sum_ref[...] = jnp.zeros_like(sum_ref)

        @pl.when(j == 0)
        def _():
            acc[...] = jnp.zeros_like(acc)

        acc[...] += _dot_nt(dp_ref[...], w_ref[0])

        @pl.when(j == NDEV - 1)
        def _():
            dh = acc[...]
            dx_ref[...] = dxr_ref[...] + dh * mod_ref[0:1, :]
            sum_ref[0:1, :] += _colsum(dh * x_ref[...])
            sum_ref[1:2, :] += _colsum(dh)

    spec = pl.BlockSpec((TS, D), lambda i, j: (i, 0))
    rspec = pl.BlockSpec((8, D), lambda i, j: (0, 0))
    return _call(
        body, name="dh", grid=(S // TS, NDEV),
        in_specs=[pl.BlockSpec((TS, NS), lambda i, j: (i, j)), pl.BlockSpec((1, D, NS), lambda i, j: (j, 0, 0)), spec, spec, rspec],
        out_specs=[spec, rspec],
        out_shape=[jax.ShapeDtypeStruct((S, D), F32), jax.ShapeDtypeStruct((8, D), F32)],
        scratch_shapes=[pltpu.VMEM((TS, D), F32)],
        compiler_params=_params("arbitrary", "arbitrary"),
    )(dproj, wg, dxres, x, mod)


def _halo_next(S, TS, BR, k):
    last = S // 8 - 1
    return pl.BlockSpec((8, BR), lambda i: (jnp.minimum((i + 1) * (TS // 8), last), k))


def _conv_a_bwd(proj, dcat, w8, BR):
    S = proj.shape[0]
    TS = _tile(S, 512)
    nt = S // TS

    def body(ab_ref, ac_ref, ax_ref, ag_ref, dy_ref, hc_ref, hx_ref, nab_ref, nag_ref, ndy_ref, w_ref, dp_ref, dw_ref, ext, dext):
        i = pl.program_id(0)

        @pl.when(i == 0)
        def _():
            dw_ref[...] = jnp.zeros_like(dw_ref)

        w0, w1, w2 = w_ref[0:1, :], w_ref[1:2, :], w_ref[2:3, :]
        ab, ac, ax, ag, dy = ab_ref[...], ac_ref[...], ax_ref[...], ag_ref[...], dy_ref[...]
        u = ac * ax
        ext[0:8, :] = jnp.where(i > 0, hc_ref[...] * hx_ref[...], 0.0)
        ext[8:8 + TS, :] = u
        u1, u2 = ext[pl.ds(7, TS), :], ext[pl.ds(6, TS), :]
        conv = u2 * w0 + u1 * w1 + u * w2
        sg = _silu(ag)
        dconv = dy * ab * sg
        dext[0:TS, :] = dconv
        dext[TS:TS + 8, :] = jnp.where(i < nt - 1, ndy_ref[...] * nab_ref[...] * _silu(nag_ref[...]), 0.0)
        du = w2 * dconv + w1 * dext[pl.ds(1, TS), :] + w0 * dext[pl.ds(2, TS), :]
        dp_ref[:, 0:BR] = (dy * conv * sg).astype(BF16)
        dp_ref[:, BR:2 * BR] = (du * ax).astype(BF16)
        dp_ref[:, 2 * BR:3 * BR] = (du * ac).astype(BF16)
        dp_ref[:, 3 * BR:4 * BR] = (dy * ab * conv * _dsilu(ag)).astype(BF16)
        dw_ref[0:1, :] += _colsum(dconv * u2)
        dw_ref[1:2, :] += _colsum(dconv * u1)
        dw_ref[2:3, :] += _colsum(dconv * u)

    rspec = pl.BlockSpec((8, BR), lambda i: (0, 0))
    return _call(
        body, name="conv_a_bwd", grid=(nt,),
        in_specs=[_chunk(TS, BR, 0), _chunk(TS, BR, 1), _chunk(TS, BR, 2), _chunk(TS, BR, 3), _chunk(TS, BR, 0),
                  _halo_prev(TS, BR, 1), _halo_prev(TS, BR, 2), _halo_next(S, TS, BR, 0), _halo_next(S, TS, BR, 3),
                  _halo_next(S, TS, BR, 0), rspec],
        out_specs=[pl.BlockSpec((TS, 4 * BR), lambda i: (i, 0)), rspec],
        out_shape=[jax.ShapeDtypeStruct((S, 12 * BR), BF16), jax.ShapeDtypeStruct((8, BR), F32)],
        scratch_shapes=[pltpu.VMEM((TS + 8, BR), F32), pltpu.VMEM((TS + 8, BR), F32)],
        compiler_params=_params("arbitrary"),
    )(proj, proj, proj, proj, dcat, proj, proj, proj, proj, dcat, w8)


def _attn_mix_bwd(proj, dcat, os_, ls_, head_ones, BR):
    S = proj.shape[0]
    TS = _tile(S, 512)

    def body(dy_ref, g_ref, o1, o2, o3, l1, l2, l3, ones_ref, d1, d2, d3, e1, e2, e3, dg_ref):
        w1, w2, w3 = _mix_weights(l1[...], l2[...], l3[...])
        mix = w1 * o1[...] + w2 * o2[...] + w3 * o3[...]
        g = g_ref[...]
        dy = dy_ref[...]
        dmix = dy * _silu(g)
        dg_ref[...] = dy * mix * _dsilu(g)
        t = jnp.dot(dmix * mix, ones_ref[...], preferred_element_type=F32, precision=lax.Precision.HIGHEST)
        d1[...] = w1 * dmix
        d2[...] = w2 * dmix
        d3[...] = w3 * dmix
        e1[...] = -w1 * t
        e2[...] = -w2 * t
        e3[...] = -w3 * t

    spec = pl.BlockSpec((TS, BR), lambda i: (i, 0))
    outs = _call(
        body, name="attn_mix_bwd", grid=(S // TS,),
        in_specs=[_chunk(TS, BR, 1), _chunk(TS, BR, 7)] + [spec] * 6 + [pl.BlockSpec((BR, BR), lambda i: (0, 0))],
        out_specs=[spec] * 7,
        out_shape=[jax.ShapeDtypeStruct((S, BR), F32)] * 7,
        compiler_params=_params("arbitrary"),
    )(dcat, proj, *os_, *ls_, head_ones)
    return outs[0:3], outs[3:6], outs[6]


def _attn_bwd(proj, do, e, lse, bias, dil, BR):
    S = proj.shape[0]
    HD, U, NU, nbu, hpb, HBK = _attn_geometry(S, BR, dil)
    scale = HD ** -0.5
    combos = [(n, r) for n in range(nbu) for r in range(dil)]

    def body(q_ref, kc_ref, kp_ref, vc_ref, vp_ref, do_ref, e_ref, l_ref, b_ref, dq_ref, dk_ref, dv_ref, db_ref,
             sbuf, dpbuf, pbuf, dsbuf, dkacc, dvacc, nxk, nxv, cark, carv):
        j = pl.program_id(1)
        mu = NU - 1 - j
        band, is_cur = _attn_mask()
        first = jnp.logical_and(band, jnp.logical_or(is_cur, mu > 0))

        @pl.when(j == 0)
        def _():
            cark[...] = jnp.zeros_like(cark)
            carv[...] = jnp.zeros_like(carv)
            db_ref[...] = jnp.zeros_like(db_ref)

        for ref in (dkacc, dvacc, nxk, nxv):
            ref[...] = jnp.zeros_like(ref)

        def keys(ref_c, ref_p, n, r):
            prev = ref_c[_attn_rows(n - 1, r, dil), :] if n > 0 else ref_p[_attn_rows(nbu - 1, r, dil), :]
            return jnp.concatenate([prev, ref_c[_attn_rows(n, r, dil), :]], axis=0).astype(BF16)

        for c, (n, r) in enumerate(combos):
            rows = _attn_rows(n, r, dil)
            q = (q_ref[rows, :] * scale).astype(BF16)
            dob = do_ref[rows, :].astype(BF16)
            kk, vv = keys(kc_ref, kp_ref, n, r), keys(vc_ref, vp_ref, n, r)
            for h in range(hpb):
                sl = slice(h * HD, (h + 1) * HD)
                sbuf[c * hpb + h] = _dot_nt(q[:, sl], kk[:, sl])
                dpbuf[c * hpb + h] = _dot_nt(dob[:, sl], vv[:, sl])
        dbs = [None] * hpb
        for c, (n, r) in enumerate(combos):
            rows = _attn_rows(n, r, dil)
            lse, ev = l_ref[rows, :], e_ref[rows, :]
            for h in range(hpb):
                one = slice(h * HD, h * HD + 1)
                s = jnp.where(band if n > 0 else first, sbuf[c * hpb + h] + b_ref[h], NEG)
                p = jnp.exp(s - lse[:, one])
                ds = p * (dpbuf[c * hpb + h] + ev[:, one])
                pbuf[c * hpb + h] = p.astype(BF16)
                dsbuf[c * hpb + h] = ds.astype(BF16)
                dbs[h] = ds if dbs[h] is None else dbs[h] + ds
        for h in range(hpb):
            db_ref[h] += dbs[h]
        for c, (n, r) in enumerate(combos):
            rows = _attn_rows(n, r, dil)
            q = (q_ref[rows, :] * scale).astype(BF16)
            dob = do_ref[rows, :].astype(BF16)
            kk = keys(kc_ref, kp_ref, n, r)
            dq, dkk, dvv = [], [], []
            for h in range(hpb):
                sl = slice(h * HD, (h + 1) * HD)
                dsb = dsbuf[c * hpb + h]
                dq.append(_dot(dsb, kk[:, sl]) * scale)
                dkk.append(_dot_tn(dsb, q[:, sl]))
                dvv.append(_dot_tn(pbuf[c * hpb + h], dob[:, sl]))
            dq_ref[rows, :] = jnp.concatenate(dq, axis=1)
            dkk, dvv = jnp.concatenate(dkk, axis=1), jnp.concatenate(dvv, axis=1)
            dkacc[rows, :] += dkk[BLK:2 * BLK, :]
            dvacc[rows, :] += dvv[BLK:2 * BLK, :]
            if n > 0:
                prow = _attn_rows(n - 1, r, dil)
                dkacc[prow, :] += dkk[0:BLK, :]
                dvacc[prow, :] += dvv[0:BLK, :]
            else:
                prow = _attn_rows(nbu - 1, r, dil)
                nxk[prow, :] += dkk[0:BLK, :]
                nxv[prow, :] += dvv[0:BLK, :]
        dk_ref[...] = dkacc[...] + cark[...]
        dv_ref[...] = dvacc[...] + carv[...]
        cark[...] = nxk[...]
        carv[...] = nxv[...]

    def cur(c):
        return pl.BlockSpec((U, ATT_LANES), lambda hb, j: (NU - 1 - j, c * HBK + hb))

    def prev(c):
        return pl.BlockSpec((U, ATT_LANES), lambda hb, j: (jnp.maximum(NU - 2 - j, 0), c * HBK + hb))

    own = pl.BlockSpec((U, ATT_LANES), lambda hb, j: (NU - 1 - j, hb))
    bspec = pl.BlockSpec((hpb, BLK, 2 * BLK), lambda hb, j: (hb, 0, 0))
    nhb = len(combos) * hpb
    unit = pltpu.VMEM((U, ATT_LANES), F32)
    return _call(
        body, name="attn_bwd_d%d" % dil, grid=(HBK, NU),
        in_specs=[cur(4), cur(5), prev(5), cur(6), prev(6), own, own, own, bspec],
        out_specs=[own, own, own, bspec],
        out_shape=[jax.ShapeDtypeStruct((S, BR), F32)] * 3 + [jax.ShapeDtypeStruct((8, BLK, 2 * BLK), F32)],
        scratch_shapes=[pltpu.VMEM((nhb, BLK, 2 * BLK), F32)] * 2 + [pltpu.VMEM((nhb, BLK, 2 * BLK), BF16)] * 2 + [unit] * 6,
        compiler_params=_params("arbitrary", "arbitrary"),
    )(proj, proj, proj, proj, proj, do, e, lse, bias)


def _attn_dsum(ds, dbg, dproj, BR):
    S = dbg.shape[0]
    TS = _tile(S, 512)

    def body(*refs):
        o_ref = refs[11]
        for k in range(3):
            o_ref[:, k * BR:(k + 1) * BR] = (refs[k][...] + refs[3 + k][...] + refs[6 + k][...]).astype(BF16)
        o_ref[:, 3 * BR:4 * BR] = refs[9][...].astype(BF16)

    spec = pl.BlockSpec((TS, BR), lambda i: (i, 0))
    return _call(
        body, name="attn_dsum", grid=(S // TS,),
        in_specs=[spec] * 10 + [pl.BlockSpec(memory_space=pl.ANY)],
        out_specs=pl.BlockSpec((TS, 4 * BR), lambda i: (i, 1)),
        out_shape=jax.ShapeDtypeStruct(dproj.shape, BF16),
        input_output_aliases={10: 0},
        compiler_params=_params("arbitrary"),
    )(*[t for trip in ds for t in trip], dbg, dproj)


def _lru_bwd(proj, h, dcat, rows, wa, wx, dproj, BR):
    S = proj.shape[0]
    TS = _tile(S, 256)
    PAD = TS // 2
    nt = S // TS

    def body(cx_ref, hx_ref, cg_ref, h_ref, hh_ref, dy_ref, r_ref, wa_ref, wx_ref, _,
             dp_ref, sum_ref, dwa_ref, dwx_ref, ext, hext, aext, dext, sa, sb, carry):
        i = pl.program_id(0)
        ti = nt - 1 - i

        @pl.when(i == 0)
        def _():
            sa[TS:TS + PAD, :] = jnp.ones((PAD, BR), F32)
            sb[TS:TS + PAD, :] = jnp.zeros((PAD, BR), F32)
            carry[...] = jnp.zeros_like(carry)
            dext[TS:TS + 8, :] = jnp.zeros((8, BR), F32)
            sum_ref[...] = jnp.zeros_like(sum_ref)
            dwa_ref[...] = jnp.zeros_like(dwa_ref)
            dwx_ref[...] = jnp.zeros_like(dwx_ref)

        w0, w1, w2, w3 = r_ref[0:1, :], r_ref[1:2, :], r_ref[2:3, :], r_ref[3:4, :]
        sp = r_ref[7:8, :]
        cx = cx_ref[...]
        ext[0:8, :] = jnp.where(ti > 0, hx_ref[...], 0.0)
        ext[8:8 + TS, :] = cx
        x3, x2, x1 = ext[pl.ds(5, TS), :], ext[pl.ds(6, TS), :], ext[pl.ds(7, TS), :]
        xc = x3 * w0 + x2 * w1 + x1 * w2 + cx * w3 + r_ref[4:5, :]
        wa_, wx_ = wa_ref[...], wx_ref[...]
        r, ig, a, mult = _lru_gates(xc, wa_, wx_, r_ref[5:6, :], r_ref[6:7, :], sp)
        cg, dy, hv = cg_ref[...], dy_ref[...], h_ref[...]
        dp_ref[:, BR:2 * BR] = (dy * hv * _dsilu(cg)).astype(BF16)
        aext[0:TS, :] = a
        aext[TS:TS + 8, :] = jnp.broadcast_to(carry[0:1, :], (8, BR))
        sa[0:TS, :] = aext[pl.ds(1, TS), :]
        sb[0:TS, :] = dy * _silu(cg)
        d = 1
        while d < TS:
            A, B = sa[0:TS, :], sb[0:TS, :]
            As, Bs = sa[pl.ds(d, TS), :], sb[pl.ds(d, TS), :]
            sb[0:TS, :] = B + A * Bs
            sa[0:TS, :] = A * As
            d *= 2
        gh = sb[0:TS, :] + sa[0:TS, :] * carry[1:2, :]
        carry[0:1, :] = a[0:1, :]
        carry[1:2, :] = gh[0:1, :]
        hext[0:8, :] = jnp.where(ti > 0, hh_ref[...], 0.0)
        hext[8:8 + TS, :] = hv
        da = gh * hext[pl.ds(7, TS), :]
        dmult = gh * ig * xc
        dig = gh * mult * xc
        dxc = gh * mult * ig
        dla = da * a - dmult * a * a / mult
        dpr = dla * (-LRU_C) * sp * r * (1.0 - r)
        dpi = dig * ig * (1.0 - ig)
        xb, dprb, dpib = xc.astype(BF16), dpr.astype(BF16), dpi.astype(BF16)
        dwa_ref[...] += _dot_tn(xb, dprb)
        dwx_ref[...] += _dot_tn(xb, dpib)
        dxc = dxc + _dot_nt(dprb, wa_) + _dot_nt(dpib, wx_)
        sum_ref[0:1, :] += _colsum(dxc * x3)
        sum_ref[1:2, :] += _colsum(dxc * x2)
        sum_ref[2:3, :] += _colsum(dxc * x1)
        sum_ref[3:4, :] += _colsum(dxc * cx)
        sum_ref[4:5, :] += _colsum(dxc)
        sum_ref[5:6, :] += _colsum(dpr)
        sum_ref[6:7, :] += _colsum(dpi)
        sum_ref[7:8, :] += _colsum(dla * (-LRU_C) * r)
        dext[0:TS, :] = dxc
        dcx = w3 * dxc + w2 * dext[pl.ds(1, TS), :] + w1 * dext[pl.ds(2, TS), :] + w0 * dext[pl.ds(3, TS), :]
        dext[TS:TS + 8, :] = dxc[0:8, :]
        dp_ref[:, 0:BR] = dcx.astype(BF16)

    def rev(k):
        return pl.BlockSpec((TS, BR), lambda i: (nt - 1 - i, k))

    def rev_halo(k):
        return pl.BlockSpec((8, BR), lambda i: (jnp.maximum((nt - 1 - i) * (TS // 8) - 1, 0), k))

    full = pl.BlockSpec((BR, BR), lambda i: (0, 0))
    rspec = pl.BlockSpec((8, BR), lambda i: (0, 0))
    return _call(
        body, name="lru_bwd", grid=(nt,),
        in_specs=[rev(8), rev_halo(8), rev(9), rev(0), rev_halo(0), rev(2), rspec, full, full, pl.BlockSpec(memory_space=pl.ANY)],
        out_specs=[pl.BlockSpec((TS, 2 * BR), lambda i: (nt - 1 - i, 4)), rspec, full, full],
        input_output_aliases={9: 0},
        out_shape=[jax.ShapeDtypeStruct(dproj.shape, BF16), jax.ShapeDtypeStruct((8, BR), F32),
                   jax.ShapeDtypeStruct((BR, BR), F32), jax.ShapeDtypeStruct((BR, BR), F32)],
        scratch_shapes=[pltpu.VMEM((TS + 8, BR), F32)] * 4 + [pltpu.VMEM((TS + PAD, BR), F32)] * 2 + [pltpu.VMEM((8, BR), F32)],
        compiler_params=_params("arbitrary"),
    )(proj, proj, proj, h, h, dcat, rows, wa, wx, dproj)


def _s5_bwd(proj, dcat, xre_all, xim_all, qw_re, qw_im, bbre, bbim, cre, cim, rows, wg, dproj, BR):
    S = proj.shape[0]
    TS, W = qw_re.shape
    PAD = TS // 2
    nt = S // TS
    CB = 8 * S5_CH
    SB = 8 * S5_STATE
    nblk = BR // CB

    def body(u_ref, dg_ref, dy_ref, xre_ref, xim_ref, hre_ref, him_ref, qr_ref, qi_ref, bbre_ref, bbim_ref, cre_ref, cim_ref,
             r_ref, wg_ref, _, dp_ref, sum_ref, dwg_ref, da_ref, dbbre_ref, dbbim_ref, dcre_ref, dcim_ref,
             sre, sim, ere, eim, ypre_s, dyp_s, carry):
        i = pl.program_id(0)
        ti = nt - 1 - i

        @pl.when(i == 0)
        def _():
            sre[TS:TS + PAD, :] = jnp.zeros((PAD, SB), F32)
            sim[TS:TS + PAD, :] = jnp.zeros((PAD, SB), F32)
            carry[...] = jnp.zeros_like(carry)
            for ref in (sum_ref, dwg_ref, da_ref, dbbre_ref, dbbim_ref, dcre_ref, dcim_ref):
                ref[...] = jnp.zeros_like(ref)

        u, dg, dy = u_ref[...], dg_ref[...], dy_ref[...]
        for m in range(nblk):
            cs, ws = slice(m * CB, (m + 1) * CB), slice(m * SB, (m + 1) * SB)
            ypre_s[:, cs] = (_dot(xre_ref[:, ws].astype(BF16), cre_ref[ws, :]) - _dot(xim_ref[:, ws].astype(BF16), cim_ref[ws, :]))
        ypre = ypre_s[...] + r_ref[0:1, :] * u
        yg = _gelu(ypre)
        wg_ = wg_ref[...]
        s = _sigmoid(_dot(yg.astype(BF16), wg_) + r_ref[1:2, :])
        dgl = dy * _silu(dg)
        dp_ref[:, BR:2 * BR] = (dy * yg * s * _dsilu(dg)).astype(BF16)
        dps = dgl * yg * s * (1.0 - s)
        dpsb = dps.astype(BF16)
        sum_ref[1:2, :] += _colsum(dps)
        dwg_ref[...] += _dot_tn(yg.astype(BF16), dpsb)
        dyp = (dgl * s + _dot_nt(dpsb, wg_)) * _dgelu(ypre)
        sum_ref[0:1, :] += _colsum(dyp * u)
        dyp_s[...] = dyp
        for m in range(nblk):
            cs, ws = slice(m * CB, (m + 1) * CB), slice(m * SB, (m + 1) * SB)
            dypb = dyp_s[:, cs].astype(BF16)
            xre, xim = xre_ref[:, ws], xim_ref[:, ws]
            dcre_ref[ws, :] += _dot_tn(xre.astype(BF16), dypb)
            dcim_ref[ws, :] -= _dot_tn(xim.astype(BF16), dypb)
            sre[0:TS, :] = _dot_nt(dypb, cre_ref[ws, :])
            sim[0:TS, :] = -_dot_nt(dypb, cim_ref[ws, :])
            d = 1
            while d < TS:
                ar, ai = qr_ref[TS - d:TS - d + 1, ws], -qi_ref[TS - d:TS - d + 1, ws]
                xr, xi = sre[0:TS, :], sim[0:TS, :]
                sr, si = sre[pl.ds(d, TS), :], sim[pl.ds(d, TS), :]
                sre[0:TS, :] = xr + ar * sr - ai * si
                sim[0:TS, :] = xi + ar * si + ai * sr
                d *= 2
            cr, ci = carry[0:1, ws], carry[1:2, ws]
            qr, qi = qr_ref[:, ws], -qi_ref[:, ws]
            gre = sre[0:TS, :] + qr * cr - qi * ci
            gim = sim[0:TS, :] + qr * ci + qi * cr
            carry[0:1, ws] = gre[0:1, :]
            carry[1:2, ws] = gim[0:1, :]
            ere[0:8, :] = jnp.where(ti > 0, hre_ref[:, ws], 0.0)
            eim[0:8, :] = jnp.where(ti > 0, him_ref[:, ws], 0.0)
            ere[8:8 + TS, :] = xre
            eim[8:8 + TS, :] = xim
            xpr, xpi = ere[pl.ds(7, TS), :], eim[pl.ds(7, TS), :]
            da_ref[0:1, ws] += _colsum(gre * xpr + gim * xpi)
            da_ref[1:2, ws] += _colsum(gim * xpr - gre * xpi)
            greb, gimb = gre.astype(BF16), gim.astype(BF16)
            ub = u_ref[:, cs].astype(BF16)
            dbbre_ref[cs, :] += _dot_tn(ub, greb)
            dbbim_ref[cs, :] += _dot_tn(ub, gimb)
            du = _dot_nt(greb, bbre_ref[cs, :]) + _dot_nt(gimb, bbim_ref[cs, :]) + r_ref[0:1, cs] * dyp_s[:, cs]
            dp_ref[:, cs] = du.astype(BF16)

    def rev(width, k):
        return pl.BlockSpec((TS, width), lambda i: (nt - 1 - i, k))

    def rev_halo(width):
        return pl.BlockSpec((8, width), lambda i: (jnp.maximum((nt - 1 - i) * (TS // 8) - 1, 0), 0))

    def const(shape):
        return pl.BlockSpec(shape, lambda i: (0, 0))

    return _call(
        body, name="s5_bwd", grid=(nt,),
        in_specs=[rev(BR, 10), rev(BR, 11), rev(BR, 3), rev(W, 0), rev(W, 0), rev_halo(W), rev_halo(W),
                  const((TS, W)), const((TS, W)), const((BR, SB)), const((BR, SB)), const((W, CB)), const((W, CB)),
                  const((8, BR)), const((BR, BR)), pl.BlockSpec(memory_space=pl.ANY)],
        input_output_aliases={15: 0},
        out_specs=[pl.BlockSpec((TS, 2 * BR), lambda i: (nt - 1 - i, 5)), const((8, BR)), const((BR, BR)), const((8, W)),
                   const((BR, SB)), const((BR, SB)), const((W, CB)), const((W, CB))],
        out_shape=[jax.ShapeDtypeStruct(dproj.shape, BF16), jax.ShapeDtypeStruct((8, BR), F32), jax.ShapeDtypeStruct((BR, BR), F32),
                   jax.ShapeDtypeStruct((8, W), F32), jax.ShapeDtypeStruct((BR, SB), F32), jax.ShapeDtypeStruct((BR, SB), F32),
                   jax.ShapeDtypeStruct((W, CB), F32), jax.ShapeDtypeStruct((W, CB), F32)],
        scratch_shapes=[pltpu.VMEM((TS + PAD, SB), F32)] * 2 + [pltpu.VMEM((TS + 8, SB), F32)] * 2
        + [pltpu.VMEM((TS, BR), F32)] * 2 + [pltpu.VMEM((8, W), F32)],
        compiler_params=_params("arbitrary"),
    )(proj, proj, dcat, xre_all, xim_all, xre_all, xim_all, qw_re, qw_im, bbre, bbim, cre, cim, rows, wg, dproj)


_WEIGHTS = ['rel_bias', 'w_ada', 'b_ada', 'w_in', 'conv_a', 'conv_c', 'conv_c_b', 'lru_wa', 'lru_ba', 'lru_wx', 'lru_bx',
            'lru_lambda', 's5_lam_re', 's5_lam_im', 's5_log_dt', 's5_b_re', 's5_b_im', 's5_c_re', 's5_c_im', 's5_d',
            's5_w_glu', 's5_b_glu', 'w_out', 'ln_g', 'ln_b']
_LAYER_SMALL = ['conv_a', 'conv_c', 'conv_c_b', 'lru_wa', 'lru_ba', 'lru_wx', 'lru_bx', 'lru_lambda', 's5_lam_re', 's5_lam_im',
                's5_log_dt', 's5_b_re', 's5_b_im', 's5_c_re', 's5_c_im', 's5_d', 's5_b_glu']
_SMALL = ['rel_bias'] + _LAYER_SMALL + ['ln_g', 'ln_b']


def _t5_bucket(dist):
    max_exact = REL_BUCKETS // 2
    nf = jnp.maximum(dist, 1).astype(F32)
    large = max_exact + (jnp.log(nf / max_exact) / math.log(REL_MAX_DIST / max_exact)
                         * (REL_BUCKETS - max_exact)).astype(jnp.int32)
    large = jnp.minimum(large, REL_BUCKETS - 1)
    return jnp.where(dist < max_exact, dist, large)


def _bias_tables(rel_bias):
    i = jnp.arange(BLK)[:, None]
    j = jnp.arange(2 * BLK)[None, :]
    delta = i + BLK - j
    out = []
    for window, dil in DILATIONS:
        bucket = _t5_bucket(jnp.clip(delta, 0, window // dil) * dil)
        onehot = (bucket[:, :, None] == jnp.arange(REL_BUCKETS)[None, None, :]).astype(F32)
        out.append(jnp.einsum('ijb,bh->hij', onehot, rel_bias, precision=lax.Precision.HIGHEST))
    return jnp.stack(out)


def _prep_layer(q):
    eye8 = jnp.eye(8, dtype=F32)
    G = q['s5_lam_re'].shape[0]
    nblk = G // 8

    def block_diag(w):
        hd = w.shape[1]
        return (w[:, :, None, :] * eye8[:, None, :, None]).reshape(8 * hd, 8 * hd)

    def compact_b(bb):
        t = jnp.transpose(bb.reshape(nblk, 8, S5_STATE, S5_CH), (0, 1, 3, 2))
        return (t[:, :, :, None, :] * eye8[None, :, None, :, None]).reshape(G * S5_CH, 8 * S5_STATE)

    def compact_c(cc):
        t = jnp.transpose(cc.reshape(nblk, 8, S5_CH, S5_STATE), (0, 1, 3, 2))
        return (t[:, :, :, None, :] * eye8[None, :, None, :, None]).reshape(G * S5_STATE, 8 * S5_CH)

    lam_re, lam_im = q['s5_lam_re'], q['s5_lam_im']
    dt = jnp.exp(q['s5_log_dt'])[:, None]
    mag = jnp.exp(lam_re * dt)
    ab_re = mag * jnp.cos(lam_im * dt)
    ab_im = mag * jnp.sin(lam_im * dt)
    den = lam_re * lam_re + lam_im * lam_im
    f_re = ((ab_re - 1.0) * lam_re + ab_im * lam_im) / den
    f_im = (ab_im * lam_re - (ab_re - 1.0) * lam_im) / den
    b_re, b_im = q['s5_b_re'], q['s5_b_im']
    bb_re = f_re[..., None] * b_re - f_im[..., None] * b_im
    bb_im = f_re[..., None] * b_im + f_im[..., None] * b_re
    return dict(
        conv_a=q['conv_a'], conv_c=q['conv_c'], conv_c_b=q['conv_c_b'], lru_ba=q['lru_ba'], lru_bx=q['lru_bx'],
        sp=jax.nn.softplus(-q['lru_lambda']), wa=block_diag(q['lru_wa']), wx=block_diag(q['lru_wx']),
        ar=ab_re.reshape(-1), ai=ab_im.reshape(-1), bbre=compact_b(bb_re), bbim=compact_b(bb_im),
        cre=compact_c(q['s5_c_re']), cim=compact_c(q['s5_c_im']), s5_d=q['s5_d'], s5_b_glu=q['s5_b_glu'])


def _powers(ar, ai, T):
    pr, pi = ar[None, :], ai[None, :]
    while pr.shape[0] < T:
        lr, li = pr[-1:], pi[-1:]
        pr, pi = jnp.concatenate([pr, pr * lr - pi * li]), jnp.concatenate([pi, pr * li + pi * lr])
    return pr[:T], pi[:T]


def _rows8(vecs, width):
    rows = [v.reshape(1, width).astype(F32) for v in vecs]
    return jnp.concatenate(rows + [jnp.zeros((8 - len(rows), width), F32)], axis=0)


def kernel(x, c, rel_bias, w_ada, b_ada, w_in, conv_a, conv_c, conv_c_b, lru_wa, lru_ba, lru_wx, lru_bx, lru_lambda, s5_lam_re, s5_lam_im, s5_log_dt, s5_b_re, s5_b_im, s5_c_re, s5_c_im, s5_d, s5_w_glu, s5_b_glu, w_out, ln_g, ln_b, loss_target, m_rel_bias, m_w_ada, m_b_ada, m_w_in, m_conv_a, m_conv_c, m_conv_c_b, m_lru_wa, m_lru_ba, m_lru_wx, m_lru_bx, m_lru_lambda, m_s5_lam_re, m_s5_lam_im, m_s5_log_dt, m_s5_b_re, m_s5_b_im, m_s5_c_re, m_s5_c_im, m_s5_d, m_s5_w_glu, m_s5_b_glu, m_w_out, m_ln_g, m_ln_b, v_rel_bias, v_w_ada, v_b_ada, v_w_in, v_conv_a, v_conv_c, v_conv_c_b, v_lru_wa, v_lru_ba, v_lru_wx, v_lru_bx, v_lru_lambda, v_s5_lam_re, v_s5_lam_im, v_s5_log_dt, v_s5_b_re, v_s5_b_im, v_s5_c_re, v_s5_c_im, v_s5_d, v_s5_w_glu, v_s5_b_glu, v_w_out, v_ln_g, v_ln_b):
    a = dict(locals())
    me = 4 * lax.axis_index("x") + 2 * lax.axis_index("y") + lax.axis_index("c")
    x0, target = x[0], loss_target[0]
    S, D = x0.shape
    BR = D // 4
    NS = w_in.shape[2]
    SH = BR // NDEV
    assert S % (BLK * DILATIONS[-1][1]) == 0 and BR % (8 * S5_CH) == 0
    TS5 = _tile(S, 256)

    small = _exchange(_pack([c, conv_a, conv_c]), True, "gather_small").reshape(NDEV, -1)
    c_all = small[:, :D]
    na, nc = DEPTH * 3 * SH, DEPTH * 4 * SH
    conv_a_full = jnp.transpose(small[:, D:D + na].reshape(NDEV, DEPTH, 3, SH), (1, 2, 0, 3)).reshape(DEPTH, 3, BR)
    conv_c_full = jnp.transpose(small[:, D + na:D + na + nc].reshape(NDEV, DEPTH, 4, SH), (1, 2, 0, 3)).reshape(DEPTH, 4, BR)
    me1 = me.reshape(1).astype(jnp.int32)
    w_in_bf = w_in.astype(BF16)
    wg_out = _exchange(w_out.astype(BF16), True, "gather_w_out")
    wg_glu = _exchange(s5_w_glu.astype(BF16), True, "gather_w_glu")

    b_cols = lax.dynamic_slice(b_ada, (0, me * NS), (DEPTH, NS)).reshape(DEPTH, 1, NS)
    ada_all = _exchange(_ada_cols(c_all, w_ada, b_cols), True, "gather_ada")
    ada_me = lax.dynamic_index_in_dim(ada_all, me, axis=2, keepdims=False)
    ada_me = jnp.transpose(ada_me, (1, 0, 2)).reshape(DEPTH, 3 * D)
    shift, scale, gate = ada_me[:, :D], ada_me[:, D:2 * D], ada_me[:, 2 * D:]

    bias_tabs, bias_pull = jax.vjp(_bias_tables, rel_bias)
    HD = BR // 8
    head_ones = jnp.kron(jnp.eye(8, dtype=F32), jnp.ones((HD, HD), F32))

    saved = []
    xl = x0
    for l in range(DEPTH):
        q = {n: a[n][l] for n in _LAYER_SMALL}
        q['conv_a'], q['conv_c'] = conv_a_full[l], conv_c_full[l]
        kp, pull = jax.vjp(_prep_layer, q)
        pw_re, pw_im = _powers(lax.stop_gradient(kp['ar']), lax.stop_gradient(kp['ai']), TS5)
        w_out_l = wg_out[:, l].reshape(D, D)
        w_glu_l = wg_glu[:, l].reshape(BR, BR)
        mod = _rows8([1.0 + scale[l], shift[l]], D)
        arows = _rows8(list(kp['conv_a']), BR)
        lrows = _rows8(list(kp['conv_c']) + [kp['conv_c_b'], kp['lru_ba'], kp['lru_bx'], kp['sp']], BR)
        srows = _rows8([kp['s5_d'], kp['s5_b_glu']], BR)
        orows = _rows8([1.0 + gate[l], ln_g[l], ln_b[l]], D)
        wa, wx = kp['wa'].astype(BF16), kp['wx'].astype(BF16)
        s5w = [kp[n].astype(BF16) for n in ('bbre', 'bbim', 'cre', 'cim')]

        hbf = _modulate(xl, mod)
        proj, w_in_l = _proj_gather(hbf, w_in_bf[l], me1)
        ya = _conv_a_fwd(proj, arows, BR)
        os_, ls_ = [], []
        for g, (_, dil) in enumerate(DILATIONS):
            o, lse = _attn_fwd(proj, bias_tabs[g], dil, BR)
            os_.append(o)
            ls_.append(lse)
        yb = _attn_mix_fwd(proj, os_, ls_, BR)
        yc, hs = _lru_fwd(proj, lrows, wa, wx, BR)
        yd, xre, xim = _s5_fwd(proj, pw_re, pw_im, *s5w, srows, w_glu_l, BR)
        xn, y, cat = _out_ln([ya, yb, yc, yd], w_out_l, xl, orows)
        saved.append(dict(x=xl, proj=proj, hbf=hbf, os=os_, ls=ls_, hs=hs, xre=xre, xim=xim, y=y, cat=cat, pull=pull,
                          mod=mod, arows=arows, lrows=lrows, srows=srows, orows=orows, wa=wa, wx=wx, s5w=s5w,
                          qw=(pw_re[::-1], pw_im[::-1]), w_in=w_in_l, w_out=w_out_l, w_glu=w_glu_l))
        xl = xn

    dout, loss_local = _loss_grad(xl, target)
    loss = lax.psum(loss_local, ("x", "y", "c"))

    dbias = jnp.zeros_like(bias_tabs)
    lgrads, dada, dw_in_parts, dw_out_parts, dw_glu_parts = [None] * DEPTH, [None] * DEPTH, [None] * DEPTH, [None] * DEPTH, [None] * DEPTH
    for l in reversed(range(DEPTH)):
        sv = saved[l]
        proj = sv['proj']
        dxres, dyb, dcat, lnsum = _ln_bwd(dout, sv['x'], sv['y'], sv['orows'], sv['w_out'])
        dw_out_parts[l] = _dw_scatter(sv['cat'], dyb, D // NDEV, True, me1, "dw_out_scatter")
        dproj, asum = _conv_a_bwd(proj, dcat, sv['arows'], BR)
        dos, es, dbg = _attn_mix_bwd(proj, dcat, sv['os'], sv['ls'], head_ones, BR)
        dqkv, dbs = [], []
        for g, (_, dil) in enumerate(DILATIONS):
            dq_, dk_, dv_, db_ = _attn_bwd(proj, dos[g], es[g], sv['ls'][g], bias_tabs[g], dil, BR)
            dqkv.append((dq_, dk_, dv_))
            dbs.append(db_)
        dbias = dbias + jnp.stack(dbs)
        dproj = _attn_dsum(dqkv, dbg, dproj, BR)
        dproj, lsum, dwa, dwx = _lru_bwd(proj, sv['hs'], dcat, sv['lrows'], sv['wa'], sv['wx'], dproj, BR)
        dproj, ssum, dwg, da_, dbbre, dbbim, dcre, dcim = _s5_bwd(
            proj, dcat, sv['xre'], sv['xim'], *sv['qw'], *sv['s5w'], sv['srows'], sv['w_glu'], dproj, BR)
        dw_glu_parts[l] = dwg.reshape(NDEV, BR // NDEV, BR)
        dout, msum = _dh(dproj, sv['w_in'], dxres, sv['x'], sv['mod'])
        dw_in_parts[l] = _dw_scatter(sv['hbf'], dproj, NS, False, me1, "dw_in_scatter")
        dada[l] = jnp.concatenate([msum[1], msum[0], lnsum[2]])
        dkp = dict(conv_a=asum[0:3], conv_c=lsum[0:4], conv_c_b=lsum[4], lru_ba=lsum[5], lru_bx=lsum[6], sp=lsum[7],
                   wa=dwa, wx=dwx, ar=da_[0], ai=da_[1], bbre=dbbre, bbim=dbbim, cre=dcre, cim=dcim,
                   s5_d=ssum[0], s5_b_glu=ssum[1])
        lgrads[l] = dict(sv['pull'](dkp)[0], ln_g=lnsum[0], ln_b=lnsum[1])
    grad_x = dout[None]

    out = {}

    def put(name, res):
        out[name] = res

    def stacked(fn):
        res = [fn(l) for l in range(DEPTH)]
        return tuple(jnp.stack([r[k] for r in res]) for k in range(4))

    put('w_in', stacked(lambda l: _adamw(dw_in_parts[l], w_in[l], m_w_in[l], v_w_in[l], "adamw_w_in")))
    put('w_out', stacked(lambda l: _adamw(dw_out_parts[l], w_out[l], m_w_out[l], v_w_out[l], "adamw_w_out")))
    put('s5_w_glu', stacked(lambda l: _adamw(_exchange(dw_glu_parts[l], False, "scatter_w_glu"), s5_w_glu[l], m_s5_w_glu[l], v_s5_w_glu[l], "adamw_w_glu")))

    dada_all = _exchange(jnp.stack(dada), True, "gather_dada")
    dada_cols = jnp.transpose(lax.dynamic_slice(dada_all, (0, 0, me * NS), (NDEV, DEPTH, NS)), (1, 0, 2))
    gw_ada, gb_ada = _ada_grads(c_all, dada_all, dada_cols)
    res = _adamw(gw_ada.reshape(DEPTH * D, NS), w_ada.reshape(DEPTH * D, NS), m_w_ada.reshape(DEPTH * D, NS),
                 v_w_ada.reshape(DEPTH * D, NS), "adamw_w_ada")
    put('w_ada', tuple(r.reshape(DEPTH, D, NS) for r in res))

    local = {'rel_bias': bias_pull(dbias)[0]}
    for n in _LAYER_SMALL + ['ln_g', 'ln_b']:
        local[n] = jnp.stack([lgrads[l][n] for l in range(DEPTH)])
    shapes = [local[n].shape for n in _SMALL]
    summed = _sum8(_exchange(_pack([local[n] for n in _SMALL]), True, "gather_small_grads"), "sum_small_grads")
    gs = dict(zip(_SMALL, _unpack(summed, shapes)))
    gs['conv_a'] = lax.dynamic_slice_in_dim(gs['conv_a'], me * SH, SH, axis=2)
    gs['conv_c'] = lax.dynamic_slice_in_dim(gs['conv_c'], me * SH, SH, axis=2)
    gs['b_ada'] = gb_ada
    names = _SMALL + ['b_ada']
    shapes = [a[n].shape for n in names]
    res = _adamw(_pack([gs[n] for n in names]), _pack([a[n] for n in names]), _pack([a['m_' + n] for n in names]),
                 _pack([a['v_' + n] for n in names]), "adamw_small")
    unpacked = [_unpack(r, shapes) for r in res]
    for k, n in enumerate(names):
        put(n, tuple(unpacked[j][k] for j in range(4)))

    return (loss, grad_x, *[out[n][0] for n in _WEIGHTS], *[out[n][1] for n in _WEIGHTS],
            *[out[n][2] for n in _WEIGHTS], *[out[n][3] for n in _WEIGHTS])
```

```python
import functools
import math

import jax
import jax.numpy as jnp
from jax import lax
from jax.experimental import pallas as pl
from jax.experimental.pallas import tpu as pltpu

F32 = jnp.float32
BF16 = jnp.bfloat16

NDEV = 8
DEPTH = 2
BLK = 128
DILATIONS = ((128, 1), (512, 4), (2048, 16))
REL_BUCKETS = 32
REL_MAX_DIST = 2048
LRU_C = 8.0
S5_CH = 16
S5_STATE = 64
ALPHA = (2 * DEPTH) ** 0.25
LN_EPS = 1e-5
ADAM_LR, ADAM_B1, ADAM_B2, ADAM_EPS, ADAM_WD, ADAM_STEP = 0.001, 0.9, 0.999, 1e-08, 0.01, 10
NEG = -1e30
VMEM_LIMIT = 56 * 1024 * 1024


def _call(body, **kw):
    return pl.pallas_call(body, **kw)


def _params(*sem):
    return pltpu.CompilerParams(dimension_semantics=sem, vmem_limit_bytes=VMEM_LIMIT)


def _sigmoid(x):
    return 1.0 / (1.0 + jnp.exp(-x))


def _silu(x):
    return x * _sigmoid(x)


def _dsilu(x):
    s = _sigmoid(x)
    return s * (1.0 + x * (1.0 - s))


_GC = math.sqrt(2.0 / math.pi)


def _gelu(x):
    return 0.5 * x * (1.0 + jnp.tanh(_GC * (x + 0.044715 * x * x * x)))


def _dgelu(x):
    t = jnp.tanh(_GC * (x + 0.044715 * x * x * x))
    return 0.5 * (1.0 + t) + 0.5 * x * (1.0 - t * t) * _GC * (1.0 + 3.0 * 0.044715 * x * x)


def _expm1(x):
    series = x * (1.0 + x * (1.0 / 2) * (1.0 + x * (1.0 / 3) * (1.0 + x * (1.0 / 4) * (1.0 + x * (1.0 / 5) * (1.0 + x * (1.0 / 6))))))
    return jnp.where(jnp.abs(x) < 0.25, series, jnp.exp(x) - 1.0)


def _dot(a, b):
    return jnp.dot(a, b, preferred_element_type=F32)


def _dot_nt(a, b):
    return lax.dot_general(a, b, (((1,), (1,)), ((), ())), preferred_element_type=F32)


def _dot_tn(a, b):
    return lax.dot_general(a, b, (((0,), (0,)), ((), ())), preferred_element_type=F32)


def _colsum(v):
    return jnp.sum(v, axis=0, keepdims=True)


def _exchange(x, gather, name):
    blk = x.shape if gather else x.shape[1:]

    def body(x_ref, o_ref, send_sems, recv_sems, local_sem):
        ix, iy, ic = lax.axis_index("x"), lax.axis_index("y"), lax.axis_index("c")
        me = 4 * ix + 2 * iy + ic

        def src(dev):
            return x_ref if gather else x_ref.at[dev]

        local = pltpu.make_async_copy(src(me), o_ref.at[me], local_sem)
        local.start()
        copies = []
        for k in range(1, NDEV):
            px = 1 - ix if k & 4 else ix
            py = 1 - iy if k & 2 else iy
            pc = 1 - ic if k & 1 else ic
            peer = 4 * px + 2 * py + pc
            cp = pltpu.make_async_remote_copy(
                src_ref=src(peer), dst_ref=o_ref.at[me], send_sem=send_sems.at[k - 1], recv_sem=recv_sems.at[k - 1],
                device_id=(px, py, pc), device_id_type=pl.DeviceIdType.MESH)
            cp.start()
            copies.append(cp)
        for cp in copies:
            cp.wait()
        local.wait()

    return _call(
        body, name=name,
        out_shape=jax.ShapeDtypeStruct((NDEV,) + tuple(blk), x.dtype),
        in_specs=[pl.BlockSpec(memory_space=pl.ANY)],
        out_specs=pl.BlockSpec(memory_space=pl.ANY),
        scratch_shapes=[pltpu.SemaphoreType.DMA((NDEV - 1,)), pltpu.SemaphoreType.DMA((NDEV - 1,)), pltpu.SemaphoreType.DMA],
    )(x)


def _pack(arrs):
    flat = jnp.concatenate([a.reshape(-1).astype(F32) for a in arrs])
    n = flat.shape[0]
    quantum = 1024 if n <= 65536 else 65536
    npad = -(-n // quantum) * quantum
    return jnp.pad(flat, (0, npad - n)).reshape(npad // 128, 128)


def _unpack(packed, shapes):
    flat = packed.reshape(-1)
    out, off = [], 0
    for s in shapes:
        n = math.prod(s)
        out.append(flat[off:off + n].reshape(s))
        off += n
    return out


def _sum8(parts, name):
    _, R, C = parts.shape
    TR = R
    for cand in (512, 256, 128, 64, 32, 16, 8):
        if R % cand == 0:
            TR = cand
            break

    def body(p_ref, o_ref):
        acc = p_ref[0]
        for d in range(1, NDEV):
            acc = acc + p_ref[d]
        o_ref[...] = acc

    return _call(
        body, name=name, grid=(R // TR,),
        in_specs=[pl.BlockSpec((NDEV, TR, C), lambda i: (0, i, 0))],
        out_specs=pl.BlockSpec((TR, C), lambda i: (i, 0)),
        out_shape=jax.ShapeDtypeStruct((R, C), F32),
        compiler_params=_params("arbitrary"),
    )(parts)


def _adamw(g_parts, w, m, v, name):
    R, C = w.shape
    nparts = g_parts.shape[0] if g_parts.ndim == 3 else 0
    TR = R
    for cand in (256, 128, 64, 32, 16, 8):
        if R % cand == 0:
            TR = cand
            break

    def body(g_ref, w_ref, m_ref, v_ref, go_ref, d_ref, mo_ref, vo_ref):
        if nparts:
            g = g_ref[0].astype(F32)
            for d in range(1, nparts):
                g = g + g_ref[d].astype(F32)
        else:
            g = g_ref[...]
        m2 = ADAM_B1 * m_ref[...] + (1.0 - ADAM_B1) * g
        v2 = ADAM_B2 * v_ref[...] + (1.0 - ADAM_B2) * (g * g)
        m_hat = m2 / (1.0 - ADAM_B1 ** ADAM_STEP)
        v_hat = v2 / (1.0 - ADAM_B2 ** ADAM_STEP)
        go_ref[...] = g
        d_ref[...] = -ADAM_LR * (m_hat / (jnp.sqrt(v_hat) + ADAM_EPS) + ADAM_WD * w_ref[...])
        mo_ref[...] = m2
        vo_ref[...] = v2

    spec = pl.BlockSpec((TR, C), lambda i: (i, 0))
    gspec = pl.BlockSpec((nparts, TR, C), lambda i: (0, i, 0)) if nparts else spec
    return _call(
        body, name=name, grid=(R // TR,),
        in_specs=[gspec, spec, spec, spec], out_specs=[spec] * 4,
        out_shape=[jax.ShapeDtypeStruct((R, C), F32)] * 4,
        compiler_params=_params("arbitrary"),
    )(g_parts, w, m, v)


def _ada_cols(c_all, w_ada, b_cols):
    L, D, NS = w_ada.shape

    def body(c_ref, w_ref, b_ref, o_ref):
        cond = _silu(c_ref[...]).astype(BF16)
        o_ref[0] = _dot(cond, w_ref[0].astype(BF16)) + b_ref[0]

    return _call(
        body, name="ada_cols", grid=(L,),
        in_specs=[pl.BlockSpec((NDEV, D), lambda l: (0, 0)), pl.BlockSpec((1, D, NS), lambda l: (l, 0, 0)),
                  pl.BlockSpec((1, 1, NS), lambda l: (l, 0, 0))],
        out_specs=pl.BlockSpec((1, NDEV, NS), lambda l: (l, 0, 0)),
        out_shape=jax.ShapeDtypeStruct((L, NDEV, NS), F32),
        compiler_params=_params("arbitrary"),
    )(c_all, w_ada, b_cols)


def _ada_grads(c_all, dada_all, dada_cols):
    _, D = c_all.shape
    L, _, NS = dada_cols.shape
    D3 = dada_all.shape[2]

    def body(c_ref, da_ref, dc_ref, gw_ref, gb_ref):
        cond = _silu(c_ref[...]).astype(BF16)
        gw_ref[0] = _dot_tn(cond, dc_ref[0].astype(BF16))
        acc = da_ref[0]
        for d in range(1, NDEV):
            acc = acc + da_ref[d]
        gb_ref[...] = acc

    return _call(
        body, name="ada_grads", grid=(L,),
        in_specs=[pl.BlockSpec((NDEV, D), lambda l: (0, 0)), pl.BlockSpec((NDEV, L, D3), lambda l: (0, 0, 0)),
                  pl.BlockSpec((1, NDEV, NS), lambda l: (l, 0, 0))],
        out_specs=[pl.BlockSpec((1, D, NS), lambda l: (l, 0, 0)), pl.BlockSpec((L, D3), lambda l: (0, 0))],
        out_shape=[jax.ShapeDtypeStruct((L, D, NS), F32), jax.ShapeDtypeStruct((L, D3), F32)],
        compiler_params=_params("arbitrary"),
    )(c_all, dada_all, dada_cols)


def _tile(S, want):
    return want if S % want == 0 else S


def _modulate(x, mod):
    S, D = x.shape
    TS = _tile(S, 512)

    def body(x_ref, mod_ref, h_ref):
        h_ref[...] = (x_ref[...] * mod_ref[0:1, :] + mod_ref[1:2, :]).astype(BF16)

    spec = pl.BlockSpec((TS, D), lambda i: (i, 0))
    return _call(
        body, name="modulate", grid=(S // TS,),
        in_specs=[spec, pl.BlockSpec((8, D), lambda i: (0, 0))], out_specs=spec,
        out_shape=jax.ShapeDtypeStruct((S, D), BF16),
        compiler_params=_params("arbitrary"),
    )(x, mod)


def _peer(k):
    ix, iy, ic = lax.axis_index("x"), lax.axis_index("y"), lax.axis_index("c")
    bx, by, bc = (k >> 2) & 1, (k >> 1) & 1, k & 1
    px, py, pc = ix + bx - 2 * ix * bx, iy + by - 2 * iy * by, ic + bc - 2 * ic * bc
    return (px, py, pc), 4 * px + 2 * py + pc


GATHER_ORDER = (0, 1, 2, 4, 3, 5, 6, 7)
SCATTER_ORDER = (2, 4, 6, 3, 5, 7, 1, 0)


def _offset(order, k):
    off = jnp.int32(order[-1])
    for step in reversed(range(len(order) - 1)):
        off = jnp.where(k == step, jnp.int32(order[step]), off)
    return off


def _proj_gather(h, w_shard, extras, me1):
    assert GATHER_ORDER[0] == 0
    S, D = h.shape
    NS = w_shard.shape[1]
    TS = _tile(S, 512)
    nt = S // TS
    ne = len(extras)

    def body(me_ref, h_ref, w_ref, *rest):
        x_refs, o_ref, wg_ref = rest[:ne], rest[ne], rest[ne + 1]
        xg_refs = rest[ne + 2:2 * ne + 2]
        wbuf, send_sems, recv_sems, local_sems, load_sems = rest[2 * ne + 2:]
        k, i = pl.program_id(0), pl.program_id(1)
        me = me_ref[0]
        off = _offset(GATHER_ORDER, k)

        def push(src, dst, which, kk):
            peer, _ = _peer(kk)
            return pltpu.make_async_remote_copy(
                src_ref=src, dst_ref=dst.at[me], send_sem=send_sems.at[which, kk], recv_sem=recv_sems.at[which, kk],
                device_id=peer, device_id_type=pl.DeviceIdType.MESH)

        def own(src, dst, which):
            return pltpu.make_async_copy(src, dst.at[me], local_sems.at[which])

        pairs = [(w_ref, wg_ref)] + list(zip(x_refs, xg_refs))

        @pl.when(jnp.logical_and(k == 0, i == 0))
        def _():
            for which, (src, dst) in enumerate(pairs):
                own(src, dst, which).start()
                for kk in GATHER_ORDER[1:]:
                    push(src, dst, which, kk).start()
            first = pltpu.make_async_copy(w_ref, wbuf.at[0], load_sems.at[0])
            first.start()
            first.wait()

        @pl.when(jnp.logical_and(k > 0, i == 0))
        def _():
            push(w_ref, wg_ref, 0, off).wait_recv()
            _, blk = _peer(off)
            load = pltpu.make_async_copy(wg_ref.at[blk], wbuf.at[k % 2], load_sems.at[k % 2])
            load.start()
            load.wait()

        o_ref[...] = _dot(h_ref[...], wbuf[k % 2])

        @pl.when(jnp.logical_and(k == NDEV - 1, i == nt - 1))
        def _():
            for which, (src, dst) in enumerate(pairs):
                for kk in range(1, NDEV):
                    cp = push(src, dst, which, kk)
                    cp.wait_send()
                    if which > 0:
                        cp.wait_recv()
                own(src, dst, which).wait()

    def col(k, i, me_ref):
        m, off = me_ref[0], _offset(GATHER_ORDER, k)
        return i, (m | off) - (m & off)

    anyspec = pl.BlockSpec(memory_space=pl.ANY)
    grid_spec = pltpu.PrefetchScalarGridSpec(
        num_scalar_prefetch=1, grid=(NDEV, nt),
        in_specs=[pl.BlockSpec((TS, D), lambda k, i, me_ref: (i, 0)), anyspec] + [anyspec] * ne,
        out_specs=[pl.BlockSpec((TS, NS), col), anyspec] + [anyspec] * ne,
        scratch_shapes=[pltpu.VMEM((2, D, NS), BF16), pltpu.SemaphoreType.DMA((1 + ne, NDEV)), pltpu.SemaphoreType.DMA((1 + ne, NDEV)),
                        pltpu.SemaphoreType.DMA((1 + ne,)), pltpu.SemaphoreType.DMA((2,))])
    return _call(
        body, name="proj_gather", grid_spec=grid_spec,
        out_shape=[jax.ShapeDtypeStruct((S, NDEV * NS), F32), jax.ShapeDtypeStruct((NDEV, D, NS), BF16)]
        + [jax.ShapeDtypeStruct((NDEV,) + e.shape, e.dtype) for e in extras],
        compiler_params=_params("arbitrary", "arbitrary"),
    )(me1, h, w_shard, *extras)


def _dw_scatter(a, b, blk, by_rows, me1, name):
    assert SCATTER_ORDER[-1] == 0
    S = a.shape[0]
    TK = _tile(S, 512)
    nk = S // TK
    shape = (blk, b.shape[1]) if by_rows else (a.shape[1], blk)
    last = NDEV - 1

    def body(me_ref, a_ref, b_ref, recv_ref, acc, stage, send_sems, recv_sems, local_sem):
        k, kk = pl.program_id(0), pl.program_id(1)
        me = me_ref[0]

        def push(off, slot):
            peer, _ = _peer(off)
            return pltpu.make_async_remote_copy(
                src_ref=stage.at[slot], dst_ref=recv_ref.at[me], send_sem=send_sems.at[off], recv_sem=recv_sems.at[off],
                device_id=peer, device_id_type=pl.DeviceIdType.MESH)

        own = pltpu.make_async_copy(stage.at[last % 2], recv_ref.at[me], local_sem)

        @pl.when(kk == 0)
        def _():
            acc[...] = jnp.zeros_like(acc)
        acc[...] += _dot_tn(a_ref[...], b_ref[...])

        @pl.when(kk == nk - 1)
        def _():
            @pl.when(k >= 2)
            def _():
                push(_offset(SCATTER_ORDER, k - 2), k % 2).wait_send()

            stage[k % 2] = acc[...].astype(BF16)

            @pl.when(k < last)
            def _():
                push(_offset(SCATTER_ORDER, k), k % 2).start()

            @pl.when(k == last)
            def _():
                own.start()
                push(SCATTER_ORDER[last - 1], (last - 1) % 2).wait_send()
                own.wait()
                for off in range(1, NDEV):
                    push(off, 0).wait_recv()

    def blk_of(k, me_ref):
        m, off = me_ref[0], _offset(SCATTER_ORDER, k)
        return (m | off) - (m & off)

    if by_rows:
        in_specs = [pl.BlockSpec((TK, blk), lambda k, kk, me_ref: (kk, blk_of(k, me_ref))),
                    pl.BlockSpec((TK, b.shape[1]), lambda k, kk, me_ref: (kk, 0))]
    else:
        in_specs = [pl.BlockSpec((TK, a.shape[1]), lambda k, kk, me_ref: (kk, 0)),
                    pl.BlockSpec((TK, blk), lambda k, kk, me_ref: (kk, blk_of(k, me_ref)))]
    grid_spec = pltpu.PrefetchScalarGridSpec(
        num_scalar_prefetch=1, grid=(NDEV, nk), in_specs=in_specs,
        out_specs=pl.BlockSpec(memory_space=pl.ANY),
        scratch_shapes=[pltpu.VMEM(shape, F32), pltpu.VMEM((2,) + shape, BF16), pltpu.SemaphoreType.DMA((NDEV,)),
                        pltpu.SemaphoreType.DMA((NDEV,)), pltpu.SemaphoreType.DMA])
    return _call(
        body, name=name, grid_spec=grid_spec,
        out_shape=jax.ShapeDtypeStruct((NDEV,) + shape, BF16),
        compiler_params=_params("arbitrary", "arbitrary"),
    )(me1, a, b)


def _chunk(TS, BR, k):
    return pl.BlockSpec((TS, BR), lambda i: (i, k))


def _halo_prev(TS, BR, k):
    return pl.BlockSpec((8, BR), lambda i: (jnp.maximum(i * (TS // 8) - 1, 0), k))


def _conv_a_fwd(proj, w8, BR):
    S = proj.shape[0]
    TS = _tile(S, 512)

    def body(ab_ref, ac_ref, ax_ref, ag_ref, hc_ref, hx_ref, w_ref, o_ref, ext):
        i = pl.program_id(0)
        u = ac_ref[...] * ax_ref[...]
        ext[0:8, :] = jnp.where(i > 0, hc_ref[...] * hx_ref[...], 0.0)
        ext[8:8 + TS, :] = u
        conv = ext[pl.ds(6, TS), :] * w_ref[0:1, :] + ext[pl.ds(7, TS), :] * w_ref[1:2, :] + u * w_ref[2:3, :]
        o_ref[...] = ab_ref[...] * conv * _silu(ag_ref[...])

    return _call(
        body, name="conv_a_fwd", grid=(S // TS,),
        in_specs=[_chunk(TS, BR, 0), _chunk(TS, BR, 1), _chunk(TS, BR, 2), _chunk(TS, BR, 3),
                  _halo_prev(TS, BR, 1), _halo_prev(TS, BR, 2), pl.BlockSpec((8, BR), lambda i: (0, 0))],
        out_specs=pl.BlockSpec((TS, BR), lambda i: (i, 0)),
        out_shape=jax.ShapeDtypeStruct((S, BR), F32),
        scratch_shapes=[pltpu.VMEM((TS + 8, BR), F32)],
        compiler_params=_params("arbitrary"),
    )(proj, proj, proj, proj, proj, proj, w8)


ATT_UNIT = 2048
ATT_LANES = 128


def _attn_mask():
    i = lax.broadcasted_iota(jnp.int32, (BLK, 2 * BLK), 0)
    j = lax.broadcasted_iota(jnp.int32, (BLK, 2 * BLK), 1)
    return jnp.logical_and(j >= i, j <= i + BLK), j >= BLK


def _attn_rows(n, r, dil):
    return pl.ds(n * BLK * dil + r, BLK, stride=dil) if dil > 1 else pl.ds(n * BLK, BLK)


def _attn_geometry(S, BR, dil):
    HD = BR // 8
    U = ATT_UNIT
    assert S % U == 0 and BR % ATT_LANES == 0 and ATT_LANES % HD == 0 and U % (BLK * dil) == 0
    return HD, U, S // U, U // (BLK * dil), ATT_LANES // HD, BR // ATT_LANES


def _attn_fwd(proj, bias, dil, BR):
    S = proj.shape[0]
    HD, U, NU, nbu, hpb, HBK = _attn_geometry(S, BR, dil)
    scale = HD ** -0.5
    combos = [(n, r) for n in range(nbu) for r in range(dil)]

    def body(q_ref, kc_ref, kp_ref, vc_ref, vp_ref, b_ref, o_ref, l_ref, sbuf, pbuf):
        m = pl.program_id(1)
        band, is_cur = _attn_mask()
        first = jnp.logical_and(band, jnp.logical_or(is_cur, m > 0))

        def keys(ref_c, ref_p, n, r):
            prev = ref_c[_attn_rows(n - 1, r, dil), :] if n > 0 else ref_p[_attn_rows(nbu - 1, r, dil), :]
            return jnp.concatenate([prev, ref_c[_attn_rows(n, r, dil), :]], axis=0).astype(BF16)

        for c, (n, r) in enumerate(combos):
            q = (q_ref[_attn_rows(n, r, dil), :] * scale).astype(BF16)
            kk = keys(kc_ref, kp_ref, n, r)
            for h in range(hpb):
                sl = slice(h * HD, (h + 1) * HD)
                sbuf[c * hpb + h] = _dot_nt(q[:, sl], kk[:, sl])
        for c, (n, r) in enumerate(combos):
            lses = []
            for h in range(hpb):
                s = jnp.where(band if n > 0 else first, sbuf[c * hpb + h] + b_ref[h], NEG)
                mx = jnp.max(s, axis=-1, keepdims=True)
                p = jnp.exp(s - mx)
                l = jnp.sum(p, axis=-1, keepdims=True)
                pbuf[c * hpb + h] = (p * (1.0 / l)).astype(BF16)
                lses.append(jnp.broadcast_to(mx + jnp.log(l), (BLK, HD)))
            l_ref[_attn_rows(n, r, dil), :] = jnp.concatenate(lses, axis=1)
        for c, (n, r) in enumerate(combos):
            vv = keys(vc_ref, vp_ref, n, r)
            o_ref[_attn_rows(n, r, dil), :] = jnp.concatenate(
                [_dot(pbuf[c * hpb + h], vv[:, h * HD:(h + 1) * HD]) for h in range(hpb)], axis=1)

    def cur(c):
        return pl.BlockSpec((U, ATT_LANES), lambda hb, m: (m, c * HBK + hb))

    def prev(c):
        return pl.BlockSpec((U, ATT_LANES), lambda hb, m: (jnp.maximum(m - 1, 0), c * HBK + hb))

    ospec = pl.BlockSpec((U, ATT_LANES), lambda hb, m: (m, hb))
    nhb = len(combos) * hpb
    return _call(
        body, name="attn_fwd_d%d" % dil, grid=(HBK, NU),
        in_specs=[cur(4), cur(5), prev(5), cur(6), prev(6), pl.BlockSpec((hpb, BLK, 2 * BLK), lambda hb, m: (hb, 0, 0))],
        out_specs=[ospec, ospec],
        out_shape=[jax.ShapeDtypeStruct((S, BR), F32)] * 2,
        scratch_shapes=[pltpu.VMEM((nhb, BLK, 2 * BLK), F32), pltpu.VMEM((nhb, BLK, 2 * BLK), BF16)],
        compiler_params=_params("arbitrary", "arbitrary"),
    )(proj, proj, proj, proj, proj, bias)


def _mix_weights(l1, l2, l3):
    m = jnp.maximum(jnp.maximum(l1, l2), l3)
    e1, e2, e3 = jnp.exp(l1 - m), jnp.exp(l2 - m), jnp.exp(l3 - m)
    inv = 1.0 / (e1 + e2 + e3)
    return e1 * inv, e2 * inv, e3 * inv


def _attn_mix_fwd(proj, os_, ls_, BR):
    S = proj.shape[0]
    TS = _tile(S, 512)

    def body(o1, o2, o3, l1, l2, l3, g_ref, y_ref):
        w1, w2, w3 = _mix_weights(l1[...], l2[...], l3[...])
        y_ref[...] = (w1 * o1[...] + w2 * o2[...] + w3 * o3[...]) * _silu(g_ref[...])

    spec = pl.BlockSpec((TS, BR), lambda i: (i, 0))
    return _call(
        body, name="attn_mix_fwd", grid=(S // TS,),
        in_specs=[spec] * 6 + [_chunk(TS, BR, 7)], out_specs=spec,
        out_shape=jax.ShapeDtypeStruct((S, BR), F32),
        compiler_params=_params("arbitrary"),
    )(*os_, *ls_, proj)


def _lru_gates(xc, wa, wx, ba, bx, sp):
    xb = xc.astype(BF16)
    r = _sigmoid(_dot(xb, wa) + ba)
    ig = _sigmoid(_dot(xb, wx) + bx)
    la = -LRU_C * r * sp
    a = jnp.exp(la)
    mult = jnp.sqrt(-_expm1(2.0 * la))
    return r, ig, a, mult


def _lru_fwd(proj, rows, wa, wx, BR):
    S = proj.shape[0]
    TS = _tile(S, 256)
    PAD = TS // 2

    def body(cx_ref, hx_ref, cg_ref, r_ref, wa_ref, wx_ref, y_ref, h_ref, ext, sa, sb, carry):
        i = pl.program_id(0)

        @pl.when(i == 0)
        def _():
            sa[0:PAD, :] = jnp.ones((PAD, BR), F32)
            sb[0:PAD, :] = jnp.zeros((PAD, BR), F32)
            carry[...] = jnp.zeros_like(carry)

        cx = cx_ref[...]
        ext[0:8, :] = jnp.where(i > 0, hx_ref[...], 0.0)
        ext[8:8 + TS, :] = cx
        xc = (ext[pl.ds(5, TS), :] * r_ref[0:1, :] + ext[pl.ds(6, TS), :] * r_ref[1:2, :]
              + ext[pl.ds(7, TS), :] * r_ref[2:3, :] + cx * r_ref[3:4, :] + r_ref[4:5, :])
        _, ig, a, mult = _lru_gates(xc, wa_ref[...], wx_ref[...], r_ref[5:6, :], r_ref[6:7, :], r_ref[7:8, :])
        sa[PAD:PAD + TS, :] = a
        sb[PAD:PAD + TS, :] = mult * ig * xc
        d = 1
        while d < TS:
            A, B = sa[PAD:PAD + TS, :], sb[PAD:PAD + TS, :]
            As, Bs = sa[pl.ds(PAD - d, TS), :], sb[pl.ds(PAD - d, TS), :]
            sb[PAD:PAD + TS, :] = A * Bs + B
            sa[PAD:PAD + TS, :] = A * As
            d *= 2
        h = sb[PAD:PAD + TS, :] + sa[PAD:PAD + TS, :] * carry[0:1, :]
        carry[0:1, :] = h[TS - 1:TS, :]
        h_ref[...] = h
        y_ref[...] = h * _silu(cg_ref[...])

    full = pl.BlockSpec((BR, BR), lambda i: (0, 0))
    spec = pl.BlockSpec((TS, BR), lambda i: (i, 0))
    return _call(
        body, name="lru_fwd", grid=(S // TS,),
        in_specs=[_chunk(TS, BR, 8), _halo_prev(TS, BR, 8), _chunk(TS, BR, 9), pl.BlockSpec((8, BR), lambda i: (0, 0)), full, full],
        out_specs=[spec, spec],
        out_shape=[jax.ShapeDtypeStruct((S, BR), F32)] * 2,
        scratch_shapes=[pltpu.VMEM((TS + 8, BR), F32), pltpu.VMEM((PAD + TS, BR), F32), pltpu.VMEM((PAD + TS, BR), F32),
                        pltpu.VMEM((8, BR), F32)],
        compiler_params=_params("arbitrary"),
    )(proj, proj, proj, rows, wa, wx)


def _s5_fwd(proj, pw_re, pw_im, bbre, bbim, cre, cim, rows, wg, BR):
    S = proj.shape[0]
    TS, W = pw_re.shape
    PAD = TS // 2
    CB = 8 * S5_CH
    SB = 8 * S5_STATE
    nblk = BR // CB

    def body(u_ref, dg_ref, pr_ref, pi_ref, bbre_ref, bbim_ref, cre_ref, cim_ref, r_ref, wg_ref,
             y_ref, xre_ref, xim_ref, sre, sim, ypre_s, carry):
        i = pl.program_id(0)

        @pl.when(i == 0)
        def _():
            sre[0:PAD, :] = jnp.zeros((PAD, SB), F32)
            sim[0:PAD, :] = jnp.zeros((PAD, SB), F32)
            carry[...] = jnp.zeros_like(carry)

        for m in range(nblk):
            cs, ws = slice(m * CB, (m + 1) * CB), slice(m * SB, (m + 1) * SB)
            ub = u_ref[:, cs].astype(BF16)
            sre[PAD:PAD + TS, :] = _dot(ub, bbre_ref[cs, :])
            sim[PAD:PAD + TS, :] = _dot(ub, bbim_ref[cs, :])
            d = 1
            while d < TS:
                ar, ai = pr_ref[d - 1:d, ws], pi_ref[d - 1:d, ws]
                xr, xi = sre[PAD:PAD + TS, :], sim[PAD:PAD + TS, :]
                sr, si = sre[pl.ds(PAD - d, TS), :], sim[pl.ds(PAD - d, TS), :]
                sre[PAD:PAD + TS, :] = xr + ar * sr - ai * si
                sim[PAD:PAD + TS, :] = xi + ar * si + ai * sr
                d *= 2
            cr, ci = carry[0:1, ws], carry[1:2, ws]
            pr, pi = pr_ref[:, ws], pi_ref[:, ws]
            xre = sre[PAD:PAD + TS, :] + pr * cr - pi * ci
            xim = sim[PAD:PAD + TS, :] + pr * ci + pi * cr
            carry[0:1, ws] = xre[TS - 1:TS, :]
            carry[1:2, ws] = xim[TS - 1:TS, :]
            xre_ref[:, ws] = xre
            xim_ref[:, ws] = xim
            ypre_s[:, cs] = _dot(xre.astype(BF16), cre_ref[ws, :]) - _dot(xim.astype(BF16), cim_ref[ws, :])
        dg = dg_ref[...]
        yg = _gelu(ypre_s[...] + r_ref[0:1, :] * u_ref[...])
        s = _sigmoid(_dot(yg.astype(BF16), wg_ref[...]) + r_ref[1:2, :])
        y_ref[...] = yg * s * _silu(dg)

    def const(shape):
        return pl.BlockSpec(shape, lambda i: (0, 0))

    return _call(
        body, name="s5_fwd", grid=(S // TS,),
        in_specs=[_chunk(TS, BR, 10), _chunk(TS, BR, 11), const((TS, W)), const((TS, W)), const((BR, SB)), const((BR, SB)),
                  const((W, CB)), const((W, CB)), const((8, BR)), const((BR, BR))],
        out_specs=[pl.BlockSpec((TS, BR), lambda i: (i, 0)), pl.BlockSpec((TS, W), lambda i: (i, 0)), pl.BlockSpec((TS, W), lambda i: (i, 0))],
        out_shape=[jax.ShapeDtypeStruct((S, BR), F32), jax.ShapeDtypeStruct((S, W), F32), jax.ShapeDtypeStruct((S, W), F32)],
        scratch_shapes=[pltpu.VMEM((PAD + TS, SB), F32), pltpu.VMEM((PAD + TS, SB), F32), pltpu.VMEM((TS, BR), F32), pltpu.VMEM((8, W), F32)],
        compiler_params=_params("arbitrary"),
    )(proj, proj, pw_re, pw_im, bbre, bbim, cre, cim, rows, wg)


def _out_ln(ys, w_out, x, rows):
    S, D = x.shape
    BR = D // 4
    TS = _tile(S, 256)

    def body(ya, yb, yc, yd, w_ref, x_ref, r_ref, o_ref, y_ref, cat_ref):
        cat = jnp.concatenate([ya[...], yb[...], yc[...], yd[...]], axis=1).astype(BF16)
        y = _dot(cat, w_ref[...])
        z = ALPHA * x_ref[...] + r_ref[0:1, :] * y
        mu = jnp.mean(z, axis=-1, keepdims=True)
        zc = z - mu
        var = jnp.mean(zc * zc, axis=-1, keepdims=True)
        o_ref[...] = zc * lax.rsqrt(var + LN_EPS) * r_ref[1:2, :] + r_ref[2:3, :]
        y_ref[...] = y
        cat_ref[...] = cat

    yspec = pl.BlockSpec((TS, BR), lambda i: (i, 0))
    spec = pl.BlockSpec((TS, D), lambda i: (i, 0))
    return _call(
        body, name="out_ln", grid=(S // TS,),
        in_specs=[yspec] * 4 + [pl.BlockSpec((D, D), lambda i: (0, 0)), spec, pl.BlockSpec((8, D), lambda i: (0, 0))],
        out_specs=[spec, spec, spec],
        out_shape=[jax.ShapeDtypeStruct((S, D), F32), jax.ShapeDtypeStruct((S, D), F32), jax.ShapeDtypeStruct((S, D), BF16)],
        compiler_params=_params("arbitrary"),
    )(*ys, w_out, x, rows)


def _loss_grad(x, target):
    S, D = x.shape
    TS = _tile(S, 512)

    def body(x_ref, t_ref, g_ref, l_ref):
        @pl.when(pl.program_id(0) == 0)
        def _():
            l_ref[...] = jnp.zeros_like(l_ref)
        diff = x_ref[...] - t_ref[...]
        g_ref[...] = diff * (1.0 / D)
        l_ref[...] += jnp.sum(_colsum(diff * diff), axis=1, keepdims=True) * (0.5 / D)

    spec = pl.BlockSpec((TS, D), lambda i: (i, 0))
    g, l = _call(
        body, name="loss_grad", grid=(S // TS,),
        in_specs=[spec, spec], out_specs=[spec, pl.BlockSpec((1, 1), lambda i: (0, 0))],
        out_shape=[jax.ShapeDtypeStruct((S, D), F32), jax.ShapeDtypeStruct((1, 1), F32)],
        compiler_params=_params("arbitrary"),
    )(x, target)
    return g, l[0, 0]


def _ln_bwd(dout, x, y, rows, w_out):
    S, D = x.shape
    TS = _tile(S, 256)

    def body(do_ref, x_ref, y_ref, r_ref, w_ref, dxr_ref, dyb_ref, dcat_ref, acc_ref):
        @pl.when(pl.program_id(0) == 0)
        def _():
            acc_ref[...] = jnp.zeros_like(acc_ref)
        g1, lg = r_ref[0:1, :], r_ref[1:2, :]
        yv = y_ref[...]
        z = ALPHA * x_ref[...] + g1 * yv
        mu = jnp.mean(z, axis=-1, keepdims=True)
        zc = z - mu
        var = jnp.mean(zc * zc, axis=-1, keepdims=True)
        rstd = lax.rsqrt(var + LN_EPS)
        xhat = zc * rstd
        do = do_ref[...]
        dxh = do * lg
        dz = rstd * (dxh - jnp.mean(dxh, axis=-1, keepdims=True) - xhat * jnp.mean(dxh * xhat, axis=-1, keepdims=True))
        dxr_ref[...] = ALPHA * dz
        dyb = (g1 * dz).astype(BF16)
        dyb_ref[...] = dyb
        dcat_ref[...] = _dot_nt(dyb, w_ref[...])
        acc_ref[0:1, :] += _colsum(do * xhat)
        acc_ref[1:2, :] += _colsum(do)
        acc_ref[2:3, :] += _colsum(dz * yv)

    spec = pl.BlockSpec((TS, D), lambda i: (i, 0))
    rspec = pl.BlockSpec((8, D), lambda i: (0, 0))
    return _call(
        body, name="ln_bwd", grid=(S // TS,),
        in_specs=[spec, spec, spec, rspec, pl.BlockSpec((D, D), lambda i: (0, 0))],
        out_specs=[spec, spec, spec, rspec],
        out_shape=[jax.ShapeDtypeStruct((S, D), F32), jax.ShapeDtypeStruct((S, D), BF16), jax.ShapeDtypeStruct((S, D), F32),
                   jax.ShapeDtypeStruct((8, D), F32)],
        compiler_params=_params("arbitrary"),
    )(dout, x, y, rows, w_out)


def _dh(dproj, wg, dxres, x, mod):
    S, D = x.shape
    NS = wg.shape[2]
    TS = _tile(S, 512)

    def body(dp_ref, w_ref, dxr_ref, x_ref, mod_ref, dx_ref, sum_ref, acc):
        i, j = pl.program_id(0), pl.program_id(1)

        @pl.when(jnp.logical_and(i == 0, j == 0))
        def _():
            sum_ref[...] = jnp.zeros_like(sum_ref)

        @pl.when(j == 0)
        def _():
            acc[...] = jnp.zeros_like(acc)

        acc[...] += _dot_nt(dp_ref[...], w_ref[0])

        @pl.when(j == NDEV - 1)
        def _():
            dh = acc[...]
            dx_ref[...] = dxr_ref[...] + dh * mod_ref[0:1, :]
            sum_ref[0:1, :] += _colsum(dh * x_ref[...])
            sum_ref[1:2, :] += _colsum(dh)

    spec = pl.BlockSpec((TS, D), lambda i, j: (i, 0))
    rspec = pl.BlockSpec((8, D), lambda i, j: (0, 0))
    return _call(
        body, name="dh", grid=(S // TS, NDEV),
        in_specs=[pl.BlockSpec((TS, NS), lambda i, j: (i, j)), pl.BlockSpec((1, D, NS), lambda i, j: (j, 0, 0)), spec, spec, rspec],
        out_specs=[spec, rspec],
        out_shape=[jax.ShapeDtypeStruct((S, D), F32), jax.ShapeDtypeStruct((8, D), F32)],
        scratch_shapes=[pltpu.VMEM((TS, D), F32)],
        compiler_params=_params("arbitrary", "arbitrary"),
    )(dproj, wg, dxres, x, mod)


def _halo_next(S, TS, BR, k):
    last = S // 8 - 1
    return pl.BlockSpec((8, BR), lambda i: (jnp.minimum((i + 1) * (TS // 8), last), k))


def _conv_a_bwd(proj, dcat, w8, BR):
    S = proj.shape[0]
    TS = _tile(S, 512)
    nt = S // TS

    def body(ab_ref, ac_ref, ax_ref, ag_ref, dy_ref, hc_ref, hx_ref, nab_ref, nag_ref, ndy_ref, w_ref, dp_ref, dw_ref, ext, dext):
        i = pl.program_id(0)

        @pl.when(i == 0)
        def _():
            dw_ref[...] = jnp.zeros_like(dw_ref)

        w0, w1, w2 = w_ref[0:1, :], w_ref[1:2, :], w_ref[2:3, :]
        ab, ac, ax, ag, dy = ab_ref[...], ac_ref[...], ax_ref[...], ag_ref[...], dy_ref[...]
        u = ac * ax
        ext[0:8, :] = jnp.where(i > 0, hc_ref[...] * hx_ref[...], 0.0)
        ext[8:8 + TS, :] = u
        u1, u2 = ext[pl.ds(7, TS), :], ext[pl.ds(6, TS), :]
        conv = u2 * w0 + u1 * w1 + u * w2
        sg = _silu(ag)
        dconv = dy * ab * sg
        dext[0:TS, :] = dconv
        dext[TS:TS + 8, :] = jnp.where(i < nt - 1, ndy_ref[...] * nab_ref[...] * _silu(nag_ref[...]), 0.0)
        du = w2 * dconv + w1 * dext[pl.ds(1, TS), :] + w0 * dext[pl.ds(2, TS), :]
        dp_ref[:, 0:BR] = (dy * conv * sg).astype(BF16)
        dp_ref[:, BR:2 * BR] = (du * ax).astype(BF16)
        dp_ref[:, 2 * BR:3 * BR] = (du * ac).astype(BF16)
        dp_ref[:, 3 * BR:4 * BR] = (dy * ab * conv * _dsilu(ag)).astype(BF16)
        dw_ref[0:1, :] += _colsum(dconv * u2)
        dw_ref[1:2, :] += _colsum(dconv * u1)
        dw_ref[2:3, :] += _colsum(dconv * u)

    rspec = pl.BlockSpec((8, BR), lambda i: (0, 0))
    return _call(
        body, name="conv_a_bwd", grid=(nt,),
        in_specs=[_chunk(TS, BR, 0), _chunk(TS, BR, 1), _chunk(TS, BR, 2), _chunk(TS, BR, 3), _chunk(TS, BR, 0),
                  _halo_prev(TS, BR, 1), _halo_prev(TS, BR, 2), _halo_next(S, TS, BR, 0), _halo_next(S, TS, BR, 3),
                  _halo_next(S, TS, BR, 0), rspec],
        out_specs=[pl.BlockSpec((TS, 4 * BR), lambda i: (i, 0)), rspec],
        out_shape=[jax.ShapeDtypeStruct((S, 12 * BR), BF16), jax.ShapeDtypeStruct((8, BR), F32)],
        scratch_shapes=[pltpu.VMEM((TS + 8, BR), F32), pltpu.VMEM((TS + 8, BR), F32)],
        compiler_params=_params("arbitrary"),
    )(proj, proj, proj, proj, dcat, proj, proj, proj, proj, dcat, w8)


def _attn_mix_bwd(proj, dcat, os_, ls_, head_ones, BR):
    S = proj.shape[0]
    TS = _tile(S, 512)

    def body(dy_ref, g_ref, o1, o2, o3, l1, l2, l3, ones_ref, d1, d2, d3, e1, e2, e3, dg_ref):
        w1, w2, w3 = _mix_weights(l1[...], l2[...], l3[...])
        mix = w1 * o1[...] + w2 * o2[...] + w3 * o3[...]
        g = g_ref[...]
        dy = dy_ref[...]
        dmix = dy * _silu(g)
        dg_ref[...] = dy * mix * _dsilu(g)
        t = jnp.dot(dmix * mix, ones_ref[...], preferred_element_type=F32, precision=lax.Precision.HIGHEST)
        d1[...] = w1 * dmix
        d2[...] = w2 * dmix
        d3[...] = w3 * dmix
        e1[...] = -w1 * t
        e2[...] = -w2 * t
        e3[...] = -w3 * t

    spec = pl.BlockSpec((TS, BR), lambda i: (i, 0))
    outs = _call(
        body, name="attn_mix_bwd", grid=(S // TS,),
        in_specs=[_chunk(TS, BR, 1), _chunk(TS, BR, 7)] + [spec] * 6 + [pl.BlockSpec((BR, BR), lambda i: (0, 0))],
        out_specs=[spec] * 7,
        out_shape=[jax.ShapeDtypeStruct((S, BR), F32)] * 7,
        compiler_params=_params("arbitrary"),
    )(dcat, proj, *os_, *ls_, head_ones)
    return outs[0:3], outs[3:6], outs[6]


def _attn_bwd(proj, do, e, lse, bias, dil, BR):
    S = proj.shape[0]
    HD, U, NU, nbu, hpb, HBK = _attn_geometry(S, BR, dil)
    scale = HD ** -0.5
    combos = [(n, r) for n in range(nbu) for r in range(dil)]

    def body(q_ref, kc_ref, kp_ref, vc_ref, vp_ref, do_ref, e_ref, l_ref, b_ref, dq_ref, dk_ref, dv_ref, db_ref,
             sbuf, dpbuf, pbuf, dsbuf, dkacc, dvacc, nxk, nxv, cark, carv):
        j = pl.program_id(1)
        mu = NU - 1 - j
        band, is_cur = _attn_mask()
        first = jnp.logical_and(band, jnp.logical_or(is_cur, mu > 0))

        @pl.when(j == 0)
        def _():
            cark[...] = jnp.zeros_like(cark)
            carv[...] = jnp.zeros_like(carv)
            db_ref[...] = jnp.zeros_like(db_ref)

        for ref in (dkacc, dvacc, nxk, nxv):
            ref[...] = jnp.zeros_like(ref)

        def keys(ref_c, ref_p, n, r):
            prev = ref_c[_attn_rows(n - 1, r, dil), :] if n > 0 else ref_p[_attn_rows(nbu - 1, r, dil), :]
            return jnp.concatenate([prev, ref_c[_attn_rows(n, r, dil), :]], axis=0).astype(BF16)

        for c, (n, r) in enumerate(combos):
            rows = _attn_rows(n, r, dil)
            q = (q_ref[rows, :] * scale).astype(BF16)
            dob = do_ref[rows, :].astype(BF16)
            kk, vv = keys(kc_ref, kp_ref, n, r), keys(vc_ref, vp_ref, n, r)
            for h in range(hpb):
                sl = slice(h * HD, (h + 1) * HD)
                sbuf[c * hpb + h] = _dot_nt(q[:, sl], kk[:, sl])
                dpbuf[c * hpb + h] = _dot_nt(dob[:, sl], vv[:, sl])
        dbs = [None] * hpb
        for c, (n, r) in enumerate(combos):
            rows = _attn_rows(n, r, dil)
            lse, ev = l_ref[rows, :], e_ref[rows, :]
            for h in range(hpb):
                one = slice(h * HD, h * HD + 1)
                s = jnp.where(band if n > 0 else first, sbuf[c * hpb + h] + b_ref[h], NEG)
                p = jnp.exp(s - lse[:, one])
                ds = p * (dpbuf[c * hpb + h] + ev[:, one])
                pbuf[c * hpb + h] = p.astype(BF16)
                dsbuf[c * hpb + h] = ds.astype(BF16)
                dbs[h] = ds if dbs[h] is None else dbs[h] + ds
        for h in range(hpb):
            db_ref[h] += dbs[h]
        for c, (n, r) in enumerate(combos):
            rows = _attn_rows(n, r, dil)
            q = (q_ref[rows, :] * scale).astype(BF16)
            dob = do_ref[rows, :].astype(BF16)
            kk = keys(kc_ref, kp_ref, n, r)
            dq, dkk, dvv = [], [], []
            for h in range(hpb):
                sl = slice(h * HD, (h + 1) * HD)
                dsb = dsbuf[c * hpb + h]
                dq.append(_dot(dsb, kk[:, sl]) * scale)
                dkk.append(_dot_tn(dsb, q[:, sl]))
                dvv.append(_dot_tn(pbuf[c * hpb + h], dob[:, sl]))
            dq_ref[rows, :] = jnp.concatenate(dq, axis=1)
            dkk, dvv = jnp.concatenate(dkk, axis=1), jnp.concatenate(dvv, axis=1)
            dkacc[rows, :] += dkk[BLK:2 * BLK, :]
            dvacc[rows, :] += dvv[BLK:2 * BLK, :]
            if n > 0:
                prow = _attn_rows(n - 1, r, dil)
                dkacc[prow, :] += dkk[0:BLK, :]
                dvacc[prow, :] += dvv[0:BLK, :]
            else:
                prow = _attn_rows(nbu - 1, r, dil)
                nxk[prow, :] += dkk[0:BLK, :]
                nxv[prow, :] += dvv[0:BLK, :]
        dk_ref[...] = dkacc[...] + cark[...]
        dv_ref[...] = dvacc[...] + carv[...]
        cark[...] = nxk[...]
        carv[...] = nxv[...]

    def cur(c):
        return pl.BlockSpec((U, ATT_LANES), lambda hb, j: (NU - 1 - j, c * HBK + hb))

    def prev(c):
        return pl.BlockSpec((U, ATT_LANES), lambda hb, j: (jnp.maximum(NU - 2 - j, 0), c * HBK + hb))

    own = pl.BlockSpec((U, ATT_LANES), lambda hb, j: (NU - 1 - j, hb))
    bspec = pl.BlockSpec((hpb, BLK, 2 * BLK), lambda hb, j: (hb, 0, 0))
    nhb = len(combos) * hpb
    unit = pltpu.VMEM((U, ATT_LANES), F32)
    return _call(
        body, name="attn_bwd_d%d" % dil, grid=(HBK, NU),
        in_specs=[cur(4), cur(5), prev(5), cur(6), prev(6), own, own, own, bspec],
        out_specs=[own, own, own, bspec],
        out_shape=[jax.ShapeDtypeStruct((S, BR), F32)] * 3 + [jax.ShapeDtypeStruct((8, BLK, 2 * BLK), F32)],
        scratch_shapes=[pltpu.VMEM((nhb, BLK, 2 * BLK), F32)] * 2 + [pltpu.VMEM((nhb, BLK, 2 * BLK), BF16)] * 2 + [unit] * 6,
        compiler_params=_params("arbitrary", "arbitrary"),
    )(proj, proj, proj, proj, proj, do, e, lse, bias)


def _attn_dsum(ds, dbg, dproj, BR):
    S = dbg.shape[0]
    TS = _tile(S, 512)

    def body(*refs):
        o_ref = refs[11]
        for k in range(3):
            o_ref[:, k * BR:(k + 1) * BR] = (refs[k][...] + refs[3 + k][...] + refs[6 + k][...]).astype(BF16)
        o_ref[:, 3 * BR:4 * BR] = refs[9][...].astype(BF16)

    spec = pl.BlockSpec((TS, BR), lambda i: (i, 0))
    return _call(
        body, name="attn_dsum", grid=(S // TS,),
        in_specs=[spec] * 10 + [pl.BlockSpec(memory_space=pl.ANY)],
        out_specs=pl.BlockSpec((TS, 4 * BR), lambda i: (i, 1)),
        out_shape=jax.ShapeDtypeStruct(dproj.shape, BF16),
        input_output_aliases={10: 0},
        compiler_params=_params("arbitrary"),
    )(*[t for trip in ds for t in trip], dbg, dproj)


def _lru_bwd(proj, h, dcat, rows, wa, wx, dproj, BR):
    S = proj.shape[0]
    TS = _tile(S, 256)
    PAD = TS // 2
    nt = S // TS

    def body(cx_ref, hx_ref, cg_ref, h_ref, hh_ref, dy_ref, r_ref, wa_ref, wx_ref, _,
             dp_ref, sum_ref, dwa_ref, dwx_ref, ext, hext, aext, dext, sa, sb, carry):
        i = pl.program_id(0)
        ti = nt - 1 - i

        @pl.when(i == 0)
        def _():
            sa[TS:TS + PAD, :] = jnp.ones((PAD, BR), F32)
            sb[TS:TS + PAD, :] = jnp.zeros((PAD, BR), F32)
            carry[...] = jnp.zeros_like(carry)
            dext[TS:TS + 8, :] = jnp.zeros((8, BR), F32)
            sum_ref[...] = jnp.zeros_like(sum_ref)
            dwa_ref[...] = jnp.zeros_like(dwa_ref)
            dwx_ref[...] = jnp.zeros_like(dwx_ref)

        w0, w1, w2, w3 = r_ref[0:1, :], r_ref[1:2, :], r_ref[2:3, :], r_ref[3:4, :]
        sp = r_ref[7:8, :]
        cx = cx_ref[...]
        ext[0:8, :] = jnp.where(ti > 0, hx_ref[...], 0.0)
        ext[8:8 + TS, :] = cx
        x3, x2, x1 = ext[pl.ds(5, TS), :], ext[pl.ds(6, TS), :], ext[pl.ds(7, TS), :]
        xc = x3 * w0 + x2 * w1 + x1 * w2 + cx * w3 + r_ref[4:5, :]
        wa_, wx_ = wa_ref[...], wx_ref[...]
        r, ig, a, mult = _lru_gates(xc, wa_, wx_, r_ref[5:6, :], r_ref[6:7, :], sp)
        cg, dy, hv = cg_ref[...], dy_ref[...], h_ref[...]
        dp_ref[:, BR:2 * BR] = (dy * hv * _dsilu(cg)).astype(BF16)
        aext[0:TS, :] = a
        aext[TS:TS + 8, :] = jnp.broadcast_to(carry[0:1, :], (8, BR))
        sa[0:TS, :] = aext[pl.ds(1, TS), :]
        sb[0:TS, :] = dy * _silu(cg)
        d = 1
        while d < TS:
            A, B = sa[0:TS, :], sb[0:TS, :]
            As, Bs = sa[pl.ds(d, TS), :], sb[pl.ds(d, TS), :]
            sb[0:TS, :] = B + A * Bs
            sa[0:TS, :] = A * As
            d *= 2
        gh = sb[0:TS, :] + sa[0:TS, :] * carry[1:2, :]
        carry[0:1, :] = a[0:1, :]
        carry[1:2, :] = gh[0:1, :]
        hext[0:8, :] = jnp.where(ti > 0, hh_ref[...], 0.0)
        hext[8:8 + TS, :] = hv
        da = gh * hext[pl.ds(7, TS), :]
        dmult = gh * ig * xc
        dig = gh * mult * xc
        dxc = gh * mult * ig
        dla = da * a - dmult * a * a / mult
        dpr = dla * (-LRU_C) * sp * r * (1.0 - r)
        dpi = dig * ig * (1.0 - ig)
        xb, dprb, dpib = xc.astype(BF16), dpr.astype(BF16), dpi.astype(BF16)
        dwa_ref[...] += _dot_tn(xb, dprb)
        dwx_ref[...] += _dot_tn(xb, dpib)
        dxc = dxc + _dot_nt(dprb, wa_) + _dot_nt(dpib, wx_)
        sum_ref[0:1, :] += _colsum(dxc * x3)
        sum_ref[1:2, :] += _colsum(dxc * x2)
        sum_ref[2:3, :] += _colsum(dxc * x1)
        sum_ref[3:4, :] += _colsum(dxc * cx)
        sum_ref[4:5, :] += _colsum(dxc)
        sum_ref[5:6, :] += _colsum(dpr)
        sum_ref[6:7, :] += _colsum(dpi)
        sum_ref[7:8, :] += _colsum(dla * (-LRU_C) * r)
        dext[0:TS, :] = dxc
        dcx = w3 * dxc + w2 * dext[pl.ds(1, TS), :] + w1 * dext[pl.ds(2, TS), :] + w0 * dext[pl.ds(3, TS), :]
        dext[TS:TS + 8, :] = dxc[0:8, :]
        dp_ref[:, 0:BR] = dcx.astype(BF16)

    def rev(k):
        return pl.BlockSpec((TS, BR), lambda i: (nt - 1 - i, k))

    def rev_halo(k):
        return pl.BlockSpec((8, BR), lambda i: (jnp.maximum((nt - 1 - i) * (TS // 8) - 1, 0), k))

    full = pl.BlockSpec((BR, BR), lambda i: (0, 0))
    rspec = pl.BlockSpec((8, BR), lambda i: (0, 0))
    return _call(
        body, name="lru_bwd", grid=(nt,),
        in_specs=[rev(8), rev_halo(8), rev(9), rev(0), rev_halo(0), rev(2), rspec, full, full, pl.BlockSpec(memory_space=pl.ANY)],
        out_specs=[pl.BlockSpec((TS, 2 * BR), lambda i: (nt - 1 - i, 4)), rspec, full, full],
        input_output_aliases={9: 0},
        out_shape=[jax.ShapeDtypeStruct(dproj.shape, BF16), jax.ShapeDtypeStruct((8, BR), F32),
                   jax.ShapeDtypeStruct((BR, BR), F32), jax.ShapeDtypeStruct((BR, BR), F32)],
        scratch_shapes=[pltpu.VMEM((TS + 8, BR), F32)] * 4 + [pltpu.VMEM((TS + PAD, BR), F32)] * 2 + [pltpu.VMEM((8, BR), F32)],
        compiler_params=_params("arbitrary"),
    )(proj, proj, proj, h, h, dcat, rows, wa, wx, dproj)


def _s5_bwd(proj, dcat, xre_all, xim_all, qw_re, qw_im, bbre, bbim, cre, cim, rows, wg, dproj, BR):
    S = proj.shape[0]
    TS, W = qw_re.shape
    PAD = TS // 2
    nt = S // TS
    CB = 8 * S5_CH
    SB = 8 * S5_STATE
    nblk = BR // CB

    def body(u_ref, dg_ref, dy_ref, xre_ref, xim_ref, hre_ref, him_ref, qr_ref, qi_ref, bbre_ref, bbim_ref, cre_ref, cim_ref,
             r_ref, wg_ref, _, dp_ref, sum_ref, dwg_ref, da_ref, dbbre_ref, dbbim_ref, dcre_ref, dcim_ref,
             sre, sim, ere, eim, ypre_s, dyp_s, carry):
        i = pl.program_id(0)
        ti = nt - 1 - i

        @pl.when(i == 0)
        def _():
            sre[TS:TS + PAD, :] = jnp.zeros((PAD, SB), F32)
            sim[TS:TS + PAD, :] = jnp.zeros((PAD, SB), F32)
            carry[...] = jnp.zeros_like(carry)
            for ref in (sum_ref, dwg_ref, da_ref, dbbre_ref, dbbim_ref, dcre_ref, dcim_ref):
                ref[...] = jnp.zeros_like(ref)

        u, dg, dy = u_ref[...], dg_ref[...], dy_ref[...]
        for m in range(nblk):
            cs, ws = slice(m * CB, (m + 1) * CB), slice(m * SB, (m + 1) * SB)
            ypre_s[:, cs] = (_dot(xre_ref[:, ws].astype(BF16), cre_ref[ws, :]) - _dot(xim_ref[:, ws].astype(BF16), cim_ref[ws, :]))
        ypre = ypre_s[...] + r_ref[0:1, :] * u
        yg = _gelu(ypre)
        wg_ = wg_ref[...]
        s = _sigmoid(_dot(yg.astype(BF16), wg_) + r_ref[1:2, :])
        dgl = dy * _silu(dg)
        dp_ref[:, BR:2 * BR] = (dy * yg * s * _dsilu(dg)).astype(BF16)
        dps = dgl * yg * s * (1.0 - s)
        dpsb = dps.astype(BF16)
        sum_ref[1:2, :] += _colsum(dps)
        dwg_ref[...] += _dot_tn(yg.astype(BF16), dpsb)
        dyp = (dgl * s + _dot_nt(dpsb, wg_)) * _dgelu(ypre)
        sum_ref[0:1, :] += _colsum(dyp * u)
        dyp_s[...] = dyp
        for m in range(nblk):
            cs, ws = slice(m * CB, (m + 1) * CB), slice(m * SB, (m + 1) * SB)
            dypb = dyp_s[:, cs].astype(BF16)
            xre, xim = xre_ref[:, ws], xim_ref[:, ws]
            dcre_ref[ws, :] += _dot_tn(xre.astype(BF16), dypb)
            dcim_ref[ws, :] -= _dot_tn(xim.astype(BF16), dypb)
            sre[0:TS, :] = _dot_nt(dypb, cre_ref[ws, :])
            sim[0:TS, :] = -_dot_nt(dypb, cim_ref[ws, :])
            d = 1
            while d < TS:
                ar, ai = qr_ref[TS - d:TS - d + 1, ws], -qi_ref[TS - d:TS - d + 1, ws]
                xr, xi = sre[0:TS, :], sim[0:TS, :]
                sr, si = sre[pl.ds(d, TS), :], sim[pl.ds(d, TS), :]
                sre[0:TS, :] = xr + ar * sr - ai * si
                sim[0:TS, :] = xi + ar * si + ai * sr
                d *= 2
            cr, ci = carry[0:1, ws], carry[1:2, ws]
            qr, qi = qr_ref[:, ws], -qi_ref[:, ws]
            gre = sre[0:TS, :] + qr * cr - qi * ci
            gim = sim[0:TS, :] + qr * ci + qi * cr
            carry[0:1, ws] = gre[0:1, :]
            carry[1:2, ws] = gim[0:1, :]
            ere[0:8, :] = jnp.where(ti > 0, hre_ref[:, ws], 0.0)
            eim[0:8, :] = jnp.where(ti > 0, him_ref[:, ws], 0.0)
            ere[8:8 + TS, :] = xre
            eim[8:8 + TS, :] = xim
            xpr, xpi = ere[pl.ds(7, TS), :], eim[pl.ds(7, TS), :]
            da_ref[0:1, ws] += _colsum(gre * xpr + gim * xpi)
            da_ref[1:2, ws] += _colsum(gim * xpr - gre * xpi)
            greb, gimb = gre.astype(BF16), gim.astype(BF16)
            ub = u_ref[:, cs].astype(BF16)
            dbbre_ref[cs, :] += _dot_tn(ub, greb)
            dbbim_ref[cs, :] += _dot_tn(ub, gimb)
            du = _dot_nt(greb, bbre_ref[cs, :]) + _dot_nt(gimb, bbim_ref[cs, :]) + r_ref[0:1, cs] * dyp_s[:, cs]
            dp_ref[:, cs] = du.astype(BF16)

    def rev(width, k):
        return pl.BlockSpec((TS, width), lambda i: (nt - 1 - i, k))

    def rev_halo(width):
        return pl.BlockSpec((8, width), lambda i: (jnp.maximum((nt - 1 - i) * (TS // 8) - 1, 0), 0))

    def const(shape):
        return pl.BlockSpec(shape, lambda i: (0, 0))

    return _call(
        body, name="s5_bwd", grid=(nt,),
        in_specs=[rev(BR, 10), rev(BR, 11), rev(BR, 3), rev(W, 0), rev(W, 0), rev_halo(W), rev_halo(W),
                  const((TS, W)), const((TS, W)), const((BR, SB)), const((BR, SB)), const((W, CB)), const((W, CB)),
                  const((8, BR)), const((BR, BR)), pl.BlockSpec(memory_space=pl.ANY)],
        input_output_aliases={15: 0},
        out_specs=[pl.BlockSpec((TS, 2 * BR), lambda i: (nt - 1 - i, 5)), const((8, BR)), const((BR, BR)), const((8, W)),
                   const((BR, SB)), const((BR, SB)), const((W, CB)), const((W, CB))],
        out_shape=[jax.ShapeDtypeStruct(dproj.shape, BF16), jax.ShapeDtypeStruct((8, BR), F32), jax.ShapeDtypeStruct((BR, BR), F32),
                   jax.ShapeDtypeStruct((8, W), F32), jax.ShapeDtypeStruct((BR, SB), F32), jax.ShapeDtypeStruct((BR, SB), F32),
                   jax.ShapeDtypeStruct((W, CB), F32), jax.ShapeDtypeStruct((W, CB), F32)],
        scratch_shapes=[pltpu.VMEM((TS + PAD, SB), F32)] * 2 + [pltpu.VMEM((TS + 8, SB), F32)] * 2
        + [pltpu.VMEM((TS, BR), F32)] * 2 + [pltpu.VMEM((8, W), F32)],
        compiler_params=_params("arbitrary"),
    )(proj, proj, dcat, xre_all, xim_all, xre_all, xim_all, qw_re, qw_im, bbre, bbim, cre, cim, rows, wg, dproj)


_WEIGHTS = ['rel_bias', 'w_ada', 'b_ada', 'w_in', 'conv_a', 'conv_c', 'conv_c_b', 'lru_wa', 'lru_ba', 'lru_wx', 'lru_bx',
            'lru_lambda', 's5_lam_re', 's5_lam_im', 's5_log_dt', 's5_b_re', 's5_b_im', 's5_c_re', 's5_c_im', 's5_d',
            's5_w_glu', 's5_b_glu', 'w_out', 'ln_g', 'ln_b']
_LAYER_SMALL = ['conv_a', 'conv_c', 'conv_c_b', 'lru_wa', 'lru_ba', 'lru_wx', 'lru_bx', 'lru_lambda', 's5_lam_re', 's5_lam_im',
                's5_log_dt', 's5_b_re', 's5_b_im', 's5_c_re', 's5_c_im', 's5_d', 's5_b_glu']
_SMALL = ['rel_bias'] + _LAYER_SMALL + ['ln_g', 'ln_b']


def _t5_bucket(dist):
    max_exact = REL_BUCKETS // 2
    nf = jnp.maximum(dist, 1).astype(F32)
    large = max_exact + (jnp.log(nf / max_exact) / math.log(REL_MAX_DIST / max_exact)
                         * (REL_BUCKETS - max_exact)).astype(jnp.int32)
    large = jnp.minimum(large, REL_BUCKETS - 1)
    return jnp.where(dist < max_exact, dist, large)


def _bias_tables(rel_bias):
    i = jnp.arange(BLK)[:, None]
    j = jnp.arange(2 * BLK)[None, :]
    delta = i + BLK - j
    out = []
    for window, dil in DILATIONS:
        bucket = _t5_bucket(jnp.clip(delta, 0, window // dil) * dil)
        onehot = (bucket[:, :, None] == jnp.arange(REL_BUCKETS)[None, None, :]).astype(F32)
        out.append(jnp.einsum('ijb,bh->hij', onehot, rel_bias, precision=lax.Precision.HIGHEST))
    return jnp.stack(out)


def _prep_layer(q):
    eye8 = jnp.eye(8, dtype=F32)
    G = q['s5_lam_re'].shape[0]
    nblk = G // 8

    def block_diag(w):
        hd = w.shape[1]
        return (w[:, :, None, :] * eye8[:, None, :, None]).reshape(8 * hd, 8 * hd)

    def compact_b(bb):
        t = jnp.transpose(bb.reshape(nblk, 8, S5_STATE, S5_CH), (0, 1, 3, 2))
        return (t[:, :, :, None, :] * eye8[None, :, None, :, None]).reshape(G * S5_CH, 8 * S5_STATE)

    def compact_c(cc):
        t = jnp.transpose(cc.reshape(nblk, 8, S5_CH, S5_STATE), (0, 1, 3, 2))
        return (t[:, :, :, None, :] * eye8[None, :, None, :, None]).reshape(G * S5_STATE, 8 * S5_CH)

    lam_re, lam_im = q['s5_lam_re'], q['s5_lam_im']
    dt = jnp.exp(q['s5_log_dt'])[:, None]
    mag = jnp.exp(lam_re * dt)
    ab_re = mag * jnp.cos(lam_im * dt)
    ab_im = mag * jnp.sin(lam_im * dt)
    den = lam_re * lam_re + lam_im * lam_im
    f_re = ((ab_re - 1.0) * lam_re + ab_im * lam_im) / den
    f_im = (ab_im * lam_re - (ab_re - 1.0) * lam_im) / den
    b_re, b_im = q['s5_b_re'], q['s5_b_im']
    bb_re = f_re[..., None] * b_re - f_im[..., None] * b_im
    bb_im = f_re[..., None] * b_im + f_im[..., None] * b_re
    return dict(
        conv_a=q['conv_a'], conv_c=q['conv_c'], conv_c_b=q['conv_c_b'], lru_ba=q['lru_ba'], lru_bx=q['lru_bx'],
        sp=jax.nn.softplus(-q['lru_lambda']), wa=block_diag(q['lru_wa']), wx=block_diag(q['lru_wx']),
        ar=ab_re.reshape(-1), ai=ab_im.reshape(-1), bbre=compact_b(bb_re), bbim=compact_b(bb_im),
        cre=compact_c(q['s5_c_re']), cim=compact_c(q['s5_c_im']), s5_d=q['s5_d'], s5_b_glu=q['s5_b_glu'])


def _powers(ar, ai, T):
    pr, pi = ar[None, :], ai[None, :]
    while pr.shape[0] < T:
        lr, li = pr[-1:], pi[-1:]
        pr, pi = jnp.concatenate([pr, pr * lr - pi * li]), jnp.concatenate([pi, pr * li + pi * lr])
    return pr[:T], pi[:T]


def _rows8(vecs, width):
    rows = [v.reshape(1, width).astype(F32) for v in vecs]
    return jnp.concatenate(rows + [jnp.zeros((8 - len(rows), width), F32)], axis=0)


def kernel(x, c, rel_bias, w_ada, b_ada, w_in, conv_a, conv_c, conv_c_b, lru_wa, lru_ba, lru_wx, lru_bx, lru_lambda, s5_lam_re, s5_lam_im, s5_log_dt, s5_b_re, s5_b_im, s5_c_re, s5_c_im, s5_d, s5_w_glu, s5_b_glu, w_out, ln_g, ln_b, loss_target, m_rel_bias, m_w_ada, m_b_ada, m_w_in, m_conv_a, m_conv_c, m_conv_c_b, m_lru_wa, m_lru_ba, m_lru_wx, m_lru_bx, m_lru_lambda, m_s5_lam_re, m_s5_lam_im, m_s5_log_dt, m_s5_b_re, m_s5_b_im, m_s5_c_re, m_s5_c_im, m_s5_d, m_s5_w_glu, m_s5_b_glu, m_w_out, m_ln_g, m_ln_b, v_rel_bias, v_w_ada, v_b_ada, v_w_in, v_conv_a, v_conv_c, v_conv_c_b, v_lru_wa, v_lru_ba, v_lru_wx, v_lru_bx, v_lru_lambda, v_s5_lam_re, v_s5_lam_im, v_s5_log_dt, v_s5_b_re, v_s5_b_im, v_s5_c_re, v_s5_c_im, v_s5_d, v_s5_w_glu, v_s5_b_glu, v_w_out, v_ln_g, v_ln_b):
    a = dict(locals())
    me = 4 * lax.axis_index("x") + 2 * lax.axis_index("y") + lax.axis_index("c")
    x0, target = x[0], loss_target[0]
    S, D = x0.shape
    BR = D // 4
    NS = w_in.shape[2]
    SH = BR // NDEV
    assert S % (BLK * DILATIONS[-1][1]) == 0 and BR % (8 * S5_CH) == 0
    TS5 = _tile(S, 256)

    small = _exchange(_pack([c, conv_a, conv_c]), True, "gather_small").reshape(NDEV, -1)
    c_all = small[:, :D]
    na, nc = DEPTH * 3 * SH, DEPTH * 4 * SH
    conv_a_full = jnp.transpose(small[:, D:D + na].reshape(NDEV, DEPTH, 3, SH), (1, 2, 0, 3)).reshape(DEPTH, 3, BR)
    conv_c_full = jnp.transpose(small[:, D + na:D + na + nc].reshape(NDEV, DEPTH, 4, SH), (1, 2, 0, 3)).reshape(DEPTH, 4, BR)
    me1 = me.reshape(1).astype(jnp.int32)
    w_in_bf, w_out_bf, w_glu_bf = w_in.astype(BF16), w_out.astype(BF16), s5_w_glu.astype(BF16)

    b_cols = lax.dynamic_slice(b_ada, (0, me * NS), (DEPTH, NS)).reshape(DEPTH, 1, NS)
    ada_all = _exchange(_ada_cols(c_all, w_ada, b_cols), True, "gather_ada")
    ada_me = lax.dynamic_index_in_dim(ada_all, me, axis=2, keepdims=False)
    ada_me = jnp.transpose(ada_me, (1, 0, 2)).reshape(DEPTH, 3 * D)
    shift, scale, gate = ada_me[:, :D], ada_me[:, D:2 * D], ada_me[:, 2 * D:]

    bias_tabs, bias_pull = jax.vjp(_bias_tables, rel_bias)
    HD = BR // 8
    head_ones = jnp.kron(jnp.eye(8, dtype=F32), jnp.ones((HD, HD), F32))

    saved = []
    xl = x0
    for l in range(DEPTH):
        q = {n: a[n][l] for n in _LAYER_SMALL}
        q['conv_a'], q['conv_c'] = conv_a_full[l], conv_c_full[l]
        kp, pull = jax.vjp(_prep_layer, q)
        pw_re, pw_im = _powers(lax.stop_gradient(kp['ar']), lax.stop_gradient(kp['ai']), TS5)
        mod = _rows8([1.0 + scale[l], shift[l]], D)
        arows = _rows8(list(kp['conv_a']), BR)
        lrows = _rows8(list(kp['conv_c']) + [kp['conv_c_b'], kp['lru_ba'], kp['lru_bx'], kp['sp']], BR)
        srows = _rows8([kp['s5_d'], kp['s5_b_glu']], BR)
        orows = _rows8([1.0 + gate[l], ln_g[l], ln_b[l]], D)
        wa, wx = kp['wa'].astype(BF16), kp['wx'].astype(BF16)
        s5w = [kp[n].astype(BF16) for n in ('bbre', 'bbim', 'cre', 'cim')]

        hbf = _modulate(xl, mod)
        proj, w_in_l, w_out_l, w_glu_l = _proj_gather(hbf, w_in_bf[l], [w_out_bf[l], w_glu_bf[l]], me1)
        w_out_l, w_glu_l = w_out_l.reshape(D, D), w_glu_l.reshape(BR, BR)
        ya = _conv_a_fwd(proj, arows, BR)
        os_, ls_ = [], []
        for g, (_, dil) in enumerate(DILATIONS):
            o, lse = _attn_fwd(proj, bias_tabs[g], dil, BR)
            os_.append(o)
            ls_.append(lse)
        yb = _attn_mix_fwd(proj, os_, ls_, BR)
        yc, hs = _lru_fwd(proj, lrows, wa, wx, BR)
        yd, xre, xim = _s5_fwd(proj, pw_re, pw_im, *s5w, srows, w_glu_l, BR)
        xn, y, cat = _out_ln([ya, yb, yc, yd], w_out_l, xl, orows)
        saved.append(dict(x=xl, proj=proj, hbf=hbf, os=os_, ls=ls_, hs=hs, xre=xre, xim=xim, y=y, cat=cat, pull=pull,
                          mod=mod, arows=arows, lrows=lrows, srows=srows, orows=orows, wa=wa, wx=wx, s5w=s5w,
                          qw=(pw_re[::-1], pw_im[::-1]), w_in=w_in_l, w_out=w_out_l, w_glu=w_glu_l))
        xl = xn

    dout, loss_local = _loss_grad(xl, target)

    dbias = jnp.zeros_like(bias_tabs)
    lgrads, dada, dw_in_parts, dw_out_parts, dw_glu_parts = [None] * DEPTH, [None] * DEPTH, [None] * DEPTH, [None] * DEPTH, [None] * DEPTH
    for l in reversed(range(DEPTH)):
        sv = saved[l]
        proj = sv['proj']
        dxres, dyb, dcat, lnsum = _ln_bwd(dout, sv['x'], sv['y'], sv['orows'], sv['w_out'])
        dw_out_parts[l] = _dw_scatter(sv['cat'], dyb, D // NDEV, True, me1, "dw_out_scatter")
        dproj, asum = _conv_a_bwd(proj, dcat, sv['arows'], BR)
        dos, es, dbg = _attn_mix_bwd(proj, dcat, sv['os'], sv['ls'], head_ones, BR)
        dqkv, dbs = [], []
        for g, (_, dil) in enumerate(DILATIONS):
            dq_, dk_, dv_, db_ = _attn_bwd(proj, dos[g], es[g], sv['ls'][g], bias_tabs[g], dil, BR)
            dqkv.append((dq_, dk_, dv_))
            dbs.append(db_)
        dbias = dbias + jnp.stack(dbs)
        dproj = _attn_dsum(dqkv, dbg, dproj, BR)
        dproj, lsum, dwa, dwx = _lru_bwd(proj, sv['hs'], dcat, sv['lrows'], sv['wa'], sv['wx'], dproj, BR)
        dproj, ssum, dwg, da_, dbbre, dbbim, dcre, dcim = _s5_bwd(
            proj, dcat, sv['xre'], sv['xim'], *sv['qw'], *sv['s5w'], sv['srows'], sv['w_glu'], dproj, BR)
        dw_glu_parts[l] = dwg.reshape(NDEV, BR // NDEV, BR)
        dout, msum = _dh(dproj, sv['w_in'], dxres, sv['x'], sv['mod'])
        dw_in_parts[l] = _dw_scatter(sv['hbf'], dproj, NS, False, me1, "dw_in_scatter")
        dada[l] = jnp.concatenate([msum[1], msum[0], lnsum[2]])
        dkp = dict(conv_a=asum[0:3], conv_c=lsum[0:4], conv_c_b=lsum[4], lru_ba=lsum[5], lru_bx=lsum[6], sp=lsum[7],
                   wa=dwa, wx=dwx, ar=da_[0], ai=da_[1], bbre=dbbre, bbim=dbbim, cre=dcre, cim=dcim,
                   s5_d=ssum[0], s5_b_glu=ssum[1])
        lgrads[l] = dict(sv['pull'](dkp)[0], ln_g=lnsum[0], ln_b=lnsum[1])
    grad_x = dout[None]

    out = {}

    def put(name, res):
        out[name] = res

    def stacked(fn):
        res = [fn(l) for l in range(DEPTH)]
        return tuple(jnp.stack([r[k] for r in res]) for k in range(4))

    put('w_in', stacked(lambda l: _adamw(dw_in_parts[l], w_in[l], m_w_in[l], v_w_in[l], "adamw_w_in")))
    put('w_out', stacked(lambda l: _adamw(dw_out_parts[l], w_out[l], m_w_out[l], v_w_out[l], "adamw_w_out")))
    put('s5_w_glu', stacked(lambda l: _adamw(_exchange(dw_glu_parts[l], False, "scatter_w_glu"), s5_w_glu[l], m_s5_w_glu[l], v_s5_w_glu[l], "adamw_w_glu")))

    dada_all = _exchange(jnp.stack(dada), True, "gather_dada")
    dada_cols = jnp.transpose(lax.dynamic_slice(dada_all, (0, 0, me * NS), (NDEV, DEPTH, NS)), (1, 0, 2))
    gw_ada, gb_ada = _ada_grads(c_all, dada_all, dada_cols)
    res = _adamw(gw_ada.reshape(DEPTH * D, NS), w_ada.reshape(DEPTH * D, NS), m_w_ada.reshape(DEPTH * D, NS),
                 v_w_ada.reshape(DEPTH * D, NS), "adamw_w_ada")
    put('w_ada', tuple(r.reshape(DEPTH, D, NS) for r in res))

    local = {'rel_bias': bias_pull(dbias)[0]}
    for n in _LAYER_SMALL + ['ln_g', 'ln_b']:
        local[n] = jnp.stack([lgrads[l][n] for l in range(DEPTH)])
    local['loss'] = loss_local.reshape(1)
    packed = _SMALL + ['loss']
    shapes = [local[n].shape for n in packed]
    summed = _sum8(_exchange(_pack([local[n] for n in packed]), True, "gather_small_grads"), "sum_small_grads")
    gs = dict(zip(packed, _unpack(summed, shapes)))
    loss = gs['loss'][0]
    gs['conv_a'] = lax.dynamic_slice_in_dim(gs['conv_a'], me * SH, SH, axis=2)
    gs['conv_c'] = lax.dynamic_slice_in_dim(gs['conv_c'], me * SH, SH, axis=2)
    gs['b_ada'] = gb_ada
    names = _SMALL + ['b_ada']
    shapes = [a[n].shape for n in names]
    res = _adamw(_pack([gs[n] for n in names]), _pack([a[n] for n in names]), _pack([a['m_' + n] for n in names]),
                 _pack([a['v_' + n] for n in names]), "adamw_small")
    unpacked = [_unpack(r, shapes) for r in res]
    for k, n in enumerate(names):
        put(n, tuple(unpacked[j][k] for j in range(4)))

    return (loss, grad_x, *[out[n][0] for n in _WEIGHTS], *[out[n][1] for n in _WEIGHTS],
            *[out[n][2] for n in _WEIGHTS], *[out[n][3] for n in _WEIGHTS])
```

```python
import functools
import math

import jax
import jax.numpy as jnp
from jax import lax
from jax.experimental import pallas as pl
from jax.experimental.pallas import tpu as pltpu

F32 = jnp.float32
BF16 = jnp.bfloat16

NDEV = 8
DEPTH = 2
BLK = 128
DILATIONS = ((128, 1), (512, 4), (2048, 16))
REL_BUCKETS = 32
REL_MAX_DIST = 2048
LRU_C = 8.0
S5_CH = 16
S5_STATE = 64
ALPHA = (2 * DEPTH) ** 0.25
LN_EPS = 1e-5
ADAM_LR, ADAM_B1, ADAM_B2, ADAM_EPS, ADAM_WD, ADAM_STEP = 0.001, 0.9, 0.999, 1e-08, 0.01, 10
NEG = -1e30
VMEM_LIMIT = 56 * 1024 * 1024


def _call(body, **kw):
    return pl.pallas_call(body, **kw)


def _params(*sem):
    return pltpu.CompilerParams(dimension_semantics=sem, vmem_limit_bytes=VMEM_LIMIT)


def _sigmoid(x):
    return 1.0 / (1.0 + jnp.exp(-x))


def _silu(x):
    return x * _sigmoid(x)


def _dsilu(x):
    s = _sigmoid(x)
    return s * (1.0 + x * (1.0 - s))


_GC = math.sqrt(2.0 / math.pi)


def _gelu(x):
    return 0.5 * x * (1.0 + jnp.tanh(_GC * (x + 0.044715 * x * x * x)))


def _dgelu(x):
    t = jnp.tanh(_GC * (x + 0.044715 * x * x * x))
    return 0.5 * (1.0 + t) + 0.5 * x * (1.0 - t * t) * _GC * (1.0 + 3.0 * 0.044715 * x * x)


def _expm1(x):
    series = x * (1.0 + x * (1.0 / 2) * (1.0 + x * (1.0 / 3) * (1.0 + x * (1.0 / 4) * (1.0 + x * (1.0 / 5) * (1.0 + x * (1.0 / 6))))))
    return jnp.where(jnp.abs(x) < 0.25, series, jnp.exp(x) - 1.0)


def _dot(a, b):
    return jnp.dot(a, b, preferred_element_type=F32)


def _dot_nt(a, b):
    return lax.dot_general(a, b, (((1,), (1,)), ((), ())), preferred_element_type=F32)


def _dot_tn(a, b):
    return lax.dot_general(a, b, (((0,), (0,)), ((), ())), preferred_element_type=F32)


def _colsum(v):
    return jnp.sum(v, axis=0, keepdims=True)


def _exchange(x, gather, name):
    blk = x.shape if gather else x.shape[1:]

    def body(x_ref, o_ref, send_sems, recv_sems, local_sem):
        ix, iy, ic = lax.axis_index("x"), lax.axis_index("y"), lax.axis_index("c")
        me = 4 * ix + 2 * iy + ic

        def src(dev):
            return x_ref if gather else x_ref.at[dev]

        local = pltpu.make_async_copy(src(me), o_ref.at[me], local_sem)
        local.start()
        copies = []
        for k in range(1, NDEV):
            px = 1 - ix if k & 4 else ix
            py = 1 - iy if k & 2 else iy
            pc = 1 - ic if k & 1 else ic
            peer = 4 * px + 2 * py + pc
            cp = pltpu.make_async_remote_copy(
                src_ref=src(peer), dst_ref=o_ref.at[me], send_sem=send_sems.at[k - 1], recv_sem=recv_sems.at[k - 1],
                device_id=(px, py, pc), device_id_type=pl.DeviceIdType.MESH)
            cp.start()
            copies.append(cp)
        for cp in copies:
            cp.wait()
        local.wait()

    return _call(
        body, name=name,
        out_shape=jax.ShapeDtypeStruct((NDEV,) + tuple(blk), x.dtype),
        in_specs=[pl.BlockSpec(memory_space=pl.ANY)],
        out_specs=pl.BlockSpec(memory_space=pl.ANY),
        scratch_shapes=[pltpu.SemaphoreType.DMA((NDEV - 1,)), pltpu.SemaphoreType.DMA((NDEV - 1,)), pltpu.SemaphoreType.DMA],
    )(x)


def _pack(arrs):
    flat = jnp.concatenate([a.reshape(-1).astype(F32) for a in arrs])
    n = flat.shape[0]
    quantum = 1024 if n <= 65536 else 65536
    npad = -(-n // quantum) * quantum
    return jnp.pad(flat, (0, npad - n)).reshape(npad // 128, 128)


def _unpack(packed, shapes):
    flat = packed.reshape(-1)
    out, off = [], 0
    for s in shapes:
        n = math.prod(s)
        out.append(flat[off:off + n].reshape(s))
        off += n
    return out


def _sum8(parts, name):
    _, R, C = parts.shape
    TR = R
    for cand in (512, 256, 128, 64, 32, 16, 8):
        if R % cand == 0:
            TR = cand
            break

    def body(p_ref, o_ref):
        acc = p_ref[0]
        for d in range(1, NDEV):
            acc = acc + p_ref[d]
        o_ref[...] = acc

    return _call(
        body, name=name, grid=(R // TR,),
        in_specs=[pl.BlockSpec((NDEV, TR, C), lambda i: (0, i, 0))],
        out_specs=pl.BlockSpec((TR, C), lambda i: (i, 0)),
        out_shape=jax.ShapeDtypeStruct((R, C), F32),
        compiler_params=_params("arbitrary"),
    )(parts)


def _adamw(g_parts, w, m, v, name):
    R, C = w.shape
    nparts = g_parts.shape[0] if g_parts.ndim == 3 else 0
    TR = R
    for cand in (256, 128, 64, 32, 16, 8):
        if R % cand == 0:
            TR = cand
            break

    def body(g_ref, w_ref, m_ref, v_ref, go_ref, d_ref, mo_ref, vo_ref):
        if nparts:
            g = g_ref[0].astype(F32)
            for d in range(1, nparts):
                g = g + g_ref[d].astype(F32)
        else:
            g = g_ref[...]
        m2 = ADAM_B1 * m_ref[...] + (1.0 - ADAM_B1) * g
        v2 = ADAM_B2 * v_ref[...] + (1.0 - ADAM_B2) * (g * g)
        m_hat = m2 / (1.0 - ADAM_B1 ** ADAM_STEP)
        v_hat = v2 / (1.0 - ADAM_B2 ** ADAM_STEP)
        go_ref[...] = g
        d_ref[...] = -ADAM_LR * (m_hat / (jnp.sqrt(v_hat) + ADAM_EPS) + ADAM_WD * w_ref[...])
        mo_ref[...] = m2
        vo_ref[...] = v2

    spec = pl.BlockSpec((TR, C), lambda i: (i, 0))
    gspec = pl.BlockSpec((nparts, TR, C), lambda i: (0, i, 0)) if nparts else spec
    return _call(
        body, name=name, grid=(R // TR,),
        in_specs=[gspec, spec, spec, spec], out_specs=[spec] * 4,
        out_shape=[jax.ShapeDtypeStruct((R, C), F32)] * 4,
        compiler_params=_params("arbitrary"),
    )(g_parts, w, m, v)


def _ada_cols(c_all, w_ada, b_cols):
    L, D, NS = w_ada.shape

    def body(c_ref, w_ref, b_ref, o_ref):
        cond = _silu(c_ref[...]).astype(BF16)
        o_ref[0] = _dot(cond, w_ref[0].astype(BF16)) + b_ref[0]

    return _call(
        body, name="ada_cols", grid=(L,),
        in_specs=[pl.BlockSpec((NDEV, D), lambda l: (0, 0)), pl.BlockSpec((1, D, NS), lambda l: (l, 0, 0)),
                  pl.BlockSpec((1, 1, NS), lambda l: (l, 0, 0))],
        out_specs=pl.BlockSpec((1, NDEV, NS), lambda l: (l, 0, 0)),
        out_shape=jax.ShapeDtypeStruct((L, NDEV, NS), F32),
        compiler_params=_params("arbitrary"),
    )(c_all, w_ada, b_cols)


def _ada_grads(c_all, dada_all, dada_cols):
    _, D = c_all.shape
    L, _, NS = dada_cols.shape
    D3 = dada_all.shape[2]

    def body(c_ref, da_ref, dc_ref, gw_ref, gb_ref):
        cond = _silu(c_ref[...]).astype(BF16)
        gw_ref[0] = _dot_tn(cond, dc_ref[0].astype(BF16))
        acc = da_ref[0]
        for d in range(1, NDEV):
            acc = acc + da_ref[d]
        gb_ref[...] = acc

    return _call(
        body, name="ada_grads", grid=(L,),
        in_specs=[pl.BlockSpec((NDEV, D), lambda l: (0, 0)), pl.BlockSpec((NDEV, L, D3), lambda l: (0, 0, 0)),
                  pl.BlockSpec((1, NDEV, NS), lambda l: (l, 0, 0))],
        out_specs=[pl.BlockSpec((1, D, NS), lambda l: (l, 0, 0)), pl.BlockSpec((L, D3), lambda l: (0, 0))],
        out_shape=[jax.ShapeDtypeStruct((L, D, NS), F32), jax.ShapeDtypeStruct((L, D3), F32)],
        compiler_params=_params("arbitrary"),
    )(c_all, dada_all, dada_cols)


def _tile(S, want):
    return want if S % want == 0 else S


def _modulate(x, mod):
    S, D = x.shape
    TS = _tile(S, 512)

    def body(x_ref, mod_ref, h_ref):
        h_ref[...] = (x_ref[...] * mod_ref[0:1, :] + mod_ref[1:2, :]).astype(BF16)

    spec = pl.BlockSpec((TS, D), lambda i: (i, 0))
    return _call(
        body, name="modulate", grid=(S // TS,),
        in_specs=[spec, pl.BlockSpec((8, D), lambda i: (0, 0))], out_specs=spec,
        out_shape=jax.ShapeDtypeStruct((S, D), BF16),
        compiler_params=_params("arbitrary"),
    )(x, mod)


def _peer(k):
    ix, iy, ic = lax.axis_index("x"), lax.axis_index("y"), lax.axis_index("c")
    bx, by, bc = (k >> 2) & 1, (k >> 1) & 1, k & 1
    px, py, pc = ix + bx - 2 * ix * bx, iy + by - 2 * iy * by, ic + bc - 2 * ic * bc
    return (px, py, pc), 4 * px + 2 * py + pc


GATHER_ORDER = (0, 1, 2, 4, 3, 5, 6, 7)
SCATTER_ORDER = (2, 4, 6, 3, 5, 7, 1, 0)


def _offset(order, k):
    off = jnp.int32(order[-1])
    for step in reversed(range(len(order) - 1)):
        off = jnp.where(k == step, jnp.int32(order[step]), off)
    return off


def _proj_gather(h, w_shard, extras, me1):
    assert GATHER_ORDER[0] == 0
    S, D = h.shape
    NS = w_shard.shape[1]
    TS = _tile(S, 512)
    nt = S // TS
    ne = len(extras)

    def body(me_ref, h_ref, w_ref, *rest):
        x_refs, o_ref, wg_ref = rest[:ne], rest[ne], rest[ne + 1]
        xg_refs = rest[ne + 2:2 * ne + 2]
        wbuf, send_sems, recv_sems, local_sems, load_sems = rest[2 * ne + 2:]
        k, i = pl.program_id(0), pl.program_id(1)
        me = me_ref[0]
        off = _offset(GATHER_ORDER, k)

        def push(src, dst, which, kk):
            peer, _ = _peer(kk)
            return pltpu.make_async_remote_copy(
                src_ref=src, dst_ref=dst.at[me], send_sem=send_sems.at[which, kk], recv_sem=recv_sems.at[which, kk],
                device_id=peer, device_id_type=pl.DeviceIdType.MESH)

        def own(src, dst, which):
            return pltpu.make_async_copy(src, dst.at[me], local_sems.at[which])

        pairs = [(w_ref, wg_ref)] + list(zip(x_refs, xg_refs))

        @pl.when(jnp.logical_and(k == 0, i == 0))
        def _():
            for which, (src, dst) in enumerate(pairs):
                own(src, dst, which).start()
                for kk in GATHER_ORDER[1:]:
                    push(src, dst, which, kk).start()
            first = pltpu.make_async_copy(w_ref, wbuf.at[0], load_sems.at[0])
            first.start()
            first.wait()

        @pl.when(jnp.logical_and(k > 0, i == 0))
        def _():
            push(w_ref, wg_ref, 0, off).wait_recv()
            _, blk = _peer(off)
            load = pltpu.make_async_copy(wg_ref.at[blk], wbuf.at[k % 2], load_sems.at[k % 2])
            load.start()
            load.wait()

        o_ref[...] = _dot(h_ref[...], wbuf[k % 2])

        @pl.when(jnp.logical_and(k == NDEV - 1, i == nt - 1))
        def _():
            for which, (src, dst) in enumerate(pairs):
                for kk in range(1, NDEV):
                    cp = push(src, dst, which, kk)
                    cp.wait_send()
                    if which > 0:
                        cp.wait_recv()
                own(src, dst, which).wait()

    def col(k, i, me_ref):
        m, off = me_ref[0], _offset(GATHER_ORDER, k)
        return i, (m | off) - (m & off)

    anyspec = pl.BlockSpec(memory_space=pl.ANY)
    grid_spec = pltpu.PrefetchScalarGridSpec(
        num_scalar_prefetch=1, grid=(NDEV, nt),
        in_specs=[pl.BlockSpec((TS, D), lambda k, i, me_ref: (i, 0)), anyspec] + [anyspec] * ne,
        out_specs=[pl.BlockSpec((TS, NS), col), anyspec] + [anyspec] * ne,
        scratch_shapes=[pltpu.VMEM((2, D, NS), BF16), pltpu.SemaphoreType.DMA((1 + ne, NDEV)), pltpu.SemaphoreType.DMA((1 + ne, NDEV)),
                        pltpu.SemaphoreType.DMA((1 + ne,)), pltpu.SemaphoreType.DMA((2,))])
    return _call(
        body, name="proj_gather", grid_spec=grid_spec,
        out_shape=[jax.ShapeDtypeStruct((S, NDEV * NS), F32), jax.ShapeDtypeStruct((NDEV, D, NS), BF16)]
        + [jax.ShapeDtypeStruct((NDEV,) + e.shape, e.dtype) for e in extras],
        compiler_params=_params("arbitrary", "arbitrary"),
    )(me1, h, w_shard, *extras)


def _dw_scatter(a, b, blk, by_rows, me1, name):
    assert SCATTER_ORDER[-1] == 0
    S = a.shape[0]
    TK = _tile(S, 512)
    nk = S // TK
    shape = (blk, b.shape[1]) if by_rows else (a.shape[1], blk)
    last = NDEV - 1

    def body(me_ref, a_ref, b_ref, recv_ref, acc, stage, send_sems, recv_sems, local_sem):
        k, kk = pl.program_id(0), pl.program_id(1)
        me = me_ref[0]

        def push(off, slot):
            peer, _ = _peer(off)
            return pltpu.make_async_remote_copy(
                src_ref=stage.at[slot], dst_ref=recv_ref.at[me], send_sem=send_sems.at[off], recv_sem=recv_sems.at[off],
                device_id=peer, device_id_type=pl.DeviceIdType.MESH)

        own = pltpu.make_async_copy(stage.at[last % 2], recv_ref.at[me], local_sem)

        @pl.when(kk == 0)
        def _():
            acc[...] = jnp.zeros_like(acc)
        acc[...] += _dot_tn(a_ref[...], b_ref[...])

        @pl.when(kk == nk - 1)
        def _():
            @pl.when(k >= 2)
            def _():
                push(_offset(SCATTER_ORDER, k - 2), k % 2).wait_send()

            stage[k % 2] = acc[...].astype(BF16)

            @pl.when(k < last)
            def _():
                push(_offset(SCATTER_ORDER, k), k % 2).start()

            @pl.when(k == last)
            def _():
                own.start()
                push(SCATTER_ORDER[last - 1], (last - 1) % 2).wait_send()
                own.wait()
                for off in range(1, NDEV):
                    push(off, 0).wait_recv()

    def blk_of(k, me_ref):
        m, off = me_ref[0], _offset(SCATTER_ORDER, k)
        return (m | off) - (m & off)

    if by_rows:
        in_specs = [pl.BlockSpec((TK, blk), lambda k, kk, me_ref: (kk, blk_of(k, me_ref))),
                    pl.BlockSpec((TK, b.shape[1]), lambda k, kk, me_ref: (kk, 0))]
    else:
        in_specs = [pl.BlockSpec((TK, a.shape[1]), lambda k, kk, me_ref: (kk, 0)),
                    pl.BlockSpec((TK, blk), lambda k, kk, me_ref: (kk, blk_of(k, me_ref)))]
    grid_spec = pltpu.PrefetchScalarGridSpec(
        num_scalar_prefetch=1, grid=(NDEV, nk), in_specs=in_specs,
        out_specs=pl.BlockSpec(memory_space=pl.ANY),
        scratch_shapes=[pltpu.VMEM(shape, F32), pltpu.VMEM((2,) + shape, BF16), pltpu.SemaphoreType.DMA((NDEV,)),
                        pltpu.SemaphoreType.DMA((NDEV,)), pltpu.SemaphoreType.DMA])
    return _call(
        body, name=name, grid_spec=grid_spec,
        out_shape=jax.ShapeDtypeStruct((NDEV,) + shape, BF16),
        compiler_params=_params("arbitrary", "arbitrary"),
    )(me1, a, b)


def _chunk(TS, BR, k):
    return pl.BlockSpec((TS, BR), lambda i: (i, k))


def _halo_prev(TS, BR, k):
    return pl.BlockSpec((8, BR), lambda i: (jnp.maximum(i * (TS // 8) - 1, 0), k))


def _conv_a_fwd(proj, w8, BR):
    S = proj.shape[0]
    TS = _tile(S, 512)

    def body(ab_ref, ac_ref, ax_ref, ag_ref, hc_ref, hx_ref, w_ref, o_ref, ext):
        i = pl.program_id(0)
        u = ac_ref[...] * ax_ref[...]
        ext[0:8, :] = jnp.where(i > 0, hc_ref[...] * hx_ref[...], 0.0)
        ext[8:8 + TS, :] = u
        conv = ext[pl.ds(6, TS), :] * w_ref[0:1, :] + ext[pl.ds(7, TS), :] * w_ref[1:2, :] + u * w_ref[2:3, :]
        o_ref[...] = ab_ref[...] * conv * _silu(ag_ref[...])

    return _call(
        body, name="conv_a_fwd", grid=(S // TS,),
        in_specs=[_chunk(TS, BR, 0), _chunk(TS, BR, 1), _chunk(TS, BR, 2), _chunk(TS, BR, 3),
                  _halo_prev(TS, BR, 1), _halo_prev(TS, BR, 2), pl.BlockSpec((8, BR), lambda i: (0, 0))],
        out_specs=pl.BlockSpec((TS, BR), lambda i: (i, 0)),
        out_shape=jax.ShapeDtypeStruct((S, BR), F32),
        scratch_shapes=[pltpu.VMEM((TS + 8, BR), F32)],
        compiler_params=_params("arbitrary"),
    )(proj, proj, proj, proj, proj, proj, w8)


ATT_UNIT = 2048
ATT_LANES = 128


def _attn_mask():
    i = lax.broadcasted_iota(jnp.int32, (BLK, 2 * BLK), 0)
    j = lax.broadcasted_iota(jnp.int32, (BLK, 2 * BLK), 1)
    return jnp.logical_and(j >= i, j <= i + BLK), j >= BLK


def _attn_rows(n, r, dil):
    return pl.ds(n * BLK * dil + r, BLK, stride=dil) if dil > 1 else pl.ds(n * BLK, BLK)


def _attn_geometry(S, BR, dil):
    HD = BR // 8
    U = ATT_UNIT
    assert S % U == 0 and BR % ATT_LANES == 0 and ATT_LANES % HD == 0 and U % (BLK * dil) == 0
    return HD, U, S // U, U // (BLK * dil), ATT_LANES // HD, BR // ATT_LANES


def _attn_fwd(proj, bias, dil, BR):
    S = proj.shape[0]
    HD, U, NU, nbu, hpb, HBK = _attn_geometry(S, BR, dil)
    scale = HD ** -0.5
    combos = [(n, r) for n in range(nbu) for r in range(dil)]

    def body(q_ref, kc_ref, kp_ref, vc_ref, vp_ref, b_ref, o_ref, l_ref, sbuf, pbuf):
        m = pl.program_id(1)
        band, is_cur = _attn_mask()
        first = jnp.logical_and(band, jnp.logical_or(is_cur, m > 0))

        def keys(ref_c, ref_p, n, r):
            prev = ref_c[_attn_rows(n - 1, r, dil), :] if n > 0 else ref_p[_attn_rows(nbu - 1, r, dil), :]
            return jnp.concatenate([prev, ref_c[_attn_rows(n, r, dil), :]], axis=0).astype(BF16)

        for c, (n, r) in enumerate(combos):
            q = (q_ref[_attn_rows(n, r, dil), :] * scale).astype(BF16)
            kk = keys(kc_ref, kp_ref, n, r)
            for h in range(hpb):
                sl = slice(h * HD, (h + 1) * HD)
                sbuf[c * hpb + h] = _dot_nt(q[:, sl], kk[:, sl])
        for c, (n, r) in enumerate(combos):
            lses = []
            for h in range(hpb):
                s = jnp.where(band if n > 0 else first, sbuf[c * hpb + h] + b_ref[h], NEG)
                mx = jnp.max(s, axis=-1, keepdims=True)
                p = jnp.exp(s - mx)
                l = jnp.sum(p, axis=-1, keepdims=True)
                pbuf[c * hpb + h] = (p * (1.0 / l)).astype(BF16)
                lses.append(jnp.broadcast_to(mx + jnp.log(l), (BLK, HD)))
            l_ref[_attn_rows(n, r, dil), :] = jnp.concatenate(lses, axis=1)
        for c, (n, r) in enumerate(combos):
            vv = keys(vc_ref, vp_ref, n, r)
            o_ref[_attn_rows(n, r, dil), :] = jnp.concatenate(
                [_dot(pbuf[c * hpb + h], vv[:, h * HD:(h + 1) * HD]) for h in range(hpb)], axis=1)

    def cur(c):
        return pl.BlockSpec((U, ATT_LANES), lambda hb, m: (m, c * HBK + hb))

    def prev(c):
        return pl.BlockSpec((U, ATT_LANES), lambda hb, m: (jnp.maximum(m - 1, 0), c * HBK + hb))

    ospec = pl.BlockSpec((U, ATT_LANES), lambda hb, m: (m, hb))
    nhb = len(combos) * hpb
    return _call(
        body, name="attn_fwd_d%d" % dil, grid=(HBK, NU),
        in_specs=[cur(4), cur(5), prev(5), cur(6), prev(6), pl.BlockSpec((hpb, BLK, 2 * BLK), lambda hb, m: (hb, 0, 0))],
        out_specs=[ospec, ospec],
        out_shape=[jax.ShapeDtypeStruct((S, BR), F32)] * 2,
        scratch_shapes=[pltpu.VMEM((nhb, BLK, 2 * BLK), F32), pltpu.VMEM((nhb, BLK, 2 * BLK), BF16)],
        compiler_params=_params("arbitrary", "arbitrary"),
    )(proj, proj, proj, proj, proj, bias)


def _mix_weights(l1, l2, l3):
    m = jnp.maximum(jnp.maximum(l1, l2), l3)
    e1, e2, e3 = jnp.exp(l1 - m), jnp.exp(l2 - m), jnp.exp(l3 - m)
    inv = 1.0 / (e1 + e2 + e3)
    return e1 * inv, e2 * inv, e3 * inv


def _attn_mix_fwd(proj, os_, ls_, BR):
    S = proj.shape[0]
    TS = _tile(S, 512)

    def body(o1, o2, o3, l1, l2, l3, g_ref, y_ref):
        w1, w2, w3 = _mix_weights(l1[...], l2[...], l3[...])
        y_ref[...] = (w1 * o1[...] + w2 * o2[...] + w3 * o3[...]) * _silu(g_ref[...])

    spec = pl.BlockSpec((TS, BR), lambda i: (i, 0))
    return _call(
        body, name="attn_mix_fwd", grid=(S // TS,),
        in_specs=[spec] * 6 + [_chunk(TS, BR, 7)], out_specs=spec,
        out_shape=jax.ShapeDtypeStruct((S, BR), F32),
        compiler_params=_params("arbitrary"),
    )(*os_, *ls_, proj)


def _lru_gates(xc, wa, wx, ba, bx, sp):
    xb = xc.astype(BF16)
    r = _sigmoid(_dot(xb, wa) + ba)
    ig = _sigmoid(_dot(xb, wx) + bx)
    la = -LRU_C * r * sp
    a = jnp.exp(la)
    mult = jnp.sqrt(-_expm1(2.0 * la))
    return r, ig, a, mult


def _lru_fwd(proj, rows, wa, wx, BR):
    S = proj.shape[0]
    TS = _tile(S, 256)
    PAD = TS // 2

    def body(cx_ref, hx_ref, cg_ref, r_ref, wa_ref, wx_ref, y_ref, h_ref, ext, sa, sb, carry):
        i = pl.program_id(0)

        @pl.when(i == 0)
        def _():
            sa[0:PAD, :] = jnp.ones((PAD, BR), F32)
            sb[0:PAD, :] = jnp.zeros((PAD, BR), F32)
            carry[...] = jnp.zeros_like(carry)

        cx = cx_ref[...]
        ext[0:8, :] = jnp.where(i > 0, hx_ref[...], 0.0)
        ext[8:8 + TS, :] = cx
        xc = (ext[pl.ds(5, TS), :] * r_ref[0:1, :] + ext[pl.ds(6, TS), :] * r_ref[1:2, :]
              + ext[pl.ds(7, TS), :] * r_ref[2:3, :] + cx * r_ref[3:4, :] + r_ref[4:5, :])
        _, ig, a, mult = _lru_gates(xc, wa_ref[...], wx_ref[...], r_ref[5:6, :], r_ref[6:7, :], r_ref[7:8, :])
        sa[PAD:PAD + TS, :] = a
        sb[PAD:PAD + TS, :] = mult * ig * xc
        d = 1
        while d < TS:
            A, B = sa[PAD:PAD + TS, :], sb[PAD:PAD + TS, :]
            As, Bs = sa[pl.ds(PAD - d, TS), :], sb[pl.ds(PAD - d, TS), :]
            sb[PAD:PAD + TS, :] = A * Bs + B
            sa[PAD:PAD + TS, :] = A * As
            d *= 2
        h = sb[PAD:PAD + TS, :] + sa[PAD:PAD + TS, :] * carry[0:1, :]
        carry[0:1, :] = h[TS - 1:TS, :]
        h_ref[...] = h
        y_ref[...] = h * _silu(cg_ref[...])

    full = pl.BlockSpec((BR, BR), lambda i: (0, 0))
    spec = pl.BlockSpec((TS, BR), lambda i: (i, 0))
    return _call(
        body, name="lru_fwd", grid=(S // TS,),
        in_specs=[_chunk(TS, BR, 8), _halo_prev(TS, BR, 8), _chunk(TS, BR, 9), pl.BlockSpec((8, BR), lambda i: (0, 0)), full, full],
        out_specs=[spec, spec],
        out_shape=[jax.ShapeDtypeStruct((S, BR), F32)] * 2,
        scratch_shapes=[pltpu.VMEM((TS + 8, BR), F32), pltpu.VMEM((PAD + TS, BR), F32), pltpu.VMEM((PAD + TS, BR), F32),
                        pltpu.VMEM((8, BR), F32)],
        compiler_params=_params("arbitrary"),
    )(proj, proj, proj, rows, wa, wx)


S5_LAGS = 8
S5_CB = 8 * S5_CH
S5_SB = 8 * S5_STATE


def _s5_fwd(proj, a8, bbl, cre, cim, rows, wg, BR):
    S = proj.shape[0]
    W = a8.shape[1]
    TS = _tile(S, 256)
    CB, SB, J = S5_CB, S5_SB, S5_LAGS
    nblk = BR // CB

    def body(u_ref, uh_ref, dg_ref, a8_ref, bbl_ref, cre_ref, cim_ref, r_ref, wg_ref,
             y_ref, xre_ref, xim_ref, uext, ulag, sre, sim, ypre_s, carry):
        i = pl.program_id(0)

        @pl.when(i == 0)
        def _():
            carry[...] = jnp.zeros_like(carry)

        uext[0:8, :] = jnp.where(i > 0, uh_ref[...], 0.0)
        uext[8:8 + TS, :] = u_ref[...]
        for m in range(nblk):
            cs, ws = slice(m * CB, (m + 1) * CB), slice(m * SB, (m + 1) * SB)
            for j in range(J):
                ulag[:, j * CB:(j + 1) * CB] = uext[pl.ds(8 - j, TS), cs].astype(BF16)
            w = _dot(ulag[...], bbl_ref[m])
            sre[...] = w[:, 0:SB]
            sim[...] = w[:, SB:2 * SB]
            ar = jnp.broadcast_to(a8_ref[0:1, ws], (8, SB))
            ai = jnp.broadcast_to(a8_ref[1:2, ws], (8, SB))
            xr, xi = carry[0:8, ws], carry[8:16, ws]
            for g in range(TS // 8):
                rg = slice(8 * g, 8 * g + 8)
                xr, xi = sre[rg, :] + ar * xr - ai * xi, sim[rg, :] + ar * xi + ai * xr
                sre[rg, :] = xr
                sim[rg, :] = xi
            carry[0:8, ws] = xr
            carry[8:16, ws] = xi
            xre, xim = sre[...], sim[...]
            xre_ref[:, ws] = xre
            xim_ref[:, ws] = xim
            ypre_s[:, cs] = _dot(xre.astype(BF16), cre_ref[ws, :]) - _dot(xim.astype(BF16), cim_ref[ws, :])
        dg = dg_ref[...]
        yg = _gelu(ypre_s[...] + r_ref[0:1, :] * u_ref[...])
        s = _sigmoid(_dot(yg.astype(BF16), wg_ref[...]) + r_ref[1:2, :])
        y_ref[...] = yg * s * _silu(dg)

    def const(shape):
        return pl.BlockSpec(shape, lambda i: (0,) * len(shape))

    return _call(
        body, name="s5_fwd", grid=(S // TS,),
        in_specs=[_chunk(TS, BR, 10), _halo_prev(TS, BR, 10), _chunk(TS, BR, 11), const((8, W)), const((nblk, J * CB, 2 * SB)),
                  const((W, CB)), const((W, CB)), const((8, BR)), const((BR, BR))],
        out_specs=[pl.BlockSpec((TS, BR), lambda i: (i, 0)), pl.BlockSpec((TS, W), lambda i: (i, 0)), pl.BlockSpec((TS, W), lambda i: (i, 0))],
        out_shape=[jax.ShapeDtypeStruct((S, BR), F32), jax.ShapeDtypeStruct((S, W), F32), jax.ShapeDtypeStruct((S, W), F32)],
        scratch_shapes=[pltpu.VMEM((TS + 8, BR), F32), pltpu.VMEM((TS, J * CB), BF16), pltpu.VMEM((TS, SB), F32), pltpu.VMEM((TS, SB), F32),
                        pltpu.VMEM((TS, BR), F32), pltpu.VMEM((16, W), F32)],
        compiler_params=_params("arbitrary"),
    )(proj, proj, proj, a8, bbl, cre, cim, rows, wg)


def _out_ln(ys, w_out, x, rows):
    S, D = x.shape
    BR = D // 4
    TS = _tile(S, 256)

    def body(ya, yb, yc, yd, w_ref, x_ref, r_ref, o_ref, y_ref, cat_ref):
        cat = jnp.concatenate([ya[...], yb[...], yc[...], yd[...]], axis=1).astype(BF16)
        y = _dot(cat, w_ref[...])
        z = ALPHA * x_ref[...] + r_ref[0:1, :] * y
        mu = jnp.mean(z, axis=-1, keepdims=True)
        zc = z - mu
        var = jnp.mean(zc * zc, axis=-1, keepdims=True)
        o_ref[...] = zc * lax.rsqrt(var + LN_EPS) * r_ref[1:2, :] + r_ref[2:3, :]
        y_ref[...] = y
        cat_ref[...] = cat

    yspec = pl.BlockSpec((TS, BR), lambda i: (i, 0))
    spec = pl.BlockSpec((TS, D), lambda i: (i, 0))
    return _call(
        body, name="out_ln", grid=(S // TS,),
        in_specs=[yspec] * 4 + [pl.BlockSpec((D, D), lambda i: (0, 0)), spec, pl.BlockSpec((8, D), lambda i: (0, 0))],
        out_specs=[spec, spec, spec],
        out_shape=[jax.ShapeDtypeStruct((S, D), F32), jax.ShapeDtypeStruct((S, D), F32), jax.ShapeDtypeStruct((S, D), BF16)],
        compiler_params=_params("arbitrary"),
    )(*ys, w_out, x, rows)


def _loss_grad(x, target):
    S, D = x.shape
    TS = _tile(S, 512)

    def body(x_ref, t_ref, g_ref, l_ref):
        @pl.when(pl.program_id(0) == 0)
        def _():
            l_ref[...] = jnp.zeros_like(l_ref)
        diff = x_ref[...] - t_ref[...]
        g_ref[...] = diff * (1.0 / D)
        l_ref[...] += jnp.sum(_colsum(diff * diff), axis=1, keepdims=True) * (0.5 / D)

    spec = pl.BlockSpec((TS, D), lambda i: (i, 0))
    g, l = _call(
        body, name="loss_grad", grid=(S // TS,),
        in_specs=[spec, spec], out_specs=[spec, pl.BlockSpec((1, 1), lambda i: (0, 0))],
        out_shape=[jax.ShapeDtypeStruct((S, D), F32), jax.ShapeDtypeStruct((1, 1), F32)],
        compiler_params=_params("arbitrary"),
    )(x, target)
    return g, l[0, 0]


def _ln_bwd(dout, x, y, rows, w_out):
    S, D = x.shape
    TS = _tile(S, 256)

    def body(do_ref, x_ref, y_ref, r_ref, w_ref, dxr_ref, dyb_ref, dcat_ref, acc_ref):
        @pl.when(pl.program_id(0) == 0)
        def _():
            acc_ref[...] = jnp.zeros_like(acc_ref)
        g1, lg = r_ref[0:1, :], r_ref[1:2, :]
        yv = y_ref[...]
        z = ALPHA * x_ref[...] + g1 * yv
        mu = jnp.mean(z, axis=-1, keepdims=True)
        zc = z - mu
        var = jnp.mean(zc * zc, axis=-1, keepdims=True)
        rstd = lax.rsqrt(var + LN_EPS)
        xhat = zc * rstd
        do = do_ref[...]
        dxh = do * lg
        dz = rstd * (dxh - jnp.mean(dxh, axis=-1, keepdims=True) - xhat * jnp.mean(dxh * xhat, axis=-1, keepdims=True))
        dxr_ref[...] = ALPHA * dz
        dyb = (g1 * dz).astype(BF16)
        dyb_ref[...] = dyb
        dcat_ref[...] = _dot_nt(dyb, w_ref[...])
        acc_ref[0:1, :] += _colsum(do * xhat)
        acc_ref[1:2, :] += _colsum(do)
        acc_ref[2:3, :] += _colsum(dz * yv)

    spec = pl.BlockSpec((TS, D), lambda i: (i, 0))
    rspec = pl.BlockSpec((8, D), lambda i: (0, 0))
    return _call(
        body, name="ln_bwd", grid=(S // TS,),
        in_specs=[spec, spec, spec, rspec, pl.BlockSpec((D, D), lambda i: (0, 0))],
        out_specs=[spec, spec, spec, rspec],
        out_shape=[jax.ShapeDtypeStruct((S, D), F32), jax.ShapeDtypeStruct((S, D), BF16), jax.ShapeDtypeStruct((S, D), F32),
                   jax.ShapeDtypeStruct((8, D), F32)],
        compiler_params=_params("arbitrary"),
    )(dout, x, y, rows, w_out)


def _dh(dproj, wg, dxres, x, mod):
    S, D = x.shape
    NS = wg.shape[2]
    TS = _tile(S, 512)

    def body(dp_ref, w_ref, dxr_ref, x_ref, mod_ref, dx_ref, sum_ref, acc):
        i, j = pl.program_id(0), pl.program_id(1)

        @pl.when(jnp.logical_and(i == 0, j == 0))
        def _():
            sum_ref[...] = jnp.zeros_like(sum_ref)

        @pl.when(j == 0)
        def _():
            acc[...] = jnp.zeros_like(acc)

        acc[...] += _dot_nt(dp_ref[...], w_ref[0])

        @pl.when(j == NDEV - 1)
        def _():
            dh = acc[...]
            dx_ref[...] = dxr_ref[...] + dh * mod_ref[0:1, :]
            sum_ref[0:1, :] += _colsum(dh * x_ref[...])
            sum_ref[1:2, :] += _colsum(dh)

    spec = pl.BlockSpec((TS, D), lambda i, j: (i, 0))
    rspec = pl.BlockSpec((8, D), lambda i, j: (0, 0))
    return _call(
        body, name="dh", grid=(S // TS, NDEV),
        in_specs=[pl.BlockSpec((TS, NS), lambda i, j: (i, j)), pl.BlockSpec((1, D, NS), lambda i, j: (j, 0, 0)), spec, spec, rspec],
        out_specs=[spec, rspec],
        out_shape=[jax.ShapeDtypeStruct((S, D), F32), jax.ShapeDtypeStruct((8, D), F32)],
        scratch_shapes=[pltpu.VMEM((TS, D), F32)],
        compiler_params=_params("arbitrary", "arbitrary"),
    )(dproj, wg, dxres, x, mod)


def _halo_next(S, TS, BR, k):
    last = S // 8 - 1
    return pl.BlockSpec((8, BR), lambda i: (jnp.minimum((i + 1) * (TS // 8), last), k))


def _conv_a_bwd(proj, dcat, w8, BR):
    S = proj.shape[0]
    TS = _tile(S, 512)
    nt = S // TS

    def body(ab_ref, ac_ref, ax_ref, ag_ref, dy_ref, hc_ref, hx_ref, nab_ref, nag_ref, ndy_ref, w_ref, dp_ref, dw_ref, ext, dext):
        i = pl.program_id(0)

        @pl.when(i == 0)
        def _():
            dw_ref[...] = jnp.zeros_like(dw_ref)

        w0, w1, w2 = w_ref[0:1, :], w_ref[1:2, :], w_ref[2:3, :]
        ab, ac, ax, ag, dy = ab_ref[...], ac_ref[...], ax_ref[...], ag_ref[...], dy_ref[...]
        u = ac * ax
        ext[0:8, :] = jnp.where(i > 0, hc_ref[...] * hx_ref[...], 0.0)
        ext[8:8 + TS, :] = u
        u1, u2 = ext[pl.ds(7, TS), :], ext[pl.ds(6, TS), :]
        conv = u2 * w0 + u1 * w1 + u * w2
        sg = _silu(ag)
        dconv = dy * ab * sg
        dext[0:TS, :] = dconv
        dext[TS:TS + 8, :] = jnp.where(i < nt - 1, ndy_ref[...] * nab_ref[...] * _silu(nag_ref[...]), 0.0)
        du = w2 * dconv + w1 * dext[pl.ds(1, TS), :] + w0 * dext[pl.ds(2, TS), :]
        dp_ref[:, 0:BR] = (dy * conv * sg).astype(BF16)
        dp_ref[:, BR:2 * BR] = (du * ax).astype(BF16)
        dp_ref[:, 2 * BR:3 * BR] = (du * ac).astype(BF16)
        dp_ref[:, 3 * BR:4 * BR] = (dy * ab * conv * _dsilu(ag)).astype(BF16)
        dw_ref[0:1, :] += _colsum(dconv * u2)
        dw_ref[1:2, :] += _colsum(dconv * u1)
        dw_ref[2:3, :] += _colsum(dconv * u)

    rspec = pl.BlockSpec((8, BR), lambda i: (0, 0))
    return _call(
        body, name="conv_a_bwd", grid=(nt,),
        in_specs=[_chunk(TS, BR, 0), _chunk(TS, BR, 1), _chunk(TS, BR, 2), _chunk(TS, BR, 3), _chunk(TS, BR, 0),
                  _halo_prev(TS, BR, 1), _halo_prev(TS, BR, 2), _halo_next(S, TS, BR, 0), _halo_next(S, TS, BR, 3),
                  _halo_next(S, TS, BR, 0), rspec],
        out_specs=[pl.BlockSpec((TS, 4 * BR), lambda i: (i, 0)), rspec],
        out_shape=[jax.ShapeDtypeStruct((S, 12 * BR), BF16), jax.ShapeDtypeStruct((8, BR), F32)],
        scratch_shapes=[pltpu.VMEM((TS + 8, BR), F32), pltpu.VMEM((TS + 8, BR), F32)],
        compiler_params=_params("arbitrary"),
    )(proj, proj, proj, proj, dcat, proj, proj, proj, proj, dcat, w8)


def _attn_mix_bwd(proj, dcat, os_, ls_, head_ones, BR):
    S = proj.shape[0]
    TS = _tile(S, 512)

    def body(dy_ref, g_ref, o1, o2, o3, l1, l2, l3, ones_ref, d1, d2, d3, e1, e2, e3, dg_ref):
        w1, w2, w3 = _mix_weights(l1[...], l2[...], l3[...])
        mix = w1 * o1[...] + w2 * o2[...] + w3 * o3[...]
        g = g_ref[...]
        dy = dy_ref[...]
        dmix = dy * _silu(g)
        dg_ref[...] = dy * mix * _dsilu(g)
        t = jnp.dot(dmix * mix, ones_ref[...], preferred_element_type=F32, precision=lax.Precision.HIGHEST)
        d1[...] = w1 * dmix
        d2[...] = w2 * dmix
        d3[...] = w3 * dmix
        e1[...] = -w1 * t
        e2[...] = -w2 * t
        e3[...] = -w3 * t

    spec = pl.BlockSpec((TS, BR), lambda i: (i, 0))
    outs = _call(
        body, name="attn_mix_bwd", grid=(S // TS,),
        in_specs=[_chunk(TS, BR, 1), _chunk(TS, BR, 7)] + [spec] * 6 + [pl.BlockSpec((BR, BR), lambda i: (0, 0))],
        out_specs=[spec] * 7,
        out_shape=[jax.ShapeDtypeStruct((S, BR), F32)] * 7,
        compiler_params=_params("arbitrary"),
    )(dcat, proj, *os_, *ls_, head_ones)
    return outs[0:3], outs[3:6], outs[6]


def _attn_bwd(proj, do, e, lse, bias, dil, BR):
    S = proj.shape[0]
    HD, U, NU, nbu, hpb, HBK = _attn_geometry(S, BR, dil)
    scale = HD ** -0.5
    combos = [(n, r) for n in range(nbu) for r in range(dil)]

    def body(q_ref, kc_ref, kp_ref, vc_ref, vp_ref, do_ref, e_ref, l_ref, b_ref, dq_ref, dk_ref, dv_ref, db_ref,
             sbuf, dpbuf, pbuf, dsbuf, dkacc, dvacc, nxk, nxv, cark, carv):
        j = pl.program_id(1)
        mu = NU - 1 - j
        band, is_cur = _attn_mask()
        first = jnp.logical_and(band, jnp.logical_or(is_cur, mu > 0))

        @pl.when(j == 0)
        def _():
            cark[...] = jnp.zeros_like(cark)
            carv[...] = jnp.zeros_like(carv)
            db_ref[...] = jnp.zeros_like(db_ref)

        for ref in (dkacc, dvacc, nxk, nxv):
            ref[...] = jnp.zeros_like(ref)

        def keys(ref_c, ref_p, n, r):
            prev = ref_c[_attn_rows(n - 1, r, dil), :] if n > 0 else ref_p[_attn_rows(nbu - 1, r, dil), :]
            return jnp.concatenate([prev, ref_c[_attn_rows(n, r, dil), :]], axis=0).astype(BF16)

        for c, (n, r) in enumerate(combos):
            rows = _attn_rows(n, r, dil)
            q = (q_ref[rows, :] * scale).astype(BF16)
            dob = do_ref[rows, :].astype(BF16)
            kk, vv = keys(kc_ref, kp_ref, n, r), keys(vc_ref, vp_ref, n, r)
            for h in range(hpb):
                sl = slice(h * HD, (h + 1) * HD)
                sbuf[c * hpb + h] = _dot_nt(q[:, sl], kk[:, sl])
                dpbuf[c * hpb + h] = _dot_nt(dob[:, sl], vv[:, sl])
        dbs = [None] * hpb
        for c, (n, r) in enumerate(combos):
            rows = _attn_rows(n, r, dil)
            lse, ev = l_ref[rows, :], e_ref[rows, :]
            for h in range(hpb):
                one = slice(h * HD, h * HD + 1)
                s = jnp.where(band if n > 0 else first, sbuf[c * hpb + h] + b_ref[h], NEG)
                p = jnp.exp(s - lse[:, one])
                ds = p * (dpbuf[c * hpb + h] + ev[:, one])
                pbuf[c * hpb + h] = p.astype(BF16)
                dsbuf[c * hpb + h] = ds.astype(BF16)
                dbs[h] = ds if dbs[h] is None else dbs[h] + ds
        for h in range(hpb):
            db_ref[h] += dbs[h]
        for c, (n, r) in enumerate(combos):
            rows = _attn_rows(n, r, dil)
            q = (q_ref[rows, :] * scale).astype(BF16)
            dob = do_ref[rows, :].astype(BF16)
            kk = keys(kc_ref, kp_ref, n, r)
            dq, dkk, dvv = [], [], []
            for h in range(hpb):
                sl = slice(h * HD, (h + 1) * HD)
                dsb = dsbuf[c * hpb + h]
                dq.append(_dot(dsb, kk[:, sl]) * scale)
                dkk.append(_dot_tn(dsb, q[:, sl]))
                dvv.append(_dot_tn(pbuf[c * hpb + h], dob[:, sl]))
            dq_ref[rows, :] = jnp.concatenate(dq, axis=1)
            dkk, dvv = jnp.concatenate(dkk, axis=1), jnp.concatenate(dvv, axis=1)
            dkacc[rows, :] += dkk[BLK:2 * BLK, :]
            dvacc[rows, :] += dvv[BLK:2 * BLK, :]
            if n > 0:
                prow = _attn_rows(n - 1, r, dil)
                dkacc[prow, :] += dkk[0:BLK, :]
                dvacc[prow, :] += dvv[0:BLK, :]
            else:
                prow = _attn_rows(nbu - 1, r, dil)
                nxk[prow, :] += dkk[0:BLK, :]
                nxv[prow, :] += dvv[0:BLK, :]
        dk_ref[...] = dkacc[...] + cark[...]
        dv_ref[...] = dvacc[...] + carv[...]
        cark[...] = nxk[...]
        carv[...] = nxv[...]

    def cur(c):
        return pl.BlockSpec((U, ATT_LANES), lambda hb, j: (NU - 1 - j, c * HBK + hb))

    def prev(c):
        return pl.BlockSpec((U, ATT_LANES), lambda hb, j: (jnp.maximum(NU - 2 - j, 0), c * HBK + hb))

    own = pl.BlockSpec((U, ATT_LANES), lambda hb, j: (NU - 1 - j, hb))
    bspec = pl.BlockSpec((hpb, BLK, 2 * BLK), lambda hb, j: (hb, 0, 0))
    nhb = len(combos) * hpb
    unit = pltpu.VMEM((U, ATT_LANES), F32)
    return _call(
        body, name="attn_bwd_d%d" % dil, grid=(HBK, NU),
        in_specs=[cur(4), cur(5), prev(5), cur(6), prev(6), own, own, own, bspec],
        out_specs=[own, own, own, bspec],
        out_shape=[jax.ShapeDtypeStruct((S, BR), F32)] * 3 + [jax.ShapeDtypeStruct((8, BLK, 2 * BLK), F32)],
        scratch_shapes=[pltpu.VMEM((nhb, BLK, 2 * BLK), F32)] * 2 + [pltpu.VMEM((nhb, BLK, 2 * BLK), BF16)] * 2 + [unit] * 6,
        compiler_params=_params("arbitrary", "arbitrary"),
    )(proj, proj, proj, proj, proj, do, e, lse, bias)


def _attn_dsum(ds, dbg, dproj, BR):
    S = dbg.shape[0]
    TS = _tile(S, 512)

    def body(*refs):
        o_ref = refs[11]
        for k in range(3):
            o_ref[:, k * BR:(k + 1) * BR] = (refs[k][...] + refs[3 + k][...] + refs[6 + k][...]).astype(BF16)
        o_ref[:, 3 * BR:4 * BR] = refs[9][...].astype(BF16)

    spec = pl.BlockSpec((TS, BR), lambda i: (i, 0))
    return _call(
        body, name="attn_dsum", grid=(S // TS,),
        in_specs=[spec] * 10 + [pl.BlockSpec(memory_space=pl.ANY)],
        out_specs=pl.BlockSpec((TS, 4 * BR), lambda i: (i, 1)),
        out_shape=jax.ShapeDtypeStruct(dproj.shape, BF16),
        input_output_aliases={10: 0},
        compiler_params=_params("arbitrary"),
    )(*[t for trip in ds for t in trip], dbg, dproj)


def _lru_bwd(proj, h, dcat, rows, wa, wx, dproj, BR):
    S = proj.shape[0]
    TS = _tile(S, 256)
    PAD = TS // 2
    nt = S // TS

    def body(cx_ref, hx_ref, cg_ref, h_ref, hh_ref, dy_ref, r_ref, wa_ref, wx_ref, _,
             dp_ref, sum_ref, dwa_ref, dwx_ref, ext, hext, aext, dext, sa, sb, carry):
        i = pl.program_id(0)
        ti = nt - 1 - i

        @pl.when(i == 0)
        def _():
            sa[TS:TS + PAD, :] = jnp.ones((PAD, BR), F32)
            sb[TS:TS + PAD, :] = jnp.zeros((PAD, BR), F32)
            carry[...] = jnp.zeros_like(carry)
            dext[TS:TS + 8, :] = jnp.zeros((8, BR), F32)
            sum_ref[...] = jnp.zeros_like(sum_ref)
            dwa_ref[...] = jnp.zeros_like(dwa_ref)
            dwx_ref[...] = jnp.zeros_like(dwx_ref)

        w0, w1, w2, w3 = r_ref[0:1, :], r_ref[1:2, :], r_ref[2:3, :], r_ref[3:4, :]
        sp = r_ref[7:8, :]
        cx = cx_ref[...]
        ext[0:8, :] = jnp.where(ti > 0, hx_ref[...], 0.0)
        ext[8:8 + TS, :] = cx
        x3, x2, x1 = ext[pl.ds(5, TS), :], ext[pl.ds(6, TS), :], ext[pl.ds(7, TS), :]
        xc = x3 * w0 + x2 * w1 + x1 * w2 + cx * w3 + r_ref[4:5, :]
        wa_, wx_ = wa_ref[...], wx_ref[...]
        r, ig, a, mult = _lru_gates(xc, wa_, wx_, r_ref[5:6, :], r_ref[6:7, :], sp)
        cg, dy, hv = cg_ref[...], dy_ref[...], h_ref[...]
        dp_ref[:, BR:2 * BR] = (dy * hv * _dsilu(cg)).astype(BF16)
        aext[0:TS, :] = a
        aext[TS:TS + 8, :] = jnp.broadcast_to(carry[0:1, :], (8, BR))
        sa[0:TS, :] = aext[pl.ds(1, TS), :]
        sb[0:TS, :] = dy * _silu(cg)
        d = 1
        while d < TS:
            A, B = sa[0:TS, :], sb[0:TS, :]
            As, Bs = sa[pl.ds(d, TS), :], sb[pl.ds(d, TS), :]
            sb[0:TS, :] = B + A * Bs
            sa[0:TS, :] = A * As
            d *= 2
        gh = sb[0:TS, :] + sa[0:TS, :] * carry[1:2, :]
        carry[0:1, :] = a[0:1, :]
        carry[1:2, :] = gh[0:1, :]
        hext[0:8, :] = jnp.where(ti > 0, hh_ref[...], 0.0)
        hext[8:8 + TS, :] = hv
        da = gh * hext[pl.ds(7, TS), :]
        dmult = gh * ig * xc
        dig = gh * mult * xc
        dxc = gh * mult * ig
        dla = da * a - dmult * a * a / mult
        dpr = dla * (-LRU_C) * sp * r * (1.0 - r)
        dpi = dig * ig * (1.0 - ig)
        xb, dprb, dpib = xc.astype(BF16), dpr.astype(BF16), dpi.astype(BF16)
        dwa_ref[...] += _dot_tn(xb, dprb)
        dwx_ref[...] += _dot_tn(xb, dpib)
        dxc = dxc + _dot_nt(dprb, wa_) + _dot_nt(dpib, wx_)
        sum_ref[0:1, :] += _colsum(dxc * x3)
        sum_ref[1:2, :] += _colsum(dxc * x2)
        sum_ref[2:3, :] += _colsum(dxc * x1)
        sum_ref[3:4, :] += _colsum(dxc * cx)
        sum_ref[4:5, :] += _colsum(dxc)
        sum_ref[5:6, :] += _colsum(dpr)
        sum_ref[6:7, :] += _colsum(dpi)
        sum_ref[7:8, :] += _colsum(dla * (-LRU_C) * r)
        dext[0:TS, :] = dxc
        dcx = w3 * dxc + w2 * dext[pl.ds(1, TS), :] + w1 * dext[pl.ds(2, TS), :] + w0 * dext[pl.ds(3, TS), :]
        dext[TS:TS + 8, :] = dxc[0:8, :]
        dp_ref[:, 0:BR] = dcx.astype(BF16)

    def rev(k):
        return pl.BlockSpec((TS, BR), lambda i: (nt - 1 - i, k))

    def rev_halo(k):
        return pl.BlockSpec((8, BR), lambda i: (jnp.maximum((nt - 1 - i) * (TS // 8) - 1, 0), k))

    full = pl.BlockSpec((BR, BR), lambda i: (0, 0))
    rspec = pl.BlockSpec((8, BR), lambda i: (0, 0))
    return _call(
        body, name="lru_bwd", grid=(nt,),
        in_specs=[rev(8), rev_halo(8), rev(9), rev(0), rev_halo(0), rev(2), rspec, full, full, pl.BlockSpec(memory_space=pl.ANY)],
        out_specs=[pl.BlockSpec((TS, 2 * BR), lambda i: (nt - 1 - i, 4)), rspec, full, full],
        input_output_aliases={9: 0},
        out_shape=[jax.ShapeDtypeStruct(dproj.shape, BF16), jax.ShapeDtypeStruct((8, BR), F32),
                   jax.ShapeDtypeStruct((BR, BR), F32), jax.ShapeDtypeStruct((BR, BR), F32)],
        scratch_shapes=[pltpu.VMEM((TS + 8, BR), F32)] * 4 + [pltpu.VMEM((TS + PAD, BR), F32)] * 2 + [pltpu.VMEM((8, BR), F32)],
        compiler_params=_params("arbitrary"),
    )(proj, proj, proj, h, h, dcat, rows, wa, wx, dproj)


def _s5_bwd(proj, dcat, xre_all, xim_all, a8, cl, bbre, bbim, cre, cim, rows, wg, dproj, BR):
    S = proj.shape[0]
    W = a8.shape[1]
    TS = _tile(S, 256)
    nt = S // TS
    CB, SB, J = S5_CB, S5_SB, S5_LAGS
    nblk = BR // CB

    def body(u_ref, dg_ref, dy_ref, xre_ref, xim_ref, hre_ref, him_ref, a8_ref, cl_ref, bbre_ref, bbim_ref, cre_ref, cim_ref,
             r_ref, wg_ref, _, dp_ref, sum_ref, dwg_ref, da_ref, dbbre_ref, dbbim_ref, dcre_ref, dcim_ref,
             sre, sim, ere, eim, ypre_s, dext, dylag, carry):
        i = pl.program_id(0)
        ti = nt - 1 - i

        @pl.when(i == 0)
        def _():
            dext[TS:TS + 8, :] = jnp.zeros((8, BR), F32)
            carry[...] = jnp.zeros_like(carry)
            for ref in (sum_ref, dwg_ref, da_ref, dbbre_ref, dbbim_ref, dcre_ref, dcim_ref):
                ref[...] = jnp.zeros_like(ref)

        u, dg, dy = u_ref[...], dg_ref[...], dy_ref[...]
        for m in range(nblk):
            cs, ws = slice(m * CB, (m + 1) * CB), slice(m * SB, (m + 1) * SB)
            ypre_s[:, cs] = (_dot(xre_ref[:, ws].astype(BF16), cre_ref[ws, :]) - _dot(xim_ref[:, ws].astype(BF16), cim_ref[ws, :]))
        ypre = ypre_s[...] + r_ref[0:1, :] * u
        yg = _gelu(ypre)
        wg_ = wg_ref[...]
        s = _sigmoid(_dot(yg.astype(BF16), wg_) + r_ref[1:2, :])
        dgl = dy * _silu(dg)
        dp_ref[:, BR:2 * BR] = (dy * yg * s * _dsilu(dg)).astype(BF16)
        dps = dgl * yg * s * (1.0 - s)
        dpsb = dps.astype(BF16)
        sum_ref[1:2, :] += _colsum(dps)
        dwg_ref[...] += _dot_tn(yg.astype(BF16), dpsb)
        dyp = (dgl * s + _dot_nt(dpsb, wg_)) * _dgelu(ypre)
        sum_ref[0:1, :] += _colsum(dyp * u)
        dext[0:TS, :] = dyp
        for m in range(nblk):
            cs, ws = slice(m * CB, (m + 1) * CB), slice(m * SB, (m + 1) * SB)
            dypb = dext[0:TS, cs].astype(BF16)
            xre, xim = xre_ref[:, ws], xim_ref[:, ws]
            dcre_ref[ws, :] += _dot_tn(xre.astype(BF16), dypb)
            dcim_ref[ws, :] -= _dot_tn(xim.astype(BF16), dypb)
            for j in range(J):
                dylag[:, j * CB:(j + 1) * CB] = dext[pl.ds(j, TS), cs].astype(BF16)
            w = _dot(dylag[...], cl_ref[m])
            sre[...] = w[:, 0:SB]
            sim[...] = w[:, SB:2 * SB]
            ar = jnp.broadcast_to(a8_ref[0:1, ws], (8, SB))
            ai = jnp.broadcast_to(a8_ref[1:2, ws], (8, SB))
            gr, gi = carry[0:8, ws], carry[8:16, ws]
            for g in reversed(range(TS // 8)):
                rg = slice(8 * g, 8 * g + 8)
                gr, gi = sre[rg, :] + ar * gr + ai * gi, sim[rg, :] + ar * gi - ai * gr
                sre[rg, :] = gr
                sim[rg, :] = gi
            carry[0:8, ws] = gr
            carry[8:16, ws] = gi
            gre, gim = sre[...], sim[...]
            ere[0:8, :] = jnp.where(ti > 0, hre_ref[:, ws], 0.0)
            eim[0:8, :] = jnp.where(ti > 0, him_ref[:, ws], 0.0)
            ere[8:8 + TS, :] = xre
            eim[8:8 + TS, :] = xim
            xpr, xpi = ere[pl.ds(7, TS), :], eim[pl.ds(7, TS), :]
            da_ref[0:1, ws] += _colsum(gre * xpr + gim * xpi)
            da_ref[1:2, ws] += _colsum(gim * xpr - gre * xpi)
            greb, gimb = gre.astype(BF16), gim.astype(BF16)
            ub = u_ref[:, cs].astype(BF16)
            dbbre_ref[cs, :] += _dot_tn(ub, greb)
            dbbim_ref[cs, :] += _dot_tn(ub, gimb)
            du = _dot_nt(greb, bbre_ref[cs, :]) + _dot_nt(gimb, bbim_ref[cs, :]) + r_ref[0:1, cs] * dext[0:TS, cs]
            dp_ref[:, cs] = du.astype(BF16)
        dext[TS:TS + 8, :] = dext[0:8, :]

    def rev(width, k):
        return pl.BlockSpec((TS, width), lambda i: (nt - 1 - i, k))

    def rev_halo(width):
        return pl.BlockSpec((8, width), lambda i: (jnp.maximum((nt - 1 - i) * (TS // 8) - 1, 0), 0))

    def const(shape):
        return pl.BlockSpec(shape, lambda i: (0,) * len(shape))

    return _call(
        body, name="s5_bwd", grid=(nt,),
        in_specs=[rev(BR, 10), rev(BR, 11), rev(BR, 3), rev(W, 0), rev(W, 0), rev_halo(W), rev_halo(W),
                  const((8, W)), const((nblk, J * CB, 2 * SB)), const((BR, SB)), const((BR, SB)), const((W, CB)), const((W, CB)),
                  const((8, BR)), const((BR, BR)), pl.BlockSpec(memory_space=pl.ANY)],
        input_output_aliases={15: 0},
        out_specs=[pl.BlockSpec((TS, 2 * BR), lambda i: (nt - 1 - i, 5)), const((8, BR)), const((BR, BR)), const((8, W)),
                   const((BR, SB)), const((BR, SB)), const((W, CB)), const((W, CB))],
        out_shape=[jax.ShapeDtypeStruct(dproj.shape, BF16), jax.ShapeDtypeStruct((8, BR), F32), jax.ShapeDtypeStruct((BR, BR), F32),
                   jax.ShapeDtypeStruct((8, W), F32), jax.ShapeDtypeStruct((BR, SB), F32), jax.ShapeDtypeStruct((BR, SB), F32),
                   jax.ShapeDtypeStruct((W, CB), F32), jax.ShapeDtypeStruct((W, CB), F32)],
        scratch_shapes=[pltpu.VMEM((TS, SB), F32)] * 2 + [pltpu.VMEM((TS + 8, SB), F32)] * 2
        + [pltpu.VMEM((TS, BR), F32), pltpu.VMEM((TS + 8, BR), F32), pltpu.VMEM((TS, J * CB), BF16), pltpu.VMEM((16, W), F32)],
        compiler_params=_params("arbitrary"),
    )(proj, proj, dcat, xre_all, xim_all, xre_all, xim_all, a8, cl, bbre, bbim, cre, cim, rows, wg, dproj)


_WEIGHTS = ['rel_bias', 'w_ada', 'b_ada', 'w_in', 'conv_a', 'conv_c', 'conv_c_b', 'lru_wa', 'lru_ba', 'lru_wx', 'lru_bx',
            'lru_lambda', 's5_lam_re', 's5_lam_im', 's5_log_dt', 's5_b_re', 's5_b_im', 's5_c_re', 's5_c_im', 's5_d',
            's5_w_glu', 's5_b_glu', 'w_out', 'ln_g', 'ln_b']
_LAYER_SMALL = ['conv_a', 'conv_c', 'conv_c_b', 'lru_wa', 'lru_ba', 'lru_wx', 'lru_bx', 'lru_lambda', 's5_lam_re', 's5_lam_im',
                's5_log_dt', 's5_b_re', 's5_b_im', 's5_c_re', 's5_c_im', 's5_d', 's5_b_glu']
_SMALL = ['rel_bias'] + _LAYER_SMALL + ['ln_g', 'ln_b']


def _t5_bucket(dist):
    max_exact = REL_BUCKETS // 2
    nf = jnp.maximum(dist, 1).astype(F32)
    large = max_exact + (jnp.log(nf / max_exact) / math.log(REL_MAX_DIST / max_exact)
                         * (REL_BUCKETS - max_exact)).astype(jnp.int32)
    large = jnp.minimum(large, REL_BUCKETS - 1)
    return jnp.where(dist < max_exact, dist, large)


def _bias_tables(rel_bias):
    i = jnp.arange(BLK)[:, None]
    j = jnp.arange(2 * BLK)[None, :]
    delta = i + BLK - j
    out = []
    for window, dil in DILATIONS:
        bucket = _t5_bucket(jnp.clip(delta, 0, window // dil) * dil)
        onehot = (bucket[:, :, None] == jnp.arange(REL_BUCKETS)[None, None, :]).astype(F32)
        out.append(jnp.einsum('ijb,bh->hij', onehot, rel_bias, precision=lax.Precision.HIGHEST))
    return jnp.stack(out)


def _prep_layer(q):
    eye8 = jnp.eye(8, dtype=F32)
    G = q['s5_lam_re'].shape[0]
    nblk = G // 8

    def block_diag(w):
        hd = w.shape[1]
        return (w[:, :, None, :] * eye8[:, None, :, None]).reshape(8 * hd, 8 * hd)

    def compact_b(bb):
        t = jnp.transpose(bb.reshape(nblk, 8, S5_STATE, S5_CH), (0, 1, 3, 2))
        return (t[:, :, :, None, :] * eye8[None, :, None, :, None]).reshape(G * S5_CH, 8 * S5_STATE)

    def compact_c(cc):
        t = jnp.transpose(cc.reshape(nblk, 8, S5_CH, S5_STATE), (0, 1, 3, 2))
        return (t[:, :, :, None, :] * eye8[None, :, None, :, None]).reshape(G * S5_STATE, 8 * S5_CH)

    lam_re, lam_im = q['s5_lam_re'], q['s5_lam_im']
    dt = jnp.exp(q['s5_log_dt'])[:, None]
    mag = jnp.exp(lam_re * dt)
    ab_re = mag * jnp.cos(lam_im * dt)
    ab_im = mag * jnp.sin(lam_im * dt)
    den = lam_re * lam_re + lam_im * lam_im
    f_re = ((ab_re - 1.0) * lam_re + ab_im * lam_im) / den
    f_im = (ab_im * lam_re - (ab_re - 1.0) * lam_im) / den
    b_re, b_im = q['s5_b_re'], q['s5_b_im']
    bb_re = f_re[..., None] * b_re - f_im[..., None] * b_im
    bb_im = f_re[..., None] * b_im + f_im[..., None] * b_re
    return dict(
        conv_a=q['conv_a'], conv_c=q['conv_c'], conv_c_b=q['conv_c_b'], lru_ba=q['lru_ba'], lru_bx=q['lru_bx'],
        sp=jax.nn.softplus(-q['lru_lambda']), wa=block_diag(q['lru_wa']), wx=block_diag(q['lru_wx']),
        ar=ab_re.reshape(-1), ai=ab_im.reshape(-1), bbre=compact_b(bb_re), bbim=compact_b(bb_im),
        cre=compact_c(q['s5_c_re']), cim=compact_c(q['s5_c_im']), s5_d=q['s5_d'], s5_b_glu=q['s5_b_glu'])


def _s5_lag_weights(kp):
    J, CB, SB = S5_LAGS, S5_CB, S5_SB
    ar, ai = kp['ar'], kp['ai']
    W = ar.shape[0]
    nblk = W // SB
    cr, ci = jnp.ones_like(ar), jnp.zeros_like(ai)
    pows = []
    for _ in range(J + 1):
        pows.append((cr, ci))
        cr, ci = cr * ar - ci * ai, cr * ai + ci * ar
    pwr = jnp.stack([p[0] for p in pows[:J]]).reshape(J, nblk, 1, SB)
    pwi = jnp.stack([p[1] for p in pows[:J]]).reshape(J, nblk, 1, SB)

    def lagged(re, im, sign):
        out = jnp.concatenate([re * pwr - sign * im * pwi, sign * re * pwi + im * pwr], axis=-1)
        return jnp.transpose(out, (1, 0, 2, 3)).reshape(nblk, J * CB, 2 * SB).astype(BF16)

    bbl = lagged(kp['bbre'].reshape(1, nblk, CB, SB), kp['bbim'].reshape(1, nblk, CB, SB), 1.0)
    c0r = jnp.transpose(kp['cre'].reshape(nblk, SB, CB), (0, 2, 1))[None]
    c0i = -jnp.transpose(kp['cim'].reshape(nblk, SB, CB), (0, 2, 1))[None]
    cl = lagged(c0r, c0i, -1.0)
    return _rows8([pows[J][0], pows[J][1]], W), bbl, cl


def _rows8(vecs, width):
    rows = [v.reshape(1, width).astype(F32) for v in vecs]
    return jnp.concatenate(rows + [jnp.zeros((8 - len(rows), width), F32)], axis=0)


def kernel(x, c, rel_bias, w_ada, b_ada, w_in, conv_a, conv_c, conv_c_b, lru_wa, lru_ba, lru_wx, lru_bx, lru_lambda, s5_lam_re, s5_lam_im, s5_log_dt, s5_b_re, s5_b_im, s5_c_re, s5_c_im, s5_d, s5_w_glu, s5_b_glu, w_out, ln_g, ln_b, loss_target, m_rel_bias, m_w_ada, m_b_ada, m_w_in, m_conv_a, m_conv_c, m_conv_c_b, m_lru_wa, m_lru_ba, m_lru_wx, m_lru_bx, m_lru_lambda, m_s5_lam_re, m_s5_lam_im, m_s5_log_dt, m_s5_b_re, m_s5_b_im, m_s5_c_re, m_s5_c_im, m_s5_d, m_s5_w_glu, m_s5_b_glu, m_w_out, m_ln_g, m_ln_b, v_rel_bias, v_w_ada, v_b_ada, v_w_in, v_conv_a, v_conv_c, v_conv_c_b, v_lru_wa, v_lru_ba, v_lru_wx, v_lru_bx, v_lru_lambda, v_s5_lam_re, v_s5_lam_im, v_s5_log_dt, v_s5_b_re, v_s5_b_im, v_s5_c_re, v_s5_c_im, v_s5_d, v_s5_w_glu, v_s5_b_glu, v_w_out, v_ln_g, v_ln_b):
    a = dict(locals())
    me = 4 * lax.axis_index("x") + 2 * lax.axis_index("y") + lax.axis_index("c")
    x0, target = x[0], loss_target[0]
    S, D = x0.shape
    BR = D // 4
    NS = w_in.shape[2]
    SH = BR // NDEV
    assert S % (BLK * DILATIONS[-1][1]) == 0 and BR % (8 * S5_CH) == 0

    small = _exchange(_pack([c, conv_a, conv_c]), True, "gather_small").reshape(NDEV, -1)
    c_all = small[:, :D]
    na, nc = DEPTH * 3 * SH, DEPTH * 4 * SH
    conv_a_full = jnp.transpose(small[:, D:D + na].reshape(NDEV, DEPTH, 3, SH), (1, 2, 0, 3)).reshape(DEPTH, 3, BR)
    conv_c_full = jnp.transpose(small[:, D + na:D + na + nc].reshape(NDEV, DEPTH, 4, SH), (1, 2, 0, 3)).reshape(DEPTH, 4, BR)
    me1 = me.reshape(1).astype(jnp.int32)
    w_in_bf, w_out_bf, w_glu_bf = w_in.astype(BF16), w_out.astype(BF16), s5_w_glu.astype(BF16)

    b_cols = lax.dynamic_slice(b_ada, (0, me * NS), (DEPTH, NS)).reshape(DEPTH, 1, NS)
    ada_all = _exchange(_ada_cols(c_all, w_ada, b_cols), True, "gather_ada")
    ada_me = lax.dynamic_index_in_dim(ada_all, me, axis=2, keepdims=False)
    ada_me = jnp.transpose(ada_me, (1, 0, 2)).reshape(DEPTH, 3 * D)
    shift, scale, gate = ada_me[:, :D], ada_me[:, D:2 * D], ada_me[:, 2 * D:]

    bias_tabs, bias_pull = jax.vjp(_bias_tables, rel_bias)
    HD = BR // 8
    head_ones = jnp.kron(jnp.eye(8, dtype=F32), jnp.ones((HD, HD), F32))

    saved = []
    xl = x0
    for l in range(DEPTH):
        q = {n: a[n][l] for n in _LAYER_SMALL}
        q['conv_a'], q['conv_c'] = conv_a_full[l], conv_c_full[l]
        kp, pull = jax.vjp(_prep_layer, q)
        a8, bbl, cl = _s5_lag_weights(kp)
        mod = _rows8([1.0 + scale[l], shift[l]], D)
        arows = _rows8(list(kp['conv_a']), BR)
        lrows = _rows8(list(kp['conv_c']) + [kp['conv_c_b'], kp['lru_ba'], kp['lru_bx'], kp['sp']], BR)
        srows = _rows8([kp['s5_d'], kp['s5_b_glu']], BR)
        orows = _rows8([1.0 + gate[l], ln_g[l], ln_b[l]], D)
        wa, wx = kp['wa'].astype(BF16), kp['wx'].astype(BF16)
        s5w = [kp[n].astype(BF16) for n in ('bbre', 'bbim', 'cre', 'cim')]

        hbf = _modulate(xl, mod)
        proj, w_in_l, w_out_l, w_glu_l = _proj_gather(hbf, w_in_bf[l], [w_out_bf[l], w_glu_bf[l]], me1)
        w_out_l, w_glu_l = w_out_l.reshape(D, D), w_glu_l.reshape(BR, BR)
        ya = _conv_a_fwd(proj, arows, BR)
        os_, ls_ = [], []
        for g, (_, dil) in enumerate(DILATIONS):
            o, lse = _attn_fwd(proj, bias_tabs[g], dil, BR)
            os_.append(o)
            ls_.append(lse)
        yb = _attn_mix_fwd(proj, os_, ls_, BR)
        yc, hs = _lru_fwd(proj, lrows, wa, wx, BR)
        yd, xre, xim = _s5_fwd(proj, a8, bbl, s5w[2], s5w[3], srows, w_glu_l, BR)
        xn, y, cat = _out_ln([ya, yb, yc, yd], w_out_l, xl, orows)
        saved.append(dict(x=xl, proj=proj, hbf=hbf, os=os_, ls=ls_, hs=hs, xre=xre, xim=xim, y=y, cat=cat, pull=pull,
                          mod=mod, arows=arows, lrows=lrows, srows=srows, orows=orows, wa=wa, wx=wx, s5w=s5w,
                          qw=(a8, cl), w_in=w_in_l, w_out=w_out_l, w_glu=w_glu_l))
        xl = xn

    dout, loss_local = _loss_grad(xl, target)

    dbias = jnp.zeros_like(bias_tabs)
    lgrads, dada, dw_in_parts, dw_out_parts, dw_glu_parts = [None] * DEPTH, [None] * DEPTH, [None] * DEPTH, [None] * DEPTH, [None] * DEPTH
    for l in reversed(range(DEPTH)):
        sv = saved[l]
        proj = sv['proj']
        dxres, dyb, dcat, lnsum = _ln_bwd(dout, sv['x'], sv['y'], sv['orows'], sv['w_out'])
        dw_out_parts[l] = _dw_scatter(sv['cat'], dyb, D // NDEV, True, me1, "dw_out_scatter")
        dproj, asum = _conv_a_bwd(proj, dcat, sv['arows'], BR)
        dos, es, dbg = _attn_mix_bwd(proj, dcat, sv['os'], sv['ls'], head_ones, BR)
        dqkv, dbs = [], []
        for g, (_, dil) in enumerate(DILATIONS):
            dq_, dk_, dv_, db_ = _attn_bwd(proj, dos[g], es[g], sv['ls'][g], bias_tabs[g], dil, BR)
            dqkv.append((dq_, dk_, dv_))
            dbs.append(db_)
        dbias = dbias + jnp.stack(dbs)
        dproj = _attn_dsum(dqkv, dbg, dproj, BR)
        dproj, lsum, dwa, dwx = _lru_bwd(proj, sv['hs'], dcat, sv['lrows'], sv['wa'], sv['wx'], dproj, BR)
        dproj, ssum, dwg, da_, dbbre, dbbim, dcre, dcim = _s5_bwd(
            proj, dcat, sv['xre'], sv['xim'], *sv['qw'], *sv['s5w'], sv['srows'], sv['w_glu'], dproj, BR)
        dw_glu_parts[l] = dwg.reshape(NDEV, BR // NDEV, BR)
        dout, msum = _dh(dproj, sv['w_in'], dxres, sv['x'], sv['mod'])
        dw_in_parts[l] = _dw_scatter(sv['hbf'], dproj, NS, False, me1, "dw_in_scatter")
        dada[l] = jnp.concatenate([msum[1], msum[0], lnsum[2]])
        dkp = dict(conv_a=asum[0:3], conv_c=lsum[0:4], conv_c_b=lsum[4], lru_ba=lsum[5], lru_bx=lsum[6], sp=lsum[7],
                   wa=dwa, wx=dwx, ar=da_[0], ai=da_[1], bbre=dbbre, bbim=dbbim, cre=dcre, cim=dcim,
                   s5_d=ssum[0], s5_b_glu=ssum[1])
        lgrads[l] = dict(sv['pull'](dkp)[0], ln_g=lnsum[0], ln_b=lnsum[1])
    grad_x = dout[None]

    out = {}

    def put(name, res):
        out[name] = res

    def stacked(fn):
        res = [fn(l) for l in range(DEPTH)]
        return tuple(jnp.stack([r[k] for r in res]) for k in range(4))

    put('w_in', stacked(lambda l: _adamw(dw_in_parts[l], w_in[l], m_w_in[l], v_w_in[l], "adamw_w_in")))
    put('w_out', stacked(lambda l: _adamw(dw_out_parts[l], w_out[l], m_w_out[l], v_w_out[l], "adamw_w_out")))
    put('s5_w_glu', stacked(lambda l: _adamw(_exchange(dw_glu_parts[l], False, "scatter_w_glu"), s5_w_glu[l], m_s5_w_glu[l], v_s5_w_glu[l], "adamw_w_glu")))

    dada_all = _exchange(jnp.stack(dada), True, "gather_dada")
    dada_cols = jnp.transpose(lax.dynamic_slice(dada_all, (0, 0, me * NS), (NDEV, DEPTH, NS)), (1, 0, 2))
    gw_ada, gb_ada = _ada_grads(c_all, dada_all, dada_cols)
    res = _adamw(gw_ada.reshape(DEPTH * D, NS), w_ada.reshape(DEPTH * D, NS), m_w_ada.reshape(DEPTH * D, NS),
                 v_w_ada.reshape(DEPTH * D, NS), "adamw_w_ada")
    put('w_ada', tuple(r.reshape(DEPTH, D, NS) for r in res))

    local = {'rel_bias': bias_pull(dbias)[0]}
    for n in _LAYER_SMALL + ['ln_g', 'ln_b']:
        local[n] = jnp.stack([lgrads[l][n] for l in range(DEPTH)])
    local['loss'] = loss_local.reshape(1)
    packed = _SMALL + ['loss']
    shapes = [local[n].shape for n in packed]
    summed = _sum8(_exchange(_pack([local[n] for n in packed]), True, "gather_small_grads"), "sum_small_grads")
    gs = dict(zip(packed, _unpack(summed, shapes)))
    loss = gs['loss'][0]
    gs['conv_a'] = lax.dynamic_slice_in_dim(gs['conv_a'], me * SH, SH, axis=2)
    gs['conv_c'] = lax.dynamic_slice_in_dim(gs['conv_c'], me * SH, SH, axis=2)
    gs['b_ada'] = gb_ada
    names = _SMALL + ['b_ada']
    shapes = [a[n].shape for n in names]
    res = _adamw(_pack([gs[n] for n in names]), _pack([a[n] for n in names]), _pack([a['m_' + n] for n in names]),
                 _pack([a['v_' + n] for n in names]), "adamw_small")
    unpacked = [_unpack(r, shapes) for r in res]
    for k, n in enumerate(names):
        put(n, tuple(unpacked[j][k] for j in range(4)))

    return (loss, grad_x, *[out[n][0] for n in _WEIGHTS], *[out[n][1] for n in _WEIGHTS],
            *[out[n][2] for n in _WEIGHTS], *[out[n][3] for n in _WEIGHTS])
```

```python
import functools
import math

import jax
import jax.numpy as jnp
from jax import lax
from jax.experimental import pallas as pl
from jax.experimental.pallas import tpu as pltpu

F32 = jnp.float32
BF16 = jnp.bfloat16

NDEV = 8
DEPTH = 2
BLK = 128
DILATIONS = ((128, 1), (512, 4), (2048, 16))
REL_BUCKETS = 32
REL_MAX_DIST = 2048
LRU_C = 8.0
S5_CH = 16
S5_STATE = 64
ALPHA = (2 * DEPTH) ** 0.25
LN_EPS = 1e-5
ADAM_LR, ADAM_B1, ADAM_B2, ADAM_EPS, ADAM_WD, ADAM_STEP = 0.001, 0.9, 0.999, 1e-08, 0.01, 10
NEG = -1e30
VMEM_LIMIT = 56 * 1024 * 1024


def _call(body, **kw):
    return pl.pallas_call(body, **kw)


def _params(*sem):
    return pltpu.CompilerParams(dimension_semantics=sem, vmem_limit_bytes=VMEM_LIMIT)


def _sigmoid(x):
    return 1.0 / (1.0 + jnp.exp(-x))


def _silu(x):
    return x * _sigmoid(x)


def _dsilu(x):
    s = _sigmoid(x)
    return s * (1.0 + x * (1.0 - s))


_GC = math.sqrt(2.0 / math.pi)


def _gelu(x):
    return 0.5 * x * (1.0 + jnp.tanh(_GC * (x + 0.044715 * x * x * x)))


def _dgelu(x):
    t = jnp.tanh(_GC * (x + 0.044715 * x * x * x))
    return 0.5 * (1.0 + t) + 0.5 * x * (1.0 - t * t) * _GC * (1.0 + 3.0 * 0.044715 * x * x)


def _expm1(x):
    series = x * (1.0 + x * (1.0 / 2) * (1.0 + x * (1.0 / 3) * (1.0 + x * (1.0 / 4) * (1.0 + x * (1.0 / 5) * (1.0 + x * (1.0 / 6))))))
    return jnp.where(jnp.abs(x) < 0.25, series, jnp.exp(x) - 1.0)


def _dot(a, b):
    return jnp.dot(a, b, preferred_element_type=F32)


def _dot_nt(a, b):
    return lax.dot_general(a, b, (((1,), (1,)), ((), ())), preferred_element_type=F32)


def _dot_tn(a, b):
    return lax.dot_general(a, b, (((0,), (0,)), ((), ())), preferred_element_type=F32)


def _colsum(v):
    return jnp.sum(v, axis=0, keepdims=True)


def _exchange(x, gather, name):
    blk = x.shape if gather else x.shape[1:]

    def gather_body(x_ref, o_ref, send_sems, recv_sems, local_sem):
        ix, iy, ic = lax.axis_index("x"), lax.axis_index("y"), lax.axis_index("c")
        me, sibling = (ix, iy, ic), (ix, iy, 1 - ic)
        chips = [(1 - ix, iy), (ix, 1 - iy), (1 - ix, 1 - iy)]

        def slot(px, py, pc):
            return o_ref.at[4 * px + 2 * py + pc]

        def copy(k, block, to, src=None):
            return pltpu.make_async_remote_copy(
                src_ref=slot(*block) if src is None else src, dst_ref=slot(*block), send_sem=send_sems.at[k],
                recv_sem=recv_sems.at[k], device_id=to, device_id_type=pl.DeviceIdType.MESH)

        mine = pltpu.make_async_copy(x_ref, slot(*me), local_sem)
        mine.start()
        first = [copy(0, me, sibling, src=x_ref)] + [copy(1 + j, me, (*chip, ic), src=x_ref) for j, chip in enumerate(chips)]
        for cp in first:
            cp.start()
        passed = [copy(4 + j, (*chip, ic), sibling) for j, chip in enumerate(chips)]
        for j, chip in enumerate(chips):
            copy(1 + j, (*chip, ic), me).wait_recv()
            passed[j].start()
        copy(0, sibling, me).wait_recv()
        for j, chip in enumerate(chips):
            copy(4 + j, (*chip, 1 - ic), me).wait_recv()
        for cp in first + passed:
            cp.wait_send()
        mine.wait()

    def scatter_body(x_ref, o_ref, send_sems, recv_sems, local_sem):
        ix, iy, ic = lax.axis_index("x"), lax.axis_index("y"), lax.axis_index("c")
        me = 4 * ix + 2 * iy + ic
        local = pltpu.make_async_copy(x_ref.at[me], o_ref.at[me], local_sem)
        local.start()
        copies = []
        for k in range(1, NDEV):
            px = 1 - ix if k & 4 else ix
            py = 1 - iy if k & 2 else iy
            pc = 1 - ic if k & 1 else ic
            peer = 4 * px + 2 * py + pc
            cp = pltpu.make_async_remote_copy(
                src_ref=x_ref.at[peer], dst_ref=o_ref.at[me], send_sem=send_sems.at[k - 1], recv_sem=recv_sems.at[k - 1],
                device_id=(px, py, pc), device_id_type=pl.DeviceIdType.MESH)
            cp.start()
            copies.append(cp)
        for cp in copies:
            cp.wait()
        local.wait()

    return _call(
        gather_body if gather else scatter_body, name=name,
        out_shape=jax.ShapeDtypeStruct((NDEV,) + tuple(blk), x.dtype),
        in_specs=[pl.BlockSpec(memory_space=pl.ANY)],
        out_specs=pl.BlockSpec(memory_space=pl.ANY),
        scratch_shapes=[pltpu.SemaphoreType.DMA((NDEV - 1,)), pltpu.SemaphoreType.DMA((NDEV - 1,)), pltpu.SemaphoreType.DMA],
    )(x)


def _pack(arrs):
    flat = jnp.concatenate([a.reshape(-1).astype(F32) for a in arrs])
    n = flat.shape[0]
    quantum = 1024 if n <= 65536 else 65536
    npad = -(-n // quantum) * quantum
    return jnp.pad(flat, (0, npad - n)).reshape(npad // 128, 128)


def _unpack(packed, shapes):
    flat = packed.reshape(-1)
    out, off = [], 0
    for s in shapes:
        n = math.prod(s)
        out.append(flat[off:off + n].reshape(s))
        off += n
    return out


def _sum8(parts, name):
    _, R, C = parts.shape
    TR = R
    for cand in (512, 256, 128, 64, 32, 16, 8):
        if R % cand == 0:
            TR = cand
            break

    def body(p_ref, o_ref):
        acc = p_ref[0]
        for d in range(1, NDEV):
            acc = acc + p_ref[d]
        o_ref[...] = acc

    return _call(
        body, name=name, grid=(R // TR,),
        in_specs=[pl.BlockSpec((NDEV, TR, C), lambda i: (0, i, 0))],
        out_specs=pl.BlockSpec((TR, C), lambda i: (i, 0)),
        out_shape=jax.ShapeDtypeStruct((R, C), F32),
        compiler_params=_params("arbitrary"),
    )(parts)


def _adamw(g_layers, w, m, v, name):
    L, R, C = w.shape
    g0 = g_layers[0]
    nparts = g0.shape[0] if g0.ndim == 3 else 0
    TR = R
    for cand in (256, 128, 64, 32, 16):
        if R % cand == 0 and cand * C <= 128 * 1024:
            TR = cand
            break

    def body(*refs):
        g_refs = refs[:L]
        w_ref, m_ref, v_ref, go_ref, d_ref, mo_ref, vo_ref = refs[L:]
        for ll in range(L):
            @pl.when(pl.program_id(0) == ll)
            def _(g_ref=g_refs[ll]):
                if nparts:
                    g = g_ref[0].astype(F32)
                    for d in range(1, nparts):
                        g = g + g_ref[d].astype(F32)
                else:
                    g = g_ref[...]
                m2 = ADAM_B1 * m_ref[0] + (1.0 - ADAM_B1) * g
                v2 = ADAM_B2 * v_ref[0] + (1.0 - ADAM_B2) * (g * g)
                m_hat = m2 / (1.0 - ADAM_B1 ** ADAM_STEP)
                v_hat = v2 / (1.0 - ADAM_B2 ** ADAM_STEP)
                go_ref[0] = g
                d_ref[0] = -ADAM_LR * (m_hat / (jnp.sqrt(v_hat) + ADAM_EPS) + ADAM_WD * w_ref[0])
                mo_ref[0] = m2
                vo_ref[0] = v2

    def gspec(ll):
        if nparts:
            return pl.BlockSpec((nparts, TR, C), lambda l, i: (0, jnp.where(l == ll, i, 0), 0))
        return pl.BlockSpec((TR, C), lambda l, i: (jnp.where(l == ll, i, 0), 0))

    spec = pl.BlockSpec((1, TR, C), lambda l, i: (l, i, 0))
    return _call(
        body, name=name, grid=(L, R // TR),
        in_specs=[gspec(ll) for ll in range(L)] + [spec, spec, spec], out_specs=[spec] * 4,
        out_shape=[jax.ShapeDtypeStruct((L, R, C), F32)] * 4,
        compiler_params=_params("arbitrary", "arbitrary"),
    )(*g_layers, w, m, v)


def _ada_cols(c_all, w_ada, b_cols):
    L, D, NS = w_ada.shape

    def body(c_ref, w_ref, b_ref, o_ref):
        cond = _silu(c_ref[...]).astype(BF16)
        o_ref[0] = _dot(cond, w_ref[0].astype(BF16)) + b_ref[0]

    return _call(
        body, name="ada_cols", grid=(L,),
        in_specs=[pl.BlockSpec((NDEV, D), lambda l: (0, 0)), pl.BlockSpec((1, D, NS), lambda l: (l, 0, 0)),
                  pl.BlockSpec((1, 1, NS), lambda l: (l, 0, 0))],
        out_specs=pl.BlockSpec((1, NDEV, NS), lambda l: (l, 0, 0)),
        out_shape=jax.ShapeDtypeStruct((L, NDEV, NS), F32),
        compiler_params=_params("arbitrary"),
    )(c_all, w_ada, b_cols)


def _ada_grads(c_all, dada_all, dada_cols):
    _, D = c_all.shape
    L, _, NS = dada_cols.shape
    D3 = dada_all.shape[2]

    def body(c_ref, da_ref, dc_ref, gw_ref, gb_ref):
        cond = _silu(c_ref[...]).astype(BF16)
        gw_ref[0] = _dot_tn(cond, dc_ref[0].astype(BF16))
        acc = da_ref[0]
        for d in range(1, NDEV):
            acc = acc + da_ref[d]
        gb_ref[...] = acc

    return _call(
        body, name="ada_grads", grid=(L,),
        in_specs=[pl.BlockSpec((NDEV, D), lambda l: (0, 0)), pl.BlockSpec((NDEV, L, D3), lambda l: (0, 0, 0)),
                  pl.BlockSpec((1, NDEV, NS), lambda l: (l, 0, 0))],
        out_specs=[pl.BlockSpec((1, D, NS), lambda l: (l, 0, 0)), pl.BlockSpec((L, D3), lambda l: (0, 0))],
        out_shape=[jax.ShapeDtypeStruct((L, D, NS), F32), jax.ShapeDtypeStruct((L, D3), F32)],
        compiler_params=_params("arbitrary"),
    )(c_all, dada_all, dada_cols)


def _tile(S, want):
    return want if S % want == 0 else S


def _modulate(x, mod):
    S, D = x.shape
    TS = _tile(S, 512)

    def body(x_ref, mod_ref, h_ref):
        h_ref[...] = (x_ref[...] * mod_ref[0:1, :] + mod_ref[1:2, :]).astype(BF16)

    spec = pl.BlockSpec((TS, D), lambda i: (i, 0))
    return _call(
        body, name="modulate", grid=(S // TS,),
        in_specs=[spec, pl.BlockSpec((8, D), lambda i: (0, 0))], out_specs=spec,
        out_shape=jax.ShapeDtypeStruct((S, D), BF16),
        compiler_params=_params("arbitrary"),
    )(x, mod)


def _peer(k):
    ix, iy, ic = lax.axis_index("x"), lax.axis_index("y"), lax.axis_index("c")
    bx, by, bc = (k >> 2) & 1, (k >> 1) & 1, k & 1
    px, py, pc = ix + bx - 2 * ix * bx, iy + by - 2 * iy * by, ic + bc - 2 * ic * bc
    return (px, py, pc), 4 * px + 2 * py + pc


GATHER_ORDER = (0, 1, 2, 4, 3, 5, 6, 7)
SCATTER_ORDER = (2, 4, 6, 3, 5, 7, 1, 0)


def _offset(order, k):
    off = jnp.int32(order[-1])
    for step in reversed(range(len(order) - 1)):
        off = jnp.where(k == step, jnp.int32(order[step]), off)
    return off


def _proj_gather(h, w_shard, extras, me1):
    assert GATHER_ORDER[0] == 0
    S, D = h.shape
    NS = w_shard.shape[1]
    TS = _tile(S, 512)
    nt = S // TS
    ne = len(extras)

    def body(me_ref, h_ref, w_ref, *rest):
        x_refs, o_ref, wg_ref = rest[:ne], rest[ne], rest[ne + 1]
        xg_refs = rest[ne + 2:2 * ne + 2]
        wbuf, send_sems, recv_sems, local_sems, load_sems = rest[2 * ne + 2:]
        k, i = pl.program_id(0), pl.program_id(1)
        me = me_ref[0]
        off = _offset(GATHER_ORDER, k)

        def push(src, dst, which, kk):
            peer, _ = _peer(kk)
            return pltpu.make_async_remote_copy(
                src_ref=src, dst_ref=dst.at[me], send_sem=send_sems.at[which, kk], recv_sem=recv_sems.at[which, kk],
                device_id=peer, device_id_type=pl.DeviceIdType.MESH)

        def own(src, dst, which):
            return pltpu.make_async_copy(src, dst.at[me], local_sems.at[which])

        pairs = [(w_ref, wg_ref)] + list(zip(x_refs, xg_refs))

        @pl.when(jnp.logical_and(k == 0, i == 0))
        def _():
            for which, (src, dst) in enumerate(pairs):
                own(src, dst, which).start()
                for kk in GATHER_ORDER[1:]:
                    push(src, dst, which, kk).start()
            first = pltpu.make_async_copy(w_ref, wbuf.at[0], load_sems.at[0])
            first.start()
            first.wait()

        @pl.when(jnp.logical_and(k > 0, i == 0))
        def _():
            push(w_ref, wg_ref, 0, off).wait_recv()
            _, blk = _peer(off)
            load = pltpu.make_async_copy(wg_ref.at[blk], wbuf.at[k % 2], load_sems.at[k % 2])
            load.start()
            load.wait()

        o_ref[...] = _dot(h_ref[...], wbuf[k % 2])

        @pl.when(jnp.logical_and(k == NDEV - 1, i == nt - 1))
        def _():
            for which, (src, dst) in enumerate(pairs):
                for kk in range(1, NDEV):
                    cp = push(src, dst, which, kk)
                    cp.wait_send()
                    if which > 0:
                        cp.wait_recv()
                own(src, dst, which).wait()

    def col(k, i, me_ref):
        m, off = me_ref[0], _offset(GATHER_ORDER, k)
        return i, (m | off) - (m & off)

    anyspec = pl.BlockSpec(memory_space=pl.ANY)
    grid_spec = pltpu.PrefetchScalarGridSpec(
        num_scalar_prefetch=1, grid=(NDEV, nt),
        in_specs=[pl.BlockSpec((TS, D), lambda k, i, me_ref: (i, 0)), anyspec] + [anyspec] * ne,
        out_specs=[pl.BlockSpec((TS, NS), col), anyspec] + [anyspec] * ne,
        scratch_shapes=[pltpu.VMEM((2, D, NS), BF16), pltpu.SemaphoreType.DMA((1 + ne, NDEV)), pltpu.SemaphoreType.DMA((1 + ne, NDEV)),
                        pltpu.SemaphoreType.DMA((1 + ne,)), pltpu.SemaphoreType.DMA((2,))])
    return _call(
        body, name="proj_gather", grid_spec=grid_spec,
        out_shape=[jax.ShapeDtypeStruct((S, NDEV * NS), F32), jax.ShapeDtypeStruct((NDEV, D, NS), BF16)]
        + [jax.ShapeDtypeStruct((NDEV,) + e.shape, e.dtype) for e in extras],
        compiler_params=_params("arbitrary", "arbitrary"),
    )(me1, h, w_shard, *extras)


def _dw_scatter(a, b, blk, by_rows, me1, name):
    assert SCATTER_ORDER[-1] == 0
    S = a.shape[0]
    TK = _tile(S, 512)
    nk = S // TK
    shape = (blk, b.shape[1]) if by_rows else (a.shape[1], blk)
    last = NDEV - 1

    def body(me_ref, a_ref, b_ref, recv_ref, acc, stage, send_sems, recv_sems, local_sem):
        k, kk = pl.program_id(0), pl.program_id(1)
        me = me_ref[0]

        def push(off, slot):
            peer, _ = _peer(off)
            return pltpu.make_async_remote_copy(
                src_ref=stage.at[slot], dst_ref=recv_ref.at[me], send_sem=send_sems.at[off], recv_sem=recv_sems.at[off],
                device_id=peer, device_id_type=pl.DeviceIdType.MESH)

        own = pltpu.make_async_copy(stage.at[last % 2], recv_ref.at[me], local_sem)

        @pl.when(kk == 0)
        def _():
            acc[...] = jnp.zeros_like(acc)
        acc[...] += _dot_tn(a_ref[...], b_ref[...])

        @pl.when(kk == nk - 1)
        def _():
            @pl.when(k >= 2)
            def _():
                push(_offset(SCATTER_ORDER, k - 2), k % 2).wait_send()

            stage[k % 2] = acc[...].astype(BF16)

            @pl.when(k < last)
            def _():
                push(_offset(SCATTER_ORDER, k), k % 2).start()

            @pl.when(k == last)
            def _():
                own.start()
                push(SCATTER_ORDER[last - 1], (last - 1) % 2).wait_send()
                own.wait()
                for off in range(1, NDEV):
                    push(off, 0).wait_recv()

    def blk_of(k, me_ref):
        m, off = me_ref[0], _offset(SCATTER_ORDER, k)
        return (m | off) - (m & off)

    if by_rows:
        in_specs = [pl.BlockSpec((TK, blk), lambda k, kk, me_ref: (kk, blk_of(k, me_ref))),
                    pl.BlockSpec((TK, b.shape[1]), lambda k, kk, me_ref: (kk, 0))]
    else:
        in_specs = [pl.BlockSpec((TK, a.shape[1]), lambda k, kk, me_ref: (kk, 0)),
                    pl.BlockSpec((TK, blk), lambda k, kk, me_ref: (kk, blk_of(k, me_ref)))]
    grid_spec = pltpu.PrefetchScalarGridSpec(
        num_scalar_prefetch=1, grid=(NDEV, nk), in_specs=in_specs,
        out_specs=pl.BlockSpec(memory_space=pl.ANY),
        scratch_shapes=[pltpu.VMEM(shape, F32), pltpu.VMEM((2,) + shape, BF16), pltpu.SemaphoreType.DMA((NDEV,)),
                        pltpu.SemaphoreType.DMA((NDEV,)), pltpu.SemaphoreType.DMA])
    return _call(
        body, name=name, grid_spec=grid_spec,
        out_shape=jax.ShapeDtypeStruct((NDEV,) + shape, BF16),
        compiler_params=_params("arbitrary", "arbitrary"),
    )(me1, a, b)


def _chunk(TS, BR, k):
    return pl.BlockSpec((TS, BR), lambda i: (i, k))


def _halo_prev(TS, BR, k):
    return pl.BlockSpec((8, BR), lambda i: (jnp.maximum(i * (TS // 8) - 1, 0), k))


def _conv_a_fwd(proj, w8, BR):
    S = proj.shape[0]
    TS = _tile(S, 512)

    def body(ab_ref, ac_ref, ax_ref, ag_ref, hc_ref, hx_ref, w_ref, o_ref, ext):
        i = pl.program_id(0)
        u = ac_ref[...] * ax_ref[...]
        ext[0:8, :] = jnp.where(i > 0, hc_ref[...] * hx_ref[...], 0.0)
        ext[8:8 + TS, :] = u
        conv = ext[pl.ds(6, TS), :] * w_ref[0:1, :] + ext[pl.ds(7, TS), :] * w_ref[1:2, :] + u * w_ref[2:3, :]
        o_ref[...] = ab_ref[...] * conv * _silu(ag_ref[...])

    return _call(
        body, name="conv_a_fwd", grid=(S // TS,),
        in_specs=[_chunk(TS, BR, 0), _chunk(TS, BR, 1), _chunk(TS, BR, 2), _chunk(TS, BR, 3),
                  _halo_prev(TS, BR, 1), _halo_prev(TS, BR, 2), pl.BlockSpec((8, BR), lambda i: (0, 0))],
        out_specs=pl.BlockSpec((TS, BR), lambda i: (i, 0)),
        out_shape=jax.ShapeDtypeStruct((S, BR), F32),
        scratch_shapes=[pltpu.VMEM((TS + 8, BR), F32)],
        compiler_params=_params("arbitrary"),
    )(proj, proj, proj, proj, proj, proj, w8)


ATT_UNIT = 2048
ATT_LANES = 128


def _attn_mask():
    i = lax.broadcasted_iota(jnp.int32, (BLK, 2 * BLK), 0)
    j = lax.broadcasted_iota(jnp.int32, (BLK, 2 * BLK), 1)
    return jnp.logical_and(j >= i, j <= i + BLK), j >= BLK


def _attn_rows(n, r, dil):
    return pl.ds(n * BLK * dil + r, BLK, stride=dil) if dil > 1 else pl.ds(n * BLK, BLK)


def _attn_geometry(S, BR, dil):
    HD = BR // 8
    U = ATT_UNIT
    assert S % U == 0 and BR % ATT_LANES == 0 and ATT_LANES % HD == 0 and U % (BLK * dil) == 0
    return HD, U, S // U, U // (BLK * dil), ATT_LANES // HD, BR // ATT_LANES


def _attn_fwd(proj, bias, dil, BR):
    S = proj.shape[0]
    HD, U, NU, nbu, hpb, HBK = _attn_geometry(S, BR, dil)
    scale = HD ** -0.5
    combos = [(n, r) for n in range(nbu) for r in range(dil)]

    def body(q_ref, kc_ref, kp_ref, vc_ref, vp_ref, b_ref, o_ref, l_ref, sbuf, pbuf):
        m = pl.program_id(1)
        band, is_cur = _attn_mask()
        first = jnp.logical_and(band, jnp.logical_or(is_cur, m > 0))

        def keys(ref_c, ref_p, n, r):
            prev = ref_c[_attn_rows(n - 1, r, dil), :] if n > 0 else ref_p[_attn_rows(nbu - 1, r, dil), :]
            return jnp.concatenate([prev, ref_c[_attn_rows(n, r, dil), :]], axis=0).astype(BF16)

        for c, (n, r) in enumerate(combos):
            q = (q_ref[_attn_rows(n, r, dil), :] * scale).astype(BF16)
            kk = keys(kc_ref, kp_ref, n, r)
            for h in range(hpb):
                sl = slice(h * HD, (h + 1) * HD)
                sbuf[c * hpb + h] = _dot_nt(q[:, sl], kk[:, sl])
        for c, (n, r) in enumerate(combos):
            lses = []
            for h in range(hpb):
                s = jnp.where(band if n > 0 else first, sbuf[c * hpb + h] + b_ref[h], NEG)
                mx = jnp.max(s, axis=-1, keepdims=True)
                p = jnp.exp(s - mx)
                l = jnp.sum(p, axis=-1, keepdims=True)
                pbuf[c * hpb + h] = (p * (1.0 / l)).astype(BF16)
                lses.append(jnp.broadcast_to(mx + jnp.log(l), (BLK, HD)))
            l_ref[_attn_rows(n, r, dil), :] = jnp.concatenate(lses, axis=1)
        for c, (n, r) in enumerate(combos):
            vv = keys(vc_ref, vp_ref, n, r)
            o_ref[_attn_rows(n, r, dil), :] = jnp.concatenate(
                [_dot(pbuf[c * hpb + h], vv[:, h * HD:(h + 1) * HD]) for h in range(hpb)], axis=1)

    def cur(c):
        return pl.BlockSpec((U, ATT_LANES), lambda hb, m: (m, c * HBK + hb))

    def prev(c):
        return pl.BlockSpec((U, ATT_LANES), lambda hb, m: (jnp.maximum(m - 1, 0), c * HBK + hb))

    ospec = pl.BlockSpec((U, ATT_LANES), lambda hb, m: (m, hb))
    nhb = len(combos) * hpb
    return _call(
        body, name="attn_fwd_d%d" % dil, grid=(HBK, NU),
        in_specs=[cur(4), cur(5), prev(5), cur(6), prev(6), pl.BlockSpec((hpb, BLK, 2 * BLK), lambda hb, m: (hb, 0, 0))],
        out_specs=[ospec, ospec],
        out_shape=[jax.ShapeDtypeStruct((S, BR), F32)] * 2,
        scratch_shapes=[pltpu.VMEM((nhb, BLK, 2 * BLK), F32), pltpu.VMEM((nhb, BLK, 2 * BLK), BF16)],
        compiler_params=_params("arbitrary", "arbitrary"),
    )(proj, proj, proj, proj, proj, bias)


def _mix_weights(l1, l2, l3):
    m = jnp.maximum(jnp.maximum(l1, l2), l3)
    e1, e2, e3 = jnp.exp(l1 - m), jnp.exp(l2 - m), jnp.exp(l3 - m)
    inv = 1.0 / (e1 + e2 + e3)
    return e1 * inv, e2 * inv, e3 * inv


def _attn_mix_fwd(proj, os_, ls_, BR):
    S = proj.shape[0]
    TS = _tile(S, 512)

    def body(o1, o2, o3, l1, l2, l3, g_ref, y_ref):
        w1, w2, w3 = _mix_weights(l1[...], l2[...], l3[...])
        y_ref[...] = (w1 * o1[...] + w2 * o2[...] + w3 * o3[...]) * _silu(g_ref[...])

    spec = pl.BlockSpec((TS, BR), lambda i: (i, 0))
    return _call(
        body, name="attn_mix_fwd", grid=(S // TS,),
        in_specs=[spec] * 6 + [_chunk(TS, BR, 7)], out_specs=spec,
        out_shape=jax.ShapeDtypeStruct((S, BR), F32),
        compiler_params=_params("arbitrary"),
    )(*os_, *ls_, proj)


def _lru_gates(xc, wa, wx, ba, bx, sp):
    xb = xc.astype(BF16)
    r = _sigmoid(_dot(xb, wa) + ba)
    ig = _sigmoid(_dot(xb, wx) + bx)
    la = -LRU_C * r * sp
    a = jnp.exp(la)
    mult = jnp.sqrt(-_expm1(2.0 * la))
    return r, ig, a, mult


def _lru_fwd(proj, rows, wa, wx, BR):
    S = proj.shape[0]
    TS = _tile(S, 256)
    PAD = TS // 2

    def body(cx_ref, hx_ref, cg_ref, r_ref, wa_ref, wx_ref, y_ref, h_ref, ext, sa, sb, carry):
        i = pl.program_id(0)

        @pl.when(i == 0)
        def _():
            sa[0:PAD, :] = jnp.ones((PAD, BR), F32)
            sb[0:PAD, :] = jnp.zeros((PAD, BR), F32)
            carry[...] = jnp.zeros_like(carry)

        cx = cx_ref[...]
        ext[0:8, :] = jnp.where(i > 0, hx_ref[...], 0.0)
        ext[8:8 + TS, :] = cx
        xc = (ext[pl.ds(5, TS), :] * r_ref[0:1, :] + ext[pl.ds(6, TS), :] * r_ref[1:2, :]
              + ext[pl.ds(7, TS), :] * r_ref[2:3, :] + cx * r_ref[3:4, :] + r_ref[4:5, :])
        _, ig, a, mult = _lru_gates(xc, wa_ref[...], wx_ref[...], r_ref[5:6, :], r_ref[6:7, :], r_ref[7:8, :])
        sa[PAD:PAD + TS, :] = a
        sb[PAD:PAD + TS, :] = mult * ig * xc
        d = 1
        while d < TS:
            A, B = sa[PAD:PAD + TS, :], sb[PAD:PAD + TS, :]
            As, Bs = sa[pl.ds(PAD - d, TS), :], sb[pl.ds(PAD - d, TS), :]
            sb[PAD:PAD + TS, :] = A * Bs + B
            sa[PAD:PAD + TS, :] = A * As
            d *= 2
        h = sb[PAD:PAD + TS, :] + sa[PAD:PAD + TS, :] * carry[0:1, :]
        carry[0:1, :] = h[TS - 1:TS, :]
        h_ref[...] = h
        y_ref[...] = h * _silu(cg_ref[...])

    full = pl.BlockSpec((BR, BR), lambda i: (0, 0))
    spec = pl.BlockSpec((TS, BR), lambda i: (i, 0))
    return _call(
        body, name="lru_fwd", grid=(S // TS,),
        in_specs=[_chunk(TS, BR, 8), _halo_prev(TS, BR, 8), _chunk(TS, BR, 9), pl.BlockSpec((8, BR), lambda i: (0, 0)), full, full],
        out_specs=[spec, spec],
        out_shape=[jax.ShapeDtypeStruct((S, BR), F32)] * 2,
        scratch_shapes=[pltpu.VMEM((TS + 8, BR), F32), pltpu.VMEM((PAD + TS, BR), F32), pltpu.VMEM((PAD + TS, BR), F32),
                        pltpu.VMEM((8, BR), F32)],
        compiler_params=_params("arbitrary"),
    )(proj, proj, proj, rows, wa, wx)


S5_LAGS = 8
S5_CB = 8 * S5_CH
S5_SB = 8 * S5_STATE


def _s5_fwd(proj, a8, bbl, cre, cim, rows, wg, BR):
    S = proj.shape[0]
    W = a8.shape[1]
    TS = _tile(S, 256)
    CB, SB, J = S5_CB, S5_SB, S5_LAGS
    nblk = BR // CB

    def body(u_ref, uh_ref, dg_ref, a8_ref, bbl_ref, cre_ref, cim_ref, r_ref, wg_ref,
             y_ref, xre_ref, xim_ref, uext, ulag, sre, sim, ypre_s, carry):
        i = pl.program_id(0)

        @pl.when(i == 0)
        def _():
            carry[...] = jnp.zeros_like(carry)

        uext[0:8, :] = jnp.where(i > 0, uh_ref[...], 0.0)
        uext[8:8 + TS, :] = u_ref[...]
        for m in range(nblk):
            cs, ws = slice(m * CB, (m + 1) * CB), slice(m * SB, (m + 1) * SB)
            for j in range(J):
                ulag[:, j * CB:(j + 1) * CB] = uext[pl.ds(8 - j, TS), cs].astype(BF16)
            w = _dot(ulag[...], bbl_ref[m])
            sre[...] = w[:, 0:SB]
            sim[...] = w[:, SB:2 * SB]
            ar = jnp.broadcast_to(a8_ref[0:1, ws], (8, SB))
            ai = jnp.broadcast_to(a8_ref[1:2, ws], (8, SB))
            xr, xi = carry[0:8, ws], carry[8:16, ws]
            for g in range(TS // 8):
                rg = slice(8 * g, 8 * g + 8)
                xr, xi = sre[rg, :] + ar * xr - ai * xi, sim[rg, :] + ar * xi + ai * xr
                sre[rg, :] = xr
                sim[rg, :] = xi
            carry[0:8, ws] = xr
            carry[8:16, ws] = xi
            xre, xim = sre[...], sim[...]
            xre_ref[:, ws] = xre
            xim_ref[:, ws] = xim
            ypre_s[:, cs] = _dot(xre.astype(BF16), cre_ref[ws, :]) - _dot(xim.astype(BF16), cim_ref[ws, :])
        dg = dg_ref[...]
        yg = _gelu(ypre_s[...] + r_ref[0:1, :] * u_ref[...])
        s = _sigmoid(_dot(yg.astype(BF16), wg_ref[...]) + r_ref[1:2, :])
        y_ref[...] = yg * s * _silu(dg)

    def const(shape):
        return pl.BlockSpec(shape, lambda i: (0,) * len(shape))

    return _call(
        body, name="s5_fwd", grid=(S // TS,),
        in_specs=[_chunk(TS, BR, 10), _halo_prev(TS, BR, 10), _chunk(TS, BR, 11), const((8, W)), const((nblk, J * CB, 2 * SB)),
                  const((W, CB)), const((W, CB)), const((8, BR)), const((BR, BR))],
        out_specs=[pl.BlockSpec((TS, BR), lambda i: (i, 0)), pl.BlockSpec((TS, W), lambda i: (i, 0)), pl.BlockSpec((TS, W), lambda i: (i, 0))],
        out_shape=[jax.ShapeDtypeStruct((S, BR), F32), jax.ShapeDtypeStruct((S, W), F32), jax.ShapeDtypeStruct((S, W), F32)],
        scratch_shapes=[pltpu.VMEM((TS + 8, BR), F32), pltpu.VMEM((TS, J * CB), BF16), pltpu.VMEM((TS, SB), F32), pltpu.VMEM((TS, SB), F32),
                        pltpu.VMEM((TS, BR), F32), pltpu.VMEM((16, W), F32)],
        compiler_params=_params("arbitrary"),
    )(proj, proj, proj, a8, bbl, cre, cim, rows, wg)


def _out_ln(ys, w_out, x, rows, target=None):
    S, D = x.shape
    BR = D // 4
    TS = _tile(S, 256)
    last = target is not None

    def body(ya, yb, yc, yd, w_ref, x_ref, r_ref, *rest):
        t_ref = rest[0] if last else None
        o_ref, y_ref, cat_ref = rest[1:4] if last else rest[0:3]
        cat = jnp.concatenate([ya[...], yb[...], yc[...], yd[...]], axis=1).astype(BF16)
        y = _dot(cat, w_ref[...])
        z = ALPHA * x_ref[...] + r_ref[0:1, :] * y
        mu = jnp.mean(z, axis=-1, keepdims=True)
        zc = z - mu
        var = jnp.mean(zc * zc, axis=-1, keepdims=True)
        xn = zc * lax.rsqrt(var + LN_EPS) * r_ref[1:2, :] + r_ref[2:3, :]
        y_ref[...] = y
        cat_ref[...] = cat
        if last:
            l_ref = rest[4]

            @pl.when(pl.program_id(0) == 0)
            def _():
                l_ref[...] = jnp.zeros_like(l_ref)
            diff = xn - t_ref[...]
            o_ref[...] = diff * (1.0 / D)
            l_ref[...] += jnp.sum(_colsum(diff * diff), axis=1, keepdims=True) * (0.5 / D)
        else:
            o_ref[...] = xn

    yspec = pl.BlockSpec((TS, BR), lambda i: (i, 0))
    spec = pl.BlockSpec((TS, D), lambda i: (i, 0))
    lspec = pl.BlockSpec((1, 1), lambda i: (0, 0))
    return _call(
        body, name="out_ln_loss" if last else "out_ln", grid=(S // TS,),
        in_specs=[yspec] * 4 + [pl.BlockSpec((D, D), lambda i: (0, 0)), spec, pl.BlockSpec((8, D), lambda i: (0, 0))] + [spec] * last,
        out_specs=[spec, spec, spec] + [lspec] * last,
        out_shape=[jax.ShapeDtypeStruct((S, D), F32), jax.ShapeDtypeStruct((S, D), F32), jax.ShapeDtypeStruct((S, D), BF16)]
        + [jax.ShapeDtypeStruct((1, 1), F32)] * last,
        compiler_params=_params("arbitrary"),
    )(*ys, w_out, x, rows, *([target] if last else []))


def _ln_bwd(dout, x, y, rows, w_out):
    S, D = x.shape
    TS = _tile(S, 256)

    def body(do_ref, x_ref, y_ref, r_ref, w_ref, dxr_ref, dyb_ref, dcat_ref, acc_ref):
        @pl.when(pl.program_id(0) == 0)
        def _():
            acc_ref[...] = jnp.zeros_like(acc_ref)
        g1, lg = r_ref[0:1, :], r_ref[1:2, :]
        yv = y_ref[...]
        z = ALPHA * x_ref[...] + g1 * yv
        mu = jnp.mean(z, axis=-1, keepdims=True)
        zc = z - mu
        var = jnp.mean(zc * zc, axis=-1, keepdims=True)
        rstd = lax.rsqrt(var + LN_EPS)
        xhat = zc * rstd
        do = do_ref[...]
        dxh = do * lg
        dz = rstd * (dxh - jnp.mean(dxh, axis=-1, keepdims=True) - xhat * jnp.mean(dxh * xhat, axis=-1, keepdims=True))
        dxr_ref[...] = ALPHA * dz
        dyb = (g1 * dz).astype(BF16)
        dyb_ref[...] = dyb
        dcat_ref[...] = _dot_nt(dyb, w_ref[...])
        acc_ref[0:1, :] += _colsum(do * xhat)
        acc_ref[1:2, :] += _colsum(do)
        acc_ref[2:3, :] += _colsum(dz * yv)

    spec = pl.BlockSpec((TS, D), lambda i: (i, 0))
    rspec = pl.BlockSpec((8, D), lambda i: (0, 0))
    return _call(
        body, name="ln_bwd", grid=(S // TS,),
        in_specs=[spec, spec, spec, rspec, pl.BlockSpec((D, D), lambda i: (0, 0))],
        out_specs=[spec, spec, spec, rspec],
        out_shape=[jax.ShapeDtypeStruct((S, D), F32), jax.ShapeDtypeStruct((S, D), BF16), jax.ShapeDtypeStruct((S, D), F32),
                   jax.ShapeDtypeStruct((8, D), F32)],
        compiler_params=_params("arbitrary"),
    )(dout, x, y, rows, w_out)


def _dh(dproj, wg, dxres, x, mod):
    S, D = x.shape
    NS = wg.shape[2]
    TS = _tile(S, 256)

    def body(dp_ref, w_hbm, dxr_ref, x_ref, mod_ref, dx_ref, sum_ref, w_vmem, sem):
        @pl.when(pl.program_id(0) == 0)
        def _():
            load = pltpu.make_async_copy(w_hbm, w_vmem, sem)
            load.start()
            sum_ref[...] = jnp.zeros_like(sum_ref)
            load.wait()

        dh = _dot_nt(dp_ref[:, 0:NS], w_vmem[0])
        for j in range(1, NDEV):
            dh = dh + _dot_nt(dp_ref[:, j * NS:(j + 1) * NS], w_vmem[j])
        dx_ref[...] = dxr_ref[...] + dh * mod_ref[0:1, :]
        sum_ref[0:1, :] += _colsum(dh * x_ref[...])
        sum_ref[1:2, :] += _colsum(dh)

    spec = pl.BlockSpec((TS, D), lambda i: (i, 0))
    rspec = pl.BlockSpec((8, D), lambda i: (0, 0))
    return _call(
        body, name="dh", grid=(S // TS,),
        in_specs=[pl.BlockSpec((TS, NDEV * NS), lambda i: (i, 0)), pl.BlockSpec(memory_space=pl.ANY), spec, spec, rspec],
        out_specs=[spec, rspec],
        out_shape=[jax.ShapeDtypeStruct((S, D), F32), jax.ShapeDtypeStruct((8, D), F32)],
        scratch_shapes=[pltpu.VMEM((NDEV, D, NS), BF16), pltpu.SemaphoreType.DMA],
        compiler_params=_params("arbitrary"),
    )(dproj, wg, dxres, x, mod)


def _halo_next(S, TS, BR, k):
    last = S // 8 - 1
    return pl.BlockSpec((8, BR), lambda i: (jnp.minimum((i + 1) * (TS // 8), last), k))


def _conv_a_bwd(proj, dcat, w8, BR):
    S = proj.shape[0]
    TS = _tile(S, 512)
    nt = S // TS

    def body(ab_ref, ac_ref, ax_ref, ag_ref, dy_ref, hc_ref, hx_ref, nab_ref, nag_ref, ndy_ref, w_ref, dp_ref, dw_ref, ext, dext):
        i = pl.program_id(0)

        @pl.when(i == 0)
        def _():
            dw_ref[...] = jnp.zeros_like(dw_ref)

        w0, w1, w2 = w_ref[0:1, :], w_ref[1:2, :], w_ref[2:3, :]
        ab, ac, ax, ag, dy = ab_ref[...], ac_ref[...], ax_ref[...], ag_ref[...], dy_ref[...]
        u = ac * ax
        ext[0:8, :] = jnp.where(i > 0, hc_ref[...] * hx_ref[...], 0.0)
        ext[8:8 + TS, :] = u
        u1, u2 = ext[pl.ds(7, TS), :], ext[pl.ds(6, TS), :]
        conv = u2 * w0 + u1 * w1 + u * w2
        sg = _silu(ag)
        dconv = dy * ab * sg
        dext[0:TS, :] = dconv
        dext[TS:TS + 8, :] = jnp.where(i < nt - 1, ndy_ref[...] * nab_ref[...] * _silu(nag_ref[...]), 0.0)
        du = w2 * dconv + w1 * dext[pl.ds(1, TS), :] + w0 * dext[pl.ds(2, TS), :]
        dp_ref[:, 0:BR] = (dy * conv * sg).astype(BF16)
        dp_ref[:, BR:2 * BR] = (du * ax).astype(BF16)
        dp_ref[:, 2 * BR:3 * BR] = (du * ac).astype(BF16)
        dp_ref[:, 3 * BR:4 * BR] = (dy * ab * conv * _dsilu(ag)).astype(BF16)
        dw_ref[0:1, :] += _colsum(dconv * u2)
        dw_ref[1:2, :] += _colsum(dconv * u1)
        dw_ref[2:3, :] += _colsum(dconv * u)

    rspec = pl.BlockSpec((8, BR), lambda i: (0, 0))
    return _call(
        body, name="conv_a_bwd", grid=(nt,),
        in_specs=[_chunk(TS, BR, 0), _chunk(TS, BR, 1), _chunk(TS, BR, 2), _chunk(TS, BR, 3), _chunk(TS, BR, 0),
                  _halo_prev(TS, BR, 1), _halo_prev(TS, BR, 2), _halo_next(S, TS, BR, 0), _halo_next(S, TS, BR, 3),
                  _halo_next(S, TS, BR, 0), rspec],
        out_specs=[pl.BlockSpec((TS, 4 * BR), lambda i: (i, 0)), rspec],
        out_shape=[jax.ShapeDtypeStruct((S, 12 * BR), BF16), jax.ShapeDtypeStruct((8, BR), F32)],
        scratch_shapes=[pltpu.VMEM((TS + 8, BR), F32), pltpu.VMEM((TS + 8, BR), F32)],
        compiler_params=_params("arbitrary"),
    )(proj, proj, proj, proj, dcat, proj, proj, proj, proj, dcat, w8)


def _attn_mix_bwd(proj, dcat, os_, ls_, head_ones, BR):
    S = proj.shape[0]
    TS = _tile(S, 512)

    def body(dy_ref, g_ref, o1, o2, o3, l1, l2, l3, ones_ref, d1, d2, d3, e1, e2, e3, dg_ref):
        w1, w2, w3 = _mix_weights(l1[...], l2[...], l3[...])
        mix = w1 * o1[...] + w2 * o2[...] + w3 * o3[...]
        g = g_ref[...]
        dy = dy_ref[...]
        dmix = dy * _silu(g)
        dg_ref[...] = dy * mix * _dsilu(g)
        t = jnp.dot(dmix * mix, ones_ref[...], preferred_element_type=F32, precision=lax.Precision.HIGHEST)
        d1[...] = w1 * dmix
        d2[...] = w2 * dmix
        d3[...] = w3 * dmix
        e1[...] = -w1 * t
        e2[...] = -w2 * t
        e3[...] = -w3 * t

    spec = pl.BlockSpec((TS, BR), lambda i: (i, 0))
    outs = _call(
        body, name="attn_mix_bwd", grid=(S // TS,),
        in_specs=[_chunk(TS, BR, 1), _chunk(TS, BR, 7)] + [spec] * 6 + [pl.BlockSpec((BR, BR), lambda i: (0, 0))],
        out_specs=[spec] * 7,
        out_shape=[jax.ShapeDtypeStruct((S, BR), F32)] * 7,
        compiler_params=_params("arbitrary"),
    )(dcat, proj, *os_, *ls_, head_ones)
    return outs[0:3], outs[3:6], outs[6]


def _attn_bwd(proj, do, e, lse, bias, dil, BR):
    S = proj.shape[0]
    HD, U, NU, nbu, hpb, HBK = _attn_geometry(S, BR, dil)
    scale = HD ** -0.5
    combos = [(n, r) for n in range(nbu) for r in range(dil)]

    def body(q_ref, kc_ref, kp_ref, vc_ref, vp_ref, do_ref, e_ref, l_ref, b_ref, dq_ref, dk_ref, dv_ref, db_ref,
             sbuf, dpbuf, pbuf, dsbuf, dkacc, dvacc, nxk, nxv, cark, carv):
        j = pl.program_id(1)
        mu = NU - 1 - j
        band, is_cur = _attn_mask()
        first = jnp.logical_and(band, jnp.logical_or(is_cur, mu > 0))

        @pl.when(j == 0)
        def _():
            cark[...] = jnp.zeros_like(cark)
            carv[...] = jnp.zeros_like(carv)
            db_ref[...] = jnp.zeros_like(db_ref)

        for ref in (dkacc, dvacc, nxk, nxv):
            ref[...] = jnp.zeros_like(ref)

        def keys(ref_c, ref_p, n, r):
            prev = ref_c[_attn_rows(n - 1, r, dil), :] if n > 0 else ref_p[_attn_rows(nbu - 1, r, dil), :]
            return jnp.concatenate([prev, ref_c[_attn_rows(n, r, dil), :]], axis=0).astype(BF16)

        for c, (n, r) in enumerate(combos):
            rows = _attn_rows(n, r, dil)
            q = (q_ref[rows, :] * scale).astype(BF16)
            dob = do_ref[rows, :].astype(BF16)
            kk, vv = keys(kc_ref, kp_ref, n, r), keys(vc_ref, vp_ref, n, r)
            for h in range(hpb):
                sl = slice(h * HD, (h + 1) * HD)
                sbuf[c * hpb + h] = _dot_nt(q[:, sl], kk[:, sl])
                dpbuf[c * hpb + h] = _dot_nt(dob[:, sl], vv[:, sl])
        dbs = [None] * hpb
        for c, (n, r) in enumerate(combos):
            rows = _attn_rows(n, r, dil)
            lse, ev = l_ref[rows, :], e_ref[rows, :]
            for h in range(hpb):
                one = slice(h * HD, h * HD + 1)
                s = jnp.where(band if n > 0 else first, sbuf[c * hpb + h] + b_ref[h], NEG)
                p = jnp.exp(s - lse[:, one])
                ds = p * (dpbuf[c * hpb + h] + ev[:, one])
                pbuf[c * hpb + h] = p.astype(BF16)
                dsbuf[c * hpb + h] = ds.astype(BF16)
                dbs[h] = ds if dbs[h] is None else dbs[h] + ds
        for h in range(hpb):
            db_ref[h] += dbs[h]
        for c, (n, r) in enumerate(combos):
            rows = _attn_rows(n, r, dil)
            q = (q_ref[rows, :] * scale).astype(BF16)
            dob = do_ref[rows, :].astype(BF16)
            kk = keys(kc_ref, kp_ref, n, r)
            dq, dkk, dvv = [], [], []
            for h in range(hpb):
                sl = slice(h * HD, (h + 1) * HD)
                dsb = dsbuf[c * hpb + h]
                dq.append(_dot(dsb, kk[:, sl]) * scale)
                dkk.append(_dot_tn(dsb, q[:, sl]))
                dvv.append(_dot_tn(pbuf[c * hpb + h], dob[:, sl]))
            dq_ref[rows, :] = jnp.concatenate(dq, axis=1)
            dkk, dvv = jnp.concatenate(dkk, axis=1), jnp.concatenate(dvv, axis=1)
            dkacc[rows, :] += dkk[BLK:2 * BLK, :]
            dvacc[rows, :] += dvv[BLK:2 * BLK, :]
            if n > 0:
                prow = _attn_rows(n - 1, r, dil)
                dkacc[prow, :] += dkk[0:BLK, :]
                dvacc[prow, :] += dvv[0:BLK, :]
            else:
                prow = _attn_rows(nbu - 1, r, dil)
                nxk[prow, :] += dkk[0:BLK, :]
                nxv[prow, :] += dvv[0:BLK, :]
        dk_ref[...] = dkacc[...] + cark[...]
        dv_ref[...] = dvacc[...] + carv[...]
        cark[...] = nxk[...]
        carv[...] = nxv[...]

    def cur(c):
        return pl.BlockSpec((U, ATT_LANES), lambda hb, j: (NU - 1 - j, c * HBK + hb))

    def prev(c):
        return pl.BlockSpec((U, ATT_LANES), lambda hb, j: (jnp.maximum(NU - 2 - j, 0), c * HBK + hb))

    own = pl.BlockSpec((U, ATT_LANES), lambda hb, j: (NU - 1 - j, hb))
    bspec = pl.BlockSpec((hpb, BLK, 2 * BLK), lambda hb, j: (hb, 0, 0))
    nhb = len(combos) * hpb
    unit = pltpu.VMEM((U, ATT_LANES), F32)
    return _call(
        body, name="attn_bwd_d%d" % dil, grid=(HBK, NU),
        in_specs=[cur(4), cur(5), prev(5), cur(6), prev(6), own, own, own, bspec],
        out_specs=[own, own, own, bspec],
        out_shape=[jax.ShapeDtypeStruct((S, BR), F32)] * 3 + [jax.ShapeDtypeStruct((8, BLK, 2 * BLK), F32)],
        scratch_shapes=[pltpu.VMEM((nhb, BLK, 2 * BLK), F32)] * 2 + [pltpu.VMEM((nhb, BLK, 2 * BLK), BF16)] * 2 + [unit] * 6,
        compiler_params=_params("arbitrary", "arbitrary"),
    )(proj, proj, proj, proj, proj, do, e, lse, bias)


def _attn_dsum(ds, dbg, dproj, BR):
    S = dbg.shape[0]
    TS = _tile(S, 512)

    def body(*refs):
        o_ref = refs[11]
        for k in range(3):
            o_ref[:, k * BR:(k + 1) * BR] = (refs[k][...] + refs[3 + k][...] + refs[6 + k][...]).astype(BF16)
        o_ref[:, 3 * BR:4 * BR] = refs[9][...].astype(BF16)

    spec = pl.BlockSpec((TS, BR), lambda i: (i, 0))
    return _call(
        body, name="attn_dsum", grid=(S // TS,),
        in_specs=[spec] * 10 + [pl.BlockSpec(memory_space=pl.ANY)],
        out_specs=pl.BlockSpec((TS, 4 * BR), lambda i: (i, 1)),
        out_shape=jax.ShapeDtypeStruct(dproj.shape, BF16),
        input_output_aliases={10: 0},
        compiler_params=_params("arbitrary"),
    )(*[t for trip in ds for t in trip], dbg, dproj)


def _lru_bwd(proj, h, dcat, rows, wa, wx, dproj, BR):
    S = proj.shape[0]
    TS = _tile(S, 256)
    PAD = TS // 2
    nt = S // TS

    def body(cx_ref, hx_ref, cg_ref, h_ref, hh_ref, dy_ref, r_ref, wa_ref, wx_ref, _,
             dp_ref, sum_ref, dwa_ref, dwx_ref, ext, hext, aext, dext, sa, sb, carry):
        i = pl.program_id(0)
        ti = nt - 1 - i

        @pl.when(i == 0)
        def _():
            sa[TS:TS + PAD, :] = jnp.ones((PAD, BR), F32)
            sb[TS:TS + PAD, :] = jnp.zeros((PAD, BR), F32)
            carry[...] = jnp.zeros_like(carry)
            dext[TS:TS + 8, :] = jnp.zeros((8, BR), F32)
            sum_ref[...] = jnp.zeros_like(sum_ref)
            dwa_ref[...] = jnp.zeros_like(dwa_ref)
            dwx_ref[...] = jnp.zeros_like(dwx_ref)

        w0, w1, w2, w3 = r_ref[0:1, :], r_ref[1:2, :], r_ref[2:3, :], r_ref[3:4, :]
        sp = r_ref[7:8, :]
        cx = cx_ref[...]
        ext[0:8, :] = jnp.where(ti > 0, hx_ref[...], 0.0)
        ext[8:8 + TS, :] = cx
        x3, x2, x1 = ext[pl.ds(5, TS), :], ext[pl.ds(6, TS), :], ext[pl.ds(7, TS), :]
        xc = x3 * w0 + x2 * w1 + x1 * w2 + cx * w3 + r_ref[4:5, :]
        wa_, wx_ = wa_ref[...], wx_ref[...]
        r, ig, a, mult = _lru_gates(xc, wa_, wx_, r_ref[5:6, :], r_ref[6:7, :], sp)
        cg, dy, hv = cg_ref[...], dy_ref[...], h_ref[...]
        dp_ref[:, BR:2 * BR] = (dy * hv * _dsilu(cg)).astype(BF16)
        aext[0:TS, :] = a
        aext[TS:TS + 8, :] = jnp.broadcast_to(carry[0:1, :], (8, BR))
        sa[0:TS, :] = aext[pl.ds(1, TS), :]
        sb[0:TS, :] = dy * _silu(cg)
        d = 1
        while d < TS:
            A, B = sa[0:TS, :], sb[0:TS, :]
            As, Bs = sa[pl.ds(d, TS), :], sb[pl.ds(d, TS), :]
            sb[0:TS, :] = B + A * Bs
            sa[0:TS, :] = A * As
            d *= 2
        gh = sb[0:TS, :] + sa[0:TS, :] * carry[1:2, :]
        carry[0:1, :] = a[0:1, :]
        carry[1:2, :] = gh[0:1, :]
        hext[0:8, :] = jnp.where(ti > 0, hh_ref[...], 0.0)
        hext[8:8 + TS, :] = hv
        da = gh * hext[pl.ds(7, TS), :]
        dmult = gh * ig * xc
        dig = gh * mult * xc
        dxc = gh * mult * ig
        dla = da * a - dmult * a * a / mult
        dpr = dla * (-LRU_C) * sp * r * (1.0 - r)
        dpi = dig * ig * (1.0 - ig)
        xb, dprb, dpib = xc.astype(BF16), dpr.astype(BF16), dpi.astype(BF16)
        dwa_ref[...] += _dot_tn(xb, dprb)
        dwx_ref[...] += _dot_tn(xb, dpib)
        dxc = dxc + _dot_nt(dprb, wa_) + _dot_nt(dpib, wx_)
        sum_ref[0:1, :] += _colsum(dxc * x3)
        sum_ref[1:2, :] += _colsum(dxc * x2)
        sum_ref[2:3, :] += _colsum(dxc * x1)
        sum_ref[3:4, :] += _colsum(dxc * cx)
        sum_ref[4:5, :] += _colsum(dxc)
        sum_ref[5:6, :] += _colsum(dpr)
        sum_ref[6:7, :] += _colsum(dpi)
        sum_ref[7:8, :] += _colsum(dla * (-LRU_C) * r)
        dext[0:TS, :] = dxc
        dcx = w3 * dxc + w2 * dext[pl.ds(1, TS), :] + w1 * dext[pl.ds(2, TS), :] + w0 * dext[pl.ds(3, TS), :]
        dext[TS:TS + 8, :] = dxc[0:8, :]
        dp_ref[:, 0:BR] = dcx.astype(BF16)

    def rev(k):
        return pl.BlockSpec((TS, BR), lambda i: (nt - 1 - i, k))

    def rev_halo(k):
        return pl.BlockSpec((8, BR), lambda i: (jnp.maximum((nt - 1 - i) * (TS // 8) - 1, 0), k))

    full = pl.BlockSpec((BR, BR), lambda i: (0, 0))
    rspec = pl.BlockSpec((8, BR), lambda i: (0, 0))
    return _call(
        body, name="lru_bwd", grid=(nt,),
        in_specs=[rev(8), rev_halo(8), rev(9), rev(0), rev_halo(0), rev(2), rspec, full, full, pl.BlockSpec(memory_space=pl.ANY)],
        out_specs=[pl.BlockSpec((TS, 2 * BR), lambda i: (nt - 1 - i, 4)), rspec, full, full],
        input_output_aliases={9: 0},
        out_shape=[jax.ShapeDtypeStruct(dproj.shape, BF16), jax.ShapeDtypeStruct((8, BR), F32),
                   jax.ShapeDtypeStruct((BR, BR), F32), jax.ShapeDtypeStruct((BR, BR), F32)],
        scratch_shapes=[pltpu.VMEM((TS + 8, BR), F32)] * 4 + [pltpu.VMEM((TS + PAD, BR), F32)] * 2 + [pltpu.VMEM((8, BR), F32)],
        compiler_params=_params("arbitrary"),
    )(proj, proj, proj, h, h, dcat, rows, wa, wx, dproj)


def _s5_bwd(proj, dcat, xre_all, xim_all, a8, cl, bbre, bbim, cre, cim, rows, wg, dproj, BR):
    S = proj.shape[0]
    W = a8.shape[1]
    TS = _tile(S, 256)
    nt = S // TS
    CB, SB, J = S5_CB, S5_SB, S5_LAGS
    nblk = BR // CB

    def body(u_ref, dg_ref, dy_ref, xre_ref, xim_ref, hre_ref, him_ref, a8_ref, cl_ref, bbre_ref, bbim_ref, cre_ref, cim_ref,
             r_ref, wg_ref, _, dp_ref, sum_ref, dwg_ref, da_ref, dbbre_ref, dbbim_ref, dcre_ref, dcim_ref,
             sre, sim, ere, eim, ypre_s, dext, dylag, carry):
        i = pl.program_id(0)
        ti = nt - 1 - i

        @pl.when(i == 0)
        def _():
            dext[TS:TS + 8, :] = jnp.zeros((8, BR), F32)
            carry[...] = jnp.zeros_like(carry)
            for ref in (sum_ref, dwg_ref, da_ref, dbbre_ref, dbbim_ref, dcre_ref, dcim_ref):
                ref[...] = jnp.zeros_like(ref)

        u, dg, dy = u_ref[...], dg_ref[...], dy_ref[...]
        for m in range(nblk):
            cs, ws = slice(m * CB, (m + 1) * CB), slice(m * SB, (m + 1) * SB)
            ypre_s[:, cs] = (_dot(xre_ref[:, ws].astype(BF16), cre_ref[ws, :]) - _dot(xim_ref[:, ws].astype(BF16), cim_ref[ws, :]))
        ypre = ypre_s[...] + r_ref[0:1, :] * u
        yg = _gelu(ypre)
        wg_ = wg_ref[...]
        s = _sigmoid(_dot(yg.astype(BF16), wg_) + r_ref[1:2, :])
        dgl = dy * _silu(dg)
        dp_ref[:, BR:2 * BR] = (dy * yg * s * _dsilu(dg)).astype(BF16)
        dps = dgl * yg * s * (1.0 - s)
        dpsb = dps.astype(BF16)
        sum_ref[1:2, :] += _colsum(dps)
        dwg_ref[...] += _dot_tn(yg.astype(BF16), dpsb)
        dyp = (dgl * s + _dot_nt(dpsb, wg_)) * _dgelu(ypre)
        sum_ref[0:1, :] += _colsum(dyp * u)
        dext[0:TS, :] = dyp
        for m in range(nblk):
            cs, ws = slice(m * CB, (m + 1) * CB), slice(m * SB, (m + 1) * SB)
            dypb = dext[0:TS, cs].astype(BF16)
            xre, xim = xre_ref[:, ws], xim_ref[:, ws]
            dcre_ref[ws, :] += _dot_tn(xre.astype(BF16), dypb)
            dcim_ref[ws, :] -= _dot_tn(xim.astype(BF16), dypb)
            for j in range(J):
                dylag[:, j * CB:(j + 1) * CB] = dext[pl.ds(j, TS), cs].astype(BF16)
            w = _dot(dylag[...], cl_ref[m])
            sre[...] = w[:, 0:SB]
            sim[...] = w[:, SB:2 * SB]
            ar = jnp.broadcast_to(a8_ref[0:1, ws], (8, SB))
            ai = jnp.broadcast_to(a8_ref[1:2, ws], (8, SB))
            gr, gi = carry[0:8, ws], carry[8:16, ws]
            for g in reversed(range(TS // 8)):
                rg = slice(8 * g, 8 * g + 8)
                gr, gi = sre[rg, :] + ar * gr + ai * gi, sim[rg, :] + ar * gi - ai * gr
                sre[rg, :] = gr
                sim[rg, :] = gi
            carry[0:8, ws] = gr
            carry[8:16, ws] = gi
            gre, gim = sre[...], sim[...]
            ere[0:8, :] = jnp.where(ti > 0, hre_ref[:, ws], 0.0)
            eim[0:8, :] = jnp.where(ti > 0, him_ref[:, ws], 0.0)
            ere[8:8 + TS, :] = xre
            eim[8:8 + TS, :] = xim
            xpr, xpi = ere[pl.ds(7, TS), :], eim[pl.ds(7, TS), :]
            da_ref[0:1, ws] += _colsum(gre * xpr + gim * xpi)
            da_ref[1:2, ws] += _colsum(gim * xpr - gre * xpi)
            greb, gimb = gre.astype(BF16), gim.astype(BF16)
            ub = u_ref[:, cs].astype(BF16)
            dbbre_ref[cs, :] += _dot_tn(ub, greb)
            dbbim_ref[cs, :] += _dot_tn(ub, gimb)
            du = _dot_nt(greb, bbre_ref[cs, :]) + _dot_nt(gimb, bbim_ref[cs, :]) + r_ref[0:1, cs] * dext[0:TS, cs]
            dp_ref[:, cs] = du.astype(BF16)
        dext[TS:TS + 8, :] = dext[0:8, :]

    def rev(width, k):
        return pl.BlockSpec((TS, width), lambda i: (nt - 1 - i, k))

    def rev_halo(width):
        return pl.BlockSpec((8, width), lambda i: (jnp.maximum((nt - 1 - i) * (TS // 8) - 1, 0), 0))

    def const(shape):
        return pl.BlockSpec(shape, lambda i: (0,) * len(shape))

    return _call(
        body, name="s5_bwd", grid=(nt,),
        in_specs=[rev(BR, 10), rev(BR, 11), rev(BR, 3), rev(W, 0), rev(W, 0), rev_halo(W), rev_halo(W),
                  const((8, W)), const((nblk, J * CB, 2 * SB)), const((BR, SB)), const((BR, SB)), const((W, CB)), const((W, CB)),
                  const((8, BR)), const((BR, BR)), pl.BlockSpec(memory_space=pl.ANY)],
        input_output_aliases={15: 0},
        out_specs=[pl.BlockSpec((TS, 2 * BR), lambda i: (nt - 1 - i, 5)), const((8, BR)), const((BR, BR)), const((8, W)),
                   const((BR, SB)), const((BR, SB)), const((W, CB)), const((W, CB))],
        out_shape=[jax.ShapeDtypeStruct(dproj.shape, BF16), jax.ShapeDtypeStruct((8, BR), F32), jax.ShapeDtypeStruct((BR, BR), F32),
                   jax.ShapeDtypeStruct((8, W), F32), jax.ShapeDtypeStruct((BR, SB), F32), jax.ShapeDtypeStruct((BR, SB), F32),
                   jax.ShapeDtypeStruct((W, CB), F32), jax.ShapeDtypeStruct((W, CB), F32)],
        scratch_shapes=[pltpu.VMEM((TS, SB), F32)] * 2 + [pltpu.VMEM((TS + 8, SB), F32)] * 2
        + [pltpu.VMEM((TS, BR), F32), pltpu.VMEM((TS + 8, BR), F32), pltpu.VMEM((TS, J * CB), BF16), pltpu.VMEM((16, W), F32)],
        compiler_params=_params("arbitrary"),
    )(proj, proj, dcat, xre_all, xim_all, xre_all, xim_all, a8, cl, bbre, bbim, cre, cim, rows, wg, dproj)


_WEIGHTS = ['rel_bias', 'w_ada', 'b_ada', 'w_in', 'conv_a', 'conv_c', 'conv_c_b', 'lru_wa', 'lru_ba', 'lru_wx', 'lru_bx',
            'lru_lambda', 's5_lam_re', 's5_lam_im', 's5_log_dt', 's5_b_re', 's5_b_im', 's5_c_re', 's5_c_im', 's5_d',
            's5_w_glu', 's5_b_glu', 'w_out', 'ln_g', 'ln_b']
_LAYER_SMALL = ['conv_a', 'conv_c', 'conv_c_b', 'lru_wa', 'lru_ba', 'lru_wx', 'lru_bx', 'lru_lambda', 's5_lam_re', 's5_lam_im',
                's5_log_dt', 's5_b_re', 's5_b_im', 's5_c_re', 's5_c_im', 's5_d', 's5_b_glu']
_SMALL = ['rel_bias'] + _LAYER_SMALL + ['ln_g', 'ln_b']


def _t5_bucket(dist):
    max_exact = REL_BUCKETS // 2
    nf = jnp.maximum(dist, 1).astype(F32)
    large = max_exact + (jnp.log(nf / max_exact) / math.log(REL_MAX_DIST / max_exact)
                         * (REL_BUCKETS - max_exact)).astype(jnp.int32)
    large = jnp.minimum(large, REL_BUCKETS - 1)
    return jnp.where(dist < max_exact, dist, large)


def _bias_tables(rel_bias):
    i = jnp.arange(BLK)[:, None]
    j = jnp.arange(2 * BLK)[None, :]
    delta = i + BLK - j
    out = []
    for window, dil in DILATIONS:
        bucket = _t5_bucket(jnp.clip(delta, 0, window // dil) * dil)
        onehot = (bucket[:, :, None] == jnp.arange(REL_BUCKETS)[None, None, :]).astype(F32)
        out.append(jnp.einsum('ijb,bh->hij', onehot, rel_bias, precision=lax.Precision.HIGHEST))
    return jnp.stack(out)


def _prep_layer(q):
    eye8 = jnp.eye(8, dtype=F32)
    G = q['s5_lam_re'].shape[0]
    nblk = G // 8

    def block_diag(w):
        hd = w.shape[1]
        return (w[:, :, None, :] * eye8[:, None, :, None]).reshape(8 * hd, 8 * hd)

    def compact_b(bb):
        t = jnp.transpose(bb.reshape(nblk, 8, S5_STATE, S5_CH), (0, 1, 3, 2))
        return (t[:, :, :, None, :] * eye8[None, :, None, :, None]).reshape(G * S5_CH, 8 * S5_STATE)

    def compact_c(cc):
        t = jnp.transpose(cc.reshape(nblk, 8, S5_CH, S5_STATE), (0, 1, 3, 2))
        return (t[:, :, :, None, :] * eye8[None, :, None, :, None]).reshape(G * S5_STATE, 8 * S5_CH)

    lam_re, lam_im = q['s5_lam_re'], q['s5_lam_im']
    dt = jnp.exp(q['s5_log_dt'])[:, None]
    mag = jnp.exp(lam_re * dt)
    ab_re = mag * jnp.cos(lam_im * dt)
    ab_im = mag * jnp.sin(lam_im * dt)
    den = lam_re * lam_re + lam_im * lam_im
    f_re = ((ab_re - 1.0) * lam_re + ab_im * lam_im) / den
    f_im = (ab_im * lam_re - (ab_re - 1.0) * lam_im) / den
    b_re, b_im = q['s5_b_re'], q['s5_b_im']
    bb_re = f_re[..., None] * b_re - f_im[..., None] * b_im
    bb_im = f_re[..., None] * b_im + f_im[..., None] * b_re
    return dict(
        conv_a=q['conv_a'], conv_c=q['conv_c'], conv_c_b=q['conv_c_b'], lru_ba=q['lru_ba'], lru_bx=q['lru_bx'],
        sp=jax.nn.softplus(-q['lru_lambda']), wa=block_diag(q['lru_wa']), wx=block_diag(q['lru_wx']),
        ar=ab_re.reshape(-1), ai=ab_im.reshape(-1), bbre=compact_b(bb_re), bbim=compact_b(bb_im),
        cre=compact_c(q['s5_c_re']), cim=compact_c(q['s5_c_im']), s5_d=q['s5_d'], s5_b_glu=q['s5_b_glu'])


def _s5_lag_weights(kp):
    J, CB, SB = S5_LAGS, S5_CB, S5_SB
    ar, ai = kp['ar'], kp['ai']
    W = ar.shape[0]
    nblk = W // SB
    cr, ci = jnp.ones_like(ar), jnp.zeros_like(ai)
    pows = []
    for _ in range(J + 1):
        pows.append((cr, ci))
        cr, ci = cr * ar - ci * ai, cr * ai + ci * ar
    pwr = jnp.stack([p[0] for p in pows[:J]]).reshape(J, nblk, 1, SB)
    pwi = jnp.stack([p[1] for p in pows[:J]]).reshape(J, nblk, 1, SB)

    def lagged(re, im, sign):
        out = jnp.concatenate([re * pwr - sign * im * pwi, sign * re * pwi + im * pwr], axis=-1)
        return jnp.transpose(out, (1, 0, 2, 3)).reshape(nblk, J * CB, 2 * SB).astype(BF16)

    bbl = lagged(kp['bbre'].reshape(1, nblk, CB, SB), kp['bbim'].reshape(1, nblk, CB, SB), 1.0)
    c0r = jnp.transpose(kp['cre'].reshape(nblk, SB, CB), (0, 2, 1))[None]
    c0i = -jnp.transpose(kp['cim'].reshape(nblk, SB, CB), (0, 2, 1))[None]
    cl = lagged(c0r, c0i, -1.0)
    return _rows8([pows[J][0], pows[J][1]], W), bbl, cl


def _rows8(vecs, width):
    rows = [v.reshape(1, width).astype(F32) for v in vecs]
    return jnp.concatenate(rows + [jnp.zeros((8 - len(rows), width), F32)], axis=0)


def kernel(x, c, rel_bias, w_ada, b_ada, w_in, conv_a, conv_c, conv_c_b, lru_wa, lru_ba, lru_wx, lru_bx, lru_lambda, s5_lam_re, s5_lam_im, s5_log_dt, s5_b_re, s5_b_im, s5_c_re, s5_c_im, s5_d, s5_w_glu, s5_b_glu, w_out, ln_g, ln_b, loss_target, m_rel_bias, m_w_ada, m_b_ada, m_w_in, m_conv_a, m_conv_c, m_conv_c_b, m_lru_wa, m_lru_ba, m_lru_wx, m_lru_bx, m_lru_lambda, m_s5_lam_re, m_s5_lam_im, m_s5_log_dt, m_s5_b_re, m_s5_b_im, m_s5_c_re, m_s5_c_im, m_s5_d, m_s5_w_glu, m_s5_b_glu, m_w_out, m_ln_g, m_ln_b, v_rel_bias, v_w_ada, v_b_ada, v_w_in, v_conv_a, v_conv_c, v_conv_c_b, v_lru_wa, v_lru_ba, v_lru_wx, v_lru_bx, v_lru_lambda, v_s5_lam_re, v_s5_lam_im, v_s5_log_dt, v_s5_b_re, v_s5_b_im, v_s5_c_re, v_s5_c_im, v_s5_d, v_s5_w_glu, v_s5_b_glu, v_w_out, v_ln_g, v_ln_b):
    a = dict(locals())
    me = 4 * lax.axis_index("x") + 2 * lax.axis_index("y") + lax.axis_index("c")
    x0, target = x[0], loss_target[0]
    S, D = x0.shape
    BR = D // 4
    NS = w_in.shape[2]
    SH = BR // NDEV
    assert S % (BLK * DILATIONS[-1][1]) == 0 and BR % (8 * S5_CH) == 0

    small = _exchange(_pack([c, conv_a, conv_c]), True, "gather_small").reshape(NDEV, -1)
    c_all = small[:, :D]
    na, nc = DEPTH * 3 * SH, DEPTH * 4 * SH
    conv_a_full = jnp.transpose(small[:, D:D + na].reshape(NDEV, DEPTH, 3, SH), (1, 2, 0, 3)).reshape(DEPTH, 3, BR)
    conv_c_full = jnp.transpose(small[:, D + na:D + na + nc].reshape(NDEV, DEPTH, 4, SH), (1, 2, 0, 3)).reshape(DEPTH, 4, BR)
    me1 = me.reshape(1).astype(jnp.int32)
    w_in_bf, w_out_bf, w_glu_bf = w_in.astype(BF16), w_out.astype(BF16), s5_w_glu.astype(BF16)

    b_cols = lax.dynamic_slice(b_ada, (0, me * NS), (DEPTH, NS)).reshape(DEPTH, 1, NS)
    ada_all = _exchange(_ada_cols(c_all, w_ada, b_cols), True, "gather_ada")
    ada_me = lax.dynamic_index_in_dim(ada_all, me, axis=2, keepdims=False)
    ada_me = jnp.transpose(ada_me, (1, 0, 2)).reshape(DEPTH, 3 * D)
    shift, scale, gate = ada_me[:, :D], ada_me[:, D:2 * D], ada_me[:, 2 * D:]

    bias_tabs, bias_pull = jax.vjp(_bias_tables, rel_bias)
    HD = BR // 8
    head_ones = jnp.kron(jnp.eye(8, dtype=F32), jnp.ones((HD, HD), F32))

    saved = []
    xl = x0
    for l in range(DEPTH):
        q = {n: a[n][l] for n in _LAYER_SMALL}
        q['conv_a'], q['conv_c'] = conv_a_full[l], conv_c_full[l]
        kp, pull = jax.vjp(_prep_layer, q)
        a8, bbl, cl = _s5_lag_weights(kp)
        mod = _rows8([1.0 + scale[l], shift[l]], D)
        arows = _rows8(list(kp['conv_a']), BR)
        lrows = _rows8(list(kp['conv_c']) + [kp['conv_c_b'], kp['lru_ba'], kp['lru_bx'], kp['sp']], BR)
        srows = _rows8([kp['s5_d'], kp['s5_b_glu']], BR)
        orows = _rows8([1.0 + gate[l], ln_g[l], ln_b[l]], D)
        wa, wx = kp['wa'].astype(BF16), kp['wx'].astype(BF16)
        s5w = [kp[n].astype(BF16) for n in ('bbre', 'bbim', 'cre', 'cim')]

        hbf = _modulate(xl, mod)
        proj, w_in_l, w_out_l, w_glu_l = _proj_gather(hbf, w_in_bf[l], [w_out_bf[l], w_glu_bf[l]], me1)
        w_out_l, w_glu_l = w_out_l.reshape(D, D), w_glu_l.reshape(BR, BR)
        ya = _conv_a_fwd(proj, arows, BR)
        os_, ls_ = [], []
        for g, (_, dil) in enumerate(DILATIONS):
            o, lse = _attn_fwd(proj, bias_tabs[g], dil, BR)
            os_.append(o)
            ls_.append(lse)
        yb = _attn_mix_fwd(proj, os_, ls_, BR)
        yc, hs = _lru_fwd(proj, lrows, wa, wx, BR)
        yd, xre, xim = _s5_fwd(proj, a8, bbl, s5w[2], s5w[3], srows, w_glu_l, BR)
        if l < DEPTH - 1:
            xn, y, cat = _out_ln([ya, yb, yc, yd], w_out_l, xl, orows)
        else:
            xn, y, cat, loss_local = _out_ln([ya, yb, yc, yd], w_out_l, xl, orows, target)
        saved.append(dict(x=xl, proj=proj, hbf=hbf, os=os_, ls=ls_, hs=hs, xre=xre, xim=xim, y=y, cat=cat, pull=pull,
                          mod=mod, arows=arows, lrows=lrows, srows=srows, orows=orows, wa=wa, wx=wx, s5w=s5w,
                          qw=(a8, cl), w_in=w_in_l, w_out=w_out_l, w_glu=w_glu_l))
        xl = xn

    dout = xl

    dbias = jnp.zeros_like(bias_tabs)
    lgrads, dada, dw_in_parts, dw_out_parts, dw_glu_parts = [None] * DEPTH, [None] * DEPTH, [None] * DEPTH, [None] * DEPTH, [None] * DEPTH
    for l in reversed(range(DEPTH)):
        sv = saved[l]
        proj = sv['proj']
        dxres, dyb, dcat, lnsum = _ln_bwd(dout, sv['x'], sv['y'], sv['orows'], sv['w_out'])
        dw_out_parts[l] = _dw_scatter(sv['cat'], dyb, D // NDEV, True, me1, "dw_out_scatter")
        dproj, asum = _conv_a_bwd(proj, dcat, sv['arows'], BR)
        dos, es, dbg = _attn_mix_bwd(proj, dcat, sv['os'], sv['ls'], head_ones, BR)
        dqkv, dbs = [], []
        for g, (_, dil) in enumerate(DILATIONS):
            dq_, dk_, dv_, db_ = _attn_bwd(proj, dos[g], es[g], sv['ls'][g], bias_tabs[g], dil, BR)
            dqkv.append((dq_, dk_, dv_))
            dbs.append(db_)
        dbias = dbias + jnp.stack(dbs)
        dproj = _attn_dsum(dqkv, dbg, dproj, BR)
        dproj, lsum, dwa, dwx = _lru_bwd(proj, sv['hs'], dcat, sv['lrows'], sv['wa'], sv['wx'], dproj, BR)
        dproj, ssum, dwg, da_, dbbre, dbbim, dcre, dcim = _s5_bwd(
            proj, dcat, sv['xre'], sv['xim'], *sv['qw'], *sv['s5w'], sv['srows'], sv['w_glu'], dproj, BR)
        dw_glu_parts[l] = dwg.reshape(NDEV, BR // NDEV, BR)
        dout, msum = _dh(dproj, sv['w_in'], dxres, sv['x'], sv['mod'])
        dw_in_parts[l] = _dw_scatter(sv['hbf'], dproj, NS, False, me1, "dw_in_scatter")
        dada[l] = jnp.concatenate([msum[1], msum[0], lnsum[2]])
        dkp = dict(conv_a=asum[0:3], conv_c=lsum[0:4], conv_c_b=lsum[4], lru_ba=lsum[5], lru_bx=lsum[6], sp=lsum[7],
                   wa=dwa, wx=dwx, ar=da_[0], ai=da_[1], bbre=dbbre, bbim=dbbim, cre=dcre, cim=dcim,
                   s5_d=ssum[0], s5_b_glu=ssum[1])
        lgrads[l] = dict(sv['pull'](dkp)[0], ln_g=lnsum[0], ln_b=lnsum[1])
    grad_x = dout[None]

    out = {}

    def put(name, res):
        out[name] = res

    put('w_in', _adamw(dw_in_parts, w_in, m_w_in, v_w_in, "adamw_w_in"))
    put('w_out', _adamw(dw_out_parts, w_out, m_w_out, v_w_out, "adamw_w_out"))
    glu_parts = [_exchange(dw_glu_parts[l], False, "scatter_w_glu") for l in range(DEPTH)]
    put('s5_w_glu', _adamw(glu_parts, s5_w_glu, m_s5_w_glu, v_s5_w_glu, "adamw_w_glu"))

    dada_all = _exchange(jnp.stack(dada), True, "gather_dada")
    dada_cols = jnp.transpose(lax.dynamic_slice(dada_all, (0, 0, me * NS), (NDEV, DEPTH, NS)), (1, 0, 2))
    gw_ada, gb_ada = _ada_grads(c_all, dada_all, dada_cols)
    put('w_ada', _adamw([gw_ada[l] for l in range(DEPTH)], w_ada, m_w_ada, v_w_ada, "adamw_w_ada"))

    local = {'rel_bias': bias_pull(dbias)[0]}
    for n in _LAYER_SMALL + ['ln_g', 'ln_b']:
        local[n] = jnp.stack([lgrads[l][n] for l in range(DEPTH)])
    local['loss'] = loss_local.reshape(1)
    packed = _SMALL + ['loss']
    shapes = [local[n].shape for n in packed]
    summed = _sum8(_exchange(_pack([local[n] for n in packed]), True, "gather_small_grads"), "sum_small_grads")
    gs = dict(zip(packed, _unpack(summed, shapes)))
    loss = gs['loss'][0]
    gs['conv_a'] = lax.dynamic_slice_in_dim(gs['conv_a'], me * SH, SH, axis=2)
    gs['conv_c'] = lax.dynamic_slice_in_dim(gs['conv_c'], me * SH, SH, axis=2)
    gs['b_ada'] = gb_ada
    names = _SMALL + ['b_ada']
    shapes = [a[n].shape for n in names]
    res = _adamw([_pack([gs[n] for n in names])], _pack([a[n] for n in names])[None], _pack([a['m_' + n] for n in names])[None],
                 _pack([a['v_' + n] for n in names])[None], "adamw_small")
    unpacked = [_unpack(r[0], shapes) for r in res]
    for k, n in enumerate(names):
        put(n, tuple(unpacked[j][k] for j in range(4)))

    return (loss, grad_x, *[out[n][0] for n in _WEIGHTS], *[out[n][1] for n in _WEIGHTS],
            *[out[n][2] for n in _WEIGHTS], *[out[n][3] for n in _WEIGHTS])
```

```python
import functools
import math

import jax
import jax.numpy as jnp
from jax import lax
from jax.experimental import pallas as pl
from jax.experimental.pallas import tpu as pltpu

F32 = jnp.float32
BF16 = jnp.bfloat16

NDEV = 8
DEPTH = 2
BLK = 128
DILATIONS = ((128, 1), (512, 4), (2048, 16))
REL_BUCKETS = 32
REL_MAX_DIST = 2048
LRU_C = 8.0
S5_CH = 16
S5_STATE = 64
ALPHA = (2 * DEPTH) ** 0.25
LN_EPS = 1e-5
ADAM_LR, ADAM_B1, ADAM_B2, ADAM_EPS, ADAM_WD, ADAM_STEP = 0.001, 0.9, 0.999, 1e-08, 0.01, 10
NEG = -1e30
VMEM_LIMIT = 56 * 1024 * 1024


def _call(body, **kw):
    return pl.pallas_call(body, **kw)


def _params(*sem):
    return pltpu.CompilerParams(dimension_semantics=sem, vmem_limit_bytes=VMEM_LIMIT)


def _sigmoid(x):
    return 1.0 / (1.0 + jnp.exp(-x))


def _silu(x):
    return x * _sigmoid(x)


def _dsilu(x):
    s = _sigmoid(x)
    return s * (1.0 + x * (1.0 - s))


_GC = math.sqrt(2.0 / math.pi)


def _gelu(x):
    return 0.5 * x * (1.0 + jnp.tanh(_GC * (x + 0.044715 * x * x * x)))


def _dgelu(x):
    t = jnp.tanh(_GC * (x + 0.044715 * x * x * x))
    return 0.5 * (1.0 + t) + 0.5 * x * (1.0 - t * t) * _GC * (1.0 + 3.0 * 0.044715 * x * x)


def _expm1(x):
    series = x * (1.0 + x * (1.0 / 2) * (1.0 + x * (1.0 / 3) * (1.0 + x * (1.0 / 4) * (1.0 + x * (1.0 / 5) * (1.0 + x * (1.0 / 6))))))
    return jnp.where(jnp.abs(x) < 0.25, series, jnp.exp(x) - 1.0)


def _dot(a, b):
    return jnp.dot(a, b, preferred_element_type=F32)


def _dot_nt(a, b):
    return lax.dot_general(a, b, (((1,), (1,)), ((), ())), preferred_element_type=F32)


def _dot_tn(a, b):
    return lax.dot_general(a, b, (((0,), (0,)), ((), ())), preferred_element_type=F32)


def _colsum(v):
    return jnp.sum(v, axis=0, keepdims=True)


def _exchange(x, gather, name):
    blk = x.shape if gather else x.shape[1:]

    def gather_body(x_ref, o_ref, send_sems, recv_sems, local_sem):
        ix, iy, ic = lax.axis_index("x"), lax.axis_index("y"), lax.axis_index("c")
        me, sibling = (ix, iy, ic), (ix, iy, 1 - ic)
        chips = [(1 - ix, iy), (ix, 1 - iy), (1 - ix, 1 - iy)]

        def slot(px, py, pc):
            return o_ref.at[4 * px + 2 * py + pc]

        def copy(k, block, to, src=None):
            return pltpu.make_async_remote_copy(
                src_ref=slot(*block) if src is None else src, dst_ref=slot(*block), send_sem=send_sems.at[k],
                recv_sem=recv_sems.at[k], device_id=to, device_id_type=pl.DeviceIdType.MESH)

        mine = pltpu.make_async_copy(x_ref, slot(*me), local_sem)
        mine.start()
        first = [copy(0, me, sibling, src=x_ref)] + [copy(1 + j, me, (*chip, ic), src=x_ref) for j, chip in enumerate(chips)]
        for cp in first:
            cp.start()
        passed = [copy(4 + j, (*chip, ic), sibling) for j, chip in enumerate(chips)]
        for j, chip in enumerate(chips):
            copy(1 + j, (*chip, ic), me).wait_recv()
            passed[j].start()
        copy(0, sibling, me).wait_recv()
        for j, chip in enumerate(chips):
            copy(4 + j, (*chip, 1 - ic), me).wait_recv()
        for cp in first + passed:
            cp.wait_send()
        mine.wait()

    def scatter_body(x_ref, o_ref, send_sems, recv_sems, local_sem):
        ix, iy, ic = lax.axis_index("x"), lax.axis_index("y"), lax.axis_index("c")
        me = 4 * ix + 2 * iy + ic
        local = pltpu.make_async_copy(x_ref.at[me], o_ref.at[me], local_sem)
        local.start()
        copies = []
        for k in range(1, NDEV):
            px = 1 - ix if k & 4 else ix
            py = 1 - iy if k & 2 else iy
            pc = 1 - ic if k & 1 else ic
            peer = 4 * px + 2 * py + pc
            cp = pltpu.make_async_remote_copy(
                src_ref=x_ref.at[peer], dst_ref=o_ref.at[me], send_sem=send_sems.at[k - 1], recv_sem=recv_sems.at[k - 1],
                device_id=(px, py, pc), device_id_type=pl.DeviceIdType.MESH)
            cp.start()
            copies.append(cp)
        for cp in copies:
            cp.wait()
        local.wait()

    return _call(
        gather_body if gather else scatter_body, name=name,
        out_shape=jax.ShapeDtypeStruct((NDEV,) + tuple(blk), x.dtype),
        in_specs=[pl.BlockSpec(memory_space=pl.ANY)],
        out_specs=pl.BlockSpec(memory_space=pl.ANY),
        scratch_shapes=[pltpu.SemaphoreType.DMA((NDEV - 1,)), pltpu.SemaphoreType.DMA((NDEV - 1,)), pltpu.SemaphoreType.DMA],
    )(x)


def _pack(arrs):
    parts = []
    for a in arrs:
        n = math.prod(a.shape)
        parts.append(jnp.pad(a.reshape(-1).astype(F32), (0, _pack_rows(n) * 128 - n)).reshape(-1, 128))
    rows = sum(p.shape[0] for p in parts)
    total = rows if rows <= 512 else -(-rows // 512) * 512
    if total > rows:
        parts.append(jnp.zeros((total - rows, 128), F32))
    return jnp.concatenate(parts, axis=0)


def _pack_rows(n):
    return -(-n // 1024) * 8


def _unpack(packed, shapes):
    out, row = [], 0
    for s in shapes:
        n = math.prod(s)
        rows = _pack_rows(n)
        out.append(packed[row:row + rows].reshape(-1)[:n].reshape(s))
        row += rows
    return out


def _sum8(parts, name):
    _, R, C = parts.shape
    TR = R
    for cand in (512, 256, 128, 64, 32, 16, 8):
        if R % cand == 0:
            TR = cand
            break

    def body(p_ref, o_ref):
        acc = p_ref[0]
        for d in range(1, NDEV):
            acc = acc + p_ref[d]
        o_ref[...] = acc

    return _call(
        body, name=name, grid=(R // TR,),
        in_specs=[pl.BlockSpec((NDEV, TR, C), lambda i: (0, i, 0))],
        out_specs=pl.BlockSpec((TR, C), lambda i: (i, 0)),
        out_shape=jax.ShapeDtypeStruct((R, C), F32),
        compiler_params=_params("arbitrary"),
    )(parts)


def _adamw(g_layers, w, m, v, name):
    L, R, C = w.shape
    g0 = g_layers[0]
    nparts = g0.shape[0] if g0.ndim == 3 else 0
    TR = R
    for cand in (256, 128, 64, 32, 16):
        if R % cand == 0 and cand * C <= 128 * 1024:
            TR = cand
            break

    def body(*refs):
        g_refs = refs[:L]
        w_ref, m_ref, v_ref, go_ref, d_ref, mo_ref, vo_ref = refs[L:]
        for ll in range(L):
            @pl.when(pl.program_id(0) == ll)
            def _(g_ref=g_refs[ll]):
                if nparts:
                    g = g_ref[0].astype(F32)
                    for d in range(1, nparts):
                        g = g + g_ref[d].astype(F32)
                else:
                    g = g_ref[...]
                m2 = ADAM_B1 * m_ref[0] + (1.0 - ADAM_B1) * g
                v2 = ADAM_B2 * v_ref[0] + (1.0 - ADAM_B2) * (g * g)
                m_hat = m2 / (1.0 - ADAM_B1 ** ADAM_STEP)
                v_hat = v2 / (1.0 - ADAM_B2 ** ADAM_STEP)
                go_ref[0] = g
                d_ref[0] = -ADAM_LR * (m_hat / (jnp.sqrt(v_hat) + ADAM_EPS) + ADAM_WD * w_ref[0])
                mo_ref[0] = m2
                vo_ref[0] = v2

    def gspec(ll):
        if nparts:
            return pl.BlockSpec((nparts, TR, C), lambda l, i: (0, jnp.where(l == ll, i, 0), 0))
        return pl.BlockSpec((TR, C), lambda l, i: (jnp.where(l == ll, i, 0), 0))

    spec = pl.BlockSpec((1, TR, C), lambda l, i: (l, i, 0))
    return _call(
        body, name=name, grid=(L, R // TR),
        in_specs=[gspec(ll) for ll in range(L)] + [spec, spec, spec], out_specs=[spec] * 4,
        out_shape=[jax.ShapeDtypeStruct((L, R, C), F32)] * 4,
        compiler_params=_params("arbitrary", "arbitrary"),
    )(*g_layers, w, m, v)


def _ada_cols(c_all, w_ada, b_cols):
    L, D, NS = w_ada.shape

    def body(c_ref, w_ref, b_ref, o_ref):
        cond = _silu(c_ref[...]).astype(BF16)
        o_ref[0] = _dot(cond, w_ref[0].astype(BF16)) + b_ref[0]

    return _call(
        body, name="ada_cols", grid=(L,),
        in_specs=[pl.BlockSpec((NDEV, D), lambda l: (0, 0)), pl.BlockSpec((1, D, NS), lambda l: (l, 0, 0)),
                  pl.BlockSpec((1, 1, NS), lambda l: (l, 0, 0))],
        out_specs=pl.BlockSpec((1, NDEV, NS), lambda l: (l, 0, 0)),
        out_shape=jax.ShapeDtypeStruct((L, NDEV, NS), F32),
        compiler_params=_params("arbitrary"),
    )(c_all, w_ada, b_cols)


def _ada_grads(c_all, dada_all, dada_cols):
    _, D = c_all.shape
    L, _, NS = dada_cols.shape
    D3 = dada_all.shape[2]

    def body(c_ref, da_ref, dc_ref, gw_ref, gb_ref):
        cond = _silu(c_ref[...]).astype(BF16)
        gw_ref[0] = _dot_tn(cond, dc_ref[0].astype(BF16))
        acc = da_ref[0]
        for d in range(1, NDEV):
            acc = acc + da_ref[d]
        gb_ref[...] = acc

    return _call(
        body, name="ada_grads", grid=(L,),
        in_specs=[pl.BlockSpec((NDEV, D), lambda l: (0, 0)), pl.BlockSpec((NDEV, L, D3), lambda l: (0, 0, 0)),
                  pl.BlockSpec((1, NDEV, NS), lambda l: (l, 0, 0))],
        out_specs=[pl.BlockSpec((1, D, NS), lambda l: (l, 0, 0)), pl.BlockSpec((L, D3), lambda l: (0, 0))],
        out_shape=[jax.ShapeDtypeStruct((L, D, NS), F32), jax.ShapeDtypeStruct((L, D3), F32)],
        compiler_params=_params("arbitrary"),
    )(c_all, dada_all, dada_cols)


def _tile(S, want):
    return want if S % want == 0 else S


def _modulate(x, mod):
    S, D = x.shape
    TS = _tile(S, 512)

    def body(x_ref, mod_ref, h_ref):
        h_ref[...] = (x_ref[...] * mod_ref[0:1, :] + mod_ref[1:2, :]).astype(BF16)

    spec = pl.BlockSpec((TS, D), lambda i: (i, 0))
    return _call(
        body, name="modulate", grid=(S // TS,),
        in_specs=[spec, pl.BlockSpec((8, D), lambda i: (0, 0))], out_specs=spec,
        out_shape=jax.ShapeDtypeStruct((S, D), BF16),
        compiler_params=_params("arbitrary"),
    )(x, mod)


def _peer(k):
    ix, iy, ic = lax.axis_index("x"), lax.axis_index("y"), lax.axis_index("c")
    bx, by, bc = (k >> 2) & 1, (k >> 1) & 1, k & 1
    px, py, pc = ix + bx - 2 * ix * bx, iy + by - 2 * iy * by, ic + bc - 2 * ic * bc
    return (px, py, pc), 4 * px + 2 * py + pc


GATHER_ORDER = (0, 1, 2, 4, 3, 5, 6, 7)
SCATTER_ORDER = (2, 4, 6, 3, 5, 7, 1, 0)


def _offset(order, k):
    off = jnp.int32(order[-1])
    for step in reversed(range(len(order) - 1)):
        off = jnp.where(k == step, jnp.int32(order[step]), off)
    return off


def _proj_gather(h, w_shard, extras, me1):
    assert GATHER_ORDER[0] == 0
    S, D = h.shape
    NS = w_shard.shape[1]
    TS = _tile(S, 512)
    nt = S // TS
    ne = len(extras)

    def body(me_ref, h_ref, w_ref, *rest):
        x_refs, o_ref, wg_ref = rest[:ne], rest[ne], rest[ne + 1]
        xg_refs = rest[ne + 2:2 * ne + 2]
        wbuf, send_sems, recv_sems, local_sems, load_sems = rest[2 * ne + 2:]
        k, i = pl.program_id(0), pl.program_id(1)
        me = me_ref[0]
        off = _offset(GATHER_ORDER, k)

        def push(src, dst, which, kk):
            peer, _ = _peer(kk)
            return pltpu.make_async_remote_copy(
                src_ref=src, dst_ref=dst.at[me], send_sem=send_sems.at[which, kk], recv_sem=recv_sems.at[which, kk],
                device_id=peer, device_id_type=pl.DeviceIdType.MESH)

        def own(src, dst, which):
            return pltpu.make_async_copy(src, dst.at[me], local_sems.at[which])

        pairs = [(w_ref, wg_ref)] + list(zip(x_refs, xg_refs))

        @pl.when(jnp.logical_and(k == 0, i == 0))
        def _():
            for which, (src, dst) in enumerate(pairs):
                own(src, dst, which).start()
                for kk in GATHER_ORDER[1:]:
                    push(src, dst, which, kk).start()
            first = pltpu.make_async_copy(w_ref, wbuf.at[0], load_sems.at[0])
            first.start()
            first.wait()

        @pl.when(jnp.logical_and(k > 0, i == 0))
        def _():
            push(w_ref, wg_ref, 0, off).wait_recv()
            _, blk = _peer(off)
            load = pltpu.make_async_copy(wg_ref.at[blk], wbuf.at[k % 2], load_sems.at[k % 2])
            load.start()
            load.wait()

        o_ref[...] = _dot(h_ref[...], wbuf[k % 2])

        @pl.when(jnp.logical_and(k == NDEV - 1, i == nt - 1))
        def _():
            for which, (src, dst) in enumerate(pairs):
                for kk in range(1, NDEV):
                    cp = push(src, dst, which, kk)
                    cp.wait_send()
                    if which > 0:
                        cp.wait_recv()
                own(src, dst, which).wait()

    def col(k, i, me_ref):
        m, off = me_ref[0], _offset(GATHER_ORDER, k)
        return i, (m | off) - (m & off)

    anyspec = pl.BlockSpec(memory_space=pl.ANY)
    grid_spec = pltpu.PrefetchScalarGridSpec(
        num_scalar_prefetch=1, grid=(NDEV, nt),
        in_specs=[pl.BlockSpec((TS, D), lambda k, i, me_ref: (i, 0)), anyspec] + [anyspec] * ne,
        out_specs=[pl.BlockSpec((TS, NS), col), anyspec] + [anyspec] * ne,
        scratch_shapes=[pltpu.VMEM((2, D, NS), BF16), pltpu.SemaphoreType.DMA((1 + ne, NDEV)), pltpu.SemaphoreType.DMA((1 + ne, NDEV)),
                        pltpu.SemaphoreType.DMA((1 + ne,)), pltpu.SemaphoreType.DMA((2,))])
    return _call(
        body, name="proj_gather", grid_spec=grid_spec,
        out_shape=[jax.ShapeDtypeStruct((S, NDEV * NS), F32), jax.ShapeDtypeStruct((NDEV, D, NS), BF16)]
        + [jax.ShapeDtypeStruct((NDEV,) + e.shape, e.dtype) for e in extras],
        compiler_params=_params("arbitrary", "arbitrary"),
    )(me1, h, w_shard, *extras)


def _dw_scatter(a, b, blk, by_rows, me1, name):
    assert SCATTER_ORDER[-1] == 0
    S = a.shape[0]
    TK = _tile(S, 512)
    nk = S // TK
    shape = (blk, b.shape[1]) if by_rows else (a.shape[1], blk)
    last = NDEV - 1

    def body(me_ref, a_ref, b_ref, recv_ref, acc, stage, send_sems, recv_sems, local_sem):
        k, kk = pl.program_id(0), pl.program_id(1)
        me = me_ref[0]

        def push(off, slot):
            peer, _ = _peer(off)
            return pltpu.make_async_remote_copy(
                src_ref=stage.at[slot], dst_ref=recv_ref.at[me], send_sem=send_sems.at[off], recv_sem=recv_sems.at[off],
                device_id=peer, device_id_type=pl.DeviceIdType.MESH)

        own = pltpu.make_async_copy(stage.at[last % 2], recv_ref.at[me], local_sem)

        @pl.when(kk == 0)
        def _():
            acc[...] = jnp.zeros_like(acc)
        acc[...] += _dot_tn(a_ref[...], b_ref[...])

        @pl.when(kk == nk - 1)
        def _():
            @pl.when(k >= 2)
            def _():
                push(_offset(SCATTER_ORDER, k - 2), k % 2).wait_send()

            stage[k % 2] = acc[...].astype(BF16)

            @pl.when(k < last)
            def _():
                push(_offset(SCATTER_ORDER, k), k % 2).start()

            @pl.when(k == last)
            def _():
                own.start()
                push(SCATTER_ORDER[last - 1], (last - 1) % 2).wait_send()
                own.wait()
                for off in range(1, NDEV):
                    push(off, 0).wait_recv()

    def blk_of(k, me_ref):
        m, off = me_ref[0], _offset(SCATTER_ORDER, k)
        return (m | off) - (m & off)

    if by_rows:
        in_specs = [pl.BlockSpec((TK, blk), lambda k, kk, me_ref: (kk, blk_of(k, me_ref))),
                    pl.BlockSpec((TK, b.shape[1]), lambda k, kk, me_ref: (kk, 0))]
    else:
        in_specs = [pl.BlockSpec((TK, a.shape[1]), lambda k, kk, me_ref: (kk, 0)),
                    pl.BlockSpec((TK, blk), lambda k, kk, me_ref: (kk, blk_of(k, me_ref)))]
    grid_spec = pltpu.PrefetchScalarGridSpec(
        num_scalar_prefetch=1, grid=(NDEV, nk), in_specs=in_specs,
        out_specs=pl.BlockSpec(memory_space=pl.ANY),
        scratch_shapes=[pltpu.VMEM(shape, F32), pltpu.VMEM((2,) + shape, BF16), pltpu.SemaphoreType.DMA((NDEV,)),
                        pltpu.SemaphoreType.DMA((NDEV,)), pltpu.SemaphoreType.DMA])
    return _call(
        body, name=name, grid_spec=grid_spec,
        out_shape=jax.ShapeDtypeStruct((NDEV,) + shape, BF16),
        compiler_params=_params("arbitrary", "arbitrary"),
    )(me1, a, b)


def _chunk(TS, BR, k):
    return pl.BlockSpec((TS, BR), lambda i: (i, k))


def _halo_prev(TS, BR, k):
    return pl.BlockSpec((8, BR), lambda i: (jnp.maximum(i * (TS // 8) - 1, 0), k))


def _conv_a_fwd(proj, w8, BR):
    S = proj.shape[0]
    TS = _tile(S, 512)

    def body(ab_ref, ac_ref, ax_ref, ag_ref, hc_ref, hx_ref, w_ref, o_ref, ext):
        i = pl.program_id(0)
        u = ac_ref[...] * ax_ref[...]
        ext[0:8, :] = jnp.where(i > 0, hc_ref[...] * hx_ref[...], 0.0)
        ext[8:8 + TS, :] = u
        conv = ext[pl.ds(6, TS), :] * w_ref[0:1, :] + ext[pl.ds(7, TS), :] * w_ref[1:2, :] + u * w_ref[2:3, :]
        o_ref[...] = ab_ref[...] * conv * _silu(ag_ref[...])

    return _call(
        body, name="conv_a_fwd", grid=(S // TS,),
        in_specs=[_chunk(TS, BR, 0), _chunk(TS, BR, 1), _chunk(TS, BR, 2), _chunk(TS, BR, 3),
                  _halo_prev(TS, BR, 1), _halo_prev(TS, BR, 2), pl.BlockSpec((8, BR), lambda i: (0, 0))],
        out_specs=pl.BlockSpec((TS, BR), lambda i: (i, 0)),
        out_shape=jax.ShapeDtypeStruct((S, BR), F32),
        scratch_shapes=[pltpu.VMEM((TS + 8, BR), F32)],
        compiler_params=_params("arbitrary"),
    )(proj, proj, proj, proj, proj, proj, w8)


ATT_UNIT = 2048
ATT_LANES = 128


def _attn_mask():
    i = lax.broadcasted_iota(jnp.int32, (BLK, 2 * BLK), 0)
    j = lax.broadcasted_iota(jnp.int32, (BLK, 2 * BLK), 1)
    return jnp.logical_and(j >= i, j <= i + BLK), j >= BLK


def _attn_rows(n, r, dil):
    return pl.ds(n * BLK * dil + r, BLK, stride=dil) if dil > 1 else pl.ds(n * BLK, BLK)


def _attn_geometry(S, BR, dil):
    HD = BR // 8
    U = ATT_UNIT
    assert S % U == 0 and BR % ATT_LANES == 0 and ATT_LANES % HD == 0 and U % (BLK * dil) == 0
    return HD, U, S // U, U // (BLK * dil), ATT_LANES // HD, BR // ATT_LANES


def _attn_fwd(proj, bias, dil, BR):
    S = proj.shape[0]
    HD, U, NU, nbu, hpb, HBK = _attn_geometry(S, BR, dil)
    scale = HD ** -0.5
    combos = [(n, r) for n in range(nbu) for r in range(dil)]

    def body(q_ref, kc_ref, kp_ref, vc_ref, vp_ref, b_ref, o_ref, l_ref, sbuf, pbuf):
        m = pl.program_id(1)
        band, is_cur = _attn_mask()
        first = jnp.logical_and(band, jnp.logical_or(is_cur, m > 0))

        def keys(ref_c, ref_p, n, r):
            prev = ref_c[_attn_rows(n - 1, r, dil), :] if n > 0 else ref_p[_attn_rows(nbu - 1, r, dil), :]
            return jnp.concatenate([prev, ref_c[_attn_rows(n, r, dil), :]], axis=0).astype(BF16)

        for c, (n, r) in enumerate(combos):
            q = (q_ref[_attn_rows(n, r, dil), :] * scale).astype(BF16)
            kk = keys(kc_ref, kp_ref, n, r)
            for h in range(hpb):
                sl = slice(h * HD, (h + 1) * HD)
                sbuf[c * hpb + h] = _dot_nt(q[:, sl], kk[:, sl])
        for c, (n, r) in enumerate(combos):
            lses = []
            for h in range(hpb):
                s = jnp.where(band if n > 0 else first, sbuf[c * hpb + h] + b_ref[h], NEG)
                mx = jnp.max(s, axis=-1, keepdims=True)
                p = jnp.exp(s - mx)
                l = jnp.sum(p, axis=-1, keepdims=True)
                pbuf[c * hpb + h] = (p * (1.0 / l)).astype(BF16)
                lses.append(jnp.broadcast_to(mx + jnp.log(l), (BLK, HD)))
            l_ref[_attn_rows(n, r, dil), :] = jnp.concatenate(lses, axis=1)
        for c, (n, r) in enumerate(combos):
            vv = keys(vc_ref, vp_ref, n, r)
            o_ref[_attn_rows(n, r, dil), :] = jnp.concatenate(
                [_dot(pbuf[c * hpb + h], vv[:, h * HD:(h + 1) * HD]) for h in range(hpb)], axis=1)

    def cur(c):
        return pl.BlockSpec((U, ATT_LANES), lambda hb, m: (m, c * HBK + hb))

    def prev(c):
        return pl.BlockSpec((U, ATT_LANES), lambda hb, m: (jnp.maximum(m - 1, 0), c * HBK + hb))

    ospec = pl.BlockSpec((U, ATT_LANES), lambda hb, m: (m, hb))
    nhb = len(combos) * hpb
    return _call(
        body, name="attn_fwd_d%d" % dil, grid=(HBK, NU),
        in_specs=[cur(4), cur(5), prev(5), cur(6), prev(6), pl.BlockSpec((hpb, BLK, 2 * BLK), lambda hb, m: (hb, 0, 0))],
        out_specs=[ospec, ospec],
        out_shape=[jax.ShapeDtypeStruct((S, BR), F32)] * 2,
        scratch_shapes=[pltpu.VMEM((nhb, BLK, 2 * BLK), F32), pltpu.VMEM((nhb, BLK, 2 * BLK), BF16)],
        compiler_params=_params("arbitrary", "arbitrary"),
    )(proj, proj, proj, proj, proj, bias)


def _mix_weights(l1, l2, l3):
    m = jnp.maximum(jnp.maximum(l1, l2), l3)
    e1, e2, e3 = jnp.exp(l1 - m), jnp.exp(l2 - m), jnp.exp(l3 - m)
    inv = 1.0 / (e1 + e2 + e3)
    return e1 * inv, e2 * inv, e3 * inv


def _attn_mix_fwd(proj, os_, ls_, BR):
    S = proj.shape[0]
    TS = _tile(S, 512)

    def body(o1, o2, o3, l1, l2, l3, g_ref, y_ref):
        w1, w2, w3 = _mix_weights(l1[...], l2[...], l3[...])
        y_ref[...] = (w1 * o1[...] + w2 * o2[...] + w3 * o3[...]) * _silu(g_ref[...])

    spec = pl.BlockSpec((TS, BR), lambda i: (i, 0))
    return _call(
        body, name="attn_mix_fwd", grid=(S // TS,),
        in_specs=[spec] * 6 + [_chunk(TS, BR, 7)], out_specs=spec,
        out_shape=jax.ShapeDtypeStruct((S, BR), F32),
        compiler_params=_params("arbitrary"),
    )(*os_, *ls_, proj)


def _lru_gates(xc, wa, wx, ba, bx, sp):
    xb = xc.astype(BF16)
    r = _sigmoid(_dot(xb, wa) + ba)
    ig = _sigmoid(_dot(xb, wx) + bx)
    la = -LRU_C * r * sp
    a = jnp.exp(la)
    mult = jnp.sqrt(-_expm1(2.0 * la))
    return r, ig, a, mult


def _lru_fwd(proj, rows, wa, wx, BR):
    S = proj.shape[0]
    TS = _tile(S, 256)
    PAD = TS // 2

    def body(cx_ref, hx_ref, cg_ref, r_ref, wa_ref, wx_ref, y_ref, h_ref, ext, sa, sb, carry):
        i = pl.program_id(0)

        @pl.when(i == 0)
        def _():
            sa[0:PAD, :] = jnp.ones((PAD, BR), F32)
            sb[0:PAD, :] = jnp.zeros((PAD, BR), F32)
            carry[...] = jnp.zeros_like(carry)

        cx = cx_ref[...]
        ext[0:8, :] = jnp.where(i > 0, hx_ref[...], 0.0)
        ext[8:8 + TS, :] = cx
        xc = (ext[pl.ds(5, TS), :] * r_ref[0:1, :] + ext[pl.ds(6, TS), :] * r_ref[1:2, :]
              + ext[pl.ds(7, TS), :] * r_ref[2:3, :] + cx * r_ref[3:4, :] + r_ref[4:5, :])
        _, ig, a, mult = _lru_gates(xc, wa_ref[...], wx_ref[...], r_ref[5:6, :], r_ref[6:7, :], r_ref[7:8, :])
        sa[PAD:PAD + TS, :] = a
        sb[PAD:PAD + TS, :] = mult * ig * xc
        d = 1
        while d < TS:
            A, B = sa[PAD:PAD + TS, :], sb[PAD:PAD + TS, :]
            As, Bs = sa[pl.ds(PAD - d, TS), :], sb[pl.ds(PAD - d, TS), :]
            sb[PAD:PAD + TS, :] = A * Bs + B
            sa[PAD:PAD + TS, :] = A * As
            d *= 2
        h = sb[PAD:PAD + TS, :] + sa[PAD:PAD + TS, :] * carry[0:1, :]
        carry[0:1, :] = h[TS - 1:TS, :]
        h_ref[...] = h
        y_ref[...] = h * _silu(cg_ref[...])

    full = pl.BlockSpec((BR, BR), lambda i: (0, 0))
    spec = pl.BlockSpec((TS, BR), lambda i: (i, 0))
    return _call(
        body, name="lru_fwd", grid=(S // TS,),
        in_specs=[_chunk(TS, BR, 8), _halo_prev(TS, BR, 8), _chunk(TS, BR, 9), pl.BlockSpec((8, BR), lambda i: (0, 0)), full, full],
        out_specs=[spec, spec],
        out_shape=[jax.ShapeDtypeStruct((S, BR), F32)] * 2,
        scratch_shapes=[pltpu.VMEM((TS + 8, BR), F32), pltpu.VMEM((PAD + TS, BR), F32), pltpu.VMEM((PAD + TS, BR), F32),
                        pltpu.VMEM((8, BR), F32)],
        compiler_params=_params("arbitrary"),
    )(proj, proj, proj, rows, wa, wx)


S5_LAGS = 8
S5_CB = 8 * S5_CH
S5_SB = 8 * S5_STATE


def _s5_fwd(proj, a8, bbl, cre, cim, rows, wg, BR):
    S = proj.shape[0]
    W = a8.shape[1]
    TS = _tile(S, 256)
    CB, SB, J = S5_CB, S5_SB, S5_LAGS
    nblk = BR // CB

    def body(u_ref, uh_ref, dg_ref, a8_ref, bbl_ref, cre_ref, cim_ref, r_ref, wg_ref,
             y_ref, xre_ref, xim_ref, uext, ulag, sre, sim, ypre_s, carry):
        i = pl.program_id(0)

        @pl.when(i == 0)
        def _():
            carry[...] = jnp.zeros_like(carry)

        uext[0:8, :] = jnp.where(i > 0, uh_ref[...], 0.0)
        uext[8:8 + TS, :] = u_ref[...]
        for m in range(nblk):
            cs, ws = slice(m * CB, (m + 1) * CB), slice(m * SB, (m + 1) * SB)
            for j in range(J):
                ulag[:, j * CB:(j + 1) * CB] = uext[pl.ds(8 - j, TS), cs].astype(BF16)
            w = _dot(ulag[...], bbl_ref[m])
            sre[...] = w[:, 0:SB]
            sim[...] = w[:, SB:2 * SB]
            ar = jnp.broadcast_to(a8_ref[0:1, ws], (8, SB))
            ai = jnp.broadcast_to(a8_ref[1:2, ws], (8, SB))
            xr, xi = carry[0:8, ws], carry[8:16, ws]
            for g in range(TS // 8):
                rg = slice(8 * g, 8 * g + 8)
                xr, xi = sre[rg, :] + ar * xr - ai * xi, sim[rg, :] + ar * xi + ai * xr
                sre[rg, :] = xr
                sim[rg, :] = xi
            carry[0:8, ws] = xr
            carry[8:16, ws] = xi
            xre, xim = sre[...], sim[...]
            xre_ref[:, ws] = xre
            xim_ref[:, ws] = xim
            ypre_s[:, cs] = _dot(xre.astype(BF16), cre_ref[ws, :]) - _dot(xim.astype(BF16), cim_ref[ws, :])
        dg = dg_ref[...]
        yg = _gelu(ypre_s[...] + r_ref[0:1, :] * u_ref[...])
        s = _sigmoid(_dot(yg.astype(BF16), wg_ref[...]) + r_ref[1:2, :])
        y_ref[...] = yg * s * _silu(dg)

    def const(shape):
        return pl.BlockSpec(shape, lambda i: (0,) * len(shape))

    return _call(
        body, name="s5_fwd", grid=(S // TS,),
        in_specs=[_chunk(TS, BR, 10), _halo_prev(TS, BR, 10), _chunk(TS, BR, 11), const((8, W)), const((nblk, J * CB, 2 * SB)),
                  const((W, CB)), const((W, CB)), const((8, BR)), const((BR, BR))],
        out_specs=[pl.BlockSpec((TS, BR), lambda i: (i, 0)), pl.BlockSpec((TS, W), lambda i: (i, 0)), pl.BlockSpec((TS, W), lambda i: (i, 0))],
        out_shape=[jax.ShapeDtypeStruct((S, BR), F32), jax.ShapeDtypeStruct((S, W), F32), jax.ShapeDtypeStruct((S, W), F32)],
        scratch_shapes=[pltpu.VMEM((TS + 8, BR), F32), pltpu.VMEM((TS, J * CB), BF16), pltpu.VMEM((TS, SB), F32), pltpu.VMEM((TS, SB), F32),
                        pltpu.VMEM((TS, BR), F32), pltpu.VMEM((16, W), F32)],
        compiler_params=_params("arbitrary"),
    )(proj, proj, proj, a8, bbl, cre, cim, rows, wg)


def _out_ln(ys, w_out, x, rows, target=None):
    S, D = x.shape
    BR = D // 4
    TS = _tile(S, 256)
    last = target is not None

    def body(ya, yb, yc, yd, w_ref, x_ref, r_ref, *rest):
        t_ref = rest[0] if last else None
        o_ref, y_ref, cat_ref = rest[1:4] if last else rest[0:3]
        cat = jnp.concatenate([ya[...], yb[...], yc[...], yd[...]], axis=1).astype(BF16)
        y = _dot(cat, w_ref[...])
        z = ALPHA * x_ref[...] + r_ref[0:1, :] * y
        mu = jnp.mean(z, axis=-1, keepdims=True)
        zc = z - mu
        var = jnp.mean(zc * zc, axis=-1, keepdims=True)
        xn = zc * lax.rsqrt(var + LN_EPS) * r_ref[1:2, :] + r_ref[2:3, :]
        y_ref[...] = y
        cat_ref[...] = cat
        if last:
            l_ref = rest[4]

            @pl.when(pl.program_id(0) == 0)
            def _():
                l_ref[...] = jnp.zeros_like(l_ref)
            diff = xn - t_ref[...]
            o_ref[...] = diff * (1.0 / D)
            l_ref[...] += jnp.sum(_colsum(diff * diff), axis=1, keepdims=True) * (0.5 / D)
        else:
            o_ref[...] = xn

    yspec = pl.BlockSpec((TS, BR), lambda i: (i, 0))
    spec = pl.BlockSpec((TS, D), lambda i: (i, 0))
    lspec = pl.BlockSpec((1, 1), lambda i: (0, 0))
    return _call(
        body, name="out_ln_loss" if last else "out_ln", grid=(S // TS,),
        in_specs=[yspec] * 4 + [pl.BlockSpec((D, D), lambda i: (0, 0)), spec, pl.BlockSpec((8, D), lambda i: (0, 0))] + [spec] * last,
        out_specs=[spec, spec, spec] + [lspec] * last,
        out_shape=[jax.ShapeDtypeStruct((S, D), F32), jax.ShapeDtypeStruct((S, D), F32), jax.ShapeDtypeStruct((S, D), BF16)]
        + [jax.ShapeDtypeStruct((1, 1), F32)] * last,
        compiler_params=_params("arbitrary"),
    )(*ys, w_out, x, rows, *([target] if last else []))


def _ln_bwd(dout, x, y, rows, w_out):
    S, D = x.shape
    TS = _tile(S, 256)

    def body(do_ref, x_ref, y_ref, r_ref, w_ref, dxr_ref, dyb_ref, dcat_ref, acc_ref):
        @pl.when(pl.program_id(0) == 0)
        def _():
            acc_ref[...] = jnp.zeros_like(acc_ref)
        g1, lg = r_ref[0:1, :], r_ref[1:2, :]
        yv = y_ref[...]
        z = ALPHA * x_ref[...] + g1 * yv
        mu = jnp.mean(z, axis=-1, keepdims=True)
        zc = z - mu
        var = jnp.mean(zc * zc, axis=-1, keepdims=True)
        rstd = lax.rsqrt(var + LN_EPS)
        xhat = zc * rstd
        do = do_ref[...]
        dxh = do * lg
        dz = rstd * (dxh - jnp.mean(dxh, axis=-1, keepdims=True) - xhat * jnp.mean(dxh * xhat, axis=-1, keepdims=True))
        dxr_ref[...] = ALPHA * dz
        dyb = (g1 * dz).astype(BF16)
        dyb_ref[...] = dyb
        dcat_ref[...] = _dot_nt(dyb, w_ref[...])
        acc_ref[0:1, :] += _colsum(do * xhat)
        acc_ref[1:2, :] += _colsum(do)
        acc_ref[2:3, :] += _colsum(dz * yv)

    spec = pl.BlockSpec((TS, D), lambda i: (i, 0))
    rspec = pl.BlockSpec((8, D), lambda i: (0, 0))
    return _call(
        body, name="ln_bwd", grid=(S // TS,),
        in_specs=[spec, spec, spec, rspec, pl.BlockSpec((D, D), lambda i: (0, 0))],
        out_specs=[spec, spec, spec, rspec],
        out_shape=[jax.ShapeDtypeStruct((S, D), F32), jax.ShapeDtypeStruct((S, D), BF16), jax.ShapeDtypeStruct((S, D), F32),
                   jax.ShapeDtypeStruct((8, D), F32)],
        compiler_params=_params("arbitrary"),
    )(dout, x, y, rows, w_out)


def _dh(dproj, wg, dxres, x, mod):
    S, D = x.shape
    NS = wg.shape[2]
    TS = _tile(S, 256)

    def body(dp_ref, w_hbm, dxr_ref, x_ref, mod_ref, dx_ref, sum_ref, w_vmem, sem):
        @pl.when(pl.program_id(0) == 0)
        def _():
            load = pltpu.make_async_copy(w_hbm, w_vmem, sem)
            load.start()
            sum_ref[...] = jnp.zeros_like(sum_ref)
            load.wait()

        dh = _dot_nt(dp_ref[:, 0:NS], w_vmem[0])
        for j in range(1, NDEV):
            dh = dh + _dot_nt(dp_ref[:, j * NS:(j + 1) * NS], w_vmem[j])
        dx_ref[...] = dxr_ref[...] + dh * mod_ref[0:1, :]
        sum_ref[0:1, :] += _colsum(dh * x_ref[...])
        sum_ref[1:2, :] += _colsum(dh)

    spec = pl.BlockSpec((TS, D), lambda i: (i, 0))
    rspec = pl.BlockSpec((8, D), lambda i: (0, 0))
    return _call(
        body, name="dh", grid=(S // TS,),
        in_specs=[pl.BlockSpec((TS, NDEV * NS), lambda i: (i, 0)), pl.BlockSpec(memory_space=pl.ANY), spec, spec, rspec],
        out_specs=[spec, rspec],
        out_shape=[jax.ShapeDtypeStruct((S, D), F32), jax.ShapeDtypeStruct((8, D), F32)],
        scratch_shapes=[pltpu.VMEM((NDEV, D, NS), BF16), pltpu.SemaphoreType.DMA],
        compiler_params=_params("arbitrary"),
    )(dproj, wg, dxres, x, mod)


def _halo_next(S, TS, BR, k):
    last = S // 8 - 1
    return pl.BlockSpec((8, BR), lambda i: (jnp.minimum((i + 1) * (TS // 8), last), k))


def _conv_a_bwd(proj, dcat, w8, BR):
    S = proj.shape[0]
    TS = _tile(S, 512)
    nt = S // TS

    def body(ab_ref, ac_ref, ax_ref, ag_ref, dy_ref, hc_ref, hx_ref, nab_ref, nag_ref, ndy_ref, w_ref, dp_ref, dw_ref, ext, dext):
        i = pl.program_id(0)

        @pl.when(i == 0)
        def _():
            dw_ref[...] = jnp.zeros_like(dw_ref)

        w0, w1, w2 = w_ref[0:1, :], w_ref[1:2, :], w_ref[2:3, :]
        ab, ac, ax, ag, dy = ab_ref[...], ac_ref[...], ax_ref[...], ag_ref[...], dy_ref[...]
        u = ac * ax
        ext[0:8, :] = jnp.where(i > 0, hc_ref[...] * hx_ref[...], 0.0)
        ext[8:8 + TS, :] = u
        u1, u2 = ext[pl.ds(7, TS), :], ext[pl.ds(6, TS), :]
        conv = u2 * w0 + u1 * w1 + u * w2
        sg = _silu(ag)
        dconv = dy * ab * sg
        dext[0:TS, :] = dconv
        dext[TS:TS + 8, :] = jnp.where(i < nt - 1, ndy_ref[...] * nab_ref[...] * _silu(nag_ref[...]), 0.0)
        du = w2 * dconv + w1 * dext[pl.ds(1, TS), :] + w0 * dext[pl.ds(2, TS), :]
        dp_ref[:, 0:BR] = (dy * conv * sg).astype(BF16)
        dp_ref[:, BR:2 * BR] = (du * ax).astype(BF16)
        dp_ref[:, 2 * BR:3 * BR] = (du * ac).astype(BF16)
        dp_ref[:, 3 * BR:4 * BR] = (dy * ab * conv * _dsilu(ag)).astype(BF16)
        dw_ref[0:1, :] += _colsum(dconv * u2)
        dw_ref[1:2, :] += _colsum(dconv * u1)
        dw_ref[2:3, :] += _colsum(dconv * u)

    rspec = pl.BlockSpec((8, BR), lambda i: (0, 0))
    return _call(
        body, name="conv_a_bwd", grid=(nt,),
        in_specs=[_chunk(TS, BR, 0), _chunk(TS, BR, 1), _chunk(TS, BR, 2), _chunk(TS, BR, 3), _chunk(TS, BR, 0),
                  _halo_prev(TS, BR, 1), _halo_prev(TS, BR, 2), _halo_next(S, TS, BR, 0), _halo_next(S, TS, BR, 3),
                  _halo_next(S, TS, BR, 0), rspec],
        out_specs=[pl.BlockSpec((TS, 4 * BR), lambda i: (i, 0)), rspec],
        out_shape=[jax.ShapeDtypeStruct((S, 12 * BR), BF16), jax.ShapeDtypeStruct((8, BR), F32)],
        scratch_shapes=[pltpu.VMEM((TS + 8, BR), F32), pltpu.VMEM((TS + 8, BR), F32)],
        compiler_params=_params("arbitrary"),
    )(proj, proj, proj, proj, dcat, proj, proj, proj, proj, dcat, w8)


def _attn_mix_bwd(proj, dcat, os_, ls_, head_ones, BR):
    S = proj.shape[0]
    TS = _tile(S, 512)

    def body(dy_ref, g_ref, o1, o2, o3, l1, l2, l3, ones_ref, d1, d2, d3, e1, e2, e3, dg_ref):
        w1, w2, w3 = _mix_weights(l1[...], l2[...], l3[...])
        mix = w1 * o1[...] + w2 * o2[...] + w3 * o3[...]
        g = g_ref[...]
        dy = dy_ref[...]
        dmix = dy * _silu(g)
        dg_ref[...] = dy * mix * _dsilu(g)
        t = jnp.dot(dmix * mix, ones_ref[...], preferred_element_type=F32, precision=lax.Precision.HIGHEST)
        d1[...] = w1 * dmix
        d2[...] = w2 * dmix
        d3[...] = w3 * dmix
        e1[...] = -w1 * t
        e2[...] = -w2 * t
        e3[...] = -w3 * t

    spec = pl.BlockSpec((TS, BR), lambda i: (i, 0))
    outs = _call(
        body, name="attn_mix_bwd", grid=(S // TS,),
        in_specs=[_chunk(TS, BR, 1), _chunk(TS, BR, 7)] + [spec] * 6 + [pl.BlockSpec((BR, BR), lambda i: (0, 0))],
        out_specs=[spec] * 7,
        out_shape=[jax.ShapeDtypeStruct((S, BR), F32)] * 7,
        compiler_params=_params("arbitrary"),
    )(dcat, proj, *os_, *ls_, head_ones)
    return outs[0:3], outs[3:6], outs[6]


def _attn_bwd(proj, do, e, lse, bias, dil, BR):
    S = proj.shape[0]
    HD, U, NU, nbu, hpb, HBK = _attn_geometry(S, BR, dil)
    scale = HD ** -0.5
    combos = [(n, r) for n in range(nbu) for r in range(dil)]

    def body(q_ref, kc_ref, kp_ref, vc_ref, vp_ref, do_ref, e_ref, l_ref, b_ref, dq_ref, dk_ref, dv_ref, db_ref,
             sbuf, dpbuf, pbuf, dsbuf, dkacc, dvacc, nxk, nxv, cark, carv):
        j = pl.program_id(1)
        mu = NU - 1 - j
        band, is_cur = _attn_mask()
        first = jnp.logical_and(band, jnp.logical_or(is_cur, mu > 0))

        @pl.when(j == 0)
        def _():
            cark[...] = jnp.zeros_like(cark)
            carv[...] = jnp.zeros_like(carv)
            db_ref[...] = jnp.zeros_like(db_ref)

        for ref in (dkacc, dvacc, nxk, nxv):
            ref[...] = jnp.zeros_like(ref)

        def keys(ref_c, ref_p, n, r):
            prev = ref_c[_attn_rows(n - 1, r, dil), :] if n > 0 else ref_p[_attn_rows(nbu - 1, r, dil), :]
            return jnp.concatenate([prev, ref_c[_attn_rows(n, r, dil), :]], axis=0).astype(BF16)

        for c, (n, r) in enumerate(combos):
            rows = _attn_rows(n, r, dil)
            q = (q_ref[rows, :] * scale).astype(BF16)
            dob = do_ref[rows, :].astype(BF16)
            kk, vv = keys(kc_ref, kp_ref, n, r), keys(vc_ref, vp_ref, n, r)
            for h in range(hpb):
                sl = slice(h * HD, (h + 1) * HD)
                sbuf[c * hpb + h] = _dot_nt(q[:, sl], kk[:, sl])
                dpbuf[c * hpb + h] = _dot_nt(dob[:, sl], vv[:, sl])
        dbs = [None] * hpb
        for c, (n, r) in enumerate(combos):
            rows = _attn_rows(n, r, dil)
            lse, ev = l_ref[rows, :], e_ref[rows, :]
            for h in range(hpb):
                one = slice(h * HD, h * HD + 1)
                s = jnp.where(band if n > 0 else first, sbuf[c * hpb + h] + b_ref[h], NEG)
                p = jnp.exp(s - lse[:, one])
                ds = p * (dpbuf[c * hpb + h] + ev[:, one])
                pbuf[c * hpb + h] = p.astype(BF16)
                dsbuf[c * hpb + h] = ds.astype(BF16)
                dbs[h] = ds if dbs[h] is None else dbs[h] + ds
        for h in range(hpb):
            db_ref[h] += dbs[h]
        for c, (n, r) in enumerate(combos):
            rows = _attn_rows(n, r, dil)
            q = (q_ref[rows, :] * scale).astype(BF16)
            dob = do_ref[rows, :].astype(BF16)
            kk = keys(kc_ref, kp_ref, n, r)
            dq, dkk, dvv = [], [], []
            for h in range(hpb):
                sl = slice(h * HD, (h + 1) * HD)
                dsb = dsbuf[c * hpb + h]
                dq.append(_dot(dsb, kk[:, sl]) * scale)
                dkk.append(_dot_tn(dsb, q[:, sl]))
                dvv.append(_dot_tn(pbuf[c * hpb + h], dob[:, sl]))
            dq_ref[rows, :] = jnp.concatenate(dq, axis=1)
            dkk, dvv = jnp.concatenate(dkk, axis=1), jnp.concatenate(dvv, axis=1)
            dkacc[rows, :] += dkk[BLK:2 * BLK, :]
            dvacc[rows, :] += dvv[BLK:2 * BLK, :]
            if n > 0:
                prow = _attn_rows(n - 1, r, dil)
                dkacc[prow, :] += dkk[0:BLK, :]
                dvacc[prow, :] += dvv[0:BLK, :]
            else:
                prow = _attn_rows(nbu - 1, r, dil)
                nxk[prow, :] += dkk[0:BLK, :]
                nxv[prow, :] += dvv[0:BLK, :]
        dk_ref[...] = dkacc[...] + cark[...]
        dv_ref[...] = dvacc[...] + carv[...]
        cark[...] = nxk[...]
        carv[...] = nxv[...]

    def cur(c):
        return pl.BlockSpec((U, ATT_LANES), lambda hb, j: (NU - 1 - j, c * HBK + hb))

    def prev(c):
        return pl.BlockSpec((U, ATT_LANES), lambda hb, j: (jnp.maximum(NU - 2 - j, 0), c * HBK + hb))

    own = pl.BlockSpec((U, ATT_LANES), lambda hb, j: (NU - 1 - j, hb))
    bspec = pl.BlockSpec((hpb, BLK, 2 * BLK), lambda hb, j: (hb, 0, 0))
    nhb = len(combos) * hpb
    unit = pltpu.VMEM((U, ATT_LANES), F32)
    return _call(
        body, name="attn_bwd_d%d" % dil, grid=(HBK, NU),
        in_specs=[cur(4), cur(5), prev(5), cur(6), prev(6), own, own, own, bspec],
        out_specs=[own, own, own, bspec],
        out_shape=[jax.ShapeDtypeStruct((S, BR), F32)] * 3 + [jax.ShapeDtypeStruct((8, BLK, 2 * BLK), F32)],
        scratch_shapes=[pltpu.VMEM((nhb, BLK, 2 * BLK), F32)] * 2 + [pltpu.VMEM((nhb, BLK, 2 * BLK), BF16)] * 2 + [unit] * 6,
        compiler_params=_params("arbitrary", "arbitrary"),
    )(proj, proj, proj, proj, proj, do, e, lse, bias)


def _attn_dsum(ds, dbg, dproj, BR):
    S = dbg.shape[0]
    TS = _tile(S, 512)

    def body(*refs):
        o_ref = refs[11]
        for k in range(3):
            o_ref[:, k * BR:(k + 1) * BR] = (refs[k][...] + refs[3 + k][...] + refs[6 + k][...]).astype(BF16)
        o_ref[:, 3 * BR:4 * BR] = refs[9][...].astype(BF16)

    spec = pl.BlockSpec((TS, BR), lambda i: (i, 0))
    return _call(
        body, name="attn_dsum", grid=(S // TS,),
        in_specs=[spec] * 10 + [pl.BlockSpec(memory_space=pl.ANY)],
        out_specs=pl.BlockSpec((TS, 4 * BR), lambda i: (i, 1)),
        out_shape=jax.ShapeDtypeStruct(dproj.shape, BF16),
        input_output_aliases={10: 0},
        compiler_params=_params("arbitrary"),
    )(*[t for trip in ds for t in trip], dbg, dproj)


def _lru_bwd(proj, h, dcat, rows, wa, wx, dproj, BR):
    S = proj.shape[0]
    TS = _tile(S, 256)
    PAD = TS // 2
    nt = S // TS

    def body(cx_ref, hx_ref, cg_ref, h_ref, hh_ref, dy_ref, r_ref, wa_ref, wx_ref, _,
             dp_ref, sum_ref, dwa_ref, dwx_ref, ext, hext, aext, dext, sa, sb, carry):
        i = pl.program_id(0)
        ti = nt - 1 - i

        @pl.when(i == 0)
        def _():
            sa[TS:TS + PAD, :] = jnp.ones((PAD, BR), F32)
            sb[TS:TS + PAD, :] = jnp.zeros((PAD, BR), F32)
            carry[...] = jnp.zeros_like(carry)
            dext[TS:TS + 8, :] = jnp.zeros((8, BR), F32)
            sum_ref[...] = jnp.zeros_like(sum_ref)
            dwa_ref[...] = jnp.zeros_like(dwa_ref)
            dwx_ref[...] = jnp.zeros_like(dwx_ref)

        w0, w1, w2, w3 = r_ref[0:1, :], r_ref[1:2, :], r_ref[2:3, :], r_ref[3:4, :]
        sp = r_ref[7:8, :]
        cx = cx_ref[...]
        ext[0:8, :] = jnp.where(ti > 0, hx_ref[...], 0.0)
        ext[8:8 + TS, :] = cx
        x3, x2, x1 = ext[pl.ds(5, TS), :], ext[pl.ds(6, TS), :], ext[pl.ds(7, TS), :]
        xc = x3 * w0 + x2 * w1 + x1 * w2 + cx * w3 + r_ref[4:5, :]
        wa_, wx_ = wa_ref[...], wx_ref[...]
        r, ig, a, mult = _lru_gates(xc, wa_, wx_, r_ref[5:6, :], r_ref[6:7, :], sp)
        cg, dy, hv = cg_ref[...], dy_ref[...], h_ref[...]
        dp_ref[:, BR:2 * BR] = (dy * hv * _dsilu(cg)).astype(BF16)
        aext[0:TS, :] = a
        aext[TS:TS + 8, :] = jnp.broadcast_to(carry[0:1, :], (8, BR))
        sa[0:TS, :] = aext[pl.ds(1, TS), :]
        sb[0:TS, :] = dy * _silu(cg)
        d = 1
        while d < TS:
            A, B = sa[0:TS, :], sb[0:TS, :]
            As, Bs = sa[pl.ds(d, TS), :], sb[pl.ds(d, TS), :]
            sb[0:TS, :] = B + A * Bs
            sa[0:TS, :] = A * As
            d *= 2
        gh = sb[0:TS, :] + sa[0:TS, :] * carry[1:2, :]
        carry[0:1, :] = a[0:1, :]
        carry[1:2, :] = gh[0:1, :]
        hext[0:8, :] = jnp.where(ti > 0, hh_ref[...], 0.0)
        hext[8:8 + TS, :] = hv
        da = gh * hext[pl.ds(7, TS), :]
        dmult = gh * ig * xc
        dig = gh * mult * xc
        dxc = gh * mult * ig
        dla = da * a - dmult * a * a / mult
        dpr = dla * (-LRU_C) * sp * r * (1.0 - r)
        dpi = dig * ig * (1.0 - ig)
        xb, dprb, dpib = xc.astype(BF16), dpr.astype(BF16), dpi.astype(BF16)
        dwa_ref[...] += _dot_tn(xb, dprb)
        dwx_ref[...] += _dot_tn(xb, dpib)
        dxc = dxc + _dot_nt(dprb, wa_) + _dot_nt(dpib, wx_)
        sum_ref[0:1, :] += _colsum(dxc * x3)
        sum_ref[1:2, :] += _colsum(dxc * x2)
        sum_ref[2:3, :] += _colsum(dxc * x1)
        sum_ref[3:4, :] += _colsum(dxc * cx)
        sum_ref[4:5, :] += _colsum(dxc)
        sum_ref[5:6, :] += _colsum(dpr)
        sum_ref[6:7, :] += _colsum(dpi)
        sum_ref[7:8, :] += _colsum(dla * (-LRU_C) * r)
        dext[0:TS, :] = dxc
        dcx = w3 * dxc + w2 * dext[pl.ds(1, TS), :] + w1 * dext[pl.ds(2, TS), :] + w0 * dext[pl.ds(3, TS), :]
        dext[TS:TS + 8, :] = dxc[0:8, :]
        dp_ref[:, 0:BR] = dcx.astype(BF16)

    def rev(k):
        return pl.BlockSpec((TS, BR), lambda i: (nt - 1 - i, k))

    def rev_halo(k):
        return pl.BlockSpec((8, BR), lambda i: (jnp.maximum((nt - 1 - i) * (TS // 8) - 1, 0), k))

    full = pl.BlockSpec((BR, BR), lambda i: (0, 0))
    rspec = pl.BlockSpec((8, BR), lambda i: (0, 0))
    return _call(
        body, name="lru_bwd", grid=(nt,),
        in_specs=[rev(8), rev_halo(8), rev(9), rev(0), rev_halo(0), rev(2), rspec, full, full, pl.BlockSpec(memory_space=pl.ANY)],
        out_specs=[pl.BlockSpec((TS, 2 * BR), lambda i: (nt - 1 - i, 4)), rspec, full, full],
        input_output_aliases={9: 0},
        out_shape=[jax.ShapeDtypeStruct(dproj.shape, BF16), jax.ShapeDtypeStruct((8, BR), F32),
                   jax.ShapeDtypeStruct((BR, BR), F32), jax.ShapeDtypeStruct((BR, BR), F32)],
        scratch_shapes=[pltpu.VMEM((TS + 8, BR), F32)] * 4 + [pltpu.VMEM((TS + PAD, BR), F32)] * 2 + [pltpu.VMEM((8, BR), F32)],
        compiler_params=_params("arbitrary"),
    )(proj, proj, proj, h, h, dcat, rows, wa, wx, dproj)


def _s5_bwd(proj, dcat, xre_all, xim_all, a8, cl, bbre, bbim, cre, cim, rows, wg, dproj, BR):
    S = proj.shape[0]
    W = a8.shape[1]
    TS = _tile(S, 256)
    nt = S // TS
    CB, SB, J = S5_CB, S5_SB, S5_LAGS
    nblk = BR // CB

    def body(u_ref, dg_ref, dy_ref, xre_ref, xim_ref, hre_ref, him_ref, a8_ref, cl_ref, bbre_ref, bbim_ref, cre_ref, cim_ref,
             r_ref, wg_ref, _, dp_ref, sum_ref, dwg_ref, da_ref, dbbre_ref, dbbim_ref, dcre_ref, dcim_ref,
             sre, sim, ere, eim, ypre_s, dext, dylag, carry):
        i = pl.program_id(0)
        ti = nt - 1 - i

        @pl.when(i == 0)
        def _():
            dext[TS:TS + 8, :] = jnp.zeros((8, BR), F32)
            carry[...] = jnp.zeros_like(carry)
            for ref in (sum_ref, dwg_ref, da_ref, dbbre_ref, dbbim_ref, dcre_ref, dcim_ref):
                ref[...] = jnp.zeros_like(ref)

        u, dg, dy = u_ref[...], dg_ref[...], dy_ref[...]
        for m in range(nblk):
            cs, ws = slice(m * CB, (m + 1) * CB), slice(m * SB, (m + 1) * SB)
            ypre_s[:, cs] = (_dot(xre_ref[:, ws].astype(BF16), cre_ref[ws, :]) - _dot(xim_ref[:, ws].astype(BF16), cim_ref[ws, :]))
        ypre = ypre_s[...] + r_ref[0:1, :] * u
        yg = _gelu(ypre)
        wg_ = wg_ref[...]
        s = _sigmoid(_dot(yg.astype(BF16), wg_) + r_ref[1:2, :])
        dgl = dy * _silu(dg)
        dp_ref[:, BR:2 * BR] = (dy * yg * s * _dsilu(dg)).astype(BF16)
        dps = dgl * yg * s * (1.0 - s)
        dpsb = dps.astype(BF16)
        sum_ref[1:2, :] += _colsum(dps)
        dwg_ref[...] += _dot_tn(yg.astype(BF16), dpsb)
        dyp = (dgl * s + _dot_nt(dpsb, wg_)) * _dgelu(ypre)
        sum_ref[0:1, :] += _colsum(dyp * u)
        dext[0:TS, :] = dyp
        for m in range(nblk):
            cs, ws = slice(m * CB, (m + 1) * CB), slice(m * SB, (m + 1) * SB)
            dypb = dext[0:TS, cs].astype(BF16)
            xre, xim = xre_ref[:, ws], xim_ref[:, ws]
            dcre_ref[ws, :] += _dot_tn(xre.astype(BF16), dypb)
            dcim_ref[ws, :] -= _dot_tn(xim.astype(BF16), dypb)
            for j in range(J):
                dylag[:, j * CB:(j + 1) * CB] = dext[pl.ds(j, TS), cs].astype(BF16)
            w = _dot(dylag[...], cl_ref[m])
            sre[...] = w[:, 0:SB]
            sim[...] = w[:, SB:2 * SB]
            ar = jnp.broadcast_to(a8_ref[0:1, ws], (8, SB))
            ai = jnp.broadcast_to(a8_ref[1:2, ws], (8, SB))
            gr, gi = carry[0:8, ws], carry[8:16, ws]
            for g in reversed(range(TS // 8)):
                rg = slice(8 * g, 8 * g + 8)
                gr, gi = sre[rg, :] + ar * gr + ai * gi, sim[rg, :] + ar * gi - ai * gr
                sre[rg, :] = gr
                sim[rg, :] = gi
            carry[0:8, ws] = gr
            carry[8:16, ws] = gi
            gre, gim = sre[...], sim[...]
            ere[0:8, :] = jnp.where(ti > 0, hre_ref[:, ws], 0.0)
            eim[0:8, :] = jnp.where(ti > 0, him_ref[:, ws], 0.0)
            ere[8:8 + TS, :] = xre
            eim[8:8 + TS, :] = xim
            xpr, xpi = ere[pl.ds(7, TS), :], eim[pl.ds(7, TS), :]
            da_ref[0:1, ws] += _colsum(gre * xpr + gim * xpi)
            da_ref[1:2, ws] += _colsum(gim * xpr - gre * xpi)
            greb, gimb = gre.astype(BF16), gim.astype(BF16)
            ub = u_ref[:, cs].astype(BF16)
            dbbre_ref[cs, :] += _dot_tn(ub, greb)
            dbbim_ref[cs, :] += _dot_tn(ub, gimb)
            du = _dot_nt(greb, bbre_ref[cs, :]) + _dot_nt(gimb, bbim_ref[cs, :]) + r_ref[0:1, cs] * dext[0:TS, cs]
            dp_ref[:, cs] = du.astype(BF16)
        dext[TS:TS + 8, :] = dext[0:8, :]

    def rev(width, k):
        return pl.BlockSpec((TS, width), lambda i: (nt - 1 - i, k))

    def rev_halo(width):
        return pl.BlockSpec((8, width), lambda i: (jnp.maximum((nt - 1 - i) * (TS // 8) - 1, 0), 0))

    def const(shape):
        return pl.BlockSpec(shape, lambda i: (0,) * len(shape))

    return _call(
        body, name="s5_bwd", grid=(nt,),
        in_specs=[rev(BR, 10), rev(BR, 11), rev(BR, 3), rev(W, 0), rev(W, 0), rev_halo(W), rev_halo(W),
                  const((8, W)), const((nblk, J * CB, 2 * SB)), const((BR, SB)), const((BR, SB)), const((W, CB)), const((W, CB)),
                  const((8, BR)), const((BR, BR)), pl.BlockSpec(memory_space=pl.ANY)],
        input_output_aliases={15: 0},
        out_specs=[pl.BlockSpec((TS, 2 * BR), lambda i: (nt - 1 - i, 5)), const((8, BR)), const((BR, BR)), const((8, W)),
                   const((BR, SB)), const((BR, SB)), const((W, CB)), const((W, CB))],
        out_shape=[jax.ShapeDtypeStruct(dproj.shape, BF16), jax.ShapeDtypeStruct((8, BR), F32), jax.ShapeDtypeStruct((BR, BR), F32),
                   jax.ShapeDtypeStruct((8, W), F32), jax.ShapeDtypeStruct((BR, SB), F32), jax.ShapeDtypeStruct((BR, SB), F32),
                   jax.ShapeDtypeStruct((W, CB), F32), jax.ShapeDtypeStruct((W, CB), F32)],
        scratch_shapes=[pltpu.VMEM((TS, SB), F32)] * 2 + [pltpu.VMEM((TS + 8, SB), F32)] * 2
        + [pltpu.VMEM((TS, BR), F32), pltpu.VMEM((TS + 8, BR), F32), pltpu.VMEM((TS, J * CB), BF16), pltpu.VMEM((16, W), F32)],
        compiler_params=_params("arbitrary"),
    )(proj, proj, dcat, xre_all, xim_all, xre_all, xim_all, a8, cl, bbre, bbim, cre, cim, rows, wg, dproj)


_WEIGHTS = ['rel_bias', 'w_ada', 'b_ada', 'w_in', 'conv_a', 'conv_c', 'conv_c_b', 'lru_wa', 'lru_ba', 'lru_wx', 'lru_bx',
            'lru_lambda', 's5_lam_re', 's5_lam_im', 's5_log_dt', 's5_b_re', 's5_b_im', 's5_c_re', 's5_c_im', 's5_d',
            's5_w_glu', 's5_b_glu', 'w_out', 'ln_g', 'ln_b']
_LAYER_SMALL = ['conv_a', 'conv_c', 'conv_c_b', 'lru_wa', 'lru_ba', 'lru_wx', 'lru_bx', 'lru_lambda', 's5_lam_re', 's5_lam_im',
                's5_log_dt', 's5_b_re', 's5_b_im', 's5_c_re', 's5_c_im', 's5_d', 's5_b_glu']
_SMALL = ['rel_bias'] + _LAYER_SMALL + ['ln_g', 'ln_b']


def _t5_bucket(dist):
    max_exact = REL_BUCKETS // 2
    nf = jnp.maximum(dist, 1).astype(F32)
    large = max_exact + (jnp.log(nf / max_exact) / math.log(REL_MAX_DIST / max_exact)
                         * (REL_BUCKETS - max_exact)).astype(jnp.int32)
    large = jnp.minimum(large, REL_BUCKETS - 1)
    return jnp.where(dist < max_exact, dist, large)


def _bias_tables(rel_bias):
    i = jnp.arange(BLK)[:, None]
    j = jnp.arange(2 * BLK)[None, :]
    delta = i + BLK - j
    out = []
    for window, dil in DILATIONS:
        bucket = _t5_bucket(jnp.clip(delta, 0, window // dil) * dil)
        onehot = (bucket[:, :, None] == jnp.arange(REL_BUCKETS)[None, None, :]).astype(F32)
        out.append(jnp.einsum('ijb,bh->hij', onehot, rel_bias, precision=lax.Precision.HIGHEST))
    return jnp.stack(out)


def _prep_layer(q):
    eye8 = jnp.eye(8, dtype=F32)
    G = q['s5_lam_re'].shape[0]
    nblk = G // 8

    def block_diag(w):
        hd = w.shape[1]
        return (w[:, :, None, :] * eye8[:, None, :, None]).reshape(8 * hd, 8 * hd)

    def compact_b(bb):
        t = jnp.transpose(bb.reshape(nblk, 8, S5_STATE, S5_CH), (0, 1, 3, 2))
        return (t[:, :, :, None, :] * eye8[None, :, None, :, None]).reshape(G * S5_CH, 8 * S5_STATE)

    def compact_c(cc):
        t = jnp.transpose(cc.reshape(nblk, 8, S5_CH, S5_STATE), (0, 1, 3, 2))
        return (t[:, :, :, None, :] * eye8[None, :, None, :, None]).reshape(G * S5_STATE, 8 * S5_CH)

    lam_re, lam_im = q['s5_lam_re'], q['s5_lam_im']
    dt = jnp.exp(q['s5_log_dt'])[:, None]
    mag = jnp.exp(lam_re * dt)
    ab_re = mag * jnp.cos(lam_im * dt)
    ab_im = mag * jnp.sin(lam_im * dt)
    den = lam_re * lam_re + lam_im * lam_im
    f_re = ((ab_re - 1.0) * lam_re + ab_im * lam_im) / den
    f_im = (ab_im * lam_re - (ab_re - 1.0) * lam_im) / den
    b_re, b_im = q['s5_b_re'], q['s5_b_im']
    bb_re = f_re[..., None] * b_re - f_im[..., None] * b_im
    bb_im = f_re[..., None] * b_im + f_im[..., None] * b_re
    return dict(
        conv_a=q['conv_a'], conv_c=q['conv_c'], conv_c_b=q['conv_c_b'], lru_ba=q['lru_ba'], lru_bx=q['lru_bx'],
        sp=jax.nn.softplus(-q['lru_lambda']), wa=block_diag(q['lru_wa']), wx=block_diag(q['lru_wx']),
        ar=ab_re.reshape(-1), ai=ab_im.reshape(-1), bbre=compact_b(bb_re), bbim=compact_b(bb_im),
        cre=compact_c(q['s5_c_re']), cim=compact_c(q['s5_c_im']), s5_d=q['s5_d'], s5_b_glu=q['s5_b_glu'])


def _s5_lag_weights(kp):
    J, CB, SB = S5_LAGS, S5_CB, S5_SB
    ar, ai = kp['ar'], kp['ai']
    W = ar.shape[0]
    nblk = W // SB
    cr, ci = jnp.ones_like(ar), jnp.zeros_like(ai)
    pows = []
    for _ in range(J + 1):
        pows.append((cr, ci))
        cr, ci = cr * ar - ci * ai, cr * ai + ci * ar
    pwr = jnp.transpose(jnp.stack([p[0] for p in pows[:J]]).reshape(J, nblk, SB), (1, 0, 2))[:, :, None, :]
    pwi = jnp.transpose(jnp.stack([p[1] for p in pows[:J]]).reshape(J, nblk, SB), (1, 0, 2))[:, :, None, :]

    def lagged(re, im, sign):
        out = jnp.concatenate([re * pwr - sign * im * pwi, sign * re * pwi + im * pwr], axis=-1)
        return out.reshape(nblk, J * CB, 2 * SB).astype(BF16)

    bbl = lagged(kp['bbre'].reshape(nblk, 1, CB, SB), kp['bbim'].reshape(nblk, 1, CB, SB), 1.0)
    c0r = jnp.transpose(kp['cre'].reshape(nblk, SB, CB), (0, 2, 1))[:, None]
    c0i = -jnp.transpose(kp['cim'].reshape(nblk, SB, CB), (0, 2, 1))[:, None]
    cl = lagged(c0r, c0i, -1.0)
    return _rows8([pows[J][0], pows[J][1]], W), bbl, cl


def _rows8(vecs, width):
    rows = [v.reshape(1, width).astype(F32) for v in vecs]
    return jnp.concatenate(rows + [jnp.zeros((8 - len(rows), width), F32)], axis=0)


def kernel(x, c, rel_bias, w_ada, b_ada, w_in, conv_a, conv_c, conv_c_b, lru_wa, lru_ba, lru_wx, lru_bx, lru_lambda, s5_lam_re, s5_lam_im, s5_log_dt, s5_b_re, s5_b_im, s5_c_re, s5_c_im, s5_d, s5_w_glu, s5_b_glu, w_out, ln_g, ln_b, loss_target, m_rel_bias, m_w_ada, m_b_ada, m_w_in, m_conv_a, m_conv_c, m_conv_c_b, m_lru_wa, m_lru_ba, m_lru_wx, m_lru_bx, m_lru_lambda, m_s5_lam_re, m_s5_lam_im, m_s5_log_dt, m_s5_b_re, m_s5_b_im, m_s5_c_re, m_s5_c_im, m_s5_d, m_s5_w_glu, m_s5_b_glu, m_w_out, m_ln_g, m_ln_b, v_rel_bias, v_w_ada, v_b_ada, v_w_in, v_conv_a, v_conv_c, v_conv_c_b, v_lru_wa, v_lru_ba, v_lru_wx, v_lru_bx, v_lru_lambda, v_s5_lam_re, v_s5_lam_im, v_s5_log_dt, v_s5_b_re, v_s5_b_im, v_s5_c_re, v_s5_c_im, v_s5_d, v_s5_w_glu, v_s5_b_glu, v_w_out, v_ln_g, v_ln_b):
    a = dict(locals())
    me = 4 * lax.axis_index("x") + 2 * lax.axis_index("y") + lax.axis_index("c")
    x0, target = x[0], loss_target[0]
    S, D = x0.shape
    BR = D // 4
    NS = w_in.shape[2]
    SH = BR // NDEV
    assert S % (BLK * DILATIONS[-1][1]) == 0 and BR % (8 * S5_CH) == 0

    small = _exchange(_pack([c, conv_a, conv_c]), True, "gather_small")
    per_dev = [_unpack(small[d], [c.shape, conv_a.shape, conv_c.shape]) for d in range(NDEV)]
    c_all = jnp.concatenate([p[0] for p in per_dev], axis=0)
    conv_a_full = jnp.concatenate([p[1] for p in per_dev], axis=2)
    conv_c_full = jnp.concatenate([p[2] for p in per_dev], axis=2)
    me1 = me.reshape(1).astype(jnp.int32)
    w_in_bf, w_out_bf, w_glu_bf = w_in.astype(BF16), w_out.astype(BF16), s5_w_glu.astype(BF16)

    b_cols = lax.dynamic_slice(b_ada, (0, me * NS), (DEPTH, NS)).reshape(DEPTH, 1, NS)
    ada_all = _exchange(_ada_cols(c_all, w_ada, b_cols), True, "gather_ada")
    ada_me = lax.dynamic_index_in_dim(ada_all, me, axis=2, keepdims=False)
    ada_me = jnp.transpose(ada_me, (1, 0, 2)).reshape(DEPTH, 3 * D)
    shift, scale, gate = ada_me[:, :D], ada_me[:, D:2 * D], ada_me[:, 2 * D:]

    bias_tabs, bias_pull = jax.vjp(_bias_tables, rel_bias)
    HD = BR // 8
    head_ones = jnp.kron(jnp.eye(8, dtype=F32), jnp.ones((HD, HD), F32))

    saved = []
    xl = x0
    for l in range(DEPTH):
        q = {n: a[n][l] for n in _LAYER_SMALL}
        q['conv_a'], q['conv_c'] = conv_a_full[l], conv_c_full[l]
        kp, pull = jax.vjp(_prep_layer, q)
        a8, bbl, cl = _s5_lag_weights(kp)
        mod = _rows8([1.0 + scale[l], shift[l]], D)
        arows = _rows8(list(kp['conv_a']), BR)
        lrows = _rows8(list(kp['conv_c']) + [kp['conv_c_b'], kp['lru_ba'], kp['lru_bx'], kp['sp']], BR)
        srows = _rows8([kp['s5_d'], kp['s5_b_glu']], BR)
        orows = _rows8([1.0 + gate[l], ln_g[l], ln_b[l]], D)
        wa, wx = kp['wa'].astype(BF16), kp['wx'].astype(BF16)
        s5w = [kp[n].astype(BF16) for n in ('bbre', 'bbim', 'cre', 'cim')]

        hbf = _modulate(xl, mod)
        proj, w_in_l, w_out_l, w_glu_l = _proj_gather(hbf, w_in_bf[l], [w_out_bf[l], w_glu_bf[l]], me1)
        w_out_l, w_glu_l = w_out_l.reshape(D, D), w_glu_l.reshape(BR, BR)
        ya = _conv_a_fwd(proj, arows, BR)
        os_, ls_ = [], []
        for g, (_, dil) in enumerate(DILATIONS):
            o, lse = _attn_fwd(proj, bias_tabs[g], dil, BR)
            os_.append(o)
            ls_.append(lse)
        yb = _attn_mix_fwd(proj, os_, ls_, BR)
        yc, hs = _lru_fwd(proj, lrows, wa, wx, BR)
        yd, xre, xim = _s5_fwd(proj, a8, bbl, s5w[2], s5w[3], srows, w_glu_l, BR)
        if l < DEPTH - 1:
            xn, y, cat = _out_ln([ya, yb, yc, yd], w_out_l, xl, orows)
        else:
            xn, y, cat, loss_local = _out_ln([ya, yb, yc, yd], w_out_l, xl, orows, target)
        saved.append(dict(x=xl, proj=proj, hbf=hbf, os=os_, ls=ls_, hs=hs, xre=xre, xim=xim, y=y, cat=cat, pull=pull,
                          mod=mod, arows=arows, lrows=lrows, srows=srows, orows=orows, wa=wa, wx=wx, s5w=s5w,
                          qw=(a8, cl), w_in=w_in_l, w_out=w_out_l, w_glu=w_glu_l))
        xl = xn

    dout = xl

    dbias = jnp.zeros_like(bias_tabs)
    lgrads, dada, dw_in_parts, dw_out_parts, dw_glu_parts = [None] * DEPTH, [None] * DEPTH, [None] * DEPTH, [None] * DEPTH, [None] * DEPTH
    for l in reversed(range(DEPTH)):
        sv = saved[l]
        proj = sv['proj']
        dxres, dyb, dcat, lnsum = _ln_bwd(dout, sv['x'], sv['y'], sv['orows'], sv['w_out'])
        dw_out_parts[l] = _dw_scatter(sv['cat'], dyb, D // NDEV, True, me1, "dw_out_scatter")
        dproj, asum = _conv_a_bwd(proj, dcat, sv['arows'], BR)
        dos, es, dbg = _attn_mix_bwd(proj, dcat, sv['os'], sv['ls'], head_ones, BR)
        dqkv, dbs = [], []
        for g, (_, dil) in enumerate(DILATIONS):
            dq_, dk_, dv_, db_ = _attn_bwd(proj, dos[g], es[g], sv['ls'][g], bias_tabs[g], dil, BR)
            dqkv.append((dq_, dk_, dv_))
            dbs.append(db_)
        dbias = dbias + jnp.stack(dbs)
        dproj = _attn_dsum(dqkv, dbg, dproj, BR)
        dproj, lsum, dwa, dwx = _lru_bwd(proj, sv['hs'], dcat, sv['lrows'], sv['wa'], sv['wx'], dproj, BR)
        dproj, ssum, dwg, da_, dbbre, dbbim, dcre, dcim = _s5_bwd(
            proj, dcat, sv['xre'], sv['xim'], *sv['qw'], *sv['s5w'], sv['srows'], sv['w_glu'], dproj, BR)
        dw_glu_parts[l] = dwg.reshape(NDEV, BR // NDEV, BR)
        dout, msum = _dh(dproj, sv['w_in'], dxres, sv['x'], sv['mod'])
        dw_in_parts[l] = _dw_scatter(sv['hbf'], dproj, NS, False, me1, "dw_in_scatter")
        dada[l] = jnp.concatenate([msum[1], msum[0], lnsum[2]])
        dkp = dict(conv_a=asum[0:3], conv_c=lsum[0:4], conv_c_b=lsum[4], lru_ba=lsum[5], lru_bx=lsum[6], sp=lsum[7],
                   wa=dwa, wx=dwx, ar=da_[0], ai=da_[1], bbre=dbbre, bbim=dbbim, cre=dcre, cim=dcim,
                   s5_d=ssum[0], s5_b_glu=ssum[1])
        lgrads[l] = dict(sv['pull'](dkp)[0], ln_g=lnsum[0], ln_b=lnsum[1])
    grad_x = dout[None]

    out = {}

    def put(name, res):
        out[name] = res

    put('w_in', _adamw(dw_in_parts, w_in, m_w_in, v_w_in, "adamw_w_in"))
    put('w_out', _adamw(dw_out_parts, w_out, m_w_out, v_w_out, "adamw_w_out"))
    glu_parts = [_exchange(dw_glu_parts[l], False, "scatter_w_glu") for l in range(DEPTH)]
    put('s5_w_glu', _adamw(glu_parts, s5_w_glu, m_s5_w_glu, v_s5_w_glu, "adamw_w_glu"))

    dada_all = _exchange(jnp.stack(dada), True, "gather_dada")
    dada_cols = jnp.transpose(lax.dynamic_slice(dada_all, (0, 0, me * NS), (NDEV, DEPTH, NS)), (1, 0, 2))
    gw_ada, gb_ada = _ada_grads(c_all, dada_all, dada_cols)
    put('w_ada', _adamw([gw_ada[l] for l in range(DEPTH)], w_ada, m_w_ada, v_w_ada, "adamw_w_ada"))

    local = {'rel_bias': bias_pull(dbias)[0]}
    for n in _LAYER_SMALL + ['ln_g', 'ln_b']:
        local[n] = jnp.stack([lgrads[l][n] for l in range(DEPTH)])
    local['loss'] = loss_local.reshape(1)
    packed = _SMALL + ['loss']
    shapes = [local[n].shape for n in packed]
    summed = _sum8(_exchange(_pack([local[n] for n in packed]), True, "gather_small_grads"), "sum_small_grads")
    gs = dict(zip(packed, _unpack(summed, shapes)))
    loss = gs['loss'][0]
    gs['conv_a'] = lax.dynamic_slice_in_dim(gs['conv_a'], me * SH, SH, axis=2)
    gs['conv_c'] = lax.dynamic_slice_in_dim(gs['conv_c'], me * SH, SH, axis=2)
    gs['b_ada'] = gb_ada
    names = _SMALL + ['b_ada']
    shapes = [a[n].shape for n in names]
    res = _adamw([_pack([gs[n] for n in names])], _pack([a[n] for n in names])[None], _pack([a['m_' + n] for n in names])[None],
                 _pack([a['v_' + n] for n in names])[None], "adamw_small")
    unpacked = [_unpack(r[0], shapes) for r in res]
    for k, n in enumerate(names):
        put(n, tuple(unpacked[j][k] for j in range(4)))

    return (loss, grad_x, *[out[n][0] for n in _WEIGHTS], *[out[n][1] for n in _WEIGHTS],
            *[out[n][2] for n in _WEIGHTS], *[out[n][3] for n in _WEIGHTS])
```

```python
import functools
import math

import jax
import jax.numpy as jnp
from jax import lax
from jax.experimental import pallas as pl
from jax.experimental.pallas import tpu as pltpu

F32 = jnp.float32
BF16 = jnp.bfloat16

NDEV = 8
DEPTH = 2
BLK = 128
DILATIONS = ((128, 1), (512, 4), (2048, 16))
REL_BUCKETS = 32
REL_MAX_DIST = 2048
LRU_C = 8.0
S5_CH = 16
S5_STATE = 64
ALPHA = (2 * DEPTH) ** 0.25
LN_EPS = 1e-5
ADAM_LR, ADAM_B1, ADAM_B2, ADAM_EPS, ADAM_WD, ADAM_STEP = 0.001, 0.9, 0.999, 1e-08, 0.01, 10
NEG = -1e30
VMEM_LIMIT = 56 * 1024 * 1024


def _call(body, **kw):
    return pl.pallas_call(body, **kw)


def _params(*sem):
    return pltpu.CompilerParams(dimension_semantics=sem, vmem_limit_bytes=VMEM_LIMIT)


def _sigmoid(x):
    return 1.0 / (1.0 + jnp.exp(-x))


def _silu(x):
    return x * _sigmoid(x)


def _dsilu(x):
    s = _sigmoid(x)
    return s * (1.0 + x * (1.0 - s))


_GC = math.sqrt(2.0 / math.pi)


def _gelu(x):
    return 0.5 * x * (1.0 + jnp.tanh(_GC * (x + 0.044715 * x * x * x)))


def _dgelu(x):
    t = jnp.tanh(_GC * (x + 0.044715 * x * x * x))
    return 0.5 * (1.0 + t) + 0.5 * x * (1.0 - t * t) * _GC * (1.0 + 3.0 * 0.044715 * x * x)


def _expm1(x):
    series = x * (1.0 + x * (1.0 / 2) * (1.0 + x * (1.0 / 3) * (1.0 + x * (1.0 / 4) * (1.0 + x * (1.0 / 5) * (1.0 + x * (1.0 / 6))))))
    return jnp.where(jnp.abs(x) < 0.25, series, jnp.exp(x) - 1.0)


def _dot(a, b):
    return jnp.dot(a, b, preferred_element_type=F32)


def _dot_nt(a, b):
    return lax.dot_general(a, b, (((1,), (1,)), ((), ())), preferred_element_type=F32)


def _dot_tn(a, b):
    return lax.dot_general(a, b, (((0,), (0,)), ((), ())), preferred_element_type=F32)


def _colsum(v):
    return jnp.sum(v, axis=0, keepdims=True)


def _exchange(x, gather, name):
    blk = x.shape if gather else x.shape[1:]

    def gather_body(x_ref, o_ref, send_sems, recv_sems, local_sem):
        ix, iy, ic = lax.axis_index("x"), lax.axis_index("y"), lax.axis_index("c")
        me, sibling = (ix, iy, ic), (ix, iy, 1 - ic)
        chips = [(1 - ix, iy), (ix, 1 - iy), (1 - ix, 1 - iy)]

        def slot(px, py, pc):
            return o_ref.at[4 * px + 2 * py + pc]

        def copy(k, block, to, src=None):
            return pltpu.make_async_remote_copy(
                src_ref=slot(*block) if src is None else src, dst_ref=slot(*block), send_sem=send_sems.at[k],
                recv_sem=recv_sems.at[k], device_id=to, device_id_type=pl.DeviceIdType.MESH)

        mine = pltpu.make_async_copy(x_ref, slot(*me), local_sem)
        mine.start()
        first = [copy(0, me, sibling, src=x_ref)] + [copy(1 + j, me, (*chip, ic), src=x_ref) for j, chip in enumerate(chips)]
        for cp in first:
            cp.start()
        passed = [copy(4 + j, (*chip, ic), sibling) for j, chip in enumerate(chips)]
        for j, chip in enumerate(chips):
            copy(1 + j, (*chip, ic), me).wait_recv()
            passed[j].start()
        copy(0, sibling, me).wait_recv()
        for j, chip in enumerate(chips):
            copy(4 + j, (*chip, 1 - ic), me).wait_recv()
        for cp in first + passed:
            cp.wait_send()
        mine.wait()

    def scatter_body(x_ref, o_ref, send_sems, recv_sems, local_sem):
        ix, iy, ic = lax.axis_index("x"), lax.axis_index("y"), lax.axis_index("c")
        me = 4 * ix + 2 * iy + ic
        local = pltpu.make_async_copy(x_ref.at[me], o_ref.at[me], local_sem)
        local.start()
        copies = []
        for k in range(1, NDEV):
            px = 1 - ix if k & 4 else ix
            py = 1 - iy if k & 2 else iy
            pc = 1 - ic if k & 1 else ic
            peer = 4 * px + 2 * py + pc
            cp = pltpu.make_async_remote_copy(
                src_ref=x_ref.at[peer], dst_ref=o_ref.at[me], send_sem=send_sems.at[k - 1], recv_sem=recv_sems.at[k - 1],
                device_id=(px, py, pc), device_id_type=pl.DeviceIdType.MESH)
            cp.start()
            copies.append(cp)
        for cp in copies:
            cp.wait()
        local.wait()

    return _call(
        gather_body if gather else scatter_body, name=name,
        out_shape=jax.ShapeDtypeStruct((NDEV,) + tuple(blk), x.dtype),
        in_specs=[pl.BlockSpec(memory_space=pl.ANY)],
        out_specs=pl.BlockSpec(memory_space=pl.ANY),
        scratch_shapes=[pltpu.SemaphoreType.DMA((NDEV - 1,)), pltpu.SemaphoreType.DMA((NDEV - 1,)), pltpu.SemaphoreType.DMA],
    )(x)


def _pack(arrs):
    parts = []
    for a in arrs:
        n = math.prod(a.shape)
        parts.append(jnp.pad(a.reshape(-1).astype(F32), (0, _pack_rows(n) * 128 - n)).reshape(-1, 128))
    rows = sum(p.shape[0] for p in parts)
    total = rows if rows <= 512 else -(-rows // 512) * 512
    if total > rows:
        parts.append(jnp.zeros((total - rows, 128), F32))
    return jnp.concatenate(parts, axis=0)


def _pack_rows(n):
    return -(-n // 1024) * 8


def _unpack(packed, shapes):
    out, row = [], 0
    for s in shapes:
        n = math.prod(s)
        rows = _pack_rows(n)
        out.append(packed[row:row + rows].reshape(-1)[:n].reshape(s))
        row += rows
    return out


def _sum8(parts, name):
    _, R, C = parts.shape
    TR = R
    for cand in (512, 256, 128, 64, 32, 16, 8):
        if R % cand == 0:
            TR = cand
            break

    def body(p_ref, o_ref):
        acc = p_ref[0]
        for d in range(1, NDEV):
            acc = acc + p_ref[d]
        o_ref[...] = acc

    return _call(
        body, name=name, grid=(R // TR,),
        in_specs=[pl.BlockSpec((NDEV, TR, C), lambda i: (0, i, 0))],
        out_specs=pl.BlockSpec((TR, C), lambda i: (i, 0)),
        out_shape=jax.ShapeDtypeStruct((R, C), F32),
        compiler_params=_params("arbitrary"),
    )(parts)


def _adamw(g_layers, w, m, v, name):
    L, R, C = w.shape
    g0 = g_layers[0]
    nparts = g0.shape[0] if g0.ndim == 3 else 0
    TR = R
    for cand in (256, 128, 64, 32, 16):
        if R % cand == 0 and cand * C <= 128 * 1024:
            TR = cand
            break

    def body(*refs):
        g_refs = refs[:L]
        w_ref, m_ref, v_ref, go_ref, d_ref, mo_ref, vo_ref = refs[L:]
        for ll in range(L):
            @pl.when(pl.program_id(0) == ll)
            def _(g_ref=g_refs[ll]):
                if nparts:
                    g = g_ref[0].astype(F32)
                    for d in range(1, nparts):
                        g = g + g_ref[d].astype(F32)
                else:
                    g = g_ref[...]
                m2 = ADAM_B1 * m_ref[0] + (1.0 - ADAM_B1) * g
                v2 = ADAM_B2 * v_ref[0] + (1.0 - ADAM_B2) * (g * g)
                m_hat = m2 / (1.0 - ADAM_B1 ** ADAM_STEP)
                v_hat = v2 / (1.0 - ADAM_B2 ** ADAM_STEP)
                go_ref[0] = g
                d_ref[0] = -ADAM_LR * (m_hat / (jnp.sqrt(v_hat) + ADAM_EPS) + ADAM_WD * w_ref[0])
                mo_ref[0] = m2
                vo_ref[0] = v2

    def gspec(ll):
        if nparts:
            return pl.BlockSpec((nparts, TR, C), lambda l, i: (0, jnp.where(l == ll, i, 0), 0))
        return pl.BlockSpec((TR, C), lambda l, i: (jnp.where(l == ll, i, 0), 0))

    spec = pl.BlockSpec((1, TR, C), lambda l, i: (l, i, 0))
    return _call(
        body, name=name, grid=(L, R // TR),
        in_specs=[gspec(ll) for ll in range(L)] + [spec, spec, spec], out_specs=[spec] * 4,
        out_shape=[jax.ShapeDtypeStruct((L, R, C), F32)] * 4,
        compiler_params=_params("arbitrary", "arbitrary"),
    )(*g_layers, w, m, v)


def _ada_cols(c_all, w_ada, b_cols):
    L, D, NS = w_ada.shape

    def body(c_ref, w_ref, b_ref, o_ref):
        cond = _silu(c_ref[...]).astype(BF16)
        o_ref[0] = _dot(cond, w_ref[0].astype(BF16)) + b_ref[0]

    return _call(
        body, name="ada_cols", grid=(L,),
        in_specs=[pl.BlockSpec((NDEV, D), lambda l: (0, 0)), pl.BlockSpec((1, D, NS), lambda l: (l, 0, 0)),
                  pl.BlockSpec((1, 1, NS), lambda l: (l, 0, 0))],
        out_specs=pl.BlockSpec((1, NDEV, NS), lambda l: (l, 0, 0)),
        out_shape=jax.ShapeDtypeStruct((L, NDEV, NS), F32),
        compiler_params=_params("arbitrary"),
    )(c_all, w_ada, b_cols)


def _ada_grads(c_all, dada_all, dada_cols):
    _, D = c_all.shape
    L, _, NS = dada_cols.shape
    D3 = dada_all.shape[2]

    def body(c_ref, da_ref, dc_ref, gw_ref, gb_ref):
        cond = _silu(c_ref[...]).astype(BF16)
        gw_ref[0] = _dot_tn(cond, dc_ref[0].astype(BF16))
        acc = da_ref[0]
        for d in range(1, NDEV):
            acc = acc + da_ref[d]
        gb_ref[...] = acc

    return _call(
        body, name="ada_grads", grid=(L,),
        in_specs=[pl.BlockSpec((NDEV, D), lambda l: (0, 0)), pl.BlockSpec((NDEV, L, D3), lambda l: (0, 0, 0)),
                  pl.BlockSpec((1, NDEV, NS), lambda l: (l, 0, 0))],
        out_specs=[pl.BlockSpec((1, D, NS), lambda l: (l, 0, 0)), pl.BlockSpec((L, D3), lambda l: (0, 0))],
        out_shape=[jax.ShapeDtypeStruct((L, D, NS), F32), jax.ShapeDtypeStruct((L, D3), F32)],
        compiler_params=_params("arbitrary"),
    )(c_all, dada_all, dada_cols)


def _tile(S, want):
    return want if S % want == 0 else S


def _modulate(x, mod):
    S, D = x.shape
    TS = _tile(S, 512)

    def body(x_ref, mod_ref, h_ref):
        h_ref[...] = (x_ref[...] * mod_ref[0:1, :] + mod_ref[1:2, :]).astype(BF16)

    spec = pl.BlockSpec((TS, D), lambda i: (i, 0))
    return _call(
        body, name="modulate", grid=(S // TS,),
        in_specs=[spec, pl.BlockSpec((8, D), lambda i: (0, 0))], out_specs=spec,
        out_shape=jax.ShapeDtypeStruct((S, D), BF16),
        compiler_params=_params("arbitrary"),
    )(x, mod)


def _peer(k):
    ix, iy, ic = lax.axis_index("x"), lax.axis_index("y"), lax.axis_index("c")
    bx, by, bc = (k >> 2) & 1, (k >> 1) & 1, k & 1
    px, py, pc = ix + bx - 2 * ix * bx, iy + by - 2 * iy * by, ic + bc - 2 * ic * bc
    return (px, py, pc), 4 * px + 2 * py + pc


GATHER_ORDER = (0, 1, 2, 4, 3, 5, 6, 7)
SCATTER_ORDER = (2, 4, 6, 3, 5, 7, 1, 0)


def _offset(order, k):
    off = jnp.int32(order[-1])
    for step in reversed(range(len(order) - 1)):
        off = jnp.where(k == step, jnp.int32(order[step]), off)
    return off


def _proj_gather(h, w_shard, extras, me1):
    assert GATHER_ORDER[0] == 0
    S, D = h.shape
    NS = w_shard.shape[1]
    TS = _tile(S, 512)
    nt = S // TS
    ne = len(extras)

    def body(me_ref, h_ref, w_ref, *rest):
        x_refs, o_ref, wg_ref = rest[:ne], rest[ne], rest[ne + 1]
        xg_refs = rest[ne + 2:2 * ne + 2]
        wbuf, send_sems, recv_sems, local_sems, load_sems = rest[2 * ne + 2:]
        k, i = pl.program_id(0), pl.program_id(1)
        me = me_ref[0]
        off = _offset(GATHER_ORDER, k)

        def push(src, dst, which, kk):
            peer, _ = _peer(kk)
            return pltpu.make_async_remote_copy(
                src_ref=src, dst_ref=dst.at[me], send_sem=send_sems.at[which, kk], recv_sem=recv_sems.at[which, kk],
                device_id=peer, device_id_type=pl.DeviceIdType.MESH)

        def own(src, dst, which):
            return pltpu.make_async_copy(src, dst.at[me], local_sems.at[which])

        pairs = [(w_ref, wg_ref)] + list(zip(x_refs, xg_refs))

        @pl.when(jnp.logical_and(k == 0, i == 0))
        def _():
            for which, (src, dst) in enumerate(pairs):
                own(src, dst, which).start()
                for kk in GATHER_ORDER[1:]:
                    push(src, dst, which, kk).start()
            first = pltpu.make_async_copy(w_ref, wbuf.at[0], load_sems.at[0])
            first.start()
            first.wait()

        @pl.when(jnp.logical_and(k > 0, i == 0))
        def _():
            push(w_ref, wg_ref, 0, off).wait_recv()
            _, blk = _peer(off)
            load = pltpu.make_async_copy(wg_ref.at[blk], wbuf.at[k % 2], load_sems.at[k % 2])
            load.start()
            load.wait()

        o_ref[...] = _dot(h_ref[...], wbuf[k % 2])

        @pl.when(jnp.logical_and(k == NDEV - 1, i == nt - 1))
        def _():
            for which, (src, dst) in enumerate(pairs):
                for kk in range(1, NDEV):
                    cp = push(src, dst, which, kk)
                    cp.wait_send()
                    if which > 0:
                        cp.wait_recv()
                own(src, dst, which).wait()

    def col(k, i, me_ref):
        m, off = me_ref[0], _offset(GATHER_ORDER, k)
        return i, (m | off) - (m & off)

    anyspec = pl.BlockSpec(memory_space=pl.ANY)
    grid_spec = pltpu.PrefetchScalarGridSpec(
        num_scalar_prefetch=1, grid=(NDEV, nt),
        in_specs=[pl.BlockSpec((TS, D), lambda k, i, me_ref: (i, 0)), anyspec] + [anyspec] * ne,
        out_specs=[pl.BlockSpec((TS, NS), col), anyspec] + [anyspec] * ne,
        scratch_shapes=[pltpu.VMEM((2, D, NS), BF16), pltpu.SemaphoreType.DMA((1 + ne, NDEV)), pltpu.SemaphoreType.DMA((1 + ne, NDEV)),
                        pltpu.SemaphoreType.DMA((1 + ne,)), pltpu.SemaphoreType.DMA((2,))])
    return _call(
        body, name="proj_gather", grid_spec=grid_spec,
        out_shape=[jax.ShapeDtypeStruct((S, NDEV * NS), F32), jax.ShapeDtypeStruct((NDEV, D, NS), BF16)]
        + [jax.ShapeDtypeStruct((NDEV,) + e.shape, e.dtype) for e in extras],
        compiler_params=_params("arbitrary", "arbitrary"),
    )(me1, h, w_shard, *extras)


def _dw_scatter(a, b, blk, by_rows, me1, name):
    assert SCATTER_ORDER[-1] == 0
    S = a.shape[0]
    TK = _tile(S, 512)
    nk = S // TK
    shape = (blk, b.shape[1]) if by_rows else (a.shape[1], blk)
    last = NDEV - 1

    def body(me_ref, a_ref, b_ref, recv_ref, acc, stage, send_sems, recv_sems, local_sem):
        k, kk = pl.program_id(0), pl.program_id(1)
        me = me_ref[0]

        def push(off, slot):
            peer, _ = _peer(off)
            return pltpu.make_async_remote_copy(
                src_ref=stage.at[slot], dst_ref=recv_ref.at[me], send_sem=send_sems.at[off], recv_sem=recv_sems.at[off],
                device_id=peer, device_id_type=pl.DeviceIdType.MESH)

        own = pltpu.make_async_copy(stage.at[last % 2], recv_ref.at[me], local_sem)

        @pl.when(kk == 0)
        def _():
            acc[...] = jnp.zeros_like(acc)
        acc[...] += _dot_tn(a_ref[...], b_ref[...])

        @pl.when(kk == nk - 1)
        def _():
            @pl.when(k >= 2)
            def _():
                push(_offset(SCATTER_ORDER, k - 2), k % 2).wait_send()

            stage[k % 2] = acc[...].astype(BF16)

            @pl.when(k < last)
            def _():
                push(_offset(SCATTER_ORDER, k), k % 2).start()

            @pl.when(k == last)
            def _():
                own.start()
                push(SCATTER_ORDER[last - 1], (last - 1) % 2).wait_send()
                own.wait()
                for off in range(1, NDEV):
                    push(off, 0).wait_recv()

    def blk_of(k, me_ref):
        m, off = me_ref[0], _offset(SCATTER_ORDER, k)
        return (m | off) - (m & off)

    if by_rows:
        in_specs = [pl.BlockSpec((TK, blk), lambda k, kk, me_ref: (kk, blk_of(k, me_ref))),
                    pl.BlockSpec((TK, b.shape[1]), lambda k, kk, me_ref: (kk, 0))]
    else:
        in_specs = [pl.BlockSpec((TK, a.shape[1]), lambda k, kk, me_ref: (kk, 0)),
                    pl.BlockSpec((TK, blk), lambda k, kk, me_ref: (kk, blk_of(k, me_ref)))]
    grid_spec = pltpu.PrefetchScalarGridSpec(
        num_scalar_prefetch=1, grid=(NDEV, nk), in_specs=in_specs,
        out_specs=pl.BlockSpec(memory_space=pl.ANY),
        scratch_shapes=[pltpu.VMEM(shape, F32), pltpu.VMEM((2,) + shape, BF16), pltpu.SemaphoreType.DMA((NDEV,)),
                        pltpu.SemaphoreType.DMA((NDEV,)), pltpu.SemaphoreType.DMA])
    return _call(
        body, name=name, grid_spec=grid_spec,
        out_shape=jax.ShapeDtypeStruct((NDEV,) + shape, BF16),
        compiler_params=_params("arbitrary", "arbitrary"),
    )(me1, a, b)


def _chunk(TS, BR, k):
    return pl.BlockSpec((TS, BR), lambda i: (i, k))


def _halo_prev(TS, BR, k):
    return pl.BlockSpec((8, BR), lambda i: (jnp.maximum(i * (TS // 8) - 1, 0), k))


def _conv_a_fwd(proj, w8, BR):
    S = proj.shape[0]
    TS = _tile(S, 512)

    def body(ab_ref, ac_ref, ax_ref, ag_ref, hc_ref, hx_ref, w_ref, o_ref, ext):
        i = pl.program_id(0)
        u = ac_ref[...] * ax_ref[...]
        ext[0:8, :] = jnp.where(i > 0, hc_ref[...] * hx_ref[...], 0.0)
        ext[8:8 + TS, :] = u
        conv = ext[pl.ds(6, TS), :] * w_ref[0:1, :] + ext[pl.ds(7, TS), :] * w_ref[1:2, :] + u * w_ref[2:3, :]
        o_ref[...] = ab_ref[...] * conv * _silu(ag_ref[...])

    return _call(
        body, name="conv_a_fwd", grid=(S // TS,),
        in_specs=[_chunk(TS, BR, 0), _chunk(TS, BR, 1), _chunk(TS, BR, 2), _chunk(TS, BR, 3),
                  _halo_prev(TS, BR, 1), _halo_prev(TS, BR, 2), pl.BlockSpec((8, BR), lambda i: (0, 0))],
        out_specs=pl.BlockSpec((TS, BR), lambda i: (i, 0)),
        out_shape=jax.ShapeDtypeStruct((S, BR), F32),
        scratch_shapes=[pltpu.VMEM((TS + 8, BR), F32)],
        compiler_params=_params("arbitrary"),
    )(proj, proj, proj, proj, proj, proj, w8)


ATT_UNIT = 2048
ATT_LANES = 128


def _attn_mask():
    i = lax.broadcasted_iota(jnp.int32, (BLK, 2 * BLK), 0)
    j = lax.broadcasted_iota(jnp.int32, (BLK, 2 * BLK), 1)
    return jnp.logical_and(j >= i, j <= i + BLK), j >= BLK


def _attn_rows(n, r, dil):
    return pl.ds(n * BLK * dil + r, BLK, stride=dil) if dil > 1 else pl.ds(n * BLK, BLK)


def _attn_geometry(S, BR, dil):
    HD = BR // 8
    U = ATT_UNIT
    assert S % U == 0 and BR % ATT_LANES == 0 and ATT_LANES % HD == 0 and U % (BLK * dil) == 0
    return HD, U, S // U, U // (BLK * dil), ATT_LANES // HD, BR // ATT_LANES


def _attn_fwd(proj, bias, dil, BR):
    S = proj.shape[0]
    HD, U, NU, nbu, hpb, HBK = _attn_geometry(S, BR, dil)
    scale = HD ** -0.5
    combos = [(n, r) for n in range(nbu) for r in range(dil)]

    def body(q_ref, kc_ref, kp_ref, vc_ref, vp_ref, b_ref, o_ref, l_ref, sbuf, pbuf):
        m = pl.program_id(1)
        band, is_cur = _attn_mask()
        first = jnp.logical_and(band, jnp.logical_or(is_cur, m > 0))

        def keys(ref_c, ref_p, n, r):
            prev = ref_c[_attn_rows(n - 1, r, dil), :] if n > 0 else ref_p[_attn_rows(nbu - 1, r, dil), :]
            return jnp.concatenate([prev, ref_c[_attn_rows(n, r, dil), :]], axis=0).astype(BF16)

        for c, (n, r) in enumerate(combos):
            q = (q_ref[_attn_rows(n, r, dil), :] * scale).astype(BF16)
            kk = keys(kc_ref, kp_ref, n, r)
            for h in range(hpb):
                sl = slice(h * HD, (h + 1) * HD)
                sbuf[c * hpb + h] = _dot_nt(q[:, sl], kk[:, sl])
        for c, (n, r) in enumerate(combos):
            lses = []
            for h in range(hpb):
                s = jnp.where(band if n > 0 else first, sbuf[c * hpb + h] + b_ref[h], NEG)
                mx = jnp.max(s, axis=-1, keepdims=True)
                p = jnp.exp(s - mx)
                l = jnp.sum(p, axis=-1, keepdims=True)
                pbuf[c * hpb + h] = (p * (1.0 / l)).astype(BF16)
                lses.append(jnp.broadcast_to(mx + jnp.log(l), (BLK, HD)))
            l_ref[_attn_rows(n, r, dil), :] = jnp.concatenate(lses, axis=1)
        for c, (n, r) in enumerate(combos):
            vv = keys(vc_ref, vp_ref, n, r)
            o_ref[_attn_rows(n, r, dil), :] = jnp.concatenate(
                [_dot(pbuf[c * hpb + h], vv[:, h * HD:(h + 1) * HD]) for h in range(hpb)], axis=1)

    def cur(c):
        return pl.BlockSpec((U, ATT_LANES), lambda hb, m: (m, c * HBK + hb))

    def prev(c):
        return pl.BlockSpec((U, ATT_LANES), lambda hb, m: (jnp.maximum(m - 1, 0), c * HBK + hb))

    ospec = pl.BlockSpec((U, ATT_LANES), lambda hb, m: (m, hb))
    nhb = len(combos) * hpb
    return _call(
        body, name="attn_fwd_d%d" % dil, grid=(HBK, NU),
        in_specs=[cur(4), cur(5), prev(5), cur(6), prev(6), pl.BlockSpec((hpb, BLK, 2 * BLK), lambda hb, m: (hb, 0, 0))],
        out_specs=[ospec, ospec],
        out_shape=[jax.ShapeDtypeStruct((S, BR), F32)] * 2,
        scratch_shapes=[pltpu.VMEM((nhb, BLK, 2 * BLK), F32), pltpu.VMEM((nhb, BLK, 2 * BLK), BF16)],
        compiler_params=_params("arbitrary", "arbitrary"),
    )(proj, proj, proj, proj, proj, bias)


def _mix_weights(l1, l2, l3):
    m = jnp.maximum(jnp.maximum(l1, l2), l3)
    e1, e2, e3 = jnp.exp(l1 - m), jnp.exp(l2 - m), jnp.exp(l3 - m)
    inv = 1.0 / (e1 + e2 + e3)
    return e1 * inv, e2 * inv, e3 * inv


def _attn_mix_fwd(proj, os_, ls_, BR):
    S = proj.shape[0]
    TS = _tile(S, 512)

    def body(o1, o2, o3, l1, l2, l3, g_ref, y_ref):
        w1, w2, w3 = _mix_weights(l1[...], l2[...], l3[...])
        y_ref[...] = (w1 * o1[...] + w2 * o2[...] + w3 * o3[...]) * _silu(g_ref[...])

    spec = pl.BlockSpec((TS, BR), lambda i: (i, 0))
    return _call(
        body, name="attn_mix_fwd", grid=(S // TS,),
        in_specs=[spec] * 6 + [_chunk(TS, BR, 7)], out_specs=spec,
        out_shape=jax.ShapeDtypeStruct((S, BR), F32),
        compiler_params=_params("arbitrary"),
    )(*os_, *ls_, proj)


def _lru_gates(xc, wa, wx, ba, bx, sp):
    xb = xc.astype(BF16)
    r = _sigmoid(_dot(xb, wa) + ba)
    ig = _sigmoid(_dot(xb, wx) + bx)
    la = -LRU_C * r * sp
    a = jnp.exp(la)
    mult = jnp.sqrt(-_expm1(2.0 * la))
    return r, ig, a, mult


def _lru_fwd(proj, rows, wa, wx, BR):
    S = proj.shape[0]
    TS = _tile(S, 256)
    PAD = TS // 2

    def body(cx_ref, hx_ref, cg_ref, r_ref, wa_ref, wx_ref, y_ref, h_ref, ext, sa, sb, carry):
        i = pl.program_id(0)

        @pl.when(i == 0)
        def _():
            sa[0:PAD, :] = jnp.ones((PAD, BR), F32)
            sb[0:PAD, :] = jnp.zeros((PAD, BR), F32)
            carry[...] = jnp.zeros_like(carry)

        cx = cx_ref[...]
        ext[0:8, :] = jnp.where(i > 0, hx_ref[...], 0.0)
        ext[8:8 + TS, :] = cx
        xc = (ext[pl.ds(5, TS), :] * r_ref[0:1, :] + ext[pl.ds(6, TS), :] * r_ref[1:2, :]
              + ext[pl.ds(7, TS), :] * r_ref[2:3, :] + cx * r_ref[3:4, :] + r_ref[4:5, :])
        _, ig, a, mult = _lru_gates(xc, wa_ref[...], wx_ref[...], r_ref[5:6, :], r_ref[6:7, :], r_ref[7:8, :])
        sa[PAD:PAD + TS, :] = a
        sb[PAD:PAD + TS, :] = mult * ig * xc
        d = 1
        while d < TS:
            A, B = sa[PAD:PAD + TS, :], sb[PAD:PAD + TS, :]
            As, Bs = sa[pl.ds(PAD - d, TS), :], sb[pl.ds(PAD - d, TS), :]
            sb[PAD:PAD + TS, :] = A * Bs + B
            sa[PAD:PAD + TS, :] = A * As
            d *= 2
        h = sb[PAD:PAD + TS, :] + sa[PAD:PAD + TS, :] * carry[0:1, :]
        carry[0:1, :] = h[TS - 1:TS, :]
        h_ref[...] = h
        y_ref[...] = h * _silu(cg_ref[...])

    full = pl.BlockSpec((BR, BR), lambda i: (0, 0))
    spec = pl.BlockSpec((TS, BR), lambda i: (i, 0))
    return _call(
        body, name="lru_fwd", grid=(S // TS,),
        in_specs=[_chunk(TS, BR, 8), _halo_prev(TS, BR, 8), _chunk(TS, BR, 9), pl.BlockSpec((8, BR), lambda i: (0, 0)), full, full],
        out_specs=[spec, spec],
        out_shape=[jax.ShapeDtypeStruct((S, BR), F32)] * 2,
        scratch_shapes=[pltpu.VMEM((TS + 8, BR), F32), pltpu.VMEM((PAD + TS, BR), F32), pltpu.VMEM((PAD + TS, BR), F32),
                        pltpu.VMEM((8, BR), F32)],
        compiler_params=_params("arbitrary"),
    )(proj, proj, proj, rows, wa, wx)


S5_LAGS = 8
S5_CB = 8 * S5_CH
S5_SB = 8 * S5_STATE


def _s5_fwd(proj, a8, bbl, cre, cim, rows, wg, BR):
    S = proj.shape[0]
    W = a8.shape[1]
    TS = _tile(S, 256)
    CB, SB, J = S5_CB, S5_SB, S5_LAGS
    nblk = BR // CB

    def body(u_ref, uh_ref, dg_ref, a8_ref, bbl_ref, cre_ref, cim_ref, r_ref, wg_ref,
             y_ref, xre_ref, xim_ref, uext, ulag, sre, sim, ypre_s, carry):
        i = pl.program_id(0)

        @pl.when(i == 0)
        def _():
            carry[...] = jnp.zeros_like(carry)

        uext[0:8, :] = jnp.where(i > 0, uh_ref[...], 0.0)
        uext[8:8 + TS, :] = u_ref[...]
        for m in range(nblk):
            cs, ws = slice(m * CB, (m + 1) * CB), slice(m * SB, (m + 1) * SB)
            for j in range(J):
                ulag[:, j * CB:(j + 1) * CB] = uext[pl.ds(8 - j, TS), cs].astype(BF16)
            w = _dot(ulag[...], bbl_ref[m])
            sre[...] = w[:, 0:SB]
            sim[...] = w[:, SB:2 * SB]
            ar = jnp.broadcast_to(a8_ref[0:1, ws], (8, SB))
            ai = jnp.broadcast_to(a8_ref[1:2, ws], (8, SB))
            xr, xi = carry[0:8, ws], carry[8:16, ws]
            for g in range(TS // 8):
                rg = slice(8 * g, 8 * g + 8)
                xr, xi = sre[rg, :] + ar * xr - ai * xi, sim[rg, :] + ar * xi + ai * xr
                sre[rg, :] = xr
                sim[rg, :] = xi
            carry[0:8, ws] = xr
            carry[8:16, ws] = xi
            xre, xim = sre[...], sim[...]
            xre_ref[:, ws] = xre
            xim_ref[:, ws] = xim
            ypre_s[:, cs] = _dot(xre.astype(BF16), cre_ref[ws, :]) - _dot(xim.astype(BF16), cim_ref[ws, :])
        dg = dg_ref[...]
        yg = _gelu(ypre_s[...] + r_ref[0:1, :] * u_ref[...])
        s = _sigmoid(_dot(yg.astype(BF16), wg_ref[...]) + r_ref[1:2, :])
        y_ref[...] = yg * s * _silu(dg)

    def const(shape):
        return pl.BlockSpec(shape, lambda i: (0,) * len(shape))

    return _call(
        body, name="s5_fwd", grid=(S // TS,),
        in_specs=[_chunk(TS, BR, 10), _halo_prev(TS, BR, 10), _chunk(TS, BR, 11), const((8, W)), const((nblk, J * CB, 2 * SB)),
                  const((W, CB)), const((W, CB)), const((8, BR)), const((BR, BR))],
        out_specs=[pl.BlockSpec((TS, BR), lambda i: (i, 0)), pl.BlockSpec((TS, W), lambda i: (i, 0)), pl.BlockSpec((TS, W), lambda i: (i, 0))],
        out_shape=[jax.ShapeDtypeStruct((S, BR), F32), jax.ShapeDtypeStruct((S, W), F32), jax.ShapeDtypeStruct((S, W), F32)],
        scratch_shapes=[pltpu.VMEM((TS + 8, BR), F32), pltpu.VMEM((TS, J * CB), BF16), pltpu.VMEM((TS, SB), F32), pltpu.VMEM((TS, SB), F32),
                        pltpu.VMEM((TS, BR), F32), pltpu.VMEM((16, W), F32)],
        compiler_params=_params("arbitrary"),
    )(proj, proj, proj, a8, bbl, cre, cim, rows, wg)


def _out_ln(ys, w_out, x, rows, target=None):
    S, D = x.shape
    BR = D // 4
    TS = _tile(S, 256)
    last = target is not None

    def body(ya, yb, yc, yd, w_ref, x_ref, r_ref, *rest):
        t_ref = rest[0] if last else None
        o_ref, y_ref, cat_ref = rest[1:4] if last else rest[0:3]
        cat = jnp.concatenate([ya[...], yb[...], yc[...], yd[...]], axis=1).astype(BF16)
        y = _dot(cat, w_ref[...])
        z = ALPHA * x_ref[...] + r_ref[0:1, :] * y
        mu = jnp.mean(z, axis=-1, keepdims=True)
        zc = z - mu
        var = jnp.mean(zc * zc, axis=-1, keepdims=True)
        xn = zc * lax.rsqrt(var + LN_EPS) * r_ref[1:2, :] + r_ref[2:3, :]
        y_ref[...] = y.astype(BF16)
        cat_ref[...] = cat
        if last:
            l_ref = rest[4]

            @pl.when(pl.program_id(0) == 0)
            def _():
                l_ref[...] = jnp.zeros_like(l_ref)
            diff = xn - t_ref[...]
            o_ref[...] = diff * (1.0 / D)
            l_ref[...] += jnp.sum(_colsum(diff * diff), axis=1, keepdims=True) * (0.5 / D)
        else:
            o_ref[...] = xn

    yspec = pl.BlockSpec((TS, BR), lambda i: (i, 0))
    spec = pl.BlockSpec((TS, D), lambda i: (i, 0))
    lspec = pl.BlockSpec((1, 1), lambda i: (0, 0))
    return _call(
        body, name="out_ln_loss" if last else "out_ln", grid=(S // TS,),
        in_specs=[yspec] * 4 + [pl.BlockSpec((D, D), lambda i: (0, 0)), spec, pl.BlockSpec((8, D), lambda i: (0, 0))] + [spec] * last,
        out_specs=[spec, spec, spec] + [lspec] * last,
        out_shape=[jax.ShapeDtypeStruct((S, D), F32), jax.ShapeDtypeStruct((S, D), BF16), jax.ShapeDtypeStruct((S, D), BF16)]
        + [jax.ShapeDtypeStruct((1, 1), F32)] * last,
        compiler_params=_params("arbitrary"),
    )(*ys, w_out, x, rows, *([target] if last else []))


def _ln_bwd(dout, x, y, rows, w_out):
    S, D = x.shape
    TS = _tile(S, 256)

    def body(do_ref, x_ref, y_ref, r_ref, w_ref, dxr_ref, dyb_ref, dcat_ref, acc_ref):
        @pl.when(pl.program_id(0) == 0)
        def _():
            acc_ref[...] = jnp.zeros_like(acc_ref)
        g1, lg = r_ref[0:1, :], r_ref[1:2, :]
        yv = y_ref[...].astype(F32)
        z = ALPHA * x_ref[...] + g1 * yv
        mu = jnp.mean(z, axis=-1, keepdims=True)
        zc = z - mu
        var = jnp.mean(zc * zc, axis=-1, keepdims=True)
        rstd = lax.rsqrt(var + LN_EPS)
        xhat = zc * rstd
        do = do_ref[...]
        dxh = do * lg
        dz = rstd * (dxh - jnp.mean(dxh, axis=-1, keepdims=True) - xhat * jnp.mean(dxh * xhat, axis=-1, keepdims=True))
        dxr_ref[...] = (ALPHA * dz).astype(BF16)
        dyb = (g1 * dz).astype(BF16)
        dyb_ref[...] = dyb
        dcat_ref[...] = _dot_nt(dyb, w_ref[...])
        acc_ref[0:1, :] += _colsum(do * xhat)
        acc_ref[1:2, :] += _colsum(do)
        acc_ref[2:3, :] += _colsum(dz * yv)

    spec = pl.BlockSpec((TS, D), lambda i: (i, 0))
    rspec = pl.BlockSpec((8, D), lambda i: (0, 0))
    return _call(
        body, name="ln_bwd", grid=(S // TS,),
        in_specs=[spec, spec, spec, rspec, pl.BlockSpec((D, D), lambda i: (0, 0))],
        out_specs=[spec, spec, spec, rspec],
        out_shape=[jax.ShapeDtypeStruct((S, D), BF16), jax.ShapeDtypeStruct((S, D), BF16), jax.ShapeDtypeStruct((S, D), F32),
                   jax.ShapeDtypeStruct((8, D), F32)],
        compiler_params=_params("arbitrary"),
    )(dout, x, y, rows, w_out)


def _dh(dproj, wg, dxres, x, mod):
    S, D = x.shape
    NS = wg.shape[2]
    TS = _tile(S, 256)

    def body(dp_ref, w_hbm, dxr_ref, x_ref, mod_ref, dx_ref, sum_ref, w_vmem, sem):
        @pl.when(pl.program_id(0) == 0)
        def _():
            load = pltpu.make_async_copy(w_hbm, w_vmem, sem)
            load.start()
            sum_ref[...] = jnp.zeros_like(sum_ref)
            load.wait()

        dh = _dot_nt(dp_ref[:, 0:NS], w_vmem[0])
        for j in range(1, NDEV):
            dh = dh + _dot_nt(dp_ref[:, j * NS:(j + 1) * NS], w_vmem[j])
        dx_ref[...] = dxr_ref[...].astype(F32) + dh * mod_ref[0:1, :]
        sum_ref[0:1, :] += _colsum(dh * x_ref[...])
        sum_ref[1:2, :] += _colsum(dh)

    spec = pl.BlockSpec((TS, D), lambda i: (i, 0))
    rspec = pl.BlockSpec((8, D), lambda i: (0, 0))
    return _call(
        body, name="dh", grid=(S // TS,),
        in_specs=[pl.BlockSpec((TS, NDEV * NS), lambda i: (i, 0)), pl.BlockSpec(memory_space=pl.ANY), spec, spec, rspec],
        out_specs=[spec, rspec],
        out_shape=[jax.ShapeDtypeStruct((S, D), F32), jax.ShapeDtypeStruct((8, D), F32)],
        scratch_shapes=[pltpu.VMEM((NDEV, D, NS), BF16), pltpu.SemaphoreType.DMA],
        compiler_params=_params("arbitrary"),
    )(dproj, wg, dxres, x, mod)


def _halo_next(S, TS, BR, k):
    last = S // 8 - 1
    return pl.BlockSpec((8, BR), lambda i: (jnp.minimum((i + 1) * (TS // 8), last), k))


def _conv_a_bwd(proj, dcat, w8, BR):
    S = proj.shape[0]
    TS = _tile(S, 512)
    nt = S // TS

    def body(ab_ref, ac_ref, ax_ref, ag_ref, dy_ref, hc_ref, hx_ref, nab_ref, nag_ref, ndy_ref, w_ref, dp_ref, dw_ref, ext, dext):
        i = pl.program_id(0)

        @pl.when(i == 0)
        def _():
            dw_ref[...] = jnp.zeros_like(dw_ref)

        w0, w1, w2 = w_ref[0:1, :], w_ref[1:2, :], w_ref[2:3, :]
        ab, ac, ax, ag, dy = ab_ref[...], ac_ref[...], ax_ref[...], ag_ref[...], dy_ref[...]
        u = ac * ax
        ext[0:8, :] = jnp.where(i > 0, hc_ref[...] * hx_ref[...], 0.0)
        ext[8:8 + TS, :] = u
        u1, u2 = ext[pl.ds(7, TS), :], ext[pl.ds(6, TS), :]
        conv = u2 * w0 + u1 * w1 + u * w2
        sg = _silu(ag)
        dconv = dy * ab * sg
        dext[0:TS, :] = dconv
        dext[TS:TS + 8, :] = jnp.where(i < nt - 1, ndy_ref[...] * nab_ref[...] * _silu(nag_ref[...]), 0.0)
        du = w2 * dconv + w1 * dext[pl.ds(1, TS), :] + w0 * dext[pl.ds(2, TS), :]
        dp_ref[:, 0:BR] = (dy * conv * sg).astype(BF16)
        dp_ref[:, BR:2 * BR] = (du * ax).astype(BF16)
        dp_ref[:, 2 * BR:3 * BR] = (du * ac).astype(BF16)
        dp_ref[:, 3 * BR:4 * BR] = (dy * ab * conv * _dsilu(ag)).astype(BF16)
        dw_ref[0:1, :] += _colsum(dconv * u2)
        dw_ref[1:2, :] += _colsum(dconv * u1)
        dw_ref[2:3, :] += _colsum(dconv * u)

    rspec = pl.BlockSpec((8, BR), lambda i: (0, 0))
    return _call(
        body, name="conv_a_bwd", grid=(nt,),
        in_specs=[_chunk(TS, BR, 0), _chunk(TS, BR, 1), _chunk(TS, BR, 2), _chunk(TS, BR, 3), _chunk(TS, BR, 0),
                  _halo_prev(TS, BR, 1), _halo_prev(TS, BR, 2), _halo_next(S, TS, BR, 0), _halo_next(S, TS, BR, 3),
                  _halo_next(S, TS, BR, 0), rspec],
        out_specs=[pl.BlockSpec((TS, 4 * BR), lambda i: (i, 0)), rspec],
        out_shape=[jax.ShapeDtypeStruct((S, 12 * BR), BF16), jax.ShapeDtypeStruct((8, BR), F32)],
        scratch_shapes=[pltpu.VMEM((TS + 8, BR), F32), pltpu.VMEM((TS + 8, BR), F32)],
        compiler_params=_params("arbitrary"),
    )(proj, proj, proj, proj, dcat, proj, proj, proj, proj, dcat, w8)


def _attn_mix_bwd(proj, dcat, os_, ls_, head_ones, BR):
    S = proj.shape[0]
    TS = _tile(S, 512)

    def body(dy_ref, g_ref, o1, o2, o3, l1, l2, l3, ones_ref, d1, d2, d3, e1, e2, e3, dg_ref):
        w1, w2, w3 = _mix_weights(l1[...], l2[...], l3[...])
        mix = w1 * o1[...] + w2 * o2[...] + w3 * o3[...]
        g = g_ref[...]
        dy = dy_ref[...]
        dmix = dy * _silu(g)
        dg_ref[...] = dy * mix * _dsilu(g)
        t = jnp.dot(dmix * mix, ones_ref[...], preferred_element_type=F32, precision=lax.Precision.HIGHEST)
        d1[...] = w1 * dmix
        d2[...] = w2 * dmix
        d3[...] = w3 * dmix
        e1[...] = -w1 * t
        e2[...] = -w2 * t
        e3[...] = -w3 * t

    spec = pl.BlockSpec((TS, BR), lambda i: (i, 0))
    outs = _call(
        body, name="attn_mix_bwd", grid=(S // TS,),
        in_specs=[_chunk(TS, BR, 1), _chunk(TS, BR, 7)] + [spec] * 6 + [pl.BlockSpec((BR, BR), lambda i: (0, 0))],
        out_specs=[spec] * 7,
        out_shape=[jax.ShapeDtypeStruct((S, BR), F32)] * 7,
        compiler_params=_params("arbitrary"),
    )(dcat, proj, *os_, *ls_, head_ones)
    return outs[0:3], outs[3:6], outs[6]


def _attn_bwd(proj, do, e, lse, bias, dil, BR):
    S = proj.shape[0]
    HD, U, NU, nbu, hpb, HBK = _attn_geometry(S, BR, dil)
    scale = HD ** -0.5
    combos = [(n, r) for n in range(nbu) for r in range(dil)]

    def body(q_ref, kc_ref, kp_ref, vc_ref, vp_ref, do_ref, e_ref, l_ref, b_ref, dq_ref, dk_ref, dv_ref, db_ref,
             sbuf, dpbuf, pbuf, dsbuf, dkacc, dvacc, nxk, nxv, cark, carv):
        j = pl.program_id(1)
        mu = NU - 1 - j
        band, is_cur = _attn_mask()
        first = jnp.logical_and(band, jnp.logical_or(is_cur, mu > 0))

        @pl.when(j == 0)
        def _():
            cark[...] = jnp.zeros_like(cark)
            carv[...] = jnp.zeros_like(carv)
            db_ref[...] = jnp.zeros_like(db_ref)

        for ref in (dkacc, dvacc, nxk, nxv):
            ref[...] = jnp.zeros_like(ref)

        def keys(ref_c, ref_p, n, r):
            prev = ref_c[_attn_rows(n - 1, r, dil), :] if n > 0 else ref_p[_attn_rows(nbu - 1, r, dil), :]
            return jnp.concatenate([prev, ref_c[_attn_rows(n, r, dil), :]], axis=0).astype(BF16)

        for c, (n, r) in enumerate(combos):
            rows = _attn_rows(n, r, dil)
            q = (q_ref[rows, :] * scale).astype(BF16)
            dob = do_ref[rows, :].astype(BF16)
            kk, vv = keys(kc_ref, kp_ref, n, r), keys(vc_ref, vp_ref, n, r)
            for h in range(hpb):
                sl = slice(h * HD, (h + 1) * HD)
                sbuf[c * hpb + h] = _dot_nt(q[:, sl], kk[:, sl])
                dpbuf[c * hpb + h] = _dot_nt(dob[:, sl], vv[:, sl])
        dbs = [None] * hpb
        for c, (n, r) in enumerate(combos):
            rows = _attn_rows(n, r, dil)
            lse, ev = l_ref[rows, :], e_ref[rows, :]
            for h in range(hpb):
                one = slice(h * HD, h * HD + 1)
                s = jnp.where(band if n > 0 else first, sbuf[c * hpb + h] + b_ref[h], NEG)
                p = jnp.exp(s - lse[:, one])
                ds = p * (dpbuf[c * hpb + h] + ev[:, one])
                pbuf[c * hpb + h] = p.astype(BF16)
                dsbuf[c * hpb + h] = ds.astype(BF16)
                dbs[h] = ds if dbs[h] is None else dbs[h] + ds
        for h in range(hpb):
            db_ref[h] += dbs[h]
        for c, (n, r) in enumerate(combos):
            rows = _attn_rows(n, r, dil)
            q = (q_ref[rows, :] * scale).astype(BF16)
            dob = do_ref[rows, :].astype(BF16)
            kk = keys(kc_ref, kp_ref, n, r)
            dq, dkk, dvv = [], [], []
            for h in range(hpb):
                sl = slice(h * HD, (h + 1) * HD)
                dsb = dsbuf[c * hpb + h]
                dq.append(_dot(dsb, kk[:, sl]) * scale)
                dkk.append(_dot_tn(dsb, q[:, sl]))
                dvv.append(_dot_tn(pbuf[c * hpb + h], dob[:, sl]))
            dq_ref[rows, :] = jnp.concatenate(dq, axis=1)
            dkk, dvv = jnp.concatenate(dkk, axis=1), jnp.concatenate(dvv, axis=1)
            dkacc[rows, :] += dkk[BLK:2 * BLK, :]
            dvacc[rows, :] += dvv[BLK:2 * BLK, :]
            if n > 0:
                prow = _attn_rows(n - 1, r, dil)
                dkacc[prow, :] += dkk[0:BLK, :]
                dvacc[prow, :] += dvv[0:BLK, :]
            else:
                prow = _attn_rows(nbu - 1, r, dil)
                nxk[prow, :] += dkk[0:BLK, :]
                nxv[prow, :] += dvv[0:BLK, :]
        dk_ref[...] = dkacc[...] + cark[...]
        dv_ref[...] = dvacc[...] + carv[...]
        cark[...] = nxk[...]
        carv[...] = nxv[...]

    def cur(c):
        return pl.BlockSpec((U, ATT_LANES), lambda hb, j: (NU - 1 - j, c * HBK + hb))

    def prev(c):
        return pl.BlockSpec((U, ATT_LANES), lambda hb, j: (jnp.maximum(NU - 2 - j, 0), c * HBK + hb))

    own = pl.BlockSpec((U, ATT_LANES), lambda hb, j: (NU - 1 - j, hb))
    bspec = pl.BlockSpec((hpb, BLK, 2 * BLK), lambda hb, j: (hb, 0, 0))
    nhb = len(combos) * hpb
    unit = pltpu.VMEM((U, ATT_LANES), F32)
    return _call(
        body, name="attn_bwd_d%d" % dil, grid=(HBK, NU),
        in_specs=[cur(4), cur(5), prev(5), cur(6), prev(6), own, own, own, bspec],
        out_specs=[own, own, own, bspec],
        out_shape=[jax.ShapeDtypeStruct((S, BR), F32)] * 3 + [jax.ShapeDtypeStruct((8, BLK, 2 * BLK), F32)],
        scratch_shapes=[pltpu.VMEM((nhb, BLK, 2 * BLK), F32)] * 2 + [pltpu.VMEM((nhb, BLK, 2 * BLK), BF16)] * 2 + [unit] * 6,
        compiler_params=_params("arbitrary", "arbitrary"),
    )(proj, proj, proj, proj, proj, do, e, lse, bias)


def _attn_dsum(ds, dbg, dproj, BR):
    S = dbg.shape[0]
    TS = _tile(S, 512)

    def body(*refs):
        o_ref = refs[11]
        for k in range(3):
            o_ref[:, k * BR:(k + 1) * BR] = (refs[k][...] + refs[3 + k][...] + refs[6 + k][...]).astype(BF16)
        o_ref[:, 3 * BR:4 * BR] = refs[9][...].astype(BF16)

    spec = pl.BlockSpec((TS, BR), lambda i: (i, 0))
    return _call(
        body, name="attn_dsum", grid=(S // TS,),
        in_specs=[spec] * 10 + [pl.BlockSpec(memory_space=pl.ANY)],
        out_specs=pl.BlockSpec((TS, 4 * BR), lambda i: (i, 1)),
        out_shape=jax.ShapeDtypeStruct(dproj.shape, BF16),
        input_output_aliases={10: 0},
        compiler_params=_params("arbitrary"),
    )(*[t for trip in ds for t in trip], dbg, dproj)


def _lru_bwd(proj, h, dcat, rows, wa, wx, dproj, BR):
    S = proj.shape[0]
    TS = _tile(S, 256)
    PAD = TS // 2
    nt = S // TS

    def body(cx_ref, hx_ref, cg_ref, h_ref, hh_ref, dy_ref, r_ref, wa_ref, wx_ref, _,
             dp_ref, sum_ref, dwa_ref, dwx_ref, ext, hext, aext, dext, sa, sb, carry):
        i = pl.program_id(0)
        ti = nt - 1 - i

        @pl.when(i == 0)
        def _():
            sa[TS:TS + PAD, :] = jnp.ones((PAD, BR), F32)
            sb[TS:TS + PAD, :] = jnp.zeros((PAD, BR), F32)
            carry[...] = jnp.zeros_like(carry)
            dext[TS:TS + 8, :] = jnp.zeros((8, BR), F32)
            sum_ref[...] = jnp.zeros_like(sum_ref)
            dwa_ref[...] = jnp.zeros_like(dwa_ref)
            dwx_ref[...] = jnp.zeros_like(dwx_ref)

        w0, w1, w2, w3 = r_ref[0:1, :], r_ref[1:2, :], r_ref[2:3, :], r_ref[3:4, :]
        sp = r_ref[7:8, :]
        cx = cx_ref[...]
        ext[0:8, :] = jnp.where(ti > 0, hx_ref[...], 0.0)
        ext[8:8 + TS, :] = cx
        x3, x2, x1 = ext[pl.ds(5, TS), :], ext[pl.ds(6, TS), :], ext[pl.ds(7, TS), :]
        xc = x3 * w0 + x2 * w1 + x1 * w2 + cx * w3 + r_ref[4:5, :]
        wa_, wx_ = wa_ref[...], wx_ref[...]
        r, ig, a, mult = _lru_gates(xc, wa_, wx_, r_ref[5:6, :], r_ref[6:7, :], sp)
        cg, dy, hv = cg_ref[...], dy_ref[...], h_ref[...]
        dp_ref[:, BR:2 * BR] = (dy * hv * _dsilu(cg)).astype(BF16)
        aext[0:TS, :] = a
        aext[TS:TS + 8, :] = jnp.broadcast_to(carry[0:1, :], (8, BR))
        sa[0:TS, :] = aext[pl.ds(1, TS), :]
        sb[0:TS, :] = dy * _silu(cg)
        d = 1
        while d < TS:
            A, B = sa[0:TS, :], sb[0:TS, :]
            As, Bs = sa[pl.ds(d, TS), :], sb[pl.ds(d, TS), :]
            sb[0:TS, :] = B + A * Bs
            sa[0:TS, :] = A * As
            d *= 2
        gh = sb[0:TS, :] + sa[0:TS, :] * carry[1:2, :]
        carry[0:1, :] = a[0:1, :]
        carry[1:2, :] = gh[0:1, :]
        hext[0:8, :] = jnp.where(ti > 0, hh_ref[...], 0.0)
        hext[8:8 + TS, :] = hv
        da = gh * hext[pl.ds(7, TS), :]
        dmult = gh * ig * xc
        dig = gh * mult * xc
        dxc = gh * mult * ig
        dla = da * a - dmult * a * a / mult
        dpr = dla * (-LRU_C) * sp * r * (1.0 - r)
        dpi = dig * ig * (1.0 - ig)
        xb, dprb, dpib = xc.astype(BF16), dpr.astype(BF16), dpi.astype(BF16)
        dwa_ref[...] += _dot_tn(xb, dprb)
        dwx_ref[...] += _dot_tn(xb, dpib)
        dxc = dxc + _dot_nt(dprb, wa_) + _dot_nt(dpib, wx_)
        sum_ref[0:1, :] += _colsum(dxc * x3)
        sum_ref[1:2, :] += _colsum(dxc * x2)
        sum_ref[2:3, :] += _colsum(dxc * x1)
        sum_ref[3:4, :] += _colsum(dxc * cx)
        sum_ref[4:5, :] += _colsum(dxc)
        sum_ref[5:6, :] += _colsum(dpr)
        sum_ref[6:7, :] += _colsum(dpi)
        sum_ref[7:8, :] += _colsum(dla * (-LRU_C) * r)
        dext[0:TS, :] = dxc
        dcx = w3 * dxc + w2 * dext[pl.ds(1, TS), :] + w1 * dext[pl.ds(2, TS), :] + w0 * dext[pl.ds(3, TS), :]
        dext[TS:TS + 8, :] = dxc[0:8, :]
        dp_ref[:, 0:BR] = dcx.astype(BF16)

    def rev(k):
        return pl.BlockSpec((TS, BR), lambda i: (nt - 1 - i, k))

    def rev_halo(k):
        return pl.BlockSpec((8, BR), lambda i: (jnp.maximum((nt - 1 - i) * (TS // 8) - 1, 0), k))

    full = pl.BlockSpec((BR, BR), lambda i: (0, 0))
    rspec = pl.BlockSpec((8, BR), lambda i: (0, 0))
    return _call(
        body, name="lru_bwd", grid=(nt,),
        in_specs=[rev(8), rev_halo(8), rev(9), rev(0), rev_halo(0), rev(2), rspec, full, full, pl.BlockSpec(memory_space=pl.ANY)],
        out_specs=[pl.BlockSpec((TS, 2 * BR), lambda i: (nt - 1 - i, 4)), rspec, full, full],
        input_output_aliases={9: 0},
        out_shape=[jax.ShapeDtypeStruct(dproj.shape, BF16), jax.ShapeDtypeStruct((8, BR), F32),
                   jax.ShapeDtypeStruct((BR, BR), F32), jax.ShapeDtypeStruct((BR, BR), F32)],
        scratch_shapes=[pltpu.VMEM((TS + 8, BR), F32)] * 4 + [pltpu.VMEM((TS + PAD, BR), F32)] * 2 + [pltpu.VMEM((8, BR), F32)],
        compiler_params=_params("arbitrary"),
    )(proj, proj, proj, h, h, dcat, rows, wa, wx, dproj)


def _s5_bwd(proj, dcat, xre_all, xim_all, a8, cl, bbre, bbim, cre, cim, rows, wg, dproj, BR):
    S = proj.shape[0]
    W = a8.shape[1]
    TS = _tile(S, 256)
    nt = S // TS
    CB, SB, J = S5_CB, S5_SB, S5_LAGS
    nblk = BR // CB

    def body(u_ref, dg_ref, dy_ref, xre_ref, xim_ref, hre_ref, him_ref, a8_ref, cl_ref, bbre_ref, bbim_ref, cre_ref, cim_ref,
             r_ref, wg_ref, _, dp_ref, sum_ref, dwg_ref, da_ref, dbbre_ref, dbbim_ref, dcre_ref, dcim_ref,
             sre, sim, ere, eim, ypre_s, dext, dylag, carry):
        i = pl.program_id(0)
        ti = nt - 1 - i

        @pl.when(i == 0)
        def _():
            dext[TS:TS + 8, :] = jnp.zeros((8, BR), F32)
            carry[...] = jnp.zeros_like(carry)
            for ref in (sum_ref, dwg_ref, da_ref, dbbre_ref, dbbim_ref, dcre_ref, dcim_ref):
                ref[...] = jnp.zeros_like(ref)

        u, dg, dy = u_ref[...], dg_ref[...], dy_ref[...]
        for m in range(nblk):
            cs, ws = slice(m * CB, (m + 1) * CB), slice(m * SB, (m + 1) * SB)
            ypre_s[:, cs] = (_dot(xre_ref[:, ws].astype(BF16), cre_ref[ws, :]) - _dot(xim_ref[:, ws].astype(BF16), cim_ref[ws, :]))
        ypre = ypre_s[...] + r_ref[0:1, :] * u
        yg = _gelu(ypre)
        wg_ = wg_ref[...]
        s = _sigmoid(_dot(yg.astype(BF16), wg_) + r_ref[1:2, :])
        dgl = dy * _silu(dg)
        dp_ref[:, BR:2 * BR] = (dy * yg * s * _dsilu(dg)).astype(BF16)
        dps = dgl * yg * s * (1.0 - s)
        dpsb = dps.astype(BF16)
        sum_ref[1:2, :] += _colsum(dps)
        dwg_ref[...] += _dot_tn(yg.astype(BF16), dpsb)
        dyp = (dgl * s + _dot_nt(dpsb, wg_)) * _dgelu(ypre)
        sum_ref[0:1, :] += _colsum(dyp * u)
        dext[0:TS, :] = dyp
        for m in range(nblk):
            cs, ws = slice(m * CB, (m + 1) * CB), slice(m * SB, (m + 1) * SB)
            dypb = dext[0:TS, cs].astype(BF16)
            xre, xim = xre_ref[:, ws], xim_ref[:, ws]
            dcre_ref[ws, :] += _dot_tn(xre.astype(BF16), dypb)
            dcim_ref[ws, :] -= _dot_tn(xim.astype(BF16), dypb)
            for j in range(J):
                dylag[:, j * CB:(j + 1) * CB] = dext[pl.ds(j, TS), cs].astype(BF16)
            w = _dot(dylag[...], cl_ref[m])
            sre[...] = w[:, 0:SB]
            sim[...] = w[:, SB:2 * SB]
            ar = jnp.broadcast_to(a8_ref[0:1, ws], (8, SB))
            ai = jnp.broadcast_to(a8_ref[1:2, ws], (8, SB))
            gr, gi = carry[0:8, ws], carry[8:16, ws]
            for g in reversed(range(TS // 8)):
                rg = slice(8 * g, 8 * g + 8)
                gr, gi = sre[rg, :] + ar * gr + ai * gi, sim[rg, :] + ar * gi - ai * gr
                sre[rg, :] = gr
                sim[rg, :] = gi
            carry[0:8, ws] = gr
            carry[8:16, ws] = gi
            gre, gim = sre[...], sim[...]
            ere[0:8, :] = jnp.where(ti > 0, hre_ref[:, ws], 0.0)
            eim[0:8, :] = jnp.where(ti > 0, him_ref[:, ws], 0.0)
            ere[8:8 + TS, :] = xre
            eim[8:8 + TS, :] = xim
            xpr, xpi = ere[pl.ds(7, TS), :], eim[pl.ds(7, TS), :]
            da_ref[0:1, ws] += _colsum(gre * xpr + gim * xpi)
            da_ref[1:2, ws] += _colsum(gim * xpr - gre * xpi)
            greb, gimb = gre.astype(BF16), gim.astype(BF16)
            ub = u_ref[:, cs].astype(BF16)
            dbbre_ref[cs, :] += _dot_tn(ub, greb)
            dbbim_ref[cs, :] += _dot_tn(ub, gimb)
            du = _dot_nt(greb, bbre_ref[cs, :]) + _dot_nt(gimb, bbim_ref[cs, :]) + r_ref[0:1, cs] * dext[0:TS, cs]
            dp_ref[:, cs] = du.astype(BF16)
        dext[TS:TS + 8, :] = dext[0:8, :]

    def rev(width, k):
        return pl.BlockSpec((TS, width), lambda i: (nt - 1 - i, k))

    def rev_halo(width):
        return pl.BlockSpec((8, width), lambda i: (jnp.maximum((nt - 1 - i) * (TS // 8) - 1, 0), 0))

    def const(shape):
        return pl.BlockSpec(shape, lambda i: (0,) * len(shape))

    return _call(
        body, name="s5_bwd", grid=(nt,),
        in_specs=[rev(BR, 10), rev(BR, 11), rev(BR, 3), rev(W, 0), rev(W, 0), rev_halo(W), rev_halo(W),
                  const((8, W)), const((nblk, J * CB, 2 * SB)), const((BR, SB)), const((BR, SB)), const((W, CB)), const((W, CB)),
                  const((8, BR)), const((BR, BR)), pl.BlockSpec(memory_space=pl.ANY)],
        input_output_aliases={15: 0},
        out_specs=[pl.BlockSpec((TS, 2 * BR), lambda i: (nt - 1 - i, 5)), const((8, BR)), const((BR, BR)), const((8, W)),
                   const((BR, SB)), const((BR, SB)), const((W, CB)), const((W, CB))],
        out_shape=[jax.ShapeDtypeStruct(dproj.shape, BF16), jax.ShapeDtypeStruct((8, BR), F32), jax.ShapeDtypeStruct((BR, BR), F32),
                   jax.ShapeDtypeStruct((8, W), F32), jax.ShapeDtypeStruct((BR, SB), F32), jax.ShapeDtypeStruct((BR, SB), F32),
                   jax.ShapeDtypeStruct((W, CB), F32), jax.ShapeDtypeStruct((W, CB), F32)],
        scratch_shapes=[pltpu.VMEM((TS, SB), F32)] * 2 + [pltpu.VMEM((TS + 8, SB), F32)] * 2
        + [pltpu.VMEM((TS, BR), F32), pltpu.VMEM((TS + 8, BR), F32), pltpu.VMEM((TS, J * CB), BF16), pltpu.VMEM((16, W), F32)],
        compiler_params=_params("arbitrary"),
    )(proj, proj, dcat, xre_all, xim_all, xre_all, xim_all, a8, cl, bbre, bbim, cre, cim, rows, wg, dproj)


_WEIGHTS = ['rel_bias', 'w_ada', 'b_ada', 'w_in', 'conv_a', 'conv_c', 'conv_c_b', 'lru_wa', 'lru_ba', 'lru_wx', 'lru_bx',
            'lru_lambda', 's5_lam_re', 's5_lam_im', 's5_log_dt', 's5_b_re', 's5_b_im', 's5_c_re', 's5_c_im', 's5_d',
            's5_w_glu', 's5_b_glu', 'w_out', 'ln_g', 'ln_b']
_LAYER_SMALL = ['conv_a', 'conv_c', 'conv_c_b', 'lru_wa', 'lru_ba', 'lru_wx', 'lru_bx', 'lru_lambda', 's5_lam_re', 's5_lam_im',
                's5_log_dt', 's5_b_re', 's5_b_im', 's5_c_re', 's5_c_im', 's5_d', 's5_b_glu']
_SMALL = ['rel_bias'] + _LAYER_SMALL + ['ln_g', 'ln_b']


def _t5_bucket(dist):
    max_exact = REL_BUCKETS // 2
    nf = jnp.maximum(dist, 1).astype(F32)
    large = max_exact + (jnp.log(nf / max_exact) / math.log(REL_MAX_DIST / max_exact)
                         * (REL_BUCKETS - max_exact)).astype(jnp.int32)
    large = jnp.minimum(large, REL_BUCKETS - 1)
    return jnp.where(dist < max_exact, dist, large)


def _bias_tables(rel_bias):
    i = jnp.arange(BLK)[:, None]
    j = jnp.arange(2 * BLK)[None, :]
    delta = i + BLK - j
    out = []
    for window, dil in DILATIONS:
        bucket = _t5_bucket(jnp.clip(delta, 0, window // dil) * dil)
        onehot = (bucket[:, :, None] == jnp.arange(REL_BUCKETS)[None, None, :]).astype(F32)
        out.append(jnp.einsum('ijb,bh->hij', onehot, rel_bias, precision=lax.Precision.HIGHEST))
    return jnp.stack(out)


def _prep_layer(q):
    eye8 = jnp.eye(8, dtype=F32)
    G = q['s5_lam_re'].shape[0]
    nblk = G // 8

    def block_diag(w):
        hd = w.shape[1]
        return (w[:, :, None, :] * eye8[:, None, :, None]).reshape(8 * hd, 8 * hd)

    def compact_b(bb):
        t = jnp.transpose(bb.reshape(nblk, 8, S5_STATE, S5_CH), (0, 1, 3, 2))
        return (t[:, :, :, None, :] * eye8[None, :, None, :, None]).reshape(G * S5_CH, 8 * S5_STATE)

    def compact_c(cc):
        t = jnp.transpose(cc.reshape(nblk, 8, S5_CH, S5_STATE), (0, 1, 3, 2))
        return (t[:, :, :, None, :] * eye8[None, :, None, :, None]).reshape(G * S5_STATE, 8 * S5_CH)

    lam_re, lam_im = q['s5_lam_re'], q['s5_lam_im']
    dt = jnp.exp(q['s5_log_dt'])[:, None]
    mag = jnp.exp(lam_re * dt)
    ab_re = mag * jnp.cos(lam_im * dt)
    ab_im = mag * jnp.sin(lam_im * dt)
    den = lam_re * lam_re + lam_im * lam_im
    f_re = ((ab_re - 1.0) * lam_re + ab_im * lam_im) / den
    f_im = (ab_im * lam_re - (ab_re - 1.0) * lam_im) / den
    b_re, b_im = q['s5_b_re'], q['s5_b_im']
    bb_re = f_re[..., None] * b_re - f_im[..., None] * b_im
    bb_im = f_re[..., None] * b_im + f_im[..., None] * b_re
    return dict(
        conv_a=q['conv_a'], conv_c=q['conv_c'], conv_c_b=q['conv_c_b'], lru_ba=q['lru_ba'], lru_bx=q['lru_bx'],
        sp=jax.nn.softplus(-q['lru_lambda']), wa=block_diag(q['lru_wa']), wx=block_diag(q['lru_wx']),
        ar=ab_re.reshape(-1), ai=ab_im.reshape(-1), bbre=compact_b(bb_re), bbim=compact_b(bb_im),
        cre=compact_c(q['s5_c_re']), cim=compact_c(q['s5_c_im']), s5_d=q['s5_d'], s5_b_glu=q['s5_b_glu'])


def _s5_lag_weights(kp):
    J, CB, SB = S5_LAGS, S5_CB, S5_SB
    ar, ai = kp['ar'], kp['ai']
    W = ar.shape[0]
    nblk = W // SB
    cr, ci = jnp.ones_like(ar), jnp.zeros_like(ai)
    pows = []
    for _ in range(J + 1):
        pows.append((cr, ci))
        cr, ci = cr * ar - ci * ai, cr * ai + ci * ar
    pwr = jnp.transpose(jnp.stack([p[0] for p in pows[:J]]).reshape(J, nblk, SB), (1, 0, 2))[:, :, None, :]
    pwi = jnp.transpose(jnp.stack([p[1] for p in pows[:J]]).reshape(J, nblk, SB), (1, 0, 2))[:, :, None, :]

    def lagged(re, im, sign):
        out = jnp.concatenate([re * pwr - sign * im * pwi, sign * re * pwi + im * pwr], axis=-1)
        return out.reshape(nblk, J * CB, 2 * SB).astype(BF16)

    bbl = lagged(kp['bbre'].reshape(nblk, 1, CB, SB), kp['bbim'].reshape(nblk, 1, CB, SB), 1.0)
    c0r = jnp.transpose(kp['cre'].reshape(nblk, SB, CB), (0, 2, 1))[:, None]
    c0i = -jnp.transpose(kp['cim'].reshape(nblk, SB, CB), (0, 2, 1))[:, None]
    cl = lagged(c0r, c0i, -1.0)
    return _rows8([pows[J][0], pows[J][1]], W), bbl, cl


def _rows8(vecs, width):
    rows = [v.reshape(1, width).astype(F32) for v in vecs]
    return jnp.concatenate(rows + [jnp.zeros((8 - len(rows), width), F32)], axis=0)


def kernel(x, c, rel_bias, w_ada, b_ada, w_in, conv_a, conv_c, conv_c_b, lru_wa, lru_ba, lru_wx, lru_bx, lru_lambda, s5_lam_re, s5_lam_im, s5_log_dt, s5_b_re, s5_b_im, s5_c_re, s5_c_im, s5_d, s5_w_glu, s5_b_glu, w_out, ln_g, ln_b, loss_target, m_rel_bias, m_w_ada, m_b_ada, m_w_in, m_conv_a, m_conv_c, m_conv_c_b, m_lru_wa, m_lru_ba, m_lru_wx, m_lru_bx, m_lru_lambda, m_s5_lam_re, m_s5_lam_im, m_s5_log_dt, m_s5_b_re, m_s5_b_im, m_s5_c_re, m_s5_c_im, m_s5_d, m_s5_w_glu, m_s5_b_glu, m_w_out, m_ln_g, m_ln_b, v_rel_bias, v_w_ada, v_b_ada, v_w_in, v_conv_a, v_conv_c, v_conv_c_b, v_lru_wa, v_lru_ba, v_lru_wx, v_lru_bx, v_lru_lambda, v_s5_lam_re, v_s5_lam_im, v_s5_log_dt, v_s5_b_re, v_s5_b_im, v_s5_c_re, v_s5_c_im, v_s5_d, v_s5_w_glu, v_s5_b_glu, v_w_out, v_ln_g, v_ln_b):
    a = dict(locals())
    me = 4 * lax.axis_index("x") + 2 * lax.axis_index("y") + lax.axis_index("c")
    x0, target = x[0], loss_target[0]
    S, D = x0.shape
    BR = D // 4
    NS = w_in.shape[2]
    SH = BR // NDEV
    assert S % (BLK * DILATIONS[-1][1]) == 0 and BR % (8 * S5_CH) == 0

    small = _exchange(_pack([c, conv_a, conv_c]), True, "gather_small")
    per_dev = [_unpack(small[d], [c.shape, conv_a.shape, conv_c.shape]) for d in range(NDEV)]
    c_all = jnp.concatenate([p[0] for p in per_dev], axis=0)
    conv_a_full = jnp.concatenate([p[1] for p in per_dev], axis=2)
    conv_c_full = jnp.concatenate([p[2] for p in per_dev], axis=2)
    me1 = me.reshape(1).astype(jnp.int32)
    w_in_bf, w_out_bf, w_glu_bf = w_in.astype(BF16), w_out.astype(BF16), s5_w_glu.astype(BF16)

    b_cols = lax.dynamic_slice(b_ada, (0, me * NS), (DEPTH, NS)).reshape(DEPTH, 1, NS)
    ada_all = _exchange(_ada_cols(c_all, w_ada, b_cols), True, "gather_ada")
    ada_me = lax.dynamic_index_in_dim(ada_all, me, axis=2, keepdims=False)
    ada_me = jnp.transpose(ada_me, (1, 0, 2)).reshape(DEPTH, 3 * D)
    shift, scale, gate = ada_me[:, :D], ada_me[:, D:2 * D], ada_me[:, 2 * D:]

    bias_tabs, bias_pull = jax.vjp(_bias_tables, rel_bias)
    HD = BR // 8
    head_ones = jnp.kron(jnp.eye(8, dtype=F32), jnp.ones((HD, HD), F32))

    saved = []
    xl = x0
    for l in range(DEPTH):
        q = {n: a[n][l] for n in _LAYER_SMALL}
        q['conv_a'], q['conv_c'] = conv_a_full[l], conv_c_full[l]
        kp, pull = jax.vjp(_prep_layer, q)
        a8, bbl, cl = _s5_lag_weights(kp)
        mod = _rows8([1.0 + scale[l], shift[l]], D)
        arows = _rows8(list(kp['conv_a']), BR)
        lrows = _rows8(list(kp['conv_c']) + [kp['conv_c_b'], kp['lru_ba'], kp['lru_bx'], kp['sp']], BR)
        srows = _rows8([kp['s5_d'], kp['s5_b_glu']], BR)
        orows = _rows8([1.0 + gate[l], ln_g[l], ln_b[l]], D)
        wa, wx = kp['wa'].astype(BF16), kp['wx'].astype(BF16)
        s5w = [kp[n].astype(BF16) for n in ('bbre', 'bbim', 'cre', 'cim')]

        hbf = _modulate(xl, mod)
        proj, w_in_l, w_out_l, w_glu_l = _proj_gather(hbf, w_in_bf[l], [w_out_bf[l], w_glu_bf[l]], me1)
        w_out_l, w_glu_l = w_out_l.reshape(D, D), w_glu_l.reshape(BR, BR)
        ya = _conv_a_fwd(proj, arows, BR)
        os_, ls_ = [], []
        for g, (_, dil) in enumerate(DILATIONS):
            o, lse = _attn_fwd(proj, bias_tabs[g], dil, BR)
            os_.append(o)
            ls_.append(lse)
        yb = _attn_mix_fwd(proj, os_, ls_, BR)
        yc, hs = _lru_fwd(proj, lrows, wa, wx, BR)
        yd, xre, xim = _s5_fwd(proj, a8, bbl, s5w[2], s5w[3], srows, w_glu_l, BR)
        if l < DEPTH - 1:
            xn, y, cat = _out_ln([ya, yb, yc, yd], w_out_l, xl, orows)
        else:
            xn, y, cat, loss_local = _out_ln([ya, yb, yc, yd], w_out_l, xl, orows, target)
        saved.append(dict(x=xl, proj=proj, hbf=hbf, os=os_, ls=ls_, hs=hs, xre=xre, xim=xim, y=y, cat=cat, pull=pull,
                          mod=mod, arows=arows, lrows=lrows, srows=srows, orows=orows, wa=wa, wx=wx, s5w=s5w,
                          qw=(a8, cl), w_in=w_in_l, w_out=w_out_l, w_glu=w_glu_l))
        xl = xn

    dout = xl

    dbias = jnp.zeros_like(bias_tabs)
    lgrads, dada, dw_in_parts, dw_out_parts, dw_glu_parts = [None] * DEPTH, [None] * DEPTH, [None] * DEPTH, [None] * DEPTH, [None] * DEPTH
    for l in reversed(range(DEPTH)):
        sv = saved[l]
        proj = sv['proj']
        dxres, dyb, dcat, lnsum = _ln_bwd(dout, sv['x'], sv['y'], sv['orows'], sv['w_out'])
        dw_out_parts[l] = _dw_scatter(sv['cat'], dyb, D // NDEV, True, me1, "dw_out_scatter")
        dproj, asum = _conv_a_bwd(proj, dcat, sv['arows'], BR)
        dos, es, dbg = _attn_mix_bwd(proj, dcat, sv['os'], sv['ls'], head_ones, BR)
        dqkv, dbs = [], []
        for g, (_, dil) in enumerate(DILATIONS):
            dq_, dk_, dv_, db_ = _attn_bwd(proj, dos[g], es[g], sv['ls'][g], bias_tabs[g], dil, BR)
            dqkv.append((dq_, dk_, dv_))
            dbs.append(db_)
        dbias = dbias + jnp.stack(dbs)
        dproj = _attn_dsum(dqkv, dbg, dproj, BR)
        dproj, lsum, dwa, dwx = _lru_bwd(proj, sv['hs'], dcat, sv['lrows'], sv['wa'], sv['wx'], dproj, BR)
        dproj, ssum, dwg, da_, dbbre, dbbim, dcre, dcim = _s5_bwd(
            proj, dcat, sv['xre'], sv['xim'], *sv['qw'], *sv['s5w'], sv['srows'], sv['w_glu'], dproj, BR)
        dw_glu_parts[l] = dwg.reshape(NDEV, BR // NDEV, BR)
        dout, msum = _dh(dproj, sv['w_in'], dxres, sv['x'], sv['mod'])
        dw_in_parts[l] = _dw_scatter(sv['hbf'], dproj, NS, False, me1, "dw_in_scatter")
        dada[l] = jnp.concatenate([msum[1], msum[0], lnsum[2]])
        dkp = dict(conv_a=asum[0:3], conv_c=lsum[0:4], conv_c_b=lsum[4], lru_ba=lsum[5], lru_bx=lsum[6], sp=lsum[7],
                   wa=dwa, wx=dwx, ar=da_[0], ai=da_[1], bbre=dbbre, bbim=dbbim, cre=dcre, cim=dcim,
                   s5_d=ssum[0], s5_b_glu=ssum[1])
        lgrads[l] = dict(sv['pull'](dkp)[0], ln_g=lnsum[0], ln_b=lnsum[1])
    grad_x = dout[None]

    out = {}

    def put(name, res):
        out[name] = res

    put('w_in', _adamw(dw_in_parts, w_in, m_w_in, v_w_in, "adamw_w_in"))
    put('w_out', _adamw(dw_out_parts, w_out, m_w_out, v_w_out, "adamw_w_out"))
    glu_parts = [_exchange(dw_glu_parts[l], False, "scatter_w_glu") for l in range(DEPTH)]
    put('s5_w_glu', _adamw(glu_parts, s5_w_glu, m_s5_w_glu, v_s5_w_glu, "adamw_w_glu"))

    dada_all = _exchange(jnp.stack(dada), True, "gather_dada")
    dada_cols = jnp.transpose(lax.dynamic_slice(dada_all, (0, 0, me * NS), (NDEV, DEPTH, NS)), (1, 0, 2))
    gw_ada, gb_ada = _ada_grads(c_all, dada_all, dada_cols)
    put('w_ada', _adamw([gw_ada[l] for l in range(DEPTH)], w_ada, m_w_ada, v_w_ada, "adamw_w_ada"))

    local = {'rel_bias': bias_pull(dbias)[0]}
    for n in _LAYER_SMALL + ['ln_g', 'ln_b']:
        local[n] = jnp.stack([lgrads[l][n] for l in range(DEPTH)])
    local['loss'] = loss_local.reshape(1)
    packed = _SMALL + ['loss']
    shapes = [local[n].shape for n in packed]
    summed = _sum8(_exchange(_pack([local[n] for n in packed]), True, "gather_small_grads"), "sum_small_grads")
    gs = dict(zip(packed, _unpack(summed, shapes)))
    loss = gs['loss'][0]
    gs['conv_a'] = lax.dynamic_slice_in_dim(gs['conv_a'], me * SH, SH, axis=2)
    gs['conv_c'] = lax.dynamic_slice_in_dim(gs['conv_c'], me * SH, SH, axis=2)
    gs['b_ada'] = gb_ada
    names = _SMALL + ['b_ada']
    shapes = [a[n].shape for n in names]
    res = _adamw([_pack([gs[n] for n in names])], _pack([a[n] for n in names])[None], _pack([a['m_' + n] for n in names])[None],
                 _pack([a['v_' + n] for n in names])[None], "adamw_small")
    unpacked = [_unpack(r[0], shapes) for r in res]
    for k, n in enumerate(names):
        put(n, tuple(unpacked[j][k] for j in range(4)))

    return (loss, grad_x, *[out[n][0] for n in _WEIGHTS], *[out[n][1] for n in _WEIGHTS],
            *[out[n][2] for n in _WEIGHTS], *[out[n][3] for n in _WEIGHTS])
```

```python
import functools
import math

import jax
import jax.numpy as jnp
from jax import lax
from jax.experimental import pallas as pl
from jax.experimental.pallas import tpu as pltpu

F32 = jnp.float32
BF16 = jnp.bfloat16

NDEV = 8
DEPTH = 2
BLK = 128
DILATIONS = ((128, 1), (512, 4), (2048, 16))
REL_BUCKETS = 32
REL_MAX_DIST = 2048
LRU_C = 8.0
S5_CH = 16
S5_STATE = 64
ALPHA = (2 * DEPTH) ** 0.25
LN_EPS = 1e-5
ADAM_LR, ADAM_B1, ADAM_B2, ADAM_EPS, ADAM_WD, ADAM_STEP = 0.001, 0.9, 0.999, 1e-08, 0.01, 10
NEG = -1e30
VMEM_LIMIT = 56 * 1024 * 1024


def _call(body, **kw):
    return pl.pallas_call(body, **kw)


def _params(*sem):
    return pltpu.CompilerParams(dimension_semantics=sem, vmem_limit_bytes=VMEM_LIMIT)


def _sigmoid(x):
    return 1.0 / (1.0 + jnp.exp(-x))


def _silu(x):
    return x * _sigmoid(x)


def _dsilu(x):
    s = _sigmoid(x)
    return s * (1.0 + x * (1.0 - s))


_GC = math.sqrt(2.0 / math.pi)


def _gelu(x):
    return 0.5 * x * (1.0 + jnp.tanh(_GC * (x + 0.044715 * x * x * x)))


def _dgelu(x):
    t = jnp.tanh(_GC * (x + 0.044715 * x * x * x))
    return 0.5 * (1.0 + t) + 0.5 * x * (1.0 - t * t) * _GC * (1.0 + 3.0 * 0.044715 * x * x)


def _expm1(x):
    series = x * (1.0 + x * (1.0 / 2) * (1.0 + x * (1.0 / 3) * (1.0 + x * (1.0 / 4) * (1.0 + x * (1.0 / 5) * (1.0 + x * (1.0 / 6))))))
    return jnp.where(jnp.abs(x) < 0.25, series, jnp.exp(x) - 1.0)


def _dot(a, b):
    return jnp.dot(a, b, preferred_element_type=F32)


def _dot_nt(a, b):
    return lax.dot_general(a, b, (((1,), (1,)), ((), ())), preferred_element_type=F32)


def _dot_tn(a, b):
    return lax.dot_general(a, b, (((0,), (0,)), ((), ())), preferred_element_type=F32)


def _colsum(v):
    return jnp.sum(v, axis=0, keepdims=True)


def _all_gather(x, name):
    def body(x_ref, o_ref, send_sems, recv_sems, local_sem):
        ix, iy, ic = lax.axis_index("x"), lax.axis_index("y"), lax.axis_index("c")
        me, sibling = (ix, iy, ic), (ix, iy, 1 - ic)
        chips = [(1 - ix, iy), (ix, 1 - iy), (1 - ix, 1 - iy)]

        def slot(px, py, pc):
            return o_ref.at[4 * px + 2 * py + pc]

        def copy(k, block, to, src=None):
            return pltpu.make_async_remote_copy(
                src_ref=slot(*block) if src is None else src, dst_ref=slot(*block), send_sem=send_sems.at[k],
                recv_sem=recv_sems.at[k], device_id=to, device_id_type=pl.DeviceIdType.MESH)

        mine = pltpu.make_async_copy(x_ref, slot(*me), local_sem)
        mine.start()
        first = [copy(0, me, sibling, src=x_ref)] + [copy(1 + j, me, (*chip, ic), src=x_ref) for j, chip in enumerate(chips)]
        for cp in first:
            cp.start()
        passed = [copy(4 + j, (*chip, ic), sibling) for j, chip in enumerate(chips)]
        for j, chip in enumerate(chips):
            copy(1 + j, (*chip, ic), me).wait_recv()
            passed[j].start()
        copy(0, sibling, me).wait_recv()
        for j, chip in enumerate(chips):
            copy(4 + j, (*chip, 1 - ic), me).wait_recv()
        for cp in first + passed:
            cp.wait_send()
        mine.wait()

    return _call(
        body, name=name,
        out_shape=jax.ShapeDtypeStruct((NDEV,) + x.shape, x.dtype),
        in_specs=[pl.BlockSpec(memory_space=pl.ANY)],
        out_specs=pl.BlockSpec(memory_space=pl.ANY),
        scratch_shapes=[pltpu.SemaphoreType.DMA((NDEV - 1,)), pltpu.SemaphoreType.DMA((NDEV - 1,)), pltpu.SemaphoreType.DMA],
    )(x)


def _pack(arrs):
    parts = []
    for a in arrs:
        n = math.prod(a.shape)
        parts.append(jnp.pad(a.reshape(-1).astype(F32), (0, _pack_rows(n) * 128 - n)).reshape(-1, 128))
    rows = sum(p.shape[0] for p in parts)
    total = rows if rows <= 512 else -(-rows // 512) * 512
    if total > rows:
        parts.append(jnp.zeros((total - rows, 128), F32))
    return jnp.concatenate(parts, axis=0)


def _pack_rows(n):
    return -(-n // 1024) * 8


def _unpack(packed, shapes):
    out, row = [], 0
    for s in shapes:
        n = math.prod(s)
        rows = _pack_rows(n)
        out.append(packed[row:row + rows].reshape(-1)[:n].reshape(s))
        row += rows
    return out


def _sum8(parts, name):
    _, R, C = parts.shape
    TR = R
    for cand in (512, 256, 128, 64, 32, 16, 8):
        if R % cand == 0:
            TR = cand
            break

    def body(p_ref, o_ref):
        acc = p_ref[0]
        for d in range(1, NDEV):
            acc = acc + p_ref[d]
        o_ref[...] = acc

    return _call(
        body, name=name, grid=(R // TR,),
        in_specs=[pl.BlockSpec((NDEV, TR, C), lambda i: (0, i, 0))],
        out_specs=pl.BlockSpec((TR, C), lambda i: (i, 0)),
        out_shape=jax.ShapeDtypeStruct((R, C), F32),
        compiler_params=_params("arbitrary"),
    )(parts)


def _adamw(g_layers, w, m, v, name):
    L, R, C = w.shape
    g0 = g_layers[0]
    nparts = g0.shape[0] if g0.ndim == 3 else 0
    TR = R
    for cand in (256, 128, 64, 32, 16):
        if R % cand == 0 and cand * C <= 128 * 1024:
            TR = cand
            break

    def body(*refs):
        g_refs = refs[:L]
        w_ref, m_ref, v_ref, go_ref, d_ref, mo_ref, vo_ref = refs[L:]
        for ll in range(L):
            @pl.when(pl.program_id(0) == ll)
            def _(g_ref=g_refs[ll]):
                if nparts:
                    g = g_ref[0].astype(F32)
                    for d in range(1, nparts):
                        g = g + g_ref[d].astype(F32)
                else:
                    g = g_ref[...]
                m2 = ADAM_B1 * m_ref[0] + (1.0 - ADAM_B1) * g
                v2 = ADAM_B2 * v_ref[0] + (1.0 - ADAM_B2) * (g * g)
                m_hat = m2 / (1.0 - ADAM_B1 ** ADAM_STEP)
                v_hat = v2 / (1.0 - ADAM_B2 ** ADAM_STEP)
                go_ref[0] = g
                d_ref[0] = -ADAM_LR * (m_hat / (jnp.sqrt(v_hat) + ADAM_EPS) + ADAM_WD * w_ref[0])
                mo_ref[0] = m2
                vo_ref[0] = v2

    def gspec(ll):
        if nparts:
            return pl.BlockSpec((nparts, TR, C), lambda l, i: (0, jnp.where(l == ll, i, 0), 0))
        return pl.BlockSpec((TR, C), lambda l, i: (jnp.where(l == ll, i, 0), 0))

    spec = pl.BlockSpec((1, TR, C), lambda l, i: (l, i, 0))
    return _call(
        body, name=name, grid=(L, R // TR),
        in_specs=[gspec(ll) for ll in range(L)] + [spec, spec, spec], out_specs=[spec] * 4,
        out_shape=[jax.ShapeDtypeStruct((L, R, C), F32)] * 4,
        compiler_params=_params("arbitrary", "arbitrary"),
    )(*g_layers, w, m, v)


def _ada_cols(c_all, w_ada, b_cols):
    L, D, NS = w_ada.shape

    def body(c_ref, w_ref, b_ref, o_ref):
        cond = _silu(c_ref[...]).astype(BF16)
        o_ref[0] = _dot(cond, w_ref[0].astype(BF16)) + b_ref[0]

    return _call(
        body, name="ada_cols", grid=(L,),
        in_specs=[pl.BlockSpec((NDEV, D), lambda l: (0, 0)), pl.BlockSpec((1, D, NS), lambda l: (l, 0, 0)),
                  pl.BlockSpec((1, 1, NS), lambda l: (l, 0, 0))],
        out_specs=pl.BlockSpec((1, NDEV, NS), lambda l: (l, 0, 0)),
        out_shape=jax.ShapeDtypeStruct((L, NDEV, NS), F32),
        compiler_params=_params("arbitrary"),
    )(c_all, w_ada, b_cols)


def _ada_grads(c_all, dada_all, dada_cols):
    _, D = c_all.shape
    L, _, NS = dada_cols.shape
    D3 = dada_all.shape[2]

    def body(c_ref, da_ref, dc_ref, gw_ref, gb_ref):
        cond = _silu(c_ref[...]).astype(BF16)
        gw_ref[0] = _dot_tn(cond, dc_ref[0].astype(BF16))
        acc = da_ref[0]
        for d in range(1, NDEV):
            acc = acc + da_ref[d]
        gb_ref[...] = acc

    return _call(
        body, name="ada_grads", grid=(L,),
        in_specs=[pl.BlockSpec((NDEV, D), lambda l: (0, 0)), pl.BlockSpec((NDEV, L, D3), lambda l: (0, 0, 0)),
                  pl.BlockSpec((1, NDEV, NS), lambda l: (l, 0, 0))],
        out_specs=[pl.BlockSpec((1, D, NS), lambda l: (l, 0, 0)), pl.BlockSpec((L, D3), lambda l: (0, 0))],
        out_shape=[jax.ShapeDtypeStruct((L, D, NS), F32), jax.ShapeDtypeStruct((L, D3), F32)],
        compiler_params=_params("arbitrary"),
    )(c_all, dada_all, dada_cols)


def _tile(S, want):
    return want if S % want == 0 else S


def _modulate(x, mod):
    S, D = x.shape
    TS = _tile(S, 512)

    def body(x_ref, mod_ref, h_ref):
        h_ref[...] = (x_ref[...] * mod_ref[0:1, :] + mod_ref[1:2, :]).astype(BF16)

    spec = pl.BlockSpec((TS, D), lambda i: (i, 0))
    return _call(
        body, name="modulate", grid=(S // TS,),
        in_specs=[spec, pl.BlockSpec((8, D), lambda i: (0, 0))], out_specs=spec,
        out_shape=jax.ShapeDtypeStruct((S, D), BF16),
        compiler_params=_params("arbitrary"),
    )(x, mod)


def _peer(k):
    ix, iy, ic = lax.axis_index("x"), lax.axis_index("y"), lax.axis_index("c")
    bx, by, bc = (k >> 2) & 1, (k >> 1) & 1, k & 1
    px, py, pc = ix + bx - 2 * ix * bx, iy + by - 2 * iy * by, ic + bc - 2 * ic * bc
    return (px, py, pc), 4 * px + 2 * py + pc


GATHER_ORDER = (0, 1, 2, 4, 3, 5, 6, 7)
SCATTER_ORDER = (2, 4, 6, 3, 5, 7, 1, 0)


def _offset(order, k):
    off = jnp.int32(order[-1])
    for step in reversed(range(len(order) - 1)):
        off = jnp.where(k == step, jnp.int32(order[step]), off)
    return off


def _proj_gather(h, w_shard, extras, me1):
    assert GATHER_ORDER[0] == 0
    S, D = h.shape
    NS = w_shard.shape[1]
    TS = _tile(S, 512)
    nt = S // TS
    ne = len(extras)

    def body(me_ref, h_ref, w_ref, *rest):
        x_refs, o_ref, wg_ref = rest[:ne], rest[ne], rest[ne + 1]
        xg_refs = rest[ne + 2:2 * ne + 2]
        wbuf, send_sems, recv_sems, local_sems, load_sems = rest[2 * ne + 2:]
        k, i = pl.program_id(0), pl.program_id(1)
        me = me_ref[0]
        off = _offset(GATHER_ORDER, k)

        def push(src, dst, which, kk):
            peer, _ = _peer(kk)
            return pltpu.make_async_remote_copy(
                src_ref=src, dst_ref=dst.at[me], send_sem=send_sems.at[which, kk], recv_sem=recv_sems.at[which, kk],
                device_id=peer, device_id_type=pl.DeviceIdType.MESH)

        def own(src, dst, which):
            return pltpu.make_async_copy(src, dst.at[me], local_sems.at[which])

        pairs = [(w_ref, wg_ref)] + list(zip(x_refs, xg_refs))

        @pl.when(jnp.logical_and(k == 0, i == 0))
        def _():
            for which, (src, dst) in enumerate(pairs):
                own(src, dst, which).start()
                for kk in GATHER_ORDER[1:]:
                    push(src, dst, which, kk).start()
            first = pltpu.make_async_copy(w_ref, wbuf.at[0], load_sems.at[0])
            first.start()
            first.wait()

        @pl.when(jnp.logical_and(k > 0, i == 0))
        def _():
            push(w_ref, wg_ref, 0, off).wait_recv()
            _, blk = _peer(off)
            load = pltpu.make_async_copy(wg_ref.at[blk], wbuf.at[k % 2], load_sems.at[k % 2])
            load.start()
            load.wait()

        o_ref[...] = _dot(h_ref[...], wbuf[k % 2])

        @pl.when(jnp.logical_and(k == NDEV - 1, i == nt - 1))
        def _():
            for which, (src, dst) in enumerate(pairs):
                for kk in range(1, NDEV):
                    cp = push(src, dst, which, kk)
                    cp.wait_send()
                    if which > 0:
                        cp.wait_recv()
                own(src, dst, which).wait()

    def col(k, i, me_ref):
        m, off = me_ref[0], _offset(GATHER_ORDER, k)
        return i, (m | off) - (m & off)

    anyspec = pl.BlockSpec(memory_space=pl.ANY)
    grid_spec = pltpu.PrefetchScalarGridSpec(
        num_scalar_prefetch=1, grid=(NDEV, nt),
        in_specs=[pl.BlockSpec((TS, D), lambda k, i, me_ref: (i, 0)), anyspec] + [anyspec] * ne,
        out_specs=[pl.BlockSpec((TS, NS), col), anyspec] + [anyspec] * ne,
        scratch_shapes=[pltpu.VMEM((2, D, NS), BF16), pltpu.SemaphoreType.DMA((1 + ne, NDEV)), pltpu.SemaphoreType.DMA((1 + ne, NDEV)),
                        pltpu.SemaphoreType.DMA((1 + ne,)), pltpu.SemaphoreType.DMA((2,))])
    return _call(
        body, name="proj_gather", grid_spec=grid_spec,
        out_shape=[jax.ShapeDtypeStruct((S, NDEV * NS), F32), jax.ShapeDtypeStruct((NDEV, D, NS), BF16)]
        + [jax.ShapeDtypeStruct((NDEV,) + e.shape, e.dtype) for e in extras],
        compiler_params=_params("arbitrary", "arbitrary"),
    )(me1, h, w_shard, *extras)


def _dw_scatter(a, b, blk, by_rows, me1, name):
    assert SCATTER_ORDER[-1] == 0
    S = a.shape[0]
    TK = _tile(S, 512)
    nk = S // TK
    shape = (blk, b.shape[1]) if by_rows else (a.shape[1], blk)
    last = NDEV - 1

    def body(me_ref, a_ref, b_ref, recv_ref, acc, stage, send_sems, recv_sems, local_sem):
        k, kk = pl.program_id(0), pl.program_id(1)
        me = me_ref[0]

        def push(off, slot):
            peer, _ = _peer(off)
            return pltpu.make_async_remote_copy(
                src_ref=stage.at[slot], dst_ref=recv_ref.at[me], send_sem=send_sems.at[off], recv_sem=recv_sems.at[off],
                device_id=peer, device_id_type=pl.DeviceIdType.MESH)

        own = pltpu.make_async_copy(stage.at[last % 2], recv_ref.at[me], local_sem)

        @pl.when(kk == 0)
        def _():
            acc[...] = jnp.zeros_like(acc)
        acc[...] += _dot_tn(a_ref[...], b_ref[...])

        @pl.when(kk == nk - 1)
        def _():
            @pl.when(k >= 2)
            def _():
                push(_offset(SCATTER_ORDER, k - 2), k % 2).wait_send()

            stage[k % 2] = acc[...].astype(BF16)

            @pl.when(k < last)
            def _():
                push(_offset(SCATTER_ORDER, k), k % 2).start()

            @pl.when(k == last)
            def _():
                own.start()
                push(SCATTER_ORDER[last - 1], (last - 1) % 2).wait_send()
                own.wait()
                for off in range(1, NDEV):
                    push(off, 0).wait_recv()

    def blk_of(k, me_ref):
        m, off = me_ref[0], _offset(SCATTER_ORDER, k)
        return (m | off) - (m & off)

    if by_rows:
        in_specs = [pl.BlockSpec((TK, blk), lambda k, kk, me_ref: (kk, blk_of(k, me_ref))),
                    pl.BlockSpec((TK, b.shape[1]), lambda k, kk, me_ref: (kk, 0))]
    else:
        in_specs = [pl.BlockSpec((TK, a.shape[1]), lambda k, kk, me_ref: (kk, 0)),
                    pl.BlockSpec((TK, blk), lambda k, kk, me_ref: (kk, blk_of(k, me_ref)))]
    grid_spec = pltpu.PrefetchScalarGridSpec(
        num_scalar_prefetch=1, grid=(NDEV, nk), in_specs=in_specs,
        out_specs=pl.BlockSpec(memory_space=pl.ANY),
        scratch_shapes=[pltpu.VMEM(shape, F32), pltpu.VMEM((2,) + shape, BF16), pltpu.SemaphoreType.DMA((NDEV,)),
                        pltpu.SemaphoreType.DMA((NDEV,)), pltpu.SemaphoreType.DMA])
    return _call(
        body, name=name, grid_spec=grid_spec,
        out_shape=jax.ShapeDtypeStruct((NDEV,) + shape, BF16),
        compiler_params=_params("arbitrary", "arbitrary"),
    )(me1, a, b)


def _chunk(TS, BR, k):
    return pl.BlockSpec((TS, BR), lambda i: (i, k))


def _halo_prev(TS, BR, k):
    return pl.BlockSpec((8, BR), lambda i: (jnp.maximum(i * (TS // 8) - 1, 0), k))


def _conv_a_fwd(proj, w8, BR):
    S = proj.shape[0]
    TS = _tile(S, 512)

    def body(ab_ref, ac_ref, ax_ref, ag_ref, hc_ref, hx_ref, w_ref, o_ref, ext):
        i = pl.program_id(0)
        u = ac_ref[...] * ax_ref[...]
        ext[0:8, :] = jnp.where(i > 0, hc_ref[...] * hx_ref[...], 0.0)
        ext[8:8 + TS, :] = u
        conv = ext[pl.ds(6, TS), :] * w_ref[0:1, :] + ext[pl.ds(7, TS), :] * w_ref[1:2, :] + u * w_ref[2:3, :]
        o_ref[...] = ab_ref[...] * conv * _silu(ag_ref[...])

    return _call(
        body, name="conv_a_fwd", grid=(S // TS,),
        in_specs=[_chunk(TS, BR, 0), _chunk(TS, BR, 1), _chunk(TS, BR, 2), _chunk(TS, BR, 3),
                  _halo_prev(TS, BR, 1), _halo_prev(TS, BR, 2), pl.BlockSpec((8, BR), lambda i: (0, 0))],
        out_specs=pl.BlockSpec((TS, BR), lambda i: (i, 0)),
        out_shape=jax.ShapeDtypeStruct((S, BR), F32),
        scratch_shapes=[pltpu.VMEM((TS + 8, BR), F32)],
        compiler_params=_params("arbitrary"),
    )(proj, proj, proj, proj, proj, proj, w8)


ATT_UNIT = 2048
ATT_LANES = 128


def _attn_mask():
    i = lax.broadcasted_iota(jnp.int32, (BLK, 2 * BLK), 0)
    j = lax.broadcasted_iota(jnp.int32, (BLK, 2 * BLK), 1)
    return jnp.logical_and(j >= i, j <= i + BLK), j >= BLK


def _attn_rows(n, r, dil):
    return pl.ds(n * BLK * dil + r, BLK, stride=dil) if dil > 1 else pl.ds(n * BLK, BLK)


def _attn_geometry(S, BR, dil):
    HD = BR // 8
    U = ATT_UNIT
    assert S % U == 0 and BR % ATT_LANES == 0 and ATT_LANES % HD == 0 and U % (BLK * dil) == 0
    return HD, U, S // U, U // (BLK * dil), ATT_LANES // HD, BR // ATT_LANES


def _attn_fwd(proj, bias, dil, BR):
    S = proj.shape[0]
    HD, U, NU, nbu, hpb, HBK = _attn_geometry(S, BR, dil)
    scale = HD ** -0.5
    combos = [(n, r) for n in range(nbu) for r in range(dil)]

    def body(q_ref, kc_ref, kp_ref, vc_ref, vp_ref, b_ref, o_ref, l_ref, sbuf, pbuf):
        m = pl.program_id(1)
        band, is_cur = _attn_mask()
        first = jnp.logical_and(band, jnp.logical_or(is_cur, m > 0))

        def keys(ref_c, ref_p, n, r):
            prev = ref_c[_attn_rows(n - 1, r, dil), :] if n > 0 else ref_p[_attn_rows(nbu - 1, r, dil), :]
            return jnp.concatenate([prev, ref_c[_attn_rows(n, r, dil), :]], axis=0).astype(BF16)

        for c, (n, r) in enumerate(combos):
            q = (q_ref[_attn_rows(n, r, dil), :] * scale).astype(BF16)
            kk = keys(kc_ref, kp_ref, n, r)
            for h in range(hpb):
                sl = slice(h * HD, (h + 1) * HD)
                sbuf[c * hpb + h] = _dot_nt(q[:, sl], kk[:, sl])
        for c, (n, r) in enumerate(combos):
            lses = []
            for h in range(hpb):
                s = jnp.where(band if n > 0 else first, sbuf[c * hpb + h] + b_ref[h], NEG)
                mx = jnp.max(s, axis=-1, keepdims=True)
                p = jnp.exp(s - mx)
                l = jnp.sum(p, axis=-1, keepdims=True)
                pbuf[c * hpb + h] = (p * (1.0 / l)).astype(BF16)
                lses.append(jnp.broadcast_to(mx + jnp.log(l), (BLK, HD)))
            l_ref[_attn_rows(n, r, dil), :] = jnp.concatenate(lses, axis=1)
        for c, (n, r) in enumerate(combos):
            vv = keys(vc_ref, vp_ref, n, r)
            o_ref[_attn_rows(n, r, dil), :] = jnp.concatenate(
                [_dot(pbuf[c * hpb + h], vv[:, h * HD:(h + 1) * HD]) for h in range(hpb)], axis=1)

    def cur(c):
        return pl.BlockSpec((U, ATT_LANES), lambda hb, m: (m, c * HBK + hb))

    def prev(c):
        return pl.BlockSpec((U, ATT_LANES), lambda hb, m: (jnp.maximum(m - 1, 0), c * HBK + hb))

    ospec = pl.BlockSpec((U, ATT_LANES), lambda hb, m: (m, hb))
    nhb = len(combos) * hpb
    return _call(
        body, name="attn_fwd_d%d" % dil, grid=(HBK, NU),
        in_specs=[cur(4), cur(5), prev(5), cur(6), prev(6), pl.BlockSpec((hpb, BLK, 2 * BLK), lambda hb, m: (hb, 0, 0))],
        out_specs=[ospec, ospec],
        out_shape=[jax.ShapeDtypeStruct((S, BR), F32)] * 2,
        scratch_shapes=[pltpu.VMEM((nhb, BLK, 2 * BLK), F32), pltpu.VMEM((nhb, BLK, 2 * BLK), BF16)],
        compiler_params=_params("arbitrary", "arbitrary"),
    )(proj, proj, proj, proj, proj, bias)


def _mix_weights(l1, l2, l3):
    m = jnp.maximum(jnp.maximum(l1, l2), l3)
    e1, e2, e3 = jnp.exp(l1 - m), jnp.exp(l2 - m), jnp.exp(l3 - m)
    inv = 1.0 / (e1 + e2 + e3)
    return e1 * inv, e2 * inv, e3 * inv


def _attn_mix_fwd(proj, os_, ls_, BR):
    S = proj.shape[0]
    TS = _tile(S, 512)

    def body(o1, o2, o3, l1, l2, l3, g_ref, y_ref):
        w1, w2, w3 = _mix_weights(l1[...], l2[...], l3[...])
        y_ref[...] = (w1 * o1[...] + w2 * o2[...] + w3 * o3[...]) * _silu(g_ref[...])

    spec = pl.BlockSpec((TS, BR), lambda i: (i, 0))
    return _call(
        body, name="attn_mix_fwd", grid=(S // TS,),
        in_specs=[spec] * 6 + [_chunk(TS, BR, 7)], out_specs=spec,
        out_shape=jax.ShapeDtypeStruct((S, BR), F32),
        compiler_params=_params("arbitrary"),
    )(*os_, *ls_, proj)


def _lru_gates(xc, wa, wx, ba, bx, sp):
    xb = xc.astype(BF16)
    r = _sigmoid(_dot(xb, wa) + ba)
    ig = _sigmoid(_dot(xb, wx) + bx)
    la = -LRU_C * r * sp
    a = jnp.exp(la)
    mult = jnp.sqrt(-_expm1(2.0 * la))
    return r, ig, a, mult


def _lru_fwd(proj, rows, wa, wx, BR):
    S = proj.shape[0]
    TS = _tile(S, 256)
    PAD = TS // 2

    def body(cx_ref, hx_ref, cg_ref, r_ref, wa_ref, wx_ref, y_ref, h_ref, ext, sa, sb, carry):
        i = pl.program_id(0)

        @pl.when(i == 0)
        def _():
            sa[0:PAD, :] = jnp.ones((PAD, BR), F32)
            sb[0:PAD, :] = jnp.zeros((PAD, BR), F32)
            carry[...] = jnp.zeros_like(carry)

        cx = cx_ref[...]
        ext[0:8, :] = jnp.where(i > 0, hx_ref[...], 0.0)
        ext[8:8 + TS, :] = cx
        xc = (ext[pl.ds(5, TS), :] * r_ref[0:1, :] + ext[pl.ds(6, TS), :] * r_ref[1:2, :]
              + ext[pl.ds(7, TS), :] * r_ref[2:3, :] + cx * r_ref[3:4, :] + r_ref[4:5, :])
        _, ig, a, mult = _lru_gates(xc, wa_ref[...], wx_ref[...], r_ref[5:6, :], r_ref[6:7, :], r_ref[7:8, :])
        sa[PAD:PAD + TS, :] = a
        sb[PAD:PAD + TS, :] = mult * ig * xc
        d = 1
        while d < TS:
            A, B = sa[PAD:PAD + TS, :], sb[PAD:PAD + TS, :]
            As, Bs = sa[pl.ds(PAD - d, TS), :], sb[pl.ds(PAD - d, TS), :]
            sb[PAD:PAD + TS, :] = A * Bs + B
            sa[PAD:PAD + TS, :] = A * As
            d *= 2
        h = sb[PAD:PAD + TS, :] + sa[PAD:PAD + TS, :] * carry[0:1, :]
        carry[0:1, :] = h[TS - 1:TS, :]
        h_ref[...] = h
        y_ref[...] = h * _silu(cg_ref[...])

    full = pl.BlockSpec((BR, BR), lambda i: (0, 0))
    spec = pl.BlockSpec((TS, BR), lambda i: (i, 0))
    return _call(
        body, name="lru_fwd", grid=(S // TS,),
        in_specs=[_chunk(TS, BR, 8), _halo_prev(TS, BR, 8), _chunk(TS, BR, 9), pl.BlockSpec((8, BR), lambda i: (0, 0)), full, full],
        out_specs=[spec, spec],
        out_shape=[jax.ShapeDtypeStruct((S, BR), F32)] * 2,
        scratch_shapes=[pltpu.VMEM((TS + 8, BR), F32), pltpu.VMEM((PAD + TS, BR), F32), pltpu.VMEM((PAD + TS, BR), F32),
                        pltpu.VMEM((8, BR), F32)],
        compiler_params=_params("arbitrary"),
    )(proj, proj, proj, rows, wa, wx)


S5_LAGS = 8
S5_CB = 8 * S5_CH
S5_SB = 8 * S5_STATE


def _s5_fwd(proj, a8, bbl, cre, cim, rows, wg, BR):
    S = proj.shape[0]
    W = a8.shape[1]
    TS = _tile(S, 256)
    CB, SB, J = S5_CB, S5_SB, S5_LAGS
    nblk = BR // CB

    def body(u_ref, uh_ref, dg_ref, a8_ref, bbl_ref, cre_ref, cim_ref, r_ref, wg_ref,
             y_ref, xre_ref, xim_ref, uext, ulag, sre, sim, ypre_s, carry):
        i = pl.program_id(0)

        @pl.when(i == 0)
        def _():
            carry[...] = jnp.zeros_like(carry)

        uext[0:8, :] = jnp.where(i > 0, uh_ref[...], 0.0)
        uext[8:8 + TS, :] = u_ref[...]
        for m in range(nblk):
            cs, ws = slice(m * CB, (m + 1) * CB), slice(m * SB, (m + 1) * SB)
            for j in range(J):
                ulag[:, j * CB:(j + 1) * CB] = uext[pl.ds(8 - j, TS), cs].astype(BF16)
            w = _dot(ulag[...], bbl_ref[m])
            sre[...] = w[:, 0:SB]
            sim[...] = w[:, SB:2 * SB]
            ar = jnp.broadcast_to(a8_ref[0:1, ws], (8, SB))
            ai = jnp.broadcast_to(a8_ref[1:2, ws], (8, SB))
            xr, xi = carry[0:8, ws], carry[8:16, ws]
            for g in range(TS // 8):
                rg = slice(8 * g, 8 * g + 8)
                xr, xi = sre[rg, :] + ar * xr - ai * xi, sim[rg, :] + ar * xi + ai * xr
                sre[rg, :] = xr
                sim[rg, :] = xi
            carry[0:8, ws] = xr
            carry[8:16, ws] = xi
            xre, xim = sre[...], sim[...]
            xre_ref[:, ws] = xre
            xim_ref[:, ws] = xim
            ypre_s[:, cs] = _dot(xre.astype(BF16), cre_ref[ws, :]) - _dot(xim.astype(BF16), cim_ref[ws, :])
        dg = dg_ref[...]
        yg = _gelu(ypre_s[...] + r_ref[0:1, :] * u_ref[...])
        s = _sigmoid(_dot(yg.astype(BF16), wg_ref[...]) + r_ref[1:2, :])
        y_ref[...] = yg * s * _silu(dg)

    def const(shape):
        return pl.BlockSpec(shape, lambda i: (0,) * len(shape))

    return _call(
        body, name="s5_fwd", grid=(S // TS,),
        in_specs=[_chunk(TS, BR, 10), _halo_prev(TS, BR, 10), _chunk(TS, BR, 11), const((8, W)), const((nblk, J * CB, 2 * SB)),
                  const((W, CB)), const((W, CB)), const((8, BR)), const((BR, BR))],
        out_specs=[pl.BlockSpec((TS, BR), lambda i: (i, 0)), pl.BlockSpec((TS, W), lambda i: (i, 0)), pl.BlockSpec((TS, W), lambda i: (i, 0))],
        out_shape=[jax.ShapeDtypeStruct((S, BR), F32), jax.ShapeDtypeStruct((S, W), F32), jax.ShapeDtypeStruct((S, W), F32)],
        scratch_shapes=[pltpu.VMEM((TS + 8, BR), F32), pltpu.VMEM((TS, J * CB), BF16), pltpu.VMEM((TS, SB), F32), pltpu.VMEM((TS, SB), F32),
                        pltpu.VMEM((TS, BR), F32), pltpu.VMEM((16, W), F32)],
        compiler_params=_params("arbitrary"),
    )(proj, proj, proj, a8, bbl, cre, cim, rows, wg)


def _out_ln(ys, w_out, x, rows, target=None):
    S, D = x.shape
    BR = D // 4
    TS = _tile(S, 256)
    last = target is not None

    def body(ya, yb, yc, yd, w_ref, x_ref, r_ref, *rest):
        t_ref = rest[0] if last else None
        o_ref, y_ref, cat_ref = rest[1:4] if last else rest[0:3]
        cat = jnp.concatenate([ya[...], yb[...], yc[...], yd[...]], axis=1).astype(BF16)
        y = _dot(cat, w_ref[...])
        z = ALPHA * x_ref[...] + r_ref[0:1, :] * y
        mu = jnp.mean(z, axis=-1, keepdims=True)
        zc = z - mu
        var = jnp.mean(zc * zc, axis=-1, keepdims=True)
        xn = zc * lax.rsqrt(var + LN_EPS) * r_ref[1:2, :] + r_ref[2:3, :]
        y_ref[...] = y
        cat_ref[...] = cat
        if last:
            l_ref = rest[4]

            @pl.when(pl.program_id(0) == 0)
            def _():
                l_ref[...] = jnp.zeros_like(l_ref)
            diff = xn - t_ref[...]
            o_ref[...] = diff * (1.0 / D)
            l_ref[...] += jnp.sum(_colsum(diff * diff), axis=1, keepdims=True) * (0.5 / D)
        else:
            o_ref[...] = xn

    yspec = pl.BlockSpec((TS, BR), lambda i: (i, 0))
    spec = pl.BlockSpec((TS, D), lambda i: (i, 0))
    lspec = pl.BlockSpec((1, 1), lambda i: (0, 0))
    return _call(
        body, name="out_ln_loss" if last else "out_ln", grid=(S // TS,),
        in_specs=[yspec] * 4 + [pl.BlockSpec((D, D), lambda i: (0, 0)), spec, pl.BlockSpec((8, D), lambda i: (0, 0))] + [spec] * last,
        out_specs=[spec, spec, spec] + [lspec] * last,
        out_shape=[jax.ShapeDtypeStruct((S, D), F32), jax.ShapeDtypeStruct((S, D), F32), jax.ShapeDtypeStruct((S, D), BF16)]
        + [jax.ShapeDtypeStruct((1, 1), F32)] * last,
        compiler_params=_params("arbitrary"),
    )(*ys, w_out, x, rows, *([target] if last else []))


def _ln_bwd(dout, x, y, rows, w_out):
    S, D = x.shape
    TS = _tile(S, 256)

    def body(do_ref, x_ref, y_ref, r_ref, w_ref, dxr_ref, dyb_ref, dcat_ref, acc_ref):
        @pl.when(pl.program_id(0) == 0)
        def _():
            acc_ref[...] = jnp.zeros_like(acc_ref)
        g1, lg = r_ref[0:1, :], r_ref[1:2, :]
        yv = y_ref[...]
        z = ALPHA * x_ref[...] + g1 * yv
        mu = jnp.mean(z, axis=-1, keepdims=True)
        zc = z - mu
        var = jnp.mean(zc * zc, axis=-1, keepdims=True)
        rstd = lax.rsqrt(var + LN_EPS)
        xhat = zc * rstd
        do = do_ref[...]
        dxh = do * lg
        dz = rstd * (dxh - jnp.mean(dxh, axis=-1, keepdims=True) - xhat * jnp.mean(dxh * xhat, axis=-1, keepdims=True))
        dxr_ref[...] = ALPHA * dz
        dyb = (g1 * dz).astype(BF16)
        dyb_ref[...] = dyb
        dcat_ref[...] = _dot_nt(dyb, w_ref[...])
        acc_ref[0:1, :] += _colsum(do * xhat)
        acc_ref[1:2, :] += _colsum(do)
        acc_ref[2:3, :] += _colsum(dz * yv)

    spec = pl.BlockSpec((TS, D), lambda i: (i, 0))
    rspec = pl.BlockSpec((8, D), lambda i: (0, 0))
    return _call(
        body, name="ln_bwd", grid=(S // TS,),
        in_specs=[spec, spec, spec, rspec, pl.BlockSpec((D, D), lambda i: (0, 0))],
        out_specs=[spec, spec, spec, rspec],
        out_shape=[jax.ShapeDtypeStruct((S, D), F32), jax.ShapeDtypeStruct((S, D), BF16), jax.ShapeDtypeStruct((S, D), F32),
                   jax.ShapeDtypeStruct((8, D), F32)],
        compiler_params=_params("arbitrary"),
    )(dout, x, y, rows, w_out)


def _dh(dproj, wg, dxres, x, mod, sides):
    S, D = x.shape
    NS = wg.shape[2]
    TS = _tile(S, 256)
    nt = S // TS
    ns = len(sides)

    def body(dp_ref, w_hbm, dxr_ref, x_ref, mod_ref, *rest):
        s_refs, dx_ref, sum_ref, so_refs = rest[:ns], rest[ns], rest[ns + 1], rest[ns + 2:2 * ns + 2]
        w_vmem, sem, send_sems, recv_sems, local_sems = rest[2 * ns + 2:]
        i = pl.program_id(0)
        ix, iy, ic = lax.axis_index("x"), lax.axis_index("y"), lax.axis_index("c")
        me = 4 * ix + 2 * iy + ic

        def exchange(which):
            (_, gather), x_side, o_side = sides[which], s_refs[which], so_refs[which]
            copies = [pltpu.make_async_copy(x_side if gather else x_side.at[me], o_side.at[me], local_sems.at[which])]
            for k in range(1, NDEV):
                px = 1 - ix if k & 4 else ix
                py = 1 - iy if k & 2 else iy
                pc = 1 - ic if k & 1 else ic
                copies.append(pltpu.make_async_remote_copy(
                    src_ref=x_side if gather else x_side.at[4 * px + 2 * py + pc], dst_ref=o_side.at[me],
                    send_sem=send_sems.at[which, k - 1], recv_sem=recv_sems.at[which, k - 1],
                    device_id=(px, py, pc), device_id_type=pl.DeviceIdType.MESH))
            return copies

        @pl.when(i == 0)
        def _():
            load = pltpu.make_async_copy(w_hbm, w_vmem, sem)
            load.start()
            for which in range(ns):
                for cp in exchange(which):
                    cp.start()
            sum_ref[...] = jnp.zeros_like(sum_ref)
            load.wait()

        dh = _dot_nt(dp_ref[:, 0:NS], w_vmem[0])
        for j in range(1, NDEV):
            dh = dh + _dot_nt(dp_ref[:, j * NS:(j + 1) * NS], w_vmem[j])
        dx_ref[...] = dxr_ref[...] + dh * mod_ref[0:1, :]
        sum_ref[0:1, :] += _colsum(dh * x_ref[...])
        sum_ref[1:2, :] += _colsum(dh)

        @pl.when(i == nt - 1)
        def _():
            for which in range(ns):
                for cp in exchange(which):
                    cp.wait()

    spec = pl.BlockSpec((TS, D), lambda i: (i, 0))
    rspec = pl.BlockSpec((8, D), lambda i: (0, 0))
    anyspec = pl.BlockSpec(memory_space=pl.ANY)
    side_shapes = [jax.ShapeDtypeStruct(((NDEV,) + a.shape) if g else a.shape, a.dtype) for a, g in sides]
    return _call(
        body, name="dh", grid=(nt,),
        in_specs=[pl.BlockSpec((TS, NDEV * NS), lambda i: (i, 0)), anyspec, spec, spec, rspec] + [anyspec] * ns,
        out_specs=[spec, rspec] + [anyspec] * ns,
        out_shape=[jax.ShapeDtypeStruct((S, D), F32), jax.ShapeDtypeStruct((8, D), F32)] + side_shapes,
        scratch_shapes=[pltpu.VMEM((NDEV, D, NS), BF16), pltpu.SemaphoreType.DMA, pltpu.SemaphoreType.DMA((max(ns, 1), NDEV - 1)),
                        pltpu.SemaphoreType.DMA((max(ns, 1), NDEV - 1)), pltpu.SemaphoreType.DMA((max(ns, 1),))],
        compiler_params=_params("arbitrary"),
    )(dproj, wg, dxres, x, mod, *[a for a, _ in sides])


def _halo_next(S, TS, BR, k):
    last = S // 8 - 1
    return pl.BlockSpec((8, BR), lambda i: (jnp.minimum((i + 1) * (TS // 8), last), k))


def _conv_a_bwd(proj, dcat, w8, BR):
    S = proj.shape[0]
    TS = _tile(S, 512)
    nt = S // TS

    def body(ab_ref, ac_ref, ax_ref, ag_ref, dy_ref, hc_ref, hx_ref, nab_ref, nag_ref, ndy_ref, w_ref, dp_ref, dw_ref, ext, dext):
        i = pl.program_id(0)

        @pl.when(i == 0)
        def _():
            dw_ref[...] = jnp.zeros_like(dw_ref)

        w0, w1, w2 = w_ref[0:1, :], w_ref[1:2, :], w_ref[2:3, :]
        ab, ac, ax, ag, dy = ab_ref[...], ac_ref[...], ax_ref[...], ag_ref[...], dy_ref[...]
        u = ac * ax
        ext[0:8, :] = jnp.where(i > 0, hc_ref[...] * hx_ref[...], 0.0)
        ext[8:8 + TS, :] = u
        u1, u2 = ext[pl.ds(7, TS), :], ext[pl.ds(6, TS), :]
        conv = u2 * w0 + u1 * w1 + u * w2
        sg = _silu(ag)
        dconv = dy * ab * sg
        dext[0:TS, :] = dconv
        dext[TS:TS + 8, :] = jnp.where(i < nt - 1, ndy_ref[...] * nab_ref[...] * _silu(nag_ref[...]), 0.0)
        du = w2 * dconv + w1 * dext[pl.ds(1, TS), :] + w0 * dext[pl.ds(2, TS), :]
        dp_ref[:, 0:BR] = (dy * conv * sg).astype(BF16)
        dp_ref[:, BR:2 * BR] = (du * ax).astype(BF16)
        dp_ref[:, 2 * BR:3 * BR] = (du * ac).astype(BF16)
        dp_ref[:, 3 * BR:4 * BR] = (dy * ab * conv * _dsilu(ag)).astype(BF16)
        dw_ref[0:1, :] += _colsum(dconv * u2)
        dw_ref[1:2, :] += _colsum(dconv * u1)
        dw_ref[2:3, :] += _colsum(dconv * u)

    rspec = pl.BlockSpec((8, BR), lambda i: (0, 0))
    return _call(
        body, name="conv_a_bwd", grid=(nt,),
        in_specs=[_chunk(TS, BR, 0), _chunk(TS, BR, 1), _chunk(TS, BR, 2), _chunk(TS, BR, 3), _chunk(TS, BR, 0),
                  _halo_prev(TS, BR, 1), _halo_prev(TS, BR, 2), _halo_next(S, TS, BR, 0), _halo_next(S, TS, BR, 3),
                  _halo_next(S, TS, BR, 0), rspec],
        out_specs=[pl.BlockSpec((TS, 4 * BR), lambda i: (i, 0)), rspec],
        out_shape=[jax.ShapeDtypeStruct((S, 12 * BR), BF16), jax.ShapeDtypeStruct((8, BR), F32)],
        scratch_shapes=[pltpu.VMEM((TS + 8, BR), F32), pltpu.VMEM((TS + 8, BR), F32)],
        compiler_params=_params("arbitrary"),
    )(proj, proj, proj, proj, dcat, proj, proj, proj, proj, dcat, w8)


def _attn_mix_bwd(proj, dcat, os_, ls_, head_ones, BR):
    S = proj.shape[0]
    TS = _tile(S, 512)

    def body(dy_ref, g_ref, o1, o2, o3, l1, l2, l3, ones_ref, d1, d2, d3, e1, e2, e3, dg_ref):
        w1, w2, w3 = _mix_weights(l1[...], l2[...], l3[...])
        mix = w1 * o1[...] + w2 * o2[...] + w3 * o3[...]
        g = g_ref[...]
        dy = dy_ref[...]
        dmix = dy * _silu(g)
        dg_ref[...] = dy * mix * _dsilu(g)
        t = jnp.dot(dmix * mix, ones_ref[...], preferred_element_type=F32, precision=lax.Precision.HIGHEST)
        d1[...] = w1 * dmix
        d2[...] = w2 * dmix
        d3[...] = w3 * dmix
        e1[...] = -w1 * t
        e2[...] = -w2 * t
        e3[...] = -w3 * t

    spec = pl.BlockSpec((TS, BR), lambda i: (i, 0))
    outs = _call(
        body, name="attn_mix_bwd", grid=(S // TS,),
        in_specs=[_chunk(TS, BR, 1), _chunk(TS, BR, 7)] + [spec] * 6 + [pl.BlockSpec((BR, BR), lambda i: (0, 0))],
        out_specs=[spec] * 7,
        out_shape=[jax.ShapeDtypeStruct((S, BR), F32)] * 7,
        compiler_params=_params("arbitrary"),
    )(dcat, proj, *os_, *ls_, head_ones)
    return outs[0:3], outs[3:6], outs[6]


def _attn_bwd(proj, do, e, lse, bias, dil, BR):
    S = proj.shape[0]
    HD, U, NU, nbu, hpb, HBK = _attn_geometry(S, BR, dil)
    scale = HD ** -0.5
    combos = [(n, r) for n in range(nbu) for r in range(dil)]

    def body(q_ref, kc_ref, kp_ref, vc_ref, vp_ref, do_ref, e_ref, l_ref, b_ref, dq_ref, dk_ref, dv_ref, db_ref,
             sbuf, dpbuf, pbuf, dsbuf, dkacc, dvacc, nxk, nxv, cark, carv):
        j = pl.program_id(1)
        mu = NU - 1 - j
        band, is_cur = _attn_mask()
        first = jnp.logical_and(band, jnp.logical_or(is_cur, mu > 0))

        @pl.when(j == 0)
        def _():
            cark[...] = jnp.zeros_like(cark)
            carv[...] = jnp.zeros_like(carv)
            db_ref[...] = jnp.zeros_like(db_ref)

        for ref in (dkacc, dvacc, nxk, nxv):
            ref[...] = jnp.zeros_like(ref)

        def keys(ref_c, ref_p, n, r):
            prev = ref_c[_attn_rows(n - 1, r, dil), :] if n > 0 else ref_p[_attn_rows(nbu - 1, r, dil), :]
            return jnp.concatenate([prev, ref_c[_attn_rows(n, r, dil), :]], axis=0).astype(BF16)

        for c, (n, r) in enumerate(combos):
            rows = _attn_rows(n, r, dil)
            q = (q_ref[rows, :] * scale).astype(BF16)
            dob = do_ref[rows, :].astype(BF16)
            kk, vv = keys(kc_ref, kp_ref, n, r), keys(vc_ref, vp_ref, n, r)
            for h in range(hpb):
                sl = slice(h * HD, (h + 1) * HD)
                sbuf[c * hpb + h] = _dot_nt(q[:, sl], kk[:, sl])
                dpbuf[c * hpb + h] = _dot_nt(dob[:, sl], vv[:, sl])
        dbs = [None] * hpb
        for c, (n, r) in enumerate(combos):
            rows = _attn_rows(n, r, dil)
            lse, ev = l_ref[rows, :], e_ref[rows, :]
            for h in range(hpb):
                one = slice(h * HD, h * HD + 1)
                s = jnp.where(band if n > 0 else first, sbuf[c * hpb + h] + b_ref[h], NEG)
                p = jnp.exp(s - lse[:, one])
                ds = p * (dpbuf[c * hpb + h] + ev[:, one])
                pbuf[c * hpb + h] = p.astype(BF16)
                dsbuf[c * hpb + h] = ds.astype(BF16)
                dbs[h] = ds if dbs[h] is None else dbs[h] + ds
        for h in range(hpb):
            db_ref[h] += dbs[h]
        for c, (n, r) in enumerate(combos):
            rows = _attn_rows(n, r, dil)
            q = (q_ref[rows, :] * scale).astype(BF16)
            dob = do_ref[rows, :].astype(BF16)
            kk = keys(kc_ref, kp_ref, n, r)
            dq, dkk, dvv = [], [], []
            for h in range(hpb):
                sl = slice(h * HD, (h + 1) * HD)
                dsb = dsbuf[c * hpb + h]
                dq.append(_dot(dsb, kk[:, sl]) * scale)
                dkk.append(_dot_tn(dsb, q[:, sl]))
                dvv.append(_dot_tn(pbuf[c * hpb + h], dob[:, sl]))
            dq_ref[rows, :] = jnp.concatenate(dq, axis=1)
            dkk, dvv = jnp.concatenate(dkk, axis=1), jnp.concatenate(dvv, axis=1)
            dkacc[rows, :] += dkk[BLK:2 * BLK, :]
            dvacc[rows, :] += dvv[BLK:2 * BLK, :]
            if n > 0:
                prow = _attn_rows(n - 1, r, dil)
                dkacc[prow, :] += dkk[0:BLK, :]
                dvacc[prow, :] += dvv[0:BLK, :]
            else:
                prow = _attn_rows(nbu - 1, r, dil)
                nxk[prow, :] += dkk[0:BLK, :]
                nxv[prow, :] += dvv[0:BLK, :]
        dk_ref[...] = dkacc[...] + cark[...]
        dv_ref[...] = dvacc[...] + carv[...]
        cark[...] = nxk[...]
        carv[...] = nxv[...]

    def cur(c):
        return pl.BlockSpec((U, ATT_LANES), lambda hb, j: (NU - 1 - j, c * HBK + hb))

    def prev(c):
        return pl.BlockSpec((U, ATT_LANES), lambda hb, j: (jnp.maximum(NU - 2 - j, 0), c * HBK + hb))

    own = pl.BlockSpec((U, ATT_LANES), lambda hb, j: (NU - 1 - j, hb))
    bspec = pl.BlockSpec((hpb, BLK, 2 * BLK), lambda hb, j: (hb, 0, 0))
    nhb = len(combos) * hpb
    unit = pltpu.VMEM((U, ATT_LANES), F32)
    return _call(
        body, name="attn_bwd_d%d" % dil, grid=(HBK, NU),
        in_specs=[cur(4), cur(5), prev(5), cur(6), prev(6), own, own, own, bspec],
        out_specs=[own, own, own, bspec],
        out_shape=[jax.ShapeDtypeStruct((S, BR), F32)] * 3 + [jax.ShapeDtypeStruct((8, BLK, 2 * BLK), F32)],
        scratch_shapes=[pltpu.VMEM((nhb, BLK, 2 * BLK), F32)] * 2 + [pltpu.VMEM((nhb, BLK, 2 * BLK), BF16)] * 2 + [unit] * 6,
        compiler_params=_params("arbitrary", "arbitrary"),
    )(proj, proj, proj, proj, proj, do, e, lse, bias)


def _attn_dsum(ds, dbg, dproj, BR):
    S = dbg.shape[0]
    TS = _tile(S, 512)

    def body(*refs):
        o_ref = refs[11]
        for k in range(3):
            o_ref[:, k * BR:(k + 1) * BR] = (refs[k][...] + refs[3 + k][...] + refs[6 + k][...]).astype(BF16)
        o_ref[:, 3 * BR:4 * BR] = refs[9][...].astype(BF16)

    spec = pl.BlockSpec((TS, BR), lambda i: (i, 0))
    return _call(
        body, name="attn_dsum", grid=(S // TS,),
        in_specs=[spec] * 10 + [pl.BlockSpec(memory_space=pl.ANY)],
        out_specs=pl.BlockSpec((TS, 4 * BR), lambda i: (i, 1)),
        out_shape=jax.ShapeDtypeStruct(dproj.shape, BF16),
        input_output_aliases={10: 0},
        compiler_params=_params("arbitrary"),
    )(*[t for trip in ds for t in trip], dbg, dproj)


def _lru_bwd(proj, h, dcat, rows, wa, wx, dproj, BR):
    S = proj.shape[0]
    TS = _tile(S, 256)
    PAD = TS // 2
    nt = S // TS

    def body(cx_ref, hx_ref, cg_ref, h_ref, hh_ref, dy_ref, r_ref, wa_ref, wx_ref, _,
             dp_ref, sum_ref, dwa_ref, dwx_ref, ext, hext, aext, dext, sa, sb, carry):
        i = pl.program_id(0)
        ti = nt - 1 - i

        @pl.when(i == 0)
        def _():
            sa[TS:TS + PAD, :] = jnp.ones((PAD, BR), F32)
            sb[TS:TS + PAD, :] = jnp.zeros((PAD, BR), F32)
            carry[...] = jnp.zeros_like(carry)
            dext[TS:TS + 8, :] = jnp.zeros((8, BR), F32)
            sum_ref[...] = jnp.zeros_like(sum_ref)
            dwa_ref[...] = jnp.zeros_like(dwa_ref)
            dwx_ref[...] = jnp.zeros_like(dwx_ref)

        w0, w1, w2, w3 = r_ref[0:1, :], r_ref[1:2, :], r_ref[2:3, :], r_ref[3:4, :]
        sp = r_ref[7:8, :]
        cx = cx_ref[...]
        ext[0:8, :] = jnp.where(ti > 0, hx_ref[...], 0.0)
        ext[8:8 + TS, :] = cx
        x3, x2, x1 = ext[pl.ds(5, TS), :], ext[pl.ds(6, TS), :], ext[pl.ds(7, TS), :]
        xc = x3 * w0 + x2 * w1 + x1 * w2 + cx * w3 + r_ref[4:5, :]
        wa_, wx_ = wa_ref[...], wx_ref[...]
        r, ig, a, mult = _lru_gates(xc, wa_, wx_, r_ref[5:6, :], r_ref[6:7, :], sp)
        cg, dy, hv = cg_ref[...], dy_ref[...], h_ref[...]
        dp_ref[:, BR:2 * BR] = (dy * hv * _dsilu(cg)).astype(BF16)
        aext[0:TS, :] = a
        aext[TS:TS + 8, :] = jnp.broadcast_to(carry[0:1, :], (8, BR))
        sa[0:TS, :] = aext[pl.ds(1, TS), :]
        sb[0:TS, :] = dy * _silu(cg)
        d = 1
        while d < TS:
            A, B = sa[0:TS, :], sb[0:TS, :]
            As, Bs = sa[pl.ds(d, TS), :], sb[pl.ds(d, TS), :]
            sb[0:TS, :] = B + A * Bs
            sa[0:TS, :] = A * As
            d *= 2
        gh = sb[0:TS, :] + sa[0:TS, :] * carry[1:2, :]
        carry[0:1, :] = a[0:1, :]
        carry[1:2, :] = gh[0:1, :]
        hext[0:8, :] = jnp.where(ti > 0, hh_ref[...], 0.0)
        hext[8:8 + TS, :] = hv
        da = gh * hext[pl.ds(7, TS), :]
        dmult = gh * ig * xc
        dig = gh * mult * xc
        dxc = gh * mult * ig
        dla = da * a - dmult * a * a / mult
        dpr = dla * (-LRU_C) * sp * r * (1.0 - r)
        dpi = dig * ig * (1.0 - ig)
        xb, dprb, dpib = xc.astype(BF16), dpr.astype(BF16), dpi.astype(BF16)
        dwa_ref[...] += _dot_tn(xb, dprb)
        dwx_ref[...] += _dot_tn(xb, dpib)
        dxc = dxc + _dot_nt(dprb, wa_) + _dot_nt(dpib, wx_)
        sum_ref[0:1, :] += _colsum(dxc * x3)
        sum_ref[1:2, :] += _colsum(dxc * x2)
        sum_ref[2:3, :] += _colsum(dxc * x1)
        sum_ref[3:4, :] += _colsum(dxc * cx)
        sum_ref[4:5, :] += _colsum(dxc)
        sum_ref[5:6, :] += _colsum(dpr)
        sum_ref[6:7, :] += _colsum(dpi)
        sum_ref[7:8, :] += _colsum(dla * (-LRU_C) * r)
        dext[0:TS, :] = dxc
        dcx = w3 * dxc + w2 * dext[pl.ds(1, TS), :] + w1 * dext[pl.ds(2, TS), :] + w0 * dext[pl.ds(3, TS), :]
        dext[TS:TS + 8, :] = dxc[0:8, :]
        dp_ref[:, 0:BR] = dcx.astype(BF16)

    def rev(k):
        return pl.BlockSpec((TS, BR), lambda i: (nt - 1 - i, k))

    def rev_halo(k):
        return pl.BlockSpec((8, BR), lambda i: (jnp.maximum((nt - 1 - i) * (TS // 8) - 1, 0), k))

    full = pl.BlockSpec((BR, BR), lambda i: (0, 0))
    rspec = pl.BlockSpec((8, BR), lambda i: (0, 0))
    return _call(
        body, name="lru_bwd", grid=(nt,),
        in_specs=[rev(8), rev_halo(8), rev(9), rev(0), rev_halo(0), rev(2), rspec, full, full, pl.BlockSpec(memory_space=pl.ANY)],
        out_specs=[pl.BlockSpec((TS, 2 * BR), lambda i: (nt - 1 - i, 4)), rspec, full, full],
        input_output_aliases={9: 0},
        out_shape=[jax.ShapeDtypeStruct(dproj.shape, BF16), jax.ShapeDtypeStruct((8, BR), F32),
                   jax.ShapeDtypeStruct((BR, BR), F32), jax.ShapeDtypeStruct((BR, BR), F32)],
        scratch_shapes=[pltpu.VMEM((TS + 8, BR), F32)] * 4 + [pltpu.VMEM((TS + PAD, BR), F32)] * 2 + [pltpu.VMEM((8, BR), F32)],
        compiler_params=_params("arbitrary"),
    )(proj, proj, proj, h, h, dcat, rows, wa, wx, dproj)


def _s5_bwd(proj, dcat, xre_all, xim_all, a8, cl, bbre, bbim, cre, cim, rows, wg, dproj, BR):
    S = proj.shape[0]
    W = a8.shape[1]
    TS = _tile(S, 256)
    nt = S // TS
    CB, SB, J = S5_CB, S5_SB, S5_LAGS
    nblk = BR // CB

    def body(u_ref, dg_ref, dy_ref, xre_ref, xim_ref, hre_ref, him_ref, a8_ref, cl_ref, bbre_ref, bbim_ref, cre_ref, cim_ref,
             r_ref, wg_ref, _, dp_ref, sum_ref, dwg_ref, da_ref, dbbre_ref, dbbim_ref, dcre_ref, dcim_ref,
             sre, sim, ere, eim, ypre_s, dext, dylag, carry):
        i = pl.program_id(0)
        ti = nt - 1 - i

        @pl.when(i == 0)
        def _():
            dext[TS:TS + 8, :] = jnp.zeros((8, BR), F32)
            carry[...] = jnp.zeros_like(carry)
            for ref in (sum_ref, dwg_ref, da_ref, dbbre_ref, dbbim_ref, dcre_ref, dcim_ref):
                ref[...] = jnp.zeros_like(ref)

        u, dg, dy = u_ref[...], dg_ref[...], dy_ref[...]
        for m in range(nblk):
            cs, ws = slice(m * CB, (m + 1) * CB), slice(m * SB, (m + 1) * SB)
            ypre_s[:, cs] = (_dot(xre_ref[:, ws].astype(BF16), cre_ref[ws, :]) - _dot(xim_ref[:, ws].astype(BF16), cim_ref[ws, :]))
        ypre = ypre_s[...] + r_ref[0:1, :] * u
        yg = _gelu(ypre)
        wg_ = wg_ref[...]
        s = _sigmoid(_dot(yg.astype(BF16), wg_) + r_ref[1:2, :])
        dgl = dy * _silu(dg)
        dp_ref[:, BR:2 * BR] = (dy * yg * s * _dsilu(dg)).astype(BF16)
        dps = dgl * yg * s * (1.0 - s)
        dpsb = dps.astype(BF16)
        sum_ref[1:2, :] += _colsum(dps)
        dwg_ref[...] += _dot_tn(yg.astype(BF16), dpsb)
        dyp = (dgl * s + _dot_nt(dpsb, wg_)) * _dgelu(ypre)
        sum_ref[0:1, :] += _colsum(dyp * u)
        dext[0:TS, :] = dyp
        for m in range(nblk):
            cs, ws = slice(m * CB, (m + 1) * CB), slice(m * SB, (m + 1) * SB)
            dypb = dext[0:TS, cs].astype(BF16)
            xre, xim = xre_ref[:, ws], xim_ref[:, ws]
            dcre_ref[ws, :] += _dot_tn(xre.astype(BF16), dypb)
            dcim_ref[ws, :] -= _dot_tn(xim.astype(BF16), dypb)
            for j in range(J):
                dylag[:, j * CB:(j + 1) * CB] = dext[pl.ds(j, TS), cs].astype(BF16)
            w = _dot(dylag[...], cl_ref[m])
            sre[...] = w[:, 0:SB]
            sim[...] = w[:, SB:2 * SB]
            ar = jnp.broadcast_to(a8_ref[0:1, ws], (8, SB))
            ai = jnp.broadcast_to(a8_ref[1:2, ws], (8, SB))
            gr, gi = carry[0:8, ws], carry[8:16, ws]
            for g in reversed(range(TS // 8)):
                rg = slice(8 * g, 8 * g + 8)
                gr, gi = sre[rg, :] + ar * gr + ai * gi, sim[rg, :] + ar * gi - ai * gr
                sre[rg, :] = gr
                sim[rg, :] = gi
            carry[0:8, ws] = gr
            carry[8:16, ws] = gi
            gre, gim = sre[...], sim[...]
            ere[0:8, :] = jnp.where(ti > 0, hre_ref[:, ws], 0.0)
            eim[0:8, :] = jnp.where(ti > 0, him_ref[:, ws], 0.0)
            ere[8:8 + TS, :] = xre
            eim[8:8 + TS, :] = xim
            xpr, xpi = ere[pl.ds(7, TS), :], eim[pl.ds(7, TS), :]
            da_ref[0:1, ws] += _colsum(gre * xpr + gim * xpi)
            da_ref[1:2, ws] += _colsum(gim * xpr - gre * xpi)
            greb, gimb = gre.astype(BF16), gim.astype(BF16)
            ub = u_ref[:, cs].astype(BF16)
            dbbre_ref[cs, :] += _dot_tn(ub, greb)
            dbbim_ref[cs, :] += _dot_tn(ub, gimb)
            du = _dot_nt(greb, bbre_ref[cs, :]) + _dot_nt(gimb, bbim_ref[cs, :]) + r_ref[0:1, cs] * dext[0:TS, cs]
            dp_ref[:, cs] = du.astype(BF16)
        dext[TS:TS + 8, :] = dext[0:8, :]

    def rev(width, k):
        return pl.BlockSpec((TS, width), lambda i: (nt - 1 - i, k))

    def rev_halo(width):
        return pl.BlockSpec((8, width), lambda i: (jnp.maximum((nt - 1 - i) * (TS // 8) - 1, 0), 0))

    def const(shape):
        return pl.BlockSpec(shape, lambda i: (0,) * len(shape))

    return _call(
        body, name="s5_bwd", grid=(nt,),
        in_specs=[rev(BR, 10), rev(BR, 11), rev(BR, 3), rev(W, 0), rev(W, 0), rev_halo(W), rev_halo(W),
                  const((8, W)), const((nblk, J * CB, 2 * SB)), const((BR, SB)), const((BR, SB)), const((W, CB)), const((W, CB)),
                  const((8, BR)), const((BR, BR)), pl.BlockSpec(memory_space=pl.ANY)],
        input_output_aliases={15: 0},
        out_specs=[pl.BlockSpec((TS, 2 * BR), lambda i: (nt - 1 - i, 5)), const((8, BR)), const((BR, BR)), const((8, W)),
                   const((BR, SB)), const((BR, SB)), const((W, CB)), const((W, CB))],
        out_shape=[jax.ShapeDtypeStruct(dproj.shape, BF16), jax.ShapeDtypeStruct((8, BR), F32), jax.ShapeDtypeStruct((BR, BR), F32),
                   jax.ShapeDtypeStruct((8, W), F32), jax.ShapeDtypeStruct((BR, SB), F32), jax.ShapeDtypeStruct((BR, SB), F32),
                   jax.ShapeDtypeStruct((W, CB), F32), jax.ShapeDtypeStruct((W, CB), F32)],
        scratch_shapes=[pltpu.VMEM((TS, SB), F32)] * 2 + [pltpu.VMEM((TS + 8, SB), F32)] * 2
        + [pltpu.VMEM((TS, BR), F32), pltpu.VMEM((TS + 8, BR), F32), pltpu.VMEM((TS, J * CB), BF16), pltpu.VMEM((16, W), F32)],
        compiler_params=_params("arbitrary"),
    )(proj, proj, dcat, xre_all, xim_all, xre_all, xim_all, a8, cl, bbre, bbim, cre, cim, rows, wg, dproj)


_WEIGHTS = ['rel_bias', 'w_ada', 'b_ada', 'w_in', 'conv_a', 'conv_c', 'conv_c_b', 'lru_wa', 'lru_ba', 'lru_wx', 'lru_bx',
            'lru_lambda', 's5_lam_re', 's5_lam_im', 's5_log_dt', 's5_b_re', 's5_b_im', 's5_c_re', 's5_c_im', 's5_d',
            's5_w_glu', 's5_b_glu', 'w_out', 'ln_g', 'ln_b']
_LAYER_SMALL = ['conv_a', 'conv_c', 'conv_c_b', 'lru_wa', 'lru_ba', 'lru_wx', 'lru_bx', 'lru_lambda', 's5_lam_re', 's5_lam_im',
                's5_log_dt', 's5_b_re', 's5_b_im', 's5_c_re', 's5_c_im', 's5_d', 's5_b_glu']
_SMALL = ['rel_bias'] + _LAYER_SMALL + ['ln_g', 'ln_b']


def _t5_bucket(dist):
    max_exact = REL_BUCKETS // 2
    nf = jnp.maximum(dist, 1).astype(F32)
    large = max_exact + (jnp.log(nf / max_exact) / math.log(REL_MAX_DIST / max_exact)
                         * (REL_BUCKETS - max_exact)).astype(jnp.int32)
    large = jnp.minimum(large, REL_BUCKETS - 1)
    return jnp.where(dist < max_exact, dist, large)


def _bias_tables(rel_bias):
    i = jnp.arange(BLK)[:, None]
    j = jnp.arange(2 * BLK)[None, :]
    delta = i + BLK - j
    out = []
    for window, dil in DILATIONS:
        bucket = _t5_bucket(jnp.clip(delta, 0, window // dil) * dil)
        onehot = (bucket[:, :, None] == jnp.arange(REL_BUCKETS)[None, None, :]).astype(F32)
        out.append(jnp.einsum('ijb,bh->hij', onehot, rel_bias, precision=lax.Precision.HIGHEST))
    return jnp.stack(out)


def _prep_layer(q):
    eye8 = jnp.eye(8, dtype=F32)
    G = q['s5_lam_re'].shape[0]
    nblk = G // 8

    def block_diag(w):
        hd = w.shape[1]
        return (w[:, :, None, :] * eye8[:, None, :, None]).reshape(8 * hd, 8 * hd)

    def compact_b(bb):
        t = jnp.transpose(bb.reshape(nblk, 8, S5_STATE, S5_CH), (0, 1, 3, 2))
        return (t[:, :, :, None, :] * eye8[None, :, None, :, None]).reshape(G * S5_CH, 8 * S5_STATE)

    def compact_c(cc):
        t = jnp.transpose(cc.reshape(nblk, 8, S5_CH, S5_STATE), (0, 1, 3, 2))
        return (t[:, :, :, None, :] * eye8[None, :, None, :, None]).reshape(G * S5_STATE, 8 * S5_CH)

    lam_re, lam_im = q['s5_lam_re'], q['s5_lam_im']
    dt = jnp.exp(q['s5_log_dt'])[:, None]
    mag = jnp.exp(lam_re * dt)
    ab_re = mag * jnp.cos(lam_im * dt)
    ab_im = mag * jnp.sin(lam_im * dt)
    den = lam_re * lam_re + lam_im * lam_im
    f_re = ((ab_re - 1.0) * lam_re + ab_im * lam_im) / den
    f_im = (ab_im * lam_re - (ab_re - 1.0) * lam_im) / den
    b_re, b_im = q['s5_b_re'], q['s5_b_im']
    bb_re = f_re[..., None] * b_re - f_im[..., None] * b_im
    bb_im = f_re[..., None] * b_im + f_im[..., None] * b_re
    return dict(
        conv_a=q['conv_a'], conv_c=q['conv_c'], conv_c_b=q['conv_c_b'], lru_ba=q['lru_ba'], lru_bx=q['lru_bx'],
        sp=jax.nn.softplus(-q['lru_lambda']), wa=block_diag(q['lru_wa']), wx=block_diag(q['lru_wx']),
        ar=ab_re.reshape(-1), ai=ab_im.reshape(-1), bbre=compact_b(bb_re), bbim=compact_b(bb_im),
        cre=compact_c(q['s5_c_re']), cim=compact_c(q['s5_c_im']), s5_d=q['s5_d'], s5_b_glu=q['s5_b_glu'])


def _s5_lag_weights(kp):
    J, CB, SB = S5_LAGS, S5_CB, S5_SB
    ar, ai = kp['ar'], kp['ai']
    W = ar.shape[0]
    nblk = W // SB
    cr, ci = jnp.ones_like(ar), jnp.zeros_like(ai)
    pows = []
    for _ in range(J + 1):
        pows.append((cr, ci))
        cr, ci = cr * ar - ci * ai, cr * ai + ci * ar
    pwr = jnp.transpose(jnp.stack([p[0] for p in pows[:J]]).reshape(J, nblk, SB), (1, 0, 2))[:, :, None, :]
    pwi = jnp.transpose(jnp.stack([p[1] for p in pows[:J]]).reshape(J, nblk, SB), (1, 0, 2))[:, :, None, :]

    def lagged(re, im, sign):
        out = jnp.concatenate([re * pwr - sign * im * pwi, sign * re * pwi + im * pwr], axis=-1)
        return out.reshape(nblk, J * CB, 2 * SB).astype(BF16)

    bbl = lagged(kp['bbre'].reshape(nblk, 1, CB, SB), kp['bbim'].reshape(nblk, 1, CB, SB), 1.0)
    c0r = jnp.transpose(kp['cre'].reshape(nblk, SB, CB), (0, 2, 1))[:, None]
    c0i = -jnp.transpose(kp['cim'].reshape(nblk, SB, CB), (0, 2, 1))[:, None]
    cl = lagged(c0r, c0i, -1.0)
    return _rows8([pows[J][0], pows[J][1]], W), bbl, cl


def _rows8(vecs, width):
    rows = [v.reshape(1, width).astype(F32) for v in vecs]
    return jnp.concatenate(rows + [jnp.zeros((8 - len(rows), width), F32)], axis=0)


def kernel(x, c, rel_bias, w_ada, b_ada, w_in, conv_a, conv_c, conv_c_b, lru_wa, lru_ba, lru_wx, lru_bx, lru_lambda, s5_lam_re, s5_lam_im, s5_log_dt, s5_b_re, s5_b_im, s5_c_re, s5_c_im, s5_d, s5_w_glu, s5_b_glu, w_out, ln_g, ln_b, loss_target, m_rel_bias, m_w_ada, m_b_ada, m_w_in, m_conv_a, m_conv_c, m_conv_c_b, m_lru_wa, m_lru_ba, m_lru_wx, m_lru_bx, m_lru_lambda, m_s5_lam_re, m_s5_lam_im, m_s5_log_dt, m_s5_b_re, m_s5_b_im, m_s5_c_re, m_s5_c_im, m_s5_d, m_s5_w_glu, m_s5_b_glu, m_w_out, m_ln_g, m_ln_b, v_rel_bias, v_w_ada, v_b_ada, v_w_in, v_conv_a, v_conv_c, v_conv_c_b, v_lru_wa, v_lru_ba, v_lru_wx, v_lru_bx, v_lru_lambda, v_s5_lam_re, v_s5_lam_im, v_s5_log_dt, v_s5_b_re, v_s5_b_im, v_s5_c_re, v_s5_c_im, v_s5_d, v_s5_w_glu, v_s5_b_glu, v_w_out, v_ln_g, v_ln_b):
    a = dict(locals())
    me = 4 * lax.axis_index("x") + 2 * lax.axis_index("y") + lax.axis_index("c")
    x0, target = x[0], loss_target[0]
    S, D = x0.shape
    BR = D // 4
    NS = w_in.shape[2]
    SH = BR // NDEV
    assert S % (BLK * DILATIONS[-1][1]) == 0 and BR % (8 * S5_CH) == 0

    small = _all_gather(_pack([c, conv_a, conv_c]), "gather_small")
    per_dev = [_unpack(small[d], [c.shape, conv_a.shape, conv_c.shape]) for d in range(NDEV)]
    c_all = jnp.concatenate([p[0] for p in per_dev], axis=0)
    conv_a_full = jnp.concatenate([p[1] for p in per_dev], axis=2)
    conv_c_full = jnp.concatenate([p[2] for p in per_dev], axis=2)
    me1 = me.reshape(1).astype(jnp.int32)
    w_in_bf, w_out_bf, w_glu_bf = w_in.astype(BF16), w_out.astype(BF16), s5_w_glu.astype(BF16)

    b_cols = lax.dynamic_slice(b_ada, (0, me * NS), (DEPTH, NS)).reshape(DEPTH, 1, NS)
    ada_all = _all_gather(_ada_cols(c_all, w_ada, b_cols), "gather_ada")
    ada_me = lax.dynamic_index_in_dim(ada_all, me, axis=2, keepdims=False)
    ada_me = jnp.transpose(ada_me, (1, 0, 2)).reshape(DEPTH, 3 * D)
    shift, scale, gate = ada_me[:, :D], ada_me[:, D:2 * D], ada_me[:, 2 * D:]

    bias_tabs, bias_pull = jax.vjp(_bias_tables, rel_bias)
    HD = BR // 8
    head_ones = jnp.kron(jnp.eye(8, dtype=F32), jnp.ones((HD, HD), F32))

    saved = []
    xl = x0
    for l in range(DEPTH):
        q = {n: a[n][l] for n in _LAYER_SMALL}
        q['conv_a'], q['conv_c'] = conv_a_full[l], conv_c_full[l]
        kp, pull = jax.vjp(_prep_layer, q)
        a8, bbl, cl = _s5_lag_weights(kp)
        mod = _rows8([1.0 + scale[l], shift[l]], D)
        arows = _rows8(list(kp['conv_a']), BR)
        lrows = _rows8(list(kp['conv_c']) + [kp['conv_c_b'], kp['lru_ba'], kp['lru_bx'], kp['sp']], BR)
        srows = _rows8([kp['s5_d'], kp['s5_b_glu']], BR)
        orows = _rows8([1.0 + gate[l], ln_g[l], ln_b[l]], D)
        wa, wx = kp['wa'].astype(BF16), kp['wx'].astype(BF16)
        s5w = [kp[n].astype(BF16) for n in ('bbre', 'bbim', 'cre', 'cim')]

        hbf = _modulate(xl, mod)
        proj, w_in_l, w_out_l, w_glu_l = _proj_gather(hbf, w_in_bf[l], [w_out_bf[l], w_glu_bf[l]], me1)
        w_out_l, w_glu_l = w_out_l.reshape(D, D), w_glu_l.reshape(BR, BR)
        ya = _conv_a_fwd(proj, arows, BR)
        os_, ls_ = [], []
        for g, (_, dil) in enumerate(DILATIONS):
            o, lse = _attn_fwd(proj, bias_tabs[g], dil, BR)
            os_.append(o)
            ls_.append(lse)
        yb = _attn_mix_fwd(proj, os_, ls_, BR)
        yc, hs = _lru_fwd(proj, lrows, wa, wx, BR)
        yd, xre, xim = _s5_fwd(proj, a8, bbl, s5w[2], s5w[3], srows, w_glu_l, BR)
        if l < DEPTH - 1:
            xn, y, cat = _out_ln([ya, yb, yc, yd], w_out_l, xl, orows)
        else:
            xn, y, cat, loss_local = _out_ln([ya, yb, yc, yd], w_out_l, xl, orows, target)
        saved.append(dict(x=xl, proj=proj, hbf=hbf, os=os_, ls=ls_, hs=hs, xre=xre, xim=xim, y=y, cat=cat, pull=pull,
                          mod=mod, arows=arows, lrows=lrows, srows=srows, orows=orows, wa=wa, wx=wx, s5w=s5w,
                          qw=(a8, cl), w_in=w_in_l, w_out=w_out_l, w_glu=w_glu_l))
        xl = xn

    dout = xl

    dbias = jnp.zeros_like(bias_tabs)
    lgrads, dada, dw_in_parts, dw_out_parts, glu_parts = [None] * DEPTH, [None] * DEPTH, [None] * DEPTH, [None] * DEPTH, [None] * DEPTH
    for l in reversed(range(DEPTH)):
        sv = saved[l]
        proj = sv['proj']
        dxres, dyb, dcat, lnsum = _ln_bwd(dout, sv['x'], sv['y'], sv['orows'], sv['w_out'])
        dw_out_parts[l] = _dw_scatter(sv['cat'], dyb, D // NDEV, True, me1, "dw_out_scatter")
        dproj, asum = _conv_a_bwd(proj, dcat, sv['arows'], BR)
        dos, es, dbg = _attn_mix_bwd(proj, dcat, sv['os'], sv['ls'], head_ones, BR)
        dqkv, dbs = [], []
        for g, (_, dil) in enumerate(DILATIONS):
            dq_, dk_, dv_, db_ = _attn_bwd(proj, dos[g], es[g], sv['ls'][g], bias_tabs[g], dil, BR)
            dqkv.append((dq_, dk_, dv_))
            dbs.append(db_)
        dbias = dbias + jnp.stack(dbs)
        dproj = _attn_dsum(dqkv, dbg, dproj, BR)
        dproj, lsum, dwa, dwx = _lru_bwd(proj, sv['hs'], dcat, sv['lrows'], sv['wa'], sv['wx'], dproj, BR)
        dproj, ssum, dwg, da_, dbbre, dbbim, dcre, dcim = _s5_bwd(
            proj, dcat, sv['xre'], sv['xim'], *sv['qw'], *sv['s5w'], sv['srows'], sv['w_glu'], dproj, BR)
        dkp = dict(conv_a=asum[0:3], conv_c=lsum[0:4], conv_c_b=lsum[4], lru_ba=lsum[5], lru_bx=lsum[6], sp=lsum[7],
                   wa=dwa, wx=dwx, ar=da_[0], ai=da_[1], bbre=dbbre, bbim=dbbim, cre=dcre, cim=dcim,
                   s5_d=ssum[0], s5_b_glu=ssum[1])
        lgrads[l] = dict(sv['pull'](dkp)[0], ln_g=lnsum[0], ln_b=lnsum[1])
        sides = [(dwg.reshape(NDEV, BR // NDEV, BR), False)]
        if l == 0:
            local = {'rel_bias': bias_pull(dbias)[0], 'loss': loss_local.reshape(1)}
            for n in _LAYER_SMALL + ['ln_g', 'ln_b']:
                local[n] = jnp.stack([lgrads[k][n] for k in range(DEPTH)])
            packed = _SMALL + ['loss']
            sides.append((_pack([local[n] for n in packed]), True))
        dout, msum, glu_parts[l], *gathered = _dh(dproj, sv['w_in'], dxres, sv['x'], sv['mod'], sides)
        dw_in_parts[l] = _dw_scatter(sv['hbf'], dproj, NS, False, me1, "dw_in_scatter")
        dada[l] = jnp.concatenate([msum[1], msum[0], lnsum[2]])
    grad_x = dout[None]

    out = {}

    def put(name, res):
        out[name] = res

    put('w_in', _adamw(dw_in_parts, w_in, m_w_in, v_w_in, "adamw_w_in"))
    put('w_out', _adamw(dw_out_parts, w_out, m_w_out, v_w_out, "adamw_w_out"))
    put('s5_w_glu', _adamw(glu_parts, s5_w_glu, m_s5_w_glu, v_s5_w_glu, "adamw_w_glu"))

    dada_all = _all_gather(jnp.stack(dada), "gather_dada")
    dada_cols = jnp.transpose(lax.dynamic_slice(dada_all, (0, 0, me * NS), (NDEV, DEPTH, NS)), (1, 0, 2))
    gw_ada, gb_ada = _ada_grads(c_all, dada_all, dada_cols)
    put('w_ada', _adamw([gw_ada[l] for l in range(DEPTH)], w_ada, m_w_ada, v_w_ada, "adamw_w_ada"))

    shapes = [local[n].shape for n in packed]
    gs = dict(zip(packed, _unpack(_sum8(gathered[0], "sum_small_grads"), shapes)))
    loss = gs['loss'][0]
    gs['conv_a'] = lax.dynamic_slice_in_dim(gs['conv_a'], me * SH, SH, axis=2)
    gs['conv_c'] = lax.dynamic_slice_in_dim(gs['conv_c'], me * SH, SH, axis=2)
    gs['b_ada'] = gb_ada
    names = _SMALL + ['b_ada']
    shapes = [a[n].shape for n in names]
    res = _adamw([_pack([gs[n] for n in names])], _pack([a[n] for n in names])[None], _pack([a['m_' + n] for n in names])[None],
                 _pack([a['v_' + n] for n in names])[None], "adamw_small")
    unpacked = [_unpack(r[0], shapes) for r in res]
    for k, n in enumerate(names):
        put(n, tuple(unpacked[j][k] for j in range(4)))

    return (loss, grad_x, *[out[n][0] for n in _WEIGHTS], *[out[n][1] for n in _WEIGHTS],
            *[out[n][2] for n in _WEIGHTS], *[out[n][3] for n in _WEIGHTS])
```

```python
import functools
import math

import jax
import jax.numpy as jnp
from jax import lax
from jax.experimental import pallas as pl
from jax.experimental.pallas import tpu as pltpu

F32 = jnp.float32
BF16 = jnp.bfloat16

NDEV = 8
DEPTH = 2
BLK = 128
DILATIONS = ((128, 1), (512, 4), (2048, 16))
REL_BUCKETS = 32
REL_MAX_DIST = 2048
LRU_C = 8.0
S5_CH = 16
S5_STATE = 64
ALPHA = (2 * DEPTH) ** 0.25
LN_EPS = 1e-5
ADAM_LR, ADAM_B1, ADAM_B2, ADAM_EPS, ADAM_WD, ADAM_STEP = 0.001, 0.9, 0.999, 1e-08, 0.01, 10
NEG = -1e30
VMEM_LIMIT = 56 * 1024 * 1024


def _call(body, **kw):
    return pl.pallas_call(body, **kw)


def _params(*sem):
    return pltpu.CompilerParams(dimension_semantics=sem, vmem_limit_bytes=VMEM_LIMIT)


def _sigmoid(x):
    return 1.0 / (1.0 + jnp.exp(-x))


def _silu(x):
    return x * _sigmoid(x)


def _dsilu(x):
    s = _sigmoid(x)
    return s * (1.0 + x * (1.0 - s))


_GC = math.sqrt(2.0 / math.pi)


def _gelu(x):
    return 0.5 * x * (1.0 + jnp.tanh(_GC * (x + 0.044715 * x * x * x)))


def _dgelu(x):
    t = jnp.tanh(_GC * (x + 0.044715 * x * x * x))
    return 0.5 * (1.0 + t) + 0.5 * x * (1.0 - t * t) * _GC * (1.0 + 3.0 * 0.044715 * x * x)


def _expm1(x):
    series = x * (1.0 + x * (1.0 / 2) * (1.0 + x * (1.0 / 3) * (1.0 + x * (1.0 / 4) * (1.0 + x * (1.0 / 5) * (1.0 + x * (1.0 / 6))))))
    return jnp.where(jnp.abs(x) < 0.25, series, jnp.exp(x) - 1.0)


def _dot(a, b):
    return jnp.dot(a, b, preferred_element_type=F32)


def _dot_nt(a, b):
    return lax.dot_general(a, b, (((1,), (1,)), ((), ())), preferred_element_type=F32)


def _dot_tn(a, b):
    return lax.dot_general(a, b, (((0,), (0,)), ((), ())), preferred_element_type=F32)


def _colsum(v):
    return jnp.sum(v, axis=0, keepdims=True)


def _all_gather(x, name):
    def body(x_ref, o_ref, send_sems, recv_sems, local_sem):
        ix, iy, ic = lax.axis_index("x"), lax.axis_index("y"), lax.axis_index("c")
        me, sibling = (ix, iy, ic), (ix, iy, 1 - ic)
        chips = [(1 - ix, iy), (ix, 1 - iy), (1 - ix, 1 - iy)]

        def slot(px, py, pc):
            return o_ref.at[4 * px + 2 * py + pc]

        def copy(k, block, to, src=None):
            return pltpu.make_async_remote_copy(
                src_ref=slot(*block) if src is None else src, dst_ref=slot(*block), send_sem=send_sems.at[k],
                recv_sem=recv_sems.at[k], device_id=to, device_id_type=pl.DeviceIdType.MESH)

        mine = pltpu.make_async_copy(x_ref, slot(*me), local_sem)
        mine.start()
        first = [copy(0, me, sibling, src=x_ref)] + [copy(1 + j, me, (*chip, ic), src=x_ref) for j, chip in enumerate(chips)]
        for cp in first:
            cp.start()
        passed = [copy(4 + j, (*chip, ic), sibling) for j, chip in enumerate(chips)]
        for j, chip in enumerate(chips):
            copy(1 + j, (*chip, ic), me).wait_recv()
            passed[j].start()
        copy(0, sibling, me).wait_recv()
        for j, chip in enumerate(chips):
            copy(4 + j, (*chip, 1 - ic), me).wait_recv()
        for cp in first + passed:
            cp.wait_send()
        mine.wait()

    return _call(
        body, name=name,
        out_shape=jax.ShapeDtypeStruct((NDEV,) + x.shape, x.dtype),
        in_specs=[pl.BlockSpec(memory_space=pl.ANY)],
        out_specs=pl.BlockSpec(memory_space=pl.ANY),
        scratch_shapes=[pltpu.SemaphoreType.DMA((NDEV - 1,)), pltpu.SemaphoreType.DMA((NDEV - 1,)), pltpu.SemaphoreType.DMA],
    )(x)


def _pack(arrs):
    parts = []
    for a in arrs:
        n = math.prod(a.shape)
        parts.append(jnp.pad(a.reshape(-1).astype(F32), (0, _pack_rows(n) * 128 - n)).reshape(-1, 128))
    rows = sum(p.shape[0] for p in parts)
    total = rows if rows <= 512 else -(-rows // 512) * 512
    if total > rows:
        parts.append(jnp.zeros((total - rows, 128), F32))
    return jnp.concatenate(parts, axis=0)


def _pack_rows(n):
    return -(-n // 1024) * 8


def _unpack(packed, shapes):
    out, row = [], 0
    for s in shapes:
        n = math.prod(s)
        rows = _pack_rows(n)
        out.append(packed[row:row + rows].reshape(-1)[:n].reshape(s))
        row += rows
    return out


def _sum8(parts, name):
    _, R, C = parts.shape
    TR = R
    for cand in (512, 256, 128, 64, 32, 16, 8):
        if R % cand == 0:
            TR = cand
            break

    def body(p_ref, o_ref):
        acc = p_ref[0]
        for d in range(1, NDEV):
            acc = acc + p_ref[d]
        o_ref[...] = acc

    return _call(
        body, name=name, grid=(R // TR,),
        in_specs=[pl.BlockSpec((NDEV, TR, C), lambda i: (0, i, 0))],
        out_specs=pl.BlockSpec((TR, C), lambda i: (i, 0)),
        out_shape=jax.ShapeDtypeStruct((R, C), F32),
        compiler_params=_params("arbitrary"),
    )(parts)


def _adamw(g_layers, w, m, v, name):
    L, R, C = w.shape
    g0 = g_layers[0]
    nparts = g0.shape[0] if g0.ndim == 3 else 0
    TR = R
    for cand in (256, 128, 64, 32, 16):
        if R % cand == 0 and cand * C <= 128 * 1024:
            TR = cand
            break

    def body(*refs):
        g_refs = refs[:L]
        w_ref, m_ref, v_ref, go_ref, d_ref, mo_ref, vo_ref = refs[L:]
        for ll in range(L):
            @pl.when(pl.program_id(0) == ll)
            def _(g_ref=g_refs[ll]):
                if nparts:
                    g = g_ref[0].astype(F32)
                    for d in range(1, nparts):
                        g = g + g_ref[d].astype(F32)
                else:
                    g = g_ref[...]
                m2 = ADAM_B1 * m_ref[0] + (1.0 - ADAM_B1) * g
                v2 = ADAM_B2 * v_ref[0] + (1.0 - ADAM_B2) * (g * g)
                m_hat = m2 / (1.0 - ADAM_B1 ** ADAM_STEP)
                v_hat = v2 / (1.0 - ADAM_B2 ** ADAM_STEP)
                go_ref[0] = g
                d_ref[0] = -ADAM_LR * (m_hat / (jnp.sqrt(v_hat) + ADAM_EPS) + ADAM_WD * w_ref[0])
                mo_ref[0] = m2
                vo_ref[0] = v2

    def gspec(ll):
        if nparts:
            return pl.BlockSpec((nparts, TR, C), lambda l, i: (0, jnp.where(l == ll, i, 0), 0))
        return pl.BlockSpec((TR, C), lambda l, i: (jnp.where(l == ll, i, 0), 0))

    spec = pl.BlockSpec((1, TR, C), lambda l, i: (l, i, 0))
    return _call(
        body, name=name, grid=(L, R // TR),
        in_specs=[gspec(ll) for ll in range(L)] + [spec, spec, spec], out_specs=[spec] * 4,
        out_shape=[jax.ShapeDtypeStruct((L, R, C), F32)] * 4,
        compiler_params=_params("arbitrary", "arbitrary"),
    )(*g_layers, w, m, v)


def _ada_cols(c_all, w_ada, b_cols):
    L, D, NS = w_ada.shape

    def body(c_ref, w_ref, b_ref, o_ref):
        cond = _silu(c_ref[...]).astype(BF16)
        o_ref[0] = _dot(cond, w_ref[0].astype(BF16)) + b_ref[0]

    return _call(
        body, name="ada_cols", grid=(L,),
        in_specs=[pl.BlockSpec((NDEV, D), lambda l: (0, 0)), pl.BlockSpec((1, D, NS), lambda l: (l, 0, 0)),
                  pl.BlockSpec((1, 1, NS), lambda l: (l, 0, 0))],
        out_specs=pl.BlockSpec((1, NDEV, NS), lambda l: (l, 0, 0)),
        out_shape=jax.ShapeDtypeStruct((L, NDEV, NS), F32),
        compiler_params=_params("arbitrary"),
    )(c_all, w_ada, b_cols)


def _ada_grads(c_all, dada_all, dada_cols):
    _, D = c_all.shape
    L, _, NS = dada_cols.shape
    D3 = dada_all.shape[2]

    def body(c_ref, da_ref, dc_ref, gw_ref, gb_ref):
        cond = _silu(c_ref[...]).astype(BF16)
        gw_ref[0] = _dot_tn(cond, dc_ref[0].astype(BF16))
        acc = da_ref[0]
        for d in range(1, NDEV):
            acc = acc + da_ref[d]
        gb_ref[...] = acc

    return _call(
        body, name="ada_grads", grid=(L,),
        in_specs=[pl.BlockSpec((NDEV, D), lambda l: (0, 0)), pl.BlockSpec((NDEV, L, D3), lambda l: (0, 0, 0)),
                  pl.BlockSpec((1, NDEV, NS), lambda l: (l, 0, 0))],
        out_specs=[pl.BlockSpec((1, D, NS), lambda l: (l, 0, 0)), pl.BlockSpec((L, D3), lambda l: (0, 0))],
        out_shape=[jax.ShapeDtypeStruct((L, D, NS), F32), jax.ShapeDtypeStruct((L, D3), F32)],
        compiler_params=_params("arbitrary"),
    )(c_all, dada_all, dada_cols)


def _tile(S, want):
    return want if S % want == 0 else S


def _modulate(x, mod):
    S, D = x.shape
    TS = _tile(S, 512)

    def body(x_ref, mod_ref, h_ref):
        h_ref[...] = (x_ref[...] * mod_ref[0:1, :] + mod_ref[1:2, :]).astype(BF16)

    spec = pl.BlockSpec((TS, D), lambda i: (i, 0))
    return _call(
        body, name="modulate", grid=(S // TS,),
        in_specs=[spec, pl.BlockSpec((8, D), lambda i: (0, 0))], out_specs=spec,
        out_shape=jax.ShapeDtypeStruct((S, D), BF16),
        compiler_params=_params("arbitrary"),
    )(x, mod)


def _peer(k):
    ix, iy, ic = lax.axis_index("x"), lax.axis_index("y"), lax.axis_index("c")
    bx, by, bc = (k >> 2) & 1, (k >> 1) & 1, k & 1
    px, py, pc = ix + bx - 2 * ix * bx, iy + by - 2 * iy * by, ic + bc - 2 * ic * bc
    return (px, py, pc), 4 * px + 2 * py + pc


GATHER_ORDER = (0, 1, 2, 4, 6, 3, 5, 7)
GATHER_DIRECT = (1, 2, 4, 6)
SCATTER_ORDER = (2, 4, 6, 3, 5, 7, 1, 0)


def _offset(order, k):
    off = jnp.int32(order[-1])
    for step in reversed(range(len(order) - 1)):
        off = jnp.where(k == step, jnp.int32(order[step]), off)
    return off


def _proj_gather(h, w_shard, extras, me1):
    assert GATHER_ORDER[0] == 0
    S, D = h.shape
    NS = w_shard.shape[1]
    TS = _tile(S, 512)
    nt = S // TS
    ne = len(extras)

    def body(me_ref, h_ref, w_ref, *rest):
        x_refs, o_ref, wg_ref = rest[:ne], rest[ne], rest[ne + 1]
        xg_refs = rest[ne + 2:2 * ne + 2]
        wbuf, send_sems, recv_sems, local_sems, load_sems = rest[2 * ne + 2:]
        k, i = pl.program_id(0), pl.program_id(1)
        me = me_ref[0]
        off = _offset(GATHER_ORDER, k)

        def push(src, dst, which, kk):
            peer, _ = _peer(kk)
            return pltpu.make_async_remote_copy(
                src_ref=src, dst_ref=dst.at[me], send_sem=send_sems.at[which, kk], recv_sem=recv_sems.at[which, kk],
                device_id=peer, device_id_type=pl.DeviceIdType.MESH)

        def own(src, dst, which):
            return pltpu.make_async_copy(src, dst.at[me], local_sems.at[which])

        pairs = [(w_ref, wg_ref)] + list(zip(x_refs, xg_refs))

        @pl.when(jnp.logical_and(k == 0, i == 0))
        def _():
            for which, (src, dst) in enumerate(pairs):
                own(src, dst, which).start()
                for kk in (GATHER_DIRECT if which == 0 else GATHER_ORDER[1:]):
                    push(src, dst, which, kk).start()
            first = pltpu.make_async_copy(w_ref, wbuf.at[0], load_sems.at[0])
            first.start()
            first.wait()

        @pl.when(jnp.logical_and(k > 0, i == 0))
        def _():
            push(w_ref, wg_ref, 0, off).wait_recv()
            _, blk = _peer(off)
            for step, o in enumerate(GATHER_ORDER):
                if o in GATHER_DIRECT and o > 1:
                    @pl.when(k == step)
                    def _(o=o):
                        sibling, _ = _peer(1)
                        pltpu.make_async_remote_copy(
                            src_ref=wg_ref.at[blk], dst_ref=wg_ref.at[blk], send_sem=send_sems.at[0, o + 1],
                            recv_sem=recv_sems.at[0, o + 1], device_id=sibling, device_id_type=pl.DeviceIdType.MESH).start()
            load = pltpu.make_async_copy(wg_ref.at[blk], wbuf.at[k % 2], load_sems.at[k % 2])
            load.start()
            load.wait()

        o_ref[...] = _dot(h_ref[...], wbuf[k % 2])

        @pl.when(jnp.logical_and(k == NDEV - 1, i == nt - 1))
        def _():
            for which, (src, dst) in enumerate(pairs):
                for kk in range(1, NDEV):
                    cp = push(src, dst, which, kk)
                    cp.wait_send()
                    if which > 0:
                        cp.wait_recv()
                own(src, dst, which).wait()

    def col(k, i, me_ref):
        m, off = me_ref[0], _offset(GATHER_ORDER, k)
        return i, (m | off) - (m & off)

    anyspec = pl.BlockSpec(memory_space=pl.ANY)
    grid_spec = pltpu.PrefetchScalarGridSpec(
        num_scalar_prefetch=1, grid=(NDEV, nt),
        in_specs=[pl.BlockSpec((TS, D), lambda k, i, me_ref: (i, 0)), anyspec] + [anyspec] * ne,
        out_specs=[pl.BlockSpec((TS, NS), col), anyspec] + [anyspec] * ne,
        scratch_shapes=[pltpu.VMEM((2, D, NS), BF16), pltpu.SemaphoreType.DMA((1 + ne, NDEV)), pltpu.SemaphoreType.DMA((1 + ne, NDEV)),
                        pltpu.SemaphoreType.DMA((1 + ne,)), pltpu.SemaphoreType.DMA((2,))])
    return _call(
        body, name="proj_gather", grid_spec=grid_spec,
        out_shape=[jax.ShapeDtypeStruct((S, NDEV * NS), F32), jax.ShapeDtypeStruct((NDEV, D, NS), BF16)]
        + [jax.ShapeDtypeStruct((NDEV,) + e.shape, e.dtype) for e in extras],
        compiler_params=_params("arbitrary", "arbitrary"),
    )(me1, h, w_shard, *extras)


def _dw_scatter(a, b, blk, by_rows, me1, name):
    assert SCATTER_ORDER[-1] == 0
    S = a.shape[0]
    TK = _tile(S, 512)
    nk = S // TK
    shape = (blk, b.shape[1]) if by_rows else (a.shape[1], blk)
    last = NDEV - 1

    def body(me_ref, a_ref, b_ref, recv_ref, acc, stage, send_sems, recv_sems, local_sem):
        k, kk = pl.program_id(0), pl.program_id(1)
        me = me_ref[0]

        def push(off, slot):
            peer, _ = _peer(off)
            return pltpu.make_async_remote_copy(
                src_ref=stage.at[slot], dst_ref=recv_ref.at[me], send_sem=send_sems.at[off], recv_sem=recv_sems.at[off],
                device_id=peer, device_id_type=pl.DeviceIdType.MESH)

        own = pltpu.make_async_copy(stage.at[last % 2], recv_ref.at[me], local_sem)

        @pl.when(kk == 0)
        def _():
            acc[...] = jnp.zeros_like(acc)
        acc[...] += _dot_tn(a_ref[...], b_ref[...])

        @pl.when(kk == nk - 1)
        def _():
            @pl.when(k >= 2)
            def _():
                push(_offset(SCATTER_ORDER, k - 2), k % 2).wait_send()

            stage[k % 2] = acc[...].astype(BF16)

            @pl.when(k < last)
            def _():
                push(_offset(SCATTER_ORDER, k), k % 2).start()

            @pl.when(k == last)
            def _():
                own.start()
                push(SCATTER_ORDER[last - 1], (last - 1) % 2).wait_send()
                own.wait()
                for off in range(1, NDEV):
                    push(off, 0).wait_recv()

    def blk_of(k, me_ref):
        m, off = me_ref[0], _offset(SCATTER_ORDER, k)
        return (m | off) - (m & off)

    if by_rows:
        in_specs = [pl.BlockSpec((TK, blk), lambda k, kk, me_ref: (kk, blk_of(k, me_ref))),
                    pl.BlockSpec((TK, b.shape[1]), lambda k, kk, me_ref: (kk, 0))]
    else:
        in_specs = [pl.BlockSpec((TK, a.shape[1]), lambda k, kk, me_ref: (kk, 0)),
                    pl.BlockSpec((TK, blk), lambda k, kk, me_ref: (kk, blk_of(k, me_ref)))]
    grid_spec = pltpu.PrefetchScalarGridSpec(
        num_scalar_prefetch=1, grid=(NDEV, nk), in_specs=in_specs,
        out_specs=pl.BlockSpec(memory_space=pl.ANY),
        scratch_shapes=[pltpu.VMEM(shape, F32), pltpu.VMEM((2,) + shape, BF16), pltpu.SemaphoreType.DMA((NDEV,)),
                        pltpu.SemaphoreType.DMA((NDEV,)), pltpu.SemaphoreType.DMA])
    return _call(
        body, name=name, grid_spec=grid_spec,
        out_shape=jax.ShapeDtypeStruct((NDEV,) + shape, BF16),
        compiler_params=_params("arbitrary", "arbitrary"),
    )(me1, a, b)


def _chunk(TS, BR, k):
    return pl.BlockSpec((TS, BR), lambda i: (i, k))


def _halo_prev(TS, BR, k):
    return pl.BlockSpec((8, BR), lambda i: (jnp.maximum(i * (TS // 8) - 1, 0), k))


def _conv_a_fwd(proj, w8, BR):
    S = proj.shape[0]
    TS = _tile(S, 512)

    def body(ab_ref, ac_ref, ax_ref, ag_ref, hc_ref, hx_ref, w_ref, o_ref, ext):
        i = pl.program_id(0)
        u = ac_ref[...] * ax_ref[...]
        ext[0:8, :] = jnp.where(i > 0, hc_ref[...] * hx_ref[...], 0.0)
        ext[8:8 + TS, :] = u
        conv = ext[pl.ds(6, TS), :] * w_ref[0:1, :] + ext[pl.ds(7, TS), :] * w_ref[1:2, :] + u * w_ref[2:3, :]
        o_ref[...] = ab_ref[...] * conv * _silu(ag_ref[...])

    return _call(
        body, name="conv_a_fwd", grid=(S // TS,),
        in_specs=[_chunk(TS, BR, 0), _chunk(TS, BR, 1), _chunk(TS, BR, 2), _chunk(TS, BR, 3),
                  _halo_prev(TS, BR, 1), _halo_prev(TS, BR, 2), pl.BlockSpec((8, BR), lambda i: (0, 0))],
        out_specs=pl.BlockSpec((TS, BR), lambda i: (i, 0)),
        out_shape=jax.ShapeDtypeStruct((S, BR), F32),
        scratch_shapes=[pltpu.VMEM((TS + 8, BR), F32)],
        compiler_params=_params("arbitrary"),
    )(proj, proj, proj, proj, proj, proj, w8)


ATT_UNIT = 2048
ATT_LANES = 128


def _attn_mask():
    i = lax.broadcasted_iota(jnp.int32, (BLK, 2 * BLK), 0)
    j = lax.broadcasted_iota(jnp.int32, (BLK, 2 * BLK), 1)
    return jnp.logical_and(j >= i, j <= i + BLK), j >= BLK


def _attn_rows(n, r, dil):
    return pl.ds(n * BLK * dil + r, BLK, stride=dil) if dil > 1 else pl.ds(n * BLK, BLK)


def _attn_geometry(S, BR, dil):
    HD = BR // 8
    U = ATT_UNIT
    assert S % U == 0 and BR % ATT_LANES == 0 and ATT_LANES % HD == 0 and U % (BLK * dil) == 0
    return HD, U, S // U, U // (BLK * dil), ATT_LANES // HD, BR // ATT_LANES


def _attn_fwd(proj, bias, dil, BR):
    S = proj.shape[0]
    HD, U, NU, nbu, hpb, HBK = _attn_geometry(S, BR, dil)
    scale = HD ** -0.5
    combos = [(n, r) for n in range(nbu) for r in range(dil)]

    def body(q_ref, kc_ref, kp_ref, vc_ref, vp_ref, b_ref, o_ref, l_ref, sbuf, pbuf):
        m = pl.program_id(1)
        band, is_cur = _attn_mask()
        first = jnp.logical_and(band, jnp.logical_or(is_cur, m > 0))

        def keys(ref_c, ref_p, n, r):
            prev = ref_c[_attn_rows(n - 1, r, dil), :] if n > 0 else ref_p[_attn_rows(nbu - 1, r, dil), :]
            return jnp.concatenate([prev, ref_c[_attn_rows(n, r, dil), :]], axis=0).astype(BF16)

        for c, (n, r) in enumerate(combos):
            q = (q_ref[_attn_rows(n, r, dil), :] * scale).astype(BF16)
            kk = keys(kc_ref, kp_ref, n, r)
            for h in range(hpb):
                sl = slice(h * HD, (h + 1) * HD)
                sbuf[c * hpb + h] = _dot_nt(q[:, sl], kk[:, sl])
        for c, (n, r) in enumerate(combos):
            lses = []
            for h in range(hpb):
                s = jnp.where(band if n > 0 else first, sbuf[c * hpb + h] + b_ref[h], NEG)
                mx = jnp.max(s, axis=-1, keepdims=True)
                p = jnp.exp(s - mx)
                l = jnp.sum(p, axis=-1, keepdims=True)
                pbuf[c * hpb + h] = (p * (1.0 / l)).astype(BF16)
                lses.append(jnp.broadcast_to(mx + jnp.log(l), (BLK, HD)))
            l_ref[_attn_rows(n, r, dil), :] = jnp.concatenate(lses, axis=1)
        for c, (n, r) in enumerate(combos):
            vv = keys(vc_ref, vp_ref, n, r)
            o_ref[_attn_rows(n, r, dil), :] = jnp.concatenate(
                [_dot(pbuf[c * hpb + h], vv[:, h * HD:(h + 1) * HD]) for h in range(hpb)], axis=1)

    def cur(c):
        return pl.BlockSpec((U, ATT_LANES), lambda hb, m: (m, c * HBK + hb))

    def prev(c):
        return pl.BlockSpec((U, ATT_LANES), lambda hb, m: (jnp.maximum(m - 1, 0), c * HBK + hb))

    ospec = pl.BlockSpec((U, ATT_LANES), lambda hb, m: (m, hb))
    nhb = len(combos) * hpb
    return _call(
        body, name="attn_fwd_d%d" % dil, grid=(HBK, NU),
        in_specs=[cur(4), cur(5), prev(5), cur(6), prev(6), pl.BlockSpec((hpb, BLK, 2 * BLK), lambda hb, m: (hb, 0, 0))],
        out_specs=[ospec, ospec],
        out_shape=[jax.ShapeDtypeStruct((S, BR), F32)] * 2,
        scratch_shapes=[pltpu.VMEM((nhb, BLK, 2 * BLK), F32), pltpu.VMEM((nhb, BLK, 2 * BLK), BF16)],
        compiler_params=_params("arbitrary", "arbitrary"),
    )(proj, proj, proj, proj, proj, bias)


def _mix_weights(l1, l2, l3):
    m = jnp.maximum(jnp.maximum(l1, l2), l3)
    e1, e2, e3 = jnp.exp(l1 - m), jnp.exp(l2 - m), jnp.exp(l3 - m)
    inv = 1.0 / (e1 + e2 + e3)
    return e1 * inv, e2 * inv, e3 * inv


def _attn_mix_fwd(proj, os_, ls_, BR):
    S = proj.shape[0]
    TS = _tile(S, 512)

    def body(o1, o2, o3, l1, l2, l3, g_ref, y_ref):
        w1, w2, w3 = _mix_weights(l1[...], l2[...], l3[...])
        y_ref[...] = (w1 * o1[...] + w2 * o2[...] + w3 * o3[...]) * _silu(g_ref[...])

    spec = pl.BlockSpec((TS, BR), lambda i: (i, 0))
    return _call(
        body, name="attn_mix_fwd", grid=(S // TS,),
        in_specs=[spec] * 6 + [_chunk(TS, BR, 7)], out_specs=spec,
        out_shape=jax.ShapeDtypeStruct((S, BR), F32),
        compiler_params=_params("arbitrary"),
    )(*os_, *ls_, proj)


def _lru_gates(xc, wa, wx, ba, bx, sp):
    xb = xc.astype(BF16)
    r = _sigmoid(_dot(xb, wa) + ba)
    ig = _sigmoid(_dot(xb, wx) + bx)
    la = -LRU_C * r * sp
    a = jnp.exp(la)
    mult = jnp.sqrt(-_expm1(2.0 * la))
    return r, ig, a, mult


def _lru_fwd(proj, rows, wa, wx, BR):
    S = proj.shape[0]
    TS = _tile(S, 256)
    PAD = TS // 2

    def body(cx_ref, hx_ref, cg_ref, r_ref, wa_ref, wx_ref, y_ref, h_ref, ext, sa, sb, carry):
        i = pl.program_id(0)

        @pl.when(i == 0)
        def _():
            sa[0:PAD, :] = jnp.ones((PAD, BR), F32)
            sb[0:PAD, :] = jnp.zeros((PAD, BR), F32)
            carry[...] = jnp.zeros_like(carry)

        cx = cx_ref[...]
        ext[0:8, :] = jnp.where(i > 0, hx_ref[...], 0.0)
        ext[8:8 + TS, :] = cx
        xc = (ext[pl.ds(5, TS), :] * r_ref[0:1, :] + ext[pl.ds(6, TS), :] * r_ref[1:2, :]
              + ext[pl.ds(7, TS), :] * r_ref[2:3, :] + cx * r_ref[3:4, :] + r_ref[4:5, :])
        _, ig, a, mult = _lru_gates(xc, wa_ref[...], wx_ref[...], r_ref[5:6, :], r_ref[6:7, :], r_ref[7:8, :])
        sa[PAD:PAD + TS, :] = a
        sb[PAD:PAD + TS, :] = mult * ig * xc
        d = 1
        while d < TS:
            A, B = sa[PAD:PAD + TS, :], sb[PAD:PAD + TS, :]
            As, Bs = sa[pl.ds(PAD - d, TS), :], sb[pl.ds(PAD - d, TS), :]
            sb[PAD:PAD + TS, :] = A * Bs + B
            sa[PAD:PAD + TS, :] = A * As
            d *= 2
        h = sb[PAD:PAD + TS, :] + sa[PAD:PAD + TS, :] * carry[0:1, :]
        carry[0:1, :] = h[TS - 1:TS, :]
        h_ref[...] = h
        y_ref[...] = h * _silu(cg_ref[...])

    full = pl.BlockSpec((BR, BR), lambda i: (0, 0))
    spec = pl.BlockSpec((TS, BR), lambda i: (i, 0))
    return _call(
        body, name="lru_fwd", grid=(S // TS,),
        in_specs=[_chunk(TS, BR, 8), _halo_prev(TS, BR, 8), _chunk(TS, BR, 9), pl.BlockSpec((8, BR), lambda i: (0, 0)), full, full],
        out_specs=[spec, spec],
        out_shape=[jax.ShapeDtypeStruct((S, BR), F32)] * 2,
        scratch_shapes=[pltpu.VMEM((TS + 8, BR), F32), pltpu.VMEM((PAD + TS, BR), F32), pltpu.VMEM((PAD + TS, BR), F32),
                        pltpu.VMEM((8, BR), F32)],
        compiler_params=_params("arbitrary"),
    )(proj, proj, proj, rows, wa, wx)


S5_LAGS = 8
S5_CB = 8 * S5_CH
S5_SB = 8 * S5_STATE


def _s5_fwd(proj, a8, bbl, cre, cim, rows, wg, BR):
    S = proj.shape[0]
    W = a8.shape[1]
    TS = _tile(S, 256)
    CB, SB, J = S5_CB, S5_SB, S5_LAGS
    nblk = BR // CB

    def body(u_ref, uh_ref, dg_ref, a8_ref, bbl_ref, cre_ref, cim_ref, r_ref, wg_ref,
             y_ref, xre_ref, xim_ref, uext, ulag, sre, sim, ypre_s, carry):
        i = pl.program_id(0)

        @pl.when(i == 0)
        def _():
            carry[...] = jnp.zeros_like(carry)

        uext[0:8, :] = jnp.where(i > 0, uh_ref[...], 0.0)
        uext[8:8 + TS, :] = u_ref[...]
        for m in range(nblk):
            cs, ws = slice(m * CB, (m + 1) * CB), slice(m * SB, (m + 1) * SB)
            for j in range(J):
                ulag[:, j * CB:(j + 1) * CB] = uext[pl.ds(8 - j, TS), cs].astype(BF16)
            w = _dot(ulag[...], bbl_ref[m])
            sre[...] = w[:, 0:SB]
            sim[...] = w[:, SB:2 * SB]
            ar = jnp.broadcast_to(a8_ref[0:1, ws], (8, SB))
            ai = jnp.broadcast_to(a8_ref[1:2, ws], (8, SB))
            xr, xi = carry[0:8, ws], carry[8:16, ws]
            for g in range(TS // 8):
                rg = slice(8 * g, 8 * g + 8)
                xr, xi = sre[rg, :] + ar * xr - ai * xi, sim[rg, :] + ar * xi + ai * xr
                sre[rg, :] = xr
                sim[rg, :] = xi
            carry[0:8, ws] = xr
            carry[8:16, ws] = xi
            xre, xim = sre[...], sim[...]
            xre_ref[:, ws] = xre
            xim_ref[:, ws] = xim
            ypre_s[:, cs] = _dot(xre.astype(BF16), cre_ref[ws, :]) - _dot(xim.astype(BF16), cim_ref[ws, :])
        dg = dg_ref[...]
        yg = _gelu(ypre_s[...] + r_ref[0:1, :] * u_ref[...])
        s = _sigmoid(_dot(yg.astype(BF16), wg_ref[...]) + r_ref[1:2, :])
        y_ref[...] = yg * s * _silu(dg)

    def const(shape):
        return pl.BlockSpec(shape, lambda i: (0,) * len(shape))

    return _call(
        body, name="s5_fwd", grid=(S // TS,),
        in_specs=[_chunk(TS, BR, 10), _halo_prev(TS, BR, 10), _chunk(TS, BR, 11), const((8, W)), const((nblk, J * CB, 2 * SB)),
                  const((W, CB)), const((W, CB)), const((8, BR)), const((BR, BR))],
        out_specs=[pl.BlockSpec((TS, BR), lambda i: (i, 0)), pl.BlockSpec((TS, W), lambda i: (i, 0)), pl.BlockSpec((TS, W), lambda i: (i, 0))],
        out_shape=[jax.ShapeDtypeStruct((S, BR), F32), jax.ShapeDtypeStruct((S, W), F32), jax.ShapeDtypeStruct((S, W), F32)],
        scratch_shapes=[pltpu.VMEM((TS + 8, BR), F32), pltpu.VMEM((TS, J * CB), BF16), pltpu.VMEM((TS, SB), F32), pltpu.VMEM((TS, SB), F32),
                        pltpu.VMEM((TS, BR), F32), pltpu.VMEM((16, W), F32)],
        compiler_params=_params("arbitrary"),
    )(proj, proj, proj, a8, bbl, cre, cim, rows, wg)


def _out_ln(ys, w_out, x, rows, target=None):
    S, D = x.shape
    BR = D // 4
    TS = _tile(S, 256)
    last = target is not None

    def body(ya, yb, yc, yd, w_ref, x_ref, r_ref, *rest):
        t_ref = rest[0] if last else None
        o_ref, y_ref, cat_ref = rest[1:4] if last else rest[0:3]
        cat = jnp.concatenate([ya[...], yb[...], yc[...], yd[...]], axis=1).astype(BF16)
        y = _dot(cat, w_ref[...])
        z = ALPHA * x_ref[...] + r_ref[0:1, :] * y
        mu = jnp.mean(z, axis=-1, keepdims=True)
        zc = z - mu
        var = jnp.mean(zc * zc, axis=-1, keepdims=True)
        xn = zc * lax.rsqrt(var + LN_EPS) * r_ref[1:2, :] + r_ref[2:3, :]
        y_ref[...] = y
        cat_ref[...] = cat
        if last:
            l_ref = rest[4]

            @pl.when(pl.program_id(0) == 0)
            def _():
                l_ref[...] = jnp.zeros_like(l_ref)
            diff = xn - t_ref[...]
            o_ref[...] = diff * (1.0 / D)
            l_ref[...] += jnp.sum(_colsum(diff * diff), axis=1, keepdims=True) * (0.5 / D)
        else:
            o_ref[...] = xn

    yspec = pl.BlockSpec((TS, BR), lambda i: (i, 0))
    spec = pl.BlockSpec((TS, D), lambda i: (i, 0))
    lspec = pl.BlockSpec((1, 1), lambda i: (0, 0))
    return _call(
        body, name="out_ln_loss" if last else "out_ln", grid=(S // TS,),
        in_specs=[yspec] * 4 + [pl.BlockSpec((D, D), lambda i: (0, 0)), spec, pl.BlockSpec((8, D), lambda i: (0, 0))] + [spec] * last,
        out_specs=[spec, spec, spec] + [lspec] * last,
        out_shape=[jax.ShapeDtypeStruct((S, D), F32), jax.ShapeDtypeStruct((S, D), F32), jax.ShapeDtypeStruct((S, D), BF16)]
        + [jax.ShapeDtypeStruct((1, 1), F32)] * last,
        compiler_params=_params("arbitrary"),
    )(*ys, w_out, x, rows, *([target] if last else []))


def _ln_bwd(dout, x, y, rows, w_out):
    S, D = x.shape
    TS = _tile(S, 256)

    def body(do_ref, x_ref, y_ref, r_ref, w_ref, dxr_ref, dyb_ref, dcat_ref, acc_ref):
        @pl.when(pl.program_id(0) == 0)
        def _():
            acc_ref[...] = jnp.zeros_like(acc_ref)
        g1, lg = r_ref[0:1, :], r_ref[1:2, :]
        yv = y_ref[...]
        z = ALPHA * x_ref[...] + g1 * yv
        mu = jnp.mean(z, axis=-1, keepdims=True)
        zc = z - mu
        var = jnp.mean(zc * zc, axis=-1, keepdims=True)
        rstd = lax.rsqrt(var + LN_EPS)
        xhat = zc * rstd
        do = do_ref[...]
        dxh = do * lg
        dz = rstd * (dxh - jnp.mean(dxh, axis=-1, keepdims=True) - xhat * jnp.mean(dxh * xhat, axis=-1, keepdims=True))
        dxr_ref[...] = ALPHA * dz
        dyb = (g1 * dz).astype(BF16)
        dyb_ref[...] = dyb
        dcat_ref[...] = _dot_nt(dyb, w_ref[...])
        acc_ref[0:1, :] += _colsum(do * xhat)
        acc_ref[1:2, :] += _colsum(do)
        acc_ref[2:3, :] += _colsum(dz * yv)

    spec = pl.BlockSpec((TS, D), lambda i: (i, 0))
    rspec = pl.BlockSpec((8, D), lambda i: (0, 0))
    return _call(
        body, name="ln_bwd", grid=(S // TS,),
        in_specs=[spec, spec, spec, rspec, pl.BlockSpec((D, D), lambda i: (0, 0))],
        out_specs=[spec, spec, spec, rspec],
        out_shape=[jax.ShapeDtypeStruct((S, D), F32), jax.ShapeDtypeStruct((S, D), BF16), jax.ShapeDtypeStruct((S, D), F32),
                   jax.ShapeDtypeStruct((8, D), F32)],
        compiler_params=_params("arbitrary"),
    )(dout, x, y, rows, w_out)


def _dh(dproj, wg, dxres, x, mod, sides):
    S, D = x.shape
    NS = wg.shape[2]
    TS = _tile(S, 256)
    nt = S // TS
    ns = len(sides)

    def body(dp_ref, w_hbm, dxr_ref, x_ref, mod_ref, *rest):
        s_refs, dx_ref, sum_ref, so_refs = rest[:ns], rest[ns], rest[ns + 1], rest[ns + 2:2 * ns + 2]
        w_vmem, sem, send_sems, recv_sems, local_sems = rest[2 * ns + 2:]
        i = pl.program_id(0)
        ix, iy, ic = lax.axis_index("x"), lax.axis_index("y"), lax.axis_index("c")
        me = 4 * ix + 2 * iy + ic

        def exchange(which):
            (_, gather), x_side, o_side = sides[which], s_refs[which], so_refs[which]
            copies = [pltpu.make_async_copy(x_side if gather else x_side.at[me], o_side.at[me], local_sems.at[which])]
            for k in range(1, NDEV):
                px = 1 - ix if k & 4 else ix
                py = 1 - iy if k & 2 else iy
                pc = 1 - ic if k & 1 else ic
                copies.append(pltpu.make_async_remote_copy(
                    src_ref=x_side if gather else x_side.at[4 * px + 2 * py + pc], dst_ref=o_side.at[me],
                    send_sem=send_sems.at[which, k - 1], recv_sem=recv_sems.at[which, k - 1],
                    device_id=(px, py, pc), device_id_type=pl.DeviceIdType.MESH))
            return copies

        @pl.when(i == 0)
        def _():
            load = pltpu.make_async_copy(w_hbm, w_vmem, sem)
            load.start()
            for which in range(ns):
                for cp in exchange(which):
                    cp.start()
            sum_ref[...] = jnp.zeros_like(sum_ref)
            load.wait()

        dh = _dot_nt(dp_ref[:, 0:NS], w_vmem[0])
        for j in range(1, NDEV):
            dh = dh + _dot_nt(dp_ref[:, j * NS:(j + 1) * NS], w_vmem[j])
        dx_ref[...] = dxr_ref[...] + dh * mod_ref[0:1, :]
        sum_ref[0:1, :] += _colsum(dh * x_ref[...])
        sum_ref[1:2, :] += _colsum(dh)

        @pl.when(i == nt - 1)
        def _():
            for which in range(ns):
                for cp in exchange(which):
                    cp.wait()

    spec = pl.BlockSpec((TS, D), lambda i: (i, 0))
    rspec = pl.BlockSpec((8, D), lambda i: (0, 0))
    anyspec = pl.BlockSpec(memory_space=pl.ANY)
    side_shapes = [jax.ShapeDtypeStruct(((NDEV,) + a.shape) if g else a.shape, a.dtype) for a, g in sides]
    return _call(
        body, name="dh", grid=(nt,),
        in_specs=[pl.BlockSpec((TS, NDEV * NS), lambda i: (i, 0)), anyspec, spec, spec, rspec] + [anyspec] * ns,
        out_specs=[spec, rspec] + [anyspec] * ns,
        out_shape=[jax.ShapeDtypeStruct((S, D), F32), jax.ShapeDtypeStruct((8, D), F32)] + side_shapes,
        scratch_shapes=[pltpu.VMEM((NDEV, D, NS), BF16), pltpu.SemaphoreType.DMA, pltpu.SemaphoreType.DMA((max(ns, 1), NDEV - 1)),
                        pltpu.SemaphoreType.DMA((max(ns, 1), NDEV - 1)), pltpu.SemaphoreType.DMA((max(ns, 1),))],
        compiler_params=_params("arbitrary"),
    )(dproj, wg, dxres, x, mod, *[a for a, _ in sides])


def _halo_next(S, TS, BR, k):
    last = S // 8 - 1
    return pl.BlockSpec((8, BR), lambda i: (jnp.minimum((i + 1) * (TS // 8), last), k))


def _conv_a_bwd(proj, dcat, w8, BR):
    S = proj.shape[0]
    TS = _tile(S, 512)
    nt = S // TS

    def body(ab_ref, ac_ref, ax_ref, ag_ref, dy_ref, hc_ref, hx_ref, nab_ref, nag_ref, ndy_ref, w_ref, dp_ref, dw_ref, ext, dext):
        i = pl.program_id(0)

        @pl.when(i == 0)
        def _():
            dw_ref[...] = jnp.zeros_like(dw_ref)

        w0, w1, w2 = w_ref[0:1, :], w_ref[1:2, :], w_ref[2:3, :]
        ab, ac, ax, ag, dy = ab_ref[...], ac_ref[...], ax_ref[...], ag_ref[...], dy_ref[...]
        u = ac * ax
        ext[0:8, :] = jnp.where(i > 0, hc_ref[...] * hx_ref[...], 0.0)
        ext[8:8 + TS, :] = u
        u1, u2 = ext[pl.ds(7, TS), :], ext[pl.ds(6, TS), :]
        conv = u2 * w0 + u1 * w1 + u * w2
        sg = _silu(ag)
        dconv = dy * ab * sg
        dext[0:TS, :] = dconv
        dext[TS:TS + 8, :] = jnp.where(i < nt - 1, ndy_ref[...] * nab_ref[...] * _silu(nag_ref[...]), 0.0)
        du = w2 * dconv + w1 * dext[pl.ds(1, TS), :] + w0 * dext[pl.ds(2, TS), :]
        dp_ref[:, 0:BR] = (dy * conv * sg).astype(BF16)
        dp_ref[:, BR:2 * BR] = (du * ax).astype(BF16)
        dp_ref[:, 2 * BR:3 * BR] = (du * ac).astype(BF16)
        dp_ref[:, 3 * BR:4 * BR] = (dy * ab * conv * _dsilu(ag)).astype(BF16)
        dw_ref[0:1, :] += _colsum(dconv * u2)
        dw_ref[1:2, :] += _colsum(dconv * u1)
        dw_ref[2:3, :] += _colsum(dconv * u)

    rspec = pl.BlockSpec((8, BR), lambda i: (0, 0))
    return _call(
        body, name="conv_a_bwd", grid=(nt,),
        in_specs=[_chunk(TS, BR, 0), _chunk(TS, BR, 1), _chunk(TS, BR, 2), _chunk(TS, BR, 3), _chunk(TS, BR, 0),
                  _halo_prev(TS, BR, 1), _halo_prev(TS, BR, 2), _halo_next(S, TS, BR, 0), _halo_next(S, TS, BR, 3),
                  _halo_next(S, TS, BR, 0), rspec],
        out_specs=[pl.BlockSpec((TS, 4 * BR), lambda i: (i, 0)), rspec],
        out_shape=[jax.ShapeDtypeStruct((S, 12 * BR), BF16), jax.ShapeDtypeStruct((8, BR), F32)],
        scratch_shapes=[pltpu.VMEM((TS + 8, BR), F32), pltpu.VMEM((TS + 8, BR), F32)],
        compiler_params=_params("arbitrary"),
    )(proj, proj, proj, proj, dcat, proj, proj, proj, proj, dcat, w8)


def _attn_mix_bwd(proj, dcat, os_, ls_, head_ones, BR):
    S = proj.shape[0]
    TS = _tile(S, 512)

    def body(dy_ref, g_ref, o1, o2, o3, l1, l2, l3, ones_ref, d1, d2, d3, e1, e2, e3, dg_ref):
        w1, w2, w3 = _mix_weights(l1[...], l2[...], l3[...])
        mix = w1 * o1[...] + w2 * o2[...] + w3 * o3[...]
        g = g_ref[...]
        dy = dy_ref[...]
        dmix = dy * _silu(g)
        dg_ref[...] = dy * mix * _dsilu(g)
        t = jnp.dot(dmix * mix, ones_ref[...], preferred_element_type=F32, precision=lax.Precision.HIGHEST)
        d1[...] = w1 * dmix
        d2[...] = w2 * dmix
        d3[...] = w3 * dmix
        e1[...] = -w1 * t
        e2[...] = -w2 * t
        e3[...] = -w3 * t

    spec = pl.BlockSpec((TS, BR), lambda i: (i, 0))
    outs = _call(
        body, name="attn_mix_bwd", grid=(S // TS,),
        in_specs=[_chunk(TS, BR, 1), _chunk(TS, BR, 7)] + [spec] * 6 + [pl.BlockSpec((BR, BR), lambda i: (0, 0))],
        out_specs=[spec] * 7,
        out_shape=[jax.ShapeDtypeStruct((S, BR), F32)] * 7,
        compiler_params=_params("arbitrary"),
    )(dcat, proj, *os_, *ls_, head_ones)
    return outs[0:3], outs[3:6], outs[6]


def _attn_bwd(proj, do, e, lse, bias, dil, BR):
    S = proj.shape[0]
    HD, U, NU, nbu, hpb, HBK = _attn_geometry(S, BR, dil)
    scale = HD ** -0.5
    combos = [(n, r) for n in range(nbu) for r in range(dil)]

    def body(q_ref, kc_ref, kp_ref, vc_ref, vp_ref, do_ref, e_ref, l_ref, b_ref, dq_ref, dk_ref, dv_ref, db_ref,
             sbuf, dpbuf, pbuf, dsbuf, dkacc, dvacc, nxk, nxv, cark, carv):
        j = pl.program_id(1)
        mu = NU - 1 - j
        band, is_cur = _attn_mask()
        first = jnp.logical_and(band, jnp.logical_or(is_cur, mu > 0))

        @pl.when(j == 0)
        def _():
            cark[...] = jnp.zeros_like(cark)
            carv[...] = jnp.zeros_like(carv)
            db_ref[...] = jnp.zeros_like(db_ref)

        for ref in (dkacc, dvacc, nxk, nxv):
            ref[...] = jnp.zeros_like(ref)

        def keys(ref_c, ref_p, n, r):
            prev = ref_c[_attn_rows(n - 1, r, dil), :] if n > 0 else ref_p[_attn_rows(nbu - 1, r, dil), :]
            return jnp.concatenate([prev, ref_c[_attn_rows(n, r, dil), :]], axis=0).astype(BF16)

        for c, (n, r) in enumerate(combos):
            rows = _attn_rows(n, r, dil)
            q = (q_ref[rows, :] * scale).astype(BF16)
            dob = do_ref[rows, :].astype(BF16)
            kk, vv = keys(kc_ref, kp_ref, n, r), keys(vc_ref, vp_ref, n, r)
            for h in range(hpb):
                sl = slice(h * HD, (h + 1) * HD)
                sbuf[c * hpb + h] = _dot_nt(q[:, sl], kk[:, sl])
                dpbuf[c * hpb + h] = _dot_nt(dob[:, sl], vv[:, sl])
        dbs = [None] * hpb
        for c, (n, r) in enumerate(combos):
            rows = _attn_rows(n, r, dil)
            lse, ev = l_ref[rows, :], e_ref[rows, :]
            for h in range(hpb):
                one = slice(h * HD, h * HD + 1)
                s = jnp.where(band if n > 0 else first, sbuf[c * hpb + h] + b_ref[h], NEG)
                p = jnp.exp(s - lse[:, one])
                ds = p * (dpbuf[c * hpb + h] + ev[:, one])
                pbuf[c * hpb + h] = p.astype(BF16)
                dsbuf[c * hpb + h] = ds.astype(BF16)
                dbs[h] = ds if dbs[h] is None else dbs[h] + ds
        for h in range(hpb):
            db_ref[h] += dbs[h]
        for c, (n, r) in enumerate(combos):
            rows = _attn_rows(n, r, dil)
            q = (q_ref[rows, :] * scale).astype(BF16)
            dob = do_ref[rows, :].astype(BF16)
            kk = keys(kc_ref, kp_ref, n, r)
            dq, dkk, dvv = [], [], []
            for h in range(hpb):
                sl = slice(h * HD, (h + 1) * HD)
                dsb = dsbuf[c * hpb + h]
                dq.append(_dot(dsb, kk[:, sl]) * scale)
                dkk.append(_dot_tn(dsb, q[:, sl]))
                dvv.append(_dot_tn(pbuf[c * hpb + h], dob[:, sl]))
            dq_ref[rows, :] = jnp.concatenate(dq, axis=1)
            dkk, dvv = jnp.concatenate(dkk, axis=1), jnp.concatenate(dvv, axis=1)
            dkacc[rows, :] += dkk[BLK:2 * BLK, :]
            dvacc[rows, :] += dvv[BLK:2 * BLK, :]
            if n > 0:
                prow = _attn_rows(n - 1, r, dil)
                dkacc[prow, :] += dkk[0:BLK, :]
                dvacc[prow, :] += dvv[0:BLK, :]
            else:
                prow = _attn_rows(nbu - 1, r, dil)
                nxk[prow, :] += dkk[0:BLK, :]
                nxv[prow, :] += dvv[0:BLK, :]
        dk_ref[...] = dkacc[...] + cark[...]
        dv_ref[...] = dvacc[...] + carv[...]
        cark[...] = nxk[...]
        carv[...] = nxv[...]

    def cur(c):
        return pl.BlockSpec((U, ATT_LANES), lambda hb, j: (NU - 1 - j, c * HBK + hb))

    def prev(c):
        return pl.BlockSpec((U, ATT_LANES), lambda hb, j: (jnp.maximum(NU - 2 - j, 0), c * HBK + hb))

    own = pl.BlockSpec((U, ATT_LANES), lambda hb, j: (NU - 1 - j, hb))
    bspec = pl.BlockSpec((hpb, BLK, 2 * BLK), lambda hb, j: (hb, 0, 0))
    nhb = len(combos) * hpb
    unit = pltpu.VMEM((U, ATT_LANES), F32)
    return _call(
        body, name="attn_bwd_d%d" % dil, grid=(HBK, NU),
        in_specs=[cur(4), cur(5), prev(5), cur(6), prev(6), own, own, own, bspec],
        out_specs=[own, own, own, bspec],
        out_shape=[jax.ShapeDtypeStruct((S, BR), F32)] * 3 + [jax.ShapeDtypeStruct((8, BLK, 2 * BLK), F32)],
        scratch_shapes=[pltpu.VMEM((nhb, BLK, 2 * BLK), F32)] * 2 + [pltpu.VMEM((nhb, BLK, 2 * BLK), BF16)] * 2 + [unit] * 6,
        compiler_params=_params("arbitrary", "arbitrary"),
    )(proj, proj, proj, proj, proj, do, e, lse, bias)


def _attn_dsum(ds, dbg, dproj, BR):
    S = dbg.shape[0]
    TS = _tile(S, 512)

    def body(*refs):
        o_ref = refs[11]
        for k in range(3):
            o_ref[:, k * BR:(k + 1) * BR] = (refs[k][...] + refs[3 + k][...] + refs[6 + k][...]).astype(BF16)
        o_ref[:, 3 * BR:4 * BR] = refs[9][...].astype(BF16)

    spec = pl.BlockSpec((TS, BR), lambda i: (i, 0))
    return _call(
        body, name="attn_dsum", grid=(S // TS,),
        in_specs=[spec] * 10 + [pl.BlockSpec(memory_space=pl.ANY)],
        out_specs=pl.BlockSpec((TS, 4 * BR), lambda i: (i, 1)),
        out_shape=jax.ShapeDtypeStruct(dproj.shape, BF16),
        input_output_aliases={10: 0},
        compiler_params=_params("arbitrary"),
    )(*[t for trip in ds for t in trip], dbg, dproj)


def _lru_bwd(proj, h, dcat, rows, wa, wx, dproj, BR):
    S = proj.shape[0]
    TS = _tile(S, 256)
    PAD = TS // 2
    nt = S // TS

    def body(cx_ref, hx_ref, cg_ref, h_ref, hh_ref, dy_ref, r_ref, wa_ref, wx_ref, _,
             dp_ref, sum_ref, dwa_ref, dwx_ref, ext, hext, aext, dext, sa, sb, carry):
        i = pl.program_id(0)
        ti = nt - 1 - i

        @pl.when(i == 0)
        def _():
            sa[TS:TS + PAD, :] = jnp.ones((PAD, BR), F32)
            sb[TS:TS + PAD, :] = jnp.zeros((PAD, BR), F32)
            carry[...] = jnp.zeros_like(carry)
            dext[TS:TS + 8, :] = jnp.zeros((8, BR), F32)
            sum_ref[...] = jnp.zeros_like(sum_ref)
            dwa_ref[...] = jnp.zeros_like(dwa_ref)
            dwx_ref[...] = jnp.zeros_like(dwx_ref)

        w0, w1, w2, w3 = r_ref[0:1, :], r_ref[1:2, :], r_ref[2:3, :], r_ref[3:4, :]
        sp = r_ref[7:8, :]
        cx = cx_ref[...]
        ext[0:8, :] = jnp.where(ti > 0, hx_ref[...], 0.0)
        ext[8:8 + TS, :] = cx
        x3, x2, x1 = ext[pl.ds(5, TS), :], ext[pl.ds(6, TS), :], ext[pl.ds(7, TS), :]
        xc = x3 * w0 + x2 * w1 + x1 * w2 + cx * w3 + r_ref[4:5, :]
        wa_, wx_ = wa_ref[...], wx_ref[...]
        r, ig, a, mult = _lru_gates(xc, wa_, wx_, r_ref[5:6, :], r_ref[6:7, :], sp)
        cg, dy, hv = cg_ref[...], dy_ref[...], h_ref[...]
        dp_ref[:, BR:2 * BR] = (dy * hv * _dsilu(cg)).astype(BF16)
        aext[0:TS, :] = a
        aext[TS:TS + 8, :] = jnp.broadcast_to(carry[0:1, :], (8, BR))
        sa[0:TS, :] = aext[pl.ds(1, TS), :]
        sb[0:TS, :] = dy * _silu(cg)
        d = 1
        while d < TS:
            A, B = sa[0:TS, :], sb[0:TS, :]
            As, Bs = sa[pl.ds(d, TS), :], sb[pl.ds(d, TS), :]
            sb[0:TS, :] = B + A * Bs
            sa[0:TS, :] = A * As
            d *= 2
        gh = sb[0:TS, :] + sa[0:TS, :] * carry[1:2, :]
        carry[0:1, :] = a[0:1, :]
        carry[1:2, :] = gh[0:1, :]
        hext[0:8, :] = jnp.where(ti > 0, hh_ref[...], 0.0)
        hext[8:8 + TS, :] = hv
        da = gh * hext[pl.ds(7, TS), :]
        dmult = gh * ig * xc
        dig = gh * mult * xc
        dxc = gh * mult * ig
        dla = da * a - dmult * a * a / mult
        dpr = dla * (-LRU_C) * sp * r * (1.0 - r)
        dpi = dig * ig * (1.0 - ig)
        xb, dprb, dpib = xc.astype(BF16), dpr.astype(BF16), dpi.astype(BF16)
        dwa_ref[...] += _dot_tn(xb, dprb)
        dwx_ref[...] += _dot_tn(xb, dpib)
        dxc = dxc + _dot_nt(dprb, wa_) + _dot_nt(dpib, wx_)
        sum_ref[0:1, :] += _colsum(dxc * x3)
        sum_ref[1:2, :] += _colsum(dxc * x2)
        sum_ref[2:3, :] += _colsum(dxc * x1)
        sum_ref[3:4, :] += _colsum(dxc * cx)
        sum_ref[4:5, :] += _colsum(dxc)
        sum_ref[5:6, :] += _colsum(dpr)
        sum_ref[6:7, :] += _colsum(dpi)
        sum_ref[7:8, :] += _colsum(dla * (-LRU_C) * r)
        dext[0:TS, :] = dxc
        dcx = w3 * dxc + w2 * dext[pl.ds(1, TS), :] + w1 * dext[pl.ds(2, TS), :] + w0 * dext[pl.ds(3, TS), :]
        dext[TS:TS + 8, :] = dxc[0:8, :]
        dp_ref[:, 0:BR] = dcx.astype(BF16)

    def rev(k):
        return pl.BlockSpec((TS, BR), lambda i: (nt - 1 - i, k))

    def rev_halo(k):
        return pl.BlockSpec((8, BR), lambda i: (jnp.maximum((nt - 1 - i) * (TS // 8) - 1, 0), k))

    full = pl.BlockSpec((BR, BR), lambda i: (0, 0))
    rspec = pl.BlockSpec((8, BR), lambda i: (0, 0))
    return _call(
        body, name="lru_bwd", grid=(nt,),
        in_specs=[rev(8), rev_halo(8), rev(9), rev(0), rev_halo(0), rev(2), rspec, full, full, pl.BlockSpec(memory_space=pl.ANY)],
        out_specs=[pl.BlockSpec((TS, 2 * BR), lambda i: (nt - 1 - i, 4)), rspec, full, full],
        input_output_aliases={9: 0},
        out_shape=[jax.ShapeDtypeStruct(dproj.shape, BF16), jax.ShapeDtypeStruct((8, BR), F32),
                   jax.ShapeDtypeStruct((BR, BR), F32), jax.ShapeDtypeStruct((BR, BR), F32)],
        scratch_shapes=[pltpu.VMEM((TS + 8, BR), F32)] * 4 + [pltpu.VMEM((TS + PAD, BR), F32)] * 2 + [pltpu.VMEM((8, BR), F32)],
        compiler_params=_params("arbitrary"),
    )(proj, proj, proj, h, h, dcat, rows, wa, wx, dproj)


def _s5_bwd(proj, dcat, xre_all, xim_all, a8, cl, bbre, bbim, cre, cim, rows, wg, dproj, BR):
    S = proj.shape[0]
    W = a8.shape[1]
    TS = _tile(S, 256)
    nt = S // TS
    CB, SB, J = S5_CB, S5_SB, S5_LAGS
    nblk = BR // CB

    def body(u_ref, dg_ref, dy_ref, xre_ref, xim_ref, hre_ref, him_ref, a8_ref, cl_ref, bbre_ref, bbim_ref, cre_ref, cim_ref,
             r_ref, wg_ref, _, dp_ref, sum_ref, dwg_ref, da_ref, dbbre_ref, dbbim_ref, dcre_ref, dcim_ref,
             sre, sim, ere, eim, ypre_s, dext, dylag, carry):
        i = pl.program_id(0)
        ti = nt - 1 - i

        @pl.when(i == 0)
        def _():
            dext[TS:TS + 8, :] = jnp.zeros((8, BR), F32)
            carry[...] = jnp.zeros_like(carry)
            for ref in (sum_ref, dwg_ref, da_ref, dbbre_ref, dbbim_ref, dcre_ref, dcim_ref):
                ref[...] = jnp.zeros_like(ref)

        u, dg, dy = u_ref[...], dg_ref[...], dy_ref[...]
        for m in range(nblk):
            cs, ws = slice(m * CB, (m + 1) * CB), slice(m * SB, (m + 1) * SB)
            ypre_s[:, cs] = (_dot(xre_ref[:, ws].astype(BF16), cre_ref[ws, :]) - _dot(xim_ref[:, ws].astype(BF16), cim_ref[ws, :]))
        ypre = ypre_s[...] + r_ref[0:1, :] * u
        yg = _gelu(ypre)
        wg_ = wg_ref[...]
        s = _sigmoid(_dot(yg.astype(BF16), wg_) + r_ref[1:2, :])
        dgl = dy * _silu(dg)
        dp_ref[:, BR:2 * BR] = (dy * yg * s * _dsilu(dg)).astype(BF16)
        dps = dgl * yg * s * (1.0 - s)
        dpsb = dps.astype(BF16)
        sum_ref[1:2, :] += _colsum(dps)
        dwg_ref[...] += _dot_tn(yg.astype(BF16), dpsb)
        dyp = (dgl * s + _dot_nt(dpsb, wg_)) * _dgelu(ypre)
        sum_ref[0:1, :] += _colsum(dyp * u)
        dext[0:TS, :] = dyp
        for m in range(nblk):
            cs, ws = slice(m * CB, (m + 1) * CB), slice(m * SB, (m + 1) * SB)
            dypb = dext[0:TS, cs].astype(BF16)
            xre, xim = xre_ref[:, ws], xim_ref[:, ws]
            dcre_ref[ws, :] += _dot_tn(xre.astype(BF16), dypb)
            dcim_ref[ws, :] -= _dot_tn(xim.astype(BF16), dypb)
            for j in range(J):
                dylag[:, j * CB:(j + 1) * CB] = dext[pl.ds(j, TS), cs].astype(BF16)
            w = _dot(dylag[...], cl_ref[m])
            sre[...] = w[:, 0:SB]
            sim[...] = w[:, SB:2 * SB]
            ar = jnp.broadcast_to(a8_ref[0:1, ws], (8, SB))
            ai = jnp.broadcast_to(a8_ref[1:2, ws], (8, SB))
            gr, gi = carry[0:8, ws], carry[8:16, ws]
            for g in reversed(range(TS // 8)):
                rg = slice(8 * g, 8 * g + 8)
                gr, gi = sre[rg, :] + ar * gr + ai * gi, sim[rg, :] + ar * gi - ai * gr
                sre[rg, :] = gr
                sim[rg, :] = gi
            carry[0:8, ws] = gr
            carry[8:16, ws] = gi
            gre, gim = sre[...], sim[...]
            ere[0:8, :] = jnp.where(ti > 0, hre_ref[:, ws], 0.0)
            eim[0:8, :] = jnp.where(ti > 0, him_ref[:, ws], 0.0)
            ere[8:8 + TS, :] = xre
            eim[8:8 + TS, :] = xim
            xpr, xpi = ere[pl.ds(7, TS), :], eim[pl.ds(7, TS), :]
            da_ref[0:1, ws] += _colsum(gre * xpr + gim * xpi)
            da_ref[1:2, ws] += _colsum(gim * xpr - gre * xpi)
            greb, gimb = gre.astype(BF16), gim.astype(BF16)
            ub = u_ref[:, cs].astype(BF16)
            dbbre_ref[cs, :] += _dot_tn(ub, greb)
            dbbim_ref[cs, :] += _dot_tn(ub, gimb)
            du = _dot_nt(greb, bbre_ref[cs, :]) + _dot_nt(gimb, bbim_ref[cs, :]) + r_ref[0:1, cs] * dext[0:TS, cs]
            dp_ref[:, cs] = du.astype(BF16)
        dext[TS:TS + 8, :] = dext[0:8, :]

    def rev(width, k):
        return pl.BlockSpec((TS, width), lambda i: (nt - 1 - i, k))

    def rev_halo(width):
        return pl.BlockSpec((8, width), lambda i: (jnp.maximum((nt - 1 - i) * (TS // 8) - 1, 0), 0))

    def const(shape):
        return pl.BlockSpec(shape, lambda i: (0,) * len(shape))

    return _call(
        body, name="s5_bwd", grid=(nt,),
        in_specs=[rev(BR, 10), rev(BR, 11), rev(BR, 3), rev(W, 0), rev(W, 0), rev_halo(W), rev_halo(W),
                  const((8, W)), const((nblk, J * CB, 2 * SB)), const((BR, SB)), const((BR, SB)), const((W, CB)), const((W, CB)),
                  const((8, BR)), const((BR, BR)), pl.BlockSpec(memory_space=pl.ANY)],
        input_output_aliases={15: 0},
        out_specs=[pl.BlockSpec((TS, 2 * BR), lambda i: (nt - 1 - i, 5)), const((8, BR)), const((BR, BR)), const((8, W)),
                   const((BR, SB)), const((BR, SB)), const((W, CB)), const((W, CB))],
        out_shape=[jax.ShapeDtypeStruct(dproj.shape, BF16), jax.ShapeDtypeStruct((8, BR), F32), jax.ShapeDtypeStruct((BR, BR), F32),
                   jax.ShapeDtypeStruct((8, W), F32), jax.ShapeDtypeStruct((BR, SB), F32), jax.ShapeDtypeStruct((BR, SB), F32),
                   jax.ShapeDtypeStruct((W, CB), F32), jax.ShapeDtypeStruct((W, CB), F32)],
        scratch_shapes=[pltpu.VMEM((TS, SB), F32)] * 2 + [pltpu.VMEM((TS + 8, SB), F32)] * 2
        + [pltpu.VMEM((TS, BR), F32), pltpu.VMEM((TS + 8, BR), F32), pltpu.VMEM((TS, J * CB), BF16), pltpu.VMEM((16, W), F32)],
        compiler_params=_params("arbitrary"),
    )(proj, proj, dcat, xre_all, xim_all, xre_all, xim_all, a8, cl, bbre, bbim, cre, cim, rows, wg, dproj)


_WEIGHTS = ['rel_bias', 'w_ada', 'b_ada', 'w_in', 'conv_a', 'conv_c', 'conv_c_b', 'lru_wa', 'lru_ba', 'lru_wx', 'lru_bx',
            'lru_lambda', 's5_lam_re', 's5_lam_im', 's5_log_dt', 's5_b_re', 's5_b_im', 's5_c_re', 's5_c_im', 's5_d',
            's5_w_glu', 's5_b_glu', 'w_out', 'ln_g', 'ln_b']
_LAYER_SMALL = ['conv_a', 'conv_c', 'conv_c_b', 'lru_wa', 'lru_ba', 'lru_wx', 'lru_bx', 'lru_lambda', 's5_lam_re', 's5_lam_im',
                's5_log_dt', 's5_b_re', 's5_b_im', 's5_c_re', 's5_c_im', 's5_d', 's5_b_glu']
_SMALL = ['rel_bias'] + _LAYER_SMALL + ['ln_g', 'ln_b']


def _t5_bucket(dist):
    max_exact = REL_BUCKETS // 2
    nf = jnp.maximum(dist, 1).astype(F32)
    large = max_exact + (jnp.log(nf / max_exact) / math.log(REL_MAX_DIST / max_exact)
                         * (REL_BUCKETS - max_exact)).astype(jnp.int32)
    large = jnp.minimum(large, REL_BUCKETS - 1)
    return jnp.where(dist < max_exact, dist, large)


def _bias_tables(rel_bias):
    i = jnp.arange(BLK)[:, None]
    j = jnp.arange(2 * BLK)[None, :]
    delta = i + BLK - j
    out = []
    for window, dil in DILATIONS:
        bucket = _t5_bucket(jnp.clip(delta, 0, window // dil) * dil)
        onehot = (bucket[:, :, None] == jnp.arange(REL_BUCKETS)[None, None, :]).astype(F32)
        out.append(jnp.einsum('ijb,bh->hij', onehot, rel_bias, precision=lax.Precision.HIGHEST))
    return jnp.stack(out)


def _prep_layer(q):
    eye8 = jnp.eye(8, dtype=F32)
    G = q['s5_lam_re'].shape[0]
    nblk = G // 8

    def block_diag(w):
        hd = w.shape[1]
        return (w[:, :, None, :] * eye8[:, None, :, None]).reshape(8 * hd, 8 * hd)

    def compact_b(bb):
        t = jnp.transpose(bb.reshape(nblk, 8, S5_STATE, S5_CH), (0, 1, 3, 2))
        return (t[:, :, :, None, :] * eye8[None, :, None, :, None]).reshape(G * S5_CH, 8 * S5_STATE)

    def compact_c(cc):
        t = jnp.transpose(cc.reshape(nblk, 8, S5_CH, S5_STATE), (0, 1, 3, 2))
        return (t[:, :, :, None, :] * eye8[None, :, None, :, None]).reshape(G * S5_STATE, 8 * S5_CH)

    lam_re, lam_im = q['s5_lam_re'], q['s5_lam_im']
    dt = jnp.exp(q['s5_log_dt'])[:, None]
    mag = jnp.exp(lam_re * dt)
    ab_re = mag * jnp.cos(lam_im * dt)
    ab_im = mag * jnp.sin(lam_im * dt)
    den = lam_re * lam_re + lam_im * lam_im
    f_re = ((ab_re - 1.0) * lam_re + ab_im * lam_im) / den
    f_im = (ab_im * lam_re - (ab_re - 1.0) * lam_im) / den
    b_re, b_im = q['s5_b_re'], q['s5_b_im']
    bb_re = f_re[..., None] * b_re - f_im[..., None] * b_im
    bb_im = f_re[..., None] * b_im + f_im[..., None] * b_re
    return dict(
        conv_a=q['conv_a'], conv_c=q['conv_c'], conv_c_b=q['conv_c_b'], lru_ba=q['lru_ba'], lru_bx=q['lru_bx'],
        sp=jax.nn.softplus(-q['lru_lambda']), wa=block_diag(q['lru_wa']), wx=block_diag(q['lru_wx']),
        ar=ab_re.reshape(-1), ai=ab_im.reshape(-1), bbre=compact_b(bb_re), bbim=compact_b(bb_im),
        cre=compact_c(q['s5_c_re']), cim=compact_c(q['s5_c_im']), s5_d=q['s5_d'], s5_b_glu=q['s5_b_glu'])


def _s5_lag_weights(kp):
    J, CB, SB = S5_LAGS, S5_CB, S5_SB
    ar, ai = kp['ar'], kp['ai']
    W = ar.shape[0]
    nblk = W // SB
    cr, ci = jnp.ones_like(ar), jnp.zeros_like(ai)
    pows = []
    for _ in range(J + 1):
        pows.append((cr, ci))
        cr, ci = cr * ar - ci * ai, cr * ai + ci * ar
    pwr = jnp.transpose(jnp.stack([p[0] for p in pows[:J]]).reshape(J, nblk, SB), (1, 0, 2))[:, :, None, :]
    pwi = jnp.transpose(jnp.stack([p[1] for p in pows[:J]]).reshape(J, nblk, SB), (1, 0, 2))[:, :, None, :]

    def lagged(re, im, sign):
        out = jnp.concatenate([re * pwr - sign * im * pwi, sign * re * pwi + im * pwr], axis=-1)
        return out.reshape(nblk, J * CB, 2 * SB).astype(BF16)

    bbl = lagged(kp['bbre'].reshape(nblk, 1, CB, SB), kp['bbim'].reshape(nblk, 1, CB, SB), 1.0)
    c0r = jnp.transpose(kp['cre'].reshape(nblk, SB, CB), (0, 2, 1))[:, None]
    c0i = -jnp.transpose(kp['cim'].reshape(nblk, SB, CB), (0, 2, 1))[:, None]
    cl = lagged(c0r, c0i, -1.0)
    return _rows8([pows[J][0], pows[J][1]], W), bbl, cl


def _rows8(vecs, width):
    rows = [v.reshape(1, width).astype(F32) for v in vecs]
    return jnp.concatenate(rows + [jnp.zeros((8 - len(rows), width), F32)], axis=0)


def kernel(x, c, rel_bias, w_ada, b_ada, w_in, conv_a, conv_c, conv_c_b, lru_wa, lru_ba, lru_wx, lru_bx, lru_lambda, s5_lam_re, s5_lam_im, s5_log_dt, s5_b_re, s5_b_im, s5_c_re, s5_c_im, s5_d, s5_w_glu, s5_b_glu, w_out, ln_g, ln_b, loss_target, m_rel_bias, m_w_ada, m_b_ada, m_w_in, m_conv_a, m_conv_c, m_conv_c_b, m_lru_wa, m_lru_ba, m_lru_wx, m_lru_bx, m_lru_lambda, m_s5_lam_re, m_s5_lam_im, m_s5_log_dt, m_s5_b_re, m_s5_b_im, m_s5_c_re, m_s5_c_im, m_s5_d, m_s5_w_glu, m_s5_b_glu, m_w_out, m_ln_g, m_ln_b, v_rel_bias, v_w_ada, v_b_ada, v_w_in, v_conv_a, v_conv_c, v_conv_c_b, v_lru_wa, v_lru_ba, v_lru_wx, v_lru_bx, v_lru_lambda, v_s5_lam_re, v_s5_lam_im, v_s5_log_dt, v_s5_b_re, v_s5_b_im, v_s5_c_re, v_s5_c_im, v_s5_d, v_s5_w_glu, v_s5_b_glu, v_w_out, v_ln_g, v_ln_b):
    a = dict(locals())
    me = 4 * lax.axis_index("x") + 2 * lax.axis_index("y") + lax.axis_index("c")
    x0, target = x[0], loss_target[0]
    S, D = x0.shape
    BR = D // 4
    NS = w_in.shape[2]
    SH = BR // NDEV
    assert S % (BLK * DILATIONS[-1][1]) == 0 and BR % (8 * S5_CH) == 0

    small = _all_gather(_pack([c, conv_a, conv_c]), "gather_small")
    per_dev = [_unpack(small[d], [c.shape, conv_a.shape, conv_c.shape]) for d in range(NDEV)]
    c_all = jnp.concatenate([p[0] for p in per_dev], axis=0)
    conv_a_full = jnp.concatenate([p[1] for p in per_dev], axis=2)
    conv_c_full = jnp.concatenate([p[2] for p in per_dev], axis=2)
    me1 = me.reshape(1).astype(jnp.int32)
    w_in_bf, w_out_bf, w_glu_bf = w_in.astype(BF16), w_out.astype(BF16), s5_w_glu.astype(BF16)

    b_cols = lax.dynamic_slice(b_ada, (0, me * NS), (DEPTH, NS)).reshape(DEPTH, 1, NS)
    ada_all = _all_gather(_ada_cols(c_all, w_ada, b_cols), "gather_ada")
    ada_me = lax.dynamic_index_in_dim(ada_all, me, axis=2, keepdims=False)
    ada_me = jnp.transpose(ada_me, (1, 0, 2)).reshape(DEPTH, 3 * D)
    shift, scale, gate = ada_me[:, :D], ada_me[:, D:2 * D], ada_me[:, 2 * D:]

    bias_tabs, bias_pull = jax.vjp(_bias_tables, rel_bias)
    HD = BR // 8
    head_ones = jnp.kron(jnp.eye(8, dtype=F32), jnp.ones((HD, HD), F32))

    saved = []
    xl = x0
    for l in range(DEPTH):
        q = {n: a[n][l] for n in _LAYER_SMALL}
        q['conv_a'], q['conv_c'] = conv_a_full[l], conv_c_full[l]
        kp, pull = jax.vjp(_prep_layer, q)
        a8, bbl, cl = _s5_lag_weights(kp)
        mod = _rows8([1.0 + scale[l], shift[l]], D)
        arows = _rows8(list(kp['conv_a']), BR)
        lrows = _rows8(list(kp['conv_c']) + [kp['conv_c_b'], kp['lru_ba'], kp['lru_bx'], kp['sp']], BR)
        srows = _rows8([kp['s5_d'], kp['s5_b_glu']], BR)
        orows = _rows8([1.0 + gate[l], ln_g[l], ln_b[l]], D)
        wa, wx = kp['wa'].astype(BF16), kp['wx'].astype(BF16)
        s5w = [kp[n].astype(BF16) for n in ('bbre', 'bbim', 'cre', 'cim')]

        hbf = _modulate(xl, mod)
        proj, w_in_l, w_out_l, w_glu_l = _proj_gather(hbf, w_in_bf[l], [w_out_bf[l], w_glu_bf[l]], me1)
        w_out_l, w_glu_l = w_out_l.reshape(D, D), w_glu_l.reshape(BR, BR)
        ya = _conv_a_fwd(proj, arows, BR)
        os_, ls_ = [], []
        for g, (_, dil) in enumerate(DILATIONS):
            o, lse = _attn_fwd(proj, bias_tabs[g], dil, BR)
            os_.append(o)
            ls_.append(lse)
        yb = _attn_mix_fwd(proj, os_, ls_, BR)
        yc, hs = _lru_fwd(proj, lrows, wa, wx, BR)
        yd, xre, xim = _s5_fwd(proj, a8, bbl, s5w[2], s5w[3], srows, w_glu_l, BR)
        if l < DEPTH - 1:
            xn, y, cat = _out_ln([ya, yb, yc, yd], w_out_l, xl, orows)
        else:
            xn, y, cat, loss_local = _out_ln([ya, yb, yc, yd], w_out_l, xl, orows, target)
        saved.append(dict(x=xl, proj=proj, hbf=hbf, os=os_, ls=ls_, hs=hs, xre=xre, xim=xim, y=y, cat=cat, pull=pull,
                          mod=mod, arows=arows, lrows=lrows, srows=srows, orows=orows, wa=wa, wx=wx, s5w=s5w,
                          qw=(a8, cl), w_in=w_in_l, w_out=w_out_l, w_glu=w_glu_l))
        xl = xn

    dout = xl

    dbias = jnp.zeros_like(bias_tabs)
    lgrads, dada, dw_in_parts, dw_out_parts, glu_parts = [None] * DEPTH, [None] * DEPTH, [None] * DEPTH, [None] * DEPTH, [None] * DEPTH
    for l in reversed(range(DEPTH)):
        sv = saved[l]
        proj = sv['proj']
        dxres, dyb, dcat, lnsum = _ln_bwd(dout, sv['x'], sv['y'], sv['orows'], sv['w_out'])
        dw_out_parts[l] = _dw_scatter(sv['cat'], dyb, D // NDEV, True, me1, "dw_out_scatter")
        dproj, asum = _conv_a_bwd(proj, dcat, sv['arows'], BR)
        dos, es, dbg = _attn_mix_bwd(proj, dcat, sv['os'], sv['ls'], head_ones, BR)
        dqkv, dbs = [], []
        for g, (_, dil) in enumerate(DILATIONS):
            dq_, dk_, dv_, db_ = _attn_bwd(proj, dos[g], es[g], sv['ls'][g], bias_tabs[g], dil, BR)
            dqkv.append((dq_, dk_, dv_))
            dbs.append(db_)
        dbias = dbias + jnp.stack(dbs)
        dproj = _attn_dsum(dqkv, dbg, dproj, BR)
        dproj, lsum, dwa, dwx = _lru_bwd(proj, sv['hs'], dcat, sv['lrows'], sv['wa'], sv['wx'], dproj, BR)
        dproj, ssum, dwg, da_, dbbre, dbbim, dcre, dcim = _s5_bwd(
            proj, dcat, sv['xre'], sv['xim'], *sv['qw'], *sv['s5w'], sv['srows'], sv['w_glu'], dproj, BR)
        dkp = dict(conv_a=asum[0:3], conv_c=lsum[0:4], conv_c_b=lsum[4], lru_ba=lsum[5], lru_bx=lsum[6], sp=lsum[7],
                   wa=dwa, wx=dwx, ar=da_[0], ai=da_[1], bbre=dbbre, bbim=dbbim, cre=dcre, cim=dcim,
                   s5_d=ssum[0], s5_b_glu=ssum[1])
        lgrads[l] = dict(sv['pull'](dkp)[0], ln_g=lnsum[0], ln_b=lnsum[1])
        sides = [(dwg.reshape(NDEV, BR // NDEV, BR), False)]
        if l == 0:
            local = {'rel_bias': bias_pull(dbias)[0], 'loss': loss_local.reshape(1)}
            for n in _LAYER_SMALL + ['ln_g', 'ln_b']:
                local[n] = jnp.stack([lgrads[k][n] for k in range(DEPTH)])
            packed = _SMALL + ['loss']
            sides.append((_pack([local[n] for n in packed]), True))
        dout, msum, glu_parts[l], *gathered = _dh(dproj, sv['w_in'], dxres, sv['x'], sv['mod'], sides)
        dw_in_parts[l] = _dw_scatter(sv['hbf'], dproj, NS, False, me1, "dw_in_scatter")
        dada[l] = jnp.concatenate([msum[1], msum[0], lnsum[2]])
    grad_x = dout[None]

    out = {}

    def put(name, res):
        out[name] = res

    put('w_in', _adamw(dw_in_parts, w_in, m_w_in, v_w_in, "adamw_w_in"))
    put('w_out', _adamw(dw_out_parts, w_out, m_w_out, v_w_out, "adamw_w_out"))
    put('s5_w_glu', _adamw(glu_parts, s5_w_glu, m_s5_w_glu, v_s5_w_glu, "adamw_w_glu"))

    dada_all = _all_gather(jnp.stack(dada), "gather_dada")
    dada_cols = jnp.transpose(lax.dynamic_slice(dada_all, (0, 0, me * NS), (NDEV, DEPTH, NS)), (1, 0, 2))
    gw_ada, gb_ada = _ada_grads(c_all, dada_all, dada_cols)
    put('w_ada', _adamw([gw_ada[l] for l in range(DEPTH)], w_ada, m_w_ada, v_w_ada, "adamw_w_ada"))

    shapes = [local[n].shape for n in packed]
    gs = dict(zip(packed, _unpack(_sum8(gathered[0], "sum_small_grads"), shapes)))
    loss = gs['loss'][0]
    gs['conv_a'] = lax.dynamic_slice_in_dim(gs['conv_a'], me * SH, SH, axis=2)
    gs['conv_c'] = lax.dynamic_slice_in_dim(gs['conv_c'], me * SH, SH, axis=2)
    gs['b_ada'] = gb_ada
    names = _SMALL + ['b_ada']
    shapes = [a[n].shape for n in names]
    res = _adamw([_pack([gs[n] for n in names])], _pack([a[n] for n in names])[None], _pack([a['m_' + n] for n in names])[None],
                 _pack([a['v_' + n] for n in names])[None], "adamw_small")
    unpacked = [_unpack(r[0], shapes) for r in res]
    for k, n in enumerate(names):
        put(n, tuple(unpacked[j][k] for j in range(4)))

    return (loss, grad_x, *[out[n][0] for n in _WEIGHTS], *[out[n][1] for n in _WEIGHTS],
            *[out[n][2] for n in _WEIGHTS], *[out[n][3] for n in _WEIGHTS])
```

```python
import functools
import math

import jax
import jax.numpy as jnp
from jax import lax
from jax.experimental import pallas as pl
from jax.experimental.pallas import tpu as pltpu

F32 = jnp.float32
BF16 = jnp.bfloat16

NDEV = 8
DEPTH = 2
BLK = 128
DILATIONS = ((128, 1), (512, 4), (2048, 16))
REL_BUCKETS = 32
REL_MAX_DIST = 2048
LRU_C = 8.0
S5_CH = 16
S5_STATE = 64
ALPHA = (2 * DEPTH) ** 0.25
LN_EPS = 1e-5
ADAM_LR, ADAM_B1, ADAM_B2, ADAM_EPS, ADAM_WD, ADAM_STEP = 0.001, 0.9, 0.999, 1e-08, 0.01, 10
NEG = -1e30
VMEM_LIMIT = 56 * 1024 * 1024


def _call(body, **kw):
    return pl.pallas_call(body, **kw)


def _params(*sem):
    return pltpu.CompilerParams(dimension_semantics=sem, vmem_limit_bytes=VMEM_LIMIT)


def _sigmoid(x):
    return 1.0 / (1.0 + jnp.exp(-x))


def _silu(x):
    return x * _sigmoid(x)


def _dsilu(x):
    s = _sigmoid(x)
    return s * (1.0 + x * (1.0 - s))


_GC = math.sqrt(2.0 / math.pi)


def _gelu(x):
    return 0.5 * x * (1.0 + jnp.tanh(_GC * (x + 0.044715 * x * x * x)))


def _dgelu(x):
    t = jnp.tanh(_GC * (x + 0.044715 * x * x * x))
    return 0.5 * (1.0 + t) + 0.5 * x * (1.0 - t * t) * _GC * (1.0 + 3.0 * 0.044715 * x * x)


def _expm1(x):
    series = x * (1.0 + x * (1.0 / 2) * (1.0 + x * (1.0 / 3) * (1.0 + x * (1.0 / 4) * (1.0 + x * (1.0 / 5) * (1.0 + x * (1.0 / 6))))))
    return jnp.where(jnp.abs(x) < 0.25, series, jnp.exp(x) - 1.0)


def _dot(a, b):
    return jnp.dot(a, b, preferred_element_type=F32)


def _dot_nt(a, b):
    return lax.dot_general(a, b, (((1,), (1,)), ((), ())), preferred_element_type=F32)


def _dot_tn(a, b):
    return lax.dot_general(a, b, (((0,), (0,)), ((), ())), preferred_element_type=F32)


def _colsum(v):
    return jnp.sum(v, axis=0, keepdims=True)


def _all_gather(x, name):
    def body(x_ref, o_ref, send_sems, recv_sems, local_sem):
        ix, iy, ic = lax.axis_index("x"), lax.axis_index("y"), lax.axis_index("c")
        me, sibling = (ix, iy, ic), (ix, iy, 1 - ic)
        chips = [(1 - ix, iy), (ix, 1 - iy), (1 - ix, 1 - iy)]

        def slot(px, py, pc):
            return o_ref.at[4 * px + 2 * py + pc]

        def copy(k, block, to, src=None):
            return pltpu.make_async_remote_copy(
                src_ref=slot(*block) if src is None else src, dst_ref=slot(*block), send_sem=send_sems.at[k],
                recv_sem=recv_sems.at[k], device_id=to, device_id_type=pl.DeviceIdType.MESH)

        mine = pltpu.make_async_copy(x_ref, slot(*me), local_sem)
        mine.start()
        first = [copy(0, me, sibling, src=x_ref)] + [copy(1 + j, me, (*chip, ic), src=x_ref) for j, chip in enumerate(chips)]
        for cp in first:
            cp.start()
        passed = [copy(4 + j, (*chip, ic), sibling) for j, chip in enumerate(chips)]
        for j, chip in enumerate(chips):
            copy(1 + j, (*chip, ic), me).wait_recv()
            passed[j].start()
        copy(0, sibling, me).wait_recv()
        for j, chip in enumerate(chips):
            copy(4 + j, (*chip, 1 - ic), me).wait_recv()
        for cp in first + passed:
            cp.wait_send()
        mine.wait()

    return _call(
        body, name=name,
        out_shape=jax.ShapeDtypeStruct((NDEV,) + x.shape, x.dtype),
        in_specs=[pl.BlockSpec(memory_space=pl.ANY)],
        out_specs=pl.BlockSpec(memory_space=pl.ANY),
        scratch_shapes=[pltpu.SemaphoreType.DMA((NDEV - 1,)), pltpu.SemaphoreType.DMA((NDEV - 1,)), pltpu.SemaphoreType.DMA],
    )(x)


def _pack(arrs):
    parts = []
    for a in arrs:
        n = math.prod(a.shape)
        parts.append(jnp.pad(a.reshape(-1).astype(F32), (0, _pack_rows(n) * 128 - n)).reshape(-1, 128))
    rows = sum(p.shape[0] for p in parts)
    total = rows if rows <= 512 else -(-rows // 512) * 512
    if total > rows:
        parts.append(jnp.zeros((total - rows, 128), F32))
    return jnp.concatenate(parts, axis=0)


def _pack_rows(n):
    return -(-n // 1024) * 8


def _unpack(packed, shapes):
    out, row = [], 0
    for s in shapes:
        n = math.prod(s)
        rows = _pack_rows(n)
        out.append(packed[row:row + rows].reshape(-1)[:n].reshape(s))
        row += rows
    return out


def _sum8(parts, name):
    _, R, C = parts.shape
    TR = R
    for cand in (512, 256, 128, 64, 32, 16, 8):
        if R % cand == 0:
            TR = cand
            break

    def body(p_ref, o_ref):
        acc = p_ref[0]
        for d in range(1, NDEV):
            acc = acc + p_ref[d]
        o_ref[...] = acc

    return _call(
        body, name=name, grid=(R // TR,),
        in_specs=[pl.BlockSpec((NDEV, TR, C), lambda i: (0, i, 0))],
        out_specs=pl.BlockSpec((TR, C), lambda i: (i, 0)),
        out_shape=jax.ShapeDtypeStruct((R, C), F32),
        compiler_params=_params("arbitrary"),
    )(parts)


def _adamw(g_layers, w, m, v, name):
    L, R, C = w.shape
    g0 = g_layers[0]
    nparts = g0.shape[0] if g0.ndim == 3 else 0
    TR = R
    for cand in (256, 128, 64, 32, 16):
        if R % cand == 0 and cand * C <= 128 * 1024:
            TR = cand
            break

    def body(*refs):
        g_refs = refs[:L]
        w_ref, m_ref, v_ref, go_ref, d_ref, mo_ref, vo_ref = refs[L:]
        for ll in range(L):
            @pl.when(pl.program_id(0) == ll)
            def _(g_ref=g_refs[ll]):
                if nparts:
                    g = g_ref[0].astype(F32)
                    for d in range(1, nparts):
                        g = g + g_ref[d].astype(F32)
                else:
                    g = g_ref[...]
                m2 = ADAM_B1 * m_ref[0] + (1.0 - ADAM_B1) * g
                v2 = ADAM_B2 * v_ref[0] + (1.0 - ADAM_B2) * (g * g)
                m_hat = m2 / (1.0 - ADAM_B1 ** ADAM_STEP)
                v_hat = v2 / (1.0 - ADAM_B2 ** ADAM_STEP)
                go_ref[0] = g
                d_ref[0] = -ADAM_LR * (m_hat / (jnp.sqrt(v_hat) + ADAM_EPS) + ADAM_WD * w_ref[0])
                mo_ref[0] = m2
                vo_ref[0] = v2

    def gspec(ll):
        if nparts:
            return pl.BlockSpec((nparts, TR, C), lambda l, i: (0, jnp.where(l == ll, i, 0), 0))
        return pl.BlockSpec((TR, C), lambda l, i: (jnp.where(l == ll, i, 0), 0))

    spec = pl.BlockSpec((1, TR, C), lambda l, i: (l, i, 0))
    return _call(
        body, name=name, grid=(L, R // TR),
        in_specs=[gspec(ll) for ll in range(L)] + [spec, spec, spec], out_specs=[spec] * 4,
        out_shape=[jax.ShapeDtypeStruct((L, R, C), F32)] * 4,
        compiler_params=_params("arbitrary", "arbitrary"),
    )(*g_layers, w, m, v)


def _ada_cols(c_all, w_ada, b_cols):
    L, D, NS = w_ada.shape

    def body(c_ref, w_ref, b_ref, o_ref):
        cond = _silu(c_ref[...]).astype(BF16)
        o_ref[0] = _dot(cond, w_ref[0].astype(BF16)) + b_ref[0]

    return _call(
        body, name="ada_cols", grid=(L,),
        in_specs=[pl.BlockSpec((NDEV, D), lambda l: (0, 0)), pl.BlockSpec((1, D, NS), lambda l: (l, 0, 0)),
                  pl.BlockSpec((1, 1, NS), lambda l: (l, 0, 0))],
        out_specs=pl.BlockSpec((1, NDEV, NS), lambda l: (l, 0, 0)),
        out_shape=jax.ShapeDtypeStruct((L, NDEV, NS), F32),
        compiler_params=_params("arbitrary"),
    )(c_all, w_ada, b_cols)


def _ada_grads(c_all, dada_all, dada_cols):
    _, D = c_all.shape
    L, _, NS = dada_cols.shape
    D3 = dada_all.shape[2]

    def body(c_ref, da_ref, dc_ref, gw_ref, gb_ref):
        cond = _silu(c_ref[...]).astype(BF16)
        gw_ref[0] = _dot_tn(cond, dc_ref[0].astype(BF16))
        acc = da_ref[0]
        for d in range(1, NDEV):
            acc = acc + da_ref[d]
        gb_ref[...] = acc

    return _call(
        body, name="ada_grads", grid=(L,),
        in_specs=[pl.BlockSpec((NDEV, D), lambda l: (0, 0)), pl.BlockSpec((NDEV, L, D3), lambda l: (0, 0, 0)),
                  pl.BlockSpec((1, NDEV, NS), lambda l: (l, 0, 0))],
        out_specs=[pl.BlockSpec((1, D, NS), lambda l: (l, 0, 0)), pl.BlockSpec((L, D3), lambda l: (0, 0))],
        out_shape=[jax.ShapeDtypeStruct((L, D, NS), F32), jax.ShapeDtypeStruct((L, D3), F32)],
        compiler_params=_params("arbitrary"),
    )(c_all, dada_all, dada_cols)


def _tile(S, want):
    return want if S % want == 0 else S


def _modulate(x, mod):
    S, D = x.shape
    TS = _tile(S, 512)

    def body(x_ref, mod_ref, h_ref):
        h_ref[...] = (x_ref[...] * mod_ref[0:1, :] + mod_ref[1:2, :]).astype(BF16)

    spec = pl.BlockSpec((TS, D), lambda i: (i, 0))
    return _call(
        body, name="modulate", grid=(S // TS,),
        in_specs=[spec, pl.BlockSpec((8, D), lambda i: (0, 0))], out_specs=spec,
        out_shape=jax.ShapeDtypeStruct((S, D), BF16),
        compiler_params=_params("arbitrary"),
    )(x, mod)


def _peer(k):
    ix, iy, ic = lax.axis_index("x"), lax.axis_index("y"), lax.axis_index("c")
    bx, by, bc = (k >> 2) & 1, (k >> 1) & 1, k & 1
    px, py, pc = ix + bx - 2 * ix * bx, iy + by - 2 * iy * by, ic + bc - 2 * ic * bc
    return (px, py, pc), 4 * px + 2 * py + pc


GATHER_ORDER = (0, 1, 2, 4, 6, 3, 5, 7)
GATHER_DIRECT = (1, 2, 4, 6)
SCATTER_ORDER = (3, 2, 5, 4, 7, 6, 1, 0)


def _offset(order, k):
    off = jnp.int32(order[-1])
    for step in reversed(range(len(order) - 1)):
        off = jnp.where(k == step, jnp.int32(order[step]), off)
    return off


def _proj_gather(h, w_shard, extras, me1):
    assert GATHER_ORDER[0] == 0
    S, D = h.shape
    NS = w_shard.shape[1]
    TS = _tile(S, 512)
    nt = S // TS
    ne = len(extras)

    def body(me_ref, h_ref, w_ref, *rest):
        x_refs, o_ref, wg_ref = rest[:ne], rest[ne], rest[ne + 1]
        xg_refs = rest[ne + 2:2 * ne + 2]
        wbuf, send_sems, recv_sems, local_sems, load_sems = rest[2 * ne + 2:]
        k, i = pl.program_id(0), pl.program_id(1)
        me = me_ref[0]
        off = _offset(GATHER_ORDER, k)

        def push(src, dst, which, kk):
            peer, _ = _peer(kk)
            return pltpu.make_async_remote_copy(
                src_ref=src, dst_ref=dst.at[me], send_sem=send_sems.at[which, kk], recv_sem=recv_sems.at[which, kk],
                device_id=peer, device_id_type=pl.DeviceIdType.MESH)

        def own(src, dst, which):
            return pltpu.make_async_copy(src, dst.at[me], local_sems.at[which])

        pairs = [(w_ref, wg_ref)] + list(zip(x_refs, xg_refs))

        @pl.when(jnp.logical_and(k == 0, i == 0))
        def _():
            for which, (src, dst) in enumerate(pairs):
                own(src, dst, which).start()
                for kk in (GATHER_DIRECT if which == 0 else GATHER_ORDER[1:]):
                    push(src, dst, which, kk).start()
            first = pltpu.make_async_copy(w_ref, wbuf.at[0], load_sems.at[0])
            first.start()
            first.wait()

        @pl.when(jnp.logical_and(k > 0, i == 0))
        def _():
            push(w_ref, wg_ref, 0, off).wait_recv()
            _, blk = _peer(off)
            for step, o in enumerate(GATHER_ORDER):
                if o in GATHER_DIRECT and o > 1:
                    @pl.when(k == step)
                    def _(o=o):
                        sibling, _ = _peer(1)
                        pltpu.make_async_remote_copy(
                            src_ref=wg_ref.at[blk], dst_ref=wg_ref.at[blk], send_sem=send_sems.at[0, o + 1],
                            recv_sem=recv_sems.at[0, o + 1], device_id=sibling, device_id_type=pl.DeviceIdType.MESH).start()
            load = pltpu.make_async_copy(wg_ref.at[blk], wbuf.at[k % 2], load_sems.at[k % 2])
            load.start()
            load.wait()

        o_ref[...] = _dot(h_ref[...], wbuf[k % 2])

        @pl.when(jnp.logical_and(k == NDEV - 1, i == nt - 1))
        def _():
            for which, (src, dst) in enumerate(pairs):
                for kk in range(1, NDEV):
                    cp = push(src, dst, which, kk)
                    cp.wait_send()
                    if which > 0:
                        cp.wait_recv()
                own(src, dst, which).wait()

    def col(k, i, me_ref):
        m, off = me_ref[0], _offset(GATHER_ORDER, k)
        return i, (m | off) - (m & off)

    anyspec = pl.BlockSpec(memory_space=pl.ANY)
    grid_spec = pltpu.PrefetchScalarGridSpec(
        num_scalar_prefetch=1, grid=(NDEV, nt),
        in_specs=[pl.BlockSpec((TS, D), lambda k, i, me_ref: (i, 0)), anyspec] + [anyspec] * ne,
        out_specs=[pl.BlockSpec((TS, NS), col), anyspec] + [anyspec] * ne,
        scratch_shapes=[pltpu.VMEM((2, D, NS), BF16), pltpu.SemaphoreType.DMA((1 + ne, NDEV)), pltpu.SemaphoreType.DMA((1 + ne, NDEV)),
                        pltpu.SemaphoreType.DMA((1 + ne,)), pltpu.SemaphoreType.DMA((2,))])
    return _call(
        body, name="proj_gather", grid_spec=grid_spec,
        out_shape=[jax.ShapeDtypeStruct((S, NDEV * NS), F32), jax.ShapeDtypeStruct((NDEV, D, NS), BF16)]
        + [jax.ShapeDtypeStruct((NDEV,) + e.shape, e.dtype) for e in extras],
        compiler_params=_params("arbitrary", "arbitrary"),
    )(me1, h, w_shard, *extras)


def _dw_scatter(a, b, blk, by_rows, me1, name):
    S = a.shape[0]
    TK = _tile(S, 512)
    nk = S // TK
    shape = (blk, b.shape[1]) if by_rows else (a.shape[1], blk)
    last = NDEV - 1
    assert all(o & 1 for o in SCATTER_ORDER[0::2]) and not any(o & 1 for o in SCATTER_ORDER[1::2]) and SCATTER_ORDER[-1] == 0
    assert all(SCATTER_ORDER[s] == SCATTER_ORDER[s + 1] + 1 for s in range(0, NDEV, 2))

    def body(me_ref, a_ref, b_ref, recv_ref, acc, stage, sib, send_sems, recv_sems, local_sem):
        k, kk = pl.program_id(0), pl.program_id(1)

        def chip_of(step):
            return _offset(SCATTER_ORDER, step) >> 1

        def to_sibling(j, slot):
            sibling, _ = _peer(1)
            return pltpu.make_async_remote_copy(
                src_ref=stage.at[slot], dst_ref=sib.at[j], send_sem=send_sems.at[0, j], recv_sem=recv_sems.at[0, j],
                device_id=sibling, device_id_type=pl.DeviceIdType.MESH)

        def to_chip(j, slot):
            peer, _ = _peer(2 * j)
            return pltpu.make_async_remote_copy(
                src_ref=stage.at[slot], dst_ref=recv_ref.at[j], send_sem=send_sems.at[1, j], recv_sem=recv_sems.at[1, j],
                device_id=peer, device_id_type=pl.DeviceIdType.MESH)

        own = pltpu.make_async_copy(stage.at[last % 2], recv_ref.at[0], local_sem)

        @pl.when(kk == 0)
        def _():
            acc[...] = jnp.zeros_like(acc)
        acc[...] += _dot_tn(a_ref[...], b_ref[...])

        @pl.when(kk == nk - 1)
        def _():
            j = chip_of(k)
            even_step = k % 2 == 0

            @pl.when(jnp.logical_and(k >= 2, even_step))
            def _():
                to_sibling(chip_of(k - 2), k % 2).wait_send()

            @pl.when(jnp.logical_and(k >= 3, jnp.logical_not(even_step)))
            def _():
                to_chip(chip_of(k - 2), k % 2).wait_send()

            @pl.when(even_step)
            def _():
                stage[k % 2] = acc[...].astype(BF16)
                to_sibling(j, k % 2).start()

            @pl.when(jnp.logical_not(even_step))
            def _():
                to_sibling(j, 0).wait_recv()
                stage[k % 2] = (acc[...] + sib[j].astype(F32)).astype(BF16)

                @pl.when(k < last)
                def _():
                    to_chip(j, k % 2).start()

                @pl.when(k == last)
                def _():
                    own.start()
                    to_sibling(SCATTER_ORDER[last - 1] >> 1, (last - 1) % 2).wait_send()
                    own.wait()
                    for jj in range(1, NDEV // 2):
                        to_chip(jj, 0).wait_recv()

    def blk_of(k, me_ref):
        m, off = me_ref[0], _offset(SCATTER_ORDER, k)
        return (m | off) - (m & off)

    if by_rows:
        in_specs = [pl.BlockSpec((TK, blk), lambda k, kk, me_ref: (kk, blk_of(k, me_ref))),
                    pl.BlockSpec((TK, b.shape[1]), lambda k, kk, me_ref: (kk, 0))]
    else:
        in_specs = [pl.BlockSpec((TK, a.shape[1]), lambda k, kk, me_ref: (kk, 0)),
                    pl.BlockSpec((TK, blk), lambda k, kk, me_ref: (kk, blk_of(k, me_ref)))]
    grid_spec = pltpu.PrefetchScalarGridSpec(
        num_scalar_prefetch=1, grid=(NDEV, nk), in_specs=in_specs,
        out_specs=pl.BlockSpec(memory_space=pl.ANY),
        scratch_shapes=[pltpu.VMEM(shape, F32), pltpu.VMEM((2,) + shape, BF16), pltpu.VMEM((NDEV // 2,) + shape, BF16),
                        pltpu.SemaphoreType.DMA((2, NDEV // 2)), pltpu.SemaphoreType.DMA((2, NDEV // 2)), pltpu.SemaphoreType.DMA])
    return _call(
        body, name=name, grid_spec=grid_spec,
        out_shape=jax.ShapeDtypeStruct((NDEV // 2,) + shape, BF16),
        compiler_params=_params("arbitrary", "arbitrary"),
    )(me1, a, b)


def _chunk(TS, BR, k):
    return pl.BlockSpec((TS, BR), lambda i: (i, k))


def _halo_prev(TS, BR, k):
    return pl.BlockSpec((8, BR), lambda i: (jnp.maximum(i * (TS // 8) - 1, 0), k))


def _conv_a_fwd(proj, w8, BR):
    S = proj.shape[0]
    TS = _tile(S, 512)

    def body(ab_ref, ac_ref, ax_ref, ag_ref, hc_ref, hx_ref, w_ref, o_ref, ext):
        i = pl.program_id(0)
        u = ac_ref[...] * ax_ref[...]
        ext[0:8, :] = jnp.where(i > 0, hc_ref[...] * hx_ref[...], 0.0)
        ext[8:8 + TS, :] = u
        conv = ext[pl.ds(6, TS), :] * w_ref[0:1, :] + ext[pl.ds(7, TS), :] * w_ref[1:2, :] + u * w_ref[2:3, :]
        o_ref[...] = ab_ref[...] * conv * _silu(ag_ref[...])

    return _call(
        body, name="conv_a_fwd", grid=(S // TS,),
        in_specs=[_chunk(TS, BR, 0), _chunk(TS, BR, 1), _chunk(TS, BR, 2), _chunk(TS, BR, 3),
                  _halo_prev(TS, BR, 1), _halo_prev(TS, BR, 2), pl.BlockSpec((8, BR), lambda i: (0, 0))],
        out_specs=pl.BlockSpec((TS, BR), lambda i: (i, 0)),
        out_shape=jax.ShapeDtypeStruct((S, BR), F32),
        scratch_shapes=[pltpu.VMEM((TS + 8, BR), F32)],
        compiler_params=_params("arbitrary"),
    )(proj, proj, proj, proj, proj, proj, w8)


ATT_UNIT = 2048
ATT_LANES = 128


def _attn_mask():
    i = lax.broadcasted_iota(jnp.int32, (BLK, 2 * BLK), 0)
    j = lax.broadcasted_iota(jnp.int32, (BLK, 2 * BLK), 1)
    return jnp.logical_and(j >= i, j <= i + BLK), j >= BLK


def _attn_rows(n, r, dil):
    return pl.ds(n * BLK * dil + r, BLK, stride=dil) if dil > 1 else pl.ds(n * BLK, BLK)


def _attn_geometry(S, BR, dil):
    HD = BR // 8
    U = ATT_UNIT
    assert S % U == 0 and BR % ATT_LANES == 0 and ATT_LANES % HD == 0 and U % (BLK * dil) == 0
    return HD, U, S // U, U // (BLK * dil), ATT_LANES // HD, BR // ATT_LANES


def _attn_fwd(proj, bias, dil, BR):
    S = proj.shape[0]
    HD, U, NU, nbu, hpb, HBK = _attn_geometry(S, BR, dil)
    scale = HD ** -0.5
    combos = [(n, r) for n in range(nbu) for r in range(dil)]

    def body(q_ref, kc_ref, kp_ref, vc_ref, vp_ref, b_ref, o_ref, l_ref, sbuf, pbuf):
        m = pl.program_id(1)
        band, is_cur = _attn_mask()
        first = jnp.logical_and(band, jnp.logical_or(is_cur, m > 0))

        def keys(ref_c, ref_p, n, r):
            prev = ref_c[_attn_rows(n - 1, r, dil), :] if n > 0 else ref_p[_attn_rows(nbu - 1, r, dil), :]
            return jnp.concatenate([prev, ref_c[_attn_rows(n, r, dil), :]], axis=0).astype(BF16)

        for c, (n, r) in enumerate(combos):
            q = (q_ref[_attn_rows(n, r, dil), :] * scale).astype(BF16)
            kk = keys(kc_ref, kp_ref, n, r)
            for h in range(hpb):
                sl = slice(h * HD, (h + 1) * HD)
                sbuf[c * hpb + h] = _dot_nt(q[:, sl], kk[:, sl])
        for c, (n, r) in enumerate(combos):
            lses = []
            for h in range(hpb):
                s = jnp.where(band if n > 0 else first, sbuf[c * hpb + h] + b_ref[h], NEG)
                mx = jnp.max(s, axis=-1, keepdims=True)
                p = jnp.exp(s - mx)
                l = jnp.sum(p, axis=-1, keepdims=True)
                pbuf[c * hpb + h] = (p * (1.0 / l)).astype(BF16)
                lses.append(jnp.broadcast_to(mx + jnp.log(l), (BLK, HD)))
            l_ref[_attn_rows(n, r, dil), :] = jnp.concatenate(lses, axis=1)
        for c, (n, r) in enumerate(combos):
            vv = keys(vc_ref, vp_ref, n, r)
            o_ref[_attn_rows(n, r, dil), :] = jnp.concatenate(
                [_dot(pbuf[c * hpb + h], vv[:, h * HD:(h + 1) * HD]) for h in range(hpb)], axis=1)

    def cur(c):
        return pl.BlockSpec((U, ATT_LANES), lambda hb, m: (m, c * HBK + hb))

    def prev(c):
        return pl.BlockSpec((U, ATT_LANES), lambda hb, m: (jnp.maximum(m - 1, 0), c * HBK + hb))

    ospec = pl.BlockSpec((U, ATT_LANES), lambda hb, m: (m, hb))
    nhb = len(combos) * hpb
    return _call(
        body, name="attn_fwd_d%d" % dil, grid=(HBK, NU),
        in_specs=[cur(4), cur(5), prev(5), cur(6), prev(6), pl.BlockSpec((hpb, BLK, 2 * BLK), lambda hb, m: (hb, 0, 0))],
        out_specs=[ospec, ospec],
        out_shape=[jax.ShapeDtypeStruct((S, BR), F32)] * 2,
        scratch_shapes=[pltpu.VMEM((nhb, BLK, 2 * BLK), F32), pltpu.VMEM((nhb, BLK, 2 * BLK), BF16)],
        compiler_params=_params("arbitrary", "arbitrary"),
    )(proj, proj, proj, proj, proj, bias)


def _mix_weights(l1, l2, l3):
    m = jnp.maximum(jnp.maximum(l1, l2), l3)
    e1, e2, e3 = jnp.exp(l1 - m), jnp.exp(l2 - m), jnp.exp(l3 - m)
    inv = 1.0 / (e1 + e2 + e3)
    return e1 * inv, e2 * inv, e3 * inv


def _attn_mix_fwd(proj, os_, ls_, BR):
    S = proj.shape[0]
    TS = _tile(S, 512)

    def body(o1, o2, o3, l1, l2, l3, g_ref, y_ref):
        w1, w2, w3 = _mix_weights(l1[...], l2[...], l3[...])
        y_ref[...] = (w1 * o1[...] + w2 * o2[...] + w3 * o3[...]) * _silu(g_ref[...])

    spec = pl.BlockSpec((TS, BR), lambda i: (i, 0))
    return _call(
        body, name="attn_mix_fwd", grid=(S // TS,),
        in_specs=[spec] * 6 + [_chunk(TS, BR, 7)], out_specs=spec,
        out_shape=jax.ShapeDtypeStruct((S, BR), F32),
        compiler_params=_params("arbitrary"),
    )(*os_, *ls_, proj)


def _lru_gates(xc, wa, wx, ba, bx, sp):
    xb = xc.astype(BF16)
    r = _sigmoid(_dot(xb, wa) + ba)
    ig = _sigmoid(_dot(xb, wx) + bx)
    la = -LRU_C * r * sp
    a = jnp.exp(la)
    mult = jnp.sqrt(-_expm1(2.0 * la))
    return r, ig, a, mult


def _lru_fwd(proj, rows, wa, wx, BR):
    S = proj.shape[0]
    TS = _tile(S, 256)
    PAD = TS // 2

    def body(cx_ref, hx_ref, cg_ref, r_ref, wa_ref, wx_ref, y_ref, h_ref, ext, sa, sb, carry):
        i = pl.program_id(0)

        @pl.when(i == 0)
        def _():
            sa[0:PAD, :] = jnp.ones((PAD, BR), F32)
            sb[0:PAD, :] = jnp.zeros((PAD, BR), F32)
            carry[...] = jnp.zeros_like(carry)

        cx = cx_ref[...]
        ext[0:8, :] = jnp.where(i > 0, hx_ref[...], 0.0)
        ext[8:8 + TS, :] = cx
        xc = (ext[pl.ds(5, TS), :] * r_ref[0:1, :] + ext[pl.ds(6, TS), :] * r_ref[1:2, :]
              + ext[pl.ds(7, TS), :] * r_ref[2:3, :] + cx * r_ref[3:4, :] + r_ref[4:5, :])
        _, ig, a, mult = _lru_gates(xc, wa_ref[...], wx_ref[...], r_ref[5:6, :], r_ref[6:7, :], r_ref[7:8, :])
        sa[PAD:PAD + TS, :] = a
        sb[PAD:PAD + TS, :] = mult * ig * xc
        d = 1
        while d < TS:
            A, B = sa[PAD:PAD + TS, :], sb[PAD:PAD + TS, :]
            As, Bs = sa[pl.ds(PAD - d, TS), :], sb[pl.ds(PAD - d, TS), :]
            sb[PAD:PAD + TS, :] = A * Bs + B
            sa[PAD:PAD + TS, :] = A * As
            d *= 2
        h = sb[PAD:PAD + TS, :] + sa[PAD:PAD + TS, :] * carry[0:1, :]
        carry[0:1, :] = h[TS - 1:TS, :]
        h_ref[...] = h
        y_ref[...] = h * _silu(cg_ref[...])

    full = pl.BlockSpec((BR, BR), lambda i: (0, 0))
    spec = pl.BlockSpec((TS, BR), lambda i: (i, 0))
    return _call(
        body, name="lru_fwd", grid=(S // TS,),
        in_specs=[_chunk(TS, BR, 8), _halo_prev(TS, BR, 8), _chunk(TS, BR, 9), pl.BlockSpec((8, BR), lambda i: (0, 0)), full, full],
        out_specs=[spec, spec],
        out_shape=[jax.ShapeDtypeStruct((S, BR), F32)] * 2,
        scratch_shapes=[pltpu.VMEM((TS + 8, BR), F32), pltpu.VMEM((PAD + TS, BR), F32), pltpu.VMEM((PAD + TS, BR), F32),
                        pltpu.VMEM((8, BR), F32)],
        compiler_params=_params("arbitrary"),
    )(proj, proj, proj, rows, wa, wx)


S5_LAGS = 8
S5_CB = 8 * S5_CH
S5_SB = 8 * S5_STATE


def _s5_fwd(proj, a8, bbl, cre, cim, rows, wg, BR):
    S = proj.shape[0]
    W = a8.shape[1]
    TS = _tile(S, 256)
    CB, SB, J = S5_CB, S5_SB, S5_LAGS
    nblk = BR // CB

    def body(u_ref, uh_ref, dg_ref, a8_ref, bbl_ref, cre_ref, cim_ref, r_ref, wg_ref,
             y_ref, xre_ref, xim_ref, uext, ulag, sre, sim, ypre_s, carry):
        i = pl.program_id(0)

        @pl.when(i == 0)
        def _():
            carry[...] = jnp.zeros_like(carry)

        uext[0:8, :] = jnp.where(i > 0, uh_ref[...], 0.0)
        uext[8:8 + TS, :] = u_ref[...]
        for m in range(nblk):
            cs, ws = slice(m * CB, (m + 1) * CB), slice(m * SB, (m + 1) * SB)
            for j in range(J):
                ulag[:, j * CB:(j + 1) * CB] = uext[pl.ds(8 - j, TS), cs].astype(BF16)
            w = _dot(ulag[...], bbl_ref[m])
            sre[...] = w[:, 0:SB]
            sim[...] = w[:, SB:2 * SB]
            ar = jnp.broadcast_to(a8_ref[0:1, ws], (8, SB))
            ai = jnp.broadcast_to(a8_ref[1:2, ws], (8, SB))
            xr, xi = carry[0:8, ws], carry[8:16, ws]
            for g in range(TS // 8):
                rg = slice(8 * g, 8 * g + 8)
                xr, xi = sre[rg, :] + ar * xr - ai * xi, sim[rg, :] + ar * xi + ai * xr
                sre[rg, :] = xr
                sim[rg, :] = xi
            carry[0:8, ws] = xr
            carry[8:16, ws] = xi
            xre, xim = sre[...], sim[...]
            xre_ref[:, ws] = xre
            xim_ref[:, ws] = xim
            ypre_s[:, cs] = _dot(xre.astype(BF16), cre_ref[ws, :]) - _dot(xim.astype(BF16), cim_ref[ws, :])
        dg = dg_ref[...]
        yg = _gelu(ypre_s[...] + r_ref[0:1, :] * u_ref[...])
        s = _sigmoid(_dot(yg.astype(BF16), wg_ref[...]) + r_ref[1:2, :])
        y_ref[...] = yg * s * _silu(dg)

    def const(shape):
        return pl.BlockSpec(shape, lambda i: (0,) * len(shape))

    return _call(
        body, name="s5_fwd", grid=(S // TS,),
        in_specs=[_chunk(TS, BR, 10), _halo_prev(TS, BR, 10), _chunk(TS, BR, 11), const((8, W)), const((nblk, J * CB, 2 * SB)),
                  const((W, CB)), const((W, CB)), const((8, BR)), const((BR, BR))],
        out_specs=[pl.BlockSpec((TS, BR), lambda i: (i, 0)), pl.BlockSpec((TS, W), lambda i: (i, 0)), pl.BlockSpec((TS, W), lambda i: (i, 0))],
        out_shape=[jax.ShapeDtypeStruct((S, BR), F32), jax.ShapeDtypeStruct((S, W), F32), jax.ShapeDtypeStruct((S, W), F32)],
        scratch_shapes=[pltpu.VMEM((TS + 8, BR), F32), pltpu.VMEM((TS, J * CB), BF16), pltpu.VMEM((TS, SB), F32), pltpu.VMEM((TS, SB), F32),
                        pltpu.VMEM((TS, BR), F32), pltpu.VMEM((16, W), F32)],
        compiler_params=_params("arbitrary"),
    )(proj, proj, proj, a8, bbl, cre, cim, rows, wg)


def _out_ln(ys, w_out, x, rows, target=None):
    S, D = x.shape
    BR = D // 4
    TS = _tile(S, 256)
    last = target is not None

    def body(ya, yb, yc, yd, w_ref, x_ref, r_ref, *rest):
        t_ref = rest[0] if last else None
        o_ref, y_ref, cat_ref = rest[1:4] if last else rest[0:3]
        cat = jnp.concatenate([ya[...], yb[...], yc[...], yd[...]], axis=1).astype(BF16)
        y = _dot(cat, w_ref[...])
        z = ALPHA * x_ref[...] + r_ref[0:1, :] * y
        mu = jnp.mean(z, axis=-1, keepdims=True)
        zc = z - mu
        var = jnp.mean(zc * zc, axis=-1, keepdims=True)
        xn = zc * lax.rsqrt(var + LN_EPS) * r_ref[1:2, :] + r_ref[2:3, :]
        y_ref[...] = y
        cat_ref[...] = cat
        if last:
            l_ref = rest[4]

            @pl.when(pl.program_id(0) == 0)
            def _():
                l_ref[...] = jnp.zeros_like(l_ref)
            diff = xn - t_ref[...]
            o_ref[...] = diff * (1.0 / D)
            l_ref[...] += jnp.sum(_colsum(diff * diff), axis=1, keepdims=True) * (0.5 / D)
        else:
            o_ref[...] = xn

    yspec = pl.BlockSpec((TS, BR), lambda i: (i, 0))
    spec = pl.BlockSpec((TS, D), lambda i: (i, 0))
    lspec = pl.BlockSpec((1, 1), lambda i: (0, 0))
    return _call(
        body, name="out_ln_loss" if last else "out_ln", grid=(S // TS,),
        in_specs=[yspec] * 4 + [pl.BlockSpec((D, D), lambda i: (0, 0)), spec, pl.BlockSpec((8, D), lambda i: (0, 0))] + [spec] * last,
        out_specs=[spec, spec, spec] + [lspec] * last,
        out_shape=[jax.ShapeDtypeStruct((S, D), F32), jax.ShapeDtypeStruct((S, D), F32), jax.ShapeDtypeStruct((S, D), BF16)]
        + [jax.ShapeDtypeStruct((1, 1), F32)] * last,
        compiler_params=_params("arbitrary"),
    )(*ys, w_out, x, rows, *([target] if last else []))


def _ln_bwd(dout, x, y, rows, w_out):
    S, D = x.shape
    TS = _tile(S, 256)

    def body(do_ref, x_ref, y_ref, r_ref, w_ref, dxr_ref, dyb_ref, dcat_ref, acc_ref):
        @pl.when(pl.program_id(0) == 0)
        def _():
            acc_ref[...] = jnp.zeros_like(acc_ref)
        g1, lg = r_ref[0:1, :], r_ref[1:2, :]
        yv = y_ref[...]
        z = ALPHA * x_ref[...] + g1 * yv
        mu = jnp.mean(z, axis=-1, keepdims=True)
        zc = z - mu
        var = jnp.mean(zc * zc, axis=-1, keepdims=True)
        rstd = lax.rsqrt(var + LN_EPS)
        xhat = zc * rstd
        do = do_ref[...]
        dxh = do * lg
        dz = rstd * (dxh - jnp.mean(dxh, axis=-1, keepdims=True) - xhat * jnp.mean(dxh * xhat, axis=-1, keepdims=True))
        dxr_ref[...] = ALPHA * dz
        dyb = (g1 * dz).astype(BF16)
        dyb_ref[...] = dyb
        dcat_ref[...] = _dot_nt(dyb, w_ref[...])
        acc_ref[0:1, :] += _colsum(do * xhat)
        acc_ref[1:2, :] += _colsum(do)
        acc_ref[2:3, :] += _colsum(dz * yv)

    spec = pl.BlockSpec((TS, D), lambda i: (i, 0))
    rspec = pl.BlockSpec((8, D), lambda i: (0, 0))
    return _call(
        body, name="ln_bwd", grid=(S // TS,),
        in_specs=[spec, spec, spec, rspec, pl.BlockSpec((D, D), lambda i: (0, 0))],
        out_specs=[spec, spec, spec, rspec],
        out_shape=[jax.ShapeDtypeStruct((S, D), F32), jax.ShapeDtypeStruct((S, D), BF16), jax.ShapeDtypeStruct((S, D), F32),
                   jax.ShapeDtypeStruct((8, D), F32)],
        compiler_params=_params("arbitrary"),
    )(dout, x, y, rows, w_out)


def _dh(dproj, wg, dxres, x, mod, sides):
    S, D = x.shape
    NS = wg.shape[2]
    TS = _tile(S, 256)
    nt = S // TS
    ns = len(sides)

    def body(dp_ref, w_hbm, dxr_ref, x_ref, mod_ref, *rest):
        s_refs, dx_ref, sum_ref, so_refs = rest[:ns], rest[ns], rest[ns + 1], rest[ns + 2:2 * ns + 2]
        w_vmem, sem, send_sems, recv_sems, local_sems = rest[2 * ns + 2:]
        i = pl.program_id(0)
        ix, iy, ic = lax.axis_index("x"), lax.axis_index("y"), lax.axis_index("c")
        me = 4 * ix + 2 * iy + ic

        def exchange(which):
            (_, gather), x_side, o_side = sides[which], s_refs[which], so_refs[which]
            copies = [pltpu.make_async_copy(x_side if gather else x_side.at[me], o_side.at[me], local_sems.at[which])]
            for k in range(1, NDEV):
                px = 1 - ix if k & 4 else ix
                py = 1 - iy if k & 2 else iy
                pc = 1 - ic if k & 1 else ic
                copies.append(pltpu.make_async_remote_copy(
                    src_ref=x_side if gather else x_side.at[4 * px + 2 * py + pc], dst_ref=o_side.at[me],
                    send_sem=send_sems.at[which, k - 1], recv_sem=recv_sems.at[which, k - 1],
                    device_id=(px, py, pc), device_id_type=pl.DeviceIdType.MESH))
            return copies

        @pl.when(i == 0)
        def _():
            load = pltpu.make_async_copy(w_hbm, w_vmem, sem)
            load.start()
            for which in range(ns):
                for cp in exchange(which):
                    cp.start()
            sum_ref[...] = jnp.zeros_like(sum_ref)
            load.wait()

        dh = _dot_nt(dp_ref[:, 0:NS], w_vmem[0])
        for j in range(1, NDEV):
            dh = dh + _dot_nt(dp_ref[:, j * NS:(j + 1) * NS], w_vmem[j])
        dx_ref[...] = dxr_ref[...] + dh * mod_ref[0:1, :]
        sum_ref[0:1, :] += _colsum(dh * x_ref[...])
        sum_ref[1:2, :] += _colsum(dh)

        @pl.when(i == nt - 1)
        def _():
            for which in range(ns):
                for cp in exchange(which):
                    cp.wait()

    spec = pl.BlockSpec((TS, D), lambda i: (i, 0))
    rspec = pl.BlockSpec((8, D), lambda i: (0, 0))
    anyspec = pl.BlockSpec(memory_space=pl.ANY)
    side_shapes = [jax.ShapeDtypeStruct(((NDEV,) + a.shape) if g else a.shape, a.dtype) for a, g in sides]
    return _call(
        body, name="dh", grid=(nt,),
        in_specs=[pl.BlockSpec((TS, NDEV * NS), lambda i: (i, 0)), anyspec, spec, spec, rspec] + [anyspec] * ns,
        out_specs=[spec, rspec] + [anyspec] * ns,
        out_shape=[jax.ShapeDtypeStruct((S, D), F32), jax.ShapeDtypeStruct((8, D), F32)] + side_shapes,
        scratch_shapes=[pltpu.VMEM((NDEV, D, NS), BF16), pltpu.SemaphoreType.DMA, pltpu.SemaphoreType.DMA((max(ns, 1), NDEV - 1)),
                        pltpu.SemaphoreType.DMA((max(ns, 1), NDEV - 1)), pltpu.SemaphoreType.DMA((max(ns, 1),))],
        compiler_params=_params("arbitrary"),
    )(dproj, wg, dxres, x, mod, *[a for a, _ in sides])


def _halo_next(S, TS, BR, k):
    last = S // 8 - 1
    return pl.BlockSpec((8, BR), lambda i: (jnp.minimum((i + 1) * (TS // 8), last), k))


def _conv_a_bwd(proj, dcat, w8, BR):
    S = proj.shape[0]
    TS = _tile(S, 512)
    nt = S // TS

    def body(ab_ref, ac_ref, ax_ref, ag_ref, dy_ref, hc_ref, hx_ref, nab_ref, nag_ref, ndy_ref, w_ref, dp_ref, dw_ref, ext, dext):
        i = pl.program_id(0)

        @pl.when(i == 0)
        def _():
            dw_ref[...] = jnp.zeros_like(dw_ref)

        w0, w1, w2 = w_ref[0:1, :], w_ref[1:2, :], w_ref[2:3, :]
        ab, ac, ax, ag, dy = ab_ref[...], ac_ref[...], ax_ref[...], ag_ref[...], dy_ref[...]
        u = ac * ax
        ext[0:8, :] = jnp.where(i > 0, hc_ref[...] * hx_ref[...], 0.0)
        ext[8:8 + TS, :] = u
        u1, u2 = ext[pl.ds(7, TS), :], ext[pl.ds(6, TS), :]
        conv = u2 * w0 + u1 * w1 + u * w2
        sg = _silu(ag)
        dconv = dy * ab * sg
        dext[0:TS, :] = dconv
        dext[TS:TS + 8, :] = jnp.where(i < nt - 1, ndy_ref[...] * nab_ref[...] * _silu(nag_ref[...]), 0.0)
        du = w2 * dconv + w1 * dext[pl.ds(1, TS), :] + w0 * dext[pl.ds(2, TS), :]
        dp_ref[:, 0:BR] = (dy * conv * sg).astype(BF16)
        dp_ref[:, BR:2 * BR] = (du * ax).astype(BF16)
        dp_ref[:, 2 * BR:3 * BR] = (du * ac).astype(BF16)
        dp_ref[:, 3 * BR:4 * BR] = (dy * ab * conv * _dsilu(ag)).astype(BF16)
        dw_ref[0:1, :] += _colsum(dconv * u2)
        dw_ref[1:2, :] += _colsum(dconv * u1)
        dw_ref[2:3, :] += _colsum(dconv * u)

    rspec = pl.BlockSpec((8, BR), lambda i: (0, 0))
    return _call(
        body, name="conv_a_bwd", grid=(nt,),
        in_specs=[_chunk(TS, BR, 0), _chunk(TS, BR, 1), _chunk(TS, BR, 2), _chunk(TS, BR, 3), _chunk(TS, BR, 0),
                  _halo_prev(TS, BR, 1), _halo_prev(TS, BR, 2), _halo_next(S, TS, BR, 0), _halo_next(S, TS, BR, 3),
                  _halo_next(S, TS, BR, 0), rspec],
        out_specs=[pl.BlockSpec((TS, 4 * BR), lambda i: (i, 0)), rspec],
        out_shape=[jax.ShapeDtypeStruct((S, 12 * BR), BF16), jax.ShapeDtypeStruct((8, BR), F32)],
        scratch_shapes=[pltpu.VMEM((TS + 8, BR), F32), pltpu.VMEM((TS + 8, BR), F32)],
        compiler_params=_params("arbitrary"),
    )(proj, proj, proj, proj, dcat, proj, proj, proj, proj, dcat, w8)


def _attn_mix_bwd(proj, dcat, os_, ls_, head_ones, BR):
    S = proj.shape[0]
    TS = _tile(S, 512)

    def body(dy_ref, g_ref, o1, o2, o3, l1, l2, l3, ones_ref, d1, d2, d3, e1, e2, e3, dg_ref):
        w1, w2, w3 = _mix_weights(l1[...], l2[...], l3[...])
        mix = w1 * o1[...] + w2 * o2[...] + w3 * o3[...]
        g = g_ref[...]
        dy = dy_ref[...]
        dmix = dy * _silu(g)
        dg_ref[...] = dy * mix * _dsilu(g)
        t = jnp.dot(dmix * mix, ones_ref[...], preferred_element_type=F32, precision=lax.Precision.HIGHEST)
        d1[...] = w1 * dmix
        d2[...] = w2 * dmix
        d3[...] = w3 * dmix
        e1[...] = -w1 * t
        e2[...] = -w2 * t
        e3[...] = -w3 * t

    spec = pl.BlockSpec((TS, BR), lambda i: (i, 0))
    outs = _call(
        body, name="attn_mix_bwd", grid=(S // TS,),
        in_specs=[_chunk(TS, BR, 1), _chunk(TS, BR, 7)] + [spec] * 6 + [pl.BlockSpec((BR, BR), lambda i: (0, 0))],
        out_specs=[spec] * 7,
        out_shape=[jax.ShapeDtypeStruct((S, BR), F32)] * 7,
        compiler_params=_params("arbitrary"),
    )(dcat, proj, *os_, *ls_, head_ones)
    return outs[0:3], outs[3:6], outs[6]


def _attn_bwd(proj, do, e, lse, bias, dil, BR):
    S = proj.shape[0]
    HD, U, NU, nbu, hpb, HBK = _attn_geometry(S, BR, dil)
    scale = HD ** -0.5
    combos = [(n, r) for n in range(nbu) for r in range(dil)]

    def body(q_ref, kc_ref, kp_ref, vc_ref, vp_ref, do_ref, e_ref, l_ref, b_ref, dq_ref, dk_ref, dv_ref, db_ref,
             sbuf, dpbuf, pbuf, dsbuf, dkacc, dvacc, nxk, nxv, cark, carv):
        j = pl.program_id(1)
        mu = NU - 1 - j
        band, is_cur = _attn_mask()
        first = jnp.logical_and(band, jnp.logical_or(is_cur, mu > 0))

        @pl.when(j == 0)
        def _():
            cark[...] = jnp.zeros_like(cark)
            carv[...] = jnp.zeros_like(carv)
            db_ref[...] = jnp.zeros_like(db_ref)

        for ref in (dkacc, dvacc, nxk, nxv):
            ref[...] = jnp.zeros_like(ref)

        def keys(ref_c, ref_p, n, r):
            prev = ref_c[_attn_rows(n - 1, r, dil), :] if n > 0 else ref_p[_attn_rows(nbu - 1, r, dil), :]
            return jnp.concatenate([prev, ref_c[_attn_rows(n, r, dil), :]], axis=0).astype(BF16)

        for c, (n, r) in enumerate(combos):
            rows = _attn_rows(n, r, dil)
            q = (q_ref[rows, :] * scale).astype(BF16)
            dob = do_ref[rows, :].astype(BF16)
            kk, vv = keys(kc_ref, kp_ref, n, r), keys(vc_ref, vp_ref, n, r)
            for h in range(hpb):
                sl = slice(h * HD, (h + 1) * HD)
                sbuf[c * hpb + h] = _dot_nt(q[:, sl], kk[:, sl])
                dpbuf[c * hpb + h] = _dot_nt(dob[:, sl], vv[:, sl])
        dbs = [None] * hpb
        for c, (n, r) in enumerate(combos):
            rows = _attn_rows(n, r, dil)
            lse, ev = l_ref[rows, :], e_ref[rows, :]
            for h in range(hpb):
                one = slice(h * HD, h * HD + 1)
                s = jnp.where(band if n > 0 else first, sbuf[c * hpb + h] + b_ref[h], NEG)
                p = jnp.exp(s - lse[:, one])
                ds = p * (dpbuf[c * hpb + h] + ev[:, one])
                pbuf[c * hpb + h] = p.astype(BF16)
                dsbuf[c * hpb + h] = ds.astype(BF16)
                dbs[h] = ds if dbs[h] is None else dbs[h] + ds
        for h in range(hpb):
            db_ref[h] += dbs[h]
        for c, (n, r) in enumerate(combos):
            rows = _attn_rows(n, r, dil)
            q = (q_ref[rows, :] * scale).astype(BF16)
            dob = do_ref[rows, :].astype(BF16)
            kk = keys(kc_ref, kp_ref, n, r)
            dq, dkk, dvv = [], [], []
            for h in range(hpb):
                sl = slice(h * HD, (h + 1) * HD)
                dsb = dsbuf[c * hpb + h]
                dq.append(_dot(dsb, kk[:, sl]) * scale)
                dkk.append(_dot_tn(dsb, q[:, sl]))
                dvv.append(_dot_tn(pbuf[c * hpb + h], dob[:, sl]))
            dq_ref[rows, :] = jnp.concatenate(dq, axis=1)
            dkk, dvv = jnp.concatenate(dkk, axis=1), jnp.concatenate(dvv, axis=1)
            dkacc[rows, :] += dkk[BLK:2 * BLK, :]
            dvacc[rows, :] += dvv[BLK:2 * BLK, :]
            if n > 0:
                prow = _attn_rows(n - 1, r, dil)
                dkacc[prow, :] += dkk[0:BLK, :]
                dvacc[prow, :] += dvv[0:BLK, :]
            else:
                prow = _attn_rows(nbu - 1, r, dil)
                nxk[prow, :] += dkk[0:BLK, :]
                nxv[prow, :] += dvv[0:BLK, :]
        dk_ref[...] = dkacc[...] + cark[...]
        dv_ref[...] = dvacc[...] + carv[...]
        cark[...] = nxk[...]
        carv[...] = nxv[...]

    def cur(c):
        return pl.BlockSpec((U, ATT_LANES), lambda hb, j: (NU - 1 - j, c * HBK + hb))

    def prev(c):
        return pl.BlockSpec((U, ATT_LANES), lambda hb, j: (jnp.maximum(NU - 2 - j, 0), c * HBK + hb))

    own = pl.BlockSpec((U, ATT_LANES), lambda hb, j: (NU - 1 - j, hb))
    bspec = pl.BlockSpec((hpb, BLK, 2 * BLK), lambda hb, j: (hb, 0, 0))
    nhb = len(combos) * hpb
    unit = pltpu.VMEM((U, ATT_LANES), F32)
    return _call(
        body, name="attn_bwd_d%d" % dil, grid=(HBK, NU),
        in_specs=[cur(4), cur(5), prev(5), cur(6), prev(6), own, own, own, bspec],
        out_specs=[own, own, own, bspec],
        out_shape=[jax.ShapeDtypeStruct((S, BR), F32)] * 3 + [jax.ShapeDtypeStruct((8, BLK, 2 * BLK), F32)],
        scratch_shapes=[pltpu.VMEM((nhb, BLK, 2 * BLK), F32)] * 2 + [pltpu.VMEM((nhb, BLK, 2 * BLK), BF16)] * 2 + [unit] * 6,
        compiler_params=_params("arbitrary", "arbitrary"),
    )(proj, proj, proj, proj, proj, do, e, lse, bias)


def _attn_dsum(ds, dbg, dproj, BR):
    S = dbg.shape[0]
    TS = _tile(S, 512)

    def body(*refs):
        o_ref = refs[11]
        for k in range(3):
            o_ref[:, k * BR:(k + 1) * BR] = (refs[k][...] + refs[3 + k][...] + refs[6 + k][...]).astype(BF16)
        o_ref[:, 3 * BR:4 * BR] = refs[9][...].astype(BF16)

    spec = pl.BlockSpec((TS, BR), lambda i: (i, 0))
    return _call(
        body, name="attn_dsum", grid=(S // TS,),
        in_specs=[spec] * 10 + [pl.BlockSpec(memory_space=pl.ANY)],
        out_specs=pl.BlockSpec((TS, 4 * BR), lambda i: (i, 1)),
        out_shape=jax.ShapeDtypeStruct(dproj.shape, BF16),
        input_output_aliases={10: 0},
        compiler_params=_params("arbitrary"),
    )(*[t for trip in ds for t in trip], dbg, dproj)


def _lru_bwd(proj, h, dcat, rows, wa, wx, dproj, BR):
    S = proj.shape[0]
    TS = _tile(S, 256)
    PAD = TS // 2
    nt = S // TS

    def body(cx_ref, hx_ref, cg_ref, h_ref, hh_ref, dy_ref, r_ref, wa_ref, wx_ref, _,
             dp_ref, sum_ref, dwa_ref, dwx_ref, ext, hext, aext, dext, sa, sb, carry):
        i = pl.program_id(0)
        ti = nt - 1 - i

        @pl.when(i == 0)
        def _():
            sa[TS:TS + PAD, :] = jnp.ones((PAD, BR), F32)
            sb[TS:TS + PAD, :] = jnp.zeros((PAD, BR), F32)
            carry[...] = jnp.zeros_like(carry)
            dext[TS:TS + 8, :] = jnp.zeros((8, BR), F32)
            sum_ref[...] = jnp.zeros_like(sum_ref)
            dwa_ref[...] = jnp.zeros_like(dwa_ref)
            dwx_ref[...] = jnp.zeros_like(dwx_ref)

        w0, w1, w2, w3 = r_ref[0:1, :], r_ref[1:2, :], r_ref[2:3, :], r_ref[3:4, :]
        sp = r_ref[7:8, :]
        cx = cx_ref[...]
        ext[0:8, :] = jnp.where(ti > 0, hx_ref[...], 0.0)
        ext[8:8 + TS, :] = cx
        x3, x2, x1 = ext[pl.ds(5, TS), :], ext[pl.ds(6, TS), :], ext[pl.ds(7, TS), :]
        xc = x3 * w0 + x2 * w1 + x1 * w2 + cx * w3 + r_ref[4:5, :]
        wa_, wx_ = wa_ref[...], wx_ref[...]
        r, ig, a, mult = _lru_gates(xc, wa_, wx_, r_ref[5:6, :], r_ref[6:7, :], sp)
        cg, dy, hv = cg_ref[...], dy_ref[...], h_ref[...]
        dp_ref[:, BR:2 * BR] = (dy * hv * _dsilu(cg)).astype(BF16)
        aext[0:TS, :] = a
        aext[TS:TS + 8, :] = jnp.broadcast_to(carry[0:1, :], (8, BR))
        sa[0:TS, :] = aext[pl.ds(1, TS), :]
        sb[0:TS, :] = dy * _silu(cg)
        d = 1
        while d < TS:
            A, B = sa[0:TS, :], sb[0:TS, :]
            As, Bs = sa[pl.ds(d, TS), :], sb[pl.ds(d, TS), :]
            sb[0:TS, :] = B + A * Bs
            sa[0:TS, :] = A * As
            d *= 2
        gh = sb[0:TS, :] + sa[0:TS, :] * carry[1:2, :]
        carry[0:1, :] = a[0:1, :]
        carry[1:2, :] = gh[0:1, :]
        hext[0:8, :] = jnp.where(ti > 0, hh_ref[...], 0.0)
        hext[8:8 + TS, :] = hv
        da = gh * hext[pl.ds(7, TS), :]
        dmult = gh * ig * xc
        dig = gh * mult * xc
        dxc = gh * mult * ig
        dla = da * a - dmult * a * a / mult
        dpr = dla * (-LRU_C) * sp * r * (1.0 - r)
        dpi = dig * ig * (1.0 - ig)
        xb, dprb, dpib = xc.astype(BF16), dpr.astype(BF16), dpi.astype(BF16)
        dwa_ref[...] += _dot_tn(xb, dprb)
        dwx_ref[...] += _dot_tn(xb, dpib)
        dxc = dxc + _dot_nt(dprb, wa_) + _dot_nt(dpib, wx_)
        sum_ref[0:1, :] += _colsum(dxc * x3)
        sum_ref[1:2, :] += _colsum(dxc * x2)
        sum_ref[2:3, :] += _colsum(dxc * x1)
        sum_ref[3:4, :] += _colsum(dxc * cx)
        sum_ref[4:5, :] += _colsum(dxc)
        sum_ref[5:6, :] += _colsum(dpr)
        sum_ref[6:7, :] += _colsum(dpi)
        sum_ref[7:8, :] += _colsum(dla * (-LRU_C) * r)
        dext[0:TS, :] = dxc
        dcx = w3 * dxc + w2 * dext[pl.ds(1, TS), :] + w1 * dext[pl.ds(2, TS), :] + w0 * dext[pl.ds(3, TS), :]
        dext[TS:TS + 8, :] = dxc[0:8, :]
        dp_ref[:, 0:BR] = dcx.astype(BF16)

    def rev(k):
        return pl.BlockSpec((TS, BR), lambda i: (nt - 1 - i, k))

    def rev_halo(k):
        return pl.BlockSpec((8, BR), lambda i: (jnp.maximum((nt - 1 - i) * (TS // 8) - 1, 0), k))

    full = pl.BlockSpec((BR, BR), lambda i: (0, 0))
    rspec = pl.BlockSpec((8, BR), lambda i: (0, 0))
    return _call(
        body, name="lru_bwd", grid=(nt,),
        in_specs=[rev(8), rev_halo(8), rev(9), rev(0), rev_halo(0), rev(2), rspec, full, full, pl.BlockSpec(memory_space=pl.ANY)],
        out_specs=[pl.BlockSpec((TS, 2 * BR), lambda i: (nt - 1 - i, 4)), rspec, full, full],
        input_output_aliases={9: 0},
        out_shape=[jax.ShapeDtypeStruct(dproj.shape, BF16), jax.ShapeDtypeStruct((8, BR), F32),
                   jax.ShapeDtypeStruct((BR, BR), F32), jax.ShapeDtypeStruct((BR, BR), F32)],
        scratch_shapes=[pltpu.VMEM((TS + 8, BR), F32)] * 4 + [pltpu.VMEM((TS + PAD, BR), F32)] * 2 + [pltpu.VMEM((8, BR), F32)],
        compiler_params=_params("arbitrary"),
    )(proj, proj, proj, h, h, dcat, rows, wa, wx, dproj)


def _s5_bwd(proj, dcat, xre_all, xim_all, a8, cl, bbre, bbim, cre, cim, rows, wg, dproj, BR):
    S = proj.shape[0]
    W = a8.shape[1]
    TS = _tile(S, 256)
    nt = S // TS
    CB, SB, J = S5_CB, S5_SB, S5_LAGS
    nblk = BR // CB

    def body(u_ref, dg_ref, dy_ref, xre_ref, xim_ref, hre_ref, him_ref, a8_ref, cl_ref, bbre_ref, bbim_ref, cre_ref, cim_ref,
             r_ref, wg_ref, _, dp_ref, sum_ref, dwg_ref, da_ref, dbbre_ref, dbbim_ref, dcre_ref, dcim_ref,
             sre, sim, ere, eim, ypre_s, dext, dylag, carry):
        i = pl.program_id(0)
        ti = nt - 1 - i

        @pl.when(i == 0)
        def _():
            dext[TS:TS + 8, :] = jnp.zeros((8, BR), F32)
            carry[...] = jnp.zeros_like(carry)
            for ref in (sum_ref, dwg_ref, da_ref, dbbre_ref, dbbim_ref, dcre_ref, dcim_ref):
                ref[...] = jnp.zeros_like(ref)

        u, dg, dy = u_ref[...], dg_ref[...], dy_ref[...]
        for m in range(nblk):
            cs, ws = slice(m * CB, (m + 1) * CB), slice(m * SB, (m + 1) * SB)
            ypre_s[:, cs] = (_dot(xre_ref[:, ws].astype(BF16), cre_ref[ws, :]) - _dot(xim_ref[:, ws].astype(BF16), cim_ref[ws, :]))
        ypre = ypre_s[...] + r_ref[0:1, :] * u
        yg = _gelu(ypre)
        wg_ = wg_ref[...]
        s = _sigmoid(_dot(yg.astype(BF16), wg_) + r_ref[1:2, :])
        dgl = dy * _silu(dg)
        dp_ref[:, BR:2 * BR] = (dy * yg * s * _dsilu(dg)).astype(BF16)
        dps = dgl * yg * s * (1.0 - s)
        dpsb = dps.astype(BF16)
        sum_ref[1:2, :] += _colsum(dps)
        dwg_ref[...] += _dot_tn(yg.astype(BF16), dpsb)
        dyp = (dgl * s + _dot_nt(dpsb, wg_)) * _dgelu(ypre)
        sum_ref[0:1, :] += _colsum(dyp * u)
        dext[0:TS, :] = dyp
        for m in range(nblk):
            cs, ws = slice(m * CB, (m + 1) * CB), slice(m * SB, (m + 1) * SB)
            dypb = dext[0:TS, cs].astype(BF16)
            xre, xim = xre_ref[:, ws], xim_ref[:, ws]
            dcre_ref[ws, :] += _dot_tn(xre.astype(BF16), dypb)
            dcim_ref[ws, :] -= _dot_tn(xim.astype(BF16), dypb)
            for j in range(J):
                dylag[:, j * CB:(j + 1) * CB] = dext[pl.ds(j, TS), cs].astype(BF16)
            w = _dot(dylag[...], cl_ref[m])
            sre[...] = w[:, 0:SB]
            sim[...] = w[:, SB:2 * SB]
            ar = jnp.broadcast_to(a8_ref[0:1, ws], (8, SB))
            ai = jnp.broadcast_to(a8_ref[1:2, ws], (8, SB))
            gr, gi = carry[0:8, ws], carry[8:16, ws]
            for g in reversed(range(TS // 8)):
                rg = slice(8 * g, 8 * g + 8)
                gr, gi = sre[rg, :] + ar * gr + ai * gi, sim[rg, :] + ar * gi - ai * gr
                sre[rg, :] = gr
                sim[rg, :] = gi
            carry[0:8, ws] = gr
            carry[8:16, ws] = gi
            gre, gim = sre[...], sim[...]
            ere[0:8, :] = jnp.where(ti > 0, hre_ref[:, ws], 0.0)
            eim[0:8, :] = jnp.where(ti > 0, him_ref[:, ws], 0.0)
            ere[8:8 + TS, :] = xre
            eim[8:8 + TS, :] = xim
            xpr, xpi = ere[pl.ds(7, TS), :], eim[pl.ds(7, TS), :]
            da_ref[0:1, ws] += _colsum(gre * xpr + gim * xpi)
            da_ref[1:2, ws] += _colsum(gim * xpr - gre * xpi)
            greb, gimb = gre.astype(BF16), gim.astype(BF16)
            ub = u_ref[:, cs].astype(BF16)
            dbbre_ref[cs, :] += _dot_tn(ub, greb)
            dbbim_ref[cs, :] += _dot_tn(ub, gimb)
            du = _dot_nt(greb, bbre_ref[cs, :]) + _dot_nt(gimb, bbim_ref[cs, :]) + r_ref[0:1, cs] * dext[0:TS, cs]
            dp_ref[:, cs] = du.astype(BF16)
        dext[TS:TS + 8, :] = dext[0:8, :]

    def rev(width, k):
        return pl.BlockSpec((TS, width), lambda i: (nt - 1 - i, k))

    def rev_halo(width):
        return pl.BlockSpec((8, width), lambda i: (jnp.maximum((nt - 1 - i) * (TS // 8) - 1, 0), 0))

    def const(shape):
        return pl.BlockSpec(shape, lambda i: (0,) * len(shape))

    return _call(
        body, name="s5_bwd", grid=(nt,),
        in_specs=[rev(BR, 10), rev(BR, 11), rev(BR, 3), rev(W, 0), rev(W, 0), rev_halo(W), rev_halo(W),
                  const((8, W)), const((nblk, J * CB, 2 * SB)), const((BR, SB)), const((BR, SB)), const((W, CB)), const((W, CB)),
                  const((8, BR)), const((BR, BR)), pl.BlockSpec(memory_space=pl.ANY)],
        input_output_aliases={15: 0},
        out_specs=[pl.BlockSpec((TS, 2 * BR), lambda i: (nt - 1 - i, 5)), const((8, BR)), const((BR, BR)), const((8, W)),
                   const((BR, SB)), const((BR, SB)), const((W, CB)), const((W, CB))],
        out_shape=[jax.ShapeDtypeStruct(dproj.shape, BF16), jax.ShapeDtypeStruct((8, BR), F32), jax.ShapeDtypeStruct((BR, BR), F32),
                   jax.ShapeDtypeStruct((8, W), F32), jax.ShapeDtypeStruct((BR, SB), F32), jax.ShapeDtypeStruct((BR, SB), F32),
                   jax.ShapeDtypeStruct((W, CB), F32), jax.ShapeDtypeStruct((W, CB), F32)],
        scratch_shapes=[pltpu.VMEM((TS, SB), F32)] * 2 + [pltpu.VMEM((TS + 8, SB), F32)] * 2
        + [pltpu.VMEM((TS, BR), F32), pltpu.VMEM((TS + 8, BR), F32), pltpu.VMEM((TS, J * CB), BF16), pltpu.VMEM((16, W), F32)],
        compiler_params=_params("arbitrary"),
    )(proj, proj, dcat, xre_all, xim_all, xre_all, xim_all, a8, cl, bbre, bbim, cre, cim, rows, wg, dproj)


_WEIGHTS = ['rel_bias', 'w_ada', 'b_ada', 'w_in', 'conv_a', 'conv_c', 'conv_c_b', 'lru_wa', 'lru_ba', 'lru_wx', 'lru_bx',
            'lru_lambda', 's5_lam_re', 's5_lam_im', 's5_log_dt', 's5_b_re', 's5_b_im', 's5_c_re', 's5_c_im', 's5_d',
            's5_w_glu', 's5_b_glu', 'w_out', 'ln_g', 'ln_b']
_LAYER_SMALL = ['conv_a', 'conv_c', 'conv_c_b', 'lru_wa', 'lru_ba', 'lru_wx', 'lru_bx', 'lru_lambda', 's5_lam_re', 's5_lam_im',
                's5_log_dt', 's5_b_re', 's5_b_im', 's5_c_re', 's5_c_im', 's5_d', 's5_b_glu']
_SMALL = ['rel_bias'] + _LAYER_SMALL + ['ln_g', 'ln_b']


def _t5_bucket(dist):
    max_exact = REL_BUCKETS // 2
    nf = jnp.maximum(dist, 1).astype(F32)
    large = max_exact + (jnp.log(nf / max_exact) / math.log(REL_MAX_DIST / max_exact)
                         * (REL_BUCKETS - max_exact)).astype(jnp.int32)
    large = jnp.minimum(large, REL_BUCKETS - 1)
    return jnp.where(dist < max_exact, dist, large)


def _bias_tables(rel_bias):
    i = jnp.arange(BLK)[:, None]
    j = jnp.arange(2 * BLK)[None, :]
    delta = i + BLK - j
    out = []
    for window, dil in DILATIONS:
        bucket = _t5_bucket(jnp.clip(delta, 0, window // dil) * dil)
        onehot = (bucket[:, :, None] == jnp.arange(REL_BUCKETS)[None, None, :]).astype(F32)
        out.append(jnp.einsum('ijb,bh->hij', onehot, rel_bias, precision=lax.Precision.HIGHEST))
    return jnp.stack(out)


def _prep_layer(q):
    eye8 = jnp.eye(8, dtype=F32)
    G = q['s5_lam_re'].shape[0]
    nblk = G // 8

    def block_diag(w):
        hd = w.shape[1]
        return (w[:, :, None, :] * eye8[:, None, :, None]).reshape(8 * hd, 8 * hd)

    def compact_b(bb):
        t = jnp.transpose(bb.reshape(nblk, 8, S5_STATE, S5_CH), (0, 1, 3, 2))
        return (t[:, :, :, None, :] * eye8[None, :, None, :, None]).reshape(G * S5_CH, 8 * S5_STATE)

    def compact_c(cc):
        t = jnp.transpose(cc.reshape(nblk, 8, S5_CH, S5_STATE), (0, 1, 3, 2))
        return (t[:, :, :, None, :] * eye8[None, :, None, :, None]).reshape(G * S5_STATE, 8 * S5_CH)

    lam_re, lam_im = q['s5_lam_re'], q['s5_lam_im']
    dt = jnp.exp(q['s5_log_dt'])[:, None]
    mag = jnp.exp(lam_re * dt)
    ab_re = mag * jnp.cos(lam_im * dt)
    ab_im = mag * jnp.sin(lam_im * dt)
    den = lam_re * lam_re + lam_im * lam_im
    f_re = ((ab_re - 1.0) * lam_re + ab_im * lam_im) / den
    f_im = (ab_im * lam_re - (ab_re - 1.0) * lam_im) / den
    b_re, b_im = q['s5_b_re'], q['s5_b_im']
    bb_re = f_re[..., None] * b_re - f_im[..., None] * b_im
    bb_im = f_re[..., None] * b_im + f_im[..., None] * b_re
    return dict(
        conv_a=q['conv_a'], conv_c=q['conv_c'], conv_c_b=q['conv_c_b'], lru_ba=q['lru_ba'], lru_bx=q['lru_bx'],
        sp=jax.nn.softplus(-q['lru_lambda']), wa=block_diag(q['lru_wa']), wx=block_diag(q['lru_wx']),
        ar=ab_re.reshape(-1), ai=ab_im.reshape(-1), bbre=compact_b(bb_re), bbim=compact_b(bb_im),
        cre=compact_c(q['s5_c_re']), cim=compact_c(q['s5_c_im']), s5_d=q['s5_d'], s5_b_glu=q['s5_b_glu'])


def _s5_lag_weights(kp):
    J, CB, SB = S5_LAGS, S5_CB, S5_SB
    ar, ai = kp['ar'], kp['ai']
    W = ar.shape[0]
    nblk = W // SB
    cr, ci = jnp.ones_like(ar), jnp.zeros_like(ai)
    pows = []
    for _ in range(J + 1):
        pows.append((cr, ci))
        cr, ci = cr * ar - ci * ai, cr * ai + ci * ar
    pwr = jnp.transpose(jnp.stack([p[0] for p in pows[:J]]).reshape(J, nblk, SB), (1, 0, 2))[:, :, None, :]
    pwi = jnp.transpose(jnp.stack([p[1] for p in pows[:J]]).reshape(J, nblk, SB), (1, 0, 2))[:, :, None, :]

    def lagged(re, im, sign):
        out = jnp.concatenate([re * pwr - sign * im * pwi, sign * re * pwi + im * pwr], axis=-1)
        return out.reshape(nblk, J * CB, 2 * SB).astype(BF16)

    bbl = lagged(kp['bbre'].reshape(nblk, 1, CB, SB), kp['bbim'].reshape(nblk, 1, CB, SB), 1.0)
    c0r = jnp.transpose(kp['cre'].reshape(nblk, SB, CB), (0, 2, 1))[:, None]
    c0i = -jnp.transpose(kp['cim'].reshape(nblk, SB, CB), (0, 2, 1))[:, None]
    cl = lagged(c0r, c0i, -1.0)
    return _rows8([pows[J][0], pows[J][1]], W), bbl, cl


def _rows8(vecs, width):
    rows = [v.reshape(1, width).astype(F32) for v in vecs]
    return jnp.concatenate(rows + [jnp.zeros((8 - len(rows), width), F32)], axis=0)


def kernel(x, c, rel_bias, w_ada, b_ada, w_in, conv_a, conv_c, conv_c_b, lru_wa, lru_ba, lru_wx, lru_bx, lru_lambda, s5_lam_re, s5_lam_im, s5_log_dt, s5_b_re, s5_b_im, s5_c_re, s5_c_im, s5_d, s5_w_glu, s5_b_glu, w_out, ln_g, ln_b, loss_target, m_rel_bias, m_w_ada, m_b_ada, m_w_in, m_conv_a, m_conv_c, m_conv_c_b, m_lru_wa, m_lru_ba, m_lru_wx, m_lru_bx, m_lru_lambda, m_s5_lam_re, m_s5_lam_im, m_s5_log_dt, m_s5_b_re, m_s5_b_im, m_s5_c_re, m_s5_c_im, m_s5_d, m_s5_w_glu, m_s5_b_glu, m_w_out, m_ln_g, m_ln_b, v_rel_bias, v_w_ada, v_b_ada, v_w_in, v_conv_a, v_conv_c, v_conv_c_b, v_lru_wa, v_lru_ba, v_lru_wx, v_lru_bx, v_lru_lambda, v_s5_lam_re, v_s5_lam_im, v_s5_log_dt, v_s5_b_re, v_s5_b_im, v_s5_c_re, v_s5_c_im, v_s5_d, v_s5_w_glu, v_s5_b_glu, v_w_out, v_ln_g, v_ln_b):
    a = dict(locals())
    me = 4 * lax.axis_index("x") + 2 * lax.axis_index("y") + lax.axis_index("c")
    x0, target = x[0], loss_target[0]
    S, D = x0.shape
    BR = D // 4
    NS = w_in.shape[2]
    SH = BR // NDEV
    assert S % (BLK * DILATIONS[-1][1]) == 0 and BR % (8 * S5_CH) == 0

    small = _all_gather(_pack([c, conv_a, conv_c]), "gather_small")
    per_dev = [_unpack(small[d], [c.shape, conv_a.shape, conv_c.shape]) for d in range(NDEV)]
    c_all = jnp.concatenate([p[0] for p in per_dev], axis=0)
    conv_a_full = jnp.concatenate([p[1] for p in per_dev], axis=2)
    conv_c_full = jnp.concatenate([p[2] for p in per_dev], axis=2)
    me1 = me.reshape(1).astype(jnp.int32)
    w_in_bf, w_out_bf, w_glu_bf = w_in.astype(BF16), w_out.astype(BF16), s5_w_glu.astype(BF16)

    b_cols = lax.dynamic_slice(b_ada, (0, me * NS), (DEPTH, NS)).reshape(DEPTH, 1, NS)
    ada_all = _all_gather(_ada_cols(c_all, w_ada, b_cols), "gather_ada")
    ada_me = lax.dynamic_index_in_dim(ada_all, me, axis=2, keepdims=False)
    ada_me = jnp.transpose(ada_me, (1, 0, 2)).reshape(DEPTH, 3 * D)
    shift, scale, gate = ada_me[:, :D], ada_me[:, D:2 * D], ada_me[:, 2 * D:]

    bias_tabs, bias_pull = jax.vjp(_bias_tables, rel_bias)
    HD = BR // 8
    head_ones = jnp.kron(jnp.eye(8, dtype=F32), jnp.ones((HD, HD), F32))

    saved = []
    xl = x0
    for l in range(DEPTH):
        q = {n: a[n][l] for n in _LAYER_SMALL}
        q['conv_a'], q['conv_c'] = conv_a_full[l], conv_c_full[l]
        kp, pull = jax.vjp(_prep_layer, q)
        a8, bbl, cl = _s5_lag_weights(kp)
        mod = _rows8([1.0 + scale[l], shift[l]], D)
        arows = _rows8(list(kp['conv_a']), BR)
        lrows = _rows8(list(kp['conv_c']) + [kp['conv_c_b'], kp['lru_ba'], kp['lru_bx'], kp['sp']], BR)
        srows = _rows8([kp['s5_d'], kp['s5_b_glu']], BR)
        orows = _rows8([1.0 + gate[l], ln_g[l], ln_b[l]], D)
        wa, wx = kp['wa'].astype(BF16), kp['wx'].astype(BF16)
        s5w = [kp[n].astype(BF16) for n in ('bbre', 'bbim', 'cre', 'cim')]

        hbf = _modulate(xl, mod)
        proj, w_in_l, w_out_l, w_glu_l = _proj_gather(hbf, w_in_bf[l], [w_out_bf[l], w_glu_bf[l]], me1)
        w_out_l, w_glu_l = w_out_l.reshape(D, D), w_glu_l.reshape(BR, BR)
        ya = _conv_a_fwd(proj, arows, BR)
        os_, ls_ = [], []
        for g, (_, dil) in enumerate(DILATIONS):
            o, lse = _attn_fwd(proj, bias_tabs[g], dil, BR)
            os_.append(o)
            ls_.append(lse)
        yb = _attn_mix_fwd(proj, os_, ls_, BR)
        yc, hs = _lru_fwd(proj, lrows, wa, wx, BR)
        yd, xre, xim = _s5_fwd(proj, a8, bbl, s5w[2], s5w[3], srows, w_glu_l, BR)
        if l < DEPTH - 1:
            xn, y, cat = _out_ln([ya, yb, yc, yd], w_out_l, xl, orows)
        else:
            xn, y, cat, loss_local = _out_ln([ya, yb, yc, yd], w_out_l, xl, orows, target)
        saved.append(dict(x=xl, proj=proj, hbf=hbf, os=os_, ls=ls_, hs=hs, xre=xre, xim=xim, y=y, cat=cat, pull=pull,
                          mod=mod, arows=arows, lrows=lrows, srows=srows, orows=orows, wa=wa, wx=wx, s5w=s5w,
                          qw=(a8, cl), w_in=w_in_l, w_out=w_out_l, w_glu=w_glu_l))
        xl = xn

    dout = xl

    dbias = jnp.zeros_like(bias_tabs)
    lgrads, dada, dw_in_parts, dw_out_parts, glu_parts = [None] * DEPTH, [None] * DEPTH, [None] * DEPTH, [None] * DEPTH, [None] * DEPTH
    for l in reversed(range(DEPTH)):
        sv = saved[l]
        proj = sv['proj']
        dxres, dyb, dcat, lnsum = _ln_bwd(dout, sv['x'], sv['y'], sv['orows'], sv['w_out'])
        dw_out_parts[l] = _dw_scatter(sv['cat'], dyb, D // NDEV, True, me1, "dw_out_scatter")
        dproj, asum = _conv_a_bwd(proj, dcat, sv['arows'], BR)
        dos, es, dbg = _attn_mix_bwd(proj, dcat, sv['os'], sv['ls'], head_ones, BR)
        dqkv, dbs = [], []
        for g, (_, dil) in enumerate(DILATIONS):
            dq_, dk_, dv_, db_ = _attn_bwd(proj, dos[g], es[g], sv['ls'][g], bias_tabs[g], dil, BR)
            dqkv.append((dq_, dk_, dv_))
            dbs.append(db_)
        dbias = dbias + jnp.stack(dbs)
        dproj = _attn_dsum(dqkv, dbg, dproj, BR)
        dproj, lsum, dwa, dwx = _lru_bwd(proj, sv['hs'], dcat, sv['lrows'], sv['wa'], sv['wx'], dproj, BR)
        dproj, ssum, dwg, da_, dbbre, dbbim, dcre, dcim = _s5_bwd(
            proj, dcat, sv['xre'], sv['xim'], *sv['qw'], *sv['s5w'], sv['srows'], sv['w_glu'], dproj, BR)
        dkp = dict(conv_a=asum[0:3], conv_c=lsum[0:4], conv_c_b=lsum[4], lru_ba=lsum[5], lru_bx=lsum[6], sp=lsum[7],
                   wa=dwa, wx=dwx, ar=da_[0], ai=da_[1], bbre=dbbre, bbim=dbbim, cre=dcre, cim=dcim,
                   s5_d=ssum[0], s5_b_glu=ssum[1])
        lgrads[l] = dict(sv['pull'](dkp)[0], ln_g=lnsum[0], ln_b=lnsum[1])
        sides = [(dwg.reshape(NDEV, BR // NDEV, BR), False)]
        if l == 0:
            local = {'rel_bias': bias_pull(dbias)[0], 'loss': loss_local.reshape(1)}
            for n in _LAYER_SMALL + ['ln_g', 'ln_b']:
                local[n] = jnp.stack([lgrads[k][n] for k in range(DEPTH)])
            packed = _SMALL + ['loss']
            sides.append((_pack([local[n] for n in packed]), True))
        dout, msum, glu_parts[l], *gathered = _dh(dproj, sv['w_in'], dxres, sv['x'], sv['mod'], sides)
        dw_in_parts[l] = _dw_scatter(sv['hbf'], dproj, NS, False, me1, "dw_in_scatter")
        dada[l] = jnp.concatenate([msum[1], msum[0], lnsum[2]])
    grad_x = dout[None]

    out = {}

    def put(name, res):
        out[name] = res

    put('w_in', _adamw(dw_in_parts, w_in, m_w_in, v_w_in, "adamw_w_in"))
    put('w_out', _adamw(dw_out_parts, w_out, m_w_out, v_w_out, "adamw_w_out"))
    put('s5_w_glu', _adamw(glu_parts, s5_w_glu, m_s5_w_glu, v_s5_w_glu, "adamw_w_glu"))

    dada_all = _all_gather(jnp.stack(dada), "gather_dada")
    dada_cols = jnp.transpose(lax.dynamic_slice(dada_all, (0, 0, me * NS), (NDEV, DEPTH, NS)), (1, 0, 2))
    gw_ada, gb_ada = _ada_grads(c_all, dada_all, dada_cols)
    put('w_ada', _adamw([gw_ada[l] for l in range(DEPTH)], w_ada, m_w_ada, v_w_ada, "adamw_w_ada"))

    shapes = [local[n].shape for n in packed]
    gs = dict(zip(packed, _unpack(_sum8(gathered[0], "sum_small_grads"), shapes)))
    loss = gs['loss'][0]
    gs['conv_a'] = lax.dynamic_slice_in_dim(gs['conv_a'], me * SH, SH, axis=2)
    gs['conv_c'] = lax.dynamic_slice_in_dim(gs['conv_c'], me * SH, SH, axis=2)
    gs['b_ada'] = gb_ada
    names = _SMALL + ['b_ada']
    shapes = [a[n].shape for n in names]
    res = _adamw([_pack([gs[n] for n in names])], _pack([a[n] for n in names])[None], _pack([a['m_' + n] for n in names])[None],
                 _pack([a['v_' + n] for n in names])[None], "adamw_small")
    unpacked = [_unpack(r[0], shapes) for r in res]
    for k, n in enumerate(names):
        put(n, tuple(unpacked[j][k] for j in range(4)))

    return (loss, grad_x, *[out[n][0] for n in _WEIGHTS], *[out[n][1] for n in _WEIGHTS],
            *[out[n][2] for n in _WEIGHTS], *[out[n][3] for n in _WEIGHTS])
```

```python
import functools
import math

import jax
import jax.numpy as jnp
from jax import lax
from jax.experimental import pallas as pl
from jax.experimental.pallas import tpu as pltpu

F32 = jnp.float32
BF16 = jnp.bfloat16

NDEV = 8
DEPTH = 2
BLK = 128
DILATIONS = ((128, 1), (512, 4), (2048, 16))
REL_BUCKETS = 32
REL_MAX_DIST = 2048
LRU_C = 8.0
S5_CH = 16
S5_STATE = 64
ALPHA = (2 * DEPTH) ** 0.25
LN_EPS = 1e-5
ADAM_LR, ADAM_B1, ADAM_B2, ADAM_EPS, ADAM_WD, ADAM_STEP = 0.001, 0.9, 0.999, 1e-08, 0.01, 10
NEG = -1e30
VMEM_LIMIT = 56 * 1024 * 1024


def _call(body, **kw):
    return pl.pallas_call(body, **kw)


def _params(*sem):
    return pltpu.CompilerParams(dimension_semantics=sem, vmem_limit_bytes=VMEM_LIMIT)


def _sigmoid(x):
    return 1.0 / (1.0 + jnp.exp(-x))


def _silu(x):
    return x * _sigmoid(x)


def _dsilu(x):
    s = _sigmoid(x)
    return s * (1.0 + x * (1.0 - s))


_GC = math.sqrt(2.0 / math.pi)


def _gelu(x):
    return 0.5 * x * (1.0 + jnp.tanh(_GC * (x + 0.044715 * x * x * x)))


def _dgelu(x):
    t = jnp.tanh(_GC * (x + 0.044715 * x * x * x))
    return 0.5 * (1.0 + t) + 0.5 * x * (1.0 - t * t) * _GC * (1.0 + 3.0 * 0.044715 * x * x)


def _expm1(x):
    series = x * (1.0 + x * (1.0 / 2) * (1.0 + x * (1.0 / 3) * (1.0 + x * (1.0 / 4) * (1.0 + x * (1.0 / 5) * (1.0 + x * (1.0 / 6))))))
    return jnp.where(jnp.abs(x) < 0.25, series, jnp.exp(x) - 1.0)


def _dot(a, b):
    return jnp.dot(a, b, preferred_element_type=F32)


def _dot_nt(a, b):
    return lax.dot_general(a, b, (((1,), (1,)), ((), ())), preferred_element_type=F32)


def _dot_tn(a, b):
    return lax.dot_general(a, b, (((0,), (0,)), ((), ())), preferred_element_type=F32)


def _colsum(v):
    return jnp.sum(v, axis=0, keepdims=True)


def _all_gather(x, name):
    def body(x_ref, o_ref, send_sems, recv_sems, local_sem):
        ix, iy, ic = lax.axis_index("x"), lax.axis_index("y"), lax.axis_index("c")
        me, sibling = (ix, iy, ic), (ix, iy, 1 - ic)
        chips = [(1 - ix, iy), (ix, 1 - iy), (1 - ix, 1 - iy)]

        def slot(px, py, pc):
            return o_ref.at[4 * px + 2 * py + pc]

        def copy(k, block, to, src=None):
            return pltpu.make_async_remote_copy(
                src_ref=slot(*block) if src is None else src, dst_ref=slot(*block), send_sem=send_sems.at[k],
                recv_sem=recv_sems.at[k], device_id=to, device_id_type=pl.DeviceIdType.MESH)

        mine = pltpu.make_async_copy(x_ref, slot(*me), local_sem)
        mine.start()
        first = [copy(0, me, sibling, src=x_ref)] + [copy(1 + j, me, (*chip, ic), src=x_ref) for j, chip in enumerate(chips)]
        for cp in first:
            cp.start()
        passed = [copy(4 + j, (*chip, ic), sibling) for j, chip in enumerate(chips)]
        for j, chip in enumerate(chips):
            copy(1 + j, (*chip, ic), me).wait_recv()
            passed[j].start()
        copy(0, sibling, me).wait_recv()
        for j, chip in enumerate(chips):
            copy(4 + j, (*chip, 1 - ic), me).wait_recv()
        for cp in first + passed:
            cp.wait_send()
        mine.wait()

    return _call(
        body, name=name,
        out_shape=jax.ShapeDtypeStruct((NDEV,) + x.shape, x.dtype),
        in_specs=[pl.BlockSpec(memory_space=pl.ANY)],
        out_specs=pl.BlockSpec(memory_space=pl.ANY),
        scratch_shapes=[pltpu.SemaphoreType.DMA((NDEV - 1,)), pltpu.SemaphoreType.DMA((NDEV - 1,)), pltpu.SemaphoreType.DMA],
    )(x)


def _pack(arrs):
    parts = []
    for a in arrs:
        n = math.prod(a.shape)
        parts.append(jnp.pad(a.reshape(-1).astype(F32), (0, _pack_rows(n) * 128 - n)).reshape(-1, 128))
    rows = sum(p.shape[0] for p in parts)
    total = rows if rows <= 512 else -(-rows // 512) * 512
    if total > rows:
        parts.append(jnp.zeros((total - rows, 128), F32))
    return jnp.concatenate(parts, axis=0)


def _pack_rows(n):
    return -(-n // 1024) * 8


def _unpack(packed, shapes):
    out, row = [], 0
    for s in shapes:
        n = math.prod(s)
        rows = _pack_rows(n)
        out.append(packed[row:row + rows].reshape(-1)[:n].reshape(s))
        row += rows
    return out


def _sum8(parts, name):
    _, R, C = parts.shape
    TR = R
    for cand in (512, 256, 128, 64, 32, 16, 8):
        if R % cand == 0:
            TR = cand
            break

    def body(p_ref, o_ref):
        acc = p_ref[0]
        for d in range(1, NDEV):
            acc = acc + p_ref[d]
        o_ref[...] = acc

    return _call(
        body, name=name, grid=(R // TR,),
        in_specs=[pl.BlockSpec((NDEV, TR, C), lambda i: (0, i, 0))],
        out_specs=pl.BlockSpec((TR, C), lambda i: (i, 0)),
        out_shape=jax.ShapeDtypeStruct((R, C), F32),
        compiler_params=_params("arbitrary"),
    )(parts)


def _adamw(g_layers, w, m, v, name):
    L, R, C = w.shape
    g0 = g_layers[0]
    nparts = g0.shape[0] if g0.ndim == 3 else 0
    TR = R
    for cand in (256, 128, 64, 32, 16):
        if R % cand == 0 and cand * C <= 128 * 1024:
            TR = cand
            break

    def body(*refs):
        g_refs = refs[:L]
        w_ref, m_ref, v_ref, go_ref, d_ref, mo_ref, vo_ref = refs[L:]
        for ll in range(L):
            @pl.when(pl.program_id(0) == ll)
            def _(g_ref=g_refs[ll]):
                if nparts:
                    g = g_ref[0].astype(F32)
                    for d in range(1, nparts):
                        g = g + g_ref[d].astype(F32)
                else:
                    g = g_ref[...]
                m2 = ADAM_B1 * m_ref[0] + (1.0 - ADAM_B1) * g
                v2 = ADAM_B2 * v_ref[0] + (1.0 - ADAM_B2) * (g * g)
                m_hat = m2 / (1.0 - ADAM_B1 ** ADAM_STEP)
                v_hat = v2 / (1.0 - ADAM_B2 ** ADAM_STEP)
                go_ref[0] = g
                d_ref[0] = -ADAM_LR * (m_hat / (jnp.sqrt(v_hat) + ADAM_EPS) + ADAM_WD * w_ref[0])
                mo_ref[0] = m2
                vo_ref[0] = v2

    def gspec(ll):
        if nparts:
            return pl.BlockSpec((nparts, TR, C), lambda l, i: (0, jnp.where(l == ll, i, 0), 0))
        return pl.BlockSpec((TR, C), lambda l, i: (jnp.where(l == ll, i, 0), 0))

    spec = pl.BlockSpec((1, TR, C), lambda l, i: (l, i, 0))
    return _call(
        body, name=name, grid=(L, R // TR),
        in_specs=[gspec(ll) for ll in range(L)] + [spec, spec, spec], out_specs=[spec] * 4,
        out_shape=[jax.ShapeDtypeStruct((L, R, C), F32)] * 4,
        compiler_params=_params("arbitrary", "arbitrary"),
    )(*g_layers, w, m, v)


def _ada_cols(c_all, w_ada, b_cols):
    L, D, NS = w_ada.shape

    def body(c_ref, w_ref, b_ref, o_ref):
        cond = _silu(c_ref[...]).astype(BF16)
        o_ref[0] = _dot(cond, w_ref[0].astype(BF16)) + b_ref[0]

    return _call(
        body, name="ada_cols", grid=(L,),
        in_specs=[pl.BlockSpec((NDEV, D), lambda l: (0, 0)), pl.BlockSpec((1, D, NS), lambda l: (l, 0, 0)),
                  pl.BlockSpec((1, 1, NS), lambda l: (l, 0, 0))],
        out_specs=pl.BlockSpec((1, NDEV, NS), lambda l: (l, 0, 0)),
        out_shape=jax.ShapeDtypeStruct((L, NDEV, NS), F32),
        compiler_params=_params("arbitrary"),
    )(c_all, w_ada, b_cols)


def _ada_grads(c_all, dada_all, dada_cols):
    _, D = c_all.shape
    L, _, NS = dada_cols.shape
    D3 = dada_all.shape[2]

    def body(c_ref, da_ref, dc_ref, gw_ref, gb_ref):
        cond = _silu(c_ref[...]).astype(BF16)
        gw_ref[0] = _dot_tn(cond, dc_ref[0].astype(BF16))
        acc = da_ref[0]
        for d in range(1, NDEV):
            acc = acc + da_ref[d]
        gb_ref[...] = acc

    return _call(
        body, name="ada_grads", grid=(L,),
        in_specs=[pl.BlockSpec((NDEV, D), lambda l: (0, 0)), pl.BlockSpec((NDEV, L, D3), lambda l: (0, 0, 0)),
                  pl.BlockSpec((1, NDEV, NS), lambda l: (l, 0, 0))],
        out_specs=[pl.BlockSpec((1, D, NS), lambda l: (l, 0, 0)), pl.BlockSpec((L, D3), lambda l: (0, 0))],
        out_shape=[jax.ShapeDtypeStruct((L, D, NS), F32), jax.ShapeDtypeStruct((L, D3), F32)],
        compiler_params=_params("arbitrary"),
    )(c_all, dada_all, dada_cols)


def _tile(S, want):
    return want if S % want == 0 else S


def _modulate(x, mod):
    S, D = x.shape
    TS = _tile(S, 512)

    def body(x_ref, mod_ref, h_ref):
        h_ref[...] = (x_ref[...] * mod_ref[0:1, :] + mod_ref[1:2, :]).astype(BF16)

    spec = pl.BlockSpec((TS, D), lambda i: (i, 0))
    return _call(
        body, name="modulate", grid=(S // TS,),
        in_specs=[spec, pl.BlockSpec((8, D), lambda i: (0, 0))], out_specs=spec,
        out_shape=jax.ShapeDtypeStruct((S, D), BF16),
        compiler_params=_params("arbitrary"),
    )(x, mod)


def _peer(k):
    ix, iy, ic = lax.axis_index("x"), lax.axis_index("y"), lax.axis_index("c")
    bx, by, bc = (k >> 2) & 1, (k >> 1) & 1, k & 1
    px, py, pc = ix + bx - 2 * ix * bx, iy + by - 2 * iy * by, ic + bc - 2 * ic * bc
    return (px, py, pc), 4 * px + 2 * py + pc


GATHER_ORDER = (0, 1, 2, 4, 6, 3, 5, 7)
GATHER_DIRECT = (1, 2, 4, 6)
SCATTER_ORDER = (3, 2, 5, 4, 7, 6, 1, 0)


def _offset(order, k):
    off = jnp.int32(order[-1])
    for step in reversed(range(len(order) - 1)):
        off = jnp.where(k == step, jnp.int32(order[step]), off)
    return off


def _proj_gather(h, w_shard, extras, me1):
    assert GATHER_ORDER[0] == 0
    S, D = h.shape
    NS = w_shard.shape[1]
    TS = _tile(S, 512)
    nt = S // TS
    ne = len(extras)

    def body(me_ref, h_hbm, w_ref, *rest):
        x_refs, o_ref, wg_ref = rest[:ne], rest[ne], rest[ne + 1]
        xg_refs = rest[ne + 2:2 * ne + 2]
        h_vmem, wbuf, send_sems, recv_sems, local_sems, load_sems, h_sems = rest[2 * ne + 2:]
        k, i = pl.program_id(0), pl.program_id(1)
        me = me_ref[0]
        off = _offset(GATHER_ORDER, k)

        def h_tile(t):
            rows = pl.ds(pl.multiple_of(t * TS, TS), TS)
            return pltpu.make_async_copy(h_hbm.at[rows, :], h_vmem.at[rows, :], h_sems.at[t])

        def push(src, dst, which, kk):
            peer, _ = _peer(kk)
            return pltpu.make_async_remote_copy(
                src_ref=src, dst_ref=dst.at[me], send_sem=send_sems.at[which, kk], recv_sem=recv_sems.at[which, kk],
                device_id=peer, device_id_type=pl.DeviceIdType.MESH)

        def own(src, dst, which):
            return pltpu.make_async_copy(src, dst.at[me], local_sems.at[which])

        pairs = [(w_ref, wg_ref)] + list(zip(x_refs, xg_refs))

        @pl.when(jnp.logical_and(k == 0, i == 0))
        def _():
            for which, (src, dst) in enumerate(pairs):
                own(src, dst, which).start()
                for kk in (GATHER_DIRECT if which == 0 else GATHER_ORDER[1:]):
                    push(src, dst, which, kk).start()
            first = pltpu.make_async_copy(w_ref, wbuf.at[0], load_sems.at[0])
            first.start()
            for t in range(nt):
                h_tile(t).start()
            first.wait()

        @pl.when(jnp.logical_and(k > 0, i == 0))
        def _():
            push(w_ref, wg_ref, 0, off).wait_recv()
            _, blk = _peer(off)
            for step, o in enumerate(GATHER_ORDER):
                if o in GATHER_DIRECT and o > 1:
                    @pl.when(k == step)
                    def _(o=o):
                        sibling, _ = _peer(1)
                        pltpu.make_async_remote_copy(
                            src_ref=wg_ref.at[blk], dst_ref=wg_ref.at[blk], send_sem=send_sems.at[0, o + 1],
                            recv_sem=recv_sems.at[0, o + 1], device_id=sibling, device_id_type=pl.DeviceIdType.MESH).start()
            load = pltpu.make_async_copy(wg_ref.at[blk], wbuf.at[k % 2], load_sems.at[k % 2])
            load.start()
            load.wait()

        @pl.when(k == 0)
        def _():
            h_tile(i).wait()

        o_ref[...] = _dot(h_vmem[pl.ds(pl.multiple_of(i * TS, TS), TS), :], wbuf[k % 2])

        @pl.when(jnp.logical_and(k == NDEV - 1, i == nt - 1))
        def _():
            for which, (src, dst) in enumerate(pairs):
                for kk in range(1, NDEV):
                    cp = push(src, dst, which, kk)
                    cp.wait_send()
                    if which > 0:
                        cp.wait_recv()
                own(src, dst, which).wait()

    def col(k, i, me_ref):
        m, off = me_ref[0], _offset(GATHER_ORDER, k)
        return i, (m | off) - (m & off)

    anyspec = pl.BlockSpec(memory_space=pl.ANY)
    grid_spec = pltpu.PrefetchScalarGridSpec(
        num_scalar_prefetch=1, grid=(NDEV, nt),
        in_specs=[anyspec, anyspec] + [anyspec] * ne,
        out_specs=[pl.BlockSpec((TS, NS), col), anyspec] + [anyspec] * ne,
        scratch_shapes=[pltpu.VMEM((S, D), BF16), pltpu.VMEM((2, D, NS), BF16), pltpu.SemaphoreType.DMA((1 + ne, NDEV)),
                        pltpu.SemaphoreType.DMA((1 + ne, NDEV)), pltpu.SemaphoreType.DMA((1 + ne,)), pltpu.SemaphoreType.DMA((2,)),
                        pltpu.SemaphoreType.DMA((nt,))])
    return _call(
        body, name="proj_gather", grid_spec=grid_spec,
        out_shape=[jax.ShapeDtypeStruct((S, NDEV * NS), F32), jax.ShapeDtypeStruct((NDEV, D, NS), BF16)]
        + [jax.ShapeDtypeStruct((NDEV,) + e.shape, e.dtype) for e in extras],
        compiler_params=_params("arbitrary", "arbitrary"),
    )(me1, h, w_shard, *extras)


def _dw_scatter(a, b, blk, by_rows, me1, name):
    S = a.shape[0]
    TK = _tile(S, 512)
    nk = S // TK
    shape = (blk, b.shape[1]) if by_rows else (a.shape[1], blk)
    last = NDEV - 1
    assert all(o & 1 for o in SCATTER_ORDER[0::2]) and not any(o & 1 for o in SCATTER_ORDER[1::2]) and SCATTER_ORDER[-1] == 0
    assert all(SCATTER_ORDER[s] == SCATTER_ORDER[s + 1] + 1 for s in range(0, NDEV, 2))

    def body(me_ref, a_ref, b_ref, recv_ref, acc, stage, sib, send_sems, recv_sems, local_sem):
        k, kk = pl.program_id(0), pl.program_id(1)

        def chip_of(step):
            return _offset(SCATTER_ORDER, step) >> 1

        def to_sibling(j, slot):
            sibling, _ = _peer(1)
            return pltpu.make_async_remote_copy(
                src_ref=stage.at[slot], dst_ref=sib.at[j], send_sem=send_sems.at[0, j], recv_sem=recv_sems.at[0, j],
                device_id=sibling, device_id_type=pl.DeviceIdType.MESH)

        def to_chip(j, slot):
            peer, _ = _peer(2 * j)
            return pltpu.make_async_remote_copy(
                src_ref=stage.at[slot], dst_ref=recv_ref.at[j], send_sem=send_sems.at[1, j], recv_sem=recv_sems.at[1, j],
                device_id=peer, device_id_type=pl.DeviceIdType.MESH)

        own = pltpu.make_async_copy(stage.at[last % 2], recv_ref.at[0], local_sem)

        @pl.when(kk == 0)
        def _():
            acc[...] = jnp.zeros_like(acc)
        acc[...] += _dot_tn(a_ref[...], b_ref[...])

        @pl.when(kk == nk - 1)
        def _():
            j = chip_of(k)
            even_step = k % 2 == 0

            @pl.when(jnp.logical_and(k >= 2, even_step))
            def _():
                to_sibling(chip_of(k - 2), k % 2).wait_send()

            @pl.when(jnp.logical_and(k >= 3, jnp.logical_not(even_step)))
            def _():
                to_chip(chip_of(k - 2), k % 2).wait_send()

            @pl.when(even_step)
            def _():
                stage[k % 2] = acc[...].astype(BF16)
                to_sibling(j, k % 2).start()

            @pl.when(jnp.logical_not(even_step))
            def _():
                to_sibling(j, 0).wait_recv()
                stage[k % 2] = (acc[...] + sib[j].astype(F32)).astype(BF16)

                @pl.when(k < last)
                def _():
                    to_chip(j, k % 2).start()

                @pl.when(k == last)
                def _():
                    own.start()
                    to_sibling(SCATTER_ORDER[last - 1] >> 1, (last - 1) % 2).wait_send()
                    own.wait()
                    for jj in range(1, NDEV // 2):
                        to_chip(jj, 0).wait_recv()

    def blk_of(k, me_ref):
        m, off = me_ref[0], _offset(SCATTER_ORDER, k)
        return (m | off) - (m & off)

    if by_rows:
        in_specs = [pl.BlockSpec((TK, blk), lambda k, kk, me_ref: (kk, blk_of(k, me_ref))),
                    pl.BlockSpec((TK, b.shape[1]), lambda k, kk, me_ref: (kk, 0))]
    else:
        in_specs = [pl.BlockSpec((TK, a.shape[1]), lambda k, kk, me_ref: (kk, 0)),
                    pl.BlockSpec((TK, blk), lambda k, kk, me_ref: (kk, blk_of(k, me_ref)))]
    grid_spec = pltpu.PrefetchScalarGridSpec(
        num_scalar_prefetch=1, grid=(NDEV, nk), in_specs=in_specs,
        out_specs=pl.BlockSpec(memory_space=pl.ANY),
        scratch_shapes=[pltpu.VMEM(shape, F32), pltpu.VMEM((2,) + shape, BF16), pltpu.VMEM((NDEV // 2,) + shape, BF16),
                        pltpu.SemaphoreType.DMA((2, NDEV // 2)), pltpu.SemaphoreType.DMA((2, NDEV // 2)), pltpu.SemaphoreType.DMA])
    return _call(
        body, name=name, grid_spec=grid_spec,
        out_shape=jax.ShapeDtypeStruct((NDEV // 2,) + shape, BF16),
        compiler_params=_params("arbitrary", "arbitrary"),
    )(me1, a, b)


def _chunk(TS, BR, k):
    return pl.BlockSpec((TS, BR), lambda i: (i, k))


def _halo_prev(TS, BR, k):
    return pl.BlockSpec((8, BR), lambda i: (jnp.maximum(i * (TS // 8) - 1, 0), k))


def _conv_a_fwd(proj, w8, BR):
    S = proj.shape[0]
    TS = _tile(S, 512)

    def body(ab_ref, ac_ref, ax_ref, ag_ref, hc_ref, hx_ref, w_ref, o_ref, ext):
        i = pl.program_id(0)
        u = ac_ref[...] * ax_ref[...]
        ext[0:8, :] = jnp.where(i > 0, hc_ref[...] * hx_ref[...], 0.0)
        ext[8:8 + TS, :] = u
        conv = ext[pl.ds(6, TS), :] * w_ref[0:1, :] + ext[pl.ds(7, TS), :] * w_ref[1:2, :] + u * w_ref[2:3, :]
        o_ref[...] = ab_ref[...] * conv * _silu(ag_ref[...])

    return _call(
        body, name="conv_a_fwd", grid=(S // TS,),
        in_specs=[_chunk(TS, BR, 0), _chunk(TS, BR, 1), _chunk(TS, BR, 2), _chunk(TS, BR, 3),
                  _halo_prev(TS, BR, 1), _halo_prev(TS, BR, 2), pl.BlockSpec((8, BR), lambda i: (0, 0))],
        out_specs=pl.BlockSpec((TS, BR), lambda i: (i, 0)),
        out_shape=jax.ShapeDtypeStruct((S, BR), F32),
        scratch_shapes=[pltpu.VMEM((TS + 8, BR), F32)],
        compiler_params=_params("arbitrary"),
    )(proj, proj, proj, proj, proj, proj, w8)


ATT_UNIT = 2048
ATT_LANES = 128


def _attn_mask():
    i = lax.broadcasted_iota(jnp.int32, (BLK, 2 * BLK), 0)
    j = lax.broadcasted_iota(jnp.int32, (BLK, 2 * BLK), 1)
    return jnp.logical_and(j >= i, j <= i + BLK), j >= BLK


def _attn_rows(n, r, dil):
    return pl.ds(n * BLK * dil + r, BLK, stride=dil) if dil > 1 else pl.ds(n * BLK, BLK)


def _attn_geometry(S, BR, dil):
    HD = BR // 8
    U = ATT_UNIT
    assert S % U == 0 and BR % ATT_LANES == 0 and ATT_LANES % HD == 0 and U % (BLK * dil) == 0
    return HD, U, S // U, U // (BLK * dil), ATT_LANES // HD, BR // ATT_LANES


def _attn_fwd(proj, bias, dil, BR):
    S = proj.shape[0]
    HD, U, NU, nbu, hpb, HBK = _attn_geometry(S, BR, dil)
    scale = HD ** -0.5
    combos = [(n, r) for n in range(nbu) for r in range(dil)]

    def body(q_ref, kc_ref, kp_ref, vc_ref, vp_ref, b_ref, o_ref, l_ref, sbuf, pbuf):
        m = pl.program_id(1)
        band, is_cur = _attn_mask()
        first = jnp.logical_and(band, jnp.logical_or(is_cur, m > 0))

        def keys(ref_c, ref_p, n, r):
            prev = ref_c[_attn_rows(n - 1, r, dil), :] if n > 0 else ref_p[_attn_rows(nbu - 1, r, dil), :]
            return jnp.concatenate([prev, ref_c[_attn_rows(n, r, dil), :]], axis=0).astype(BF16)

        for c, (n, r) in enumerate(combos):
            q = (q_ref[_attn_rows(n, r, dil), :] * scale).astype(BF16)
            kk = keys(kc_ref, kp_ref, n, r)
            for h in range(hpb):
                sl = slice(h * HD, (h + 1) * HD)
                sbuf[c * hpb + h] = _dot_nt(q[:, sl], kk[:, sl])
        for c, (n, r) in enumerate(combos):
            lses = []
            for h in range(hpb):
                s = jnp.where(band if n > 0 else first, sbuf[c * hpb + h] + b_ref[h], NEG)
                mx = jnp.max(s, axis=-1, keepdims=True)
                p = jnp.exp(s - mx)
                l = jnp.sum(p, axis=-1, keepdims=True)
                pbuf[c * hpb + h] = (p * (1.0 / l)).astype(BF16)
                lses.append(jnp.broadcast_to(mx + jnp.log(l), (BLK, HD)))
            l_ref[_attn_rows(n, r, dil), :] = jnp.concatenate(lses, axis=1)
        for c, (n, r) in enumerate(combos):
            vv = keys(vc_ref, vp_ref, n, r)
            o_ref[_attn_rows(n, r, dil), :] = jnp.concatenate(
                [_dot(pbuf[c * hpb + h], vv[:, h * HD:(h + 1) * HD]) for h in range(hpb)], axis=1)

    def cur(c):
        return pl.BlockSpec((U, ATT_LANES), lambda hb, m: (m, c * HBK + hb))

    def prev(c):
        return pl.BlockSpec((U, ATT_LANES), lambda hb, m: (jnp.maximum(m - 1, 0), c * HBK + hb))

    ospec = pl.BlockSpec((U, ATT_LANES), lambda hb, m: (m, hb))
    nhb = len(combos) * hpb
    return _call(
        body, name="attn_fwd_d%d" % dil, grid=(HBK, NU),
        in_specs=[cur(4), cur(5), prev(5), cur(6), prev(6), pl.BlockSpec((hpb, BLK, 2 * BLK), lambda hb, m: (hb, 0, 0))],
        out_specs=[ospec, ospec],
        out_shape=[jax.ShapeDtypeStruct((S, BR), F32)] * 2,
        scratch_shapes=[pltpu.VMEM((nhb, BLK, 2 * BLK), F32), pltpu.VMEM((nhb, BLK, 2 * BLK), BF16)],
        compiler_params=_params("arbitrary", "arbitrary"),
    )(proj, proj, proj, proj, proj, bias)


def _mix_weights(l1, l2, l3):
    m = jnp.maximum(jnp.maximum(l1, l2), l3)
    e1, e2, e3 = jnp.exp(l1 - m), jnp.exp(l2 - m), jnp.exp(l3 - m)
    inv = 1.0 / (e1 + e2 + e3)
    return e1 * inv, e2 * inv, e3 * inv


def _attn_mix_fwd(proj, os_, ls_, BR):
    S = proj.shape[0]
    TS = _tile(S, 512)

    def body(o1, o2, o3, l1, l2, l3, g_ref, y_ref):
        w1, w2, w3 = _mix_weights(l1[...], l2[...], l3[...])
        y_ref[...] = (w1 * o1[...] + w2 * o2[...] + w3 * o3[...]) * _silu(g_ref[...])

    spec = pl.BlockSpec((TS, BR), lambda i: (i, 0))
    return _call(
        body, name="attn_mix_fwd", grid=(S // TS,),
        in_specs=[spec] * 6 + [_chunk(TS, BR, 7)], out_specs=spec,
        out_shape=jax.ShapeDtypeStruct((S, BR), F32),
        compiler_params=_params("arbitrary"),
    )(*os_, *ls_, proj)


def _lru_gates(xc, wa, wx, ba, bx, sp):
    xb = xc.astype(BF16)
    r = _sigmoid(_dot(xb, wa) + ba)
    ig = _sigmoid(_dot(xb, wx) + bx)
    la = -LRU_C * r * sp
    a = jnp.exp(la)
    mult = jnp.sqrt(-_expm1(2.0 * la))
    return r, ig, a, mult


def _lru_fwd(proj, rows, wa, wx, BR):
    S = proj.shape[0]
    TS = _tile(S, 256)
    PAD = TS // 2

    def body(cx_ref, hx_ref, cg_ref, r_ref, wa_ref, wx_ref, y_ref, h_ref, ext, sa, sb, carry):
        i = pl.program_id(0)

        @pl.when(i == 0)
        def _():
            sa[0:PAD, :] = jnp.ones((PAD, BR), F32)
            sb[0:PAD, :] = jnp.zeros((PAD, BR), F32)
            carry[...] = jnp.zeros_like(carry)

        cx = cx_ref[...]
        ext[0:8, :] = jnp.where(i > 0, hx_ref[...], 0.0)
        ext[8:8 + TS, :] = cx
        xc = (ext[pl.ds(5, TS), :] * r_ref[0:1, :] + ext[pl.ds(6, TS), :] * r_ref[1:2, :]
              + ext[pl.ds(7, TS), :] * r_ref[2:3, :] + cx * r_ref[3:4, :] + r_ref[4:5, :])
        _, ig, a, mult = _lru_gates(xc, wa_ref[...], wx_ref[...], r_ref[5:6, :], r_ref[6:7, :], r_ref[7:8, :])
        sa[PAD:PAD + TS, :] = a
        sb[PAD:PAD + TS, :] = mult * ig * xc
        d = 1
        while d < TS:
            A, B = sa[PAD:PAD + TS, :], sb[PAD:PAD + TS, :]
            As, Bs = sa[pl.ds(PAD - d, TS), :], sb[pl.ds(PAD - d, TS), :]
            sb[PAD:PAD + TS, :] = A * Bs + B
            sa[PAD:PAD + TS, :] = A * As
            d *= 2
        h = sb[PAD:PAD + TS, :] + sa[PAD:PAD + TS, :] * carry[0:1, :]
        carry[0:1, :] = h[TS - 1:TS, :]
        h_ref[...] = h
        y_ref[...] = h * _silu(cg_ref[...])

    full = pl.BlockSpec((BR, BR), lambda i: (0, 0))
    spec = pl.BlockSpec((TS, BR), lambda i: (i, 0))
    return _call(
        body, name="lru_fwd", grid=(S // TS,),
        in_specs=[_chunk(TS, BR, 8), _halo_prev(TS, BR, 8), _chunk(TS, BR, 9), pl.BlockSpec((8, BR), lambda i: (0, 0)), full, full],
        out_specs=[spec, spec],
        out_shape=[jax.ShapeDtypeStruct((S, BR), F32)] * 2,
        scratch_shapes=[pltpu.VMEM((TS + 8, BR), F32), pltpu.VMEM((PAD + TS, BR), F32), pltpu.VMEM((PAD + TS, BR), F32),
                        pltpu.VMEM((8, BR), F32)],
        compiler_params=_params("arbitrary"),
    )(proj, proj, proj, rows, wa, wx)


S5_LAGS = 8
S5_CB = 8 * S5_CH
S5_SB = 8 * S5_STATE


def _s5_fwd(proj, a8, bbl, cre, cim, rows, wg, BR):
    S = proj.shape[0]
    W = a8.shape[1]
    TS = _tile(S, 256)
    CB, SB, J = S5_CB, S5_SB, S5_LAGS
    nblk = BR // CB

    def body(u_ref, uh_ref, dg_ref, a8_ref, bbl_ref, cre_ref, cim_ref, r_ref, wg_ref,
             y_ref, xre_ref, xim_ref, uext, ulag, sre, sim, ypre_s, carry):
        i = pl.program_id(0)

        @pl.when(i == 0)
        def _():
            carry[...] = jnp.zeros_like(carry)

        uext[0:8, :] = jnp.where(i > 0, uh_ref[...], 0.0)
        uext[8:8 + TS, :] = u_ref[...]
        for m in range(nblk):
            cs, ws = slice(m * CB, (m + 1) * CB), slice(m * SB, (m + 1) * SB)
            for j in range(J):
                ulag[:, j * CB:(j + 1) * CB] = uext[pl.ds(8 - j, TS), cs].astype(BF16)
            w = _dot(ulag[...], bbl_ref[m])
            sre[...] = w[:, 0:SB]
            sim[...] = w[:, SB:2 * SB]
            ar = jnp.broadcast_to(a8_ref[0:1, ws], (8, SB))
            ai = jnp.broadcast_to(a8_ref[1:2, ws], (8, SB))
            xr, xi = carry[0:8, ws], carry[8:16, ws]
            for g in range(TS // 8):
                rg = slice(8 * g, 8 * g + 8)
                xr, xi = sre[rg, :] + ar * xr - ai * xi, sim[rg, :] + ar * xi + ai * xr
                sre[rg, :] = xr
                sim[rg, :] = xi
            carry[0:8, ws] = xr
            carry[8:16, ws] = xi
            xre, xim = sre[...], sim[...]
            xre_ref[:, ws] = xre
            xim_ref[:, ws] = xim
            ypre_s[:, cs] = _dot(xre.astype(BF16), cre_ref[ws, :]) - _dot(xim.astype(BF16), cim_ref[ws, :])
        dg = dg_ref[...]
        yg = _gelu(ypre_s[...] + r_ref[0:1, :] * u_ref[...])
        s = _sigmoid(_dot(yg.astype(BF16), wg_ref[...]) + r_ref[1:2, :])
        y_ref[...] = yg * s * _silu(dg)

    def const(shape):
        return pl.BlockSpec(shape, lambda i: (0,) * len(shape))

    return _call(
        body, name="s5_fwd", grid=(S // TS,),
        in_specs=[_chunk(TS, BR, 10), _halo_prev(TS, BR, 10), _chunk(TS, BR, 11), const((8, W)), const((nblk, J * CB, 2 * SB)),
                  const((W, CB)), const((W, CB)), const((8, BR)), const((BR, BR))],
        out_specs=[pl.BlockSpec((TS, BR), lambda i: (i, 0)), pl.BlockSpec((TS, W), lambda i: (i, 0)), pl.BlockSpec((TS, W), lambda i: (i, 0))],
        out_shape=[jax.ShapeDtypeStruct((S, BR), F32), jax.ShapeDtypeStruct((S, W), F32), jax.ShapeDtypeStruct((S, W), F32)],
        scratch_shapes=[pltpu.VMEM((TS + 8, BR), F32), pltpu.VMEM((TS, J * CB), BF16), pltpu.VMEM((TS, SB), F32), pltpu.VMEM((TS, SB), F32),
                        pltpu.VMEM((TS, BR), F32), pltpu.VMEM((16, W), F32)],
        compiler_params=_params("arbitrary"),
    )(proj, proj, proj, a8, bbl, cre, cim, rows, wg)


def _out_ln(ys, w_out, x, rows, target=None):
    S, D = x.shape
    BR = D // 4
    TS = _tile(S, 256)
    last = target is not None

    def body(ya, yb, yc, yd, w_ref, x_ref, r_ref, *rest):
        t_ref = rest[0] if last else None
        o_ref, y_ref, cat_ref = rest[1:4] if last else rest[0:3]
        cat = jnp.concatenate([ya[...], yb[...], yc[...], yd[...]], axis=1).astype(BF16)
        y = _dot(cat, w_ref[...])
        z = ALPHA * x_ref[...] + r_ref[0:1, :] * y
        mu = jnp.mean(z, axis=-1, keepdims=True)
        zc = z - mu
        var = jnp.mean(zc * zc, axis=-1, keepdims=True)
        xn = zc * lax.rsqrt(var + LN_EPS) * r_ref[1:2, :] + r_ref[2:3, :]
        y_ref[...] = y
        cat_ref[...] = cat
        if last:
            l_ref = rest[4]

            @pl.when(pl.program_id(0) == 0)
            def _():
                l_ref[...] = jnp.zeros_like(l_ref)
            diff = xn - t_ref[...]
            o_ref[...] = diff * (1.0 / D)
            l_ref[...] += jnp.sum(_colsum(diff * diff), axis=1, keepdims=True) * (0.5 / D)
        else:
            o_ref[...] = xn

    yspec = pl.BlockSpec((TS, BR), lambda i: (i, 0))
    spec = pl.BlockSpec((TS, D), lambda i: (i, 0))
    lspec = pl.BlockSpec((1, 1), lambda i: (0, 0))
    return _call(
        body, name="out_ln_loss" if last else "out_ln", grid=(S // TS,),
        in_specs=[yspec] * 4 + [pl.BlockSpec((D, D), lambda i: (0, 0)), spec, pl.BlockSpec((8, D), lambda i: (0, 0))] + [spec] * last,
        out_specs=[spec, spec, spec] + [lspec] * last,
        out_shape=[jax.ShapeDtypeStruct((S, D), F32), jax.ShapeDtypeStruct((S, D), F32), jax.ShapeDtypeStruct((S, D), BF16)]
        + [jax.ShapeDtypeStruct((1, 1), F32)] * last,
        compiler_params=_params("arbitrary"),
    )(*ys, w_out, x, rows, *([target] if last else []))


def _ln_bwd(dout, x, y, rows, w_out):
    S, D = x.shape
    TS = _tile(S, 256)

    def body(do_ref, x_ref, y_ref, r_ref, w_ref, dxr_ref, dyb_ref, dcat_ref, acc_ref):
        @pl.when(pl.program_id(0) == 0)
        def _():
            acc_ref[...] = jnp.zeros_like(acc_ref)
        g1, lg = r_ref[0:1, :], r_ref[1:2, :]
        yv = y_ref[...]
        z = ALPHA * x_ref[...] + g1 * yv
        mu = jnp.mean(z, axis=-1, keepdims=True)
        zc = z - mu
        var = jnp.mean(zc * zc, axis=-1, keepdims=True)
        rstd = lax.rsqrt(var + LN_EPS)
        xhat = zc * rstd
        do = do_ref[...]
        dxh = do * lg
        dz = rstd * (dxh - jnp.mean(dxh, axis=-1, keepdims=True) - xhat * jnp.mean(dxh * xhat, axis=-1, keepdims=True))
        dxr_ref[...] = ALPHA * dz
        dyb = (g1 * dz).astype(BF16)
        dyb_ref[...] = dyb
        dcat_ref[...] = _dot_nt(dyb, w_ref[...])
        acc_ref[0:1, :] += _colsum(do * xhat)
        acc_ref[1:2, :] += _colsum(do)
        acc_ref[2:3, :] += _colsum(dz * yv)

    spec = pl.BlockSpec((TS, D), lambda i: (i, 0))
    rspec = pl.BlockSpec((8, D), lambda i: (0, 0))
    return _call(
        body, name="ln_bwd", grid=(S // TS,),
        in_specs=[spec, spec, spec, rspec, pl.BlockSpec((D, D), lambda i: (0, 0))],
        out_specs=[spec, spec, spec, rspec],
        out_shape=[jax.ShapeDtypeStruct((S, D), F32), jax.ShapeDtypeStruct((S, D), BF16), jax.ShapeDtypeStruct((S, D), F32),
                   jax.ShapeDtypeStruct((8, D), F32)],
        compiler_params=_params("arbitrary"),
    )(dout, x, y, rows, w_out)


def _dh(dproj, wg, dxres, x, mod, sides):
    S, D = x.shape
    NS = wg.shape[2]
    TS = _tile(S, 256)
    nt = S // TS
    ns = len(sides)

    def body(dp_ref, w_hbm, dxr_ref, x_ref, mod_ref, *rest):
        s_refs, dx_ref, sum_ref, so_refs = rest[:ns], rest[ns], rest[ns + 1], rest[ns + 2:2 * ns + 2]
        w_vmem, sem, send_sems, recv_sems, local_sems = rest[2 * ns + 2:]
        i = pl.program_id(0)
        ix, iy, ic = lax.axis_index("x"), lax.axis_index("y"), lax.axis_index("c")
        me = 4 * ix + 2 * iy + ic

        def exchange(which):
            (_, gather), x_side, o_side = sides[which], s_refs[which], so_refs[which]
            copies = [pltpu.make_async_copy(x_side if gather else x_side.at[me], o_side.at[me], local_sems.at[which])]
            for k in range(1, NDEV):
                px = 1 - ix if k & 4 else ix
                py = 1 - iy if k & 2 else iy
                pc = 1 - ic if k & 1 else ic
                copies.append(pltpu.make_async_remote_copy(
                    src_ref=x_side if gather else x_side.at[4 * px + 2 * py + pc], dst_ref=o_side.at[me],
                    send_sem=send_sems.at[which, k - 1], recv_sem=recv_sems.at[which, k - 1],
                    device_id=(px, py, pc), device_id_type=pl.DeviceIdType.MESH))
            return copies

        @pl.when(i == 0)
        def _():
            load = pltpu.make_async_copy(w_hbm, w_vmem, sem)
            load.start()
            for which in range(ns):
                for cp in exchange(which):
                    cp.start()
            sum_ref[...] = jnp.zeros_like(sum_ref)
            load.wait()

        dh = _dot_nt(dp_ref[:, 0:NS], w_vmem[0])
        for j in range(1, NDEV):
            dh = dh + _dot_nt(dp_ref[:, j * NS:(j + 1) * NS], w_vmem[j])
        dx_ref[...] = dxr_ref[...] + dh * mod_ref[0:1, :]
        sum_ref[0:1, :] += _colsum(dh * x_ref[...])
        sum_ref[1:2, :] += _colsum(dh)

        @pl.when(i == nt - 1)
        def _():
            for which in range(ns):
                for cp in exchange(which):
                    cp.wait()

    spec = pl.BlockSpec((TS, D), lambda i: (i, 0))
    rspec = pl.BlockSpec((8, D), lambda i: (0, 0))
    anyspec = pl.BlockSpec(memory_space=pl.ANY)
    side_shapes = [jax.ShapeDtypeStruct(((NDEV,) + a.shape) if g else a.shape, a.dtype) for a, g in sides]
    return _call(
        body, name="dh", grid=(nt,),
        in_specs=[pl.BlockSpec((TS, NDEV * NS), lambda i: (i, 0)), anyspec, spec, spec, rspec] + [anyspec] * ns,
        out_specs=[spec, rspec] + [anyspec] * ns,
        out_shape=[jax.ShapeDtypeStruct((S, D), F32), jax.ShapeDtypeStruct((8, D), F32)] + side_shapes,
        scratch_shapes=[pltpu.VMEM((NDEV, D, NS), BF16), pltpu.SemaphoreType.DMA, pltpu.SemaphoreType.DMA((max(ns, 1), NDEV - 1)),
                        pltpu.SemaphoreType.DMA((max(ns, 1), NDEV - 1)), pltpu.SemaphoreType.DMA((max(ns, 1),))],
        compiler_params=_params("arbitrary"),
    )(dproj, wg, dxres, x, mod, *[a for a, _ in sides])


def _halo_next(S, TS, BR, k):
    last = S // 8 - 1
    return pl.BlockSpec((8, BR), lambda i: (jnp.minimum((i + 1) * (TS // 8), last), k))


def _conv_a_bwd(proj, dcat, w8, BR):
    S = proj.shape[0]
    TS = _tile(S, 512)
    nt = S // TS

    def body(ab_ref, ac_ref, ax_ref, ag_ref, dy_ref, hc_ref, hx_ref, nab_ref, nag_ref, ndy_ref, w_ref, dp_ref, dw_ref, ext, dext):
        i = pl.program_id(0)

        @pl.when(i == 0)
        def _():
            dw_ref[...] = jnp.zeros_like(dw_ref)

        w0, w1, w2 = w_ref[0:1, :], w_ref[1:2, :], w_ref[2:3, :]
        ab, ac, ax, ag, dy = ab_ref[...], ac_ref[...], ax_ref[...], ag_ref[...], dy_ref[...]
        u = ac * ax
        ext[0:8, :] = jnp.where(i > 0, hc_ref[...] * hx_ref[...], 0.0)
        ext[8:8 + TS, :] = u
        u1, u2 = ext[pl.ds(7, TS), :], ext[pl.ds(6, TS), :]
        conv = u2 * w0 + u1 * w1 + u * w2
        sg = _silu(ag)
        dconv = dy * ab * sg
        dext[0:TS, :] = dconv
        dext[TS:TS + 8, :] = jnp.where(i < nt - 1, ndy_ref[...] * nab_ref[...] * _silu(nag_ref[...]), 0.0)
        du = w2 * dconv + w1 * dext[pl.ds(1, TS), :] + w0 * dext[pl.ds(2, TS), :]
        dp_ref[:, 0:BR] = (dy * conv * sg).astype(BF16)
        dp_ref[:, BR:2 * BR] = (du * ax).astype(BF16)
        dp_ref[:, 2 * BR:3 * BR] = (du * ac).astype(BF16)
        dp_ref[:, 3 * BR:4 * BR] = (dy * ab * conv * _dsilu(ag)).astype(BF16)
        dw_ref[0:1, :] += _colsum(dconv * u2)
        dw_ref[1:2, :] += _colsum(dconv * u1)
        dw_ref[2:3, :] += _colsum(dconv * u)

    rspec = pl.BlockSpec((8, BR), lambda i: (0, 0))
    return _call(
        body, name="conv_a_bwd", grid=(nt,),
        in_specs=[_chunk(TS, BR, 0), _chunk(TS, BR, 1), _chunk(TS, BR, 2), _chunk(TS, BR, 3), _chunk(TS, BR, 0),
                  _halo_prev(TS, BR, 1), _halo_prev(TS, BR, 2), _halo_next(S, TS, BR, 0), _halo_next(S, TS, BR, 3),
                  _halo_next(S, TS, BR, 0), rspec],
        out_specs=[pl.BlockSpec((TS, 4 * BR), lambda i: (i, 0)), rspec],
        out_shape=[jax.ShapeDtypeStruct((S, 12 * BR), BF16), jax.ShapeDtypeStruct((8, BR), F32)],
        scratch_shapes=[pltpu.VMEM((TS + 8, BR), F32), pltpu.VMEM((TS + 8, BR), F32)],
        compiler_params=_params("arbitrary"),
    )(proj, proj, proj, proj, dcat, proj, proj, proj, proj, dcat, w8)


def _attn_mix_bwd(proj, dcat, os_, ls_, head_ones, BR):
    S = proj.shape[0]
    TS = _tile(S, 512)

    def body(dy_ref, g_ref, o1, o2, o3, l1, l2, l3, ones_ref, d1, d2, d3, e1, e2, e3, dg_ref):
        w1, w2, w3 = _mix_weights(l1[...], l2[...], l3[...])
        mix = w1 * o1[...] + w2 * o2[...] + w3 * o3[...]
        g = g_ref[...]
        dy = dy_ref[...]
        dmix = dy * _silu(g)
        dg_ref[...] = dy * mix * _dsilu(g)
        t = jnp.dot(dmix * mix, ones_ref[...], preferred_element_type=F32, precision=lax.Precision.HIGHEST)
        d1[...] = w1 * dmix
        d2[...] = w2 * dmix
        d3[...] = w3 * dmix
        e1[...] = -w1 * t
        e2[...] = -w2 * t
        e3[...] = -w3 * t

    spec = pl.BlockSpec((TS, BR), lambda i: (i, 0))
    outs = _call(
        body, name="attn_mix_bwd", grid=(S // TS,),
        in_specs=[_chunk(TS, BR, 1), _chunk(TS, BR, 7)] + [spec] * 6 + [pl.BlockSpec((BR, BR), lambda i: (0, 0))],
        out_specs=[spec] * 7,
        out_shape=[jax.ShapeDtypeStruct((S, BR), F32)] * 7,
        compiler_params=_params("arbitrary"),
    )(dcat, proj, *os_, *ls_, head_ones)
    return outs[0:3], outs[3:6], outs[6]


def _attn_bwd(proj, do, e, lse, bias, dil, BR):
    S = proj.shape[0]
    HD, U, NU, nbu, hpb, HBK = _attn_geometry(S, BR, dil)
    scale = HD ** -0.5
    combos = [(n, r) for n in range(nbu) for r in range(dil)]

    def body(q_ref, kc_ref, kp_ref, vc_ref, vp_ref, do_ref, e_ref, l_ref, b_ref, dq_ref, dk_ref, dv_ref, db_ref,
             sbuf, dpbuf, pbuf, dsbuf, dkacc, dvacc, nxk, nxv, cark, carv):
        j = pl.program_id(1)
        mu = NU - 1 - j
        band, is_cur = _attn_mask()
        first = jnp.logical_and(band, jnp.logical_or(is_cur, mu > 0))

        @pl.when(j == 0)
        def _():
            cark[...] = jnp.zeros_like(cark)
            carv[...] = jnp.zeros_like(carv)
            db_ref[...] = jnp.zeros_like(db_ref)

        for ref in (dkacc, dvacc, nxk, nxv):
            ref[...] = jnp.zeros_like(ref)

        def keys(ref_c, ref_p, n, r):
            prev = ref_c[_attn_rows(n - 1, r, dil), :] if n > 0 else ref_p[_attn_rows(nbu - 1, r, dil), :]
            return jnp.concatenate([prev, ref_c[_attn_rows(n, r, dil), :]], axis=0).astype(BF16)

        for c, (n, r) in enumerate(combos):
            rows = _attn_rows(n, r, dil)
            q = (q_ref[rows, :] * scale).astype(BF16)
            dob = do_ref[rows, :].astype(BF16)
            kk, vv = keys(kc_ref, kp_ref, n, r), keys(vc_ref, vp_ref, n, r)
            for h in range(hpb):
                sl = slice(h * HD, (h + 1) * HD)
                sbuf[c * hpb + h] = _dot_nt(q[:, sl], kk[:, sl])
                dpbuf[c * hpb + h] = _dot_nt(dob[:, sl], vv[:, sl])
        dbs = [None] * hpb
        for c, (n, r) in enumerate(combos):
            rows = _attn_rows(n, r, dil)
            lse, ev = l_ref[rows, :], e_ref[rows, :]
            for h in range(hpb):
                one = slice(h * HD, h * HD + 1)
                s = jnp.where(band if n > 0 else first, sbuf[c * hpb + h] + b_ref[h], NEG)
                p = jnp.exp(s - lse[:, one])
                ds = p * (dpbuf[c * hpb + h] + ev[:, one])
                pbuf[c * hpb + h] = p.astype(BF16)
                dsbuf[c * hpb + h] = ds.astype(BF16)
                dbs[h] = ds if dbs[h] is None else dbs[h] + ds
        for h in range(hpb):
            db_ref[h] += dbs[h]
        for c, (n, r) in enumerate(combos):
            rows = _attn_rows(n, r, dil)
            q = (q_ref[rows, :] * scale).astype(BF16)
            dob = do_ref[rows, :].astype(BF16)
            kk = keys(kc_ref, kp_ref, n, r)
            dq, dkk, dvv = [], [], []
            for h in range(hpb):
                sl = slice(h * HD, (h + 1) * HD)
                dsb = dsbuf[c * hpb + h]
                dq.append(_dot(dsb, kk[:, sl]) * scale)
                dkk.append(_dot_tn(dsb, q[:, sl]))
                dvv.append(_dot_tn(pbuf[c * hpb + h], dob[:, sl]))
            dq_ref[rows, :] = jnp.concatenate(dq, axis=1)
            dkk, dvv = jnp.concatenate(dkk, axis=1), jnp.concatenate(dvv, axis=1)
            dkacc[rows, :] += dkk[BLK:2 * BLK, :]
            dvacc[rows, :] += dvv[BLK:2 * BLK, :]
            if n > 0:
                prow = _attn_rows(n - 1, r, dil)
                dkacc[prow, :] += dkk[0:BLK, :]
                dvacc[prow, :] += dvv[0:BLK, :]
            else:
                prow = _attn_rows(nbu - 1, r, dil)
                nxk[prow, :] += dkk[0:BLK, :]
                nxv[prow, :] += dvv[0:BLK, :]
        dk_ref[...] = dkacc[...] + cark[...]
        dv_ref[...] = dvacc[...] + carv[...]
        cark[...] = nxk[...]
        carv[...] = nxv[...]

    def cur(c):
        return pl.BlockSpec((U, ATT_LANES), lambda hb, j: (NU - 1 - j, c * HBK + hb))

    def prev(c):
        return pl.BlockSpec((U, ATT_LANES), lambda hb, j: (jnp.maximum(NU - 2 - j, 0), c * HBK + hb))

    own = pl.BlockSpec((U, ATT_LANES), lambda hb, j: (NU - 1 - j, hb))
    bspec = pl.BlockSpec((hpb, BLK, 2 * BLK), lambda hb, j: (hb, 0, 0))
    nhb = len(combos) * hpb
    unit = pltpu.VMEM((U, ATT_LANES), F32)
    return _call(
        body, name="attn_bwd_d%d" % dil, grid=(HBK, NU),
        in_specs=[cur(4), cur(5), prev(5), cur(6), prev(6), own, own, own, bspec],
        out_specs=[own, own, own, bspec],
        out_shape=[jax.ShapeDtypeStruct((S, BR), F32)] * 3 + [jax.ShapeDtypeStruct((8, BLK, 2 * BLK), F32)],
        scratch_shapes=[pltpu.VMEM((nhb, BLK, 2 * BLK), F32)] * 2 + [pltpu.VMEM((nhb, BLK, 2 * BLK), BF16)] * 2 + [unit] * 6,
        compiler_params=_params("arbitrary", "arbitrary"),
    )(proj, proj, proj, proj, proj, do, e, lse, bias)


def _attn_dsum(ds, dbg, dproj, BR):
    S = dbg.shape[0]
    TS = _tile(S, 512)

    def body(*refs):
        o_ref = refs[11]
        for k in range(3):
            o_ref[:, k * BR:(k + 1) * BR] = (refs[k][...] + refs[3 + k][...] + refs[6 + k][...]).astype(BF16)
        o_ref[:, 3 * BR:4 * BR] = refs[9][...].astype(BF16)

    spec = pl.BlockSpec((TS, BR), lambda i: (i, 0))
    return _call(
        body, name="attn_dsum", grid=(S // TS,),
        in_specs=[spec] * 10 + [pl.BlockSpec(memory_space=pl.ANY)],
        out_specs=pl.BlockSpec((TS, 4 * BR), lambda i: (i, 1)),
        out_shape=jax.ShapeDtypeStruct(dproj.shape, BF16),
        input_output_aliases={10: 0},
        compiler_params=_params("arbitrary"),
    )(*[t for trip in ds for t in trip], dbg, dproj)


def _lru_bwd(proj, h, dcat, rows, wa, wx, dproj, BR):
    S = proj.shape[0]
    TS = _tile(S, 256)
    PAD = TS // 2
    nt = S // TS

    def body(cx_ref, hx_ref, cg_ref, h_ref, hh_ref, dy_ref, r_ref, wa_ref, wx_ref, _,
             dp_ref, sum_ref, dwa_ref, dwx_ref, ext, hext, aext, dext, sa, sb, carry):
        i = pl.program_id(0)
        ti = nt - 1 - i

        @pl.when(i == 0)
        def _():
            sa[TS:TS + PAD, :] = jnp.ones((PAD, BR), F32)
            sb[TS:TS + PAD, :] = jnp.zeros((PAD, BR), F32)
            carry[...] = jnp.zeros_like(carry)
            dext[TS:TS + 8, :] = jnp.zeros((8, BR), F32)
            sum_ref[...] = jnp.zeros_like(sum_ref)
            dwa_ref[...] = jnp.zeros_like(dwa_ref)
            dwx_ref[...] = jnp.zeros_like(dwx_ref)

        w0, w1, w2, w3 = r_ref[0:1, :], r_ref[1:2, :], r_ref[2:3, :], r_ref[3:4, :]
        sp = r_ref[7:8, :]
        cx = cx_ref[...]
        ext[0:8, :] = jnp.where(ti > 0, hx_ref[...], 0.0)
        ext[8:8 + TS, :] = cx
        x3, x2, x1 = ext[pl.ds(5, TS), :], ext[pl.ds(6, TS), :], ext[pl.ds(7, TS), :]
        xc = x3 * w0 + x2 * w1 + x1 * w2 + cx * w3 + r_ref[4:5, :]
        wa_, wx_ = wa_ref[...], wx_ref[...]
        r, ig, a, mult = _lru_gates(xc, wa_, wx_, r_ref[5:6, :], r_ref[6:7, :], sp)
        cg, dy, hv = cg_ref[...], dy_ref[...], h_ref[...]
        dp_ref[:, BR:2 * BR] = (dy * hv * _dsilu(cg)).astype(BF16)
        aext[0:TS, :] = a
        aext[TS:TS + 8, :] = jnp.broadcast_to(carry[0:1, :], (8, BR))
        sa[0:TS, :] = aext[pl.ds(1, TS), :]
        sb[0:TS, :] = dy * _silu(cg)
        d = 1
        while d < TS:
            A, B = sa[0:TS, :], sb[0:TS, :]
            As, Bs = sa[pl.ds(d, TS), :], sb[pl.ds(d, TS), :]
            sb[0:TS, :] = B + A * Bs
            sa[0:TS, :] = A * As
            d *= 2
        gh = sb[0:TS, :] + sa[0:TS, :] * carry[1:2, :]
        carry[0:1, :] = a[0:1, :]
        carry[1:2, :] = gh[0:1, :]
        hext[0:8, :] = jnp.where(ti > 0, hh_ref[...], 0.0)
        hext[8:8 + TS, :] = hv
        da = gh * hext[pl.ds(7, TS), :]
        dmult = gh * ig * xc
        dig = gh * mult * xc
        dxc = gh * mult * ig
        dla = da * a - dmult * a * a / mult
        dpr = dla * (-LRU_C) * sp * r * (1.0 - r)
        dpi = dig * ig * (1.0 - ig)
        xb, dprb, dpib = xc.astype(BF16), dpr.astype(BF16), dpi.astype(BF16)
        dwa_ref[...] += _dot_tn(xb, dprb)
        dwx_ref[...] += _dot_tn(xb, dpib)
        dxc = dxc + _dot_nt(dprb, wa_) + _dot_nt(dpib, wx_)
        sum_ref[0:1, :] += _colsum(dxc * x3)
        sum_ref[1:2, :] += _colsum(dxc * x2)
        sum_ref[2:3, :] += _colsum(dxc * x1)
        sum_ref[3:4, :] += _colsum(dxc * cx)
        sum_ref[4:5, :] += _colsum(dxc)
        sum_ref[5:6, :] += _colsum(dpr)
        sum_ref[6:7, :] += _colsum(dpi)
        sum_ref[7:8, :] += _colsum(dla * (-LRU_C) * r)
        dext[0:TS, :] = dxc
        dcx = w3 * dxc + w2 * dext[pl.ds(1, TS), :] + w1 * dext[pl.ds(2, TS), :] + w0 * dext[pl.ds(3, TS), :]
        dext[TS:TS + 8, :] = dxc[0:8, :]
        dp_ref[:, 0:BR] = dcx.astype(BF16)

    def rev(k):
        return pl.BlockSpec((TS, BR), lambda i: (nt - 1 - i, k))

    def rev_halo(k):
        return pl.BlockSpec((8, BR), lambda i: (jnp.maximum((nt - 1 - i) * (TS // 8) - 1, 0), k))

    full = pl.BlockSpec((BR, BR), lambda i: (0, 0))
    rspec = pl.BlockSpec((8, BR), lambda i: (0, 0))
    return _call(
        body, name="lru_bwd", grid=(nt,),
        in_specs=[rev(8), rev_halo(8), rev(9), rev(0), rev_halo(0), rev(2), rspec, full, full, pl.BlockSpec(memory_space=pl.ANY)],
        out_specs=[pl.BlockSpec((TS, 2 * BR), lambda i: (nt - 1 - i, 4)), rspec, full, full],
        input_output_aliases={9: 0},
        out_shape=[jax.ShapeDtypeStruct(dproj.shape, BF16), jax.ShapeDtypeStruct((8, BR), F32),
                   jax.ShapeDtypeStruct((BR, BR), F32), jax.ShapeDtypeStruct((BR, BR), F32)],
        scratch_shapes=[pltpu.VMEM((TS + 8, BR), F32)] * 4 + [pltpu.VMEM((TS + PAD, BR), F32)] * 2 + [pltpu.VMEM((8, BR), F32)],
        compiler_params=_params("arbitrary"),
    )(proj, proj, proj, h, h, dcat, rows, wa, wx, dproj)


def _s5_bwd(proj, dcat, xre_all, xim_all, a8, cl, bbre, bbim, cre, cim, rows, wg, dproj, BR):
    S = proj.shape[0]
    W = a8.shape[1]
    TS = _tile(S, 256)
    nt = S // TS
    CB, SB, J = S5_CB, S5_SB, S5_LAGS
    nblk = BR // CB

    def body(u_ref, dg_ref, dy_ref, xre_ref, xim_ref, hre_ref, him_ref, a8_ref, cl_ref, bbre_ref, bbim_ref, cre_ref, cim_ref,
             r_ref, wg_ref, _, dp_ref, sum_ref, dwg_ref, da_ref, dbbre_ref, dbbim_ref, dcre_ref, dcim_ref,
             sre, sim, ere, eim, ypre_s, dext, dylag, carry):
        i = pl.program_id(0)
        ti = nt - 1 - i

        @pl.when(i == 0)
        def _():
            dext[TS:TS + 8, :] = jnp.zeros((8, BR), F32)
            carry[...] = jnp.zeros_like(carry)
            for ref in (sum_ref, dwg_ref, da_ref, dbbre_ref, dbbim_ref, dcre_ref, dcim_ref):
                ref[...] = jnp.zeros_like(ref)

        u, dg, dy = u_ref[...], dg_ref[...], dy_ref[...]
        for m in range(nblk):
            cs, ws = slice(m * CB, (m + 1) * CB), slice(m * SB, (m + 1) * SB)
            ypre_s[:, cs] = (_dot(xre_ref[:, ws].astype(BF16), cre_ref[ws, :]) - _dot(xim_ref[:, ws].astype(BF16), cim_ref[ws, :]))
        ypre = ypre_s[...] + r_ref[0:1, :] * u
        yg = _gelu(ypre)
        wg_ = wg_ref[...]
        s = _sigmoid(_dot(yg.astype(BF16), wg_) + r_ref[1:2, :])
        dgl = dy * _silu(dg)
        dp_ref[:, BR:2 * BR] = (dy * yg * s * _dsilu(dg)).astype(BF16)
        dps = dgl * yg * s * (1.0 - s)
        dpsb = dps.astype(BF16)
        sum_ref[1:2, :] += _colsum(dps)
        dwg_ref[...] += _dot_tn(yg.astype(BF16), dpsb)
        dyp = (dgl * s + _dot_nt(dpsb, wg_)) * _dgelu(ypre)
        sum_ref[0:1, :] += _colsum(dyp * u)
        dext[0:TS, :] = dyp
        for m in range(nblk):
            cs, ws = slice(m * CB, (m + 1) * CB), slice(m * SB, (m + 1) * SB)
            dypb = dext[0:TS, cs].astype(BF16)
            xre, xim = xre_ref[:, ws], xim_ref[:, ws]
            dcre_ref[ws, :] += _dot_tn(xre.astype(BF16), dypb)
            dcim_ref[ws, :] -= _dot_tn(xim.astype(BF16), dypb)
            for j in range(J):
                dylag[:, j * CB:(j + 1) * CB] = dext[pl.ds(j, TS), cs].astype(BF16)
            w = _dot(dylag[...], cl_ref[m])
            sre[...] = w[:, 0:SB]
            sim[...] = w[:, SB:2 * SB]
            ar = jnp.broadcast_to(a8_ref[0:1, ws], (8, SB))
            ai = jnp.broadcast_to(a8_ref[1:2, ws], (8, SB))
            gr, gi = carry[0:8, ws], carry[8:16, ws]
            for g in reversed(range(TS // 8)):
                rg = slice(8 * g, 8 * g + 8)
                gr, gi = sre[rg, :] + ar * gr + ai * gi, sim[rg, :] + ar * gi - ai * gr
                sre[rg, :] = gr
                sim[rg, :] = gi
            carry[0:8, ws] = gr
            carry[8:16, ws] = gi
            gre, gim = sre[...], sim[...]
            ere[0:8, :] = jnp.where(ti > 0, hre_ref[:, ws], 0.0)
            eim[0:8, :] = jnp.where(ti > 0, him_ref[:, ws], 0.0)
            ere[8:8 + TS, :] = xre
            eim[8:8 + TS, :] = xim
            xpr, xpi = ere[pl.ds(7, TS), :], eim[pl.ds(7, TS), :]
            da_ref[0:1, ws] += _colsum(gre * xpr + gim * xpi)
            da_ref[1:2, ws] += _colsum(gim * xpr - gre * xpi)
            greb, gimb = gre.astype(BF16), gim.astype(BF16)
            ub = u_ref[:, cs].astype(BF16)
            dbbre_ref[cs, :] += _dot_tn(ub, greb)
            dbbim_ref[cs, :] += _dot_tn(ub, gimb)
            du = _dot_nt(greb, bbre_ref[cs, :]) + _dot_nt(gimb, bbim_ref[cs, :]) + r_ref[0:1, cs] * dext[0:TS, cs]
            dp_ref[:, cs] = du.astype(BF16)
        dext[TS:TS + 8, :] = dext[0:8, :]

    def rev(width, k):
        return pl.BlockSpec((TS, width), lambda i: (nt - 1 - i, k))

    def rev_halo(width):
        return pl.BlockSpec((8, width), lambda i: (jnp.maximum((nt - 1 - i) * (TS // 8) - 1, 0), 0))

    def const(shape):
        return pl.BlockSpec(shape, lambda i: (0,) * len(shape))

    return _call(
        body, name="s5_bwd", grid=(nt,),
        in_specs=[rev(BR, 10), rev(BR, 11), rev(BR, 3), rev(W, 0), rev(W, 0), rev_halo(W), rev_halo(W),
                  const((8, W)), const((nblk, J * CB, 2 * SB)), const((BR, SB)), const((BR, SB)), const((W, CB)), const((W, CB)),
                  const((8, BR)), const((BR, BR)), pl.BlockSpec(memory_space=pl.ANY)],
        input_output_aliases={15: 0},
        out_specs=[pl.BlockSpec((TS, 2 * BR), lambda i: (nt - 1 - i, 5)), const((8, BR)), const((BR, BR)), const((8, W)),
                   const((BR, SB)), const((BR, SB)), const((W, CB)), const((W, CB))],
        out_shape=[jax.ShapeDtypeStruct(dproj.shape, BF16), jax.ShapeDtypeStruct((8, BR), F32), jax.ShapeDtypeStruct((BR, BR), F32),
                   jax.ShapeDtypeStruct((8, W), F32), jax.ShapeDtypeStruct((BR, SB), F32), jax.ShapeDtypeStruct((BR, SB), F32),
                   jax.ShapeDtypeStruct((W, CB), F32), jax.ShapeDtypeStruct((W, CB), F32)],
        scratch_shapes=[pltpu.VMEM((TS, SB), F32)] * 2 + [pltpu.VMEM((TS + 8, SB), F32)] * 2
        + [pltpu.VMEM((TS, BR), F32), pltpu.VMEM((TS + 8, BR), F32), pltpu.VMEM((TS, J * CB), BF16), pltpu.VMEM((16, W), F32)],
        compiler_params=_params("arbitrary"),
    )(proj, proj, dcat, xre_all, xim_all, xre_all, xim_all, a8, cl, bbre, bbim, cre, cim, rows, wg, dproj)


_WEIGHTS = ['rel_bias', 'w_ada', 'b_ada', 'w_in', 'conv_a', 'conv_c', 'conv_c_b', 'lru_wa', 'lru_ba', 'lru_wx', 'lru_bx',
            'lru_lambda', 's5_lam_re', 's5_lam_im', 's5_log_dt', 's5_b_re', 's5_b_im', 's5_c_re', 's5_c_im', 's5_d',
            's5_w_glu', 's5_b_glu', 'w_out', 'ln_g', 'ln_b']
_LAYER_SMALL = ['conv_a', 'conv_c', 'conv_c_b', 'lru_wa', 'lru_ba', 'lru_wx', 'lru_bx', 'lru_lambda', 's5_lam_re', 's5_lam_im',
                's5_log_dt', 's5_b_re', 's5_b_im', 's5_c_re', 's5_c_im', 's5_d', 's5_b_glu']
_SMALL = ['rel_bias'] + _LAYER_SMALL + ['ln_g', 'ln_b']


def _t5_bucket(dist):
    max_exact = REL_BUCKETS // 2
    nf = jnp.maximum(dist, 1).astype(F32)
    large = max_exact + (jnp.log(nf / max_exact) / math.log(REL_MAX_DIST / max_exact)
                         * (REL_BUCKETS - max_exact)).astype(jnp.int32)
    large = jnp.minimum(large, REL_BUCKETS - 1)
    return jnp.where(dist < max_exact, dist, large)


def _bias_tables(rel_bias):
    i = jnp.arange(BLK)[:, None]
    j = jnp.arange(2 * BLK)[None, :]
    delta = i + BLK - j
    out = []
    for window, dil in DILATIONS:
        bucket = _t5_bucket(jnp.clip(delta, 0, window // dil) * dil)
        onehot = (bucket[:, :, None] == jnp.arange(REL_BUCKETS)[None, None, :]).astype(F32)
        out.append(jnp.einsum('ijb,bh->hij', onehot, rel_bias, precision=lax.Precision.HIGHEST))
    return jnp.stack(out)


def _prep_layer(q):
    eye8 = jnp.eye(8, dtype=F32)
    G = q['s5_lam_re'].shape[0]
    nblk = G // 8

    def block_diag(w):
        hd = w.shape[1]
        return (w[:, :, None, :] * eye8[:, None, :, None]).reshape(8 * hd, 8 * hd)

    def compact_b(bb):
        t = jnp.transpose(bb.reshape(nblk, 8, S5_STATE, S5_CH), (0, 1, 3, 2))
        return (t[:, :, :, None, :] * eye8[None, :, None, :, None]).reshape(G * S5_CH, 8 * S5_STATE)

    def compact_c(cc):
        t = jnp.transpose(cc.reshape(nblk, 8, S5_CH, S5_STATE), (0, 1, 3, 2))
        return (t[:, :, :, None, :] * eye8[None, :, None, :, None]).reshape(G * S5_STATE, 8 * S5_CH)

    lam_re, lam_im = q['s5_lam_re'], q['s5_lam_im']
    dt = jnp.exp(q['s5_log_dt'])[:, None]
    mag = jnp.exp(lam_re * dt)
    ab_re = mag * jnp.cos(lam_im * dt)
    ab_im = mag * jnp.sin(lam_im * dt)
    den = lam_re * lam_re + lam_im * lam_im
    f_re = ((ab_re - 1.0) * lam_re + ab_im * lam_im) / den
    f_im = (ab_im * lam_re - (ab_re - 1.0) * lam_im) / den
    b_re, b_im = q['s5_b_re'], q['s5_b_im']
    bb_re = f_re[..., None] * b_re - f_im[..., None] * b_im
    bb_im = f_re[..., None] * b_im + f_im[..., None] * b_re
    return dict(
        conv_a=q['conv_a'], conv_c=q['conv_c'], conv_c_b=q['conv_c_b'], lru_ba=q['lru_ba'], lru_bx=q['lru_bx'],
        sp=jax.nn.softplus(-q['lru_lambda']), wa=block_diag(q['lru_wa']), wx=block_diag(q['lru_wx']),
        ar=ab_re.reshape(-1), ai=ab_im.reshape(-1), bbre=compact_b(bb_re), bbim=compact_b(bb_im),
        cre=compact_c(q['s5_c_re']), cim=compact_c(q['s5_c_im']), s5_d=q['s5_d'], s5_b_glu=q['s5_b_glu'])


def _s5_lag_weights(kp):
    J, CB, SB = S5_LAGS, S5_CB, S5_SB
    ar, ai = kp['ar'], kp['ai']
    W = ar.shape[0]
    nblk = W // SB
    cr, ci = jnp.ones_like(ar), jnp.zeros_like(ai)
    pows = []
    for _ in range(J + 1):
        pows.append((cr, ci))
        cr, ci = cr * ar - ci * ai, cr * ai + ci * ar
    pwr = jnp.transpose(jnp.stack([p[0] for p in pows[:J]]).reshape(J, nblk, SB), (1, 0, 2))[:, :, None, :]
    pwi = jnp.transpose(jnp.stack([p[1] for p in pows[:J]]).reshape(J, nblk, SB), (1, 0, 2))[:, :, None, :]

    def lagged(re, im, sign):
        out = jnp.concatenate([re * pwr - sign * im * pwi, sign * re * pwi + im * pwr], axis=-1)
        return out.reshape(nblk, J * CB, 2 * SB).astype(BF16)

    bbl = lagged(kp['bbre'].reshape(nblk, 1, CB, SB), kp['bbim'].reshape(nblk, 1, CB, SB), 1.0)
    c0r = jnp.transpose(kp['cre'].reshape(nblk, SB, CB), (0, 2, 1))[:, None]
    c0i = -jnp.transpose(kp['cim'].reshape(nblk, SB, CB), (0, 2, 1))[:, None]
    cl = lagged(c0r, c0i, -1.0)
    return _rows8([pows[J][0], pows[J][1]], W), bbl, cl


def _rows8(vecs, width):
    rows = [v.reshape(1, width).astype(F32) for v in vecs]
    return jnp.concatenate(rows + [jnp.zeros((8 - len(rows), width), F32)], axis=0)


def kernel(x, c, rel_bias, w_ada, b_ada, w_in, conv_a, conv_c, conv_c_b, lru_wa, lru_ba, lru_wx, lru_bx, lru_lambda, s5_lam_re, s5_lam_im, s5_log_dt, s5_b_re, s5_b_im, s5_c_re, s5_c_im, s5_d, s5_w_glu, s5_b_glu, w_out, ln_g, ln_b, loss_target, m_rel_bias, m_w_ada, m_b_ada, m_w_in, m_conv_a, m_conv_c, m_conv_c_b, m_lru_wa, m_lru_ba, m_lru_wx, m_lru_bx, m_lru_lambda, m_s5_lam_re, m_s5_lam_im, m_s5_log_dt, m_s5_b_re, m_s5_b_im, m_s5_c_re, m_s5_c_im, m_s5_d, m_s5_w_glu, m_s5_b_glu, m_w_out, m_ln_g, m_ln_b, v_rel_bias, v_w_ada, v_b_ada, v_w_in, v_conv_a, v_conv_c, v_conv_c_b, v_lru_wa, v_lru_ba, v_lru_wx, v_lru_bx, v_lru_lambda, v_s5_lam_re, v_s5_lam_im, v_s5_log_dt, v_s5_b_re, v_s5_b_im, v_s5_c_re, v_s5_c_im, v_s5_d, v_s5_w_glu, v_s5_b_glu, v_w_out, v_ln_g, v_ln_b):
    a = dict(locals())
    me = 4 * lax.axis_index("x") + 2 * lax.axis_index("y") + lax.axis_index("c")
    x0, target = x[0], loss_target[0]
    S, D = x0.shape
    BR = D // 4
    NS = w_in.shape[2]
    SH = BR // NDEV
    assert S % (BLK * DILATIONS[-1][1]) == 0 and BR % (8 * S5_CH) == 0

    small = _all_gather(_pack([c, conv_a, conv_c]), "gather_small")
    per_dev = [_unpack(small[d], [c.shape, conv_a.shape, conv_c.shape]) for d in range(NDEV)]
    c_all = jnp.concatenate([p[0] for p in per_dev], axis=0)
    conv_a_full = jnp.concatenate([p[1] for p in per_dev], axis=2)
    conv_c_full = jnp.concatenate([p[2] for p in per_dev], axis=2)
    me1 = me.reshape(1).astype(jnp.int32)
    w_in_bf, w_out_bf, w_glu_bf = w_in.astype(BF16), w_out.astype(BF16), s5_w_glu.astype(BF16)

    b_cols = lax.dynamic_slice(b_ada, (0, me * NS), (DEPTH, NS)).reshape(DEPTH, 1, NS)
    ada_all = _all_gather(_ada_cols(c_all, w_ada, b_cols), "gather_ada")
    ada_me = lax.dynamic_index_in_dim(ada_all, me, axis=2, keepdims=False)
    ada_me = jnp.transpose(ada_me, (1, 0, 2)).reshape(DEPTH, 3 * D)
    shift, scale, gate = ada_me[:, :D], ada_me[:, D:2 * D], ada_me[:, 2 * D:]

    bias_tabs, bias_pull = jax.vjp(_bias_tables, rel_bias)
    HD = BR // 8
    head_ones = jnp.kron(jnp.eye(8, dtype=F32), jnp.ones((HD, HD), F32))

    saved = []
    xl = x0
    for l in range(DEPTH):
        q = {n: a[n][l] for n in _LAYER_SMALL}
        q['conv_a'], q['conv_c'] = conv_a_full[l], conv_c_full[l]
        kp, pull = jax.vjp(_prep_layer, q)
        a8, bbl, cl = _s5_lag_weights(kp)
        mod = _rows8([1.0 + scale[l], shift[l]], D)
        arows = _rows8(list(kp['conv_a']), BR)
        lrows = _rows8(list(kp['conv_c']) + [kp['conv_c_b'], kp['lru_ba'], kp['lru_bx'], kp['sp']], BR)
        srows = _rows8([kp['s5_d'], kp['s5_b_glu']], BR)
        orows = _rows8([1.0 + gate[l], ln_g[l], ln_b[l]], D)
        wa, wx = kp['wa'].astype(BF16), kp['wx'].astype(BF16)
        s5w = [kp[n].astype(BF16) for n in ('bbre', 'bbim', 'cre', 'cim')]

        hbf = _modulate(xl, mod)
        proj, w_in_l, w_out_l, w_glu_l = _proj_gather(hbf, w_in_bf[l], [w_out_bf[l], w_glu_bf[l]], me1)
        w_out_l, w_glu_l = w_out_l.reshape(D, D), w_glu_l.reshape(BR, BR)
        ya = _conv_a_fwd(proj, arows, BR)
        os_, ls_ = [], []
        for g, (_, dil) in enumerate(DILATIONS):
            o, lse = _attn_fwd(proj, bias_tabs[g], dil, BR)
            os_.append(o)
            ls_.append(lse)
        yb = _attn_mix_fwd(proj, os_, ls_, BR)
        yc, hs = _lru_fwd(proj, lrows, wa, wx, BR)
        yd, xre, xim = _s5_fwd(proj, a8, bbl, s5w[2], s5w[3], srows, w_glu_l, BR)
        if l < DEPTH - 1:
            xn, y, cat = _out_ln([ya, yb, yc, yd], w_out_l, xl, orows)
        else:
            xn, y, cat, loss_local = _out_ln([ya, yb, yc, yd], w_out_l, xl, orows, target)
        saved.append(dict(x=xl, proj=proj, hbf=hbf, os=os_, ls=ls_, hs=hs, xre=xre, xim=xim, y=y, cat=cat, pull=pull,
                          mod=mod, arows=arows, lrows=lrows, srows=srows, orows=orows, wa=wa, wx=wx, s5w=s5w,
                          qw=(a8, cl), w_in=w_in_l, w_out=w_out_l, w_glu=w_glu_l))
        xl = xn

    dout = xl

    dbias = jnp.zeros_like(bias_tabs)
    lgrads, dada, dw_in_parts, dw_out_parts, glu_parts = [None] * DEPTH, [None] * DEPTH, [None] * DEPTH, [None] * DEPTH, [None] * DEPTH
    for l in reversed(range(DEPTH)):
        sv = saved[l]
        proj = sv['proj']
        dxres, dyb, dcat, lnsum = _ln_bwd(dout, sv['x'], sv['y'], sv['orows'], sv['w_out'])
        dw_out_parts[l] = _dw_scatter(sv['cat'], dyb, D // NDEV, True, me1, "dw_out_scatter")
        dproj, asum = _conv_a_bwd(proj, dcat, sv['arows'], BR)
        dos, es, dbg = _attn_mix_bwd(proj, dcat, sv['os'], sv['ls'], head_ones, BR)
        dqkv, dbs = [], []
        for g, (_, dil) in enumerate(DILATIONS):
            dq_, dk_, dv_, db_ = _attn_bwd(proj, dos[g], es[g], sv['ls'][g], bias_tabs[g], dil, BR)
            dqkv.append((dq_, dk_, dv_))
            dbs.append(db_)
        dbias = dbias + jnp.stack(dbs)
        dproj = _attn_dsum(dqkv, dbg, dproj, BR)
        dproj, lsum, dwa, dwx = _lru_bwd(proj, sv['hs'], dcat, sv['lrows'], sv['wa'], sv['wx'], dproj, BR)
        dproj, ssum, dwg, da_, dbbre, dbbim, dcre, dcim = _s5_bwd(
            proj, dcat, sv['xre'], sv['xim'], *sv['qw'], *sv['s5w'], sv['srows'], sv['w_glu'], dproj, BR)
        dkp = dict(conv_a=asum[0:3], conv_c=lsum[0:4], conv_c_b=lsum[4], lru_ba=lsum[5], lru_bx=lsum[6], sp=lsum[7],
                   wa=dwa, wx=dwx, ar=da_[0], ai=da_[1], bbre=dbbre, bbim=dbbim, cre=dcre, cim=dcim,
                   s5_d=ssum[0], s5_b_glu=ssum[1])
        lgrads[l] = dict(sv['pull'](dkp)[0], ln_g=lnsum[0], ln_b=lnsum[1])
        sides = [(dwg.reshape(NDEV, BR // NDEV, BR), False)]
        if l == 0:
            local = {'rel_bias': bias_pull(dbias)[0], 'loss': loss_local.reshape(1)}
            for n in _LAYER_SMALL + ['ln_g', 'ln_b']:
                local[n] = jnp.stack([lgrads[k][n] for k in range(DEPTH)])
            packed = _SMALL + ['loss']
            sides.append((_pack([local[n] for n in packed]), True))
        dout, msum, glu_parts[l], *gathered = _dh(dproj, sv['w_in'], dxres, sv['x'], sv['mod'], sides)
        dw_in_parts[l] = _dw_scatter(sv['hbf'], dproj, NS, False, me1, "dw_in_scatter")
        dada[l] = jnp.concatenate([msum[1], msum[0], lnsum[2]])
    grad_x = dout[None]

    out = {}

    def put(name, res):
        out[name] = res

    put('w_in', _adamw(dw_in_parts, w_in, m_w_in, v_w_in, "adamw_w_in"))
    put('w_out', _adamw(dw_out_parts, w_out, m_w_out, v_w_out, "adamw_w_out"))
    put('s5_w_glu', _adamw(glu_parts, s5_w_glu, m_s5_w_glu, v_s5_w_glu, "adamw_w_glu"))

    dada_all = _all_gather(jnp.stack(dada), "gather_dada")
    dada_cols = jnp.transpose(lax.dynamic_slice(dada_all, (0, 0, me * NS), (NDEV, DEPTH, NS)), (1, 0, 2))
    gw_ada, gb_ada = _ada_grads(c_all, dada_all, dada_cols)
    put('w_ada', _adamw([gw_ada[l] for l in range(DEPTH)], w_ada, m_w_ada, v_w_ada, "adamw_w_ada"))

    shapes = [local[n].shape for n in packed]
    gs = dict(zip(packed, _unpack(_sum8(gathered[0], "sum_small_grads"), shapes)))
    loss = gs['loss'][0]
    gs['conv_a'] = lax.dynamic_slice_in_dim(gs['conv_a'], me * SH, SH, axis=2)
    gs['conv_c'] = lax.dynamic_slice_in_dim(gs['conv_c'], me * SH, SH, axis=2)
    gs['b_ada'] = gb_ada
    names = _SMALL + ['b_ada']
    shapes = [a[n].shape for n in names]
    res = _adamw([_pack([gs[n] for n in names])], _pack([a[n] for n in names])[None], _pack([a['m_' + n] for n in names])[None],
                 _pack([a['v_' + n] for n in names])[None], "adamw_small")
    unpacked = [_unpack(r[0], shapes) for r in res]
    for k, n in enumerate(names):
        put(n, tuple(unpacked[j][k] for j in range(4)))

    return (loss, grad_x, *[out[n][0] for n in _WEIGHTS], *[out[n][1] for n in _WEIGHTS],
            *[out[n][2] for n in _WEIGHTS], *[out[n][3] for n in _WEIGHTS])
```

```python
import math

import jax
import jax.numpy as jnp
from jax import lax
from jax.experimental import pallas as pl
from jax.experimental.pallas import tpu as pltpu

F32 = jnp.float32
BF16 = jnp.bfloat16

NDEV = 8
DEPTH = 2
BLK = 128
DILATIONS = ((128, 1), (512, 4), (2048, 16))
REL_BUCKETS = 32
REL_MAX_DIST = 2048
LRU_C = 8.0
S5_CH = 16
S5_STATE = 64
ALPHA = (2 * DEPTH) ** 0.25
LN_EPS = 1e-5
ADAM_LR, ADAM_B1, ADAM_B2, ADAM_EPS, ADAM_WD, ADAM_STEP = 0.001, 0.9, 0.999, 1e-08, 0.01, 10
NEG = -1e30
VMEM_LIMIT = 56 * 1024 * 1024


def _call(body, **kw):
    return pl.pallas_call(body, **kw)


def _params(*sem):
    return pltpu.CompilerParams(dimension_semantics=sem, vmem_limit_bytes=VMEM_LIMIT)


def _sigmoid(x):
    return 1.0 / (1.0 + jnp.exp(-x))


def _silu(x):
    return x * _sigmoid(x)


def _dsilu(x):
    s = _sigmoid(x)
    return s * (1.0 + x * (1.0 - s))


_GC = math.sqrt(2.0 / math.pi)


def _gelu(x):
    return 0.5 * x * (1.0 + jnp.tanh(_GC * (x + 0.044715 * x * x * x)))


def _dgelu(x):
    t = jnp.tanh(_GC * (x + 0.044715 * x * x * x))
    return 0.5 * (1.0 + t) + 0.5 * x * (1.0 - t * t) * _GC * (1.0 + 3.0 * 0.044715 * x * x)


def _expm1(x):
    series = x * (1.0 + x * (1.0 / 2) * (1.0 + x * (1.0 / 3) * (1.0 + x * (1.0 / 4) * (1.0 + x * (1.0 / 5) * (1.0 + x * (1.0 / 6))))))
    return jnp.where(jnp.abs(x) < 0.25, series, jnp.exp(x) - 1.0)


def _dot(a, b):
    return jnp.dot(a, b, preferred_element_type=F32)


def _dot_nt(a, b):
    return lax.dot_general(a, b, (((1,), (1,)), ((), ())), preferred_element_type=F32)


def _dot_tn(a, b):
    return lax.dot_general(a, b, (((0,), (0,)), ((), ())), preferred_element_type=F32)


def _colsum(v):
    return jnp.sum(v, axis=0, keepdims=True)


def _all_gather(x, name):
    def body(x_ref, o_ref, send_sems, recv_sems, local_sem):
        ix, iy, ic = lax.axis_index("x"), lax.axis_index("y"), lax.axis_index("c")
        me, sibling = (ix, iy, ic), (ix, iy, 1 - ic)
        chips = [(1 - ix, iy), (ix, 1 - iy), (1 - ix, 1 - iy)]

        def slot(px, py, pc):
            return o_ref.at[4 * px + 2 * py + pc]

        def copy(k, block, to, src=None):
            return pltpu.make_async_remote_copy(
                src_ref=slot(*block) if src is None else src, dst_ref=slot(*block), send_sem=send_sems.at[k],
                recv_sem=recv_sems.at[k], device_id=to, device_id_type=pl.DeviceIdType.MESH)

        mine = pltpu.make_async_copy(x_ref, slot(*me), local_sem)
        mine.start()
        first = [copy(0, me, sibling, src=x_ref)] + [copy(1 + j, me, (*chip, ic), src=x_ref) for j, chip in enumerate(chips)]
        for cp in first:
            cp.start()
        passed = [copy(4 + j, (*chip, ic), sibling) for j, chip in enumerate(chips)]
        for j, chip in enumerate(chips):
            copy(1 + j, (*chip, ic), me).wait_recv()
            passed[j].start()
        copy(0, sibling, me).wait_recv()
        for j, chip in enumerate(chips):
            copy(4 + j, (*chip, 1 - ic), me).wait_recv()
        for cp in first + passed:
            cp.wait_send()
        mine.wait()

    return _call(
        body, name=name,
        out_shape=jax.ShapeDtypeStruct((NDEV,) + x.shape, x.dtype),
        in_specs=[pl.BlockSpec(memory_space=pl.ANY)],
        out_specs=pl.BlockSpec(memory_space=pl.ANY),
        scratch_shapes=[pltpu.SemaphoreType.DMA((NDEV - 1,)), pltpu.SemaphoreType.DMA((NDEV - 1,)), pltpu.SemaphoreType.DMA],
    )(x)


def _pack(arrs):
    parts = []
    for a in arrs:
        n = math.prod(a.shape)
        parts.append(jnp.pad(a.reshape(-1).astype(F32), (0, _pack_rows(n) * 128 - n)).reshape(-1, 128))
    rows = sum(p.shape[0] for p in parts)
    total = rows if rows <= 512 else -(-rows // 512) * 512
    if total > rows:
        parts.append(jnp.zeros((total - rows, 128), F32))
    return jnp.concatenate(parts, axis=0)


def _pack_rows(n):
    return -(-n // 1024) * 8


def _unpack(packed, shapes):
    out, row = [], 0
    for s in shapes:
        n = math.prod(s)
        rows = _pack_rows(n)
        out.append(packed[row:row + rows].reshape(-1)[:n].reshape(s))
        row += rows
    return out


def _sum8(parts, name):
    _, R, C = parts.shape
    TR = R
    for cand in (512, 256, 128, 64, 32, 16, 8):
        if R % cand == 0:
            TR = cand
            break

    def body(p_ref, o_ref):
        acc = p_ref[0]
        for d in range(1, NDEV):
            acc = acc + p_ref[d]
        o_ref[...] = acc

    return _call(
        body, name=name, grid=(R // TR,),
        in_specs=[pl.BlockSpec((NDEV, TR, C), lambda i: (0, i, 0))],
        out_specs=pl.BlockSpec((TR, C), lambda i: (i, 0)),
        out_shape=jax.ShapeDtypeStruct((R, C), F32),
        compiler_params=_params("arbitrary"),
    )(parts)


def _adamw(g_layers, w, m, v, name):
    L, R, C = w.shape
    g0 = g_layers[0]
    nparts = g0.shape[0] if g0.ndim == 3 else 0
    TR = R
    for cand in (256, 128, 64, 32, 16):
        if R % cand == 0 and cand * C <= 128 * 1024:
            TR = cand
            break

    def body(*refs):
        g_refs = refs[:L]
        w_ref, m_ref, v_ref, go_ref, d_ref, mo_ref, vo_ref = refs[L:]
        for ll in range(L):
            @pl.when(pl.program_id(0) == ll)
            def _(g_ref=g_refs[ll]):
                if nparts:
                    g = g_ref[0].astype(F32)
                    for d in range(1, nparts):
                        g = g + g_ref[d].astype(F32)
                else:
                    g = g_ref[...]
                m2 = ADAM_B1 * m_ref[0] + (1.0 - ADAM_B1) * g
                v2 = ADAM_B2 * v_ref[0] + (1.0 - ADAM_B2) * (g * g)
                m_hat = m2 / (1.0 - ADAM_B1 ** ADAM_STEP)
                v_hat = v2 / (1.0 - ADAM_B2 ** ADAM_STEP)
                go_ref[0] = g
                d_ref[0] = -ADAM_LR * (m_hat / (jnp.sqrt(v_hat) + ADAM_EPS) + ADAM_WD * w_ref[0])
                mo_ref[0] = m2
                vo_ref[0] = v2

    def gspec(ll):
        if nparts:
            return pl.BlockSpec((nparts, TR, C), lambda l, i: (0, jnp.where(l == ll, i, 0), 0))
        return pl.BlockSpec((TR, C), lambda l, i: (jnp.where(l == ll, i, 0), 0))

    spec = pl.BlockSpec((1, TR, C), lambda l, i: (l, i, 0))
    return _call(
        body, name=name, grid=(L, R // TR),
        in_specs=[gspec(ll) for ll in range(L)] + [spec, spec, spec], out_specs=[spec] * 4,
        out_shape=[jax.ShapeDtypeStruct((L, R, C), F32)] * 4,
        compiler_params=_params("arbitrary", "arbitrary"),
    )(*g_layers, w, m, v)


def _ada_cols(c_all, w_ada, b_cols):
    L, D, NS = w_ada.shape

    def body(c_ref, w_ref, b_ref, o_ref):
        cond = _silu(c_ref[...]).astype(BF16)
        o_ref[0] = _dot(cond, w_ref[0].astype(BF16)) + b_ref[0]

    return _call(
        body, name="ada_cols", grid=(L,),
        in_specs=[pl.BlockSpec((NDEV, D), lambda l: (0, 0)), pl.BlockSpec((1, D, NS), lambda l: (l, 0, 0)),
                  pl.BlockSpec((1, 1, NS), lambda l: (l, 0, 0))],
        out_specs=pl.BlockSpec((1, NDEV, NS), lambda l: (l, 0, 0)),
        out_shape=jax.ShapeDtypeStruct((L, NDEV, NS), F32),
        compiler_params=_params("arbitrary"),
    )(c_all, w_ada, b_cols)


def _ada_grads(c_all, dada_all, dada_cols):
    _, D = c_all.shape
    L, _, NS = dada_cols.shape
    D3 = dada_all.shape[2]

    def body(c_ref, da_ref, dc_ref, gw_ref, gb_ref):
        cond = _silu(c_ref[...]).astype(BF16)
        gw_ref[0] = _dot_tn(cond, dc_ref[0].astype(BF16))
        acc = da_ref[0]
        for d in range(1, NDEV):
            acc = acc + da_ref[d]
        gb_ref[...] = acc

    return _call(
        body, name="ada_grads", grid=(L,),
        in_specs=[pl.BlockSpec((NDEV, D), lambda l: (0, 0)), pl.BlockSpec((NDEV, L, D3), lambda l: (0, 0, 0)),
                  pl.BlockSpec((1, NDEV, NS), lambda l: (l, 0, 0))],
        out_specs=[pl.BlockSpec((1, D, NS), lambda l: (l, 0, 0)), pl.BlockSpec((L, D3), lambda l: (0, 0))],
        out_shape=[jax.ShapeDtypeStruct((L, D, NS), F32), jax.ShapeDtypeStruct((L, D3), F32)],
        compiler_params=_params("arbitrary"),
    )(c_all, dada_all, dada_cols)


def _tile(S, want):
    return want if S % want == 0 else S


def _modulate(x, mod):
    S, D = x.shape
    TS = _tile(S, 512)

    def body(x_ref, mod_ref, h_ref):
        h_ref[...] = (x_ref[...] * mod_ref[0:1, :] + mod_ref[1:2, :]).astype(BF16)

    spec = pl.BlockSpec((TS, D), lambda i: (i, 0))
    return _call(
        body, name="modulate", grid=(S // TS,),
        in_specs=[spec, pl.BlockSpec((8, D), lambda i: (0, 0))], out_specs=spec,
        out_shape=jax.ShapeDtypeStruct((S, D), BF16),
        compiler_params=_params("arbitrary"),
    )(x, mod)


def _peer(k):
    ix, iy, ic = lax.axis_index("x"), lax.axis_index("y"), lax.axis_index("c")
    bx, by, bc = (k >> 2) & 1, (k >> 1) & 1, k & 1
    px, py, pc = ix + bx - 2 * ix * bx, iy + by - 2 * iy * by, ic + bc - 2 * ic * bc
    return (px, py, pc), 4 * px + 2 * py + pc


GATHER_ORDER = (0, 1, 2, 4, 6, 3, 5, 7)
GATHER_DIRECT = (1, 2, 4, 6)
SCATTER_ORDER = (3, 2, 5, 4, 7, 6, 1, 0)


def _offset(order, k):
    off = jnp.int32(order[-1])
    for step in reversed(range(len(order) - 1)):
        off = jnp.where(k == step, jnp.int32(order[step]), off)
    return off


def _proj_gather(h, w_shard, extras, me1):
    assert GATHER_ORDER[0] == 0
    S, D = h.shape
    NS = w_shard.shape[1]
    TS = _tile(S, 1024)
    nt = S // TS
    ne = len(extras)

    def body(me_ref, h_ref, w_ref, *rest):
        x_refs, o_ref, wg_ref = rest[:ne], rest[ne], rest[ne + 1]
        xg_refs = rest[ne + 2:2 * ne + 2]
        wbuf, send_sems, recv_sems, local_sems, load_sems = rest[2 * ne + 2:]
        k, i = pl.program_id(0), pl.program_id(1)
        me = me_ref[0]
        off = _offset(GATHER_ORDER, k)

        def push(src, dst, which, kk):
            peer, _ = _peer(kk)
            return pltpu.make_async_remote_copy(
                src_ref=src, dst_ref=dst.at[me], send_sem=send_sems.at[which, kk], recv_sem=recv_sems.at[which, kk],
                device_id=peer, device_id_type=pl.DeviceIdType.MESH)

        def own(src, dst, which):
            return pltpu.make_async_copy(src, dst.at[me], local_sems.at[which])

        pairs = [(w_ref, wg_ref)] + list(zip(x_refs, xg_refs))

        @pl.when(jnp.logical_and(k == 0, i == 0))
        def _():
            for which, (src, dst) in enumerate(pairs):
                own(src, dst, which).start()
                for kk in (GATHER_DIRECT if which == 0 else GATHER_ORDER[1:]):
                    push(src, dst, which, kk).start()
            first = pltpu.make_async_copy(w_ref, wbuf.at[0], load_sems.at[0])
            first.start()
            first.wait()

        @pl.when(jnp.logical_and(k > 0, i == 0))
        def _():
            push(w_ref, wg_ref, 0, off).wait_recv()
            _, blk = _peer(off)
            for step, o in enumerate(GATHER_ORDER):
                if o in GATHER_DIRECT and o > 1:
                    @pl.when(k == step)
                    def _(o=o):
                        sibling, _ = _peer(1)
                        pltpu.make_async_remote_copy(
                            src_ref=wg_ref.at[blk], dst_ref=wg_ref.at[blk], send_sem=send_sems.at[0, o + 1],
                            recv_sem=recv_sems.at[0, o + 1], device_id=sibling, device_id_type=pl.DeviceIdType.MESH).start()
            load = pltpu.make_async_copy(wg_ref.at[blk], wbuf.at[k % 2], load_sems.at[k % 2])
            load.start()
            load.wait()

        o_ref[...] = _dot(h_ref[...], wbuf[k % 2])

        @pl.when(jnp.logical_and(k == NDEV - 1, i == nt - 1))
        def _():
            for which, (src, dst) in enumerate(pairs):
                for kk in range(1, NDEV):
                    cp = push(src, dst, which, kk)
                    cp.wait_send()
                    if which > 0:
                        cp.wait_recv()
                own(src, dst, which).wait()

    def col(k, i, me_ref):
        m, off = me_ref[0], _offset(GATHER_ORDER, k)
        return i, (m | off) - (m & off)

    anyspec = pl.BlockSpec(memory_space=pl.ANY)
    grid_spec = pltpu.PrefetchScalarGridSpec(
        num_scalar_prefetch=1, grid=(NDEV, nt),
        in_specs=[pl.BlockSpec((TS, D), lambda k, i, me_ref: (i, 0)), anyspec] + [anyspec] * ne,
        out_specs=[pl.BlockSpec((TS, NS), col), anyspec] + [anyspec] * ne,
        scratch_shapes=[pltpu.VMEM((2, D, NS), BF16), pltpu.SemaphoreType.DMA((1 + ne, NDEV)), pltpu.SemaphoreType.DMA((1 + ne, NDEV)),
                        pltpu.SemaphoreType.DMA((1 + ne,)), pltpu.SemaphoreType.DMA((2,))])
    return _call(
        body, name="proj_gather", grid_spec=grid_spec,
        out_shape=[jax.ShapeDtypeStruct((S, NDEV * NS), F32), jax.ShapeDtypeStruct((NDEV, D, NS), BF16)]
        + [jax.ShapeDtypeStruct((NDEV,) + e.shape, e.dtype) for e in extras],
        compiler_params=_params("arbitrary", "arbitrary"),
    )(me1, h, w_shard, *extras)


def _dw_scatter(a, b, blk, by_rows, me1, name):
    S = a.shape[0]
    TK = _tile(S, 1024)
    nk = S // TK
    shape = (blk, b.shape[1]) if by_rows else (a.shape[1], blk)
    last = NDEV - 1
    assert all(o & 1 for o in SCATTER_ORDER[0::2]) and not any(o & 1 for o in SCATTER_ORDER[1::2]) and SCATTER_ORDER[-1] == 0
    assert all(SCATTER_ORDER[s] == SCATTER_ORDER[s + 1] + 1 for s in range(0, NDEV, 2))

    def body(me_ref, a_ref, b_ref, recv_ref, acc, stage, sib, send_sems, recv_sems, local_sem):
        k, kk = pl.program_id(0), pl.program_id(1)

        def chip_of(step):
            return _offset(SCATTER_ORDER, step) >> 1

        def to_sibling(j, slot):
            sibling, _ = _peer(1)
            return pltpu.make_async_remote_copy(
                src_ref=stage.at[slot], dst_ref=sib.at[j], send_sem=send_sems.at[0, j], recv_sem=recv_sems.at[0, j],
                device_id=sibling, device_id_type=pl.DeviceIdType.MESH)

        def to_chip(j, slot):
            peer, _ = _peer(2 * j)
            return pltpu.make_async_remote_copy(
                src_ref=stage.at[slot], dst_ref=recv_ref.at[j], send_sem=send_sems.at[1, j], recv_sem=recv_sems.at[1, j],
                device_id=peer, device_id_type=pl.DeviceIdType.MESH)

        own = pltpu.make_async_copy(stage.at[last % 2], recv_ref.at[0], local_sem)

        @pl.when(kk == 0)
        def _():
            acc[...] = jnp.zeros_like(acc)
        acc[...] += _dot_tn(a_ref[...], b_ref[...])

        @pl.when(kk == nk - 1)
        def _():
            j = chip_of(k)
            even_step = k % 2 == 0

            @pl.when(jnp.logical_and(k >= 2, even_step))
            def _():
                to_sibling(chip_of(k - 2), k % 2).wait_send()

            @pl.when(jnp.logical_and(k >= 3, jnp.logical_not(even_step)))
            def _():
                to_chip(chip_of(k - 2), k % 2).wait_send()

            @pl.when(even_step)
            def _():
                stage[k % 2] = acc[...].astype(BF16)
                to_sibling(j, k % 2).start()

            @pl.when(jnp.logical_not(even_step))
            def _():
                to_sibling(j, 0).wait_recv()
                stage[k % 2] = (acc[...] + sib[j].astype(F32)).astype(BF16)

                @pl.when(k < last)
                def _():
                    to_chip(j, k % 2).start()

                @pl.when(k == last)
                def _():
                    own.start()
                    to_sibling(SCATTER_ORDER[last - 1] >> 1, (last - 1) % 2).wait_send()
                    own.wait()
                    for jj in range(1, NDEV // 2):
                        to_chip(jj, 0).wait_recv()

    def blk_of(k, me_ref):
        m, off = me_ref[0], _offset(SCATTER_ORDER, k)
        return (m | off) - (m & off)

    if by_rows:
        in_specs = [pl.BlockSpec((TK, blk), lambda k, kk, me_ref: (kk, blk_of(k, me_ref))),
                    pl.BlockSpec((TK, b.shape[1]), lambda k, kk, me_ref: (kk, 0))]
    else:
        in_specs = [pl.BlockSpec((TK, a.shape[1]), lambda k, kk, me_ref: (kk, 0)),
                    pl.BlockSpec((TK, blk), lambda k, kk, me_ref: (kk, blk_of(k, me_ref)))]
    grid_spec = pltpu.PrefetchScalarGridSpec(
        num_scalar_prefetch=1, grid=(NDEV, nk), in_specs=in_specs,
        out_specs=pl.BlockSpec(memory_space=pl.ANY),
        scratch_shapes=[pltpu.VMEM(shape, F32), pltpu.VMEM((2,) + shape, BF16), pltpu.VMEM((NDEV // 2,) + shape, BF16),
                        pltpu.SemaphoreType.DMA((2, NDEV // 2)), pltpu.SemaphoreType.DMA((2, NDEV // 2)), pltpu.SemaphoreType.DMA])
    return _call(
        body, name=name, grid_spec=grid_spec,
        out_shape=jax.ShapeDtypeStruct((NDEV // 2,) + shape, BF16),
        compiler_params=_params("arbitrary", "arbitrary"),
    )(me1, a, b)


def _chunk(TS, BR, k):
    return pl.BlockSpec((TS, BR), lambda i: (i, k))


def _halo_prev(TS, BR, k):
    return pl.BlockSpec((8, BR), lambda i: (jnp.maximum(i * (TS // 8) - 1, 0), k))


def _conv_a_fwd(proj, w8, BR):
    S = proj.shape[0]
    TS = _tile(S, 512)

    def body(ab_ref, ac_ref, ax_ref, ag_ref, hc_ref, hx_ref, w_ref, o_ref, ext):
        i = pl.program_id(0)
        u = ac_ref[...] * ax_ref[...]
        ext[0:8, :] = jnp.where(i > 0, hc_ref[...] * hx_ref[...], 0.0)
        ext[8:8 + TS, :] = u
        conv = ext[pl.ds(6, TS), :] * w_ref[0:1, :] + ext[pl.ds(7, TS), :] * w_ref[1:2, :] + u * w_ref[2:3, :]
        o_ref[...] = ab_ref[...] * conv * _silu(ag_ref[...])

    return _call(
        body, name="conv_a_fwd", grid=(S // TS,),
        in_specs=[_chunk(TS, BR, 0), _chunk(TS, BR, 1), _chunk(TS, BR, 2), _chunk(TS, BR, 3),
                  _halo_prev(TS, BR, 1), _halo_prev(TS, BR, 2), pl.BlockSpec((8, BR), lambda i: (0, 0))],
        out_specs=pl.BlockSpec((TS, BR), lambda i: (i, 0)),
        out_shape=jax.ShapeDtypeStruct((S, BR), F32),
        scratch_shapes=[pltpu.VMEM((TS + 8, BR), F32)],
        compiler_params=_params("arbitrary"),
    )(proj, proj, proj, proj, proj, proj, w8)


ATT_UNIT = 2048
ATT_LANES = 128


def _attn_mask():
    i = lax.broadcasted_iota(jnp.int32, (BLK, 2 * BLK), 0)
    j = lax.broadcasted_iota(jnp.int32, (BLK, 2 * BLK), 1)
    return jnp.logical_and(j >= i, j <= i + BLK), j >= BLK


def _attn_rows(n, r, dil):
    return pl.ds(n * BLK * dil + r, BLK, stride=dil) if dil > 1 else pl.ds(n * BLK, BLK)


def _attn_geometry(S, BR, dil):
    HD = BR // 8
    U = ATT_UNIT
    assert S % U == 0 and BR % ATT_LANES == 0 and ATT_LANES % HD == 0 and U % (BLK * dil) == 0
    return HD, U, S // U, U // (BLK * dil), ATT_LANES // HD, BR // ATT_LANES


def _attn_fwd(proj, bias, dil, BR):
    S = proj.shape[0]
    HD, U, NU, nbu, hpb, HBK = _attn_geometry(S, BR, dil)
    scale = HD ** -0.5
    combos = [(n, r) for n in range(nbu) for r in range(dil)]

    def body(q_ref, kc_ref, kp_ref, vc_ref, vp_ref, b_ref, o_ref, l_ref, sbuf, pbuf):
        m = pl.program_id(1)
        band, is_cur = _attn_mask()
        first = jnp.logical_and(band, jnp.logical_or(is_cur, m > 0))

        def keys(ref_c, ref_p, n, r):
            prev = ref_c[_attn_rows(n - 1, r, dil), :] if n > 0 else ref_p[_attn_rows(nbu - 1, r, dil), :]
            return jnp.concatenate([prev, ref_c[_attn_rows(n, r, dil), :]], axis=0).astype(BF16)

        for c, (n, r) in enumerate(combos):
            q = (q_ref[_attn_rows(n, r, dil), :] * scale).astype(BF16)
            kk = keys(kc_ref, kp_ref, n, r)
            for h in range(hpb):
                sl = slice(h * HD, (h + 1) * HD)
                sbuf[c * hpb + h] = _dot_nt(q[:, sl], kk[:, sl])
        for c, (n, r) in enumerate(combos):
            lses = []
            for h in range(hpb):
                s = jnp.where(band if n > 0 else first, sbuf[c * hpb + h] + b_ref[h], NEG)
                mx = jnp.max(s, axis=-1, keepdims=True)
                p = jnp.exp(s - mx)
                l = jnp.sum(p, axis=-1, keepdims=True)
                pbuf[c * hpb + h] = (p * (1.0 / l)).astype(BF16)
                lses.append(jnp.broadcast_to(mx + jnp.log(l), (BLK, HD)))
            l_ref[_attn_rows(n, r, dil), :] = jnp.concatenate(lses, axis=1)
        for c, (n, r) in enumerate(combos):
            vv = keys(vc_ref, vp_ref, n, r)
            o_ref[_attn_rows(n, r, dil), :] = jnp.concatenate(
                [_dot(pbuf[c * hpb + h], vv[:, h * HD:(h + 1) * HD]) for h in range(hpb)], axis=1)

    def cur(c):
        return pl.BlockSpec((U, ATT_LANES), lambda hb, m: (m, c * HBK + hb))

    def prev(c):
        return pl.BlockSpec((U, ATT_LANES), lambda hb, m: (jnp.maximum(m - 1, 0), c * HBK + hb))

    ospec = pl.BlockSpec((U, ATT_LANES), lambda hb, m: (m, hb))
    nhb = len(combos) * hpb
    return _call(
        body, name="attn_fwd_d%d" % dil, grid=(HBK, NU),
        in_specs=[cur(4), cur(5), prev(5), cur(6), prev(6), pl.BlockSpec((hpb, BLK, 2 * BLK), lambda hb, m: (hb, 0, 0))],
        out_specs=[ospec, ospec],
        out_shape=[jax.ShapeDtypeStruct((S, BR), F32)] * 2,
        scratch_shapes=[pltpu.VMEM((nhb, BLK, 2 * BLK), F32), pltpu.VMEM((nhb, BLK, 2 * BLK), BF16)],
        compiler_params=_params("arbitrary", "arbitrary"),
    )(proj, proj, proj, proj, proj, bias)


def _mix_weights(l1, l2, l3):
    m = jnp.maximum(jnp.maximum(l1, l2), l3)
    e1, e2, e3 = jnp.exp(l1 - m), jnp.exp(l2 - m), jnp.exp(l3 - m)
    inv = 1.0 / (e1 + e2 + e3)
    return e1 * inv, e2 * inv, e3 * inv


def _attn_mix_fwd(proj, os_, ls_, BR):
    S = proj.shape[0]
    TS = _tile(S, 512)

    def body(o1, o2, o3, l1, l2, l3, g_ref, y_ref):
        w1, w2, w3 = _mix_weights(l1[...], l2[...], l3[...])
        y_ref[...] = (w1 * o1[...] + w2 * o2[...] + w3 * o3[...]) * _silu(g_ref[...])

    spec = pl.BlockSpec((TS, BR), lambda i: (i, 0))
    return _call(
        body, name="attn_mix_fwd", grid=(S // TS,),
        in_specs=[spec] * 6 + [_chunk(TS, BR, 7)], out_specs=spec,
        out_shape=jax.ShapeDtypeStruct((S, BR), F32),
        compiler_params=_params("arbitrary"),
    )(*os_, *ls_, proj)


def _lru_gates(xc, wa, wx, ba, bx, sp):
    xb = xc.astype(BF16)
    r = _sigmoid(_dot(xb, wa) + ba)
    ig = _sigmoid(_dot(xb, wx) + bx)
    la = -LRU_C * r * sp
    a = jnp.exp(la)
    mult = jnp.sqrt(-_expm1(2.0 * la))
    return r, ig, a, mult


def _lru_fwd(proj, rows, wa, wx, BR):
    S = proj.shape[0]
    TS = _tile(S, 256)
    PAD = TS // 2

    def body(cx_ref, hx_ref, cg_ref, r_ref, wa_ref, wx_ref, y_ref, h_ref, ext, sa, sb, carry):
        i = pl.program_id(0)

        @pl.when(i == 0)
        def _():
            sa[0:PAD, :] = jnp.ones((PAD, BR), F32)
            sb[0:PAD, :] = jnp.zeros((PAD, BR), F32)
            carry[...] = jnp.zeros_like(carry)

        cx = cx_ref[...]
        ext[0:8, :] = jnp.where(i > 0, hx_ref[...], 0.0)
        ext[8:8 + TS, :] = cx
        xc = (ext[pl.ds(5, TS), :] * r_ref[0:1, :] + ext[pl.ds(6, TS), :] * r_ref[1:2, :]
              + ext[pl.ds(7, TS), :] * r_ref[2:3, :] + cx * r_ref[3:4, :] + r_ref[4:5, :])
        _, ig, a, mult = _lru_gates(xc, wa_ref[...], wx_ref[...], r_ref[5:6, :], r_ref[6:7, :], r_ref[7:8, :])
        sa[PAD:PAD + TS, :] = a
        sb[PAD:PAD + TS, :] = mult * ig * xc
        d = 1
        while d < TS:
            A, B = sa[PAD:PAD + TS, :], sb[PAD:PAD + TS, :]
            As, Bs = sa[pl.ds(PAD - d, TS), :], sb[pl.ds(PAD - d, TS), :]
            sb[PAD:PAD + TS, :] = A * Bs + B
            sa[PAD:PAD + TS, :] = A * As
            d *= 2
        h = sb[PAD:PAD + TS, :] + sa[PAD:PAD + TS, :] * carry[0:1, :]
        carry[0:1, :] = h[TS - 1:TS, :]
        h_ref[...] = h
        y_ref[...] = h * _silu(cg_ref[...])

    full = pl.BlockSpec((BR, BR), lambda i: (0, 0))
    spec = pl.BlockSpec((TS, BR), lambda i: (i, 0))
    return _call(
        body, name="lru_fwd", grid=(S // TS,),
        in_specs=[_chunk(TS, BR, 8), _halo_prev(TS, BR, 8), _chunk(TS, BR, 9), pl.BlockSpec((8, BR), lambda i: (0, 0)), full, full],
        out_specs=[spec, spec],
        out_shape=[jax.ShapeDtypeStruct((S, BR), F32)] * 2,
        scratch_shapes=[pltpu.VMEM((TS + 8, BR), F32), pltpu.VMEM((PAD + TS, BR), F32), pltpu.VMEM((PAD + TS, BR), F32),
                        pltpu.VMEM((8, BR), F32)],
        compiler_params=_params("arbitrary"),
    )(proj, proj, proj, rows, wa, wx)


S5_LAGS = 8
S5_CB = 8 * S5_CH
S5_SB = 8 * S5_STATE


def _s5_fwd(proj, a8, bbl, cre, cim, rows, wg, BR):
    S = proj.shape[0]
    W = a8.shape[1]
    TS = _tile(S, 256)
    CB, SB, J = S5_CB, S5_SB, S5_LAGS
    nblk = BR // CB

    def body(u_ref, uh_ref, dg_ref, a8_ref, bbl_ref, cre_ref, cim_ref, r_ref, wg_ref,
             y_ref, xre_ref, xim_ref, uext, ulag, sre, sim, ypre_s, carry):
        i = pl.program_id(0)

        @pl.when(i == 0)
        def _():
            carry[...] = jnp.zeros_like(carry)

        uext[0:8, :] = jnp.where(i > 0, uh_ref[...], 0.0)
        uext[8:8 + TS, :] = u_ref[...]
        for m in range(nblk):
            cs, ws = slice(m * CB, (m + 1) * CB), slice(m * SB, (m + 1) * SB)
            for j in range(J):
                ulag[:, j * CB:(j + 1) * CB] = uext[pl.ds(8 - j, TS), cs].astype(BF16)
            w = _dot(ulag[...], bbl_ref[m])
            sre[...] = w[:, 0:SB]
            sim[...] = w[:, SB:2 * SB]
            ar = jnp.broadcast_to(a8_ref[0:1, ws], (8, SB))
            ai = jnp.broadcast_to(a8_ref[1:2, ws], (8, SB))
            xr, xi = carry[0:8, ws], carry[8:16, ws]
            for g in range(TS // 8):
                rg = slice(8 * g, 8 * g + 8)
                xr, xi = sre[rg, :] + ar * xr - ai * xi, sim[rg, :] + ar * xi + ai * xr
                sre[rg, :] = xr
                sim[rg, :] = xi
            carry[0:8, ws] = xr
            carry[8:16, ws] = xi
            xre, xim = sre[...], sim[...]
            xre_ref[:, ws] = xre
            xim_ref[:, ws] = xim
            ypre_s[:, cs] = _dot(xre.astype(BF16), cre_ref[ws, :]) - _dot(xim.astype(BF16), cim_ref[ws, :])
        dg = dg_ref[...]
        yg = _gelu(ypre_s[...] + r_ref[0:1, :] * u_ref[...])
        s = _sigmoid(_dot(yg.astype(BF16), wg_ref[...]) + r_ref[1:2, :])
        y_ref[...] = yg * s * _silu(dg)

    def const(shape):
        return pl.BlockSpec(shape, lambda i: (0,) * len(shape))

    return _call(
        body, name="s5_fwd", grid=(S // TS,),
        in_specs=[_chunk(TS, BR, 10), _halo_prev(TS, BR, 10), _chunk(TS, BR, 11), const((8, W)), const((nblk, J * CB, 2 * SB)),
                  const((W, CB)), const((W, CB)), const((8, BR)), const((BR, BR))],
        out_specs=[pl.BlockSpec((TS, BR), lambda i: (i, 0)), pl.BlockSpec((TS, W), lambda i: (i, 0)), pl.BlockSpec((TS, W), lambda i: (i, 0))],
        out_shape=[jax.ShapeDtypeStruct((S, BR), F32), jax.ShapeDtypeStruct((S, W), F32), jax.ShapeDtypeStruct((S, W), F32)],
        scratch_shapes=[pltpu.VMEM((TS + 8, BR), F32), pltpu.VMEM((TS, J * CB), BF16), pltpu.VMEM((TS, SB), F32), pltpu.VMEM((TS, SB), F32),
                        pltpu.VMEM((TS, BR), F32), pltpu.VMEM((16, W), F32)],
        compiler_params=_params("arbitrary"),
    )(proj, proj, proj, a8, bbl, cre, cim, rows, wg)


def _out_ln(ys, w_out, x, rows, target=None):
    S, D = x.shape
    BR = D // 4
    TS = _tile(S, 256)
    last = target is not None

    def body(ya, yb, yc, yd, w_ref, x_ref, r_ref, *rest):
        t_ref = rest[0] if last else None
        o_ref, y_ref, cat_ref = rest[1:4] if last else rest[0:3]
        cat = jnp.concatenate([ya[...], yb[...], yc[...], yd[...]], axis=1).astype(BF16)
        y = _dot(cat, w_ref[...])
        z = ALPHA * x_ref[...] + r_ref[0:1, :] * y
        mu = jnp.mean(z, axis=-1, keepdims=True)
        zc = z - mu
        var = jnp.mean(zc * zc, axis=-1, keepdims=True)
        xn = zc * lax.rsqrt(var + LN_EPS) * r_ref[1:2, :] + r_ref[2:3, :]
        y_ref[...] = y
        cat_ref[...] = cat
        if last:
            l_ref = rest[4]

            @pl.when(pl.program_id(0) == 0)
            def _():
                l_ref[...] = jnp.zeros_like(l_ref)
            diff = xn - t_ref[...]
            o_ref[...] = diff * (1.0 / D)
            l_ref[...] += jnp.sum(_colsum(diff * diff), axis=1, keepdims=True) * (0.5 / D)
        else:
            o_ref[...] = xn

    yspec = pl.BlockSpec((TS, BR), lambda i: (i, 0))
    spec = pl.BlockSpec((TS, D), lambda i: (i, 0))
    lspec = pl.BlockSpec((1, 1), lambda i: (0, 0))
    return _call(
        body, name="out_ln_loss" if last else "out_ln", grid=(S // TS,),
        in_specs=[yspec] * 4 + [pl.BlockSpec((D, D), lambda i: (0, 0)), spec, pl.BlockSpec((8, D), lambda i: (0, 0))] + [spec] * last,
        out_specs=[spec, spec, spec] + [lspec] * last,
        out_shape=[jax.ShapeDtypeStruct((S, D), F32), jax.ShapeDtypeStruct((S, D), F32), jax.ShapeDtypeStruct((S, D), BF16)]
        + [jax.ShapeDtypeStruct((1, 1), F32)] * last,
        compiler_params=_params("arbitrary"),
    )(*ys, w_out, x, rows, *([target] if last else []))


def _ln_bwd(dout, x, y, rows, w_out):
    S, D = x.shape
    TS = _tile(S, 256)

    def body(do_ref, x_ref, y_ref, r_ref, w_ref, dxr_ref, dyb_ref, dcat_ref, acc_ref):
        @pl.when(pl.program_id(0) == 0)
        def _():
            acc_ref[...] = jnp.zeros_like(acc_ref)
        g1, lg = r_ref[0:1, :], r_ref[1:2, :]
        yv = y_ref[...]
        z = ALPHA * x_ref[...] + g1 * yv
        mu = jnp.mean(z, axis=-1, keepdims=True)
        zc = z - mu
        var = jnp.mean(zc * zc, axis=-1, keepdims=True)
        rstd = lax.rsqrt(var + LN_EPS)
        xhat = zc * rstd
        do = do_ref[...]
        dxh = do * lg
        dz = rstd * (dxh - jnp.mean(dxh, axis=-1, keepdims=True) - xhat * jnp.mean(dxh * xhat, axis=-1, keepdims=True))
        dxr_ref[...] = ALPHA * dz
        dyb = (g1 * dz).astype(BF16)
        dyb_ref[...] = dyb
        dcat_ref[...] = _dot_nt(dyb, w_ref[...])
        acc_ref[0:1, :] += _colsum(do * xhat)
        acc_ref[1:2, :] += _colsum(do)
        acc_ref[2:3, :] += _colsum(dz * yv)

    spec = pl.BlockSpec((TS, D), lambda i: (i, 0))
    rspec = pl.BlockSpec((8, D), lambda i: (0, 0))
    return _call(
        body, name="ln_bwd", grid=(S // TS,),
        in_specs=[spec, spec, spec, rspec, pl.BlockSpec((D, D), lambda i: (0, 0))],
        out_specs=[spec, spec, spec, rspec],
        out_shape=[jax.ShapeDtypeStruct((S, D), F32), jax.ShapeDtypeStruct((S, D), BF16), jax.ShapeDtypeStruct((S, D), F32),
                   jax.ShapeDtypeStruct((8, D), F32)],
        compiler_params=_params("arbitrary"),
    )(dout, x, y, rows, w_out)


def _dh(dproj, wg, dxres, x, mod, sides):
    S, D = x.shape
    NS = wg.shape[2]
    TS = _tile(S, 256)
    nt = S // TS
    ns = len(sides)

    def body(dp_ref, w_hbm, dxr_ref, x_ref, mod_ref, *rest):
        s_refs, dx_ref, sum_ref, so_refs = rest[:ns], rest[ns], rest[ns + 1], rest[ns + 2:2 * ns + 2]
        w_vmem, sem, send_sems, recv_sems, local_sems = rest[2 * ns + 2:]
        i = pl.program_id(0)
        ix, iy, ic = lax.axis_index("x"), lax.axis_index("y"), lax.axis_index("c")
        me = 4 * ix + 2 * iy + ic

        def exchange(which):
            (_, gather), x_side, o_side = sides[which], s_refs[which], so_refs[which]
            copies = [pltpu.make_async_copy(x_side if gather else x_side.at[me], o_side.at[me], local_sems.at[which])]
            for k in range(1, NDEV):
                px = 1 - ix if k & 4 else ix
                py = 1 - iy if k & 2 else iy
                pc = 1 - ic if k & 1 else ic
                copies.append(pltpu.make_async_remote_copy(
                    src_ref=x_side if gather else x_side.at[4 * px + 2 * py + pc], dst_ref=o_side.at[me],
                    send_sem=send_sems.at[which, k - 1], recv_sem=recv_sems.at[which, k - 1],
                    device_id=(px, py, pc), device_id_type=pl.DeviceIdType.MESH))
            return copies

        @pl.when(i == 0)
        def _():
            load = pltpu.make_async_copy(w_hbm, w_vmem, sem)
            load.start()
            for which in range(ns):
                for cp in exchange(which):
                    cp.start()
            sum_ref[...] = jnp.zeros_like(sum_ref)
            load.wait()

        dh = _dot_nt(dp_ref[:, 0:NS], w_vmem[0])
        for j in range(1, NDEV):
            dh = dh + _dot_nt(dp_ref[:, j * NS:(j + 1) * NS], w_vmem[j])
        dx_ref[...] = dxr_ref[...] + dh * mod_ref[0:1, :]
        sum_ref[0:1, :] += _colsum(dh * x_ref[...])
        sum_ref[1:2, :] += _colsum(dh)

        @pl.when(i == nt - 1)
        def _():
            for which in range(ns):
                for cp in exchange(which):
                    cp.wait()

    spec = pl.BlockSpec((TS, D), lambda i: (i, 0))
    rspec = pl.BlockSpec((8, D), lambda i: (0, 0))
    anyspec = pl.BlockSpec(memory_space=pl.ANY)
    side_shapes = [jax.ShapeDtypeStruct(((NDEV,) + a.shape) if g else a.shape, a.dtype) for a, g in sides]
    return _call(
        body, name="dh", grid=(nt,),
        in_specs=[pl.BlockSpec((TS, NDEV * NS), lambda i: (i, 0)), anyspec, spec, spec, rspec] + [anyspec] * ns,
        out_specs=[spec, rspec] + [anyspec] * ns,
        out_shape=[jax.ShapeDtypeStruct((S, D), F32), jax.ShapeDtypeStruct((8, D), F32)] + side_shapes,
        scratch_shapes=[pltpu.VMEM((NDEV, D, NS), BF16), pltpu.SemaphoreType.DMA, pltpu.SemaphoreType.DMA((max(ns, 1), NDEV - 1)),
                        pltpu.SemaphoreType.DMA((max(ns, 1), NDEV - 1)), pltpu.SemaphoreType.DMA((max(ns, 1),))],
        compiler_params=_params("arbitrary"),
    )(dproj, wg, dxres, x, mod, *[a for a, _ in sides])


def _halo_next(S, TS, BR, k):
    last = S // 8 - 1
    return pl.BlockSpec((8, BR), lambda i: (jnp.minimum((i + 1) * (TS // 8), last), k))


def _conv_a_bwd(proj, dcat, w8, BR):
    S = proj.shape[0]
    TS = _tile(S, 512)
    nt = S // TS

    def body(ab_ref, ac_ref, ax_ref, ag_ref, dy_ref, hc_ref, hx_ref, nab_ref, nag_ref, ndy_ref, w_ref, dp_ref, dw_ref, ext, dext):
        i = pl.program_id(0)

        @pl.when(i == 0)
        def _():
            dw_ref[...] = jnp.zeros_like(dw_ref)

        w0, w1, w2 = w_ref[0:1, :], w_ref[1:2, :], w_ref[2:3, :]
        ab, ac, ax, ag, dy = ab_ref[...], ac_ref[...], ax_ref[...], ag_ref[...], dy_ref[...]
        u = ac * ax
        ext[0:8, :] = jnp.where(i > 0, hc_ref[...] * hx_ref[...], 0.0)
        ext[8:8 + TS, :] = u
        u1, u2 = ext[pl.ds(7, TS), :], ext[pl.ds(6, TS), :]
        conv = u2 * w0 + u1 * w1 + u * w2
        sg = _silu(ag)
        dconv = dy * ab * sg
        dext[0:TS, :] = dconv
        dext[TS:TS + 8, :] = jnp.where(i < nt - 1, ndy_ref[...] * nab_ref[...] * _silu(nag_ref[...]), 0.0)
        du = w2 * dconv + w1 * dext[pl.ds(1, TS), :] + w0 * dext[pl.ds(2, TS), :]
        dp_ref[:, 0:BR] = (dy * conv * sg).astype(BF16)
        dp_ref[:, BR:2 * BR] = (du * ax).astype(BF16)
        dp_ref[:, 2 * BR:3 * BR] = (du * ac).astype(BF16)
        dp_ref[:, 3 * BR:4 * BR] = (dy * ab * conv * _dsilu(ag)).astype(BF16)
        dw_ref[0:1, :] += _colsum(dconv * u2)
        dw_ref[1:2, :] += _colsum(dconv * u1)
        dw_ref[2:3, :] += _colsum(dconv * u)

    rspec = pl.BlockSpec((8, BR), lambda i: (0, 0))
    return _call(
        body, name="conv_a_bwd", grid=(nt,),
        in_specs=[_chunk(TS, BR, 0), _chunk(TS, BR, 1), _chunk(TS, BR, 2), _chunk(TS, BR, 3), _chunk(TS, BR, 0),
                  _halo_prev(TS, BR, 1), _halo_prev(TS, BR, 2), _halo_next(S, TS, BR, 0), _halo_next(S, TS, BR, 3),
                  _halo_next(S, TS, BR, 0), rspec],
        out_specs=[pl.BlockSpec((TS, 4 * BR), lambda i: (i, 0)), rspec],
        out_shape=[jax.ShapeDtypeStruct((S, 12 * BR), BF16), jax.ShapeDtypeStruct((8, BR), F32)],
        scratch_shapes=[pltpu.VMEM((TS + 8, BR), F32), pltpu.VMEM((TS + 8, BR), F32)],
        compiler_params=_params("arbitrary"),
    )(proj, proj, proj, proj, dcat, proj, proj, proj, proj, dcat, w8)


def _attn_mix_bwd(proj, dcat, os_, ls_, head_ones, BR):
    S = proj.shape[0]
    TS = _tile(S, 512)

    def body(dy_ref, g_ref, o1, o2, o3, l1, l2, l3, ones_ref, d1, d2, d3, e1, e2, e3, dg_ref):
        w1, w2, w3 = _mix_weights(l1[...], l2[...], l3[...])
        mix = w1 * o1[...] + w2 * o2[...] + w3 * o3[...]
        g = g_ref[...]
        dy = dy_ref[...]
        dmix = dy * _silu(g)
        dg_ref[...] = dy * mix * _dsilu(g)
        t = jnp.dot(dmix * mix, ones_ref[...], preferred_element_type=F32, precision=lax.Precision.HIGHEST)
        d1[...] = w1 * dmix
        d2[...] = w2 * dmix
        d3[...] = w3 * dmix
        e1[...] = -w1 * t
        e2[...] = -w2 * t
        e3[...] = -w3 * t

    spec = pl.BlockSpec((TS, BR), lambda i: (i, 0))
    outs = _call(
        body, name="attn_mix_bwd", grid=(S // TS,),
        in_specs=[_chunk(TS, BR, 1), _chunk(TS, BR, 7)] + [spec] * 6 + [pl.BlockSpec((BR, BR), lambda i: (0, 0))],
        out_specs=[spec] * 7,
        out_shape=[jax.ShapeDtypeStruct((S, BR), F32)] * 7,
        compiler_params=_params("arbitrary"),
    )(dcat, proj, *os_, *ls_, head_ones)
    return outs[0:3], outs[3:6], outs[6]


def _attn_bwd(proj, do, e, lse, bias, dil, BR):
    S = proj.shape[0]
    HD, U, NU, nbu, hpb, HBK = _attn_geometry(S, BR, dil)
    scale = HD ** -0.5
    combos = [(n, r) for n in range(nbu) for r in range(dil)]

    def body(q_ref, kc_ref, kp_ref, vc_ref, vp_ref, do_ref, e_ref, l_ref, b_ref, dq_ref, dk_ref, dv_ref, db_ref,
             sbuf, dpbuf, pbuf, dsbuf, dkacc, dvacc, nxk, nxv, cark, carv):
        j = pl.program_id(1)
        mu = NU - 1 - j
        band, is_cur = _attn_mask()
        first = jnp.logical_and(band, jnp.logical_or(is_cur, mu > 0))

        @pl.when(j == 0)
        def _():
            cark[...] = jnp.zeros_like(cark)
            carv[...] = jnp.zeros_like(carv)
            db_ref[...] = jnp.zeros_like(db_ref)

        for ref in (dkacc, dvacc, nxk, nxv):
            ref[...] = jnp.zeros_like(ref)

        def keys(ref_c, ref_p, n, r):
            prev = ref_c[_attn_rows(n - 1, r, dil), :] if n > 0 else ref_p[_attn_rows(nbu - 1, r, dil), :]
            return jnp.concatenate([prev, ref_c[_attn_rows(n, r, dil), :]], axis=0).astype(BF16)

        for c, (n, r) in enumerate(combos):
            rows = _attn_rows(n, r, dil)
            q = (q_ref[rows, :] * scale).astype(BF16)
            dob = do_ref[rows, :].astype(BF16)
            kk, vv = keys(kc_ref, kp_ref, n, r), keys(vc_ref, vp_ref, n, r)
            for h in range(hpb):
                sl = slice(h * HD, (h + 1) * HD)
                sbuf[c * hpb + h] = _dot_nt(q[:, sl], kk[:, sl])
                dpbuf[c * hpb + h] = _dot_nt(dob[:, sl], vv[:, sl])
        dbs = [None] * hpb
        for c, (n, r) in enumerate(combos):
            rows = _attn_rows(n, r, dil)
            lse, ev = l_ref[rows, :], e_ref[rows, :]
            for h in range(hpb):
                one = slice(h * HD, h * HD + 1)
                s = jnp.where(band if n > 0 else first, sbuf[c * hpb + h] + b_ref[h], NEG)
                p = jnp.exp(s - lse[:, one])
                ds = p * (dpbuf[c * hpb + h] + ev[:, one])
                pbuf[c * hpb + h] = p.astype(BF16)
                dsbuf[c * hpb + h] = ds.astype(BF16)
                dbs[h] = ds if dbs[h] is None else dbs[h] + ds
        for h in range(hpb):
            db_ref[h] += dbs[h]
        for c, (n, r) in enumerate(combos):
            rows = _attn_rows(n, r, dil)
            q = (q_ref[rows, :] * scale).astype(BF16)
            dob = do_ref[rows, :].astype(BF16)
            kk = keys(kc_ref, kp_ref, n, r)
            dq, dkk, dvv = [], [], []
            for h in range(hpb):
                sl = slice(h * HD, (h + 1) * HD)
                dsb = dsbuf[c * hpb + h]
                dq.append(_dot(dsb, kk[:, sl]) * scale)
                dkk.append(_dot_tn(dsb, q[:, sl]))
                dvv.append(_dot_tn(pbuf[c * hpb + h], dob[:, sl]))
            dq_ref[rows, :] = jnp.concatenate(dq, axis=1)
            dkk, dvv = jnp.concatenate(dkk, axis=1), jnp.concatenate(dvv, axis=1)
            dkacc[rows, :] += dkk[BLK:2 * BLK, :]
            dvacc[rows, :] += dvv[BLK:2 * BLK, :]
            if n > 0:
                prow = _attn_rows(n - 1, r, dil)
                dkacc[prow, :] += dkk[0:BLK, :]
                dvacc[prow, :] += dvv[0:BLK, :]
            else:
                prow = _attn_rows(nbu - 1, r, dil)
                nxk[prow, :] += dkk[0:BLK, :]
                nxv[prow, :] += dvv[0:BLK, :]
        dk_ref[...] = dkacc[...] + cark[...]
        dv_ref[...] = dvacc[...] + carv[...]
        cark[...] = nxk[...]
        carv[...] = nxv[...]

    def cur(c):
        return pl.BlockSpec((U, ATT_LANES), lambda hb, j: (NU - 1 - j, c * HBK + hb))

    def prev(c):
        return pl.BlockSpec((U, ATT_LANES), lambda hb, j: (jnp.maximum(NU - 2 - j, 0), c * HBK + hb))

    own = pl.BlockSpec((U, ATT_LANES), lambda hb, j: (NU - 1 - j, hb))
    bspec = pl.BlockSpec((hpb, BLK, 2 * BLK), lambda hb, j: (hb, 0, 0))
    nhb = len(combos) * hpb
    unit = pltpu.VMEM((U, ATT_LANES), F32)
    return _call(
        body, name="attn_bwd_d%d" % dil, grid=(HBK, NU),
        in_specs=[cur(4), cur(5), prev(5), cur(6), prev(6), own, own, own, bspec],
        out_specs=[own, own, own, bspec],
        out_shape=[jax.ShapeDtypeStruct((S, BR), F32)] * 3 + [jax.ShapeDtypeStruct((8, BLK, 2 * BLK), F32)],
        scratch_shapes=[pltpu.VMEM((nhb, BLK, 2 * BLK), F32)] * 2 + [pltpu.VMEM((nhb, BLK, 2 * BLK), BF16)] * 2 + [unit] * 6,
        compiler_params=_params("arbitrary", "arbitrary"),
    )(proj, proj, proj, proj, proj, do, e, lse, bias)


def _attn_dsum(ds, dbg, dproj, BR):
    S = dbg.shape[0]
    TS = _tile(S, 512)

    def body(*refs):
        o_ref = refs[11]
        for k in range(3):
            o_ref[:, k * BR:(k + 1) * BR] = (refs[k][...] + refs[3 + k][...] + refs[6 + k][...]).astype(BF16)
        o_ref[:, 3 * BR:4 * BR] = refs[9][...].astype(BF16)

    spec = pl.BlockSpec((TS, BR), lambda i: (i, 0))
    return _call(
        body, name="attn_dsum", grid=(S // TS,),
        in_specs=[spec] * 10 + [pl.BlockSpec(memory_space=pl.ANY)],
        out_specs=pl.BlockSpec((TS, 4 * BR), lambda i: (i, 1)),
        out_shape=jax.ShapeDtypeStruct(dproj.shape, BF16),
        input_output_aliases={10: 0},
        compiler_params=_params("arbitrary"),
    )(*[t for trip in ds for t in trip], dbg, dproj)


def _lru_bwd(proj, h, dcat, rows, wa, wx, dproj, BR):
    S = proj.shape[0]
    TS = _tile(S, 256)
    PAD = TS // 2
    nt = S // TS

    def body(cx_ref, hx_ref, cg_ref, h_ref, hh_ref, dy_ref, r_ref, wa_ref, wx_ref, _,
             dp_ref, sum_ref, dwa_ref, dwx_ref, ext, hext, aext, dext, sa, sb, carry):
        i = pl.program_id(0)
        ti = nt - 1 - i

        @pl.when(i == 0)
        def _():
            sa[TS:TS + PAD, :] = jnp.ones((PAD, BR), F32)
            sb[TS:TS + PAD, :] = jnp.zeros((PAD, BR), F32)
            carry[...] = jnp.zeros_like(carry)
            dext[TS:TS + 8, :] = jnp.zeros((8, BR), F32)
            sum_ref[...] = jnp.zeros_like(sum_ref)
            dwa_ref[...] = jnp.zeros_like(dwa_ref)
            dwx_ref[...] = jnp.zeros_like(dwx_ref)

        w0, w1, w2, w3 = r_ref[0:1, :], r_ref[1:2, :], r_ref[2:3, :], r_ref[3:4, :]
        sp = r_ref[7:8, :]
        cx = cx_ref[...]
        ext[0:8, :] = jnp.where(ti > 0, hx_ref[...], 0.0)
        ext[8:8 + TS, :] = cx
        x3, x2, x1 = ext[pl.ds(5, TS), :], ext[pl.ds(6, TS), :], ext[pl.ds(7, TS), :]
        xc = x3 * w0 + x2 * w1 + x1 * w2 + cx * w3 + r_ref[4:5, :]
        wa_, wx_ = wa_ref[...], wx_ref[...]
        r, ig, a, mult = _lru_gates(xc, wa_, wx_, r_ref[5:6, :], r_ref[6:7, :], sp)
        cg, dy, hv = cg_ref[...], dy_ref[...], h_ref[...]
        dp_ref[:, BR:2 * BR] = (dy * hv * _dsilu(cg)).astype(BF16)
        aext[0:TS, :] = a
        aext[TS:TS + 8, :] = jnp.broadcast_to(carry[0:1, :], (8, BR))
        sa[0:TS, :] = aext[pl.ds(1, TS), :]
        sb[0:TS, :] = dy * _silu(cg)
        d = 1
        while d < TS:
            A, B = sa[0:TS, :], sb[0:TS, :]
            As, Bs = sa[pl.ds(d, TS), :], sb[pl.ds(d, TS), :]
            sb[0:TS, :] = B + A * Bs
            sa[0:TS, :] = A * As
            d *= 2
        gh = sb[0:TS, :] + sa[0:TS, :] * carry[1:2, :]
        carry[0:1, :] = a[0:1, :]
        carry[1:2, :] = gh[0:1, :]
        hext[0:8, :] = jnp.where(ti > 0, hh_ref[...], 0.0)
        hext[8:8 + TS, :] = hv
        da = gh * hext[pl.ds(7, TS), :]
        dmult = gh * ig * xc
        dig = gh * mult * xc
        dxc = gh * mult * ig
        dla = da * a - dmult * a * a / mult
        dpr = dla * (-LRU_C) * sp * r * (1.0 - r)
        dpi = dig * ig * (1.0 - ig)
        xb, dprb, dpib = xc.astype(BF16), dpr.astype(BF16), dpi.astype(BF16)
        dwa_ref[...] += _dot_tn(xb, dprb)
        dwx_ref[...] += _dot_tn(xb, dpib)
        dxc = dxc + _dot_nt(dprb, wa_) + _dot_nt(dpib, wx_)
        sum_ref[0:1, :] += _colsum(dxc * x3)
        sum_ref[1:2, :] += _colsum(dxc * x2)
        sum_ref[2:3, :] += _colsum(dxc * x1)
        sum_ref[3:4, :] += _colsum(dxc * cx)
        sum_ref[4:5, :] += _colsum(dxc)
        sum_ref[5:6, :] += _colsum(dpr)
        sum_ref[6:7, :] += _colsum(dpi)
        sum_ref[7:8, :] += _colsum(dla * (-LRU_C) * r)
        dext[0:TS, :] = dxc
        dcx = w3 * dxc + w2 * dext[pl.ds(1, TS), :] + w1 * dext[pl.ds(2, TS), :] + w0 * dext[pl.ds(3, TS), :]
        dext[TS:TS + 8, :] = dxc[0:8, :]
        dp_ref[:, 0:BR] = dcx.astype(BF16)

    def rev(k):
        return pl.BlockSpec((TS, BR), lambda i: (nt - 1 - i, k))

    def rev_halo(k):
        return pl.BlockSpec((8, BR), lambda i: (jnp.maximum((nt - 1 - i) * (TS // 8) - 1, 0), k))

    full = pl.BlockSpec((BR, BR), lambda i: (0, 0))
    rspec = pl.BlockSpec((8, BR), lambda i: (0, 0))
    return _call(
        body, name="lru_bwd", grid=(nt,),
        in_specs=[rev(8), rev_halo(8), rev(9), rev(0), rev_halo(0), rev(2), rspec, full, full, pl.BlockSpec(memory_space=pl.ANY)],
        out_specs=[pl.BlockSpec((TS, 2 * BR), lambda i: (nt - 1 - i, 4)), rspec, full, full],
        input_output_aliases={9: 0},
        out_shape=[jax.ShapeDtypeStruct(dproj.shape, BF16), jax.ShapeDtypeStruct((8, BR), F32),
                   jax.ShapeDtypeStruct((BR, BR), F32), jax.ShapeDtypeStruct((BR, BR), F32)],
        scratch_shapes=[pltpu.VMEM((TS + 8, BR), F32)] * 4 + [pltpu.VMEM((TS + PAD, BR), F32)] * 2 + [pltpu.VMEM((8, BR), F32)],
        compiler_params=_params("arbitrary"),
    )(proj, proj, proj, h, h, dcat, rows, wa, wx, dproj)


def _s5_bwd(proj, dcat, xre_all, xim_all, a8, cl, bbre, bbim, cre, cim, rows, wg, dproj, BR):
    S = proj.shape[0]
    W = a8.shape[1]
    TS = _tile(S, 256)
    nt = S // TS
    CB, SB, J = S5_CB, S5_SB, S5_LAGS
    nblk = BR // CB

    def body(u_ref, dg_ref, dy_ref, xre_ref, xim_ref, hre_ref, him_ref, a8_ref, cl_ref, bbre_ref, bbim_ref, cre_ref, cim_ref,
             r_ref, wg_ref, _, dp_ref, sum_ref, dwg_ref, da_ref, dbbre_ref, dbbim_ref, dcre_ref, dcim_ref,
             sre, sim, ere, eim, ypre_s, dext, dylag, carry):
        i = pl.program_id(0)
        ti = nt - 1 - i

        @pl.when(i == 0)
        def _():
            dext[TS:TS + 8, :] = jnp.zeros((8, BR), F32)
            carry[...] = jnp.zeros_like(carry)
            for ref in (sum_ref, dwg_ref, da_ref, dbbre_ref, dbbim_ref, dcre_ref, dcim_ref):
                ref[...] = jnp.zeros_like(ref)

        u, dg, dy = u_ref[...], dg_ref[...], dy_ref[...]
        for m in range(nblk):
            cs, ws = slice(m * CB, (m + 1) * CB), slice(m * SB, (m + 1) * SB)
            ypre_s[:, cs] = (_dot(xre_ref[:, ws].astype(BF16), cre_ref[ws, :]) - _dot(xim_ref[:, ws].astype(BF16), cim_ref[ws, :]))
        ypre = ypre_s[...] + r_ref[0:1, :] * u
        yg = _gelu(ypre)
        wg_ = wg_ref[...]
        s = _sigmoid(_dot(yg.astype(BF16), wg_) + r_ref[1:2, :])
        dgl = dy * _silu(dg)
        dp_ref[:, BR:2 * BR] = (dy * yg * s * _dsilu(dg)).astype(BF16)
        dps = dgl * yg * s * (1.0 - s)
        dpsb = dps.astype(BF16)
        sum_ref[1:2, :] += _colsum(dps)
        dwg_ref[...] += _dot_tn(yg.astype(BF16), dpsb)
        dyp = (dgl * s + _dot_nt(dpsb, wg_)) * _dgelu(ypre)
        sum_ref[0:1, :] += _colsum(dyp * u)
        dext[0:TS, :] = dyp
        for m in range(nblk):
            cs, ws = slice(m * CB, (m + 1) * CB), slice(m * SB, (m + 1) * SB)
            dypb = dext[0:TS, cs].astype(BF16)
            xre, xim = xre_ref[:, ws], xim_ref[:, ws]
            dcre_ref[ws, :] += _dot_tn(xre.astype(BF16), dypb)
            dcim_ref[ws, :] -= _dot_tn(xim.astype(BF16), dypb)
            for j in range(J):
                dylag[:, j * CB:(j + 1) * CB] = dext[pl.ds(j, TS), cs].astype(BF16)
            w = _dot(dylag[...], cl_ref[m])
            sre[...] = w[:, 0:SB]
            sim[...] = w[:, SB:2 * SB]
            ar = jnp.broadcast_to(a8_ref[0:1, ws], (8, SB))
            ai = jnp.broadcast_to(a8_ref[1:2, ws], (8, SB))
            gr, gi = carry[0:8, ws], carry[8:16, ws]
            for g in reversed(range(TS // 8)):
                rg = slice(8 * g, 8 * g + 8)
                gr, gi = sre[rg, :] + ar * gr + ai * gi, sim[rg, :] + ar * gi - ai * gr
                sre[rg, :] = gr
                sim[rg, :] = gi
            carry[0:8, ws] = gr
            carry[8:16, ws] = gi
            gre, gim = sre[...], sim[...]
            ere[0:8, :] = jnp.where(ti > 0, hre_ref[:, ws], 0.0)
            eim[0:8, :] = jnp.where(ti > 0, him_ref[:, ws], 0.0)
            ere[8:8 + TS, :] = xre
            eim[8:8 + TS, :] = xim
            xpr, xpi = ere[pl.ds(7, TS), :], eim[pl.ds(7, TS), :]
            da_ref[0:1, ws] += _colsum(gre * xpr + gim * xpi)
            da_ref[1:2, ws] += _colsum(gim * xpr - gre * xpi)
            greb, gimb = gre.astype(BF16), gim.astype(BF16)
            ub = u_ref[:, cs].astype(BF16)
            dbbre_ref[cs, :] += _dot_tn(ub, greb)
            dbbim_ref[cs, :] += _dot_tn(ub, gimb)
            du = _dot_nt(greb, bbre_ref[cs, :]) + _dot_nt(gimb, bbim_ref[cs, :]) + r_ref[0:1, cs] * dext[0:TS, cs]
            dp_ref[:, cs] = du.astype(BF16)
        dext[TS:TS + 8, :] = dext[0:8, :]

    def rev(width, k):
        return pl.BlockSpec((TS, width), lambda i: (nt - 1 - i, k))

    def rev_halo(width):
        return pl.BlockSpec((8, width), lambda i: (jnp.maximum((nt - 1 - i) * (TS // 8) - 1, 0), 0))

    def const(shape):
        return pl.BlockSpec(shape, lambda i: (0,) * len(shape))

    return _call(
        body, name="s5_bwd", grid=(nt,),
        in_specs=[rev(BR, 10), rev(BR, 11), rev(BR, 3), rev(W, 0), rev(W, 0), rev_halo(W), rev_halo(W),
                  const((8, W)), const((nblk, J * CB, 2 * SB)), const((BR, SB)), const((BR, SB)), const((W, CB)), const((W, CB)),
                  const((8, BR)), const((BR, BR)), pl.BlockSpec(memory_space=pl.ANY)],
        input_output_aliases={15: 0},
        out_specs=[pl.BlockSpec((TS, 2 * BR), lambda i: (nt - 1 - i, 5)), const((8, BR)), const((BR, BR)), const((8, W)),
                   const((BR, SB)), const((BR, SB)), const((W, CB)), const((W, CB))],
        out_shape=[jax.ShapeDtypeStruct(dproj.shape, BF16), jax.ShapeDtypeStruct((8, BR), F32), jax.ShapeDtypeStruct((BR, BR), F32),
                   jax.ShapeDtypeStruct((8, W), F32), jax.ShapeDtypeStruct((BR, SB), F32), jax.ShapeDtypeStruct((BR, SB), F32),
                   jax.ShapeDtypeStruct((W, CB), F32), jax.ShapeDtypeStruct((W, CB), F32)],
        scratch_shapes=[pltpu.VMEM((TS, SB), F32)] * 2 + [pltpu.VMEM((TS + 8, SB), F32)] * 2
        + [pltpu.VMEM((TS, BR), F32), pltpu.VMEM((TS + 8, BR), F32), pltpu.VMEM((TS, J * CB), BF16), pltpu.VMEM((16, W), F32)],
        compiler_params=_params("arbitrary"),
    )(proj, proj, dcat, xre_all, xim_all, xre_all, xim_all, a8, cl, bbre, bbim, cre, cim, rows, wg, dproj)


_WEIGHTS = ['rel_bias', 'w_ada', 'b_ada', 'w_in', 'conv_a', 'conv_c', 'conv_c_b', 'lru_wa', 'lru_ba', 'lru_wx', 'lru_bx',
            'lru_lambda', 's5_lam_re', 's5_lam_im', 's5_log_dt', 's5_b_re', 's5_b_im', 's5_c_re', 's5_c_im', 's5_d',
            's5_w_glu', 's5_b_glu', 'w_out', 'ln_g', 'ln_b']
_LAYER_SMALL = ['conv_a', 'conv_c', 'conv_c_b', 'lru_wa', 'lru_ba', 'lru_wx', 'lru_bx', 'lru_lambda', 's5_lam_re', 's5_lam_im',
                's5_log_dt', 's5_b_re', 's5_b_im', 's5_c_re', 's5_c_im', 's5_d', 's5_b_glu']
_SMALL = ['rel_bias'] + _LAYER_SMALL + ['ln_g', 'ln_b']


def _t5_bucket(dist):
    max_exact = REL_BUCKETS // 2
    nf = jnp.maximum(dist, 1).astype(F32)
    large = max_exact + (jnp.log(nf / max_exact) / math.log(REL_MAX_DIST / max_exact)
                         * (REL_BUCKETS - max_exact)).astype(jnp.int32)
    large = jnp.minimum(large, REL_BUCKETS - 1)
    return jnp.where(dist < max_exact, dist, large)


def _bias_tables(rel_bias):
    i = jnp.arange(BLK)[:, None]
    j = jnp.arange(2 * BLK)[None, :]
    delta = i + BLK - j
    out = []
    for window, dil in DILATIONS:
        bucket = _t5_bucket(jnp.clip(delta, 0, window // dil) * dil)
        onehot = (bucket[:, :, None] == jnp.arange(REL_BUCKETS)[None, None, :]).astype(F32)
        out.append(jnp.einsum('ijb,bh->hij', onehot, rel_bias, precision=lax.Precision.HIGHEST))
    return jnp.stack(out)


def _prep_layer(q):
    eye8 = jnp.eye(8, dtype=F32)
    G = q['s5_lam_re'].shape[0]
    nblk = G // 8

    def block_diag(w):
        hd = w.shape[1]
        return (w[:, :, None, :] * eye8[:, None, :, None]).reshape(8 * hd, 8 * hd)

    def compact_b(bb):
        t = jnp.transpose(bb.reshape(nblk, 8, S5_STATE, S5_CH), (0, 1, 3, 2))
        return (t[:, :, :, None, :] * eye8[None, :, None, :, None]).reshape(G * S5_CH, 8 * S5_STATE)

    def compact_c(cc):
        t = jnp.transpose(cc.reshape(nblk, 8, S5_CH, S5_STATE), (0, 1, 3, 2))
        return (t[:, :, :, None, :] * eye8[None, :, None, :, None]).reshape(G * S5_STATE, 8 * S5_CH)

    lam_re, lam_im = q['s5_lam_re'], q['s5_lam_im']
    dt = jnp.exp(q['s5_log_dt'])[:, None]
    mag = jnp.exp(lam_re * dt)
    ab_re = mag * jnp.cos(lam_im * dt)
    ab_im = mag * jnp.sin(lam_im * dt)
    den = lam_re * lam_re + lam_im * lam_im
    f_re = ((ab_re - 1.0) * lam_re + ab_im * lam_im) / den
    f_im = (ab_im * lam_re - (ab_re - 1.0) * lam_im) / den
    b_re, b_im = q['s5_b_re'], q['s5_b_im']
    bb_re = f_re[..., None] * b_re - f_im[..., None] * b_im
    bb_im = f_re[..., None] * b_im + f_im[..., None] * b_re
    return dict(
        conv_a=q['conv_a'], conv_c=q['conv_c'], conv_c_b=q['conv_c_b'], lru_ba=q['lru_ba'], lru_bx=q['lru_bx'],
        sp=jax.nn.softplus(-q['lru_lambda']), wa=block_diag(q['lru_wa']), wx=block_diag(q['lru_wx']),
        ar=ab_re.reshape(-1), ai=ab_im.reshape(-1), bbre=compact_b(bb_re), bbim=compact_b(bb_im),
        cre=compact_c(q['s5_c_re']), cim=compact_c(q['s5_c_im']), s5_d=q['s5_d'], s5_b_glu=q['s5_b_glu'])


def _s5_lag_weights(kp):
    J, CB, SB = S5_LAGS, S5_CB, S5_SB
    ar, ai = kp['ar'], kp['ai']
    W = ar.shape[0]
    nblk = W // SB
    cr, ci = jnp.ones_like(ar), jnp.zeros_like(ai)
    pows = []
    for _ in range(J + 1):
        pows.append((cr, ci))
        cr, ci = cr * ar - ci * ai, cr * ai + ci * ar
    pwr = jnp.transpose(jnp.stack([p[0] for p in pows[:J]]).reshape(J, nblk, SB), (1, 0, 2))[:, :, None, :]
    pwi = jnp.transpose(jnp.stack([p[1] for p in pows[:J]]).reshape(J, nblk, SB), (1, 0, 2))[:, :, None, :]

    def lagged(re, im, sign):
        out = jnp.concatenate([re * pwr - sign * im * pwi, sign * re * pwi + im * pwr], axis=-1)
        return out.reshape(nblk, J * CB, 2 * SB).astype(BF16)

    bbl = lagged(kp['bbre'].reshape(nblk, 1, CB, SB), kp['bbim'].reshape(nblk, 1, CB, SB), 1.0)
    c0r = jnp.transpose(kp['cre'].reshape(nblk, SB, CB), (0, 2, 1))[:, None]
    c0i = -jnp.transpose(kp['cim'].reshape(nblk, SB, CB), (0, 2, 1))[:, None]
    cl = lagged(c0r, c0i, -1.0)
    return _rows8([pows[J][0], pows[J][1]], W), bbl, cl


def _rows8(vecs, width):
    rows = [v.reshape(1, width).astype(F32) for v in vecs]
    return jnp.concatenate(rows + [jnp.zeros((8 - len(rows), width), F32)], axis=0)


def kernel(x, c, rel_bias, w_ada, b_ada, w_in, conv_a, conv_c, conv_c_b, lru_wa, lru_ba, lru_wx, lru_bx, lru_lambda, s5_lam_re, s5_lam_im, s5_log_dt, s5_b_re, s5_b_im, s5_c_re, s5_c_im, s5_d, s5_w_glu, s5_b_glu, w_out, ln_g, ln_b, loss_target, m_rel_bias, m_w_ada, m_b_ada, m_w_in, m_conv_a, m_conv_c, m_conv_c_b, m_lru_wa, m_lru_ba, m_lru_wx, m_lru_bx, m_lru_lambda, m_s5_lam_re, m_s5_lam_im, m_s5_log_dt, m_s5_b_re, m_s5_b_im, m_s5_c_re, m_s5_c_im, m_s5_d, m_s5_w_glu, m_s5_b_glu, m_w_out, m_ln_g, m_ln_b, v_rel_bias, v_w_ada, v_b_ada, v_w_in, v_conv_a, v_conv_c, v_conv_c_b, v_lru_wa, v_lru_ba, v_lru_wx, v_lru_bx, v_lru_lambda, v_s5_lam_re, v_s5_lam_im, v_s5_log_dt, v_s5_b_re, v_s5_b_im, v_s5_c_re, v_s5_c_im, v_s5_d, v_s5_w_glu, v_s5_b_glu, v_w_out, v_ln_g, v_ln_b):
    a = dict(locals())
    me = 4 * lax.axis_index("x") + 2 * lax.axis_index("y") + lax.axis_index("c")
    x0, target = x[0], loss_target[0]
    S, D = x0.shape
    BR = D // 4
    NS = w_in.shape[2]
    SH = BR // NDEV
    assert S % (BLK * DILATIONS[-1][1]) == 0 and BR % (8 * S5_CH) == 0

    small = _all_gather(_pack([c, conv_a, conv_c]), "gather_small")
    per_dev = [_unpack(small[d], [c.shape, conv_a.shape, conv_c.shape]) for d in range(NDEV)]
    c_all = jnp.concatenate([p[0] for p in per_dev], axis=0)
    conv_a_full = jnp.concatenate([p[1] for p in per_dev], axis=2)
    conv_c_full = jnp.concatenate([p[2] for p in per_dev], axis=2)
    me1 = me.reshape(1).astype(jnp.int32)
    w_in_bf, w_out_bf, w_glu_bf = w_in.astype(BF16), w_out.astype(BF16), s5_w_glu.astype(BF16)

    b_cols = lax.dynamic_slice(b_ada, (0, me * NS), (DEPTH, NS)).reshape(DEPTH, 1, NS)
    ada_all = _all_gather(_ada_cols(c_all, w_ada, b_cols), "gather_ada")
    ada_me = lax.dynamic_index_in_dim(ada_all, me, axis=2, keepdims=False)
    ada_me = jnp.transpose(ada_me, (1, 0, 2)).reshape(DEPTH, 3 * D)
    shift, scale, gate = ada_me[:, :D], ada_me[:, D:2 * D], ada_me[:, 2 * D:]

    bias_tabs, bias_pull = jax.vjp(_bias_tables, rel_bias)
    HD = BR // 8
    head_ones = jnp.kron(jnp.eye(8, dtype=F32), jnp.ones((HD, HD), F32))

    saved = []
    xl = x0
    for l in range(DEPTH):
        q = {n: a[n][l] for n in _LAYER_SMALL}
        q['conv_a'], q['conv_c'] = conv_a_full[l], conv_c_full[l]
        kp, pull = jax.vjp(_prep_layer, q)
        a8, bbl, cl = _s5_lag_weights(kp)
        mod = _rows8([1.0 + scale[l], shift[l]], D)
        arows = _rows8(list(kp['conv_a']), BR)
        lrows = _rows8(list(kp['conv_c']) + [kp['conv_c_b'], kp['lru_ba'], kp['lru_bx'], kp['sp']], BR)
        srows = _rows8([kp['s5_d'], kp['s5_b_glu']], BR)
        orows = _rows8([1.0 + gate[l], ln_g[l], ln_b[l]], D)
        wa, wx = kp['wa'].astype(BF16), kp['wx'].astype(BF16)
        s5w = [kp[n].astype(BF16) for n in ('bbre', 'bbim', 'cre', 'cim')]

        hbf = _modulate(xl, mod)
        proj, w_in_l, w_out_l, w_glu_l = _proj_gather(hbf, w_in_bf[l], [w_out_bf[l], w_glu_bf[l]], me1)
        w_out_l, w_glu_l = w_out_l.reshape(D, D), w_glu_l.reshape(BR, BR)
        ya = _conv_a_fwd(proj, arows, BR)
        os_, ls_ = [], []
        for g, (_, dil) in enumerate(DILATIONS):
            o, lse = _attn_fwd(proj, bias_tabs[g], dil, BR)
            os_.append(o)
            ls_.append(lse)
        yb = _attn_mix_fwd(proj, os_, ls_, BR)
        yc, hs = _lru_fwd(proj, lrows, wa, wx, BR)
        yd, xre, xim = _s5_fwd(proj, a8, bbl, s5w[2], s5w[3], srows, w_glu_l, BR)
        if l < DEPTH - 1:
            xn, y, cat = _out_ln([ya, yb, yc, yd], w_out_l, xl, orows)
        else:
            xn, y, cat, loss_local = _out_ln([ya, yb, yc, yd], w_out_l, xl, orows, target)
        saved.append(dict(x=xl, proj=proj, hbf=hbf, os=os_, ls=ls_, hs=hs, xre=xre, xim=xim, y=y, cat=cat, pull=pull,
                          mod=mod, arows=arows, lrows=lrows, srows=srows, orows=orows, wa=wa, wx=wx, s5w=s5w,
                          qw=(a8, cl), w_in=w_in_l, w_out=w_out_l, w_glu=w_glu_l))
        xl = xn

    dout = xl

    dbias = jnp.zeros_like(bias_tabs)
    lgrads, dada, dw_in_parts, dw_out_parts, glu_parts = [None] * DEPTH, [None] * DEPTH, [None] * DEPTH, [None] * DEPTH, [None] * DEPTH
    for l in reversed(range(DEPTH)):
        sv = saved[l]
        proj = sv['proj']
        dxres, dyb, dcat, lnsum = _ln_bwd(dout, sv['x'], sv['y'], sv['orows'], sv['w_out'])
        dw_out_parts[l] = _dw_scatter(sv['cat'], dyb, D // NDEV, True, me1, "dw_out_scatter")
        dproj, asum = _conv_a_bwd(proj, dcat, sv['arows'], BR)
        dos, es, dbg = _attn_mix_bwd(proj, dcat, sv['os'], sv['ls'], head_ones, BR)
        dqkv, dbs = [], []
        for g, (_, dil) in enumerate(DILATIONS):
            dq_, dk_, dv_, db_ = _attn_bwd(proj, dos[g], es[g], sv['ls'][g], bias_tabs[g], dil, BR)
            dqkv.append((dq_, dk_, dv_))
            dbs.append(db_)
        dbias = dbias + jnp.stack(dbs)
        dproj = _attn_dsum(dqkv, dbg, dproj, BR)
        dproj, lsum, dwa, dwx = _lru_bwd(proj, sv['hs'], dcat, sv['lrows'], sv['wa'], sv['wx'], dproj, BR)
        dproj, ssum, dwg, da_, dbbre, dbbim, dcre, dcim = _s5_bwd(
            proj, dcat, sv['xre'], sv['xim'], *sv['qw'], *sv['s5w'], sv['srows'], sv['w_glu'], dproj, BR)
        dkp = dict(conv_a=asum[0:3], conv_c=lsum[0:4], conv_c_b=lsum[4], lru_ba=lsum[5], lru_bx=lsum[6], sp=lsum[7],
                   wa=dwa, wx=dwx, ar=da_[0], ai=da_[1], bbre=dbbre, bbim=dbbim, cre=dcre, cim=dcim,
                   s5_d=ssum[0], s5_b_glu=ssum[1])
        lgrads[l] = dict(sv['pull'](dkp)[0], ln_g=lnsum[0], ln_b=lnsum[1])
        sides = [(dwg.reshape(NDEV, BR // NDEV, BR), False)]
        if l == 0:
            local = {'rel_bias': bias_pull(dbias)[0], 'loss': loss_local.reshape(1)}
            for n in _LAYER_SMALL + ['ln_g', 'ln_b']:
                local[n] = jnp.stack([lgrads[k][n] for k in range(DEPTH)])
            packed = _SMALL + ['loss']
            sides.append((_pack([local[n] for n in packed]), True))
        dout, msum, glu_parts[l], *gathered = _dh(dproj, sv['w_in'], dxres, sv['x'], sv['mod'], sides)
        dw_in_parts[l] = _dw_scatter(sv['hbf'], dproj, NS, False, me1, "dw_in_scatter")
        dada[l] = jnp.concatenate([msum[1], msum[0], lnsum[2]])
    grad_x = dout[None]

    out = {}

    def put(name, res):
        out[name] = res

    put('w_in', _adamw(dw_in_parts, w_in, m_w_in, v_w_in, "adamw_w_in"))
    put('w_out', _adamw(dw_out_parts, w_out, m_w_out, v_w_out, "adamw_w_out"))
    put('s5_w_glu', _adamw(glu_parts, s5_w_glu, m_s5_w_glu, v_s5_w_glu, "adamw_w_glu"))

    dada_all = _all_gather(jnp.stack(dada), "gather_dada")
    dada_cols = jnp.transpose(lax.dynamic_slice(dada_all, (0, 0, me * NS), (NDEV, DEPTH, NS)), (1, 0, 2))
    gw_ada, gb_ada = _ada_grads(c_all, dada_all, dada_cols)
    put('w_ada', _adamw([gw_ada[l] for l in range(DEPTH)], w_ada, m_w_ada, v_w_ada, "adamw_w_ada"))

    shapes = [local[n].shape for n in packed]
    gs = dict(zip(packed, _unpack(_sum8(gathered[0], "sum_small_grads"), shapes)))
    loss = gs['loss'][0]
    gs['conv_a'] = lax.dynamic_slice_in_dim(gs['conv_a'], me * SH, SH, axis=2)
    gs['conv_c'] = lax.dynamic_slice_in_dim(gs['conv_c'], me * SH, SH, axis=2)
    gs['b_ada'] = gb_ada
    names = _SMALL + ['b_ada']
    shapes = [a[n].shape for n in names]
    res = _adamw([_pack([gs[n] for n in names])], _pack([a[n] for n in names])[None], _pack([a['m_' + n] for n in names])[None],
                 _pack([a['v_' + n] for n in names])[None], "adamw_small")
    unpacked = [_unpack(r[0], shapes) for r in res]
    for k, n in enumerate(names):
        put(n, tuple(unpacked[j][k] for j in range(4)))

    return (loss, grad_x, *[out[n][0] for n in _WEIGHTS], *[out[n][1] for n in _WEIGHTS],
            *[out[n][2] for n in _WEIGHTS], *[out[n][3] for n in _WEIGHTS])
```

```python
import math

import jax
import jax.numpy as jnp
from jax import lax
from jax.experimental import pallas as pl
from jax.experimental.pallas import tpu as pltpu

F32 = jnp.float32
BF16 = jnp.bfloat16

NDEV = 8
DEPTH = 2
BLK = 128
DILATIONS = ((128, 1), (512, 4), (2048, 16))
REL_BUCKETS = 32
REL_MAX_DIST = 2048
LRU_C = 8.0
S5_CH = 16
S5_STATE = 64
ALPHA = (2 * DEPTH) ** 0.25
LN_EPS = 1e-5
ADAM_LR, ADAM_B1, ADAM_B2, ADAM_EPS, ADAM_WD, ADAM_STEP = 0.001, 0.9, 0.999, 1e-08, 0.01, 10
NEG = -1e30
VMEM_LIMIT = 56 * 1024 * 1024


def _call(body, **kw):
    return pl.pallas_call(body, **kw)


def _params(*sem):
    return pltpu.CompilerParams(dimension_semantics=sem, vmem_limit_bytes=VMEM_LIMIT)


def _sigmoid(x):
    return 1.0 / (1.0 + jnp.exp(-x))


def _silu(x):
    return x * _sigmoid(x)


def _dsilu(x):
    s = _sigmoid(x)
    return s * (1.0 + x * (1.0 - s))


_GC = math.sqrt(2.0 / math.pi)


def _gelu(x):
    return 0.5 * x * (1.0 + jnp.tanh(_GC * (x + 0.044715 * x * x * x)))


def _dgelu(x):
    t = jnp.tanh(_GC * (x + 0.044715 * x * x * x))
    return 0.5 * (1.0 + t) + 0.5 * x * (1.0 - t * t) * _GC * (1.0 + 3.0 * 0.044715 * x * x)


def _expm1(x):
    series = x * (1.0 + x * (1.0 / 2) * (1.0 + x * (1.0 / 3) * (1.0 + x * (1.0 / 4) * (1.0 + x * (1.0 / 5) * (1.0 + x * (1.0 / 6))))))
    return jnp.where(jnp.abs(x) < 0.25, series, jnp.exp(x) - 1.0)


def _dot(a, b):
    return jnp.dot(a, b, preferred_element_type=F32)


def _dot_nt(a, b):
    return lax.dot_general(a, b, (((1,), (1,)), ((), ())), preferred_element_type=F32)


def _dot_tn(a, b):
    return lax.dot_general(a, b, (((0,), (0,)), ((), ())), preferred_element_type=F32)


def _colsum(v):
    return jnp.sum(v, axis=0, keepdims=True)


def _all_gather(x, name):
    def body(x_ref, o_ref, send_sems, recv_sems, local_sem):
        ix, iy, ic = lax.axis_index("x"), lax.axis_index("y"), lax.axis_index("c")
        me, sibling = (ix, iy, ic), (ix, iy, 1 - ic)
        chips = [(1 - ix, iy), (ix, 1 - iy), (1 - ix, 1 - iy)]

        def slot(px, py, pc):
            return o_ref.at[4 * px + 2 * py + pc]

        def copy(k, block, to, src=None):
            return pltpu.make_async_remote_copy(
                src_ref=slot(*block) if src is None else src, dst_ref=slot(*block), send_sem=send_sems.at[k],
                recv_sem=recv_sems.at[k], device_id=to, device_id_type=pl.DeviceIdType.MESH)

        mine = pltpu.make_async_copy(x_ref, slot(*me), local_sem)
        mine.start()
        first = [copy(0, me, sibling, src=x_ref)] + [copy(1 + j, me, (*chip, ic), src=x_ref) for j, chip in enumerate(chips)]
        for cp in first:
            cp.start()
        passed = [copy(4 + j, (*chip, ic), sibling) for j, chip in enumerate(chips)]
        for j, chip in enumerate(chips):
            copy(1 + j, (*chip, ic), me).wait_recv()
            passed[j].start()
        copy(0, sibling, me).wait_recv()
        for j, chip in enumerate(chips):
            copy(4 + j, (*chip, 1 - ic), me).wait_recv()
        for cp in first + passed:
            cp.wait_send()
        mine.wait()

    return _call(
        body, name=name,
        out_shape=jax.ShapeDtypeStruct((NDEV,) + x.shape, x.dtype),
        in_specs=[pl.BlockSpec(memory_space=pl.ANY)],
        out_specs=pl.BlockSpec(memory_space=pl.ANY),
        scratch_shapes=[pltpu.SemaphoreType.DMA((NDEV - 1,)), pltpu.SemaphoreType.DMA((NDEV - 1,)), pltpu.SemaphoreType.DMA],
    )(x)


def _pack(arrs):
    parts = []
    for a in arrs:
        n = math.prod(a.shape)
        parts.append(jnp.pad(a.reshape(-1).astype(F32), (0, _pack_rows(n) * 128 - n)).reshape(-1, 128))
    rows = sum(p.shape[0] for p in parts)
    total = rows if rows <= 512 else -(-rows // 512) * 512
    if total > rows:
        parts.append(jnp.zeros((total - rows, 128), F32))
    return jnp.concatenate(parts, axis=0)


def _pack_rows(n):
    return -(-n // 1024) * 8


def _unpack(packed, shapes):
    out, row = [], 0
    for s in shapes:
        n = math.prod(s)
        rows = _pack_rows(n)
        out.append(packed[row:row + rows].reshape(-1)[:n].reshape(s))
        row += rows
    return out


def _sum8(parts, name):
    _, R, C = parts.shape
    TR = R
    for cand in (512, 256, 128, 64, 32, 16, 8):
        if R % cand == 0:
            TR = cand
            break

    def body(p_ref, o_ref):
        acc = p_ref[0]
        for d in range(1, NDEV):
            acc = acc + p_ref[d]
        o_ref[...] = acc

    return _call(
        body, name=name, grid=(R // TR,),
        in_specs=[pl.BlockSpec((NDEV, TR, C), lambda i: (0, i, 0))],
        out_specs=pl.BlockSpec((TR, C), lambda i: (i, 0)),
        out_shape=jax.ShapeDtypeStruct((R, C), F32),
        compiler_params=_params("arbitrary"),
    )(parts)


def _adamw(g_layers, w, m, v, name):
    L, R, C = w.shape
    g0 = g_layers[0]
    nparts = g0.shape[0] if g0.ndim == 3 else 0
    TR = R
    for cand in (256, 128, 64, 32, 16):
        if R % cand == 0 and cand * C <= 128 * 1024:
            TR = cand
            break

    def body(*refs):
        g_refs = refs[:L]
        w_ref, m_ref, v_ref, go_ref, d_ref, mo_ref, vo_ref = refs[L:]
        for ll in range(L):
            @pl.when(pl.program_id(0) == ll)
            def _(g_ref=g_refs[ll]):
                if nparts:
                    g = g_ref[0].astype(F32)
                    for d in range(1, nparts):
                        g = g + g_ref[d].astype(F32)
                else:
                    g = g_ref[...]
                m2 = ADAM_B1 * m_ref[0] + (1.0 - ADAM_B1) * g
                v2 = ADAM_B2 * v_ref[0] + (1.0 - ADAM_B2) * (g * g)
                m_hat = m2 / (1.0 - ADAM_B1 ** ADAM_STEP)
                v_hat = v2 / (1.0 - ADAM_B2 ** ADAM_STEP)
                go_ref[0] = g
                d_ref[0] = -ADAM_LR * (m_hat / (jnp.sqrt(v_hat) + ADAM_EPS) + ADAM_WD * w_ref[0])
                mo_ref[0] = m2
                vo_ref[0] = v2

    def gspec(ll):
        if nparts:
            return pl.BlockSpec((nparts, TR, C), lambda l, i: (0, jnp.where(l == ll, i, 0), 0))
        return pl.BlockSpec((TR, C), lambda l, i: (jnp.where(l == ll, i, 0), 0))

    spec = pl.BlockSpec((1, TR, C), lambda l, i: (l, i, 0))
    return _call(
        body, name=name, grid=(L, R // TR),
        in_specs=[gspec(ll) for ll in range(L)] + [spec, spec, spec], out_specs=[spec] * 4,
        out_shape=[jax.ShapeDtypeStruct((L, R, C), F32)] * 4,
        compiler_params=_params("arbitrary", "arbitrary"),
    )(*g_layers, w, m, v)


def _ada_cols(c_all, w_ada, b_cols):
    L, D, NS = w_ada.shape

    def body(c_ref, w_ref, b_ref, o_ref):
        cond = _silu(c_ref[...]).astype(BF16)
        o_ref[0] = _dot(cond, w_ref[0].astype(BF16)) + b_ref[0]

    return _call(
        body, name="ada_cols", grid=(L,),
        in_specs=[pl.BlockSpec((NDEV, D), lambda l: (0, 0)), pl.BlockSpec((1, D, NS), lambda l: (l, 0, 0)),
                  pl.BlockSpec((1, 1, NS), lambda l: (l, 0, 0))],
        out_specs=pl.BlockSpec((1, NDEV, NS), lambda l: (l, 0, 0)),
        out_shape=jax.ShapeDtypeStruct((L, NDEV, NS), F32),
        compiler_params=_params("arbitrary"),
    )(c_all, w_ada, b_cols)


def _ada_grads(c_all, dada_all, dada_cols):
    _, D = c_all.shape
    L, _, NS = dada_cols.shape
    D3 = dada_all.shape[2]

    def body(c_ref, da_ref, dc_ref, gw_ref, gb_ref):
        cond = _silu(c_ref[...]).astype(BF16)
        gw_ref[0] = _dot_tn(cond, dc_ref[0].astype(BF16))
        acc = da_ref[0]
        for d in range(1, NDEV):
            acc = acc + da_ref[d]
        gb_ref[...] = acc

    return _call(
        body, name="ada_grads", grid=(L,),
        in_specs=[pl.BlockSpec((NDEV, D), lambda l: (0, 0)), pl.BlockSpec((NDEV, L, D3), lambda l: (0, 0, 0)),
                  pl.BlockSpec((1, NDEV, NS), lambda l: (l, 0, 0))],
        out_specs=[pl.BlockSpec((1, D, NS), lambda l: (l, 0, 0)), pl.BlockSpec((L, D3), lambda l: (0, 0))],
        out_shape=[jax.ShapeDtypeStruct((L, D, NS), F32), jax.ShapeDtypeStruct((L, D3), F32)],
        compiler_params=_params("arbitrary"),
    )(c_all, dada_all, dada_cols)


def _tile(S, want):
    return want if S % want == 0 else S


def _modulate(x, mod):
    S, D = x.shape
    TS = _tile(S, 512)

    def body(x_ref, mod_ref, h_ref):
        h_ref[...] = (x_ref[...] * mod_ref[0:1, :] + mod_ref[1:2, :]).astype(BF16)

    spec = pl.BlockSpec((TS, D), lambda i: (i, 0))
    return _call(
        body, name="modulate", grid=(S // TS,),
        in_specs=[spec, pl.BlockSpec((8, D), lambda i: (0, 0))], out_specs=spec,
        out_shape=jax.ShapeDtypeStruct((S, D), BF16),
        compiler_params=_params("arbitrary"),
    )(x, mod)


def _peer(k):
    ix, iy, ic = lax.axis_index("x"), lax.axis_index("y"), lax.axis_index("c")
    bx, by, bc = (k >> 2) & 1, (k >> 1) & 1, k & 1
    px, py, pc = ix + bx - 2 * ix * bx, iy + by - 2 * iy * by, ic + bc - 2 * ic * bc
    return (px, py, pc), 4 * px + 2 * py + pc


GATHER_ORDER = (0, 1, 2, 4, 6, 3, 5, 7)
GATHER_DIRECT = (1, 2, 4, 6)
SCATTER_ORDER = (3, 2, 5, 4, 7, 6, 1, 0)


def _offset(order, k):
    off = jnp.int32(order[-1])
    for step in reversed(range(len(order) - 1)):
        off = jnp.where(k == step, jnp.int32(order[step]), off)
    return off


def _proj_gather(h, w_shard, extras, me1):
    assert GATHER_ORDER[0] == 0
    S, D = h.shape
    NS = w_shard.shape[1]
    TS = _tile(S, 2048)
    nt = S // TS
    ne = len(extras)

    def body(me_ref, h_ref, w_ref, *rest):
        x_refs, o_ref, wg_ref = rest[:ne], rest[ne], rest[ne + 1]
        xg_refs = rest[ne + 2:2 * ne + 2]
        wbuf, send_sems, recv_sems, local_sems, load_sems = rest[2 * ne + 2:]
        k, i = pl.program_id(0), pl.program_id(1)
        me = me_ref[0]
        off = _offset(GATHER_ORDER, k)

        def push(src, dst, which, kk):
            peer, _ = _peer(kk)
            return pltpu.make_async_remote_copy(
                src_ref=src, dst_ref=dst.at[me], send_sem=send_sems.at[which, kk], recv_sem=recv_sems.at[which, kk],
                device_id=peer, device_id_type=pl.DeviceIdType.MESH)

        def own(src, dst, which):
            return pltpu.make_async_copy(src, dst.at[me], local_sems.at[which])

        pairs = [(w_ref, wg_ref)] + list(zip(x_refs, xg_refs))

        @pl.when(jnp.logical_and(k == 0, i == 0))
        def _():
            for which, (src, dst) in enumerate(pairs):
                own(src, dst, which).start()
                for kk in (GATHER_DIRECT if which == 0 else GATHER_ORDER[1:]):
                    push(src, dst, which, kk).start()
            first = pltpu.make_async_copy(w_ref, wbuf.at[0], load_sems.at[0])
            first.start()
            first.wait()

        @pl.when(jnp.logical_and(k > 0, i == 0))
        def _():
            push(w_ref, wg_ref, 0, off).wait_recv()
            _, blk = _peer(off)
            for step, o in enumerate(GATHER_ORDER):
                if o in GATHER_DIRECT and o > 1:
                    @pl.when(k == step)
                    def _(o=o):
                        sibling, _ = _peer(1)
                        pltpu.make_async_remote_copy(
                            src_ref=wg_ref.at[blk], dst_ref=wg_ref.at[blk], send_sem=send_sems.at[0, o + 1],
                            recv_sem=recv_sems.at[0, o + 1], device_id=sibling, device_id_type=pl.DeviceIdType.MESH).start()
            load = pltpu.make_async_copy(wg_ref.at[blk], wbuf.at[k % 2], load_sems.at[k % 2])
            load.start()
            load.wait()

        o_ref[...] = _dot(h_ref[...], wbuf[k % 2])

        @pl.when(jnp.logical_and(k == NDEV - 1, i == nt - 1))
        def _():
            for which, (src, dst) in enumerate(pairs):
                for kk in range(1, NDEV):
                    cp = push(src, dst, which, kk)
                    cp.wait_send()
                    if which > 0:
                        cp.wait_recv()
                own(src, dst, which).wait()

    def col(k, i, me_ref):
        m, off = me_ref[0], _offset(GATHER_ORDER, k)
        return i, (m | off) - (m & off)

    anyspec = pl.BlockSpec(memory_space=pl.ANY)
    grid_spec = pltpu.PrefetchScalarGridSpec(
        num_scalar_prefetch=1, grid=(NDEV, nt),
        in_specs=[pl.BlockSpec((TS, D), lambda k, i, me_ref: (i, 0)), anyspec] + [anyspec] * ne,
        out_specs=[pl.BlockSpec((TS, NS), col), anyspec] + [anyspec] * ne,
        scratch_shapes=[pltpu.VMEM((2, D, NS), BF16), pltpu.SemaphoreType.DMA((1 + ne, NDEV)), pltpu.SemaphoreType.DMA((1 + ne, NDEV)),
                        pltpu.SemaphoreType.DMA((1 + ne,)), pltpu.SemaphoreType.DMA((2,))])
    return _call(
        body, name="proj_gather", grid_spec=grid_spec,
        out_shape=[jax.ShapeDtypeStruct((S, NDEV * NS), F32), jax.ShapeDtypeStruct((NDEV, D, NS), BF16)]
        + [jax.ShapeDtypeStruct((NDEV,) + e.shape, e.dtype) for e in extras],
        compiler_params=_params("arbitrary", "arbitrary"),
    )(me1, h, w_shard, *extras)


def _dw_scatter(a, b, blk, by_rows, me1, name):
    S = a.shape[0]
    TK = _tile(S, 1024)
    nk = S // TK
    shape = (blk, b.shape[1]) if by_rows else (a.shape[1], blk)
    last = NDEV - 1
    assert all(o & 1 for o in SCATTER_ORDER[0::2]) and not any(o & 1 for o in SCATTER_ORDER[1::2]) and SCATTER_ORDER[-1] == 0
    assert all(SCATTER_ORDER[s] == SCATTER_ORDER[s + 1] + 1 for s in range(0, NDEV, 2))

    def body(me_ref, a_ref, b_ref, recv_ref, acc, stage, sib, send_sems, recv_sems, local_sem):
        k, kk = pl.program_id(0), pl.program_id(1)

        def chip_of(step):
            return _offset(SCATTER_ORDER, step) >> 1

        def to_sibling(j, slot):
            sibling, _ = _peer(1)
            return pltpu.make_async_remote_copy(
                src_ref=stage.at[slot], dst_ref=sib.at[j], send_sem=send_sems.at[0, j], recv_sem=recv_sems.at[0, j],
                device_id=sibling, device_id_type=pl.DeviceIdType.MESH)

        def to_chip(j, slot):
            peer, _ = _peer(2 * j)
            return pltpu.make_async_remote_copy(
                src_ref=stage.at[slot], dst_ref=recv_ref.at[j], send_sem=send_sems.at[1, j], recv_sem=recv_sems.at[1, j],
                device_id=peer, device_id_type=pl.DeviceIdType.MESH)

        own = pltpu.make_async_copy(stage.at[last % 2], recv_ref.at[0], local_sem)

        @pl.when(kk == 0)
        def _():
            acc[...] = jnp.zeros_like(acc)
        acc[...] += _dot_tn(a_ref[...], b_ref[...])

        @pl.when(kk == nk - 1)
        def _():
            j = chip_of(k)
            even_step = k % 2 == 0

            @pl.when(jnp.logical_and(k >= 2, even_step))
            def _():
                to_sibling(chip_of(k - 2), k % 2).wait_send()

            @pl.when(jnp.logical_and(k >= 3, jnp.logical_not(even_step)))
            def _():
                to_chip(chip_of(k - 2), k % 2).wait_send()

            @pl.when(even_step)
            def _():
                stage[k % 2] = acc[...].astype(BF16)
                to_sibling(j, k % 2).start()

            @pl.when(jnp.logical_not(even_step))
            def _():
                to_sibling(j, 0).wait_recv()
                stage[k % 2] = (acc[...] + sib[j].astype(F32)).astype(BF16)

                @pl.when(k < last)
                def _():
                    to_chip(j, k % 2).start()

                @pl.when(k == last)
                def _():
                    own.start()
                    to_sibling(SCATTER_ORDER[last - 1] >> 1, (last - 1) % 2).wait_send()
                    own.wait()
                    for jj in range(1, NDEV // 2):
                        to_chip(jj, 0).wait_recv()

    def blk_of(k, me_ref):
        m, off = me_ref[0], _offset(SCATTER_ORDER, k)
        return (m | off) - (m & off)

    if by_rows:
        in_specs = [pl.BlockSpec((TK, blk), lambda k, kk, me_ref: (kk, blk_of(k, me_ref))),
                    pl.BlockSpec((TK, b.shape[1]), lambda k, kk, me_ref: (kk, 0))]
    else:
        in_specs = [pl.BlockSpec((TK, a.shape[1]), lambda k, kk, me_ref: (kk, 0)),
                    pl.BlockSpec((TK, blk), lambda k, kk, me_ref: (kk, blk_of(k, me_ref)))]
    grid_spec = pltpu.PrefetchScalarGridSpec(
        num_scalar_prefetch=1, grid=(NDEV, nk), in_specs=in_specs,
        out_specs=pl.BlockSpec(memory_space=pl.ANY),
        scratch_shapes=[pltpu.VMEM(shape, F32), pltpu.VMEM((2,) + shape, BF16), pltpu.VMEM((NDEV // 2,) + shape, BF16),
                        pltpu.SemaphoreType.DMA((2, NDEV // 2)), pltpu.SemaphoreType.DMA((2, NDEV // 2)), pltpu.SemaphoreType.DMA])
    return _call(
        body, name=name, grid_spec=grid_spec,
        out_shape=jax.ShapeDtypeStruct((NDEV // 2,) + shape, BF16),
        compiler_params=_params("arbitrary", "arbitrary"),
    )(me1, a, b)


def _chunk(TS, BR, k):
    return pl.BlockSpec((TS, BR), lambda i: (i, k))


def _halo_prev(TS, BR, k):
    return pl.BlockSpec((8, BR), lambda i: (jnp.maximum(i * (TS // 8) - 1, 0), k))


def _conv_a_fwd(proj, w8, BR):
    S = proj.shape[0]
    TS = _tile(S, 512)

    def body(ab_ref, ac_ref, ax_ref, ag_ref, hc_ref, hx_ref, w_ref, o_ref, ext):
        i = pl.program_id(0)
        u = ac_ref[...] * ax_ref[...]
        ext[0:8, :] = jnp.where(i > 0, hc_ref[...] * hx_ref[...], 0.0)
        ext[8:8 + TS, :] = u
        conv = ext[pl.ds(6, TS), :] * w_ref[0:1, :] + ext[pl.ds(7, TS), :] * w_ref[1:2, :] + u * w_ref[2:3, :]
        o_ref[...] = ab_ref[...] * conv * _silu(ag_ref[...])

    return _call(
        body, name="conv_a_fwd", grid=(S // TS,),
        in_specs=[_chunk(TS, BR, 0), _chunk(TS, BR, 1), _chunk(TS, BR, 2), _chunk(TS, BR, 3),
                  _halo_prev(TS, BR, 1), _halo_prev(TS, BR, 2), pl.BlockSpec((8, BR), lambda i: (0, 0))],
        out_specs=pl.BlockSpec((TS, BR), lambda i: (i, 0)),
        out_shape=jax.ShapeDtypeStruct((S, BR), F32),
        scratch_shapes=[pltpu.VMEM((TS + 8, BR), F32)],
        compiler_params=_params("arbitrary"),
    )(proj, proj, proj, proj, proj, proj, w8)


ATT_UNIT = 2048
ATT_LANES = 128


def _attn_mask():
    i = lax.broadcasted_iota(jnp.int32, (BLK, 2 * BLK), 0)
    j = lax.broadcasted_iota(jnp.int32, (BLK, 2 * BLK), 1)
    return jnp.logical_and(j >= i, j <= i + BLK), j >= BLK


def _attn_rows(n, r, dil):
    return pl.ds(n * BLK * dil + r, BLK, stride=dil) if dil > 1 else pl.ds(n * BLK, BLK)


def _attn_geometry(S, BR, dil):
    HD = BR // 8
    U = ATT_UNIT
    assert S % U == 0 and BR % ATT_LANES == 0 and ATT_LANES % HD == 0 and U % (BLK * dil) == 0
    return HD, U, S // U, U // (BLK * dil), ATT_LANES // HD, BR // ATT_LANES


def _attn_fwd(proj, bias, dil, BR):
    S = proj.shape[0]
    HD, U, NU, nbu, hpb, HBK = _attn_geometry(S, BR, dil)
    scale = HD ** -0.5
    combos = [(n, r) for n in range(nbu) for r in range(dil)]

    def body(q_ref, kc_ref, kp_ref, vc_ref, vp_ref, b_ref, o_ref, l_ref, sbuf, pbuf):
        m = pl.program_id(1)
        band, is_cur = _attn_mask()
        first = jnp.logical_and(band, jnp.logical_or(is_cur, m > 0))

        def keys(ref_c, ref_p, n, r):
            prev = ref_c[_attn_rows(n - 1, r, dil), :] if n > 0 else ref_p[_attn_rows(nbu - 1, r, dil), :]
            return jnp.concatenate([prev, ref_c[_attn_rows(n, r, dil), :]], axis=0).astype(BF16)

        for c, (n, r) in enumerate(combos):
            q = (q_ref[_attn_rows(n, r, dil), :] * scale).astype(BF16)
            kk = keys(kc_ref, kp_ref, n, r)
            for h in range(hpb):
                sl = slice(h * HD, (h + 1) * HD)
                sbuf[c * hpb + h] = _dot_nt(q[:, sl], kk[:, sl])
        for c, (n, r) in enumerate(combos):
            lses = []
            for h in range(hpb):
                s = jnp.where(band if n > 0 else first, sbuf[c * hpb + h] + b_ref[h], NEG)
                mx = jnp.max(s, axis=-1, keepdims=True)
                p = jnp.exp(s - mx)
                l = jnp.sum(p, axis=-1, keepdims=True)
                pbuf[c * hpb + h] = (p * (1.0 / l)).astype(BF16)
                lses.append(jnp.broadcast_to(mx + jnp.log(l), (BLK, HD)))
            l_ref[_attn_rows(n, r, dil), :] = jnp.concatenate(lses, axis=1)
        for c, (n, r) in enumerate(combos):
            vv = keys(vc_ref, vp_ref, n, r)
            o_ref[_attn_rows(n, r, dil), :] = jnp.concatenate(
                [_dot(pbuf[c * hpb + h], vv[:, h * HD:(h + 1) * HD]) for h in range(hpb)], axis=1)

    def cur(c):
        return pl.BlockSpec((U, ATT_LANES), lambda hb, m: (m, c * HBK + hb))

    def prev(c):
        return pl.BlockSpec((U, ATT_LANES), lambda hb, m: (jnp.maximum(m - 1, 0), c * HBK + hb))

    ospec = pl.BlockSpec((U, ATT_LANES), lambda hb, m: (m, hb))
    nhb = len(combos) * hpb
    return _call(
        body, name="attn_fwd_d%d" % dil, grid=(HBK, NU),
        in_specs=[cur(4), cur(5), prev(5), cur(6), prev(6), pl.BlockSpec((hpb, BLK, 2 * BLK), lambda hb, m: (hb, 0, 0))],
        out_specs=[ospec, ospec],
        out_shape=[jax.ShapeDtypeStruct((S, BR), F32)] * 2,
        scratch_shapes=[pltpu.VMEM((nhb, BLK, 2 * BLK), F32), pltpu.VMEM((nhb, BLK, 2 * BLK), BF16)],
        compiler_params=_params("arbitrary", "arbitrary"),
    )(proj, proj, proj, proj, proj, bias)


def _mix_weights(l1, l2, l3):
    m = jnp.maximum(jnp.maximum(l1, l2), l3)
    e1, e2, e3 = jnp.exp(l1 - m), jnp.exp(l2 - m), jnp.exp(l3 - m)
    inv = 1.0 / (e1 + e2 + e3)
    return e1 * inv, e2 * inv, e3 * inv


def _attn_mix_fwd(proj, os_, ls_, BR):
    S = proj.shape[0]
    TS = _tile(S, 512)

    def body(o1, o2, o3, l1, l2, l3, g_ref, y_ref):
        w1, w2, w3 = _mix_weights(l1[...], l2[...], l3[...])
        y_ref[...] = (w1 * o1[...] + w2 * o2[...] + w3 * o3[...]) * _silu(g_ref[...])

    spec = pl.BlockSpec((TS, BR), lambda i: (i, 0))
    return _call(
        body, name="attn_mix_fwd", grid=(S // TS,),
        in_specs=[spec] * 6 + [_chunk(TS, BR, 7)], out_specs=spec,
        out_shape=jax.ShapeDtypeStruct((S, BR), F32),
        compiler_params=_params("arbitrary"),
    )(*os_, *ls_, proj)


def _lru_gates(xc, wa, wx, ba, bx, sp):
    xb = xc.astype(BF16)
    r = _sigmoid(_dot(xb, wa) + ba)
    ig = _sigmoid(_dot(xb, wx) + bx)
    la = -LRU_C * r * sp
    a = jnp.exp(la)
    mult = jnp.sqrt(-_expm1(2.0 * la))
    return r, ig, a, mult


def _lru_fwd(proj, rows, wa, wx, BR):
    S = proj.shape[0]
    TS = _tile(S, 256)
    PAD = TS // 2

    def body(cx_ref, hx_ref, cg_ref, r_ref, wa_ref, wx_ref, y_ref, h_ref, ext, sa, sb, carry):
        i = pl.program_id(0)

        @pl.when(i == 0)
        def _():
            sa[0:PAD, :] = jnp.ones((PAD, BR), F32)
            sb[0:PAD, :] = jnp.zeros((PAD, BR), F32)
            carry[...] = jnp.zeros_like(carry)

        cx = cx_ref[...]
        ext[0:8, :] = jnp.where(i > 0, hx_ref[...], 0.0)
        ext[8:8 + TS, :] = cx
        xc = (ext[pl.ds(5, TS), :] * r_ref[0:1, :] + ext[pl.ds(6, TS), :] * r_ref[1:2, :]
              + ext[pl.ds(7, TS), :] * r_ref[2:3, :] + cx * r_ref[3:4, :] + r_ref[4:5, :])
        _, ig, a, mult = _lru_gates(xc, wa_ref[...], wx_ref[...], r_ref[5:6, :], r_ref[6:7, :], r_ref[7:8, :])
        sa[PAD:PAD + TS, :] = a
        sb[PAD:PAD + TS, :] = mult * ig * xc
        d = 1
        while d < TS:
            A, B = sa[PAD:PAD + TS, :], sb[PAD:PAD + TS, :]
            As, Bs = sa[pl.ds(PAD - d, TS), :], sb[pl.ds(PAD - d, TS), :]
            sb[PAD:PAD + TS, :] = A * Bs + B
            sa[PAD:PAD + TS, :] = A * As
            d *= 2
        h = sb[PAD:PAD + TS, :] + sa[PAD:PAD + TS, :] * carry[0:1, :]
        carry[0:1, :] = h[TS - 1:TS, :]
        h_ref[...] = h
        y_ref[...] = h * _silu(cg_ref[...])

    full = pl.BlockSpec((BR, BR), lambda i: (0, 0))
    spec = pl.BlockSpec((TS, BR), lambda i: (i, 0))
    return _call(
        body, name="lru_fwd", grid=(S // TS,),
        in_specs=[_chunk(TS, BR, 8), _halo_prev(TS, BR, 8), _chunk(TS, BR, 9), pl.BlockSpec((8, BR), lambda i: (0, 0)), full, full],
        out_specs=[spec, spec],
        out_shape=[jax.ShapeDtypeStruct((S, BR), F32)] * 2,
        scratch_shapes=[pltpu.VMEM((TS + 8, BR), F32), pltpu.VMEM((PAD + TS, BR), F32), pltpu.VMEM((PAD + TS, BR), F32),
                        pltpu.VMEM((8, BR), F32)],
        compiler_params=_params("arbitrary"),
    )(proj, proj, proj, rows, wa, wx)


S5_LAGS = 8
S5_CB = 8 * S5_CH
S5_SB = 8 * S5_STATE


def _s5_fwd(proj, a8, bbl, cre, cim, rows, wg, BR):
    S = proj.shape[0]
    W = a8.shape[1]
    TS = _tile(S, 512)
    CB, SB, J = S5_CB, S5_SB, S5_LAGS
    nblk = BR // CB

    def body(u_ref, uh_ref, dg_ref, a8_ref, bbl_ref, cre_ref, cim_ref, r_ref, wg_ref,
             y_ref, xre_ref, xim_ref, uext, ulag, sre, sim, ypre_s, carry):
        i = pl.program_id(0)

        @pl.when(i == 0)
        def _():
            carry[...] = jnp.zeros_like(carry)

        uext[0:8, :] = jnp.where(i > 0, uh_ref[...], 0.0)
        uext[8:8 + TS, :] = u_ref[...]
        for m in range(nblk):
            cs, ws = slice(m * CB, (m + 1) * CB), slice(m * SB, (m + 1) * SB)
            for j in range(J):
                ulag[:, j * CB:(j + 1) * CB] = uext[pl.ds(8 - j, TS), cs].astype(BF16)
            w = _dot(ulag[...], bbl_ref[m])
            sre[...] = w[:, 0:SB]
            sim[...] = w[:, SB:2 * SB]
            ar = jnp.broadcast_to(a8_ref[0:1, ws], (8, SB))
            ai = jnp.broadcast_to(a8_ref[1:2, ws], (8, SB))
            xr, xi = carry[0:8, ws], carry[8:16, ws]
            for g in range(TS // 8):
                rg = slice(8 * g, 8 * g + 8)
                xr, xi = sre[rg, :] + ar * xr - ai * xi, sim[rg, :] + ar * xi + ai * xr
                sre[rg, :] = xr
                sim[rg, :] = xi
            carry[0:8, ws] = xr
            carry[8:16, ws] = xi
            xre, xim = sre[...], sim[...]
            xre_ref[:, ws] = xre
            xim_ref[:, ws] = xim
            ypre_s[:, cs] = _dot(xre.astype(BF16), cre_ref[ws, :]) - _dot(xim.astype(BF16), cim_ref[ws, :])
        dg = dg_ref[...]
        yg = _gelu(ypre_s[...] + r_ref[0:1, :] * u_ref[...])
        s = _sigmoid(_dot(yg.astype(BF16), wg_ref[...]) + r_ref[1:2, :])
        y_ref[...] = yg * s * _silu(dg)

    def const(shape):
        return pl.BlockSpec(shape, lambda i: (0,) * len(shape))

    return _call(
        body, name="s5_fwd", grid=(S // TS,),
        in_specs=[_chunk(TS, BR, 10), _halo_prev(TS, BR, 10), _chunk(TS, BR, 11), const((8, W)), const((nblk, J * CB, 2 * SB)),
                  const((W, CB)), const((W, CB)), const((8, BR)), const((BR, BR))],
        out_specs=[pl.BlockSpec((TS, BR), lambda i: (i, 0)), pl.BlockSpec((TS, W), lambda i: (i, 0)), pl.BlockSpec((TS, W), lambda i: (i, 0))],
        out_shape=[jax.ShapeDtypeStruct((S, BR), F32), jax.ShapeDtypeStruct((S, W), F32), jax.ShapeDtypeStruct((S, W), F32)],
        scratch_shapes=[pltpu.VMEM((TS + 8, BR), F32), pltpu.VMEM((TS, J * CB), BF16), pltpu.VMEM((TS, SB), F32), pltpu.VMEM((TS, SB), F32),
                        pltpu.VMEM((TS, BR), F32), pltpu.VMEM((16, W), F32)],
        compiler_params=_params("arbitrary"),
    )(proj, proj, proj, a8, bbl, cre, cim, rows, wg)


def _out_ln(ys, w_out, x, rows, target=None):
    S, D = x.shape
    BR = D // 4
    TS = _tile(S, 256)
    last = target is not None

    def body(ya, yb, yc, yd, w_ref, x_ref, r_ref, *rest):
        t_ref = rest[0] if last else None
        o_ref, y_ref, cat_ref = rest[1:4] if last else rest[0:3]
        cat = jnp.concatenate([ya[...], yb[...], yc[...], yd[...]], axis=1).astype(BF16)
        y = _dot(cat, w_ref[...])
        z = ALPHA * x_ref[...] + r_ref[0:1, :] * y
        mu = jnp.mean(z, axis=-1, keepdims=True)
        zc = z - mu
        var = jnp.mean(zc * zc, axis=-1, keepdims=True)
        xn = zc * lax.rsqrt(var + LN_EPS) * r_ref[1:2, :] + r_ref[2:3, :]
        y_ref[...] = y
        cat_ref[...] = cat
        if last:
            l_ref = rest[4]

            @pl.when(pl.program_id(0) == 0)
            def _():
                l_ref[...] = jnp.zeros_like(l_ref)
            diff = xn - t_ref[...]
            o_ref[...] = diff * (1.0 / D)
            l_ref[...] += jnp.sum(_colsum(diff * diff), axis=1, keepdims=True) * (0.5 / D)
        else:
            o_ref[...] = xn

    yspec = pl.BlockSpec((TS, BR), lambda i: (i, 0))
    spec = pl.BlockSpec((TS, D), lambda i: (i, 0))
    lspec = pl.BlockSpec((1, 1), lambda i: (0, 0))
    return _call(
        body, name="out_ln_loss" if last else "out_ln", grid=(S // TS,),
        in_specs=[yspec] * 4 + [pl.BlockSpec((D, D), lambda i: (0, 0)), spec, pl.BlockSpec((8, D), lambda i: (0, 0))] + [spec] * last,
        out_specs=[spec, spec, spec] + [lspec] * last,
        out_shape=[jax.ShapeDtypeStruct((S, D), F32), jax.ShapeDtypeStruct((S, D), F32), jax.ShapeDtypeStruct((S, D), BF16)]
        + [jax.ShapeDtypeStruct((1, 1), F32)] * last,
        compiler_params=_params("arbitrary"),
    )(*ys, w_out, x, rows, *([target] if last else []))


def _ln_bwd(dout, x, y, rows, w_out):
    S, D = x.shape
    TS = _tile(S, 256)

    def body(do_ref, x_ref, y_ref, r_ref, w_ref, dxr_ref, dyb_ref, dcat_ref, acc_ref):
        @pl.when(pl.program_id(0) == 0)
        def _():
            acc_ref[...] = jnp.zeros_like(acc_ref)
        g1, lg = r_ref[0:1, :], r_ref[1:2, :]
        yv = y_ref[...]
        z = ALPHA * x_ref[...] + g1 * yv
        mu = jnp.mean(z, axis=-1, keepdims=True)
        zc = z - mu
        var = jnp.mean(zc * zc, axis=-1, keepdims=True)
        rstd = lax.rsqrt(var + LN_EPS)
        xhat = zc * rstd
        do = do_ref[...]
        dxh = do * lg
        dz = rstd * (dxh - jnp.mean(dxh, axis=-1, keepdims=True) - xhat * jnp.mean(dxh * xhat, axis=-1, keepdims=True))
        dxr_ref[...] = ALPHA * dz
        dyb = (g1 * dz).astype(BF16)
        dyb_ref[...] = dyb
        dcat_ref[...] = _dot_nt(dyb, w_ref[...])
        acc_ref[0:1, :] += _colsum(do * xhat)
        acc_ref[1:2, :] += _colsum(do)
        acc_ref[2:3, :] += _colsum(dz * yv)

    spec = pl.BlockSpec((TS, D), lambda i: (i, 0))
    rspec = pl.BlockSpec((8, D), lambda i: (0, 0))
    return _call(
        body, name="ln_bwd", grid=(S // TS,),
        in_specs=[spec, spec, spec, rspec, pl.BlockSpec((D, D), lambda i: (0, 0))],
        out_specs=[spec, spec, spec, rspec],
        out_shape=[jax.ShapeDtypeStruct((S, D), F32), jax.ShapeDtypeStruct((S, D), BF16), jax.ShapeDtypeStruct((S, D), F32),
                   jax.ShapeDtypeStruct((8, D), F32)],
        compiler_params=_params("arbitrary"),
    )(dout, x, y, rows, w_out)


def _dh(dproj, wg, dxres, x, mod, sides):
    S, D = x.shape
    NS = wg.shape[2]
    TS = _tile(S, 256)
    nt = S // TS
    ns = len(sides)

    def body(dp_ref, w_hbm, dxr_ref, x_ref, mod_ref, *rest):
        s_refs, dx_ref, sum_ref, so_refs = rest[:ns], rest[ns], rest[ns + 1], rest[ns + 2:2 * ns + 2]
        w_vmem, sem, send_sems, recv_sems, local_sems = rest[2 * ns + 2:]
        i = pl.program_id(0)
        ix, iy, ic = lax.axis_index("x"), lax.axis_index("y"), lax.axis_index("c")
        me = 4 * ix + 2 * iy + ic

        def exchange(which):
            (_, gather), x_side, o_side = sides[which], s_refs[which], so_refs[which]
            copies = [pltpu.make_async_copy(x_side if gather else x_side.at[me], o_side.at[me], local_sems.at[which])]
            for k in range(1, NDEV):
                px = 1 - ix if k & 4 else ix
                py = 1 - iy if k & 2 else iy
                pc = 1 - ic if k & 1 else ic
                copies.append(pltpu.make_async_remote_copy(
                    src_ref=x_side if gather else x_side.at[4 * px + 2 * py + pc], dst_ref=o_side.at[me],
                    send_sem=send_sems.at[which, k - 1], recv_sem=recv_sems.at[which, k - 1],
                    device_id=(px, py, pc), device_id_type=pl.DeviceIdType.MESH))
            return copies

        @pl.when(i == 0)
        def _():
            load = pltpu.make_async_copy(w_hbm, w_vmem, sem)
            load.start()
            for which in range(ns):
                for cp in exchange(which):
                    cp.start()
            sum_ref[...] = jnp.zeros_like(sum_ref)
            load.wait()

        dh = _dot_nt(dp_ref[:, 0:NS], w_vmem[0])
        for j in range(1, NDEV):
            dh = dh + _dot_nt(dp_ref[:, j * NS:(j + 1) * NS], w_vmem[j])
        dx_ref[...] = dxr_ref[...] + dh * mod_ref[0:1, :]
        sum_ref[0:1, :] += _colsum(dh * x_ref[...])
        sum_ref[1:2, :] += _colsum(dh)

        @pl.when(i == nt - 1)
        def _():
            for which in range(ns):
                for cp in exchange(which):
                    cp.wait()

    spec = pl.BlockSpec((TS, D), lambda i: (i, 0))
    rspec = pl.BlockSpec((8, D), lambda i: (0, 0))
    anyspec = pl.BlockSpec(memory_space=pl.ANY)
    side_shapes = [jax.ShapeDtypeStruct(((NDEV,) + a.shape) if g else a.shape, a.dtype) for a, g in sides]
    return _call(
        body, name="dh", grid=(nt,),
        in_specs=[pl.BlockSpec((TS, NDEV * NS), lambda i: (i, 0)), anyspec, spec, spec, rspec] + [anyspec] * ns,
        out_specs=[spec, rspec] + [anyspec] * ns,
        out_shape=[jax.ShapeDtypeStruct((S, D), F32), jax.ShapeDtypeStruct((8, D), F32)] + side_shapes,
        scratch_shapes=[pltpu.VMEM((NDEV, D, NS), BF16), pltpu.SemaphoreType.DMA, pltpu.SemaphoreType.DMA((max(ns, 1), NDEV - 1)),
                        pltpu.SemaphoreType.DMA((max(ns, 1), NDEV - 1)), pltpu.SemaphoreType.DMA((max(ns, 1),))],
        compiler_params=_params("arbitrary"),
    )(dproj, wg, dxres, x, mod, *[a for a, _ in sides])


def _halo_next(S, TS, BR, k):
    last = S // 8 - 1
    return pl.BlockSpec((8, BR), lambda i: (jnp.minimum((i + 1) * (TS // 8), last), k))


def _conv_a_bwd(proj, dcat, w8, BR):
    S = proj.shape[0]
    TS = _tile(S, 512)
    nt = S // TS

    def body(ab_ref, ac_ref, ax_ref, ag_ref, dy_ref, hc_ref, hx_ref, nab_ref, nag_ref, ndy_ref, w_ref, dp_ref, dw_ref, ext, dext):
        i = pl.program_id(0)

        @pl.when(i == 0)
        def _():
            dw_ref[...] = jnp.zeros_like(dw_ref)

        w0, w1, w2 = w_ref[0:1, :], w_ref[1:2, :], w_ref[2:3, :]
        ab, ac, ax, ag, dy = ab_ref[...], ac_ref[...], ax_ref[...], ag_ref[...], dy_ref[...]
        u = ac * ax
        ext[0:8, :] = jnp.where(i > 0, hc_ref[...] * hx_ref[...], 0.0)
        ext[8:8 + TS, :] = u
        u1, u2 = ext[pl.ds(7, TS), :], ext[pl.ds(6, TS), :]
        conv = u2 * w0 + u1 * w1 + u * w2
        sg = _silu(ag)
        dconv = dy * ab * sg
        dext[0:TS, :] = dconv
        dext[TS:TS + 8, :] = jnp.where(i < nt - 1, ndy_ref[...] * nab_ref[...] * _silu(nag_ref[...]), 0.0)
        du = w2 * dconv + w1 * dext[pl.ds(1, TS), :] + w0 * dext[pl.ds(2, TS), :]
        dp_ref[:, 0:BR] = (dy * conv * sg).astype(BF16)
        dp_ref[:, BR:2 * BR] = (du * ax).astype(BF16)
        dp_ref[:, 2 * BR:3 * BR] = (du * ac).astype(BF16)
        dp_ref[:, 3 * BR:4 * BR] = (dy * ab * conv * _dsilu(ag)).astype(BF16)
        dw_ref[0:1, :] += _colsum(dconv * u2)
        dw_ref[1:2, :] += _colsum(dconv * u1)
        dw_ref[2:3, :] += _colsum(dconv * u)

    rspec = pl.BlockSpec((8, BR), lambda i: (0, 0))
    return _call(
        body, name="conv_a_bwd", grid=(nt,),
        in_specs=[_chunk(TS, BR, 0), _chunk(TS, BR, 1), _chunk(TS, BR, 2), _chunk(TS, BR, 3), _chunk(TS, BR, 0),
                  _halo_prev(TS, BR, 1), _halo_prev(TS, BR, 2), _halo_next(S, TS, BR, 0), _halo_next(S, TS, BR, 3),
                  _halo_next(S, TS, BR, 0), rspec],
        out_specs=[pl.BlockSpec((TS, 4 * BR), lambda i: (i, 0)), rspec],
        out_shape=[jax.ShapeDtypeStruct((S, 12 * BR), BF16), jax.ShapeDtypeStruct((8, BR), F32)],
        scratch_shapes=[pltpu.VMEM((TS + 8, BR), F32), pltpu.VMEM((TS + 8, BR), F32)],
        compiler_params=_params("arbitrary"),
    )(proj, proj, proj, proj, dcat, proj, proj, proj, proj, dcat, w8)


def _attn_mix_bwd(proj, dcat, os_, ls_, head_ones, BR):
    S = proj.shape[0]
    TS = _tile(S, 512)

    def body(dy_ref, g_ref, o1, o2, o3, l1, l2, l3, ones_ref, d1, d2, d3, e1, e2, e3, dg_ref):
        w1, w2, w3 = _mix_weights(l1[...], l2[...], l3[...])
        mix = w1 * o1[...] + w2 * o2[...] + w3 * o3[...]
        g = g_ref[...]
        dy = dy_ref[...]
        dmix = dy * _silu(g)
        dg_ref[...] = dy * mix * _dsilu(g)
        t = jnp.dot(dmix * mix, ones_ref[...], preferred_element_type=F32, precision=lax.Precision.HIGHEST)
        d1[...] = w1 * dmix
        d2[...] = w2 * dmix
        d3[...] = w3 * dmix
        e1[...] = -w1 * t
        e2[...] = -w2 * t
        e3[...] = -w3 * t

    spec = pl.BlockSpec((TS, BR), lambda i: (i, 0))
    outs = _call(
        body, name="attn_mix_bwd", grid=(S // TS,),
        in_specs=[_chunk(TS, BR, 1), _chunk(TS, BR, 7)] + [spec] * 6 + [pl.BlockSpec((BR, BR), lambda i: (0, 0))],
        out_specs=[spec] * 7,
        out_shape=[jax.ShapeDtypeStruct((S, BR), F32)] * 7,
        compiler_params=_params("arbitrary"),
    )(dcat, proj, *os_, *ls_, head_ones)
    return outs[0:3], outs[3:6], outs[6]


def _attn_bwd(proj, do, e, lse, bias, dil, BR):
    S = proj.shape[0]
    HD, U, NU, nbu, hpb, HBK = _attn_geometry(S, BR, dil)
    scale = HD ** -0.5
    combos = [(n, r) for n in range(nbu) for r in range(dil)]

    def body(q_ref, kc_ref, kp_ref, vc_ref, vp_ref, do_ref, e_ref, l_ref, b_ref, dq_ref, dk_ref, dv_ref, db_ref,
             sbuf, dpbuf, pbuf, dsbuf, dkacc, dvacc, nxk, nxv, cark, carv):
        j = pl.program_id(1)
        mu = NU - 1 - j
        band, is_cur = _attn_mask()
        first = jnp.logical_and(band, jnp.logical_or(is_cur, mu > 0))

        @pl.when(j == 0)
        def _():
            cark[...] = jnp.zeros_like(cark)
            carv[...] = jnp.zeros_like(carv)
            db_ref[...] = jnp.zeros_like(db_ref)

        for ref in (dkacc, dvacc, nxk, nxv):
            ref[...] = jnp.zeros_like(ref)

        def keys(ref_c, ref_p, n, r):
            prev = ref_c[_attn_rows(n - 1, r, dil), :] if n > 0 else ref_p[_attn_rows(nbu - 1, r, dil), :]
            return jnp.concatenate([prev, ref_c[_attn_rows(n, r, dil), :]], axis=0).astype(BF16)

        for c, (n, r) in enumerate(combos):
            rows = _attn_rows(n, r, dil)
            q = (q_ref[rows, :] * scale).astype(BF16)
            dob = do_ref[rows, :].astype(BF16)
            kk, vv = keys(kc_ref, kp_ref, n, r), keys(vc_ref, vp_ref, n, r)
            for h in range(hpb):
                sl = slice(h * HD, (h + 1) * HD)
                sbuf[c * hpb + h] = _dot_nt(q[:, sl], kk[:, sl])
                dpbuf[c * hpb + h] = _dot_nt(dob[:, sl], vv[:, sl])
        dbs = [None] * hpb
        for c, (n, r) in enumerate(combos):
            rows = _attn_rows(n, r, dil)
            lse, ev = l_ref[rows, :], e_ref[rows, :]
            for h in range(hpb):
                one = slice(h * HD, h * HD + 1)
                s = jnp.where(band if n > 0 else first, sbuf[c * hpb + h] + b_ref[h], NEG)
                p = jnp.exp(s - lse[:, one])
                ds = p * (dpbuf[c * hpb + h] + ev[:, one])
                pbuf[c * hpb + h] = p.astype(BF16)
                dsbuf[c * hpb + h] = ds.astype(BF16)
                dbs[h] = ds if dbs[h] is None else dbs[h] + ds
        for h in range(hpb):
            db_ref[h] += dbs[h]
        for c, (n, r) in enumerate(combos):
            rows = _attn_rows(n, r, dil)
            q = (q_ref[rows, :] * scale).astype(BF16)
            dob = do_ref[rows, :].astype(BF16)
            kk = keys(kc_ref, kp_ref, n, r)
            dq, dkk, dvv = [], [], []
            for h in range(hpb):
                sl = slice(h * HD, (h + 1) * HD)
                dsb = dsbuf[c * hpb + h]
                dq.append(_dot(dsb, kk[:, sl]) * scale)
                dkk.append(_dot_tn(dsb, q[:, sl]))
                dvv.append(_dot_tn(pbuf[c * hpb + h], dob[:, sl]))
            dq_ref[rows, :] = jnp.concatenate(dq, axis=1)
            dkk, dvv = jnp.concatenate(dkk, axis=1), jnp.concatenate(dvv, axis=1)
            dkacc[rows, :] += dkk[BLK:2 * BLK, :]
            dvacc[rows, :] += dvv[BLK:2 * BLK, :]
            if n > 0:
                prow = _attn_rows(n - 1, r, dil)
                dkacc[prow, :] += dkk[0:BLK, :]
                dvacc[prow, :] += dvv[0:BLK, :]
            else:
                prow = _attn_rows(nbu - 1, r, dil)
                nxk[prow, :] += dkk[0:BLK, :]
                nxv[prow, :] += dvv[0:BLK, :]
        dk_ref[...] = dkacc[...] + cark[...]
        dv_ref[...] = dvacc[...] + carv[...]
        cark[...] = nxk[...]
        carv[...] = nxv[...]

    def cur(c):
        return pl.BlockSpec((U, ATT_LANES), lambda hb, j: (NU - 1 - j, c * HBK + hb))

    def prev(c):
        return pl.BlockSpec((U, ATT_LANES), lambda hb, j: (jnp.maximum(NU - 2 - j, 0), c * HBK + hb))

    own = pl.BlockSpec((U, ATT_LANES), lambda hb, j: (NU - 1 - j, hb))
    bspec = pl.BlockSpec((hpb, BLK, 2 * BLK), lambda hb, j: (hb, 0, 0))
    nhb = len(combos) * hpb
    unit = pltpu.VMEM((U, ATT_LANES), F32)
    return _call(
        body, name="attn_bwd_d%d" % dil, grid=(HBK, NU),
        in_specs=[cur(4), cur(5), prev(5), cur(6), prev(6), own, own, own, bspec],
        out_specs=[own, own, own, bspec],
        out_shape=[jax.ShapeDtypeStruct((S, BR), F32)] * 3 + [jax.ShapeDtypeStruct((8, BLK, 2 * BLK), F32)],
        scratch_shapes=[pltpu.VMEM((nhb, BLK, 2 * BLK), F32)] * 2 + [pltpu.VMEM((nhb, BLK, 2 * BLK), BF16)] * 2 + [unit] * 6,
        compiler_params=_params("arbitrary", "arbitrary"),
    )(proj, proj, proj, proj, proj, do, e, lse, bias)


def _attn_dsum(ds, dbg, dproj, BR):
    S = dbg.shape[0]
    TS = _tile(S, 512)

    def body(*refs):
        o_ref = refs[11]
        for k in range(3):
            o_ref[:, k * BR:(k + 1) * BR] = (refs[k][...] + refs[3 + k][...] + refs[6 + k][...]).astype(BF16)
        o_ref[:, 3 * BR:4 * BR] = refs[9][...].astype(BF16)

    spec = pl.BlockSpec((TS, BR), lambda i: (i, 0))
    return _call(
        body, name="attn_dsum", grid=(S // TS,),
        in_specs=[spec] * 10 + [pl.BlockSpec(memory_space=pl.ANY)],
        out_specs=pl.BlockSpec((TS, 4 * BR), lambda i: (i, 1)),
        out_shape=jax.ShapeDtypeStruct(dproj.shape, BF16),
        input_output_aliases={10: 0},
        compiler_params=_params("arbitrary"),
    )(*[t for trip in ds for t in trip], dbg, dproj)


def _lru_bwd(proj, h, dcat, rows, wa, wx, dproj, BR):
    S = proj.shape[0]
    TS = _tile(S, 256)
    PAD = TS // 2
    nt = S // TS

    def body(cx_ref, hx_ref, cg_ref, h_ref, hh_ref, dy_ref, r_ref, wa_ref, wx_ref, _,
             dp_ref, sum_ref, dwa_ref, dwx_ref, ext, hext, aext, dext, sa, sb, carry):
        i = pl.program_id(0)
        ti = nt - 1 - i

        @pl.when(i == 0)
        def _():
            sa[TS:TS + PAD, :] = jnp.ones((PAD, BR), F32)
            sb[TS:TS + PAD, :] = jnp.zeros((PAD, BR), F32)
            carry[...] = jnp.zeros_like(carry)
            dext[TS:TS + 8, :] = jnp.zeros((8, BR), F32)
            sum_ref[...] = jnp.zeros_like(sum_ref)
            dwa_ref[...] = jnp.zeros_like(dwa_ref)
            dwx_ref[...] = jnp.zeros_like(dwx_ref)

        w0, w1, w2, w3 = r_ref[0:1, :], r_ref[1:2, :], r_ref[2:3, :], r_ref[3:4, :]
        sp = r_ref[7:8, :]
        cx = cx_ref[...]
        ext[0:8, :] = jnp.where(ti > 0, hx_ref[...], 0.0)
        ext[8:8 + TS, :] = cx
        x3, x2, x1 = ext[pl.ds(5, TS), :], ext[pl.ds(6, TS), :], ext[pl.ds(7, TS), :]
        xc = x3 * w0 + x2 * w1 + x1 * w2 + cx * w3 + r_ref[4:5, :]
        wa_, wx_ = wa_ref[...], wx_ref[...]
        r, ig, a, mult = _lru_gates(xc, wa_, wx_, r_ref[5:6, :], r_ref[6:7, :], sp)
        cg, dy, hv = cg_ref[...], dy_ref[...], h_ref[...]
        dp_ref[:, BR:2 * BR] = (dy * hv * _dsilu(cg)).astype(BF16)
        aext[0:TS, :] = a
        aext[TS:TS + 8, :] = jnp.broadcast_to(carry[0:1, :], (8, BR))
        sa[0:TS, :] = aext[pl.ds(1, TS), :]
        sb[0:TS, :] = dy * _silu(cg)
        d = 1
        while d < TS:
            A, B = sa[0:TS, :], sb[0:TS, :]
            As, Bs = sa[pl.ds(d, TS), :], sb[pl.ds(d, TS), :]
            sb[0:TS, :] = B + A * Bs
            sa[0:TS, :] = A * As
            d *= 2
        gh = sb[0:TS, :] + sa[0:TS, :] * carry[1:2, :]
        carry[0:1, :] = a[0:1, :]
        carry[1:2, :] = gh[0:1, :]
        hext[0:8, :] = jnp.where(ti > 0, hh_ref[...], 0.0)
        hext[8:8 + TS, :] = hv
        da = gh * hext[pl.ds(7, TS), :]
        dmult = gh * ig * xc
        dig = gh * mult * xc
        dxc = gh * mult * ig
        dla = da * a - dmult * a * a / mult
        dpr = dla * (-LRU_C) * sp * r * (1.0 - r)
        dpi = dig * ig * (1.0 - ig)
        xb, dprb, dpib = xc.astype(BF16), dpr.astype(BF16), dpi.astype(BF16)
        dwa_ref[...] += _dot_tn(xb, dprb)
        dwx_ref[...] += _dot_tn(xb, dpib)
        dxc = dxc + _dot_nt(dprb, wa_) + _dot_nt(dpib, wx_)
        sum_ref[0:1, :] += _colsum(dxc * x3)
        sum_ref[1:2, :] += _colsum(dxc * x2)
        sum_ref[2:3, :] += _colsum(dxc * x1)
        sum_ref[3:4, :] += _colsum(dxc * cx)
        sum_ref[4:5, :] += _colsum(dxc)
        sum_ref[5:6, :] += _colsum(dpr)
        sum_ref[6:7, :] += _colsum(dpi)
        sum_ref[7:8, :] += _colsum(dla * (-LRU_C) * r)
        dext[0:TS, :] = dxc
        dcx = w3 * dxc + w2 * dext[pl.ds(1, TS), :] + w1 * dext[pl.ds(2, TS), :] + w0 * dext[pl.ds(3, TS), :]
        dext[TS:TS + 8, :] = dxc[0:8, :]
        dp_ref[:, 0:BR] = dcx.astype(BF16)

    def rev(k):
        return pl.BlockSpec((TS, BR), lambda i: (nt - 1 - i, k))

    def rev_halo(k):
        return pl.BlockSpec((8, BR), lambda i: (jnp.maximum((nt - 1 - i) * (TS // 8) - 1, 0), k))

    full = pl.BlockSpec((BR, BR), lambda i: (0, 0))
    rspec = pl.BlockSpec((8, BR), lambda i: (0, 0))
    return _call(
        body, name="lru_bwd", grid=(nt,),
        in_specs=[rev(8), rev_halo(8), rev(9), rev(0), rev_halo(0), rev(2), rspec, full, full, pl.BlockSpec(memory_space=pl.ANY)],
        out_specs=[pl.BlockSpec((TS, 2 * BR), lambda i: (nt - 1 - i, 4)), rspec, full, full],
        input_output_aliases={9: 0},
        out_shape=[jax.ShapeDtypeStruct(dproj.shape, BF16), jax.ShapeDtypeStruct((8, BR), F32),
                   jax.ShapeDtypeStruct((BR, BR), F32), jax.ShapeDtypeStruct((BR, BR), F32)],
        scratch_shapes=[pltpu.VMEM((TS + 8, BR), F32)] * 4 + [pltpu.VMEM((TS + PAD, BR), F32)] * 2 + [pltpu.VMEM((8, BR), F32)],
        compiler_params=_params("arbitrary"),
    )(proj, proj, proj, h, h, dcat, rows, wa, wx, dproj)


def _s5_bwd(proj, dcat, xre_all, xim_all, a8, cl, bbre, bbim, cre, cim, rows, wg, dproj, BR):
    S = proj.shape[0]
    W = a8.shape[1]
    TS = _tile(S, 512)
    nt = S // TS
    CB, SB, J = S5_CB, S5_SB, S5_LAGS
    nblk = BR // CB

    def body(u_ref, dg_ref, dy_ref, xre_ref, xim_ref, hre_ref, him_ref, a8_ref, cl_ref, bbre_ref, bbim_ref, cre_ref, cim_ref,
             r_ref, wg_ref, _, dp_ref, sum_ref, dwg_ref, da_ref, dbbre_ref, dbbim_ref, dcre_ref, dcim_ref,
             sre, sim, ere, eim, ypre_s, dext, dylag, carry):
        i = pl.program_id(0)
        ti = nt - 1 - i

        @pl.when(i == 0)
        def _():
            dext[TS:TS + 8, :] = jnp.zeros((8, BR), F32)
            carry[...] = jnp.zeros_like(carry)
            for ref in (sum_ref, dwg_ref, da_ref, dbbre_ref, dbbim_ref, dcre_ref, dcim_ref):
                ref[...] = jnp.zeros_like(ref)

        u, dg, dy = u_ref[...], dg_ref[...], dy_ref[...]
        for m in range(nblk):
            cs, ws = slice(m * CB, (m + 1) * CB), slice(m * SB, (m + 1) * SB)
            ypre_s[:, cs] = (_dot(xre_ref[:, ws].astype(BF16), cre_ref[ws, :]) - _dot(xim_ref[:, ws].astype(BF16), cim_ref[ws, :]))
        ypre = ypre_s[...] + r_ref[0:1, :] * u
        yg = _gelu(ypre)
        wg_ = wg_ref[...]
        s = _sigmoid(_dot(yg.astype(BF16), wg_) + r_ref[1:2, :])
        dgl = dy * _silu(dg)
        dp_ref[:, BR:2 * BR] = (dy * yg * s * _dsilu(dg)).astype(BF16)
        dps = dgl * yg * s * (1.0 - s)
        dpsb = dps.astype(BF16)
        sum_ref[1:2, :] += _colsum(dps)
        dwg_ref[...] += _dot_tn(yg.astype(BF16), dpsb)
        dyp = (dgl * s + _dot_nt(dpsb, wg_)) * _dgelu(ypre)
        sum_ref[0:1, :] += _colsum(dyp * u)
        dext[0:TS, :] = dyp
        for m in range(nblk):
            cs, ws = slice(m * CB, (m + 1) * CB), slice(m * SB, (m + 1) * SB)
            dypb = dext[0:TS, cs].astype(BF16)
            xre, xim = xre_ref[:, ws], xim_ref[:, ws]
            dcre_ref[ws, :] += _dot_tn(xre.astype(BF16), dypb)
            dcim_ref[ws, :] -= _dot_tn(xim.astype(BF16), dypb)
            for j in range(J):
                dylag[:, j * CB:(j + 1) * CB] = dext[pl.ds(j, TS), cs].astype(BF16)
            w = _dot(dylag[...], cl_ref[m])
            sre[...] = w[:, 0:SB]
            sim[...] = w[:, SB:2 * SB]
            ar = jnp.broadcast_to(a8_ref[0:1, ws], (8, SB))
            ai = jnp.broadcast_to(a8_ref[1:2, ws], (8, SB))
            gr, gi = carry[0:8, ws], carry[8:16, ws]
            for g in reversed(range(TS // 8)):
                rg = slice(8 * g, 8 * g + 8)
                gr, gi = sre[rg, :] + ar * gr + ai * gi, sim[rg, :] + ar * gi - ai * gr
                sre[rg, :] = gr
                sim[rg, :] = gi
            carry[0:8, ws] = gr
            carry[8:16, ws] = gi
            gre, gim = sre[...], sim[...]
            ere[0:8, :] = jnp.where(ti > 0, hre_ref[:, ws], 0.0)
            eim[0:8, :] = jnp.where(ti > 0, him_ref[:, ws], 0.0)
            ere[8:8 + TS, :] = xre
            eim[8:8 + TS, :] = xim
            xpr, xpi = ere[pl.ds(7, TS), :], eim[pl.ds(7, TS), :]
            da_ref[0:1, ws] += _colsum(gre * xpr + gim * xpi)
            da_ref[1:2, ws] += _colsum(gim * xpr - gre * xpi)
            greb, gimb = gre.astype(BF16), gim.astype(BF16)
            ub = u_ref[:, cs].astype(BF16)
            dbbre_ref[cs, :] += _dot_tn(ub, greb)
            dbbim_ref[cs, :] += _dot_tn(ub, gimb)
            du = _dot_nt(greb, bbre_ref[cs, :]) + _dot_nt(gimb, bbim_ref[cs, :]) + r_ref[0:1, cs] * dext[0:TS, cs]
            dp_ref[:, cs] = du.astype(BF16)
        dext[TS:TS + 8, :] = dext[0:8, :]

    def rev(width, k):
        return pl.BlockSpec((TS, width), lambda i: (nt - 1 - i, k))

    def rev_halo(width):
        return pl.BlockSpec((8, width), lambda i: (jnp.maximum((nt - 1 - i) * (TS // 8) - 1, 0), 0))

    def const(shape):
        return pl.BlockSpec(shape, lambda i: (0,) * len(shape))

    return _call(
        body, name="s5_bwd", grid=(nt,),
        in_specs=[rev(BR, 10), rev(BR, 11), rev(BR, 3), rev(W, 0), rev(W, 0), rev_halo(W), rev_halo(W),
                  const((8, W)), const((nblk, J * CB, 2 * SB)), const((BR, SB)), const((BR, SB)), const((W, CB)), const((W, CB)),
                  const((8, BR)), const((BR, BR)), pl.BlockSpec(memory_space=pl.ANY)],
        input_output_aliases={15: 0},
        out_specs=[pl.BlockSpec((TS, 2 * BR), lambda i: (nt - 1 - i, 5)), const((8, BR)), const((BR, BR)), const((8, W)),
                   const((BR, SB)), const((BR, SB)), const((W, CB)), const((W, CB))],
        out_shape=[jax.ShapeDtypeStruct(dproj.shape, BF16), jax.ShapeDtypeStruct((8, BR), F32), jax.ShapeDtypeStruct((BR, BR), F32),
                   jax.ShapeDtypeStruct((8, W), F32), jax.ShapeDtypeStruct((BR, SB), F32), jax.ShapeDtypeStruct((BR, SB), F32),
                   jax.ShapeDtypeStruct((W, CB), F32), jax.ShapeDtypeStruct((W, CB), F32)],
        scratch_shapes=[pltpu.VMEM((TS, SB), F32)] * 2 + [pltpu.VMEM((TS + 8, SB), F32)] * 2
        + [pltpu.VMEM((TS, BR), F32), pltpu.VMEM((TS + 8, BR), F32), pltpu.VMEM((TS, J * CB), BF16), pltpu.VMEM((16, W), F32)],
        compiler_params=_params("arbitrary"),
    )(proj, proj, dcat, xre_all, xim_all, xre_all, xim_all, a8, cl, bbre, bbim, cre, cim, rows, wg, dproj)


_WEIGHTS = ['rel_bias', 'w_ada', 'b_ada', 'w_in', 'conv_a', 'conv_c', 'conv_c_b', 'lru_wa', 'lru_ba', 'lru_wx', 'lru_bx',
            'lru_lambda', 's5_lam_re', 's5_lam_im', 's5_log_dt', 's5_b_re', 's5_b_im', 's5_c_re', 's5_c_im', 's5_d',
            's5_w_glu', 's5_b_glu', 'w_out', 'ln_g', 'ln_b']
_LAYER_SMALL = ['conv_a', 'conv_c', 'conv_c_b', 'lru_wa', 'lru_ba', 'lru_wx', 'lru_bx', 'lru_lambda', 's5_lam_re', 's5_lam_im',
                's5_log_dt', 's5_b_re', 's5_b_im', 's5_c_re', 's5_c_im', 's5_d', 's5_b_glu']
_SMALL = ['rel_bias'] + _LAYER_SMALL + ['ln_g', 'ln_b']


def _t5_bucket(dist):
    max_exact = REL_BUCKETS // 2
    nf = jnp.maximum(dist, 1).astype(F32)
    large = max_exact + (jnp.log(nf / max_exact) / math.log(REL_MAX_DIST / max_exact)
                         * (REL_BUCKETS - max_exact)).astype(jnp.int32)
    large = jnp.minimum(large, REL_BUCKETS - 1)
    return jnp.where(dist < max_exact, dist, large)


def _bias_tables(rel_bias):
    i = jnp.arange(BLK)[:, None]
    j = jnp.arange(2 * BLK)[None, :]
    delta = i + BLK - j
    out = []
    for window, dil in DILATIONS:
        bucket = _t5_bucket(jnp.clip(delta, 0, window // dil) * dil)
        onehot = (bucket[:, :, None] == jnp.arange(REL_BUCKETS)[None, None, :]).astype(F32)
        out.append(jnp.einsum('ijb,bh->hij', onehot, rel_bias, precision=lax.Precision.HIGHEST))
    return jnp.stack(out)


def _prep_layer(q):
    eye8 = jnp.eye(8, dtype=F32)
    G = q['s5_lam_re'].shape[0]
    nblk = G // 8

    def block_diag(w):
        hd = w.shape[1]
        return (w[:, :, None, :] * eye8[:, None, :, None]).reshape(8 * hd, 8 * hd)

    def compact_b(bb):
        t = jnp.transpose(bb.reshape(nblk, 8, S5_STATE, S5_CH), (0, 1, 3, 2))
        return (t[:, :, :, None, :] * eye8[None, :, None, :, None]).reshape(G * S5_CH, 8 * S5_STATE)

    def compact_c(cc):
        t = jnp.transpose(cc.reshape(nblk, 8, S5_CH, S5_STATE), (0, 1, 3, 2))
        return (t[:, :, :, None, :] * eye8[None, :, None, :, None]).reshape(G * S5_STATE, 8 * S5_CH)

    lam_re, lam_im = q['s5_lam_re'], q['s5_lam_im']
    dt = jnp.exp(q['s5_log_dt'])[:, None]
    mag = jnp.exp(lam_re * dt)
    ab_re = mag * jnp.cos(lam_im * dt)
    ab_im = mag * jnp.sin(lam_im * dt)
    den = lam_re * lam_re + lam_im * lam_im
    f_re = ((ab_re - 1.0) * lam_re + ab_im * lam_im) / den
    f_im = (ab_im * lam_re - (ab_re - 1.0) * lam_im) / den
    b_re, b_im = q['s5_b_re'], q['s5_b_im']
    bb_re = f_re[..., None] * b_re - f_im[..., None] * b_im
    bb_im = f_re[..., None] * b_im + f_im[..., None] * b_re
    return dict(
        conv_a=q['conv_a'], conv_c=q['conv_c'], conv_c_b=q['conv_c_b'], lru_ba=q['lru_ba'], lru_bx=q['lru_bx'],
        sp=jax.nn.softplus(-q['lru_lambda']), wa=block_diag(q['lru_wa']), wx=block_diag(q['lru_wx']),
        ar=ab_re.reshape(-1), ai=ab_im.reshape(-1), bbre=compact_b(bb_re), bbim=compact_b(bb_im),
        cre=compact_c(q['s5_c_re']), cim=compact_c(q['s5_c_im']), s5_d=q['s5_d'], s5_b_glu=q['s5_b_glu'])


def _s5_lag_weights(kp):
    J, CB, SB = S5_LAGS, S5_CB, S5_SB
    ar, ai = kp['ar'], kp['ai']
    W = ar.shape[0]
    nblk = W // SB
    cr, ci = jnp.ones_like(ar), jnp.zeros_like(ai)
    pows = []
    for _ in range(J + 1):
        pows.append((cr, ci))
        cr, ci = cr * ar - ci * ai, cr * ai + ci * ar
    pwr = jnp.transpose(jnp.stack([p[0] for p in pows[:J]]).reshape(J, nblk, SB), (1, 0, 2))[:, :, None, :]
    pwi = jnp.transpose(jnp.stack([p[1] for p in pows[:J]]).reshape(J, nblk, SB), (1, 0, 2))[:, :, None, :]

    def lagged(re, im, sign):
        out = jnp.concatenate([re * pwr - sign * im * pwi, sign * re * pwi + im * pwr], axis=-1)
        return out.reshape(nblk, J * CB, 2 * SB).astype(BF16)

    bbl = lagged(kp['bbre'].reshape(nblk, 1, CB, SB), kp['bbim'].reshape(nblk, 1, CB, SB), 1.0)
    c0r = jnp.transpose(kp['cre'].reshape(nblk, SB, CB), (0, 2, 1))[:, None]
    c0i = -jnp.transpose(kp['cim'].reshape(nblk, SB, CB), (0, 2, 1))[:, None]
    cl = lagged(c0r, c0i, -1.0)
    return _rows8([pows[J][0], pows[J][1]], W), bbl, cl


def _rows8(vecs, width):
    rows = [v.reshape(1, width).astype(F32) for v in vecs]
    return jnp.concatenate(rows + [jnp.zeros((8 - len(rows), width), F32)], axis=0)


def kernel(x, c, rel_bias, w_ada, b_ada, w_in, conv_a, conv_c, conv_c_b, lru_wa, lru_ba, lru_wx, lru_bx, lru_lambda, s5_lam_re, s5_lam_im, s5_log_dt, s5_b_re, s5_b_im, s5_c_re, s5_c_im, s5_d, s5_w_glu, s5_b_glu, w_out, ln_g, ln_b, loss_target, m_rel_bias, m_w_ada, m_b_ada, m_w_in, m_conv_a, m_conv_c, m_conv_c_b, m_lru_wa, m_lru_ba, m_lru_wx, m_lru_bx, m_lru_lambda, m_s5_lam_re, m_s5_lam_im, m_s5_log_dt, m_s5_b_re, m_s5_b_im, m_s5_c_re, m_s5_c_im, m_s5_d, m_s5_w_glu, m_s5_b_glu, m_w_out, m_ln_g, m_ln_b, v_rel_bias, v_w_ada, v_b_ada, v_w_in, v_conv_a, v_conv_c, v_conv_c_b, v_lru_wa, v_lru_ba, v_lru_wx, v_lru_bx, v_lru_lambda, v_s5_lam_re, v_s5_lam_im, v_s5_log_dt, v_s5_b_re, v_s5_b_im, v_s5_c_re, v_s5_c_im, v_s5_d, v_s5_w_glu, v_s5_b_glu, v_w_out, v_ln_g, v_ln_b):
    a = dict(locals())
    me = 4 * lax.axis_index("x") + 2 * lax.axis_index("y") + lax.axis_index("c")
    x0, target = x[0], loss_target[0]
    S, D = x0.shape
    BR = D // 4
    NS = w_in.shape[2]
    SH = BR // NDEV
    assert S % (BLK * DILATIONS[-1][1]) == 0 and BR % (8 * S5_CH) == 0

    small = _all_gather(_pack([c, conv_a, conv_c]), "gather_small")
    per_dev = [_unpack(small[d], [c.shape, conv_a.shape, conv_c.shape]) for d in range(NDEV)]
    c_all = jnp.concatenate([p[0] for p in per_dev], axis=0)
    conv_a_full = jnp.concatenate([p[1] for p in per_dev], axis=2)
    conv_c_full = jnp.concatenate([p[2] for p in per_dev], axis=2)
    me1 = me.reshape(1).astype(jnp.int32)
    w_in_bf, w_out_bf, w_glu_bf = w_in.astype(BF16), w_out.astype(BF16), s5_w_glu.astype(BF16)

    b_cols = lax.dynamic_slice(b_ada, (0, me * NS), (DEPTH, NS)).reshape(DEPTH, 1, NS)
    ada_all = _all_gather(_ada_cols(c_all, w_ada, b_cols), "gather_ada")
    ada_me = lax.dynamic_index_in_dim(ada_all, me, axis=2, keepdims=False)
    ada_me = jnp.transpose(ada_me, (1, 0, 2)).reshape(DEPTH, 3 * D)
    shift, scale, gate = ada_me[:, :D], ada_me[:, D:2 * D], ada_me[:, 2 * D:]

    bias_tabs, bias_pull = jax.vjp(_bias_tables, rel_bias)
    HD = BR // 8
    head_ones = jnp.kron(jnp.eye(8, dtype=F32), jnp.ones((HD, HD), F32))

    saved = []
    xl = x0
    for l in range(DEPTH):
        q = {n: a[n][l] for n in _LAYER_SMALL}
        q['conv_a'], q['conv_c'] = conv_a_full[l], conv_c_full[l]
        kp, pull = jax.vjp(_prep_layer, q)
        a8, bbl, cl = _s5_lag_weights(kp)
        mod = _rows8([1.0 + scale[l], shift[l]], D)
        arows = _rows8(list(kp['conv_a']), BR)
        lrows = _rows8(list(kp['conv_c']) + [kp['conv_c_b'], kp['lru_ba'], kp['lru_bx'], kp['sp']], BR)
        srows = _rows8([kp['s5_d'], kp['s5_b_glu']], BR)
        orows = _rows8([1.0 + gate[l], ln_g[l], ln_b[l]], D)
        wa, wx = kp['wa'].astype(BF16), kp['wx'].astype(BF16)
        s5w = [kp[n].astype(BF16) for n in ('bbre', 'bbim', 'cre', 'cim')]

        hbf = _modulate(xl, mod)
        proj, w_in_l, w_out_l, w_glu_l = _proj_gather(hbf, w_in_bf[l], [w_out_bf[l], w_glu_bf[l]], me1)
        w_out_l, w_glu_l = w_out_l.reshape(D, D), w_glu_l.reshape(BR, BR)
        ya = _conv_a_fwd(proj, arows, BR)
        os_, ls_ = [], []
        for g, (_, dil) in enumerate(DILATIONS):
            o, lse = _attn_fwd(proj, bias_tabs[g], dil, BR)
            os_.append(o)
            ls_.append(lse)
        yb = _attn_mix_fwd(proj, os_, ls_, BR)
        yc, hs = _lru_fwd(proj, lrows, wa, wx, BR)
        yd, xre, xim = _s5_fwd(proj, a8, bbl, s5w[2], s5w[3], srows, w_glu_l, BR)
        if l < DEPTH - 1:
            xn, y, cat = _out_ln([ya, yb, yc, yd], w_out_l, xl, orows)
        else:
            xn, y, cat, loss_local = _out_ln([ya, yb, yc, yd], w_out_l, xl, orows, target)
        saved.append(dict(x=xl, proj=proj, hbf=hbf, os=os_, ls=ls_, hs=hs, xre=xre, xim=xim, y=y, cat=cat, pull=pull,
                          mod=mod, arows=arows, lrows=lrows, srows=srows, orows=orows, wa=wa, wx=wx, s5w=s5w,
                          qw=(a8, cl), w_in=w_in_l, w_out=w_out_l, w_glu=w_glu_l))
        xl = xn

    dout = xl

    dbias = jnp.zeros_like(bias_tabs)
    lgrads, dada, dw_in_parts, dw_out_parts, glu_parts = [None] * DEPTH, [None] * DEPTH, [None] * DEPTH, [None] * DEPTH, [None] * DEPTH
    for l in reversed(range(DEPTH)):
        sv = saved[l]
        proj = sv['proj']
        dxres, dyb, dcat, lnsum = _ln_bwd(dout, sv['x'], sv['y'], sv['orows'], sv['w_out'])
        dw_out_parts[l] = _dw_scatter(sv['cat'], dyb, D // NDEV, True, me1, "dw_out_scatter")
        dproj, asum = _conv_a_bwd(proj, dcat, sv['arows'], BR)
        dos, es, dbg = _attn_mix_bwd(proj, dcat, sv['os'], sv['ls'], head_ones, BR)
        dqkv, dbs = [], []
        for g, (_, dil) in enumerate(DILATIONS):
            dq_, dk_, dv_, db_ = _attn_bwd(proj, dos[g], es[g], sv['ls'][g], bias_tabs[g], dil, BR)
            dqkv.append((dq_, dk_, dv_))
            dbs.append(db_)
        dbias = dbias + jnp.stack(dbs)
        dproj = _attn_dsum(dqkv, dbg, dproj, BR)
        dproj, lsum, dwa, dwx = _lru_bwd(proj, sv['hs'], dcat, sv['lrows'], sv['wa'], sv['wx'], dproj, BR)
        dproj, ssum, dwg, da_, dbbre, dbbim, dcre, dcim = _s5_bwd(
            proj, dcat, sv['xre'], sv['xim'], *sv['qw'], *sv['s5w'], sv['srows'], sv['w_glu'], dproj, BR)
        dkp = dict(conv_a=asum[0:3], conv_c=lsum[0:4], conv_c_b=lsum[4], lru_ba=lsum[5], lru_bx=lsum[6], sp=lsum[7],
                   wa=dwa, wx=dwx, ar=da_[0], ai=da_[1], bbre=dbbre, bbim=dbbim, cre=dcre, cim=dcim,
                   s5_d=ssum[0], s5_b_glu=ssum[1])
        lgrads[l] = dict(sv['pull'](dkp)[0], ln_g=lnsum[0], ln_b=lnsum[1])
        sides = [(dwg.reshape(NDEV, BR // NDEV, BR), False)]
        if l == 0:
            local = {'rel_bias': bias_pull(dbias)[0], 'loss': loss_local.reshape(1)}
            for n in _LAYER_SMALL + ['ln_g', 'ln_b']:
                local[n] = jnp.stack([lgrads[k][n] for k in range(DEPTH)])
            packed = _SMALL + ['loss']
            sides.append((_pack([local[n] for n in packed]), True))
        dout, msum, glu_parts[l], *gathered = _dh(dproj, sv['w_in'], dxres, sv['x'], sv['mod'], sides)
        dw_in_parts[l] = _dw_scatter(sv['hbf'], dproj, NS, False, me1, "dw_in_scatter")
        dada[l] = jnp.concatenate([msum[1], msum[0], lnsum[2]])
    grad_x = dout[None]

    out = {}

    def put(name, res):
        out[name] = res

    put('w_in', _adamw(dw_in_parts, w_in, m_w_in, v_w_in, "adamw_w_in"))
    put('w_out', _adamw(dw_out_parts, w_out, m_w_out, v_w_out, "adamw_w_out"))
    put('s5_w_glu', _adamw(glu_parts, s5_w_glu, m_s5_w_glu, v_s5_w_glu, "adamw_w_glu"))

    dada_all = _all_gather(jnp.stack(dada), "gather_dada")
    dada_cols = jnp.transpose(lax.dynamic_slice(dada_all, (0, 0, me * NS), (NDEV, DEPTH, NS)), (1, 0, 2))
    gw_ada, gb_ada = _ada_grads(c_all, dada_all, dada_cols)
    put('w_ada', _adamw([gw_ada[l] for l in range(DEPTH)], w_ada, m_w_ada, v_w_ada, "adamw_w_ada"))

    shapes = [local[n].shape for n in packed]
    gs = dict(zip(packed, _unpack(_sum8(gathered[0], "sum_small_grads"), shapes)))
    loss = gs['loss'][0]
    gs['conv_a'] = lax.dynamic_slice_in_dim(gs['conv_a'], me * SH, SH, axis=2)
    gs['conv_c'] = lax.dynamic_slice_in_dim(gs['conv_c'], me * SH, SH, axis=2)
    gs['b_ada'] = gb_ada
    names = _SMALL + ['b_ada']
    shapes = [a[n].shape for n in names]
    res = _adamw([_pack([gs[n] for n in names])], _pack([a[n] for n in names])[None], _pack([a['m_' + n] for n in names])[None],
                 _pack([a['v_' + n] for n in names])[None], "adamw_small")
    unpacked = [_unpack(r[0], shapes) for r in res]
    for k, n in enumerate(names):
        put(n, tuple(unpacked[j][k] for j in range(4)))

    return (loss, grad_x, *[out[n][0] for n in _WEIGHTS], *[out[n][1] for n in _WEIGHTS],
            *[out[n][2] for n in _WEIGHTS], *[out[n][3] for n in _WEIGHTS])
```

```python
import math

import jax
import jax.numpy as jnp
from jax import lax
from jax.experimental import pallas as pl
from jax.experimental.pallas import tpu as pltpu

F32 = jnp.float32
BF16 = jnp.bfloat16

NDEV = 8
DEPTH = 2
BLK = 128
DILATIONS = ((128, 1), (512, 4), (2048, 16))
REL_BUCKETS = 32
REL_MAX_DIST = 2048
LRU_C = 8.0
S5_CH = 16
S5_STATE = 64
ALPHA = (2 * DEPTH) ** 0.25
LN_EPS = 1e-5
ADAM_LR, ADAM_B1, ADAM_B2, ADAM_EPS, ADAM_WD, ADAM_STEP = 0.001, 0.9, 0.999, 1e-08, 0.01, 10
NEG = -1e30
VMEM_LIMIT = 56 * 1024 * 1024
TILE_STREAM = 512
TILE_RESIDENT = 256
TILE_PROJ = 2048
TILE_DW = 1024


def _call(body, **kw):
    return pl.pallas_call(body, **kw)


def _params(*sem):
    return pltpu.CompilerParams(dimension_semantics=sem, vmem_limit_bytes=VMEM_LIMIT)


def _sigmoid(x):
    return 1.0 / (1.0 + jnp.exp(-x))


def _silu(x):
    return x * _sigmoid(x)


def _dsilu(x):
    s = _sigmoid(x)
    return s * (1.0 + x * (1.0 - s))


_GC = math.sqrt(2.0 / math.pi)


def _gelu(x):
    return 0.5 * x * (1.0 + jnp.tanh(_GC * (x + 0.044715 * x * x * x)))


def _dgelu(x):
    t = jnp.tanh(_GC * (x + 0.044715 * x * x * x))
    return 0.5 * (1.0 + t) + 0.5 * x * (1.0 - t * t) * _GC * (1.0 + 3.0 * 0.044715 * x * x)


def _expm1(x):
    series = x * (1.0 + x * (1.0 / 2) * (1.0 + x * (1.0 / 3) * (1.0 + x * (1.0 / 4) * (1.0 + x * (1.0 / 5) * (1.0 + x * (1.0 / 6))))))
    return jnp.where(jnp.abs(x) < 0.25, series, jnp.exp(x) - 1.0)


def _dot(a, b):
    return jnp.dot(a, b, preferred_element_type=F32)


def _dot_nt(a, b):
    return lax.dot_general(a, b, (((1,), (1,)), ((), ())), preferred_element_type=F32)


def _dot_tn(a, b):
    return lax.dot_general(a, b, (((0,), (0,)), ((), ())), preferred_element_type=F32)


def _colsum(v):
    return jnp.sum(v, axis=0, keepdims=True)


def _all_gather(x, name):
    def body(x_ref, o_ref, send_sems, recv_sems, local_sem):
        ix, iy, ic = lax.axis_index("x"), lax.axis_index("y"), lax.axis_index("c")
        me, sibling = (ix, iy, ic), (ix, iy, 1 - ic)
        chips = [(1 - ix, iy), (ix, 1 - iy), (1 - ix, 1 - iy)]

        def slot(px, py, pc):
            return o_ref.at[4 * px + 2 * py + pc]

        def copy(k, block, to, src=None):
            return pltpu.make_async_remote_copy(
                src_ref=slot(*block) if src is None else src, dst_ref=slot(*block), send_sem=send_sems.at[k],
                recv_sem=recv_sems.at[k], device_id=to, device_id_type=pl.DeviceIdType.MESH)

        mine = pltpu.make_async_copy(x_ref, slot(*me), local_sem)
        mine.start()
        first = [copy(0, me, sibling, src=x_ref)] + [copy(1 + j, me, (*chip, ic), src=x_ref) for j, chip in enumerate(chips)]
        for cp in first:
            cp.start()
        passed = [copy(4 + j, (*chip, ic), sibling) for j, chip in enumerate(chips)]
        for j, chip in enumerate(chips):
            copy(1 + j, (*chip, ic), me).wait_recv()
            passed[j].start()
        copy(0, sibling, me).wait_recv()
        for j, chip in enumerate(chips):
            copy(4 + j, (*chip, 1 - ic), me).wait_recv()
        for cp in first + passed:
            cp.wait_send()
        mine.wait()

    return _call(
        body, name=name,
        out_shape=jax.ShapeDtypeStruct((NDEV,) + x.shape, x.dtype),
        in_specs=[pl.BlockSpec(memory_space=pl.ANY)],
        out_specs=pl.BlockSpec(memory_space=pl.ANY),
        scratch_shapes=[pltpu.SemaphoreType.DMA((NDEV - 1,)), pltpu.SemaphoreType.DMA((NDEV - 1,)), pltpu.SemaphoreType.DMA],
    )(x)


def _pack(arrs):
    parts = []
    for a in arrs:
        n = math.prod(a.shape)
        parts.append(jnp.pad(a.reshape(-1).astype(F32), (0, _pack_rows(n) * 128 - n)).reshape(-1, 128))
    rows = sum(p.shape[0] for p in parts)
    total = rows if rows <= 512 else -(-rows // 512) * 512
    if total > rows:
        parts.append(jnp.zeros((total - rows, 128), F32))
    return jnp.concatenate(parts, axis=0)


def _pack_rows(n):
    return -(-n // 1024) * 8


def _unpack(packed, shapes):
    out, row = [], 0
    for s in shapes:
        n = math.prod(s)
        rows = _pack_rows(n)
        out.append(packed[row:row + rows].reshape(-1)[:n].reshape(s))
        row += rows
    return out


def _sum8(parts, name):
    _, R, C = parts.shape
    TR = R
    for cand in (512, 256, 128, 64, 32, 16, 8):
        if R % cand == 0:
            TR = cand
            break

    def body(p_ref, o_ref):
        acc = p_ref[0]
        for d in range(1, NDEV):
            acc = acc + p_ref[d]
        o_ref[...] = acc

    return _call(
        body, name=name, grid=(R // TR,),
        in_specs=[pl.BlockSpec((NDEV, TR, C), lambda i: (0, i, 0))],
        out_specs=pl.BlockSpec((TR, C), lambda i: (i, 0)),
        out_shape=jax.ShapeDtypeStruct((R, C), F32),
        compiler_params=_params("arbitrary"),
    )(parts)


def _adamw(g_layers, w, m, v, name):
    L, R, C = w.shape
    g0 = g_layers[0]
    nparts = g0.shape[0] if g0.ndim == 3 else 0
    TR = R
    for cand in (256, 128, 64, 32, 16):
        if R % cand == 0 and cand * C <= 128 * 1024:
            TR = cand
            break

    def body(*refs):
        g_refs = refs[:L]
        w_ref, m_ref, v_ref, go_ref, d_ref, mo_ref, vo_ref = refs[L:]
        for ll in range(L):
            @pl.when(pl.program_id(0) == ll)
            def _(g_ref=g_refs[ll]):
                if nparts:
                    g = g_ref[0].astype(F32)
                    for d in range(1, nparts):
                        g = g + g_ref[d].astype(F32)
                else:
                    g = g_ref[...]
                m2 = ADAM_B1 * m_ref[0] + (1.0 - ADAM_B1) * g
                v2 = ADAM_B2 * v_ref[0] + (1.0 - ADAM_B2) * (g * g)
                m_hat = m2 / (1.0 - ADAM_B1 ** ADAM_STEP)
                v_hat = v2 / (1.0 - ADAM_B2 ** ADAM_STEP)
                go_ref[0] = g
                d_ref[0] = -ADAM_LR * (m_hat / (jnp.sqrt(v_hat) + ADAM_EPS) + ADAM_WD * w_ref[0])
                mo_ref[0] = m2
                vo_ref[0] = v2

    def gspec(ll):
        if nparts:
            return pl.BlockSpec((nparts, TR, C), lambda l, i: (0, jnp.where(l == ll, i, 0), 0))
        return pl.BlockSpec((TR, C), lambda l, i: (jnp.where(l == ll, i, 0), 0))

    spec = pl.BlockSpec((1, TR, C), lambda l, i: (l, i, 0))
    return _call(
        body, name=name, grid=(L, R // TR),
        in_specs=[gspec(ll) for ll in range(L)] + [spec, spec, spec], out_specs=[spec] * 4,
        out_shape=[jax.ShapeDtypeStruct((L, R, C), F32)] * 4,
        compiler_params=_params("arbitrary", "arbitrary"),
    )(*g_layers, w, m, v)


def _ada_cols(c_all, w_ada, b_cols):
    L, D, NS = w_ada.shape

    def body(c_ref, w_ref, b_ref, o_ref):
        cond = _silu(c_ref[...]).astype(BF16)
        o_ref[0] = _dot(cond, w_ref[0].astype(BF16)) + b_ref[0]

    return _call(
        body, name="ada_cols", grid=(L,),
        in_specs=[pl.BlockSpec((NDEV, D), lambda l: (0, 0)), pl.BlockSpec((1, D, NS), lambda l: (l, 0, 0)),
                  pl.BlockSpec((1, 1, NS), lambda l: (l, 0, 0))],
        out_specs=pl.BlockSpec((1, NDEV, NS), lambda l: (l, 0, 0)),
        out_shape=jax.ShapeDtypeStruct((L, NDEV, NS), F32),
        compiler_params=_params("arbitrary"),
    )(c_all, w_ada, b_cols)


def _ada_grads(c_all, dada_all, dada_cols):
    _, D = c_all.shape
    L, _, NS = dada_cols.shape
    D3 = dada_all.shape[2]

    def body(c_ref, da_ref, dc_ref, gw_ref, gb_ref):
        cond = _silu(c_ref[...]).astype(BF16)
        gw_ref[0] = _dot_tn(cond, dc_ref[0].astype(BF16))
        acc = da_ref[0]
        for d in range(1, NDEV):
            acc = acc + da_ref[d]
        gb_ref[...] = acc

    return _call(
        body, name="ada_grads", grid=(L,),
        in_specs=[pl.BlockSpec((NDEV, D), lambda l: (0, 0)), pl.BlockSpec((NDEV, L, D3), lambda l: (0, 0, 0)),
                  pl.BlockSpec((1, NDEV, NS), lambda l: (l, 0, 0))],
        out_specs=[pl.BlockSpec((1, D, NS), lambda l: (l, 0, 0)), pl.BlockSpec((L, D3), lambda l: (0, 0))],
        out_shape=[jax.ShapeDtypeStruct((L, D, NS), F32), jax.ShapeDtypeStruct((L, D3), F32)],
        compiler_params=_params("arbitrary"),
    )(c_all, dada_all, dada_cols)


def _tile(S, want):
    return want if S % want == 0 else S


def _modulate(x, mod):
    S, D = x.shape
    TS = _tile(S, TILE_STREAM)

    def body(x_ref, mod_ref, h_ref):
        h_ref[...] = (x_ref[...] * mod_ref[0:1, :] + mod_ref[1:2, :]).astype(BF16)

    spec = pl.BlockSpec((TS, D), lambda i: (i, 0))
    return _call(
        body, name="modulate", grid=(S // TS,),
        in_specs=[spec, pl.BlockSpec((8, D), lambda i: (0, 0))], out_specs=spec,
        out_shape=jax.ShapeDtypeStruct((S, D), BF16),
        compiler_params=_params("arbitrary"),
    )(x, mod)


def _peer(k):
    ix, iy, ic = lax.axis_index("x"), lax.axis_index("y"), lax.axis_index("c")
    bx, by, bc = (k >> 2) & 1, (k >> 1) & 1, k & 1
    px, py, pc = ix + bx - 2 * ix * bx, iy + by - 2 * iy * by, ic + bc - 2 * ic * bc
    return (px, py, pc), 4 * px + 2 * py + pc


GATHER_ORDER = (0, 1, 2, 4, 6, 3, 5, 7)
GATHER_DIRECT = (1, 2, 4, 6)
SCATTER_ORDER = (3, 2, 5, 4, 7, 6, 1, 0)


def _offset(order, k):
    off = jnp.int32(order[-1])
    for step in reversed(range(len(order) - 1)):
        off = jnp.where(k == step, jnp.int32(order[step]), off)
    return off


def _proj_gather(h, w_shard, extras, me1):
    assert GATHER_ORDER[0] == 0
    S, D = h.shape
    NS = w_shard.shape[1]
    TS = _tile(S, TILE_PROJ)
    nt = S // TS
    ne = len(extras)

    def body(me_ref, h_ref, w_ref, *rest):
        x_refs, o_ref, wg_ref = rest[:ne], rest[ne], rest[ne + 1]
        xg_refs = rest[ne + 2:2 * ne + 2]
        wbuf, send_sems, recv_sems, local_sems, load_sems = rest[2 * ne + 2:]
        k, i = pl.program_id(0), pl.program_id(1)
        me = me_ref[0]
        off = _offset(GATHER_ORDER, k)

        def push(src, dst, which, kk):
            peer, _ = _peer(kk)
            return pltpu.make_async_remote_copy(
                src_ref=src, dst_ref=dst.at[me], send_sem=send_sems.at[which, kk], recv_sem=recv_sems.at[which, kk],
                device_id=peer, device_id_type=pl.DeviceIdType.MESH)

        def own(src, dst, which):
            return pltpu.make_async_copy(src, dst.at[me], local_sems.at[which])

        pairs = [(w_ref, wg_ref)] + list(zip(x_refs, xg_refs))

        @pl.when(jnp.logical_and(k == 0, i == 0))
        def _():
            for which, (src, dst) in enumerate(pairs):
                own(src, dst, which).start()
                for kk in (GATHER_DIRECT if which == 0 else GATHER_ORDER[1:]):
                    push(src, dst, which, kk).start()
            first = pltpu.make_async_copy(w_ref, wbuf.at[0], load_sems.at[0])
            first.start()
            first.wait()

        @pl.when(jnp.logical_and(k > 0, i == 0))
        def _():
            push(w_ref, wg_ref, 0, off).wait_recv()
            _, blk = _peer(off)
            for step, o in enumerate(GATHER_ORDER):
                if o in GATHER_DIRECT and o > 1:
                    @pl.when(k == step)
                    def _(o=o):
                        sibling, _ = _peer(1)
                        pltpu.make_async_remote_copy(
                            src_ref=wg_ref.at[blk], dst_ref=wg_ref.at[blk], send_sem=send_sems.at[0, o + 1],
                            recv_sem=recv_sems.at[0, o + 1], device_id=sibling, device_id_type=pl.DeviceIdType.MESH).start()
            load = pltpu.make_async_copy(wg_ref.at[blk], wbuf.at[k % 2], load_sems.at[k % 2])
            load.start()
            load.wait()

        o_ref[...] = _dot(h_ref[...], wbuf[k % 2])

        @pl.when(jnp.logical_and(k == NDEV - 1, i == nt - 1))
        def _():
            for which, (src, dst) in enumerate(pairs):
                for kk in range(1, NDEV):
                    cp = push(src, dst, which, kk)
                    cp.wait_send()
                    if which > 0:
                        cp.wait_recv()
                own(src, dst, which).wait()

    def col(k, i, me_ref):
        m, off = me_ref[0], _offset(GATHER_ORDER, k)
        return i, (m | off) - (m & off)

    anyspec = pl.BlockSpec(memory_space=pl.ANY)
    grid_spec = pltpu.PrefetchScalarGridSpec(
        num_scalar_prefetch=1, grid=(NDEV, nt),
        in_specs=[pl.BlockSpec((TS, D), lambda k, i, me_ref: (i, 0)), anyspec] + [anyspec] * ne,
        out_specs=[pl.BlockSpec((TS, NS), col), anyspec] + [anyspec] * ne,
        scratch_shapes=[pltpu.VMEM((2, D, NS), BF16), pltpu.SemaphoreType.DMA((1 + ne, NDEV)), pltpu.SemaphoreType.DMA((1 + ne, NDEV)),
                        pltpu.SemaphoreType.DMA((1 + ne,)), pltpu.SemaphoreType.DMA((2,))])
    return _call(
        body, name="proj_gather", grid_spec=grid_spec,
        out_shape=[jax.ShapeDtypeStruct((S, NDEV * NS), F32), jax.ShapeDtypeStruct((NDEV, D, NS), BF16)]
        + [jax.ShapeDtypeStruct((NDEV,) + e.shape, e.dtype) for e in extras],
        compiler_params=_params("arbitrary", "arbitrary"),
    )(me1, h, w_shard, *extras)


def _dw_scatter(a, b, blk, by_rows, me1, name):
    S = a.shape[0]
    TK = _tile(S, TILE_DW)
    nk = S // TK
    shape = (blk, b.shape[1]) if by_rows else (a.shape[1], blk)
    last = NDEV - 1
    assert all(o & 1 for o in SCATTER_ORDER[0::2]) and not any(o & 1 for o in SCATTER_ORDER[1::2]) and SCATTER_ORDER[-1] == 0
    assert all(SCATTER_ORDER[s] == SCATTER_ORDER[s + 1] + 1 for s in range(0, NDEV, 2))

    def body(me_ref, a_ref, b_ref, recv_ref, acc, stage, sib, send_sems, recv_sems, local_sem):
        k, kk = pl.program_id(0), pl.program_id(1)

        def chip_of(step):
            return _offset(SCATTER_ORDER, step) >> 1

        def to_sibling(j, slot):
            sibling, _ = _peer(1)
            return pltpu.make_async_remote_copy(
                src_ref=stage.at[slot], dst_ref=sib.at[j], send_sem=send_sems.at[0, j], recv_sem=recv_sems.at[0, j],
                device_id=sibling, device_id_type=pl.DeviceIdType.MESH)

        def to_chip(j, slot):
            peer, _ = _peer(2 * j)
            return pltpu.make_async_remote_copy(
                src_ref=stage.at[slot], dst_ref=recv_ref.at[j], send_sem=send_sems.at[1, j], recv_sem=recv_sems.at[1, j],
                device_id=peer, device_id_type=pl.DeviceIdType.MESH)

        own = pltpu.make_async_copy(stage.at[last % 2], recv_ref.at[0], local_sem)

        @pl.when(kk == 0)
        def _():
            acc[...] = jnp.zeros_like(acc)
        acc[...] += _dot_tn(a_ref[...], b_ref[...])

        @pl.when(kk == nk - 1)
        def _():
            j = chip_of(k)
            even_step = k % 2 == 0

            @pl.when(jnp.logical_and(k >= 2, even_step))
            def _():
                to_sibling(chip_of(k - 2), k % 2).wait_send()

            @pl.when(jnp.logical_and(k >= 3, jnp.logical_not(even_step)))
            def _():
                to_chip(chip_of(k - 2), k % 2).wait_send()

            @pl.when(even_step)
            def _():
                stage[k % 2] = acc[...].astype(BF16)
                to_sibling(j, k % 2).start()

            @pl.when(jnp.logical_not(even_step))
            def _():
                to_sibling(j, 0).wait_recv()
                stage[k % 2] = (acc[...] + sib[j].astype(F32)).astype(BF16)

                @pl.when(k < last)
                def _():
                    to_chip(j, k % 2).start()

                @pl.when(k == last)
                def _():
                    own.start()
                    to_sibling(SCATTER_ORDER[last - 1] >> 1, (last - 1) % 2).wait_send()
                    own.wait()
                    for jj in range(1, NDEV // 2):
                        to_chip(jj, 0).wait_recv()

    def blk_of(k, me_ref):
        m, off = me_ref[0], _offset(SCATTER_ORDER, k)
        return (m | off) - (m & off)

    if by_rows:
        in_specs = [pl.BlockSpec((TK, blk), lambda k, kk, me_ref: (kk, blk_of(k, me_ref))),
                    pl.BlockSpec((TK, b.shape[1]), lambda k, kk, me_ref: (kk, 0))]
    else:
        in_specs = [pl.BlockSpec((TK, a.shape[1]), lambda k, kk, me_ref: (kk, 0)),
                    pl.BlockSpec((TK, blk), lambda k, kk, me_ref: (kk, blk_of(k, me_ref)))]
    grid_spec = pltpu.PrefetchScalarGridSpec(
        num_scalar_prefetch=1, grid=(NDEV, nk), in_specs=in_specs,
        out_specs=pl.BlockSpec(memory_space=pl.ANY),
        scratch_shapes=[pltpu.VMEM(shape, F32), pltpu.VMEM((2,) + shape, BF16), pltpu.VMEM((NDEV // 2,) + shape, BF16),
                        pltpu.SemaphoreType.DMA((2, NDEV // 2)), pltpu.SemaphoreType.DMA((2, NDEV // 2)), pltpu.SemaphoreType.DMA])
    return _call(
        body, name=name, grid_spec=grid_spec,
        out_shape=jax.ShapeDtypeStruct((NDEV // 2,) + shape, BF16),
        compiler_params=_params("arbitrary", "arbitrary"),
    )(me1, a, b)


def _chunk(TS, BR, k):
    return pl.BlockSpec((TS, BR), lambda i: (i, k))


def _halo_prev(TS, BR, k):
    return pl.BlockSpec((8, BR), lambda i: (jnp.maximum(i * (TS // 8) - 1, 0), k))


def _conv_a_fwd(proj, w8, BR):
    S = proj.shape[0]
    TS = _tile(S, TILE_STREAM)

    def body(ab_ref, ac_ref, ax_ref, ag_ref, hc_ref, hx_ref, w_ref, o_ref, ext):
        i = pl.program_id(0)
        u = ac_ref[...] * ax_ref[...]
        ext[0:8, :] = jnp.where(i > 0, hc_ref[...] * hx_ref[...], 0.0)
        ext[8:8 + TS, :] = u
        conv = ext[pl.ds(6, TS), :] * w_ref[0:1, :] + ext[pl.ds(7, TS), :] * w_ref[1:2, :] + u * w_ref[2:3, :]
        o_ref[...] = ab_ref[...] * conv * _silu(ag_ref[...])

    return _call(
        body, name="conv_a_fwd", grid=(S // TS,),
        in_specs=[_chunk(TS, BR, 0), _chunk(TS, BR, 1), _chunk(TS, BR, 2), _chunk(TS, BR, 3),
                  _halo_prev(TS, BR, 1), _halo_prev(TS, BR, 2), pl.BlockSpec((8, BR), lambda i: (0, 0))],
        out_specs=pl.BlockSpec((TS, BR), lambda i: (i, 0)),
        out_shape=jax.ShapeDtypeStruct((S, BR), F32),
        scratch_shapes=[pltpu.VMEM((TS + 8, BR), F32)],
        compiler_params=_params("arbitrary"),
    )(proj, proj, proj, proj, proj, proj, w8)


ATT_UNIT = 2048
ATT_LANES = 128


def _attn_mask():
    i = lax.broadcasted_iota(jnp.int32, (BLK, 2 * BLK), 0)
    j = lax.broadcasted_iota(jnp.int32, (BLK, 2 * BLK), 1)
    return jnp.logical_and(j >= i, j <= i + BLK), j >= BLK


def _attn_rows(n, r, dil):
    return pl.ds(n * BLK * dil + r, BLK, stride=dil) if dil > 1 else pl.ds(n * BLK, BLK)


def _attn_geometry(S, BR, dil):
    HD = BR // 8
    U = ATT_UNIT
    assert S % U == 0 and BR % ATT_LANES == 0 and ATT_LANES % HD == 0 and U % (BLK * dil) == 0
    return HD, U, S // U, U // (BLK * dil), ATT_LANES // HD, BR // ATT_LANES


def _attn_fwd(proj, bias, dil, BR):
    S = proj.shape[0]
    HD, U, NU, nbu, hpb, HBK = _attn_geometry(S, BR, dil)
    scale = HD ** -0.5
    combos = [(n, r) for n in range(nbu) for r in range(dil)]

    def body(q_ref, kc_ref, kp_ref, vc_ref, vp_ref, b_ref, o_ref, l_ref, sbuf, pbuf):
        m = pl.program_id(1)
        band, is_cur = _attn_mask()
        first = jnp.logical_and(band, jnp.logical_or(is_cur, m > 0))

        def keys(ref_c, ref_p, n, r):
            prev = ref_c[_attn_rows(n - 1, r, dil), :] if n > 0 else ref_p[_attn_rows(nbu - 1, r, dil), :]
            return jnp.concatenate([prev, ref_c[_attn_rows(n, r, dil), :]], axis=0).astype(BF16)

        for c, (n, r) in enumerate(combos):
            q = (q_ref[_attn_rows(n, r, dil), :] * scale).astype(BF16)
            kk = keys(kc_ref, kp_ref, n, r)
            for h in range(hpb):
                sl = slice(h * HD, (h + 1) * HD)
                sbuf[c * hpb + h] = _dot_nt(q[:, sl], kk[:, sl])
        for c, (n, r) in enumerate(combos):
            lses = []
            for h in range(hpb):
                s = jnp.where(band if n > 0 else first, sbuf[c * hpb + h] + b_ref[h], NEG)
                mx = jnp.max(s, axis=-1, keepdims=True)
                p = jnp.exp(s - mx)
                l = jnp.sum(p, axis=-1, keepdims=True)
                pbuf[c * hpb + h] = (p * (1.0 / l)).astype(BF16)
                lses.append(jnp.broadcast_to(mx + jnp.log(l), (BLK, HD)))
            l_ref[_attn_rows(n, r, dil), :] = jnp.concatenate(lses, axis=1)
        for c, (n, r) in enumerate(combos):
            vv = keys(vc_ref, vp_ref, n, r)
            o_ref[_attn_rows(n, r, dil), :] = jnp.concatenate(
                [_dot(pbuf[c * hpb + h], vv[:, h * HD:(h + 1) * HD]) for h in range(hpb)], axis=1)

    def cur(c):
        return pl.BlockSpec((U, ATT_LANES), lambda hb, m: (m, c * HBK + hb))

    def prev(c):
        return pl.BlockSpec((U, ATT_LANES), lambda hb, m: (jnp.maximum(m - 1, 0), c * HBK + hb))

    ospec = pl.BlockSpec((U, ATT_LANES), lambda hb, m: (m, hb))
    nhb = len(combos) * hpb
    return _call(
        body, name="attn_fwd_d%d" % dil, grid=(HBK, NU),
        in_specs=[cur(4), cur(5), prev(5), cur(6), prev(6), pl.BlockSpec((hpb, BLK, 2 * BLK), lambda hb, m: (hb, 0, 0))],
        out_specs=[ospec, ospec],
        out_shape=[jax.ShapeDtypeStruct((S, BR), F32)] * 2,
        scratch_shapes=[pltpu.VMEM((nhb, BLK, 2 * BLK), F32), pltpu.VMEM((nhb, BLK, 2 * BLK), BF16)],
        compiler_params=_params("arbitrary", "arbitrary"),
    )(proj, proj, proj, proj, proj, bias)


def _mix_weights(l1, l2, l3):
    m = jnp.maximum(jnp.maximum(l1, l2), l3)
    e1, e2, e3 = jnp.exp(l1 - m), jnp.exp(l2 - m), jnp.exp(l3 - m)
    inv = 1.0 / (e1 + e2 + e3)
    return e1 * inv, e2 * inv, e3 * inv


def _attn_mix_fwd(proj, os_, ls_, BR):
    S = proj.shape[0]
    TS = _tile(S, TILE_STREAM)

    def body(o1, o2, o3, l1, l2, l3, g_ref, y_ref):
        w1, w2, w3 = _mix_weights(l1[...], l2[...], l3[...])
        y_ref[...] = (w1 * o1[...] + w2 * o2[...] + w3 * o3[...]) * _silu(g_ref[...])

    spec = pl.BlockSpec((TS, BR), lambda i: (i, 0))
    return _call(
        body, name="attn_mix_fwd", grid=(S // TS,),
        in_specs=[spec] * 6 + [_chunk(TS, BR, 7)], out_specs=spec,
        out_shape=jax.ShapeDtypeStruct((S, BR), F32),
        compiler_params=_params("arbitrary"),
    )(*os_, *ls_, proj)


def _lru_gates(xc, wa, wx, ba, bx, sp):
    xb = xc.astype(BF16)
    r = _sigmoid(_dot(xb, wa) + ba)
    ig = _sigmoid(_dot(xb, wx) + bx)
    la = -LRU_C * r * sp
    a = jnp.exp(la)
    mult = jnp.sqrt(-_expm1(2.0 * la))
    return r, ig, a, mult


def _lru_fwd(proj, rows, wa, wx, BR):
    S = proj.shape[0]
    TS = _tile(S, TILE_RESIDENT)
    PAD = TS // 2

    def body(cx_ref, hx_ref, cg_ref, r_ref, wa_ref, wx_ref, y_ref, h_ref, ext, sa, sb, carry):
        i = pl.program_id(0)

        @pl.when(i == 0)
        def _():
            sa[0:PAD, :] = jnp.ones((PAD, BR), F32)
            sb[0:PAD, :] = jnp.zeros((PAD, BR), F32)
            carry[...] = jnp.zeros_like(carry)

        cx = cx_ref[...]
        ext[0:8, :] = jnp.where(i > 0, hx_ref[...], 0.0)
        ext[8:8 + TS, :] = cx
        xc = (ext[pl.ds(5, TS), :] * r_ref[0:1, :] + ext[pl.ds(6, TS), :] * r_ref[1:2, :]
              + ext[pl.ds(7, TS), :] * r_ref[2:3, :] + cx * r_ref[3:4, :] + r_ref[4:5, :])
        _, ig, a, mult = _lru_gates(xc, wa_ref[...], wx_ref[...], r_ref[5:6, :], r_ref[6:7, :], r_ref[7:8, :])
        sa[PAD:PAD + TS, :] = a
        sb[PAD:PAD + TS, :] = mult * ig * xc
        d = 1
        while d < TS:
            A, B = sa[PAD:PAD + TS, :], sb[PAD:PAD + TS, :]
            As, Bs = sa[pl.ds(PAD - d, TS), :], sb[pl.ds(PAD - d, TS), :]
            sb[PAD:PAD + TS, :] = A * Bs + B
            sa[PAD:PAD + TS, :] = A * As
            d *= 2
        h = sb[PAD:PAD + TS, :] + sa[PAD:PAD + TS, :] * carry[0:1, :]
        carry[0:1, :] = h[TS - 1:TS, :]
        h_ref[...] = h
        y_ref[...] = h * _silu(cg_ref[...])

    full = pl.BlockSpec((BR, BR), lambda i: (0, 0))
    spec = pl.BlockSpec((TS, BR), lambda i: (i, 0))
    return _call(
        body, name="lru_fwd", grid=(S // TS,),
        in_specs=[_chunk(TS, BR, 8), _halo_prev(TS, BR, 8), _chunk(TS, BR, 9), pl.BlockSpec((8, BR), lambda i: (0, 0)), full, full],
        out_specs=[spec, spec],
        out_shape=[jax.ShapeDtypeStruct((S, BR), F32)] * 2,
        scratch_shapes=[pltpu.VMEM((TS + 8, BR), F32), pltpu.VMEM((PAD + TS, BR), F32), pltpu.VMEM((PAD + TS, BR), F32),
                        pltpu.VMEM((8, BR), F32)],
        compiler_params=_params("arbitrary"),
    )(proj, proj, proj, rows, wa, wx)


S5_LAGS = 8
S5_CB = 8 * S5_CH
S5_SB = 8 * S5_STATE


def _s5_fwd(proj, a8, bbl, cre, cim, rows, wg, BR):
    S = proj.shape[0]
    W = a8.shape[1]
    TS = _tile(S, TILE_STREAM)
    CB, SB, J = S5_CB, S5_SB, S5_LAGS
    nblk = BR // CB

    def body(u_ref, uh_ref, dg_ref, a8_ref, bbl_ref, cre_ref, cim_ref, r_ref, wg_ref,
             y_ref, xre_ref, xim_ref, uext, ulag, sre, sim, ypre_s, carry):
        i = pl.program_id(0)

        @pl.when(i == 0)
        def _():
            carry[...] = jnp.zeros_like(carry)

        uext[0:8, :] = jnp.where(i > 0, uh_ref[...], 0.0)
        uext[8:8 + TS, :] = u_ref[...]
        for m in range(nblk):
            cs, ws = slice(m * CB, (m + 1) * CB), slice(m * SB, (m + 1) * SB)
            for j in range(J):
                ulag[:, j * CB:(j + 1) * CB] = uext[pl.ds(8 - j, TS), cs].astype(BF16)
            w = _dot(ulag[...], bbl_ref[m])
            sre[...] = w[:, 0:SB]
            sim[...] = w[:, SB:2 * SB]
            ar = jnp.broadcast_to(a8_ref[0:1, ws], (8, SB))
            ai = jnp.broadcast_to(a8_ref[1:2, ws], (8, SB))
            xr, xi = carry[0:8, ws], carry[8:16, ws]
            for g in range(TS // 8):
                rg = slice(8 * g, 8 * g + 8)
                xr, xi = sre[rg, :] + ar * xr - ai * xi, sim[rg, :] + ar * xi + ai * xr
                sre[rg, :] = xr
                sim[rg, :] = xi
            carry[0:8, ws] = xr
            carry[8:16, ws] = xi
            xre, xim = sre[...], sim[...]
            xre_ref[:, ws] = xre
            xim_ref[:, ws] = xim
            ypre_s[:, cs] = _dot(xre.astype(BF16), cre_ref[ws, :]) - _dot(xim.astype(BF16), cim_ref[ws, :])
        dg = dg_ref[...]
        yg = _gelu(ypre_s[...] + r_ref[0:1, :] * u_ref[...])
        s = _sigmoid(_dot(yg.astype(BF16), wg_ref[...]) + r_ref[1:2, :])
        y_ref[...] = yg * s * _silu(dg)

    def const(shape):
        return pl.BlockSpec(shape, lambda i: (0,) * len(shape))

    return _call(
        body, name="s5_fwd", grid=(S // TS,),
        in_specs=[_chunk(TS, BR, 10), _halo_prev(TS, BR, 10), _chunk(TS, BR, 11), const((8, W)), const((nblk, J * CB, 2 * SB)),
                  const((W, CB)), const((W, CB)), const((8, BR)), const((BR, BR))],
        out_specs=[pl.BlockSpec((TS, BR), lambda i: (i, 0)), pl.BlockSpec((TS, W), lambda i: (i, 0)), pl.BlockSpec((TS, W), lambda i: (i, 0))],
        out_shape=[jax.ShapeDtypeStruct((S, BR), F32), jax.ShapeDtypeStruct((S, W), F32), jax.ShapeDtypeStruct((S, W), F32)],
        scratch_shapes=[pltpu.VMEM((TS + 8, BR), F32), pltpu.VMEM((TS, J * CB), BF16), pltpu.VMEM((TS, SB), F32), pltpu.VMEM((TS, SB), F32),
                        pltpu.VMEM((TS, BR), F32), pltpu.VMEM((16, W), F32)],
        compiler_params=_params("arbitrary"),
    )(proj, proj, proj, a8, bbl, cre, cim, rows, wg)


def _out_ln(ys, w_out, x, rows, target=None):
    S, D = x.shape
    BR = D // 4
    TS = _tile(S, TILE_RESIDENT)
    last = target is not None

    def body(ya, yb, yc, yd, w_ref, x_ref, r_ref, *rest):
        t_ref = rest[0] if last else None
        o_ref, y_ref, cat_ref = rest[1:4] if last else rest[0:3]
        cat = jnp.concatenate([ya[...], yb[...], yc[...], yd[...]], axis=1).astype(BF16)
        y = _dot(cat, w_ref[...])
        z = ALPHA * x_ref[...] + r_ref[0:1, :] * y
        mu = jnp.mean(z, axis=-1, keepdims=True)
        zc = z - mu
        var = jnp.mean(zc * zc, axis=-1, keepdims=True)
        xn = zc * lax.rsqrt(var + LN_EPS) * r_ref[1:2, :] + r_ref[2:3, :]
        y_ref[...] = y
        cat_ref[...] = cat
        if last:
            l_ref = rest[4]

            @pl.when(pl.program_id(0) == 0)
            def _():
                l_ref[...] = jnp.zeros_like(l_ref)
            diff = xn - t_ref[...]
            o_ref[...] = diff * (1.0 / D)
            l_ref[...] += jnp.sum(_colsum(diff * diff), axis=1, keepdims=True) * (0.5 / D)
        else:
            o_ref[...] = xn

    yspec = pl.BlockSpec((TS, BR), lambda i: (i, 0))
    spec = pl.BlockSpec((TS, D), lambda i: (i, 0))
    lspec = pl.BlockSpec((1, 1), lambda i: (0, 0))
    return _call(
        body, name="out_ln_loss" if last else "out_ln", grid=(S // TS,),
        in_specs=[yspec] * 4 + [pl.BlockSpec((D, D), lambda i: (0, 0)), spec, pl.BlockSpec((8, D), lambda i: (0, 0))] + [spec] * last,
        out_specs=[spec, spec, spec] + [lspec] * last,
        out_shape=[jax.ShapeDtypeStruct((S, D), F32), jax.ShapeDtypeStruct((S, D), F32), jax.ShapeDtypeStruct((S, D), BF16)]
        + [jax.ShapeDtypeStruct((1, 1), F32)] * last,
        compiler_params=_params("arbitrary"),
    )(*ys, w_out, x, rows, *([target] if last else []))


def _ln_bwd(dout, x, y, rows, w_out):
    S, D = x.shape
    TS = _tile(S, TILE_RESIDENT)

    def body(do_ref, x_ref, y_ref, r_ref, w_ref, dxr_ref, dyb_ref, dcat_ref, acc_ref):
        @pl.when(pl.program_id(0) == 0)
        def _():
            acc_ref[...] = jnp.zeros_like(acc_ref)
        g1, lg = r_ref[0:1, :], r_ref[1:2, :]
        yv = y_ref[...]
        z = ALPHA * x_ref[...] + g1 * yv
        mu = jnp.mean(z, axis=-1, keepdims=True)
        zc = z - mu
        var = jnp.mean(zc * zc, axis=-1, keepdims=True)
        rstd = lax.rsqrt(var + LN_EPS)
        xhat = zc * rstd
        do = do_ref[...]
        dxh = do * lg
        dz = rstd * (dxh - jnp.mean(dxh, axis=-1, keepdims=True) - xhat * jnp.mean(dxh * xhat, axis=-1, keepdims=True))
        dxr_ref[...] = ALPHA * dz
        dyb = (g1 * dz).astype(BF16)
        dyb_ref[...] = dyb
        dcat_ref[...] = _dot_nt(dyb, w_ref[...])
        acc_ref[0:1, :] += _colsum(do * xhat)
        acc_ref[1:2, :] += _colsum(do)
        acc_ref[2:3, :] += _colsum(dz * yv)

    spec = pl.BlockSpec((TS, D), lambda i: (i, 0))
    rspec = pl.BlockSpec((8, D), lambda i: (0, 0))
    return _call(
        body, name="ln_bwd", grid=(S // TS,),
        in_specs=[spec, spec, spec, rspec, pl.BlockSpec((D, D), lambda i: (0, 0))],
        out_specs=[spec, spec, spec, rspec],
        out_shape=[jax.ShapeDtypeStruct((S, D), F32), jax.ShapeDtypeStruct((S, D), BF16), jax.ShapeDtypeStruct((S, D), F32),
                   jax.ShapeDtypeStruct((8, D), F32)],
        compiler_params=_params("arbitrary"),
    )(dout, x, y, rows, w_out)


def _dh(dproj, wg, dxres, x, mod, sides):
    S, D = x.shape
    NS = wg.shape[2]
    TS = _tile(S, TILE_RESIDENT)
    nt = S // TS
    ns = len(sides)

    def body(dp_ref, w_hbm, dxr_ref, x_ref, mod_ref, *rest):
        s_refs, dx_ref, sum_ref, so_refs = rest[:ns], rest[ns], rest[ns + 1], rest[ns + 2:2 * ns + 2]
        w_vmem, sem, send_sems, recv_sems, local_sems = rest[2 * ns + 2:]
        i = pl.program_id(0)
        ix, iy, ic = lax.axis_index("x"), lax.axis_index("y"), lax.axis_index("c")
        me = 4 * ix + 2 * iy + ic

        def exchange(which):
            (_, gather), x_side, o_side = sides[which], s_refs[which], so_refs[which]
            copies = [pltpu.make_async_copy(x_side if gather else x_side.at[me], o_side.at[me], local_sems.at[which])]
            for k in range(1, NDEV):
                px = 1 - ix if k & 4 else ix
                py = 1 - iy if k & 2 else iy
                pc = 1 - ic if k & 1 else ic
                copies.append(pltpu.make_async_remote_copy(
                    src_ref=x_side if gather else x_side.at[4 * px + 2 * py + pc], dst_ref=o_side.at[me],
                    send_sem=send_sems.at[which, k - 1], recv_sem=recv_sems.at[which, k - 1],
                    device_id=(px, py, pc), device_id_type=pl.DeviceIdType.MESH))
            return copies

        @pl.when(i == 0)
        def _():
            load = pltpu.make_async_copy(w_hbm, w_vmem, sem)
            load.start()
            for which in range(ns):
                for cp in exchange(which):
                    cp.start()
            sum_ref[...] = jnp.zeros_like(sum_ref)
            load.wait()

        dh = _dot_nt(dp_ref[:, 0:NS], w_vmem[0])
        for j in range(1, NDEV):
            dh = dh + _dot_nt(dp_ref[:, j * NS:(j + 1) * NS], w_vmem[j])
        dx_ref[...] = dxr_ref[...] + dh * mod_ref[0:1, :]
        sum_ref[0:1, :] += _colsum(dh * x_ref[...])
        sum_ref[1:2, :] += _colsum(dh)

        @pl.when(i == nt - 1)
        def _():
            for which in range(ns):
                for cp in exchange(which):
                    cp.wait()

    spec = pl.BlockSpec((TS, D), lambda i: (i, 0))
    rspec = pl.BlockSpec((8, D), lambda i: (0, 0))
    anyspec = pl.BlockSpec(memory_space=pl.ANY)
    side_shapes = [jax.ShapeDtypeStruct(((NDEV,) + a.shape) if g else a.shape, a.dtype) for a, g in sides]
    return _call(
        body, name="dh", grid=(nt,),
        in_specs=[pl.BlockSpec((TS, NDEV * NS), lambda i: (i, 0)), anyspec, spec, spec, rspec] + [anyspec] * ns,
        out_specs=[spec, rspec] + [anyspec] * ns,
        out_shape=[jax.ShapeDtypeStruct((S, D), F32), jax.ShapeDtypeStruct((8, D), F32)] + side_shapes,
        scratch_shapes=[pltpu.VMEM((NDEV, D, NS), BF16), pltpu.SemaphoreType.DMA, pltpu.SemaphoreType.DMA((max(ns, 1), NDEV - 1)),
                        pltpu.SemaphoreType.DMA((max(ns, 1), NDEV - 1)), pltpu.SemaphoreType.DMA((max(ns, 1),))],
        compiler_params=_params("arbitrary"),
    )(dproj, wg, dxres, x, mod, *[a for a, _ in sides])


def _halo_next(S, TS, BR, k):
    last = S // 8 - 1
    return pl.BlockSpec((8, BR), lambda i: (jnp.minimum((i + 1) * (TS // 8), last), k))


def _conv_a_bwd(proj, dcat, w8, BR):
    S = proj.shape[0]
    TS = _tile(S, TILE_STREAM)
    nt = S // TS

    def body(ab_ref, ac_ref, ax_ref, ag_ref, dy_ref, hc_ref, hx_ref, nab_ref, nag_ref, ndy_ref, w_ref, dp_ref, dw_ref, ext, dext):
        i = pl.program_id(0)

        @pl.when(i == 0)
        def _():
            dw_ref[...] = jnp.zeros_like(dw_ref)

        w0, w1, w2 = w_ref[0:1, :], w_ref[1:2, :], w_ref[2:3, :]
        ab, ac, ax, ag, dy = ab_ref[...], ac_ref[...], ax_ref[...], ag_ref[...], dy_ref[...]
        u = ac * ax
        ext[0:8, :] = jnp.where(i > 0, hc_ref[...] * hx_ref[...], 0.0)
        ext[8:8 + TS, :] = u
        u1, u2 = ext[pl.ds(7, TS), :], ext[pl.ds(6, TS), :]
        conv = u2 * w0 + u1 * w1 + u * w2
        sg = _silu(ag)
        dconv = dy * ab * sg
        dext[0:TS, :] = dconv
        dext[TS:TS + 8, :] = jnp.where(i < nt - 1, ndy_ref[...] * nab_ref[...] * _silu(nag_ref[...]), 0.0)
        du = w2 * dconv + w1 * dext[pl.ds(1, TS), :] + w0 * dext[pl.ds(2, TS), :]
        dp_ref[:, 0:BR] = (dy * conv * sg).astype(BF16)
        dp_ref[:, BR:2 * BR] = (du * ax).astype(BF16)
        dp_ref[:, 2 * BR:3 * BR] = (du * ac).astype(BF16)
        dp_ref[:, 3 * BR:4 * BR] = (dy * ab * conv * _dsilu(ag)).astype(BF16)
        dw_ref[0:1, :] += _colsum(dconv * u2)
        dw_ref[1:2, :] += _colsum(dconv * u1)
        dw_ref[2:3, :] += _colsum(dconv * u)

    rspec = pl.BlockSpec((8, BR), lambda i: (0, 0))
    return _call(
        body, name="conv_a_bwd", grid=(nt,),
        in_specs=[_chunk(TS, BR, 0), _chunk(TS, BR, 1), _chunk(TS, BR, 2), _chunk(TS, BR, 3), _chunk(TS, BR, 0),
                  _halo_prev(TS, BR, 1), _halo_prev(TS, BR, 2), _halo_next(S, TS, BR, 0), _halo_next(S, TS, BR, 3),
                  _halo_next(S, TS, BR, 0), rspec],
        out_specs=[pl.BlockSpec((TS, 4 * BR), lambda i: (i, 0)), rspec],
        out_shape=[jax.ShapeDtypeStruct((S, 12 * BR), BF16), jax.ShapeDtypeStruct((8, BR), F32)],
        scratch_shapes=[pltpu.VMEM((TS + 8, BR), F32), pltpu.VMEM((TS + 8, BR), F32)],
        compiler_params=_params("arbitrary"),
    )(proj, proj, proj, proj, dcat, proj, proj, proj, proj, dcat, w8)


def _attn_mix_bwd(proj, dcat, os_, ls_, head_ones, BR):
    S = proj.shape[0]
    TS = _tile(S, TILE_STREAM)

    def body(dy_ref, g_ref, o1, o2, o3, l1, l2, l3, ones_ref, d1, d2, d3, e1, e2, e3, dg_ref):
        w1, w2, w3 = _mix_weights(l1[...], l2[...], l3[...])
        mix = w1 * o1[...] + w2 * o2[...] + w3 * o3[...]
        g = g_ref[...]
        dy = dy_ref[...]
        dmix = dy * _silu(g)
        dg_ref[...] = dy * mix * _dsilu(g)
        t = jnp.dot(dmix * mix, ones_ref[...], preferred_element_type=F32, precision=lax.Precision.HIGHEST)
        d1[...] = w1 * dmix
        d2[...] = w2 * dmix
        d3[...] = w3 * dmix
        e1[...] = -w1 * t
        e2[...] = -w2 * t
        e3[...] = -w3 * t

    spec = pl.BlockSpec((TS, BR), lambda i: (i, 0))
    outs = _call(
        body, name="attn_mix_bwd", grid=(S // TS,),
        in_specs=[_chunk(TS, BR, 1), _chunk(TS, BR, 7)] + [spec] * 6 + [pl.BlockSpec((BR, BR), lambda i: (0, 0))],
        out_specs=[spec] * 7,
        out_shape=[jax.ShapeDtypeStruct((S, BR), F32)] * 7,
        compiler_params=_params("arbitrary"),
    )(dcat, proj, *os_, *ls_, head_ones)
    return outs[0:3], outs[3:6], outs[6]


def _attn_bwd(proj, do, e, lse, bias, dil, BR):
    S = proj.shape[0]
    HD, U, NU, nbu, hpb, HBK = _attn_geometry(S, BR, dil)
    scale = HD ** -0.5
    combos = [(n, r) for n in range(nbu) for r in range(dil)]

    def body(q_ref, kc_ref, kp_ref, vc_ref, vp_ref, do_ref, e_ref, l_ref, b_ref, dq_ref, dk_ref, dv_ref, db_ref,
             sbuf, dpbuf, pbuf, dsbuf, qs, dos, kks, dkacc, dvacc, nxk, nxv, cark, carv):
        j = pl.program_id(1)
        mu = NU - 1 - j
        band, is_cur = _attn_mask()
        first = jnp.logical_and(band, jnp.logical_or(is_cur, mu > 0))

        @pl.when(j == 0)
        def _():
            cark[...] = jnp.zeros_like(cark)
            carv[...] = jnp.zeros_like(carv)
            db_ref[...] = jnp.zeros_like(db_ref)

        if nbu > 1:
            nxk[...] = jnp.zeros_like(nxk)
            nxv[...] = jnp.zeros_like(nxv)

        def keys(ref_c, ref_p, n, r):
            prev = ref_c[_attn_rows(n - 1, r, dil), :] if n > 0 else ref_p[_attn_rows(nbu - 1, r, dil), :]
            return jnp.concatenate([prev, ref_c[_attn_rows(n, r, dil), :]], axis=0).astype(BF16)

        for c, (n, r) in enumerate(combos):
            rows = _attn_rows(n, r, dil)
            q = (q_ref[rows, :] * scale).astype(BF16)
            dob = do_ref[rows, :].astype(BF16)
            kk, vv = keys(kc_ref, kp_ref, n, r), keys(vc_ref, vp_ref, n, r)
            qs[c], dos[c], kks[c] = q, dob, kk
            for h in range(hpb):
                sl = slice(h * HD, (h + 1) * HD)
                sbuf[c * hpb + h] = _dot_nt(q[:, sl], kk[:, sl])
                dpbuf[c * hpb + h] = _dot_nt(dob[:, sl], vv[:, sl])
        dbs = [None] * hpb
        for c, (n, r) in enumerate(combos):
            rows = _attn_rows(n, r, dil)
            lse, ev = l_ref[rows, :], e_ref[rows, :]
            for h in range(hpb):
                one = slice(h * HD, h * HD + 1)
                s = jnp.where(band if n > 0 else first, sbuf[c * hpb + h] + b_ref[h], NEG)
                p = jnp.exp(s - lse[:, one])
                ds = p * (dpbuf[c * hpb + h] + ev[:, one])
                pbuf[c * hpb + h] = p.astype(BF16)
                dsbuf[c * hpb + h] = ds.astype(BF16)
                dbs[h] = ds if dbs[h] is None else dbs[h] + ds
        for h in range(hpb):
            db_ref[h] += dbs[h]
        for c, (n, r) in enumerate(combos):
            rows = _attn_rows(n, r, dil)
            q, dob, kk = qs[c], dos[c], kks[c]
            dq, dkk, dvv = [], [], []
            for h in range(hpb):
                sl = slice(h * HD, (h + 1) * HD)
                dsb = dsbuf[c * hpb + h]
                dq.append(_dot(dsb, kk[:, sl]) * scale)
                dkk.append(_dot_tn(dsb, q[:, sl]))
                dvv.append(_dot_tn(pbuf[c * hpb + h], dob[:, sl]))
            dq_ref[rows, :] = jnp.concatenate(dq, axis=1)
            dkk, dvv = jnp.concatenate(dkk, axis=1), jnp.concatenate(dvv, axis=1)
            dkacc[rows, :] = dkk[BLK:2 * BLK, :]
            dvacc[rows, :] = dvv[BLK:2 * BLK, :]
            if n > 0:
                prow = _attn_rows(n - 1, r, dil)
                dkacc[prow, :] += dkk[0:BLK, :]
                dvacc[prow, :] += dvv[0:BLK, :]
            else:
                prow = _attn_rows(nbu - 1, r, dil)
                nxk[prow, :] = dkk[0:BLK, :]
                nxv[prow, :] = dvv[0:BLK, :]
        dk_ref[...] = dkacc[...] + cark[...]
        dv_ref[...] = dvacc[...] + carv[...]
        cark[...] = nxk[...]
        carv[...] = nxv[...]

    def cur(c):
        return pl.BlockSpec((U, ATT_LANES), lambda hb, j: (NU - 1 - j, c * HBK + hb))

    def prev(c):
        return pl.BlockSpec((U, ATT_LANES), lambda hb, j: (jnp.maximum(NU - 2 - j, 0), c * HBK + hb))

    own = pl.BlockSpec((U, ATT_LANES), lambda hb, j: (NU - 1 - j, hb))
    bspec = pl.BlockSpec((hpb, BLK, 2 * BLK), lambda hb, j: (hb, 0, 0))
    nhb = len(combos) * hpb
    unit = pltpu.VMEM((U, ATT_LANES), F32)
    return _call(
        body, name="attn_bwd_d%d" % dil, grid=(HBK, NU),
        in_specs=[cur(4), cur(5), prev(5), cur(6), prev(6), own, own, own, bspec],
        out_specs=[own, own, own, bspec],
        out_shape=[jax.ShapeDtypeStruct((S, BR), F32)] * 3 + [jax.ShapeDtypeStruct((8, BLK, 2 * BLK), F32)],
        scratch_shapes=[pltpu.VMEM((nhb, BLK, 2 * BLK), F32)] * 2 + [pltpu.VMEM((nhb, BLK, 2 * BLK), BF16)] * 2
        + [pltpu.VMEM((len(combos), BLK, ATT_LANES), BF16)] * 2 + [pltpu.VMEM((len(combos), 2 * BLK, ATT_LANES), BF16)] + [unit] * 6,
        compiler_params=_params("arbitrary", "arbitrary"),
    )(proj, proj, proj, proj, proj, do, e, lse, bias)


def _attn_dsum(ds, dbg, dproj, BR):
    S = dbg.shape[0]
    TS = _tile(S, TILE_STREAM)

    def body(*refs):
        o_ref = refs[11]
        for k in range(3):
            o_ref[:, k * BR:(k + 1) * BR] = (refs[k][...] + refs[3 + k][...] + refs[6 + k][...]).astype(BF16)
        o_ref[:, 3 * BR:4 * BR] = refs[9][...].astype(BF16)

    spec = pl.BlockSpec((TS, BR), lambda i: (i, 0))
    return _call(
        body, name="attn_dsum", grid=(S // TS,),
        in_specs=[spec] * 10 + [pl.BlockSpec(memory_space=pl.ANY)],
        out_specs=pl.BlockSpec((TS, 4 * BR), lambda i: (i, 1)),
        out_shape=jax.ShapeDtypeStruct(dproj.shape, BF16),
        input_output_aliases={10: 0},
        compiler_params=_params("arbitrary"),
    )(*[t for trip in ds for t in trip], dbg, dproj)


def _lru_bwd(proj, h, dcat, rows, wa, wx, dproj, BR):
    S = proj.shape[0]
    TS = _tile(S, TILE_RESIDENT)
    PAD = TS // 2
    nt = S // TS

    def body(cx_ref, hx_ref, cg_ref, h_ref, hh_ref, dy_ref, r_ref, wa_ref, wx_ref, _,
             dp_ref, sum_ref, dwa_ref, dwx_ref, ext, hext, aext, dext, sa, sb, carry):
        i = pl.program_id(0)
        ti = nt - 1 - i

        @pl.when(i == 0)
        def _():
            sa[TS:TS + PAD, :] = jnp.ones((PAD, BR), F32)
            sb[TS:TS + PAD, :] = jnp.zeros((PAD, BR), F32)
            carry[...] = jnp.zeros_like(carry)
            dext[TS:TS + 8, :] = jnp.zeros((8, BR), F32)
            sum_ref[...] = jnp.zeros_like(sum_ref)
            dwa_ref[...] = jnp.zeros_like(dwa_ref)
            dwx_ref[...] = jnp.zeros_like(dwx_ref)

        w0, w1, w2, w3 = r_ref[0:1, :], r_ref[1:2, :], r_ref[2:3, :], r_ref[3:4, :]
        sp = r_ref[7:8, :]
        cx = cx_ref[...]
        ext[0:8, :] = jnp.where(ti > 0, hx_ref[...], 0.0)
        ext[8:8 + TS, :] = cx
        x3, x2, x1 = ext[pl.ds(5, TS), :], ext[pl.ds(6, TS), :], ext[pl.ds(7, TS), :]
        xc = x3 * w0 + x2 * w1 + x1 * w2 + cx * w3 + r_ref[4:5, :]
        wa_, wx_ = wa_ref[...], wx_ref[...]
        r, ig, a, mult = _lru_gates(xc, wa_, wx_, r_ref[5:6, :], r_ref[6:7, :], sp)
        cg, dy, hv = cg_ref[...], dy_ref[...], h_ref[...]
        dp_ref[:, BR:2 * BR] = (dy * hv * _dsilu(cg)).astype(BF16)
        aext[0:TS, :] = a
        aext[TS:TS + 8, :] = jnp.broadcast_to(carry[0:1, :], (8, BR))
        sa[0:TS, :] = aext[pl.ds(1, TS), :]
        sb[0:TS, :] = dy * _silu(cg)
        d = 1
        while d < TS:
            A, B = sa[0:TS, :], sb[0:TS, :]
            As, Bs = sa[pl.ds(d, TS), :], sb[pl.ds(d, TS), :]
            sb[0:TS, :] = B + A * Bs
            sa[0:TS, :] = A * As
            d *= 2
        gh = sb[0:TS, :] + sa[0:TS, :] * carry[1:2, :]
        carry[0:1, :] = a[0:1, :]
        carry[1:2, :] = gh[0:1, :]
        hext[0:8, :] = jnp.where(ti > 0, hh_ref[...], 0.0)
        hext[8:8 + TS, :] = hv
        da = gh * hext[pl.ds(7, TS), :]
        dmult = gh * ig * xc
        dig = gh * mult * xc
        dxc = gh * mult * ig
        dla = da * a - dmult * a * a / mult
        dpr = dla * (-LRU_C) * sp * r * (1.0 - r)
        dpi = dig * ig * (1.0 - ig)
        xb, dprb, dpib = xc.astype(BF16), dpr.astype(BF16), dpi.astype(BF16)
        dwa_ref[...] += _dot_tn(xb, dprb)
        dwx_ref[...] += _dot_tn(xb, dpib)
        dxc = dxc + _dot_nt(dprb, wa_) + _dot_nt(dpib, wx_)
        sum_ref[0:1, :] += _colsum(dxc * x3)
        sum_ref[1:2, :] += _colsum(dxc * x2)
        sum_ref[2:3, :] += _colsum(dxc * x1)
        sum_ref[3:4, :] += _colsum(dxc * cx)
        sum_ref[4:5, :] += _colsum(dxc)
        sum_ref[5:6, :] += _colsum(dpr)
        sum_ref[6:7, :] += _colsum(dpi)
        sum_ref[7:8, :] += _colsum(dla * (-LRU_C) * r)
        dext[0:TS, :] = dxc
        dcx = w3 * dxc + w2 * dext[pl.ds(1, TS), :] + w1 * dext[pl.ds(2, TS), :] + w0 * dext[pl.ds(3, TS), :]
        dext[TS:TS + 8, :] = dxc[0:8, :]
        dp_ref[:, 0:BR] = dcx.astype(BF16)

    def rev(k):
        return pl.BlockSpec((TS, BR), lambda i: (nt - 1 - i, k))

    def rev_halo(k):
        return pl.BlockSpec((8, BR), lambda i: (jnp.maximum((nt - 1 - i) * (TS // 8) - 1, 0), k))

    full = pl.BlockSpec((BR, BR), lambda i: (0, 0))
    rspec = pl.BlockSpec((8, BR), lambda i: (0, 0))
    return _call(
        body, name="lru_bwd", grid=(nt,),
        in_specs=[rev(8), rev_halo(8), rev(9), rev(0), rev_halo(0), rev(2), rspec, full, full, pl.BlockSpec(memory_space=pl.ANY)],
        out_specs=[pl.BlockSpec((TS, 2 * BR), lambda i: (nt - 1 - i, 4)), rspec, full, full],
        input_output_aliases={9: 0},
        out_shape=[jax.ShapeDtypeStruct(dproj.shape, BF16), jax.ShapeDtypeStruct((8, BR), F32),
                   jax.ShapeDtypeStruct((BR, BR), F32), jax.ShapeDtypeStruct((BR, BR), F32)],
        scratch_shapes=[pltpu.VMEM((TS + 8, BR), F32)] * 4 + [pltpu.VMEM((TS + PAD, BR), F32)] * 2 + [pltpu.VMEM((8, BR), F32)],
        compiler_params=_params("arbitrary"),
    )(proj, proj, proj, h, h, dcat, rows, wa, wx, dproj)


def _s5_bwd(proj, dcat, xre_all, xim_all, a8, cl, bbre, bbim, cre, cim, rows, wg, dproj, BR):
    S = proj.shape[0]
    W = a8.shape[1]
    TS = _tile(S, TILE_STREAM)
    nt = S // TS
    CB, SB, J = S5_CB, S5_SB, S5_LAGS
    nblk = BR // CB

    def body(u_ref, dg_ref, dy_ref, xre_ref, xim_ref, hre_ref, him_ref, a8_ref, cl_ref, bbre_ref, bbim_ref, cre_ref, cim_ref,
             r_ref, wg_ref, _, dp_ref, sum_ref, dwg_ref, da_ref, dbbre_ref, dbbim_ref, dcre_ref, dcim_ref,
             sre, sim, ere, eim, ypre_s, dext, dylag, carry):
        i = pl.program_id(0)
        ti = nt - 1 - i

        @pl.when(i == 0)
        def _():
            dext[TS:TS + 8, :] = jnp.zeros((8, BR), F32)
            carry[...] = jnp.zeros_like(carry)
            for ref in (sum_ref, dwg_ref, da_ref, dbbre_ref, dbbim_ref, dcre_ref, dcim_ref):
                ref[...] = jnp.zeros_like(ref)

        u, dg, dy = u_ref[...], dg_ref[...], dy_ref[...]
        for m in range(nblk):
            cs, ws = slice(m * CB, (m + 1) * CB), slice(m * SB, (m + 1) * SB)
            ypre_s[:, cs] = (_dot(xre_ref[:, ws].astype(BF16), cre_ref[ws, :]) - _dot(xim_ref[:, ws].astype(BF16), cim_ref[ws, :]))
        ypre = ypre_s[...] + r_ref[0:1, :] * u
        yg = _gelu(ypre)
        wg_ = wg_ref[...]
        s = _sigmoid(_dot(yg.astype(BF16), wg_) + r_ref[1:2, :])
        dgl = dy * _silu(dg)
        dp_ref[:, BR:2 * BR] = (dy * yg * s * _dsilu(dg)).astype(BF16)
        dps = dgl * yg * s * (1.0 - s)
        dpsb = dps.astype(BF16)
        sum_ref[1:2, :] += _colsum(dps)
        dwg_ref[...] += _dot_tn(yg.astype(BF16), dpsb)
        dyp = (dgl * s + _dot_nt(dpsb, wg_)) * _dgelu(ypre)
        sum_ref[0:1, :] += _colsum(dyp * u)
        dext[0:TS, :] = dyp
        for m in range(nblk):
            cs, ws = slice(m * CB, (m + 1) * CB), slice(m * SB, (m + 1) * SB)
            dypb = dext[0:TS, cs].astype(BF16)
            xre, xim = xre_ref[:, ws], xim_ref[:, ws]
            dcre_ref[ws, :] += _dot_tn(xre.astype(BF16), dypb)
            dcim_ref[ws, :] -= _dot_tn(xim.astype(BF16), dypb)
            for j in range(J):
                dylag[:, j * CB:(j + 1) * CB] = dext[pl.ds(j, TS), cs].astype(BF16)
            w = _dot(dylag[...], cl_ref[m])
            sre[...] = w[:, 0:SB]
            sim[...] = w[:, SB:2 * SB]
            ar = jnp.broadcast_to(a8_ref[0:1, ws], (8, SB))
            ai = jnp.broadcast_to(a8_ref[1:2, ws], (8, SB))
            gr, gi = carry[0:8, ws], carry[8:16, ws]
            for g in reversed(range(TS // 8)):
                rg = slice(8 * g, 8 * g + 8)
                gr, gi = sre[rg, :] + ar * gr + ai * gi, sim[rg, :] + ar * gi - ai * gr
                sre[rg, :] = gr
                sim[rg, :] = gi
            carry[0:8, ws] = gr
            carry[8:16, ws] = gi
            gre, gim = sre[...], sim[...]
            ere[0:8, :] = jnp.where(ti > 0, hre_ref[:, ws], 0.0)
            eim[0:8, :] = jnp.where(ti > 0, him_ref[:, ws], 0.0)
            ere[8:8 + TS, :] = xre
            eim[8:8 + TS, :] = xim
            xpr, xpi = ere[pl.ds(7, TS), :], eim[pl.ds(7, TS), :]
            da_ref[0:1, ws] += _colsum(gre * xpr + gim * xpi)
            da_ref[1:2, ws] += _colsum(gim * xpr - gre * xpi)
            greb, gimb = gre.astype(BF16), gim.astype(BF16)
            ub = u_ref[:, cs].astype(BF16)
            dbbre_ref[cs, :] += _dot_tn(ub, greb)
            dbbim_ref[cs, :] += _dot_tn(ub, gimb)
            du = _dot_nt(greb, bbre_ref[cs, :]) + _dot_nt(gimb, bbim_ref[cs, :]) + r_ref[0:1, cs] * dext[0:TS, cs]
            dp_ref[:, cs] = du.astype(BF16)
        dext[TS:TS + 8, :] = dext[0:8, :]

    def rev(width, k):
        return pl.BlockSpec((TS, width), lambda i: (nt - 1 - i, k))

    def rev_halo(width):
        return pl.BlockSpec((8, width), lambda i: (jnp.maximum((nt - 1 - i) * (TS // 8) - 1, 0), 0))

    def const(shape):
        return pl.BlockSpec(shape, lambda i: (0,) * len(shape))

    return _call(
        body, name="s5_bwd", grid=(nt,),
        in_specs=[rev(BR, 10), rev(BR, 11), rev(BR, 3), rev(W, 0), rev(W, 0), rev_halo(W), rev_halo(W),
                  const((8, W)), const((nblk, J * CB, 2 * SB)), const((BR, SB)), const((BR, SB)), const((W, CB)), const((W, CB)),
                  const((8, BR)), const((BR, BR)), pl.BlockSpec(memory_space=pl.ANY)],
        input_output_aliases={15: 0},
        out_specs=[pl.BlockSpec((TS, 2 * BR), lambda i: (nt - 1 - i, 5)), const((8, BR)), const((BR, BR)), const((8, W)),
                   const((BR, SB)), const((BR, SB)), const((W, CB)), const((W, CB))],
        out_shape=[jax.ShapeDtypeStruct(dproj.shape, BF16), jax.ShapeDtypeStruct((8, BR), F32), jax.ShapeDtypeStruct((BR, BR), F32),
                   jax.ShapeDtypeStruct((8, W), F32), jax.ShapeDtypeStruct((BR, SB), F32), jax.ShapeDtypeStruct((BR, SB), F32),
                   jax.ShapeDtypeStruct((W, CB), F32), jax.ShapeDtypeStruct((W, CB), F32)],
        scratch_shapes=[pltpu.VMEM((TS, SB), F32)] * 2 + [pltpu.VMEM((TS + 8, SB), F32)] * 2
        + [pltpu.VMEM((TS, BR), F32), pltpu.VMEM((TS + 8, BR), F32), pltpu.VMEM((TS, J * CB), BF16), pltpu.VMEM((16, W), F32)],
        compiler_params=_params("arbitrary"),
    )(proj, proj, dcat, xre_all, xim_all, xre_all, xim_all, a8, cl, bbre, bbim, cre, cim, rows, wg, dproj)


_WEIGHTS = ['rel_bias', 'w_ada', 'b_ada', 'w_in', 'conv_a', 'conv_c', 'conv_c_b', 'lru_wa', 'lru_ba', 'lru_wx', 'lru_bx',
            'lru_lambda', 's5_lam_re', 's5_lam_im', 's5_log_dt', 's5_b_re', 's5_b_im', 's5_c_re', 's5_c_im', 's5_d',
            's5_w_glu', 's5_b_glu', 'w_out', 'ln_g', 'ln_b']
_LAYER_SMALL = ['conv_a', 'conv_c', 'conv_c_b', 'lru_wa', 'lru_ba', 'lru_wx', 'lru_bx', 'lru_lambda', 's5_lam_re', 's5_lam_im',
                's5_log_dt', 's5_b_re', 's5_b_im', 's5_c_re', 's5_c_im', 's5_d', 's5_b_glu']
_SMALL = ['rel_bias'] + _LAYER_SMALL + ['ln_g', 'ln_b']


def _t5_bucket(dist):
    max_exact = REL_BUCKETS // 2
    nf = jnp.maximum(dist, 1).astype(F32)
    large = max_exact + (jnp.log(nf / max_exact) / math.log(REL_MAX_DIST / max_exact)
                         * (REL_BUCKETS - max_exact)).astype(jnp.int32)
    large = jnp.minimum(large, REL_BUCKETS - 1)
    return jnp.where(dist < max_exact, dist, large)


def _bias_tables(rel_bias):
    i = jnp.arange(BLK)[:, None]
    j = jnp.arange(2 * BLK)[None, :]
    delta = i + BLK - j
    out = []
    for window, dil in DILATIONS:
        bucket = _t5_bucket(jnp.clip(delta, 0, window // dil) * dil)
        onehot = (bucket[:, :, None] == jnp.arange(REL_BUCKETS)[None, None, :]).astype(F32)
        out.append(jnp.einsum('ijb,bh->hij', onehot, rel_bias, precision=lax.Precision.HIGHEST))
    return jnp.stack(out)


def _prep_layer(q):
    eye8 = jnp.eye(8, dtype=F32)
    G = q['s5_lam_re'].shape[0]
    nblk = G // 8

    def block_diag(w):
        hd = w.shape[1]
        return (w[:, :, None, :] * eye8[:, None, :, None]).reshape(8 * hd, 8 * hd)

    def compact_b(bb):
        t = jnp.transpose(bb.reshape(nblk, 8, S5_STATE, S5_CH), (0, 1, 3, 2))
        return (t[:, :, :, None, :] * eye8[None, :, None, :, None]).reshape(G * S5_CH, 8 * S5_STATE)

    def compact_c(cc):
        t = jnp.transpose(cc.reshape(nblk, 8, S5_CH, S5_STATE), (0, 1, 3, 2))
        return (t[:, :, :, None, :] * eye8[None, :, None, :, None]).reshape(G * S5_STATE, 8 * S5_CH)

    lam_re, lam_im = q['s5_lam_re'], q['s5_lam_im']
    dt = jnp.exp(q['s5_log_dt'])[:, None]
    mag = jnp.exp(lam_re * dt)
    ab_re = mag * jnp.cos(lam_im * dt)
    ab_im = mag * jnp.sin(lam_im * dt)
    den = lam_re * lam_re + lam_im * lam_im
    f_re = ((ab_re - 1.0) * lam_re + ab_im * lam_im) / den
    f_im = (ab_im * lam_re - (ab_re - 1.0) * lam_im) / den
    b_re, b_im = q['s5_b_re'], q['s5_b_im']
    bb_re = f_re[..., None] * b_re - f_im[..., None] * b_im
    bb_im = f_re[..., None] * b_im + f_im[..., None] * b_re
    return dict(
        conv_a=q['conv_a'], conv_c=q['conv_c'], conv_c_b=q['conv_c_b'], lru_ba=q['lru_ba'], lru_bx=q['lru_bx'],
        sp=jax.nn.softplus(-q['lru_lambda']), wa=block_diag(q['lru_wa']), wx=block_diag(q['lru_wx']),
        ar=ab_re.reshape(-1), ai=ab_im.reshape(-1), bbre=compact_b(bb_re), bbim=compact_b(bb_im),
        cre=compact_c(q['s5_c_re']), cim=compact_c(q['s5_c_im']), s5_d=q['s5_d'], s5_b_glu=q['s5_b_glu'])


def _s5_lag_weights(kp):
    J, CB, SB = S5_LAGS, S5_CB, S5_SB
    ar, ai = kp['ar'], kp['ai']
    W = ar.shape[0]
    nblk = W // SB
    cr, ci = jnp.ones_like(ar), jnp.zeros_like(ai)
    pows = []
    for _ in range(J + 1):
        pows.append((cr, ci))
        cr, ci = cr * ar - ci * ai, cr * ai + ci * ar
    pwr = jnp.transpose(jnp.stack([p[0] for p in pows[:J]]).reshape(J, nblk, SB), (1, 0, 2))[:, :, None, :]
    pwi = jnp.transpose(jnp.stack([p[1] for p in pows[:J]]).reshape(J, nblk, SB), (1, 0, 2))[:, :, None, :]

    def lagged(re, im, sign):
        out = jnp.concatenate([re * pwr - sign * im * pwi, sign * re * pwi + im * pwr], axis=-1)
        return out.reshape(nblk, J * CB, 2 * SB).astype(BF16)

    bbl = lagged(kp['bbre'].reshape(nblk, 1, CB, SB), kp['bbim'].reshape(nblk, 1, CB, SB), 1.0)
    c0r = jnp.transpose(kp['cre'].reshape(nblk, SB, CB), (0, 2, 1))[:, None]
    c0i = -jnp.transpose(kp['cim'].reshape(nblk, SB, CB), (0, 2, 1))[:, None]
    cl = lagged(c0r, c0i, -1.0)
    return _rows8([pows[J][0], pows[J][1]], W), bbl, cl


def _rows8(vecs, width):
    rows = [v.reshape(1, width).astype(F32) for v in vecs]
    return jnp.concatenate(rows + [jnp.zeros((8 - len(rows), width), F32)], axis=0)


def kernel(x, c, rel_bias, w_ada, b_ada, w_in, conv_a, conv_c, conv_c_b, lru_wa, lru_ba, lru_wx, lru_bx, lru_lambda, s5_lam_re, s5_lam_im, s5_log_dt, s5_b_re, s5_b_im, s5_c_re, s5_c_im, s5_d, s5_w_glu, s5_b_glu, w_out, ln_g, ln_b, loss_target, m_rel_bias, m_w_ada, m_b_ada, m_w_in, m_conv_a, m_conv_c, m_conv_c_b, m_lru_wa, m_lru_ba, m_lru_wx, m_lru_bx, m_lru_lambda, m_s5_lam_re, m_s5_lam_im, m_s5_log_dt, m_s5_b_re, m_s5_b_im, m_s5_c_re, m_s5_c_im, m_s5_d, m_s5_w_glu, m_s5_b_glu, m_w_out, m_ln_g, m_ln_b, v_rel_bias, v_w_ada, v_b_ada, v_w_in, v_conv_a, v_conv_c, v_conv_c_b, v_lru_wa, v_lru_ba, v_lru_wx, v_lru_bx, v_lru_lambda, v_s5_lam_re, v_s5_lam_im, v_s5_log_dt, v_s5_b_re, v_s5_b_im, v_s5_c_re, v_s5_c_im, v_s5_d, v_s5_w_glu, v_s5_b_glu, v_w_out, v_ln_g, v_ln_b):
    a = dict(locals())
    me = 4 * lax.axis_index("x") + 2 * lax.axis_index("y") + lax.axis_index("c")
    x0, target = x[0], loss_target[0]
    S, D = x0.shape
    BR = D // 4
    NS = w_in.shape[2]
    SH = BR // NDEV
    assert S % (BLK * DILATIONS[-1][1]) == 0 and BR % (8 * S5_CH) == 0

    small = _all_gather(_pack([c, conv_a, conv_c]), "gather_small")
    per_dev = [_unpack(small[d], [c.shape, conv_a.shape, conv_c.shape]) for d in range(NDEV)]
    c_all = jnp.concatenate([p[0] for p in per_dev], axis=0)
    conv_a_full = jnp.concatenate([p[1] for p in per_dev], axis=2)
    conv_c_full = jnp.concatenate([p[2] for p in per_dev], axis=2)
    me1 = me.reshape(1).astype(jnp.int32)
    w_in_bf, w_out_bf, w_glu_bf = w_in.astype(BF16), w_out.astype(BF16), s5_w_glu.astype(BF16)

    b_cols = lax.dynamic_slice(b_ada, (0, me * NS), (DEPTH, NS)).reshape(DEPTH, 1, NS)
    ada_all = _all_gather(_ada_cols(c_all, w_ada, b_cols), "gather_ada")
    ada_me = lax.dynamic_index_in_dim(ada_all, me, axis=2, keepdims=False)
    ada_me = jnp.transpose(ada_me, (1, 0, 2)).reshape(DEPTH, 3 * D)
    shift, scale, gate = ada_me[:, :D], ada_me[:, D:2 * D], ada_me[:, 2 * D:]

    bias_tabs, bias_pull = jax.vjp(_bias_tables, rel_bias)
    HD = BR // 8
    head_ones = jnp.kron(jnp.eye(8, dtype=F32), jnp.ones((HD, HD), F32))

    saved = []
    xl = x0
    for l in range(DEPTH):
        q = {n: a[n][l] for n in _LAYER_SMALL}
        q['conv_a'], q['conv_c'] = conv_a_full[l], conv_c_full[l]
        kp, pull = jax.vjp(_prep_layer, q)
        a8, bbl, cl = _s5_lag_weights(kp)
        mod = _rows8([1.0 + scale[l], shift[l]], D)
        arows = _rows8(list(kp['conv_a']), BR)
        lrows = _rows8(list(kp['conv_c']) + [kp['conv_c_b'], kp['lru_ba'], kp['lru_bx'], kp['sp']], BR)
        srows = _rows8([kp['s5_d'], kp['s5_b_glu']], BR)
        orows = _rows8([1.0 + gate[l], ln_g[l], ln_b[l]], D)
        wa, wx = kp['wa'].astype(BF16), kp['wx'].astype(BF16)
        s5w = [kp[n].astype(BF16) for n in ('bbre', 'bbim', 'cre', 'cim')]

        hbf = _modulate(xl, mod)
        proj, w_in_l, w_out_l, w_glu_l = _proj_gather(hbf, w_in_bf[l], [w_out_bf[l], w_glu_bf[l]], me1)
        w_out_l, w_glu_l = w_out_l.reshape(D, D), w_glu_l.reshape(BR, BR)
        ya = _conv_a_fwd(proj, arows, BR)
        os_, ls_ = [], []
        for g, (_, dil) in enumerate(DILATIONS):
            o, lse = _attn_fwd(proj, bias_tabs[g], dil, BR)
            os_.append(o)
            ls_.append(lse)
        yb = _attn_mix_fwd(proj, os_, ls_, BR)
        yc, hs = _lru_fwd(proj, lrows, wa, wx, BR)
        yd, xre, xim = _s5_fwd(proj, a8, bbl, s5w[2], s5w[3], srows, w_glu_l, BR)
        if l < DEPTH - 1:
            xn, y, cat = _out_ln([ya, yb, yc, yd], w_out_l, xl, orows)
        else:
            xn, y, cat, loss_local = _out_ln([ya, yb, yc, yd], w_out_l, xl, orows, target)
        saved.append(dict(x=xl, proj=proj, hbf=hbf, os=os_, ls=ls_, hs=hs, xre=xre, xim=xim, y=y, cat=cat, pull=pull,
                          mod=mod, arows=arows, lrows=lrows, srows=srows, orows=orows, wa=wa, wx=wx, s5w=s5w,
                          qw=(a8, cl), w_in=w_in_l, w_out=w_out_l, w_glu=w_glu_l))
        xl = xn

    dout = xl

    dbias = jnp.zeros_like(bias_tabs)
    lgrads, dada, dw_in_parts, dw_out_parts, glu_parts = [None] * DEPTH, [None] * DEPTH, [None] * DEPTH, [None] * DEPTH, [None] * DEPTH
    for l in reversed(range(DEPTH)):
        sv = saved[l]
        proj = sv['proj']
        dxres, dyb, dcat, lnsum = _ln_bwd(dout, sv['x'], sv['y'], sv['orows'], sv['w_out'])
        dw_out_parts[l] = _dw_scatter(sv['cat'], dyb, D // NDEV, True, me1, "dw_out_scatter")
        dproj, asum = _conv_a_bwd(proj, dcat, sv['arows'], BR)
        dos, es, dbg = _attn_mix_bwd(proj, dcat, sv['os'], sv['ls'], head_ones, BR)
        dqkv, dbs = [], []
        for g, (_, dil) in enumerate(DILATIONS):
            dq_, dk_, dv_, db_ = _attn_bwd(proj, dos[g], es[g], sv['ls'][g], bias_tabs[g], dil, BR)
            dqkv.append((dq_, dk_, dv_))
            dbs.append(db_)
        dbias = dbias + jnp.stack(dbs)
        dproj = _attn_dsum(dqkv, dbg, dproj, BR)
        dproj, lsum, dwa, dwx = _lru_bwd(proj, sv['hs'], dcat, sv['lrows'], sv['wa'], sv['wx'], dproj, BR)
        dproj, ssum, dwg, da_, dbbre, dbbim, dcre, dcim = _s5_bwd(
            proj, dcat, sv['xre'], sv['xim'], *sv['qw'], *sv['s5w'], sv['srows'], sv['w_glu'], dproj, BR)
        dkp = dict(conv_a=asum[0:3], conv_c=lsum[0:4], conv_c_b=lsum[4], lru_ba=lsum[5], lru_bx=lsum[6], sp=lsum[7],
                   wa=dwa, wx=dwx, ar=da_[0], ai=da_[1], bbre=dbbre, bbim=dbbim, cre=dcre, cim=dcim,
                   s5_d=ssum[0], s5_b_glu=ssum[1])
        lgrads[l] = dict(sv['pull'](dkp)[0], ln_g=lnsum[0], ln_b=lnsum[1])
        sides = [(dwg.reshape(NDEV, BR // NDEV, BR), False)]
        if l == 0:
            local = {'rel_bias': bias_pull(dbias)[0], 'loss': loss_local.reshape(1)}
            for n in _LAYER_SMALL + ['ln_g', 'ln_b']:
                local[n] = jnp.stack([lgrads[k][n] for k in range(DEPTH)])
            packed = _SMALL + ['loss']
            sides.append((_pack([local[n] for n in packed]), True))
        dout, msum, glu_parts[l], *gathered = _dh(dproj, sv['w_in'], dxres, sv['x'], sv['mod'], sides)
        dw_in_parts[l] = _dw_scatter(sv['hbf'], dproj, NS, False, me1, "dw_in_scatter")
        dada[l] = jnp.concatenate([msum[1], msum[0], lnsum[2]])
    grad_x = dout[None]

    out = {}

    def put(name, res):
        out[name] = res

    put('w_in', _adamw(dw_in_parts, w_in, m_w_in, v_w_in, "adamw_w_in"))
    put('w_out', _adamw(dw_out_parts, w_out, m_w_out, v_w_out, "adamw_w_out"))
    put('s5_w_glu', _adamw(glu_parts, s5_w_glu, m_s5_w_glu, v_s5_w_glu, "adamw_w_glu"))

    dada_all = _all_gather(jnp.stack(dada), "gather_dada")
    dada_cols = jnp.transpose(lax.dynamic_slice(dada_all, (0, 0, me * NS), (NDEV, DEPTH, NS)), (1, 0, 2))
    gw_ada, gb_ada = _ada_grads(c_all, dada_all, dada_cols)
    put('w_ada', _adamw([gw_ada[l] for l in range(DEPTH)], w_ada, m_w_ada, v_w_ada, "adamw_w_ada"))

    shapes = [local[n].shape for n in packed]
    gs = dict(zip(packed, _unpack(_sum8(gathered[0], "sum_small_grads"), shapes)))
    loss = gs['loss'][0]
    gs['conv_a'] = lax.dynamic_slice_in_dim(gs['conv_a'], me * SH, SH, axis=2)
    gs['conv_c'] = lax.dynamic_slice_in_dim(gs['conv_c'], me * SH, SH, axis=2)
    gs['b_ada'] = gb_ada
    names = _SMALL + ['b_ada']
    shapes = [a[n].shape for n in names]
    res = _adamw([_pack([gs[n] for n in names])], _pack([a[n] for n in names])[None], _pack([a['m_' + n] for n in names])[None],
                 _pack([a['v_' + n] for n in names])[None], "adamw_small")
    unpacked = [_unpack(r[0], shapes) for r in res]
    for k, n in enumerate(names):
        put(n, tuple(unpacked[j][k] for j in range(4)))

    return (loss, grad_x, *[out[n][0] for n in _WEIGHTS], *[out[n][1] for n in _WEIGHTS],
            *[out[n][2] for n in _WEIGHTS], *[out[n][3] for n in _WEIGHTS])
```

```python
import math

import jax
import jax.numpy as jnp
from jax import lax
from jax.experimental import pallas as pl
from jax.experimental.pallas import tpu as pltpu

F32 = jnp.float32
BF16 = jnp.bfloat16

NDEV = 8
DEPTH = 2
BLK = 128
DILATIONS = ((128, 1), (512, 4), (2048, 16))
REL_BUCKETS = 32
REL_MAX_DIST = 2048
LRU_C = 8.0
S5_CH = 16
S5_STATE = 64
ALPHA = (2 * DEPTH) ** 0.25
LN_EPS = 1e-5
ADAM_LR, ADAM_B1, ADAM_B2, ADAM_EPS, ADAM_WD, ADAM_STEP = 0.001, 0.9, 0.999, 1e-08, 0.01, 10
NEG = -1e30
VMEM_LIMIT = 56 * 1024 * 1024
TILE_STREAM = 512
TILE_RESIDENT = 256
TILE_PROJ = 2048


def _call(body, **kw):
    return pl.pallas_call(body, **kw)


def _params(*sem):
    return pltpu.CompilerParams(dimension_semantics=sem, vmem_limit_bytes=VMEM_LIMIT)


def _sigmoid(x):
    return 1.0 / (1.0 + jnp.exp(-x))


def _silu(x):
    return x * _sigmoid(x)


def _dsilu(x):
    s = _sigmoid(x)
    return s * (1.0 + x * (1.0 - s))


_GC = math.sqrt(2.0 / math.pi)


def _gelu(x):
    return 0.5 * x * (1.0 + jnp.tanh(_GC * (x + 0.044715 * x * x * x)))


def _dgelu(x):
    t = jnp.tanh(_GC * (x + 0.044715 * x * x * x))
    return 0.5 * (1.0 + t) + 0.5 * x * (1.0 - t * t) * _GC * (1.0 + 3.0 * 0.044715 * x * x)


def _expm1(x):
    series = x * (1.0 + x * (1.0 / 2) * (1.0 + x * (1.0 / 3) * (1.0 + x * (1.0 / 4) * (1.0 + x * (1.0 / 5) * (1.0 + x * (1.0 / 6))))))
    return jnp.where(jnp.abs(x) < 0.25, series, jnp.exp(x) - 1.0)


def _dot(a, b):
    return jnp.dot(a, b, preferred_element_type=F32)


def _dot_nt(a, b):
    return lax.dot_general(a, b, (((1,), (1,)), ((), ())), preferred_element_type=F32)


def _dot_tn(a, b):
    return lax.dot_general(a, b, (((0,), (0,)), ((), ())), preferred_element_type=F32)


def _colsum(v):
    return jnp.sum(v, axis=0, keepdims=True)


def _all_gather(x, name):
    def body(x_ref, o_ref, send_sems, recv_sems, local_sem):
        ix, iy, ic = lax.axis_index("x"), lax.axis_index("y"), lax.axis_index("c")
        me, sibling = (ix, iy, ic), (ix, iy, 1 - ic)
        chips = [(1 - ix, iy), (ix, 1 - iy), (1 - ix, 1 - iy)]

        def slot(px, py, pc):
            return o_ref.at[4 * px + 2 * py + pc]

        def copy(k, block, to, src=None):
            return pltpu.make_async_remote_copy(
                src_ref=slot(*block) if src is None else src, dst_ref=slot(*block), send_sem=send_sems.at[k],
                recv_sem=recv_sems.at[k], device_id=to, device_id_type=pl.DeviceIdType.MESH)

        mine = pltpu.make_async_copy(x_ref, slot(*me), local_sem)
        mine.start()
        first = [copy(0, me, sibling, src=x_ref)] + [copy(1 + j, me, (*chip, ic), src=x_ref) for j, chip in enumerate(chips)]
        for cp in first:
            cp.start()
        passed = [copy(4 + j, (*chip, ic), sibling) for j, chip in enumerate(chips)]
        for j, chip in enumerate(chips):
            copy(1 + j, (*chip, ic), me).wait_recv()
            passed[j].start()
        copy(0, sibling, me).wait_recv()
        for j, chip in enumerate(chips):
            copy(4 + j, (*chip, 1 - ic), me).wait_recv()
        for cp in first + passed:
            cp.wait_send()
        mine.wait()

    return _call(
        body, name=name,
        out_shape=jax.ShapeDtypeStruct((NDEV,) + x.shape, x.dtype),
        in_specs=[pl.BlockSpec(memory_space=pl.ANY)],
        out_specs=pl.BlockSpec(memory_space=pl.ANY),
        scratch_shapes=[pltpu.SemaphoreType.DMA((NDEV - 1,)), pltpu.SemaphoreType.DMA((NDEV - 1,)), pltpu.SemaphoreType.DMA],
    )(x)


def _pack(arrs):
    parts = []
    for a in arrs:
        n = math.prod(a.shape)
        parts.append(jnp.pad(a.reshape(-1).astype(F32), (0, _pack_rows(n) * 128 - n)).reshape(-1, 128))
    rows = sum(p.shape[0] for p in parts)
    total = rows if rows <= 512 else -(-rows // 512) * 512
    if total > rows:
        parts.append(jnp.zeros((total - rows, 128), F32))
    return jnp.concatenate(parts, axis=0)


def _pack_rows(n):
    return -(-n // 1024) * 8


def _unpack(packed, shapes):
    out, row = [], 0
    for s in shapes:
        n = math.prod(s)
        rows = _pack_rows(n)
        out.append(packed[row:row + rows].reshape(-1)[:n].reshape(s))
        row += rows
    return out


def _sum8(parts, name):
    _, R, C = parts.shape
    TR = R
    for cand in (512, 256, 128, 64, 32, 16, 8):
        if R % cand == 0:
            TR = cand
            break

    def body(p_ref, o_ref):
        acc = p_ref[0]
        for d in range(1, NDEV):
            acc = acc + p_ref[d]
        o_ref[...] = acc

    return _call(
        body, name=name, grid=(R // TR,),
        in_specs=[pl.BlockSpec((NDEV, TR, C), lambda i: (0, i, 0))],
        out_specs=pl.BlockSpec((TR, C), lambda i: (i, 0)),
        out_shape=jax.ShapeDtypeStruct((R, C), F32),
        compiler_params=_params("arbitrary"),
    )(parts)


def _adamw(g_layers, w, m, v, name):
    L, R, C = w.shape
    g0 = g_layers[0]
    nparts = g0.shape[0] if g0.ndim == 3 else 0
    TR = R
    for cand in (256, 128, 64, 32, 16):
        if R % cand == 0 and cand * C <= 128 * 1024:
            TR = cand
            break

    def body(*refs):
        g_refs = refs[:L]
        w_ref, m_ref, v_ref, go_ref, d_ref, mo_ref, vo_ref = refs[L:]
        for ll in range(L):
            @pl.when(pl.program_id(0) == ll)
            def _(g_ref=g_refs[ll]):
                if nparts:
                    g = g_ref[0].astype(F32)
                    for d in range(1, nparts):
                        g = g + g_ref[d].astype(F32)
                else:
                    g = g_ref[...]
                m2 = ADAM_B1 * m_ref[0] + (1.0 - ADAM_B1) * g
                v2 = ADAM_B2 * v_ref[0] + (1.0 - ADAM_B2) * (g * g)
                m_hat = m2 / (1.0 - ADAM_B1 ** ADAM_STEP)
                v_hat = v2 / (1.0 - ADAM_B2 ** ADAM_STEP)
                go_ref[0] = g
                d_ref[0] = -ADAM_LR * (m_hat / (jnp.sqrt(v_hat) + ADAM_EPS) + ADAM_WD * w_ref[0])
                mo_ref[0] = m2
                vo_ref[0] = v2

    def gspec(ll):
        if nparts:
            return pl.BlockSpec((nparts, TR, C), lambda l, i: (0, jnp.where(l == ll, i, 0), 0))
        return pl.BlockSpec((TR, C), lambda l, i: (jnp.where(l == ll, i, 0), 0))

    spec = pl.BlockSpec((1, TR, C), lambda l, i: (l, i, 0))
    return _call(
        body, name=name, grid=(L, R // TR),
        in_specs=[gspec(ll) for ll in range(L)] + [spec, spec, spec], out_specs=[spec] * 4,
        out_shape=[jax.ShapeDtypeStruct((L, R, C), F32)] * 4,
        compiler_params=_params("arbitrary", "arbitrary"),
    )(*g_layers, w, m, v)


def _ada_cols(c_all, w_ada, b_cols):
    L, D, NS = w_ada.shape

    def body(c_ref, w_ref, b_ref, o_ref):
        cond = _silu(c_ref[...]).astype(BF16)
        o_ref[0] = _dot(cond, w_ref[0].astype(BF16)) + b_ref[0]

    return _call(
        body, name="ada_cols", grid=(L,),
        in_specs=[pl.BlockSpec((NDEV, D), lambda l: (0, 0)), pl.BlockSpec((1, D, NS), lambda l: (l, 0, 0)),
                  pl.BlockSpec((1, 1, NS), lambda l: (l, 0, 0))],
        out_specs=pl.BlockSpec((1, NDEV, NS), lambda l: (l, 0, 0)),
        out_shape=jax.ShapeDtypeStruct((L, NDEV, NS), F32),
        compiler_params=_params("arbitrary"),
    )(c_all, w_ada, b_cols)


def _ada_grads(c_all, dada_all, dada_cols):
    _, D = c_all.shape
    L, _, NS = dada_cols.shape
    D3 = dada_all.shape[2]

    def body(c_ref, da_ref, dc_ref, gw_ref, gb_ref):
        cond = _silu(c_ref[...]).astype(BF16)
        gw_ref[0] = _dot_tn(cond, dc_ref[0].astype(BF16))
        acc = da_ref[0]
        for d in range(1, NDEV):
            acc = acc + da_ref[d]
        gb_ref[...] = acc

    return _call(
        body, name="ada_grads", grid=(L,),
        in_specs=[pl.BlockSpec((NDEV, D), lambda l: (0, 0)), pl.BlockSpec((NDEV, L, D3), lambda l: (0, 0, 0)),
                  pl.BlockSpec((1, NDEV, NS), lambda l: (l, 0, 0))],
        out_specs=[pl.BlockSpec((1, D, NS), lambda l: (l, 0, 0)), pl.BlockSpec((L, D3), lambda l: (0, 0))],
        out_shape=[jax.ShapeDtypeStruct((L, D, NS), F32), jax.ShapeDtypeStruct((L, D3), F32)],
        compiler_params=_params("arbitrary"),
    )(c_all, dada_all, dada_cols)


def _tile(S, want):
    return want if S % want == 0 else S


def _modulate(x, mod):
    S, D = x.shape
    TS = _tile(S, TILE_STREAM)

    def body(x_ref, mod_ref, h_ref):
        h_ref[...] = (x_ref[...] * mod_ref[0:1, :] + mod_ref[1:2, :]).astype(BF16)

    spec = pl.BlockSpec((TS, D), lambda i: (i, 0))
    return _call(
        body, name="modulate", grid=(S // TS,),
        in_specs=[spec, pl.BlockSpec((8, D), lambda i: (0, 0))], out_specs=spec,
        out_shape=jax.ShapeDtypeStruct((S, D), BF16),
        compiler_params=_params("arbitrary"),
    )(x, mod)


def _peer(k):
    ix, iy, ic = lax.axis_index("x"), lax.axis_index("y"), lax.axis_index("c")
    bx, by, bc = (k >> 2) & 1, (k >> 1) & 1, k & 1
    px, py, pc = ix + bx - 2 * ix * bx, iy + by - 2 * iy * by, ic + bc - 2 * ic * bc
    return (px, py, pc), 4 * px + 2 * py + pc


GATHER_ORDER = (0, 1, 2, 4, 6, 3, 5, 7)
GATHER_DIRECT = (1, 2, 4, 6)
SCATTER_ORDER = (3, 2, 5, 4, 7, 6, 1, 0)


def _offset(order, k):
    off = jnp.int32(order[-1])
    for step in reversed(range(len(order) - 1)):
        off = jnp.where(k == step, jnp.int32(order[step]), off)
    return off


def _proj_gather(h, w_shard, extras, me1):
    assert GATHER_ORDER[0] == 0
    S, D = h.shape
    NS = w_shard.shape[1]
    TS = _tile(S, TILE_PROJ)
    nt = S // TS
    ne = len(extras)

    def body(me_ref, h_ref, w_ref, *rest):
        x_refs, o_ref, wg_ref = rest[:ne], rest[ne], rest[ne + 1]
        xg_refs = rest[ne + 2:2 * ne + 2]
        wbuf, send_sems, recv_sems, local_sems, load_sems = rest[2 * ne + 2:]
        k, i = pl.program_id(0), pl.program_id(1)
        me = me_ref[0]
        off = _offset(GATHER_ORDER, k)

        def push(src, dst, which, kk):
            peer, _ = _peer(kk)
            return pltpu.make_async_remote_copy(
                src_ref=src, dst_ref=dst.at[me], send_sem=send_sems.at[which, kk], recv_sem=recv_sems.at[which, kk],
                device_id=peer, device_id_type=pl.DeviceIdType.MESH)

        def own(src, dst, which):
            return pltpu.make_async_copy(src, dst.at[me], local_sems.at[which])

        pairs = [(w_ref, wg_ref)] + list(zip(x_refs, xg_refs))

        @pl.when(jnp.logical_and(k == 0, i == 0))
        def _():
            for which, (src, dst) in enumerate(pairs):
                own(src, dst, which).start()
                for kk in (GATHER_DIRECT if which == 0 else GATHER_ORDER[1:]):
                    push(src, dst, which, kk).start()
            first = pltpu.make_async_copy(w_ref, wbuf.at[0], load_sems.at[0])
            first.start()
            first.wait()

        @pl.when(jnp.logical_and(k > 0, i == 0))
        def _():
            push(w_ref, wg_ref, 0, off).wait_recv()
            _, blk = _peer(off)
            for step, o in enumerate(GATHER_ORDER):
                if o in GATHER_DIRECT and o > 1:
                    @pl.when(k == step)
                    def _(o=o):
                        sibling, _ = _peer(1)
                        pltpu.make_async_remote_copy(
                            src_ref=wg_ref.at[blk], dst_ref=wg_ref.at[blk], send_sem=send_sems.at[0, o + 1],
                            recv_sem=recv_sems.at[0, o + 1], device_id=sibling, device_id_type=pl.DeviceIdType.MESH).start()
            load = pltpu.make_async_copy(wg_ref.at[blk], wbuf.at[k % 2], load_sems.at[k % 2])
            load.start()
            load.wait()

        o_ref[...] = _dot(h_ref[...], wbuf[k % 2])

        @pl.when(jnp.logical_and(k == NDEV - 1, i == nt - 1))
        def _():
            for which, (src, dst) in enumerate(pairs):
                for kk in range(1, NDEV):
                    cp = push(src, dst, which, kk)
                    cp.wait_send()
                    if which > 0:
                        cp.wait_recv()
                own(src, dst, which).wait()

    def col(k, i, me_ref):
        m, off = me_ref[0], _offset(GATHER_ORDER, k)
        return i, (m | off) - (m & off)

    anyspec = pl.BlockSpec(memory_space=pl.ANY)
    grid_spec = pltpu.PrefetchScalarGridSpec(
        num_scalar_prefetch=1, grid=(NDEV, nt),
        in_specs=[pl.BlockSpec((TS, D), lambda k, i, me_ref: (i, 0)), anyspec] + [anyspec] * ne,
        out_specs=[pl.BlockSpec((TS, NS), col), anyspec] + [anyspec] * ne,
        scratch_shapes=[pltpu.VMEM((2, D, NS), BF16), pltpu.SemaphoreType.DMA((1 + ne, NDEV)), pltpu.SemaphoreType.DMA((1 + ne, NDEV)),
                        pltpu.SemaphoreType.DMA((1 + ne,)), pltpu.SemaphoreType.DMA((2,))])
    return _call(
        body, name="proj_gather", grid_spec=grid_spec,
        out_shape=[jax.ShapeDtypeStruct((S, NDEV * NS), F32), jax.ShapeDtypeStruct((NDEV, D, NS), BF16)]
        + [jax.ShapeDtypeStruct((NDEV,) + e.shape, e.dtype) for e in extras],
        compiler_params=_params("arbitrary", "arbitrary"),
    )(me1, h, w_shard, *extras)


def _dw_scatter(a, b, blk, by_rows, me1, name):
    S = a.shape[0]
    shape = (blk, b.shape[1]) if by_rows else (a.shape[1], blk)
    fixed = shape[0] * shape[1] * (4 + 2 * 2 + 4 * 2)
    per_row = 2 * 2 * (shape[0] + shape[1])
    TK = next((t for t in (2048, 1024, 512, 256) if S % t == 0 and fixed + t * per_row <= 0.8 * VMEM_LIMIT), S)
    nk = S // TK
    last = NDEV - 1
    assert all(o & 1 for o in SCATTER_ORDER[0::2]) and not any(o & 1 for o in SCATTER_ORDER[1::2]) and SCATTER_ORDER[-1] == 0
    assert all(SCATTER_ORDER[s] == SCATTER_ORDER[s + 1] + 1 for s in range(0, NDEV, 2))

    def body(me_ref, a_ref, b_ref, recv_ref, acc, stage, sib, send_sems, recv_sems, local_sem):
        k, kk = pl.program_id(0), pl.program_id(1)

        def chip_of(step):
            return _offset(SCATTER_ORDER, step) >> 1

        def to_sibling(j, slot):
            sibling, _ = _peer(1)
            return pltpu.make_async_remote_copy(
                src_ref=stage.at[slot], dst_ref=sib.at[j], send_sem=send_sems.at[0, j], recv_sem=recv_sems.at[0, j],
                device_id=sibling, device_id_type=pl.DeviceIdType.MESH)

        def to_chip(j, slot):
            peer, _ = _peer(2 * j)
            return pltpu.make_async_remote_copy(
                src_ref=stage.at[slot], dst_ref=recv_ref.at[j], send_sem=send_sems.at[1, j], recv_sem=recv_sems.at[1, j],
                device_id=peer, device_id_type=pl.DeviceIdType.MESH)

        own = pltpu.make_async_copy(stage.at[last % 2], recv_ref.at[0], local_sem)

        @pl.when(kk == 0)
        def _():
            acc[...] = jnp.zeros_like(acc)
        acc[...] += _dot_tn(a_ref[...], b_ref[...])

        @pl.when(kk == nk - 1)
        def _():
            j = chip_of(k)
            even_step = k % 2 == 0

            @pl.when(jnp.logical_and(k >= 2, even_step))
            def _():
                to_sibling(chip_of(k - 2), k % 2).wait_send()

            @pl.when(jnp.logical_and(k >= 3, jnp.logical_not(even_step)))
            def _():
                to_chip(chip_of(k - 2), k % 2).wait_send()

            @pl.when(even_step)
            def _():
                stage[k % 2] = acc[...].astype(BF16)
                to_sibling(j, k % 2).start()

            @pl.when(jnp.logical_not(even_step))
            def _():
                to_sibling(j, 0).wait_recv()
                stage[k % 2] = (acc[...] + sib[j].astype(F32)).astype(BF16)

                @pl.when(k < last)
                def _():
                    to_chip(j, k % 2).start()

                @pl.when(k == last)
                def _():
                    own.start()
                    to_sibling(SCATTER_ORDER[last - 1] >> 1, (last - 1) % 2).wait_send()
                    own.wait()
                    for jj in range(1, NDEV // 2):
                        to_chip(jj, 0).wait_recv()

    def blk_of(k, me_ref):
        m, off = me_ref[0], _offset(SCATTER_ORDER, k)
        return (m | off) - (m & off)

    if by_rows:
        in_specs = [pl.BlockSpec((TK, blk), lambda k, kk, me_ref: (kk, blk_of(k, me_ref))),
                    pl.BlockSpec((TK, b.shape[1]), lambda k, kk, me_ref: (kk, 0))]
    else:
        in_specs = [pl.BlockSpec((TK, a.shape[1]), lambda k, kk, me_ref: (kk, 0)),
                    pl.BlockSpec((TK, blk), lambda k, kk, me_ref: (kk, blk_of(k, me_ref)))]
    grid_spec = pltpu.PrefetchScalarGridSpec(
        num_scalar_prefetch=1, grid=(NDEV, nk), in_specs=in_specs,
        out_specs=pl.BlockSpec(memory_space=pl.ANY),
        scratch_shapes=[pltpu.VMEM(shape, F32), pltpu.VMEM((2,) + shape, BF16), pltpu.VMEM((NDEV // 2,) + shape, BF16),
                        pltpu.SemaphoreType.DMA((2, NDEV // 2)), pltpu.SemaphoreType.DMA((2, NDEV // 2)), pltpu.SemaphoreType.DMA])
    return _call(
        body, name=name, grid_spec=grid_spec,
        out_shape=jax.ShapeDtypeStruct((NDEV // 2,) + shape, BF16),
        compiler_params=_params("arbitrary", "arbitrary"),
    )(me1, a, b)


def _chunk(TS, BR, k):
    return pl.BlockSpec((TS, BR), lambda i: (i, k))


def _halo_prev(TS, BR, k):
    return pl.BlockSpec((8, BR), lambda i: (jnp.maximum(i * (TS // 8) - 1, 0), k))


def _conv_a_fwd(proj, w8, BR):
    S = proj.shape[0]
    TS = _tile(S, TILE_STREAM)

    def body(ab_ref, ac_ref, ax_ref, ag_ref, hc_ref, hx_ref, w_ref, o_ref, ext):
        i = pl.program_id(0)
        u = ac_ref[...] * ax_ref[...]
        ext[0:8, :] = jnp.where(i > 0, hc_ref[...] * hx_ref[...], 0.0)
        ext[8:8 + TS, :] = u
        conv = ext[pl.ds(6, TS), :] * w_ref[0:1, :] + ext[pl.ds(7, TS), :] * w_ref[1:2, :] + u * w_ref[2:3, :]
        o_ref[...] = ab_ref[...] * conv * _silu(ag_ref[...])

    return _call(
        body, name="conv_a_fwd", grid=(S // TS,),
        in_specs=[_chunk(TS, BR, 0), _chunk(TS, BR, 1), _chunk(TS, BR, 2), _chunk(TS, BR, 3),
                  _halo_prev(TS, BR, 1), _halo_prev(TS, BR, 2), pl.BlockSpec((8, BR), lambda i: (0, 0))],
        out_specs=pl.BlockSpec((TS, BR), lambda i: (i, 0)),
        out_shape=jax.ShapeDtypeStruct((S, BR), F32),
        scratch_shapes=[pltpu.VMEM((TS + 8, BR), F32)],
        compiler_params=_params("arbitrary"),
    )(proj, proj, proj, proj, proj, proj, w8)


ATT_UNIT = 2048
ATT_LANES = 128


def _attn_mask():
    i = lax.broadcasted_iota(jnp.int32, (BLK, 2 * BLK), 0)
    j = lax.broadcasted_iota(jnp.int32, (BLK, 2 * BLK), 1)
    return jnp.logical_and(j >= i, j <= i + BLK), j >= BLK


def _attn_rows(n, r, dil):
    return pl.ds(n * BLK * dil + r, BLK, stride=dil) if dil > 1 else pl.ds(n * BLK, BLK)


def _attn_geometry(S, BR, dil):
    HD = BR // 8
    U = ATT_UNIT
    assert S % U == 0 and BR % ATT_LANES == 0 and ATT_LANES % HD == 0 and U % (BLK * dil) == 0
    return HD, U, S // U, U // (BLK * dil), ATT_LANES // HD, BR // ATT_LANES


def _attn_fwd(proj, bias, dil, BR):
    S = proj.shape[0]
    HD, U, NU, nbu, hpb, HBK = _attn_geometry(S, BR, dil)
    scale = HD ** -0.5
    combos = [(n, r) for n in range(nbu) for r in range(dil)]

    def body(q_ref, kc_ref, kp_ref, vc_ref, vp_ref, b_ref, o_ref, l_ref, sbuf, pbuf):
        m = pl.program_id(1)
        band, is_cur = _attn_mask()
        first = jnp.logical_and(band, jnp.logical_or(is_cur, m > 0))

        def keys(ref_c, ref_p, n, r):
            prev = ref_c[_attn_rows(n - 1, r, dil), :] if n > 0 else ref_p[_attn_rows(nbu - 1, r, dil), :]
            return jnp.concatenate([prev, ref_c[_attn_rows(n, r, dil), :]], axis=0).astype(BF16)

        for c, (n, r) in enumerate(combos):
            q = (q_ref[_attn_rows(n, r, dil), :] * scale).astype(BF16)
            kk = keys(kc_ref, kp_ref, n, r)
            for h in range(hpb):
                sl = slice(h * HD, (h + 1) * HD)
                sbuf[c * hpb + h] = _dot_nt(q[:, sl], kk[:, sl])
        for c, (n, r) in enumerate(combos):
            lses = []
            for h in range(hpb):
                s = jnp.where(band if n > 0 else first, sbuf[c * hpb + h] + b_ref[h], NEG)
                mx = jnp.max(s, axis=-1, keepdims=True)
                p = jnp.exp(s - mx)
                l = jnp.sum(p, axis=-1, keepdims=True)
                pbuf[c * hpb + h] = (p * (1.0 / l)).astype(BF16)
                lses.append(jnp.broadcast_to(mx + jnp.log(l), (BLK, HD)))
            l_ref[_attn_rows(n, r, dil), :] = jnp.concatenate(lses, axis=1)
        for c, (n, r) in enumerate(combos):
            vv = keys(vc_ref, vp_ref, n, r)
            o_ref[_attn_rows(n, r, dil), :] = jnp.concatenate(
                [_dot(pbuf[c * hpb + h], vv[:, h * HD:(h + 1) * HD]) for h in range(hpb)], axis=1)

    def cur(c):
        return pl.BlockSpec((U, ATT_LANES), lambda hb, m: (m, c * HBK + hb))

    def prev(c):
        return pl.BlockSpec((U, ATT_LANES), lambda hb, m: (jnp.maximum(m - 1, 0), c * HBK + hb))

    ospec = pl.BlockSpec((U, ATT_LANES), lambda hb, m: (m, hb))
    nhb = len(combos) * hpb
    return _call(
        body, name="attn_fwd_d%d" % dil, grid=(HBK, NU),
        in_specs=[cur(4), cur(5), prev(5), cur(6), prev(6), pl.BlockSpec((hpb, BLK, 2 * BLK), lambda hb, m: (hb, 0, 0))],
        out_specs=[ospec, ospec],
        out_shape=[jax.ShapeDtypeStruct((S, BR), F32)] * 2,
        scratch_shapes=[pltpu.VMEM((nhb, BLK, 2 * BLK), F32), pltpu.VMEM((nhb, BLK, 2 * BLK), BF16)],
        compiler_params=_params("arbitrary", "arbitrary"),
    )(proj, proj, proj, proj, proj, bias)


def _mix_weights(l1, l2, l3):
    m = jnp.maximum(jnp.maximum(l1, l2), l3)
    e1, e2, e3 = jnp.exp(l1 - m), jnp.exp(l2 - m), jnp.exp(l3 - m)
    inv = 1.0 / (e1 + e2 + e3)
    return e1 * inv, e2 * inv, e3 * inv


def _attn_mix_fwd(proj, os_, ls_, BR):
    S = proj.shape[0]
    TS = _tile(S, TILE_STREAM)

    def body(o1, o2, o3, l1, l2, l3, g_ref, y_ref):
        w1, w2, w3 = _mix_weights(l1[...], l2[...], l3[...])
        y_ref[...] = (w1 * o1[...] + w2 * o2[...] + w3 * o3[...]) * _silu(g_ref[...])

    spec = pl.BlockSpec((TS, BR), lambda i: (i, 0))
    return _call(
        body, name="attn_mix_fwd", grid=(S // TS,),
        in_specs=[spec] * 6 + [_chunk(TS, BR, 7)], out_specs=spec,
        out_shape=jax.ShapeDtypeStruct((S, BR), F32),
        compiler_params=_params("arbitrary"),
    )(*os_, *ls_, proj)


def _lru_gates(xc, wa, wx, ba, bx, sp):
    xb = xc.astype(BF16)
    r = _sigmoid(_dot(xb, wa) + ba)
    ig = _sigmoid(_dot(xb, wx) + bx)
    la = -LRU_C * r * sp
    a = jnp.exp(la)
    mult = jnp.sqrt(-_expm1(2.0 * la))
    return r, ig, a, mult


def _lru_fwd(proj, rows, wa, wx, BR):
    S = proj.shape[0]
    TS = _tile(S, TILE_RESIDENT)
    PAD = TS // 2

    def body(cx_ref, hx_ref, cg_ref, r_ref, wa_ref, wx_ref, y_ref, h_ref, ext, sa, sb, carry):
        i = pl.program_id(0)

        @pl.when(i == 0)
        def _():
            sa[0:PAD, :] = jnp.ones((PAD, BR), F32)
            sb[0:PAD, :] = jnp.zeros((PAD, BR), F32)
            carry[...] = jnp.zeros_like(carry)

        cx = cx_ref[...]
        ext[0:8, :] = jnp.where(i > 0, hx_ref[...], 0.0)
        ext[8:8 + TS, :] = cx
        xc = (ext[pl.ds(5, TS), :] * r_ref[0:1, :] + ext[pl.ds(6, TS), :] * r_ref[1:2, :]
              + ext[pl.ds(7, TS), :] * r_ref[2:3, :] + cx * r_ref[3:4, :] + r_ref[4:5, :])
        _, ig, a, mult = _lru_gates(xc, wa_ref[...], wx_ref[...], r_ref[5:6, :], r_ref[6:7, :], r_ref[7:8, :])
        sa[PAD:PAD + TS, :] = a
        sb[PAD:PAD + TS, :] = mult * ig * xc
        d = 1
        while d < TS:
            A, B = sa[PAD:PAD + TS, :], sb[PAD:PAD + TS, :]
            As, Bs = sa[pl.ds(PAD - d, TS), :], sb[pl.ds(PAD - d, TS), :]
            sb[PAD:PAD + TS, :] = A * Bs + B
            sa[PAD:PAD + TS, :] = A * As
            d *= 2
        h = sb[PAD:PAD + TS, :] + sa[PAD:PAD + TS, :] * carry[0:1, :]
        carry[0:1, :] = h[TS - 1:TS, :]
        h_ref[...] = h
        y_ref[...] = h * _silu(cg_ref[...])

    full = pl.BlockSpec((BR, BR), lambda i: (0, 0))
    spec = pl.BlockSpec((TS, BR), lambda i: (i, 0))
    return _call(
        body, name="lru_fwd", grid=(S // TS,),
        in_specs=[_chunk(TS, BR, 8), _halo_prev(TS, BR, 8), _chunk(TS, BR, 9), pl.BlockSpec((8, BR), lambda i: (0, 0)), full, full],
        out_specs=[spec, spec],
        out_shape=[jax.ShapeDtypeStruct((S, BR), F32)] * 2,
        scratch_shapes=[pltpu.VMEM((TS + 8, BR), F32), pltpu.VMEM((PAD + TS, BR), F32), pltpu.VMEM((PAD + TS, BR), F32),
                        pltpu.VMEM((8, BR), F32)],
        compiler_params=_params("arbitrary"),
    )(proj, proj, proj, rows, wa, wx)


S5_LAGS = 8
S5_CB = 8 * S5_CH
S5_SB = 8 * S5_STATE


def _s5_fwd(proj, a8, bbl, cre, cim, rows, wg, BR):
    S = proj.shape[0]
    W = a8.shape[1]
    TS = _tile(S, TILE_STREAM)
    CB, SB, J = S5_CB, S5_SB, S5_LAGS
    nblk = BR // CB

    def body(u_ref, uh_ref, dg_ref, a8_ref, bbl_ref, cre_ref, cim_ref, r_ref, wg_ref,
             y_ref, xre_ref, xim_ref, uext, ulag, sre, sim, ypre_s, carry):
        i = pl.program_id(0)

        @pl.when(i == 0)
        def _():
            carry[...] = jnp.zeros_like(carry)

        uext[0:8, :] = jnp.where(i > 0, uh_ref[...], 0.0)
        uext[8:8 + TS, :] = u_ref[...]
        for m in range(nblk):
            cs, ws = slice(m * CB, (m + 1) * CB), slice(m * SB, (m + 1) * SB)
            for j in range(J):
                ulag[:, j * CB:(j + 1) * CB] = uext[pl.ds(8 - j, TS), cs].astype(BF16)
            w = _dot(ulag[...], bbl_ref[m])
            sre[...] = w[:, 0:SB]
            sim[...] = w[:, SB:2 * SB]
            ar = jnp.broadcast_to(a8_ref[0:1, ws], (8, SB))
            ai = jnp.broadcast_to(a8_ref[1:2, ws], (8, SB))
            xr, xi = carry[0:8, ws], carry[8:16, ws]
            for g in range(TS // 8):
                rg = slice(8 * g, 8 * g + 8)
                xr, xi = sre[rg, :] + ar * xr - ai * xi, sim[rg, :] + ar * xi + ai * xr
                sre[rg, :] = xr
                sim[rg, :] = xi
            carry[0:8, ws] = xr
            carry[8:16, ws] = xi
            xre, xim = sre[...], sim[...]
            xre_ref[:, ws] = xre
            xim_ref[:, ws] = xim
            ypre_s[:, cs] = _dot(xre.astype(BF16), cre_ref[ws, :]) - _dot(xim.astype(BF16), cim_ref[ws, :])
        dg = dg_ref[...]
        yg = _gelu(ypre_s[...] + r_ref[0:1, :] * u_ref[...])
        s = _sigmoid(_dot(yg.astype(BF16), wg_ref[...]) + r_ref[1:2, :])
        y_ref[...] = yg * s * _silu(dg)

    def const(shape):
        return pl.BlockSpec(shape, lambda i: (0,) * len(shape))

    return _call(
        body, name="s5_fwd", grid=(S // TS,),
        in_specs=[_chunk(TS, BR, 10), _halo_prev(TS, BR, 10), _chunk(TS, BR, 11), const((8, W)), const((nblk, J * CB, 2 * SB)),
                  const((W, CB)), const((W, CB)), const((8, BR)), const((BR, BR))],
        out_specs=[pl.BlockSpec((TS, BR), lambda i: (i, 0)), pl.BlockSpec((TS, W), lambda i: (i, 0)), pl.BlockSpec((TS, W), lambda i: (i, 0))],
        out_shape=[jax.ShapeDtypeStruct((S, BR), F32), jax.ShapeDtypeStruct((S, W), F32), jax.ShapeDtypeStruct((S, W), F32)],
        scratch_shapes=[pltpu.VMEM((TS + 8, BR), F32), pltpu.VMEM((TS, J * CB), BF16), pltpu.VMEM((TS, SB), F32), pltpu.VMEM((TS, SB), F32),
                        pltpu.VMEM((TS, BR), F32), pltpu.VMEM((16, W), F32)],
        compiler_params=_params("arbitrary"),
    )(proj, proj, proj, a8, bbl, cre, cim, rows, wg)


def _out_ln(ys, w_out, x, rows, target=None):
    S, D = x.shape
    BR = D // 4
    TS = _tile(S, TILE_RESIDENT)
    last = target is not None

    def body(ya, yb, yc, yd, w_ref, x_ref, r_ref, *rest):
        t_ref = rest[0] if last else None
        o_ref, y_ref, cat_ref = rest[1:4] if last else rest[0:3]
        cat = jnp.concatenate([ya[...], yb[...], yc[...], yd[...]], axis=1).astype(BF16)
        y = _dot(cat, w_ref[...])
        z = ALPHA * x_ref[...] + r_ref[0:1, :] * y
        mu = jnp.mean(z, axis=-1, keepdims=True)
        zc = z - mu
        var = jnp.mean(zc * zc, axis=-1, keepdims=True)
        xn = zc * lax.rsqrt(var + LN_EPS) * r_ref[1:2, :] + r_ref[2:3, :]
        y_ref[...] = y.astype(BF16)
        cat_ref[...] = cat
        if last:
            l_ref = rest[4]

            @pl.when(pl.program_id(0) == 0)
            def _():
                l_ref[...] = jnp.zeros_like(l_ref)
            diff = xn - t_ref[...]
            o_ref[...] = diff * (1.0 / D)
            l_ref[...] += jnp.sum(_colsum(diff * diff), axis=1, keepdims=True) * (0.5 / D)
        else:
            o_ref[...] = xn

    yspec = pl.BlockSpec((TS, BR), lambda i: (i, 0))
    spec = pl.BlockSpec((TS, D), lambda i: (i, 0))
    lspec = pl.BlockSpec((1, 1), lambda i: (0, 0))
    return _call(
        body, name="out_ln_loss" if last else "out_ln", grid=(S // TS,),
        in_specs=[yspec] * 4 + [pl.BlockSpec((D, D), lambda i: (0, 0)), spec, pl.BlockSpec((8, D), lambda i: (0, 0))] + [spec] * last,
        out_specs=[spec, spec, spec] + [lspec] * last,
        out_shape=[jax.ShapeDtypeStruct((S, D), F32), jax.ShapeDtypeStruct((S, D), BF16), jax.ShapeDtypeStruct((S, D), BF16)]
        + [jax.ShapeDtypeStruct((1, 1), F32)] * last,
        compiler_params=_params("arbitrary"),
    )(*ys, w_out, x, rows, *([target] if last else []))


def _ln_bwd(dout, x, y, rows, w_out):
    S, D = x.shape
    TS = _tile(S, TILE_RESIDENT)

    def body(do_ref, x_ref, y_ref, r_ref, w_ref, dxr_ref, dyb_ref, dcat_ref, acc_ref):
        @pl.when(pl.program_id(0) == 0)
        def _():
            acc_ref[...] = jnp.zeros_like(acc_ref)
        g1, lg = r_ref[0:1, :], r_ref[1:2, :]
        yv = y_ref[...].astype(F32)
        z = ALPHA * x_ref[...] + g1 * yv
        mu = jnp.mean(z, axis=-1, keepdims=True)
        zc = z - mu
        var = jnp.mean(zc * zc, axis=-1, keepdims=True)
        rstd = lax.rsqrt(var + LN_EPS)
        xhat = zc * rstd
        do = do_ref[...]
        dxh = do * lg
        dz = rstd * (dxh - jnp.mean(dxh, axis=-1, keepdims=True) - xhat * jnp.mean(dxh * xhat, axis=-1, keepdims=True))
        dxr_ref[...] = (ALPHA * dz).astype(BF16)
        dyb = (g1 * dz).astype(BF16)
        dyb_ref[...] = dyb
        dcat_ref[...] = _dot_nt(dyb, w_ref[...])
        acc_ref[0:1, :] += _colsum(do * xhat)
        acc_ref[1:2, :] += _colsum(do)
        acc_ref[2:3, :] += _colsum(dz * yv)

    spec = pl.BlockSpec((TS, D), lambda i: (i, 0))
    rspec = pl.BlockSpec((8, D), lambda i: (0, 0))
    return _call(
        body, name="ln_bwd", grid=(S // TS,),
        in_specs=[spec, spec, spec, rspec, pl.BlockSpec((D, D), lambda i: (0, 0))],
        out_specs=[spec, spec, spec, rspec],
        out_shape=[jax.ShapeDtypeStruct((S, D), BF16), jax.ShapeDtypeStruct((S, D), BF16), jax.ShapeDtypeStruct((S, D), F32),
                   jax.ShapeDtypeStruct((8, D), F32)],
        compiler_params=_params("arbitrary"),
    )(dout, x, y, rows, w_out)


def _dh(dproj, wg, dxres, x, mod, sides):
    S, D = x.shape
    NS = wg.shape[2]
    TS = _tile(S, TILE_RESIDENT)
    nt = S // TS
    ns = len(sides)

    def body(dp_ref, w_hbm, dxr_ref, x_ref, mod_ref, *rest):
        s_refs, dx_ref, sum_ref, so_refs = rest[:ns], rest[ns], rest[ns + 1], rest[ns + 2:2 * ns + 2]
        w_vmem, sem, send_sems, recv_sems, local_sems = rest[2 * ns + 2:]
        i = pl.program_id(0)
        ix, iy, ic = lax.axis_index("x"), lax.axis_index("y"), lax.axis_index("c")
        me = 4 * ix + 2 * iy + ic

        def exchange(which):
            (_, gather), x_side, o_side = sides[which], s_refs[which], so_refs[which]
            copies = [pltpu.make_async_copy(x_side if gather else x_side.at[me], o_side.at[me], local_sems.at[which])]
            for k in range(1, NDEV):
                px = 1 - ix if k & 4 else ix
                py = 1 - iy if k & 2 else iy
                pc = 1 - ic if k & 1 else ic
                copies.append(pltpu.make_async_remote_copy(
                    src_ref=x_side if gather else x_side.at[4 * px + 2 * py + pc], dst_ref=o_side.at[me],
                    send_sem=send_sems.at[which, k - 1], recv_sem=recv_sems.at[which, k - 1],
                    device_id=(px, py, pc), device_id_type=pl.DeviceIdType.MESH))
            return copies

        @pl.when(i == 0)
        def _():
            load = pltpu.make_async_copy(w_hbm, w_vmem, sem)
            load.start()
            for which in range(ns):
                for cp in exchange(which):
                    cp.start()
            sum_ref[...] = jnp.zeros_like(sum_ref)
            load.wait()

        dh = _dot_nt(dp_ref[:, 0:NS], w_vmem[0])
        for j in range(1, NDEV):
            dh = dh + _dot_nt(dp_ref[:, j * NS:(j + 1) * NS], w_vmem[j])
        dx_ref[...] = dxr_ref[...].astype(F32) + dh * mod_ref[0:1, :]
        sum_ref[0:1, :] += _colsum(dh * x_ref[...])
        sum_ref[1:2, :] += _colsum(dh)

        @pl.when(i == nt - 1)
        def _():
            for which in range(ns):
                for cp in exchange(which):
                    cp.wait()

    spec = pl.BlockSpec((TS, D), lambda i: (i, 0))
    rspec = pl.BlockSpec((8, D), lambda i: (0, 0))
    anyspec = pl.BlockSpec(memory_space=pl.ANY)
    side_shapes = [jax.ShapeDtypeStruct(((NDEV,) + a.shape) if g else a.shape, a.dtype) for a, g in sides]
    return _call(
        body, name="dh", grid=(nt,),
        in_specs=[pl.BlockSpec((TS, NDEV * NS), lambda i: (i, 0)), anyspec, spec, spec, rspec] + [anyspec] * ns,
        out_specs=[spec, rspec] + [anyspec] * ns,
        out_shape=[jax.ShapeDtypeStruct((S, D), F32), jax.ShapeDtypeStruct((8, D), F32)] + side_shapes,
        scratch_shapes=[pltpu.VMEM((NDEV, D, NS), BF16), pltpu.SemaphoreType.DMA, pltpu.SemaphoreType.DMA((max(ns, 1), NDEV - 1)),
                        pltpu.SemaphoreType.DMA((max(ns, 1), NDEV - 1)), pltpu.SemaphoreType.DMA((max(ns, 1),))],
        compiler_params=_params("arbitrary"),
    )(dproj, wg, dxres, x, mod, *[a for a, _ in sides])


def _halo_next(S, TS, BR, k):
    last = S // 8 - 1
    return pl.BlockSpec((8, BR), lambda i: (jnp.minimum((i + 1) * (TS // 8), last), k))


def _conv_a_bwd(proj, dcat, w8, BR):
    S = proj.shape[0]
    TS = _tile(S, TILE_STREAM)
    nt = S // TS

    def body(ab_ref, ac_ref, ax_ref, ag_ref, dy_ref, hc_ref, hx_ref, nab_ref, nag_ref, ndy_ref, w_ref, dp_ref, dw_ref, ext, dext):
        i = pl.program_id(0)

        @pl.when(i == 0)
        def _():
            dw_ref[...] = jnp.zeros_like(dw_ref)

        w0, w1, w2 = w_ref[0:1, :], w_ref[1:2, :], w_ref[2:3, :]
        ab, ac, ax, ag, dy = ab_ref[...], ac_ref[...], ax_ref[...], ag_ref[...], dy_ref[...]
        u = ac * ax
        ext[0:8, :] = jnp.where(i > 0, hc_ref[...] * hx_ref[...], 0.0)
        ext[8:8 + TS, :] = u
        u1, u2 = ext[pl.ds(7, TS), :], ext[pl.ds(6, TS), :]
        conv = u2 * w0 + u1 * w1 + u * w2
        sg = _silu(ag)
        dconv = dy * ab * sg
        dext[0:TS, :] = dconv
        dext[TS:TS + 8, :] = jnp.where(i < nt - 1, ndy_ref[...] * nab_ref[...] * _silu(nag_ref[...]), 0.0)
        du = w2 * dconv + w1 * dext[pl.ds(1, TS), :] + w0 * dext[pl.ds(2, TS), :]
        dp_ref[:, 0:BR] = (dy * conv * sg).astype(BF16)
        dp_ref[:, BR:2 * BR] = (du * ax).astype(BF16)
        dp_ref[:, 2 * BR:3 * BR] = (du * ac).astype(BF16)
        dp_ref[:, 3 * BR:4 * BR] = (dy * ab * conv * _dsilu(ag)).astype(BF16)
        dw_ref[0:1, :] += _colsum(dconv * u2)
        dw_ref[1:2, :] += _colsum(dconv * u1)
        dw_ref[2:3, :] += _colsum(dconv * u)

    rspec = pl.BlockSpec((8, BR), lambda i: (0, 0))
    return _call(
        body, name="conv_a_bwd", grid=(nt,),
        in_specs=[_chunk(TS, BR, 0), _chunk(TS, BR, 1), _chunk(TS, BR, 2), _chunk(TS, BR, 3), _chunk(TS, BR, 0),
                  _halo_prev(TS, BR, 1), _halo_prev(TS, BR, 2), _halo_next(S, TS, BR, 0), _halo_next(S, TS, BR, 3),
                  _halo_next(S, TS, BR, 0), rspec],
        out_specs=[pl.BlockSpec((TS, 4 * BR), lambda i: (i, 0)), rspec],
        out_shape=[jax.ShapeDtypeStruct((S, 12 * BR), BF16), jax.ShapeDtypeStruct((8, BR), F32)],
        scratch_shapes=[pltpu.VMEM((TS + 8, BR), F32), pltpu.VMEM((TS + 8, BR), F32)],
        compiler_params=_params("arbitrary"),
    )(proj, proj, proj, proj, dcat, proj, proj, proj, proj, dcat, w8)


def _attn_mix_bwd(proj, dcat, os_, ls_, head_ones, BR):
    S = proj.shape[0]
    TS = _tile(S, TILE_STREAM)

    def body(dy_ref, g_ref, o1, o2, o3, l1, l2, l3, ones_ref, d1, d2, d3, e1, e2, e3, dg_ref):
        w1, w2, w3 = _mix_weights(l1[...], l2[...], l3[...])
        mix = w1 * o1[...] + w2 * o2[...] + w3 * o3[...]
        g = g_ref[...]
        dy = dy_ref[...]
        dmix = dy * _silu(g)
        dg_ref[...] = dy * mix * _dsilu(g)
        t = jnp.dot(dmix * mix, ones_ref[...], preferred_element_type=F32, precision=lax.Precision.HIGHEST)
        d1[...] = w1 * dmix
        d2[...] = w2 * dmix
        d3[...] = w3 * dmix
        e1[...] = -w1 * t
        e2[...] = -w2 * t
        e3[...] = -w3 * t

    spec = pl.BlockSpec((TS, BR), lambda i: (i, 0))
    outs = _call(
        body, name="attn_mix_bwd", grid=(S // TS,),
        in_specs=[_chunk(TS, BR, 1), _chunk(TS, BR, 7)] + [spec] * 6 + [pl.BlockSpec((BR, BR), lambda i: (0, 0))],
        out_specs=[spec] * 7,
        out_shape=[jax.ShapeDtypeStruct((S, BR), F32)] * 7,
        compiler_params=_params("arbitrary"),
    )(dcat, proj, *os_, *ls_, head_ones)
    return outs[0:3], outs[3:6], outs[6]


def _attn_bwd(proj, do, e, lse, bias, dil, BR):
    S = proj.shape[0]
    HD, U, NU, nbu, hpb, HBK = _attn_geometry(S, BR, dil)
    scale = HD ** -0.5
    combos = [(n, r) for n in range(nbu) for r in range(dil)]

    def body(q_ref, kc_ref, kp_ref, vc_ref, vp_ref, do_ref, e_ref, l_ref, b_ref, dq_ref, dk_ref, dv_ref, db_ref,
             sbuf, dpbuf, pbuf, dsbuf, qs, dos, kks, dkacc, dvacc, nxk, nxv, cark, carv):
        j = pl.program_id(1)
        mu = NU - 1 - j
        band, is_cur = _attn_mask()
        first = jnp.logical_and(band, jnp.logical_or(is_cur, mu > 0))

        @pl.when(j == 0)
        def _():
            cark[...] = jnp.zeros_like(cark)
            carv[...] = jnp.zeros_like(carv)
            db_ref[...] = jnp.zeros_like(db_ref)

        if nbu > 1:
            nxk[...] = jnp.zeros_like(nxk)
            nxv[...] = jnp.zeros_like(nxv)

        def keys(ref_c, ref_p, n, r):
            prev = ref_c[_attn_rows(n - 1, r, dil), :] if n > 0 else ref_p[_attn_rows(nbu - 1, r, dil), :]
            return jnp.concatenate([prev, ref_c[_attn_rows(n, r, dil), :]], axis=0).astype(BF16)

        for c, (n, r) in enumerate(combos):
            rows = _attn_rows(n, r, dil)
            q = (q_ref[rows, :] * scale).astype(BF16)
            dob = do_ref[rows, :].astype(BF16)
            kk, vv = keys(kc_ref, kp_ref, n, r), keys(vc_ref, vp_ref, n, r)
            qs[c], dos[c], kks[c] = q, dob, kk
            for h in range(hpb):
                sl = slice(h * HD, (h + 1) * HD)
                sbuf[c * hpb + h] = _dot_nt(q[:, sl], kk[:, sl])
                dpbuf[c * hpb + h] = _dot_nt(dob[:, sl], vv[:, sl])
        dbs = [None] * hpb
        for c, (n, r) in enumerate(combos):
            rows = _attn_rows(n, r, dil)
            lse, ev = l_ref[rows, :], e_ref[rows, :]
            for h in range(hpb):
                one = slice(h * HD, h * HD + 1)
                s = jnp.where(band if n > 0 else first, sbuf[c * hpb + h] + b_ref[h], NEG)
                p = jnp.exp(s - lse[:, one])
                ds = p * (dpbuf[c * hpb + h] + ev[:, one])
                pbuf[c * hpb + h] = p.astype(BF16)
                dsbuf[c * hpb + h] = ds.astype(BF16)
                dbs[h] = ds if dbs[h] is None else dbs[h] + ds
        for h in range(hpb):
            db_ref[h] += dbs[h]
        for c, (n, r) in enumerate(combos):
            rows = _attn_rows(n, r, dil)
            q, dob, kk = qs[c], dos[c], kks[c]
            dq, dkk, dvv = [], [], []
            for h in range(hpb):
                sl = slice(h * HD, (h + 1) * HD)
                dsb = dsbuf[c * hpb + h]
                dq.append(_dot(dsb, kk[:, sl]) * scale)
                dkk.append(_dot_tn(dsb, q[:, sl]))
                dvv.append(_dot_tn(pbuf[c * hpb + h], dob[:, sl]))
            dq_ref[rows, :] = jnp.concatenate(dq, axis=1)
            dkk, dvv = jnp.concatenate(dkk, axis=1), jnp.concatenate(dvv, axis=1)
            dkacc[rows, :] = dkk[BLK:2 * BLK, :]
            dvacc[rows, :] = dvv[BLK:2 * BLK, :]
            if n > 0:
                prow = _attn_rows(n - 1, r, dil)
                dkacc[prow, :] += dkk[0:BLK, :]
                dvacc[prow, :] += dvv[0:BLK, :]
            else:
                prow = _attn_rows(nbu - 1, r, dil)
                nxk[prow, :] = dkk[0:BLK, :]
                nxv[prow, :] = dvv[0:BLK, :]
        dk_ref[...] = dkacc[...] + cark[...]
        dv_ref[...] = dvacc[...] + carv[...]
        cark[...] = nxk[...]
        carv[...] = nxv[...]

    def cur(c):
        return pl.BlockSpec((U, ATT_LANES), lambda hb, j: (NU - 1 - j, c * HBK + hb))

    def prev(c):
        return pl.BlockSpec((U, ATT_LANES), lambda hb, j: (jnp.maximum(NU - 2 - j, 0), c * HBK + hb))

    own = pl.BlockSpec((U, ATT_LANES), lambda hb, j: (NU - 1 - j, hb))
    bspec = pl.BlockSpec((hpb, BLK, 2 * BLK), lambda hb, j: (hb, 0, 0))
    nhb = len(combos) * hpb
    unit = pltpu.VMEM((U, ATT_LANES), F32)
    return _call(
        body, name="attn_bwd_d%d" % dil, grid=(HBK, NU),
        in_specs=[cur(4), cur(5), prev(5), cur(6), prev(6), own, own, own, bspec],
        out_specs=[own, own, own, bspec],
        out_shape=[jax.ShapeDtypeStruct((S, BR), F32)] * 3 + [jax.ShapeDtypeStruct((8, BLK, 2 * BLK), F32)],
        scratch_shapes=[pltpu.VMEM((nhb, BLK, 2 * BLK), F32)] * 2 + [pltpu.VMEM((nhb, BLK, 2 * BLK), BF16)] * 2
        + [pltpu.VMEM((len(combos), BLK, ATT_LANES), BF16)] * 2 + [pltpu.VMEM((len(combos), 2 * BLK, ATT_LANES), BF16)] + [unit] * 6,
        compiler_params=_params("arbitrary", "arbitrary"),
    )(proj, proj, proj, proj, proj, do, e, lse, bias)


def _attn_dsum(ds, dbg, dproj, BR):
    S = dbg.shape[0]
    TS = _tile(S, TILE_STREAM)

    def body(*refs):
        o_ref = refs[11]
        for k in range(3):
            o_ref[:, k * BR:(k + 1) * BR] = (refs[k][...] + refs[3 + k][...] + refs[6 + k][...]).astype(BF16)
        o_ref[:, 3 * BR:4 * BR] = refs[9][...].astype(BF16)

    spec = pl.BlockSpec((TS, BR), lambda i: (i, 0))
    return _call(
        body, name="attn_dsum", grid=(S // TS,),
        in_specs=[spec] * 10 + [pl.BlockSpec(memory_space=pl.ANY)],
        out_specs=pl.BlockSpec((TS, 4 * BR), lambda i: (i, 1)),
        out_shape=jax.ShapeDtypeStruct(dproj.shape, BF16),
        input_output_aliases={10: 0},
        compiler_params=_params("arbitrary"),
    )(*[t for trip in ds for t in trip], dbg, dproj)


def _lru_bwd(proj, h, dcat, rows, wa, wx, dproj, BR):
    S = proj.shape[0]
    TS = _tile(S, TILE_RESIDENT)
    PAD = TS // 2
    nt = S // TS

    def body(cx_ref, hx_ref, cg_ref, h_ref, hh_ref, dy_ref, r_ref, wa_ref, wx_ref, _,
             dp_ref, sum_ref, dwa_ref, dwx_ref, ext, hext, aext, dext, sa, sb, carry):
        i = pl.program_id(0)
        ti = nt - 1 - i

        @pl.when(i == 0)
        def _():
            sa[TS:TS + PAD, :] = jnp.ones((PAD, BR), F32)
            sb[TS:TS + PAD, :] = jnp.zeros((PAD, BR), F32)
            carry[...] = jnp.zeros_like(carry)
            dext[TS:TS + 8, :] = jnp.zeros((8, BR), F32)
            sum_ref[...] = jnp.zeros_like(sum_ref)
            dwa_ref[...] = jnp.zeros_like(dwa_ref)
            dwx_ref[...] = jnp.zeros_like(dwx_ref)

        w0, w1, w2, w3 = r_ref[0:1, :], r_ref[1:2, :], r_ref[2:3, :], r_ref[3:4, :]
        sp = r_ref[7:8, :]
        cx = cx_ref[...]
        ext[0:8, :] = jnp.where(ti > 0, hx_ref[...], 0.0)
        ext[8:8 + TS, :] = cx
        x3, x2, x1 = ext[pl.ds(5, TS), :], ext[pl.ds(6, TS), :], ext[pl.ds(7, TS), :]
        xc = x3 * w0 + x2 * w1 + x1 * w2 + cx * w3 + r_ref[4:5, :]
        wa_, wx_ = wa_ref[...], wx_ref[...]
        r, ig, a, mult = _lru_gates(xc, wa_, wx_, r_ref[5:6, :], r_ref[6:7, :], sp)
        cg, dy, hv = cg_ref[...], dy_ref[...], h_ref[...]
        dp_ref[:, BR:2 * BR] = (dy * hv * _dsilu(cg)).astype(BF16)
        aext[0:TS, :] = a
        aext[TS:TS + 8, :] = jnp.broadcast_to(carry[0:1, :], (8, BR))
        sa[0:TS, :] = aext[pl.ds(1, TS), :]
        sb[0:TS, :] = dy * _silu(cg)
        d = 1
        while d < TS:
            A, B = sa[0:TS, :], sb[0:TS, :]
            As, Bs = sa[pl.ds(d, TS), :], sb[pl.ds(d, TS), :]
            sb[0:TS, :] = B + A * Bs
            sa[0:TS, :] = A * As
            d *= 2
        gh = sb[0:TS, :] + sa[0:TS, :] * carry[1:2, :]
        carry[0:1, :] = a[0:1, :]
        carry[1:2, :] = gh[0:1, :]
        hext[0:8, :] = jnp.where(ti > 0, hh_ref[...], 0.0)
        hext[8:8 + TS, :] = hv
        da = gh * hext[pl.ds(7, TS), :]
        dmult = gh * ig * xc
        dig = gh * mult * xc
        dxc = gh * mult * ig
        dla = da * a - dmult * a * a / mult
        dpr = dla * (-LRU_C) * sp * r * (1.0 - r)
        dpi = dig * ig * (1.0 - ig)
        xb, dprb, dpib = xc.astype(BF16), dpr.astype(BF16), dpi.astype(BF16)
        dwa_ref[...] += _dot_tn(xb, dprb)
        dwx_ref[...] += _dot_tn(xb, dpib)
        dxc = dxc + _dot_nt(dprb, wa_) + _dot_nt(dpib, wx_)
        sum_ref[0:1, :] += _colsum(dxc * x3)
        sum_ref[1:2, :] += _colsum(dxc * x2)
        sum_ref[2:3, :] += _colsum(dxc * x1)
        sum_ref[3:4, :] += _colsum(dxc * cx)
        sum_ref[4:5, :] += _colsum(dxc)
        sum_ref[5:6, :] += _colsum(dpr)
        sum_ref[6:7, :] += _colsum(dpi)
        sum_ref[7:8, :] += _colsum(dla * (-LRU_C) * r)
        dext[0:TS, :] = dxc
        dcx = w3 * dxc + w2 * dext[pl.ds(1, TS), :] + w1 * dext[pl.ds(2, TS), :] + w0 * dext[pl.ds(3, TS), :]
        dext[TS:TS + 8, :] = dxc[0:8, :]
        dp_ref[:, 0:BR] = dcx.astype(BF16)

    def rev(k):
        return pl.BlockSpec((TS, BR), lambda i: (nt - 1 - i, k))

    def rev_halo(k):
        return pl.BlockSpec((8, BR), lambda i: (jnp.maximum((nt - 1 - i) * (TS // 8) - 1, 0), k))

    full = pl.BlockSpec((BR, BR), lambda i: (0, 0))
    rspec = pl.BlockSpec((8, BR), lambda i: (0, 0))
    return _call(
        body, name="lru_bwd", grid=(nt,),
        in_specs=[rev(8), rev_halo(8), rev(9), rev(0), rev_halo(0), rev(2), rspec, full, full, pl.BlockSpec(memory_space=pl.ANY)],
        out_specs=[pl.BlockSpec((TS, 2 * BR), lambda i: (nt - 1 - i, 4)), rspec, full, full],
        input_output_aliases={9: 0},
        out_shape=[jax.ShapeDtypeStruct(dproj.shape, BF16), jax.ShapeDtypeStruct((8, BR), F32),
                   jax.ShapeDtypeStruct((BR, BR), F32), jax.ShapeDtypeStruct((BR, BR), F32)],
        scratch_shapes=[pltpu.VMEM((TS + 8, BR), F32)] * 4 + [pltpu.VMEM((TS + PAD, BR), F32)] * 2 + [pltpu.VMEM((8, BR), F32)],
        compiler_params=_params("arbitrary"),
    )(proj, proj, proj, h, h, dcat, rows, wa, wx, dproj)


def _s5_bwd(proj, dcat, xre_all, xim_all, a8, cl, bbre, bbim, cre, cim, rows, wg, dproj, BR):
    S = proj.shape[0]
    W = a8.shape[1]
    TS = _tile(S, TILE_STREAM)
    nt = S // TS
    CB, SB, J = S5_CB, S5_SB, S5_LAGS
    nblk = BR // CB

    def body(u_ref, dg_ref, dy_ref, xre_ref, xim_ref, hre_ref, him_ref, a8_ref, cl_ref, bbre_ref, bbim_ref, cre_ref, cim_ref,
             r_ref, wg_ref, _, dp_ref, sum_ref, dwg_ref, da_ref, dbbre_ref, dbbim_ref, dcre_ref, dcim_ref,
             sre, sim, ere, eim, ypre_s, dext, dylag, carry):
        i = pl.program_id(0)
        ti = nt - 1 - i

        @pl.when(i == 0)
        def _():
            dext[TS:TS + 8, :] = jnp.zeros((8, BR), F32)
            carry[...] = jnp.zeros_like(carry)
            for ref in (sum_ref, dwg_ref, da_ref, dbbre_ref, dbbim_ref, dcre_ref, dcim_ref):
                ref[...] = jnp.zeros_like(ref)

        u, dg, dy = u_ref[...], dg_ref[...], dy_ref[...]
        for m in range(nblk):
            cs, ws = slice(m * CB, (m + 1) * CB), slice(m * SB, (m + 1) * SB)
            ypre_s[:, cs] = (_dot(xre_ref[:, ws].astype(BF16), cre_ref[ws, :]) - _dot(xim_ref[:, ws].astype(BF16), cim_ref[ws, :]))
        ypre = ypre_s[...] + r_ref[0:1, :] * u
        yg = _gelu(ypre)
        wg_ = wg_ref[...]
        s = _sigmoid(_dot(yg.astype(BF16), wg_) + r_ref[1:2, :])
        dgl = dy * _silu(dg)
        dp_ref[:, BR:2 * BR] = (dy * yg * s * _dsilu(dg)).astype(BF16)
        dps = dgl * yg * s * (1.0 - s)
        dpsb = dps.astype(BF16)
        sum_ref[1:2, :] += _colsum(dps)
        dwg_ref[...] += _dot_tn(yg.astype(BF16), dpsb)
        dyp = (dgl * s + _dot_nt(dpsb, wg_)) * _dgelu(ypre)
        sum_ref[0:1, :] += _colsum(dyp * u)
        dext[0:TS, :] = dyp
        for m in range(nblk):
            cs, ws = slice(m * CB, (m + 1) * CB), slice(m * SB, (m + 1) * SB)
            dypb = dext[0:TS, cs].astype(BF16)
            xre, xim = xre_ref[:, ws], xim_ref[:, ws]
            dcre_ref[ws, :] += _dot_tn(xre.astype(BF16), dypb)
            dcim_ref[ws, :] -= _dot_tn(xim.astype(BF16), dypb)
            for j in range(J):
                dylag[:, j * CB:(j + 1) * CB] = dext[pl.ds(j, TS), cs].astype(BF16)
            w = _dot(dylag[...], cl_ref[m])
            sre[...] = w[:, 0:SB]
            sim[...] = w[:, SB:2 * SB]
            ar = jnp.broadcast_to(a8_ref[0:1, ws], (8, SB))
            ai = jnp.broadcast_to(a8_ref[1:2, ws], (8, SB))
            gr, gi = carry[0:8, ws], carry[8:16, ws]
            for g in reversed(range(TS // 8)):
                rg = slice(8 * g, 8 * g + 8)
                gr, gi = sre[rg, :] + ar * gr + ai * gi, sim[rg, :] + ar * gi - ai * gr
                sre[rg, :] = gr
                sim[rg, :] = gi
            carry[0:8, ws] = gr
            carry[8:16, ws] = gi
            gre, gim = sre[...], sim[...]
            ere[0:8, :] = jnp.where(ti > 0, hre_ref[:, ws], 0.0)
            eim[0:8, :] = jnp.where(ti > 0, him_ref[:, ws], 0.0)
            ere[8:8 + TS, :] = xre
            eim[8:8 + TS, :] = xim
            xpr, xpi = ere[pl.ds(7, TS), :], eim[pl.ds(7, TS), :]
            da_ref[0:1, ws] += _colsum(gre * xpr + gim * xpi)
            da_ref[1:2, ws] += _colsum(gim * xpr - gre * xpi)
            greb, gimb = gre.astype(BF16), gim.astype(BF16)
            ub = u_ref[:, cs].astype(BF16)
            dbbre_ref[cs, :] += _dot_tn(ub, greb)
            dbbim_ref[cs, :] += _dot_tn(ub, gimb)
            du = _dot_nt(greb, bbre_ref[cs, :]) + _dot_nt(gimb, bbim_ref[cs, :]) + r_ref[0:1, cs] * dext[0:TS, cs]
            dp_ref[:, cs] = du.astype(BF16)
        dext[TS:TS + 8, :] = dext[0:8, :]

    def rev(width, k):
        return pl.BlockSpec((TS, width), lambda i: (nt - 1 - i, k))

    def rev_halo(width):
        return pl.BlockSpec((8, width), lambda i: (jnp.maximum((nt - 1 - i) * (TS // 8) - 1, 0), 0))

    def const(shape):
        return pl.BlockSpec(shape, lambda i: (0,) * len(shape))

    return _call(
        body, name="s5_bwd", grid=(nt,),
        in_specs=[rev(BR, 10), rev(BR, 11), rev(BR, 3), rev(W, 0), rev(W, 0), rev_halo(W), rev_halo(W),
                  const((8, W)), const((nblk, J * CB, 2 * SB)), const((BR, SB)), const((BR, SB)), const((W, CB)), const((W, CB)),
                  const((8, BR)), const((BR, BR)), pl.BlockSpec(memory_space=pl.ANY)],
        input_output_aliases={15: 0},
        out_specs=[pl.BlockSpec((TS, 2 * BR), lambda i: (nt - 1 - i, 5)), const((8, BR)), const((BR, BR)), const((8, W)),
                   const((BR, SB)), const((BR, SB)), const((W, CB)), const((W, CB))],
        out_shape=[jax.ShapeDtypeStruct(dproj.shape, BF16), jax.ShapeDtypeStruct((8, BR), F32), jax.ShapeDtypeStruct((BR, BR), F32),
                   jax.ShapeDtypeStruct((8, W), F32), jax.ShapeDtypeStruct((BR, SB), F32), jax.ShapeDtypeStruct((BR, SB), F32),
                   jax.ShapeDtypeStruct((W, CB), F32), jax.ShapeDtypeStruct((W, CB), F32)],
        scratch_shapes=[pltpu.VMEM((TS, SB), F32)] * 2 + [pltpu.VMEM((TS + 8, SB), F32)] * 2
        + [pltpu.VMEM((TS, BR), F32), pltpu.VMEM((TS + 8, BR), F32), pltpu.VMEM((TS, J * CB), BF16), pltpu.VMEM((16, W), F32)],
        compiler_params=_params("arbitrary"),
    )(proj, proj, dcat, xre_all, xim_all, xre_all, xim_all, a8, cl, bbre, bbim, cre, cim, rows, wg, dproj)


_WEIGHTS = ['rel_bias', 'w_ada', 'b_ada', 'w_in', 'conv_a', 'conv_c', 'conv_c_b', 'lru_wa', 'lru_ba', 'lru_wx', 'lru_bx',
            'lru_lambda', 's5_lam_re', 's5_lam_im', 's5_log_dt', 's5_b_re', 's5_b_im', 's5_c_re', 's5_c_im', 's5_d',
            's5_w_glu', 's5_b_glu', 'w_out', 'ln_g', 'ln_b']
_LAYER_SMALL = ['conv_a', 'conv_c', 'conv_c_b', 'lru_wa', 'lru_ba', 'lru_wx', 'lru_bx', 'lru_lambda', 's5_lam_re', 's5_lam_im',
                's5_log_dt', 's5_b_re', 's5_b_im', 's5_c_re', 's5_c_im', 's5_d', 's5_b_glu']
_SMALL = ['rel_bias'] + _LAYER_SMALL + ['ln_g', 'ln_b']


def _t5_bucket(dist):
    max_exact = REL_BUCKETS // 2
    nf = jnp.maximum(dist, 1).astype(F32)
    large = max_exact + (jnp.log(nf / max_exact) / math.log(REL_MAX_DIST / max_exact)
                         * (REL_BUCKETS - max_exact)).astype(jnp.int32)
    large = jnp.minimum(large, REL_BUCKETS - 1)
    return jnp.where(dist < max_exact, dist, large)


def _bias_tables(rel_bias):
    i = jnp.arange(BLK)[:, None]
    j = jnp.arange(2 * BLK)[None, :]
    delta = i + BLK - j
    out = []
    for window, dil in DILATIONS:
        bucket = _t5_bucket(jnp.clip(delta, 0, window // dil) * dil)
        onehot = (bucket[:, :, None] == jnp.arange(REL_BUCKETS)[None, None, :]).astype(F32)
        out.append(jnp.einsum('ijb,bh->hij', onehot, rel_bias, precision=lax.Precision.HIGHEST))
    return jnp.stack(out)


def _prep_layer(q):
    eye8 = jnp.eye(8, dtype=F32)
    G = q['s5_lam_re'].shape[0]
    nblk = G // 8

    def block_diag(w):
        hd = w.shape[1]
        return (w[:, :, None, :] * eye8[:, None, :, None]).reshape(8 * hd, 8 * hd)

    def compact_b(bb):
        t = jnp.transpose(bb.reshape(nblk, 8, S5_STATE, S5_CH), (0, 1, 3, 2))
        return (t[:, :, :, None, :] * eye8[None, :, None, :, None]).reshape(G * S5_CH, 8 * S5_STATE)

    def compact_c(cc):
        t = jnp.transpose(cc.reshape(nblk, 8, S5_CH, S5_STATE), (0, 1, 3, 2))
        return (t[:, :, :, None, :] * eye8[None, :, None, :, None]).reshape(G * S5_STATE, 8 * S5_CH)

    lam_re, lam_im = q['s5_lam_re'], q['s5_lam_im']
    dt = jnp.exp(q['s5_log_dt'])[:, None]
    mag = jnp.exp(lam_re * dt)
    ab_re = mag * jnp.cos(lam_im * dt)
    ab_im = mag * jnp.sin(lam_im * dt)
    den = lam_re * lam_re + lam_im * lam_im
    f_re = ((ab_re - 1.0) * lam_re + ab_im * lam_im) / den
    f_im = (ab_im * lam_re - (ab_re - 1.0) * lam_im) / den
    b_re, b_im = q['s5_b_re'], q['s5_b_im']
    bb_re = f_re[..., None] * b_re - f_im[..., None] * b_im
    bb_im = f_re[..., None] * b_im + f_im[..., None] * b_re
    return dict(
        conv_a=q['conv_a'], conv_c=q['conv_c'], conv_c_b=q['conv_c_b'], lru_ba=q['lru_ba'], lru_bx=q['lru_bx'],
        sp=jax.nn.softplus(-q['lru_lambda']), wa=block_diag(q['lru_wa']), wx=block_diag(q['lru_wx']),
        ar=ab_re.reshape(-1), ai=ab_im.reshape(-1), bbre=compact_b(bb_re), bbim=compact_b(bb_im),
        cre=compact_c(q['s5_c_re']), cim=compact_c(q['s5_c_im']), s5_d=q['s5_d'], s5_b_glu=q['s5_b_glu'])


def _s5_lag_weights(kp):
    J, CB, SB = S5_LAGS, S5_CB, S5_SB
    ar, ai = kp['ar'], kp['ai']
    W = ar.shape[0]
    nblk = W // SB
    cr, ci = jnp.ones_like(ar), jnp.zeros_like(ai)
    pows = []
    for _ in range(J + 1):
        pows.append((cr, ci))
        cr, ci = cr * ar - ci * ai, cr * ai + ci * ar
    pwr = jnp.transpose(jnp.stack([p[0] for p in pows[:J]]).reshape(J, nblk, SB), (1, 0, 2))[:, :, None, :]
    pwi = jnp.transpose(jnp.stack([p[1] for p in pows[:J]]).reshape(J, nblk, SB), (1, 0, 2))[:, :, None, :]

    def lagged(re, im, sign):
        out = jnp.concatenate([re * pwr - sign * im * pwi, sign * re * pwi + im * pwr], axis=-1)
        return out.reshape(nblk, J * CB, 2 * SB).astype(BF16)

    bbl = lagged(kp['bbre'].reshape(nblk, 1, CB, SB), kp['bbim'].reshape(nblk, 1, CB, SB), 1.0)
    c0r = jnp.transpose(kp['cre'].reshape(nblk, SB, CB), (0, 2, 1))[:, None]
    c0i = -jnp.transpose(kp['cim'].reshape(nblk, SB, CB), (0, 2, 1))[:, None]
    cl = lagged(c0r, c0i, -1.0)
    return _rows8([pows[J][0], pows[J][1]], W), bbl, cl


def _rows8(vecs, width):
    rows = [v.reshape(1, width).astype(F32) for v in vecs]
    return jnp.concatenate(rows + [jnp.zeros((8 - len(rows), width), F32)], axis=0)


def kernel(x, c, rel_bias, w_ada, b_ada, w_in, conv_a, conv_c, conv_c_b, lru_wa, lru_ba, lru_wx, lru_bx, lru_lambda, s5_lam_re, s5_lam_im, s5_log_dt, s5_b_re, s5_b_im, s5_c_re, s5_c_im, s5_d, s5_w_glu, s5_b_glu, w_out, ln_g, ln_b, loss_target, m_rel_bias, m_w_ada, m_b_ada, m_w_in, m_conv_a, m_conv_c, m_conv_c_b, m_lru_wa, m_lru_ba, m_lru_wx, m_lru_bx, m_lru_lambda, m_s5_lam_re, m_s5_lam_im, m_s5_log_dt, m_s5_b_re, m_s5_b_im, m_s5_c_re, m_s5_c_im, m_s5_d, m_s5_w_glu, m_s5_b_glu, m_w_out, m_ln_g, m_ln_b, v_rel_bias, v_w_ada, v_b_ada, v_w_in, v_conv_a, v_conv_c, v_conv_c_b, v_lru_wa, v_lru_ba, v_lru_wx, v_lru_bx, v_lru_lambda, v_s5_lam_re, v_s5_lam_im, v_s5_log_dt, v_s5_b_re, v_s5_b_im, v_s5_c_re, v_s5_c_im, v_s5_d, v_s5_w_glu, v_s5_b_glu, v_w_out, v_ln_g, v_ln_b):
    a = dict(locals())
    me = 4 * lax.axis_index("x") + 2 * lax.axis_index("y") + lax.axis_index("c")
    x0, target = x[0], loss_target[0]
    S, D = x0.shape
    BR = D // 4
    NS = w_in.shape[2]
    SH = BR // NDEV
    assert S % (BLK * DILATIONS[-1][1]) == 0 and BR % (8 * S5_CH) == 0

    small = _all_gather(_pack([c, conv_a, conv_c]), "gather_small")
    per_dev = [_unpack(small[d], [c.shape, conv_a.shape, conv_c.shape]) for d in range(NDEV)]
    c_all = jnp.concatenate([p[0] for p in per_dev], axis=0)
    conv_a_full = jnp.concatenate([p[1] for p in per_dev], axis=2)
    conv_c_full = jnp.concatenate([p[2] for p in per_dev], axis=2)
    me1 = me.reshape(1).astype(jnp.int32)
    w_in_bf, w_out_bf, w_glu_bf = w_in.astype(BF16), w_out.astype(BF16), s5_w_glu.astype(BF16)

    b_cols = lax.dynamic_slice(b_ada, (0, me * NS), (DEPTH, NS)).reshape(DEPTH, 1, NS)
    ada_all = _all_gather(_ada_cols(c_all, w_ada, b_cols), "gather_ada")
    ada_me = lax.dynamic_index_in_dim(ada_all, me, axis=2, keepdims=False)
    ada_me = jnp.transpose(ada_me, (1, 0, 2)).reshape(DEPTH, 3 * D)
    shift, scale, gate = ada_me[:, :D], ada_me[:, D:2 * D], ada_me[:, 2 * D:]

    bias_tabs, bias_pull = jax.vjp(_bias_tables, rel_bias)
    HD = BR // 8
    head_ones = jnp.kron(jnp.eye(8, dtype=F32), jnp.ones((HD, HD), F32))

    saved = []
    xl = x0
    for l in range(DEPTH):
        q = {n: a[n][l] for n in _LAYER_SMALL}
        q['conv_a'], q['conv_c'] = conv_a_full[l], conv_c_full[l]
        kp, pull = jax.vjp(_prep_layer, q)
        a8, bbl, cl = _s5_lag_weights(kp)
        mod = _rows8([1.0 + scale[l], shift[l]], D)
        arows = _rows8(list(kp['conv_a']), BR)
        lrows = _rows8(list(kp['conv_c']) + [kp['conv_c_b'], kp['lru_ba'], kp['lru_bx'], kp['sp']], BR)
        srows = _rows8([kp['s5_d'], kp['s5_b_glu']], BR)
        orows = _rows8([1.0 + gate[l], ln_g[l], ln_b[l]], D)
        wa, wx = kp['wa'].astype(BF16), kp['wx'].astype(BF16)
        s5w = [kp[n].astype(BF16) for n in ('bbre', 'bbim', 'cre', 'cim')]

        hbf = _modulate(xl, mod)
        proj, w_in_l, w_out_l, w_glu_l = _proj_gather(hbf, w_in_bf[l], [w_out_bf[l], w_glu_bf[l]], me1)
        w_out_l, w_glu_l = w_out_l.reshape(D, D), w_glu_l.reshape(BR, BR)
        ya = _conv_a_fwd(proj, arows, BR)
        os_, ls_ = [], []
        for g, (_, dil) in enumerate(DILATIONS):
            o, lse = _attn_fwd(proj, bias_tabs[g], dil, BR)
            os_.append(o)
            ls_.append(lse)
        yb = _attn_mix_fwd(proj, os_, ls_, BR)
        yc, hs = _lru_fwd(proj, lrows, wa, wx, BR)
        yd, xre, xim = _s5_fwd(proj, a8, bbl, s5w[2], s5w[3], srows, w_glu_l, BR)
        if l < DEPTH - 1:
            xn, y, cat = _out_ln([ya, yb, yc, yd], w_out_l, xl, orows)
        else:
            xn, y, cat, loss_local = _out_ln([ya, yb, yc, yd], w_out_l, xl, orows, target)
        saved.append(dict(x=xl, proj=proj, hbf=hbf, os=os_, ls=ls_, hs=hs, xre=xre, xim=xim, y=y, cat=cat, pull=pull,
                          mod=mod, arows=arows, lrows=lrows, srows=srows, orows=orows, wa=wa, wx=wx, s5w=s5w,
                          qw=(a8, cl), w_in=w_in_l, w_out=w_out_l, w_glu=w_glu_l))
        xl = xn

    dout = xl

    dbias = jnp.zeros_like(bias_tabs)
    lgrads, dada, dw_in_parts, dw_out_parts, glu_parts = [None] * DEPTH, [None] * DEPTH, [None] * DEPTH, [None] * DEPTH, [None] * DEPTH
    for l in reversed(range(DEPTH)):
        sv = saved[l]
        proj = sv['proj']
        dxres, dyb, dcat, lnsum = _ln_bwd(dout, sv['x'], sv['y'], sv['orows'], sv['w_out'])
        dw_out_parts[l] = _dw_scatter(sv['cat'], dyb, D // NDEV, True, me1, "dw_out_scatter")
        dproj, asum = _conv_a_bwd(proj, dcat, sv['arows'], BR)
        dos, es, dbg = _attn_mix_bwd(proj, dcat, sv['os'], sv['ls'], head_ones, BR)
        dqkv, dbs = [], []
        for g, (_, dil) in enumerate(DILATIONS):
            dq_, dk_, dv_, db_ = _attn_bwd(proj, dos[g], es[g], sv['ls'][g], bias_tabs[g], dil, BR)
            dqkv.append((dq_, dk_, dv_))
            dbs.append(db_)
        dbias = dbias + jnp.stack(dbs)
        dproj = _attn_dsum(dqkv, dbg, dproj, BR)
        dproj, lsum, dwa, dwx = _lru_bwd(proj, sv['hs'], dcat, sv['lrows'], sv['wa'], sv['wx'], dproj, BR)
        dproj, ssum, dwg, da_, dbbre, dbbim, dcre, dcim = _s5_bwd(
            proj, dcat, sv['xre'], sv['xim'], *sv['qw'], *sv['s5w'], sv['srows'], sv['w_glu'], dproj, BR)
        dkp = dict(conv_a=asum[0:3], conv_c=lsum[0:4], conv_c_b=lsum[4], lru_ba=lsum[5], lru_bx=lsum[6], sp=lsum[7],
                   wa=dwa, wx=dwx, ar=da_[0], ai=da_[1], bbre=dbbre, bbim=dbbim, cre=dcre, cim=dcim,
                   s5_d=ssum[0], s5_b_glu=ssum[1])
        lgrads[l] = dict(sv['pull'](dkp)[0], ln_g=lnsum[0], ln_b=lnsum[1])
        sides = [(dwg.reshape(NDEV, BR // NDEV, BR), False)]
        if l == 0:
            local = {'rel_bias': bias_pull(dbias)[0], 'loss': loss_local.reshape(1)}
            for n in _LAYER_SMALL + ['ln_g', 'ln_b']:
                local[n] = jnp.stack([lgrads[k][n] for k in range(DEPTH)])
            packed = _SMALL + ['loss']
            sides.append((_pack([local[n] for n in packed]), True))
        dout, msum, glu_parts[l], *gathered = _dh(dproj, sv['w_in'], dxres, sv['x'], sv['mod'], sides)
        dw_in_parts[l] = _dw_scatter(sv['hbf'], dproj, NS, False, me1, "dw_in_scatter")
        dada[l] = jnp.concatenate([msum[1], msum[0], lnsum[2]])
    grad_x = dout[None]

    out = {}

    def put(name, res):
        out[name] = res

    put('w_in', _adamw(dw_in_parts, w_in, m_w_in, v_w_in, "adamw_w_in"))
    put('w_out', _adamw(dw_out_parts, w_out, m_w_out, v_w_out, "adamw_w_out"))
    put('s5_w_glu', _adamw(glu_parts, s5_w_glu, m_s5_w_glu, v_s5_w_glu, "adamw_w_glu"))

    dada_all = _all_gather(jnp.stack(dada), "gather_dada")
    dada_cols = jnp.transpose(lax.dynamic_slice(dada_all, (0, 0, me * NS), (NDEV, DEPTH, NS)), (1, 0, 2))
    gw_ada, gb_ada = _ada_grads(c_all, dada_all, dada_cols)
    put('w_ada', _adamw([gw_ada[l] for l in range(DEPTH)], w_ada, m_w_ada, v_w_ada, "adamw_w_ada"))

    shapes = [local[n].shape for n in packed]
    gs = dict(zip(packed, _unpack(_sum8(gathered[0], "sum_small_grads"), shapes)))
    loss = gs['loss'][0]
    gs['conv_a'] = lax.dynamic_slice_in_dim(gs['conv_a'], me * SH, SH, axis=2)
    gs['conv_c'] = lax.dynamic_slice_in_dim(gs['conv_c'], me * SH, SH, axis=2)
    gs['b_ada'] = gb_ada
    names = _SMALL + ['b_ada']
    shapes = [a[n].shape for n in names]
    res = _adamw([_pack([gs[n] for n in names])], _pack([a[n] for n in names])[None], _pack([a['m_' + n] for n in names])[None],
                 _pack([a['v_' + n] for n in names])[None], "adamw_small")
    unpacked = [_unpack(r[0], shapes) for r in res]
    for k, n in enumerate(names):
        put(n, tuple(unpacked[j][k] for j in range(4)))

    return (loss, grad_x, *[out[n][0] for n in _WEIGHTS], *[out[n][1] for n in _WEIGHTS],
            *[out[n][2] for n in _WEIGHTS], *[out[n][3] for n in _WEIGHTS])
```

```python
import math

import jax
import jax.numpy as jnp
from jax import lax
from jax.experimental import pallas as pl
from jax.experimental.pallas import tpu as pltpu

F32 = jnp.float32
BF16 = jnp.bfloat16

NDEV = 8
DEPTH = 2
BLK = 128
DILATIONS = ((128, 1), (512, 4), (2048, 16))
REL_BUCKETS = 32
REL_MAX_DIST = 2048
LRU_C = 8.0
S5_CH = 16
S5_STATE = 64
ALPHA = (2 * DEPTH) ** 0.25
LN_EPS = 1e-5
ADAM_LR, ADAM_B1, ADAM_B2, ADAM_EPS, ADAM_WD, ADAM_STEP = 0.001, 0.9, 0.999, 1e-08, 0.01, 10
NEG = -1e30
VMEM_LIMIT = 56 * 1024 * 1024
TILE_STREAM = 512
TILE_RESIDENT = 256
TILE_PROJ = 2048


def _call(body, **kw):
    return pl.pallas_call(body, **kw)


def _params(*sem):
    return pltpu.CompilerParams(dimension_semantics=sem, vmem_limit_bytes=VMEM_LIMIT)


def _sigmoid(x):
    return 1.0 / (1.0 + jnp.exp(-x))


def _silu(x):
    return x * _sigmoid(x)


def _dsilu(x):
    s = _sigmoid(x)
    return s * (1.0 + x * (1.0 - s))


_GC = math.sqrt(2.0 / math.pi)


def _gelu(x):
    return 0.5 * x * (1.0 + jnp.tanh(_GC * (x + 0.044715 * x * x * x)))


def _dgelu(x):
    t = jnp.tanh(_GC * (x + 0.044715 * x * x * x))
    return 0.5 * (1.0 + t) + 0.5 * x * (1.0 - t * t) * _GC * (1.0 + 3.0 * 0.044715 * x * x)


def _expm1(x):
    series = x * (1.0 + x * (1.0 / 2) * (1.0 + x * (1.0 / 3) * (1.0 + x * (1.0 / 4) * (1.0 + x * (1.0 / 5) * (1.0 + x * (1.0 / 6))))))
    return jnp.where(jnp.abs(x) < 0.25, series, jnp.exp(x) - 1.0)


def _dot(a, b):
    return jnp.dot(a, b, preferred_element_type=F32)


def _dot_nt(a, b):
    return lax.dot_general(a, b, (((1,), (1,)), ((), ())), preferred_element_type=F32)


def _dot_tn(a, b):
    return lax.dot_general(a, b, (((0,), (0,)), ((), ())), preferred_element_type=F32)


def _colsum(v):
    return jnp.sum(v, axis=0, keepdims=True)


def _all_gather(x, name):
    def body(x_ref, o_ref, send_sems, recv_sems, local_sem):
        ix, iy, ic = lax.axis_index("x"), lax.axis_index("y"), lax.axis_index("c")
        me, sibling = (ix, iy, ic), (ix, iy, 1 - ic)
        chips = [(1 - ix, iy), (ix, 1 - iy), (1 - ix, 1 - iy)]

        def slot(px, py, pc):
            return o_ref.at[4 * px + 2 * py + pc]

        def copy(k, block, to, src=None):
            return pltpu.make_async_remote_copy(
                src_ref=slot(*block) if src is None else src, dst_ref=slot(*block), send_sem=send_sems.at[k],
                recv_sem=recv_sems.at[k], device_id=to, device_id_type=pl.DeviceIdType.MESH)

        mine = pltpu.make_async_copy(x_ref, slot(*me), local_sem)
        mine.start()
        first = [copy(0, me, sibling, src=x_ref)] + [copy(1 + j, me, (*chip, ic), src=x_ref) for j, chip in enumerate(chips)]
        for cp in first:
            cp.start()
        passed = [copy(4 + j, (*chip, ic), sibling) for j, chip in enumerate(chips)]
        for j, chip in enumerate(chips):
            copy(1 + j, (*chip, ic), me).wait_recv()
            passed[j].start()
        copy(0, sibling, me).wait_recv()
        for j, chip in enumerate(chips):
            copy(4 + j, (*chip, 1 - ic), me).wait_recv()
        for cp in first + passed:
            cp.wait_send()
        mine.wait()

    return _call(
        body, name=name,
        out_shape=jax.ShapeDtypeStruct((NDEV,) + x.shape, x.dtype),
        in_specs=[pl.BlockSpec(memory_space=pl.ANY)],
        out_specs=pl.BlockSpec(memory_space=pl.ANY),
        scratch_shapes=[pltpu.SemaphoreType.DMA((NDEV - 1,)), pltpu.SemaphoreType.DMA((NDEV - 1,)), pltpu.SemaphoreType.DMA],
    )(x)


def _pack(arrs):
    parts = []
    for a in arrs:
        n = math.prod(a.shape)
        parts.append(jnp.pad(a.reshape(-1).astype(F32), (0, _pack_rows(n) * 128 - n)).reshape(-1, 128))
    rows = sum(p.shape[0] for p in parts)
    total = rows if rows <= 512 else -(-rows // 512) * 512
    if total > rows:
        parts.append(jnp.zeros((total - rows, 128), F32))
    return jnp.concatenate(parts, axis=0)


def _pack_rows(n):
    return -(-n // 1024) * 8


def _unpack(packed, shapes):
    out, row = [], 0
    for s in shapes:
        n = math.prod(s)
        rows = _pack_rows(n)
        out.append(packed[row:row + rows].reshape(-1)[:n].reshape(s))
        row += rows
    return out


def _sum8(parts, name):
    _, R, C = parts.shape
    TR = R
    for cand in (512, 256, 128, 64, 32, 16, 8):
        if R % cand == 0:
            TR = cand
            break

    def body(p_ref, o_ref):
        acc = p_ref[0]
        for d in range(1, NDEV):
            acc = acc + p_ref[d]
        o_ref[...] = acc

    return _call(
        body, name=name, grid=(R // TR,),
        in_specs=[pl.BlockSpec((NDEV, TR, C), lambda i: (0, i, 0))],
        out_specs=pl.BlockSpec((TR, C), lambda i: (i, 0)),
        out_shape=jax.ShapeDtypeStruct((R, C), F32),
        compiler_params=_params("arbitrary"),
    )(parts)


def _adamw(g_layers, w, m, v, name):
    L, R, C = w.shape
    g0 = g_layers[0]
    nparts = g0.shape[0] if g0.ndim == 3 else 0
    TR = R
    for cand in (256, 128, 64, 32, 16):
        if R % cand == 0 and cand * C <= 128 * 1024:
            TR = cand
            break

    def body(*refs):
        g_refs = refs[:L]
        w_ref, m_ref, v_ref, go_ref, d_ref, mo_ref, vo_ref = refs[L:]
        for ll in range(L):
            @pl.when(pl.program_id(0) == ll)
            def _(g_ref=g_refs[ll]):
                if nparts:
                    g = g_ref[0].astype(F32)
                    for d in range(1, nparts):
                        g = g + g_ref[d].astype(F32)
                else:
                    g = g_ref[...]
                m2 = ADAM_B1 * m_ref[0] + (1.0 - ADAM_B1) * g
                v2 = ADAM_B2 * v_ref[0] + (1.0 - ADAM_B2) * (g * g)
                m_hat = m2 / (1.0 - ADAM_B1 ** ADAM_STEP)
                v_hat = v2 / (1.0 - ADAM_B2 ** ADAM_STEP)
                go_ref[0] = g
                d_ref[0] = -ADAM_LR * (m_hat / (jnp.sqrt(v_hat) + ADAM_EPS) + ADAM_WD * w_ref[0])
                mo_ref[0] = m2
                vo_ref[0] = v2

    def gspec(ll):
        if nparts:
            return pl.BlockSpec((nparts, TR, C), lambda l, i: (0, jnp.where(l == ll, i, 0), 0))
        return pl.BlockSpec((TR, C), lambda l, i: (jnp.where(l == ll, i, 0), 0))

    spec = pl.BlockSpec((1, TR, C), lambda l, i: (l, i, 0))
    return _call(
        body, name=name, grid=(L, R // TR),
        in_specs=[gspec(ll) for ll in range(L)] + [spec, spec, spec], out_specs=[spec] * 4,
        out_shape=[jax.ShapeDtypeStruct((L, R, C), F32)] * 4,
        compiler_params=_params("arbitrary", "arbitrary"),
    )(*g_layers, w, m, v)


def _ada_cols(c_all, w_ada, b_cols):
    L, D, NS = w_ada.shape

    def body(c_ref, w_ref, b_ref, o_ref):
        cond = _silu(c_ref[...]).astype(BF16)
        o_ref[0] = _dot(cond, w_ref[0].astype(BF16)) + b_ref[0]

    return _call(
        body, name="ada_cols", grid=(L,),
        in_specs=[pl.BlockSpec((NDEV, D), lambda l: (0, 0)), pl.BlockSpec((1, D, NS), lambda l: (l, 0, 0)),
                  pl.BlockSpec((1, 1, NS), lambda l: (l, 0, 0))],
        out_specs=pl.BlockSpec((1, NDEV, NS), lambda l: (l, 0, 0)),
        out_shape=jax.ShapeDtypeStruct((L, NDEV, NS), F32),
        compiler_params=_params("arbitrary"),
    )(c_all, w_ada, b_cols)


def _ada_grads(c_all, dada_all, dada_cols):
    _, D = c_all.shape
    L, _, NS = dada_cols.shape
    D3 = dada_all.shape[2]

    def body(c_ref, da_ref, dc_ref, gw_ref, gb_ref):
        cond = _silu(c_ref[...]).astype(BF16)
        gw_ref[0] = _dot_tn(cond, dc_ref[0].astype(BF16))
        acc = da_ref[0]
        for d in range(1, NDEV):
            acc = acc + da_ref[d]
        gb_ref[...] = acc

    return _call(
        body, name="ada_grads", grid=(L,),
        in_specs=[pl.BlockSpec((NDEV, D), lambda l: (0, 0)), pl.BlockSpec((NDEV, L, D3), lambda l: (0, 0, 0)),
                  pl.BlockSpec((1, NDEV, NS), lambda l: (l, 0, 0))],
        out_specs=[pl.BlockSpec((1, D, NS), lambda l: (l, 0, 0)), pl.BlockSpec((L, D3), lambda l: (0, 0))],
        out_shape=[jax.ShapeDtypeStruct((L, D, NS), F32), jax.ShapeDtypeStruct((L, D3), F32)],
        compiler_params=_params("arbitrary"),
    )(c_all, dada_all, dada_cols)


def _tile(S, want):
    return want if S % want == 0 else S


def _modulate(x, mod):
    S, D = x.shape
    TS = _tile(S, TILE_STREAM)

    def body(x_ref, mod_ref, h_ref):
        h_ref[...] = (x_ref[...] * mod_ref[0:1, :] + mod_ref[1:2, :]).astype(BF16)

    spec = pl.BlockSpec((TS, D), lambda i: (i, 0))
    return _call(
        body, name="modulate", grid=(S // TS,),
        in_specs=[spec, pl.BlockSpec((8, D), lambda i: (0, 0))], out_specs=spec,
        out_shape=jax.ShapeDtypeStruct((S, D), BF16),
        compiler_params=_params("arbitrary"),
    )(x, mod)


def _peer(k):
    ix, iy, ic = lax.axis_index("x"), lax.axis_index("y"), lax.axis_index("c")
    bx, by, bc = (k >> 2) & 1, (k >> 1) & 1, k & 1
    px, py, pc = ix + bx - 2 * ix * bx, iy + by - 2 * iy * by, ic + bc - 2 * ic * bc
    return (px, py, pc), 4 * px + 2 * py + pc


GATHER_ORDER = (0, 1, 2, 4, 6, 3, 5, 7)
GATHER_DIRECT = (1, 2, 4, 6)
SCATTER_ORDER = (3, 2, 5, 4, 7, 6, 1, 0)


def _offset(order, k):
    off = jnp.int32(order[-1])
    for step in reversed(range(len(order) - 1)):
        off = jnp.where(k == step, jnp.int32(order[step]), off)
    return off


def _proj_gather(h, w_shard, extras, me1):
    assert GATHER_ORDER[0] == 0
    S, D = h.shape
    NS = w_shard.shape[1]
    TS = _tile(S, TILE_PROJ)
    nt = S // TS
    ne = len(extras)

    def body(me_ref, h_ref, w_ref, *rest):
        x_refs, o_ref, wg_ref = rest[:ne], rest[ne], rest[ne + 1]
        xg_refs = rest[ne + 2:2 * ne + 2]
        wbuf, send_sems, recv_sems, local_sems, load_sems = rest[2 * ne + 2:]
        k, i = pl.program_id(0), pl.program_id(1)
        me = me_ref[0]
        off = _offset(GATHER_ORDER, k)

        def push(src, dst, which, kk):
            peer, _ = _peer(kk)
            return pltpu.make_async_remote_copy(
                src_ref=src, dst_ref=dst.at[me], send_sem=send_sems.at[which, kk], recv_sem=recv_sems.at[which, kk],
                device_id=peer, device_id_type=pl.DeviceIdType.MESH)

        def own(src, dst, which):
            return pltpu.make_async_copy(src, dst.at[me], local_sems.at[which])

        pairs = [(w_ref, wg_ref)] + list(zip(x_refs, xg_refs))

        @pl.when(jnp.logical_and(k == 0, i == 0))
        def _():
            for which, (src, dst) in enumerate(pairs):
                own(src, dst, which).start()
                for kk in (GATHER_DIRECT if which == 0 else GATHER_ORDER[1:]):
                    push(src, dst, which, kk).start()
            first = pltpu.make_async_copy(w_ref, wbuf.at[0], load_sems.at[0])
            first.start()
            first.wait()

        @pl.when(jnp.logical_and(k > 0, i == 0))
        def _():
            push(w_ref, wg_ref, 0, off).wait_recv()
            _, blk = _peer(off)
            for step, o in enumerate(GATHER_ORDER):
                if o in GATHER_DIRECT and o > 1:
                    @pl.when(k == step)
                    def _(o=o):
                        sibling, _ = _peer(1)
                        pltpu.make_async_remote_copy(
                            src_ref=wg_ref.at[blk], dst_ref=wg_ref.at[blk], send_sem=send_sems.at[0, o + 1],
                            recv_sem=recv_sems.at[0, o + 1], device_id=sibling, device_id_type=pl.DeviceIdType.MESH).start()
            load = pltpu.make_async_copy(wg_ref.at[blk], wbuf.at[k % 2], load_sems.at[k % 2])
            load.start()
            load.wait()

        o_ref[...] = _dot(h_ref[...], wbuf[k % 2])

        @pl.when(jnp.logical_and(k == NDEV - 1, i == nt - 1))
        def _():
            for which, (src, dst) in enumerate(pairs):
                for kk in range(1, NDEV):
                    cp = push(src, dst, which, kk)
                    cp.wait_send()
                    if which > 0:
                        cp.wait_recv()
                own(src, dst, which).wait()

    def col(k, i, me_ref):
        m, off = me_ref[0], _offset(GATHER_ORDER, k)
        return i, (m | off) - (m & off)

    anyspec = pl.BlockSpec(memory_space=pl.ANY)
    grid_spec = pltpu.PrefetchScalarGridSpec(
        num_scalar_prefetch=1, grid=(NDEV, nt),
        in_specs=[pl.BlockSpec((TS, D), lambda k, i, me_ref: (i, 0)), anyspec] + [anyspec] * ne,
        out_specs=[pl.BlockSpec((TS, NS), col), anyspec] + [anyspec] * ne,
        scratch_shapes=[pltpu.VMEM((2, D, NS), BF16), pltpu.SemaphoreType.DMA((1 + ne, NDEV)), pltpu.SemaphoreType.DMA((1 + ne, NDEV)),
                        pltpu.SemaphoreType.DMA((1 + ne,)), pltpu.SemaphoreType.DMA((2,))])
    return _call(
        body, name="proj_gather", grid_spec=grid_spec,
        out_shape=[jax.ShapeDtypeStruct((S, NDEV * NS), F32), jax.ShapeDtypeStruct((NDEV, D, NS), BF16)]
        + [jax.ShapeDtypeStruct((NDEV,) + e.shape, e.dtype) for e in extras],
        compiler_params=_params("arbitrary", "arbitrary"),
    )(me1, h, w_shard, *extras)


def _side_copies(gather, x_side, o_side, send_sems, recv_sems, local_sems, which):
    ix, iy, ic = lax.axis_index("x"), lax.axis_index("y"), lax.axis_index("c")
    me = 4 * ix + 2 * iy + ic
    copies = [pltpu.make_async_copy(x_side if gather else x_side.at[me], o_side.at[me], local_sems.at[which])]
    for k in range(1, NDEV):
        px = 1 - ix if k & 4 else ix
        py = 1 - iy if k & 2 else iy
        pc = 1 - ic if k & 1 else ic
        copies.append(pltpu.make_async_remote_copy(
            src_ref=x_side if gather else x_side.at[4 * px + 2 * py + pc], dst_ref=o_side.at[me],
            send_sem=send_sems.at[which, k - 1], recv_sem=recv_sems.at[which, k - 1],
            device_id=(px, py, pc), device_id_type=pl.DeviceIdType.MESH))
    return copies


def _side_scratch(ns):
    n = max(ns, 1)
    return [pltpu.SemaphoreType.DMA((n, NDEV - 1)), pltpu.SemaphoreType.DMA((n, NDEV - 1)), pltpu.SemaphoreType.DMA((n,))]


def _side_shapes(sides):
    return [jax.ShapeDtypeStruct(((NDEV,) + a.shape) if g else a.shape, a.dtype) for a, g in sides]


def _dw_scatter(a, b, blk, by_rows, me1, name, sides=()):
    S = a.shape[0]
    shape = (blk, b.shape[1]) if by_rows else (a.shape[1], blk)
    fixed = shape[0] * shape[1] * (4 + 2 * 2 + 4 * 2)
    per_row = 2 * 2 * (shape[0] + shape[1])
    TK = next((t for t in (2048, 1024, 512, 256) if S % t == 0 and fixed + t * per_row <= 0.8 * VMEM_LIMIT), S)
    nk = S // TK
    last = NDEV - 1
    assert all(o & 1 for o in SCATTER_ORDER[0::2]) and not any(o & 1 for o in SCATTER_ORDER[1::2]) and SCATTER_ORDER[-1] == 0
    assert all(SCATTER_ORDER[s] == SCATTER_ORDER[s + 1] + 1 for s in range(0, NDEV, 2))

    ns = len(sides)

    def body(me_ref, a_ref, b_ref, *rest):
        s_refs, recv_ref, so_refs = rest[:ns], rest[ns], rest[ns + 1:2 * ns + 1]
        acc, stage, sib, send_sems, recv_sems, local_sem, side_send, side_recv, side_local = rest[2 * ns + 1:]
        k, kk = pl.program_id(0), pl.program_id(1)

        def exchange(which):
            return _side_copies(sides[which][1], s_refs[which], so_refs[which], side_send, side_recv, side_local, which)

        @pl.when(jnp.logical_and(k == 0, kk == 0))
        def _():
            for which in range(ns):
                for cp in exchange(which):
                    cp.start()

        def chip_of(step):
            return _offset(SCATTER_ORDER, step) >> 1

        def to_sibling(j, slot):
            sibling, _ = _peer(1)
            return pltpu.make_async_remote_copy(
                src_ref=stage.at[slot], dst_ref=sib.at[j], send_sem=send_sems.at[0, j], recv_sem=recv_sems.at[0, j],
                device_id=sibling, device_id_type=pl.DeviceIdType.MESH)

        def to_chip(j, slot):
            peer, _ = _peer(2 * j)
            return pltpu.make_async_remote_copy(
                src_ref=stage.at[slot], dst_ref=recv_ref.at[j], send_sem=send_sems.at[1, j], recv_sem=recv_sems.at[1, j],
                device_id=peer, device_id_type=pl.DeviceIdType.MESH)

        own = pltpu.make_async_copy(stage.at[last % 2], recv_ref.at[0], local_sem)

        @pl.when(kk == 0)
        def _():
            acc[...] = jnp.zeros_like(acc)
        acc[...] += _dot_tn(a_ref[...], b_ref[...])

        @pl.when(kk == nk - 1)
        def _():
            j = chip_of(k)
            even_step = k % 2 == 0

            @pl.when(jnp.logical_and(k >= 2, even_step))
            def _():
                to_sibling(chip_of(k - 2), k % 2).wait_send()

            @pl.when(jnp.logical_and(k >= 3, jnp.logical_not(even_step)))
            def _():
                to_chip(chip_of(k - 2), k % 2).wait_send()

            @pl.when(even_step)
            def _():
                stage[k % 2] = acc[...].astype(BF16)
                to_sibling(j, k % 2).start()

            @pl.when(jnp.logical_not(even_step))
            def _():
                to_sibling(j, 0).wait_recv()
                stage[k % 2] = (acc[...] + sib[j].astype(F32)).astype(BF16)

                @pl.when(k < last)
                def _():
                    to_chip(j, k % 2).start()

                @pl.when(k == last)
                def _():
                    own.start()
                    to_sibling(SCATTER_ORDER[last - 1] >> 1, (last - 1) % 2).wait_send()
                    own.wait()
                    for jj in range(1, NDEV // 2):
                        to_chip(jj, 0).wait_recv()
                    for which in range(ns):
                        for cp in exchange(which):
                            cp.wait()

    def blk_of(k, me_ref):
        m, off = me_ref[0], _offset(SCATTER_ORDER, k)
        return (m | off) - (m & off)

    if by_rows:
        in_specs = [pl.BlockSpec((TK, blk), lambda k, kk, me_ref: (kk, blk_of(k, me_ref))),
                    pl.BlockSpec((TK, b.shape[1]), lambda k, kk, me_ref: (kk, 0))]
    else:
        in_specs = [pl.BlockSpec((TK, a.shape[1]), lambda k, kk, me_ref: (kk, 0)),
                    pl.BlockSpec((TK, blk), lambda k, kk, me_ref: (kk, blk_of(k, me_ref)))]
    anyspec = pl.BlockSpec(memory_space=pl.ANY)
    grid_spec = pltpu.PrefetchScalarGridSpec(
        num_scalar_prefetch=1, grid=(NDEV, nk), in_specs=in_specs + [anyspec] * ns,
        out_specs=[anyspec] * (1 + ns),
        scratch_shapes=[pltpu.VMEM(shape, F32), pltpu.VMEM((2,) + shape, BF16), pltpu.VMEM((NDEV // 2,) + shape, BF16),
                        pltpu.SemaphoreType.DMA((2, NDEV // 2)), pltpu.SemaphoreType.DMA((2, NDEV // 2)), pltpu.SemaphoreType.DMA]
        + _side_scratch(ns))
    return _call(
        body, name=name, grid_spec=grid_spec,
        out_shape=[jax.ShapeDtypeStruct((NDEV // 2,) + shape, BF16)] + _side_shapes(sides),
        compiler_params=_params("arbitrary", "arbitrary"),
    )(me1, a, b, *[s for s, _ in sides])


def _chunk(TS, BR, k):
    return pl.BlockSpec((TS, BR), lambda i: (i, k))


def _halo_prev(TS, BR, k):
    return pl.BlockSpec((8, BR), lambda i: (jnp.maximum(i * (TS // 8) - 1, 0), k))


def _conv_a_fwd(proj, w8, BR):
    S = proj.shape[0]
    TS = _tile(S, TILE_STREAM)

    def body(ab_ref, ac_ref, ax_ref, ag_ref, hc_ref, hx_ref, w_ref, o_ref, ext):
        i = pl.program_id(0)
        u = ac_ref[...] * ax_ref[...]
        ext[0:8, :] = jnp.where(i > 0, hc_ref[...] * hx_ref[...], 0.0)
        ext[8:8 + TS, :] = u
        conv = ext[pl.ds(6, TS), :] * w_ref[0:1, :] + ext[pl.ds(7, TS), :] * w_ref[1:2, :] + u * w_ref[2:3, :]
        o_ref[...] = ab_ref[...] * conv * _silu(ag_ref[...])

    return _call(
        body, name="conv_a_fwd", grid=(S // TS,),
        in_specs=[_chunk(TS, BR, 0), _chunk(TS, BR, 1), _chunk(TS, BR, 2), _chunk(TS, BR, 3),
                  _halo_prev(TS, BR, 1), _halo_prev(TS, BR, 2), pl.BlockSpec((8, BR), lambda i: (0, 0))],
        out_specs=pl.BlockSpec((TS, BR), lambda i: (i, 0)),
        out_shape=jax.ShapeDtypeStruct((S, BR), F32),
        scratch_shapes=[pltpu.VMEM((TS + 8, BR), F32)],
        compiler_params=_params("arbitrary"),
    )(proj, proj, proj, proj, proj, proj, w8)


ATT_UNIT = 2048
ATT_LANES = 128


def _attn_mask():
    i = lax.broadcasted_iota(jnp.int32, (BLK, 2 * BLK), 0)
    j = lax.broadcasted_iota(jnp.int32, (BLK, 2 * BLK), 1)
    return jnp.logical_and(j >= i, j <= i + BLK), j >= BLK


def _attn_rows(n, r, dil):
    return pl.ds(n * BLK * dil + r, BLK, stride=dil) if dil > 1 else pl.ds(n * BLK, BLK)


def _attn_geometry(S, BR, dil):
    HD = BR // 8
    U = ATT_UNIT
    assert S % U == 0 and BR % ATT_LANES == 0 and ATT_LANES % HD == 0 and U % (BLK * dil) == 0
    return HD, U, S // U, U // (BLK * dil), ATT_LANES // HD, BR // ATT_LANES


def _attn_fwd(proj, bias, dil, BR):
    S = proj.shape[0]
    HD, U, NU, nbu, hpb, HBK = _attn_geometry(S, BR, dil)
    scale = HD ** -0.5
    combos = [(n, r) for n in range(nbu) for r in range(dil)]

    def body(q_ref, kc_ref, kp_ref, vc_ref, vp_ref, b_ref, o_ref, l_ref, sbuf, pbuf):
        m = pl.program_id(1)
        band, is_cur = _attn_mask()
        first = jnp.logical_and(band, jnp.logical_or(is_cur, m > 0))

        def keys(ref_c, ref_p, n, r):
            prev = ref_c[_attn_rows(n - 1, r, dil), :] if n > 0 else ref_p[_attn_rows(nbu - 1, r, dil), :]
            return jnp.concatenate([prev, ref_c[_attn_rows(n, r, dil), :]], axis=0).astype(BF16)

        for c, (n, r) in enumerate(combos):
            q = (q_ref[_attn_rows(n, r, dil), :] * scale).astype(BF16)
            kk = keys(kc_ref, kp_ref, n, r)
            for h in range(hpb):
                sl = slice(h * HD, (h + 1) * HD)
                sbuf[c * hpb + h] = _dot_nt(q[:, sl], kk[:, sl])
        for c, (n, r) in enumerate(combos):
            lses = []
            for h in range(hpb):
                s = jnp.where(band if n > 0 else first, sbuf[c * hpb + h] + b_ref[h], NEG)
                mx = jnp.max(s, axis=-1, keepdims=True)
                p = jnp.exp(s - mx)
                l = jnp.sum(p, axis=-1, keepdims=True)
                pbuf[c * hpb + h] = (p * (1.0 / l)).astype(BF16)
                lses.append(jnp.broadcast_to(mx + jnp.log(l), (BLK, HD)))
            l_ref[_attn_rows(n, r, dil), :] = jnp.concatenate(lses, axis=1)
        for c, (n, r) in enumerate(combos):
            vv = keys(vc_ref, vp_ref, n, r)
            o_ref[_attn_rows(n, r, dil), :] = jnp.concatenate(
                [_dot(pbuf[c * hpb + h], vv[:, h * HD:(h + 1) * HD]) for h in range(hpb)], axis=1)

    def cur(c):
        return pl.BlockSpec((U, ATT_LANES), lambda hb, m: (m, c * HBK + hb))

    def prev(c):
        return pl.BlockSpec((U, ATT_LANES), lambda hb, m: (jnp.maximum(m - 1, 0), c * HBK + hb))

    ospec = pl.BlockSpec((U, ATT_LANES), lambda hb, m: (m, hb))
    nhb = len(combos) * hpb
    return _call(
        body, name="attn_fwd_d%d" % dil, grid=(HBK, NU),
        in_specs=[cur(4), cur(5), prev(5), cur(6), prev(6), pl.BlockSpec((hpb, BLK, 2 * BLK), lambda hb, m: (hb, 0, 0))],
        out_specs=[ospec, ospec],
        out_shape=[jax.ShapeDtypeStruct((S, BR), F32)] * 2,
        scratch_shapes=[pltpu.VMEM((nhb, BLK, 2 * BLK), F32), pltpu.VMEM((nhb, BLK, 2 * BLK), BF16)],
        compiler_params=_params("arbitrary", "arbitrary"),
    )(proj, proj, proj, proj, proj, bias)


def _mix_weights(l1, l2, l3):
    m = jnp.maximum(jnp.maximum(l1, l2), l3)
    e1, e2, e3 = jnp.exp(l1 - m), jnp.exp(l2 - m), jnp.exp(l3 - m)
    inv = 1.0 / (e1 + e2 + e3)
    return e1 * inv, e2 * inv, e3 * inv


def _attn_mix_fwd(proj, os_, ls_, BR):
    S = proj.shape[0]
    TS = _tile(S, TILE_STREAM)

    def body(o1, o2, o3, l1, l2, l3, g_ref, y_ref):
        w1, w2, w3 = _mix_weights(l1[...], l2[...], l3[...])
        y_ref[...] = (w1 * o1[...] + w2 * o2[...] + w3 * o3[...]) * _silu(g_ref[...])

    spec = pl.BlockSpec((TS, BR), lambda i: (i, 0))
    return _call(
        body, name="attn_mix_fwd", grid=(S // TS,),
        in_specs=[spec] * 6 + [_chunk(TS, BR, 7)], out_specs=spec,
        out_shape=jax.ShapeDtypeStruct((S, BR), F32),
        compiler_params=_params("arbitrary"),
    )(*os_, *ls_, proj)


def _lru_gates(xc, wa, wx, ba, bx, sp):
    xb = xc.astype(BF16)
    r = _sigmoid(_dot(xb, wa) + ba)
    ig = _sigmoid(_dot(xb, wx) + bx)
    la = -LRU_C * r * sp
    a = jnp.exp(la)
    mult = jnp.sqrt(-_expm1(2.0 * la))
    return r, ig, a, mult


def _lru_fwd(proj, rows, wa, wx, BR):
    S = proj.shape[0]
    TS = _tile(S, TILE_RESIDENT)
    PAD = TS // 2

    def body(cx_ref, hx_ref, cg_ref, r_ref, wa_ref, wx_ref, y_ref, h_ref, ext, sa, sb, carry):
        i = pl.program_id(0)

        @pl.when(i == 0)
        def _():
            sa[0:PAD, :] = jnp.ones((PAD, BR), F32)
            sb[0:PAD, :] = jnp.zeros((PAD, BR), F32)
            carry[...] = jnp.zeros_like(carry)

        cx = cx_ref[...]
        ext[0:8, :] = jnp.where(i > 0, hx_ref[...], 0.0)
        ext[8:8 + TS, :] = cx
        xc = (ext[pl.ds(5, TS), :] * r_ref[0:1, :] + ext[pl.ds(6, TS), :] * r_ref[1:2, :]
              + ext[pl.ds(7, TS), :] * r_ref[2:3, :] + cx * r_ref[3:4, :] + r_ref[4:5, :])
        _, ig, a, mult = _lru_gates(xc, wa_ref[...], wx_ref[...], r_ref[5:6, :], r_ref[6:7, :], r_ref[7:8, :])
        sa[PAD:PAD + TS, :] = a
        sb[PAD:PAD + TS, :] = mult * ig * xc
        d = 1
        while d < TS:
            A, B = sa[PAD:PAD + TS, :], sb[PAD:PAD + TS, :]
            As, Bs = sa[pl.ds(PAD - d, TS), :], sb[pl.ds(PAD - d, TS), :]
            sb[PAD:PAD + TS, :] = A * Bs + B
            sa[PAD:PAD + TS, :] = A * As
            d *= 2
        h = sb[PAD:PAD + TS, :] + sa[PAD:PAD + TS, :] * carry[0:1, :]
        carry[0:1, :] = h[TS - 1:TS, :]
        h_ref[...] = h
        y_ref[...] = h * _silu(cg_ref[...])

    full = pl.BlockSpec((BR, BR), lambda i: (0, 0))
    spec = pl.BlockSpec((TS, BR), lambda i: (i, 0))
    return _call(
        body, name="lru_fwd", grid=(S // TS,),
        in_specs=[_chunk(TS, BR, 8), _halo_prev(TS, BR, 8), _chunk(TS, BR, 9), pl.BlockSpec((8, BR), lambda i: (0, 0)), full, full],
        out_specs=[spec, spec],
        out_shape=[jax.ShapeDtypeStruct((S, BR), F32)] * 2,
        scratch_shapes=[pltpu.VMEM((TS + 8, BR), F32), pltpu.VMEM((PAD + TS, BR), F32), pltpu.VMEM((PAD + TS, BR), F32),
                        pltpu.VMEM((8, BR), F32)],
        compiler_params=_params("arbitrary"),
    )(proj, proj, proj, rows, wa, wx)


S5_LAGS = 8
S5_CB = 8 * S5_CH
S5_SB = 8 * S5_STATE


def _s5_fwd(proj, a8, bbl, cre, cim, rows, wg, BR):
    S = proj.shape[0]
    W = a8.shape[1]
    TS = _tile(S, TILE_STREAM)
    CB, SB, J = S5_CB, S5_SB, S5_LAGS
    nblk = BR // CB

    def body(u_ref, uh_ref, dg_ref, a8_ref, bbl_ref, cre_ref, cim_ref, r_ref, wg_ref,
             y_ref, xre_ref, xim_ref, uext, ulag, sre, sim, ypre_s, carry):
        i = pl.program_id(0)

        @pl.when(i == 0)
        def _():
            carry[...] = jnp.zeros_like(carry)

        uext[0:8, :] = jnp.where(i > 0, uh_ref[...], 0.0)
        uext[8:8 + TS, :] = u_ref[...]
        for m in range(nblk):
            cs, ws = slice(m * CB, (m + 1) * CB), slice(m * SB, (m + 1) * SB)
            for j in range(J):
                ulag[:, j * CB:(j + 1) * CB] = uext[pl.ds(8 - j, TS), cs].astype(BF16)
            w = _dot(ulag[...], bbl_ref[m])
            sre[...] = w[:, 0:SB]
            sim[...] = w[:, SB:2 * SB]
            ar = jnp.broadcast_to(a8_ref[0:1, ws], (8, SB))
            ai = jnp.broadcast_to(a8_ref[1:2, ws], (8, SB))
            xr, xi = carry[0:8, ws], carry[8:16, ws]
            for g in range(TS // 8):
                rg = slice(8 * g, 8 * g + 8)
                xr, xi = sre[rg, :] + ar * xr - ai * xi, sim[rg, :] + ar * xi + ai * xr
                sre[rg, :] = xr
                sim[rg, :] = xi
            carry[0:8, ws] = xr
            carry[8:16, ws] = xi
            xre, xim = sre[...], sim[...]
            xre_ref[:, ws] = xre
            xim_ref[:, ws] = xim
            ypre_s[:, cs] = _dot(xre.astype(BF16), cre_ref[ws, :]) - _dot(xim.astype(BF16), cim_ref[ws, :])
        dg = dg_ref[...]
        yg = _gelu(ypre_s[...] + r_ref[0:1, :] * u_ref[...])
        s = _sigmoid(_dot(yg.astype(BF16), wg_ref[...]) + r_ref[1:2, :])
        y_ref[...] = yg * s * _silu(dg)

    def const(shape):
        return pl.BlockSpec(shape, lambda i: (0,) * len(shape))

    return _call(
        body, name="s5_fwd", grid=(S // TS,),
        in_specs=[_chunk(TS, BR, 10), _halo_prev(TS, BR, 10), _chunk(TS, BR, 11), const((8, W)), const((nblk, J * CB, 2 * SB)),
                  const((W, CB)), const((W, CB)), const((8, BR)), const((BR, BR))],
        out_specs=[pl.BlockSpec((TS, BR), lambda i: (i, 0)), pl.BlockSpec((TS, W), lambda i: (i, 0)), pl.BlockSpec((TS, W), lambda i: (i, 0))],
        out_shape=[jax.ShapeDtypeStruct((S, BR), F32), jax.ShapeDtypeStruct((S, W), F32), jax.ShapeDtypeStruct((S, W), F32)],
        scratch_shapes=[pltpu.VMEM((TS + 8, BR), F32), pltpu.VMEM((TS, J * CB), BF16), pltpu.VMEM((TS, SB), F32), pltpu.VMEM((TS, SB), F32),
                        pltpu.VMEM((TS, BR), F32), pltpu.VMEM((16, W), F32)],
        compiler_params=_params("arbitrary"),
    )(proj, proj, proj, a8, bbl, cre, cim, rows, wg)


def _out_ln(ys, w_out, x, rows, target=None):
    S, D = x.shape
    BR = D // 4
    TS = _tile(S, TILE_RESIDENT)
    last = target is not None

    def body(ya, yb, yc, yd, w_ref, x_ref, r_ref, *rest):
        t_ref = rest[0] if last else None
        o_ref, y_ref, cat_ref = rest[1:4] if last else rest[0:3]
        cat = jnp.concatenate([ya[...], yb[...], yc[...], yd[...]], axis=1).astype(BF16)
        y = _dot(cat, w_ref[...])
        z = ALPHA * x_ref[...] + r_ref[0:1, :] * y
        mu = jnp.mean(z, axis=-1, keepdims=True)
        zc = z - mu
        var = jnp.mean(zc * zc, axis=-1, keepdims=True)
        xn = zc * lax.rsqrt(var + LN_EPS) * r_ref[1:2, :] + r_ref[2:3, :]
        y_ref[...] = y.astype(BF16)
        cat_ref[...] = cat
        if last:
            l_ref = rest[4]

            @pl.when(pl.program_id(0) == 0)
            def _():
                l_ref[...] = jnp.zeros_like(l_ref)
            diff = xn - t_ref[...]
            o_ref[...] = diff * (1.0 / D)
            l_ref[...] += jnp.sum(_colsum(diff * diff), axis=1, keepdims=True) * (0.5 / D)
        else:
            o_ref[...] = xn

    yspec = pl.BlockSpec((TS, BR), lambda i: (i, 0))
    spec = pl.BlockSpec((TS, D), lambda i: (i, 0))
    lspec = pl.BlockSpec((1, 1), lambda i: (0, 0))
    return _call(
        body, name="out_ln_loss" if last else "out_ln", grid=(S // TS,),
        in_specs=[yspec] * 4 + [pl.BlockSpec((D, D), lambda i: (0, 0)), spec, pl.BlockSpec((8, D), lambda i: (0, 0))] + [spec] * last,
        out_specs=[spec, spec, spec] + [lspec] * last,
        out_shape=[jax.ShapeDtypeStruct((S, D), F32), jax.ShapeDtypeStruct((S, D), BF16), jax.ShapeDtypeStruct((S, D), BF16)]
        + [jax.ShapeDtypeStruct((1, 1), F32)] * last,
        compiler_params=_params("arbitrary"),
    )(*ys, w_out, x, rows, *([target] if last else []))


def _ln_bwd(dout, x, y, rows, w_out):
    S, D = x.shape
    TS = _tile(S, TILE_RESIDENT)

    def body(do_ref, x_ref, y_ref, r_ref, w_ref, dxr_ref, dyb_ref, dcat_ref, acc_ref):
        @pl.when(pl.program_id(0) == 0)
        def _():
            acc_ref[...] = jnp.zeros_like(acc_ref)
        g1, lg = r_ref[0:1, :], r_ref[1:2, :]
        yv = y_ref[...].astype(F32)
        z = ALPHA * x_ref[...] + g1 * yv
        mu = jnp.mean(z, axis=-1, keepdims=True)
        zc = z - mu
        var = jnp.mean(zc * zc, axis=-1, keepdims=True)
        rstd = lax.rsqrt(var + LN_EPS)
        xhat = zc * rstd
        do = do_ref[...]
        dxh = do * lg
        dz = rstd * (dxh - jnp.mean(dxh, axis=-1, keepdims=True) - xhat * jnp.mean(dxh * xhat, axis=-1, keepdims=True))
        dxr_ref[...] = (ALPHA * dz).astype(BF16)
        dyb = (g1 * dz).astype(BF16)
        dyb_ref[...] = dyb
        dcat_ref[...] = _dot_nt(dyb, w_ref[...])
        acc_ref[0:1, :] += _colsum(do * xhat)
        acc_ref[1:2, :] += _colsum(do)
        acc_ref[2:3, :] += _colsum(dz * yv)

    spec = pl.BlockSpec((TS, D), lambda i: (i, 0))
    rspec = pl.BlockSpec((8, D), lambda i: (0, 0))
    return _call(
        body, name="ln_bwd", grid=(S // TS,),
        in_specs=[spec, spec, spec, rspec, pl.BlockSpec((D, D), lambda i: (0, 0))],
        out_specs=[spec, spec, spec, rspec],
        out_shape=[jax.ShapeDtypeStruct((S, D), BF16), jax.ShapeDtypeStruct((S, D), BF16), jax.ShapeDtypeStruct((S, D), F32),
                   jax.ShapeDtypeStruct((8, D), F32)],
        compiler_params=_params("arbitrary"),
    )(dout, x, y, rows, w_out)


def _dh(dproj, wg, dxres, x, mod, sides):
    S, D = x.shape
    NS = wg.shape[2]
    TS = _tile(S, TILE_RESIDENT)
    nt = S // TS
    ns = len(sides)

    def body(dp_ref, w_hbm, dxr_ref, x_ref, mod_ref, *rest):
        s_refs, dx_ref, sum_ref, so_refs = rest[:ns], rest[ns], rest[ns + 1], rest[ns + 2:2 * ns + 2]
        w_vmem, sem, send_sems, recv_sems, local_sems = rest[2 * ns + 2:]
        i = pl.program_id(0)

        def exchange(which):
            return _side_copies(sides[which][1], s_refs[which], so_refs[which], send_sems, recv_sems, local_sems, which)

        @pl.when(i == 0)
        def _():
            load = pltpu.make_async_copy(w_hbm, w_vmem, sem)
            load.start()
            for which in range(ns):
                for cp in exchange(which):
                    cp.start()
            sum_ref[...] = jnp.zeros_like(sum_ref)
            load.wait()

        dh = _dot_nt(dp_ref[:, 0:NS], w_vmem[0])
        for j in range(1, NDEV):
            dh = dh + _dot_nt(dp_ref[:, j * NS:(j + 1) * NS], w_vmem[j])
        dx_ref[...] = dxr_ref[...].astype(F32) + dh * mod_ref[0:1, :]
        sum_ref[0:1, :] += _colsum(dh * x_ref[...])
        sum_ref[1:2, :] += _colsum(dh)

        @pl.when(i == nt - 1)
        def _():
            for which in range(ns):
                for cp in exchange(which):
                    cp.wait()

    spec = pl.BlockSpec((TS, D), lambda i: (i, 0))
    rspec = pl.BlockSpec((8, D), lambda i: (0, 0))
    anyspec = pl.BlockSpec(memory_space=pl.ANY)
    side_shapes = [jax.ShapeDtypeStruct(((NDEV,) + a.shape) if g else a.shape, a.dtype) for a, g in sides]
    return _call(
        body, name="dh", grid=(nt,),
        in_specs=[pl.BlockSpec((TS, NDEV * NS), lambda i: (i, 0)), anyspec, spec, spec, rspec] + [anyspec] * ns,
        out_specs=[spec, rspec] + [anyspec] * ns,
        out_shape=[jax.ShapeDtypeStruct((S, D), F32), jax.ShapeDtypeStruct((8, D), F32)] + side_shapes,
        scratch_shapes=[pltpu.VMEM((NDEV, D, NS), BF16), pltpu.SemaphoreType.DMA, pltpu.SemaphoreType.DMA((max(ns, 1), NDEV - 1)),
                        pltpu.SemaphoreType.DMA((max(ns, 1), NDEV - 1)), pltpu.SemaphoreType.DMA((max(ns, 1),))],
        compiler_params=_params("arbitrary"),
    )(dproj, wg, dxres, x, mod, *[a for a, _ in sides])


def _halo_next(S, TS, BR, k):
    last = S // 8 - 1
    return pl.BlockSpec((8, BR), lambda i: (jnp.minimum((i + 1) * (TS // 8), last), k))


def _conv_a_bwd(proj, dcat, w8, BR):
    S = proj.shape[0]
    TS = _tile(S, TILE_STREAM)
    nt = S // TS

    def body(ab_ref, ac_ref, ax_ref, ag_ref, dy_ref, hc_ref, hx_ref, nab_ref, nag_ref, ndy_ref, w_ref, dp_ref, dw_ref, ext, dext):
        i = pl.program_id(0)

        @pl.when(i == 0)
        def _():
            dw_ref[...] = jnp.zeros_like(dw_ref)

        w0, w1, w2 = w_ref[0:1, :], w_ref[1:2, :], w_ref[2:3, :]
        ab, ac, ax, ag, dy = ab_ref[...], ac_ref[...], ax_ref[...], ag_ref[...], dy_ref[...]
        u = ac * ax
        ext[0:8, :] = jnp.where(i > 0, hc_ref[...] * hx_ref[...], 0.0)
        ext[8:8 + TS, :] = u
        u1, u2 = ext[pl.ds(7, TS), :], ext[pl.ds(6, TS), :]
        conv = u2 * w0 + u1 * w1 + u * w2
        sg = _silu(ag)
        dconv = dy * ab * sg
        dext[0:TS, :] = dconv
        dext[TS:TS + 8, :] = jnp.where(i < nt - 1, ndy_ref[...] * nab_ref[...] * _silu(nag_ref[...]), 0.0)
        du = w2 * dconv + w1 * dext[pl.ds(1, TS), :] + w0 * dext[pl.ds(2, TS), :]
        dp_ref[:, 0:BR] = (dy * conv * sg).astype(BF16)
        dp_ref[:, BR:2 * BR] = (du * ax).astype(BF16)
        dp_ref[:, 2 * BR:3 * BR] = (du * ac).astype(BF16)
        dp_ref[:, 3 * BR:4 * BR] = (dy * ab * conv * _dsilu(ag)).astype(BF16)
        dw_ref[0:1, :] += _colsum(dconv * u2)
        dw_ref[1:2, :] += _colsum(dconv * u1)
        dw_ref[2:3, :] += _colsum(dconv * u)

    rspec = pl.BlockSpec((8, BR), lambda i: (0, 0))
    return _call(
        body, name="conv_a_bwd", grid=(nt,),
        in_specs=[_chunk(TS, BR, 0), _chunk(TS, BR, 1), _chunk(TS, BR, 2), _chunk(TS, BR, 3), _chunk(TS, BR, 0),
                  _halo_prev(TS, BR, 1), _halo_prev(TS, BR, 2), _halo_next(S, TS, BR, 0), _halo_next(S, TS, BR, 3),
                  _halo_next(S, TS, BR, 0), rspec],
        out_specs=[pl.BlockSpec((TS, 4 * BR), lambda i: (i, 0)), rspec],
        out_shape=[jax.ShapeDtypeStruct((S, 12 * BR), BF16), jax.ShapeDtypeStruct((8, BR), F32)],
        scratch_shapes=[pltpu.VMEM((TS + 8, BR), F32), pltpu.VMEM((TS + 8, BR), F32)],
        compiler_params=_params("arbitrary"),
    )(proj, proj, proj, proj, dcat, proj, proj, proj, proj, dcat, w8)


def _attn_mix_bwd(proj, dcat, os_, ls_, head_ones, BR):
    S = proj.shape[0]
    TS = _tile(S, TILE_STREAM)

    def body(dy_ref, g_ref, o1, o2, o3, l1, l2, l3, ones_ref, d1, d2, d3, e1, e2, e3, dg_ref):
        w1, w2, w3 = _mix_weights(l1[...], l2[...], l3[...])
        mix = w1 * o1[...] + w2 * o2[...] + w3 * o3[...]
        g = g_ref[...]
        dy = dy_ref[...]
        dmix = dy * _silu(g)
        dg_ref[...] = dy * mix * _dsilu(g)
        t = jnp.dot(dmix * mix, ones_ref[...], preferred_element_type=F32, precision=lax.Precision.HIGHEST)
        d1[...] = w1 * dmix
        d2[...] = w2 * dmix
        d3[...] = w3 * dmix
        e1[...] = -w1 * t
        e2[...] = -w2 * t
        e3[...] = -w3 * t

    spec = pl.BlockSpec((TS, BR), lambda i: (i, 0))
    outs = _call(
        body, name="attn_mix_bwd", grid=(S // TS,),
        in_specs=[_chunk(TS, BR, 1), _chunk(TS, BR, 7)] + [spec] * 6 + [pl.BlockSpec((BR, BR), lambda i: (0, 0))],
        out_specs=[spec] * 7,
        out_shape=[jax.ShapeDtypeStruct((S, BR), F32)] * 7,
        compiler_params=_params("arbitrary"),
    )(dcat, proj, *os_, *ls_, head_ones)
    return outs[0:3], outs[3:6], outs[6]


def _attn_bwd(proj, do, e, lse, bias, dil, BR):
    S = proj.shape[0]
    HD, U, NU, nbu, hpb, HBK = _attn_geometry(S, BR, dil)
    scale = HD ** -0.5
    combos = [(n, r) for n in range(nbu) for r in range(dil)]

    def body(q_ref, kc_ref, kp_ref, vc_ref, vp_ref, do_ref, e_ref, l_ref, b_ref, dq_ref, dk_ref, dv_ref, db_ref,
             sbuf, dpbuf, pbuf, dsbuf, qs, dos, kks, dkacc, dvacc, nxk, nxv, cark, carv):
        j = pl.program_id(1)
        mu = NU - 1 - j
        band, is_cur = _attn_mask()
        first = jnp.logical_and(band, jnp.logical_or(is_cur, mu > 0))

        @pl.when(j == 0)
        def _():
            cark[...] = jnp.zeros_like(cark)
            carv[...] = jnp.zeros_like(carv)
            db_ref[...] = jnp.zeros_like(db_ref)

        if nbu > 1:
            nxk[...] = jnp.zeros_like(nxk)
            nxv[...] = jnp.zeros_like(nxv)

        def keys(ref_c, ref_p, n, r):
            prev = ref_c[_attn_rows(n - 1, r, dil), :] if n > 0 else ref_p[_attn_rows(nbu - 1, r, dil), :]
            return jnp.concatenate([prev, ref_c[_attn_rows(n, r, dil), :]], axis=0).astype(BF16)

        for c, (n, r) in enumerate(combos):
            rows = _attn_rows(n, r, dil)
            q = (q_ref[rows, :] * scale).astype(BF16)
            dob = do_ref[rows, :].astype(BF16)
            kk, vv = keys(kc_ref, kp_ref, n, r), keys(vc_ref, vp_ref, n, r)
            qs[c], dos[c], kks[c] = q, dob, kk
            for h in range(hpb):
                sl = slice(h * HD, (h + 1) * HD)
                sbuf[c * hpb + h] = _dot_nt(q[:, sl], kk[:, sl])
                dpbuf[c * hpb + h] = _dot_nt(dob[:, sl], vv[:, sl])
        dbs = [None] * hpb
        for c, (n, r) in enumerate(combos):
            rows = _attn_rows(n, r, dil)
            lse, ev = l_ref[rows, :], e_ref[rows, :]
            for h in range(hpb):
                one = slice(h * HD, h * HD + 1)
                s = jnp.where(band if n > 0 else first, sbuf[c * hpb + h] + b_ref[h], NEG)
                p = jnp.exp(s - lse[:, one])
                ds = p * (dpbuf[c * hpb + h] + ev[:, one])
                pbuf[c * hpb + h] = p.astype(BF16)
                dsbuf[c * hpb + h] = ds.astype(BF16)
                dbs[h] = ds if dbs[h] is None else dbs[h] + ds
        for h in range(hpb):
            db_ref[h] += dbs[h]
        for c, (n, r) in enumerate(combos):
            rows = _attn_rows(n, r, dil)
            q, dob, kk = qs[c], dos[c], kks[c]
            dq, dkk, dvv = [], [], []
            for h in range(hpb):
                sl = slice(h * HD, (h + 1) * HD)
                dsb = dsbuf[c * hpb + h]
                dq.append(_dot(dsb, kk[:, sl]) * scale)
                dkk.append(_dot_tn(dsb, q[:, sl]))
                dvv.append(_dot_tn(pbuf[c * hpb + h], dob[:, sl]))
            dq_ref[rows, :] = jnp.concatenate(dq, axis=1)
            dkk, dvv = jnp.concatenate(dkk, axis=1), jnp.concatenate(dvv, axis=1)
            dkacc[rows, :] = dkk[BLK:2 * BLK, :]
            dvacc[rows, :] = dvv[BLK:2 * BLK, :]
            if n > 0:
                prow = _attn_rows(n - 1, r, dil)
                dkacc[prow, :] += dkk[0:BLK, :]
                dvacc[prow, :] += dvv[0:BLK, :]
            else:
                prow = _attn_rows(nbu - 1, r, dil)
                nxk[prow, :] = dkk[0:BLK, :]
                nxv[prow, :] = dvv[0:BLK, :]
        dk_ref[...] = dkacc[...] + cark[...]
        dv_ref[...] = dvacc[...] + carv[...]
        cark[...] = nxk[...]
        carv[...] = nxv[...]

    def cur(c):
        return pl.BlockSpec((U, ATT_LANES), lambda hb, j: (NU - 1 - j, c * HBK + hb))

    def prev(c):
        return pl.BlockSpec((U, ATT_LANES), lambda hb, j: (jnp.maximum(NU - 2 - j, 0), c * HBK + hb))

    own = pl.BlockSpec((U, ATT_LANES), lambda hb, j: (NU - 1 - j, hb))
    bspec = pl.BlockSpec((hpb, BLK, 2 * BLK), lambda hb, j: (hb, 0, 0))
    nhb = len(combos) * hpb
    unit = pltpu.VMEM((U, ATT_LANES), F32)
    return _call(
        body, name="attn_bwd_d%d" % dil, grid=(HBK, NU),
        in_specs=[cur(4), cur(5), prev(5), cur(6), prev(6), own, own, own, bspec],
        out_specs=[own, own, own, bspec],
        out_shape=[jax.ShapeDtypeStruct((S, BR), F32)] * 3 + [jax.ShapeDtypeStruct((8, BLK, 2 * BLK), F32)],
        scratch_shapes=[pltpu.VMEM((nhb, BLK, 2 * BLK), F32)] * 2 + [pltpu.VMEM((nhb, BLK, 2 * BLK), BF16)] * 2
        + [pltpu.VMEM((len(combos), BLK, ATT_LANES), BF16)] * 2 + [pltpu.VMEM((len(combos), 2 * BLK, ATT_LANES), BF16)] + [unit] * 6,
        compiler_params=_params("arbitrary", "arbitrary"),
    )(proj, proj, proj, proj, proj, do, e, lse, bias)


def _attn_dsum(ds, dbg, dproj, BR):
    S = dbg.shape[0]
    TS = _tile(S, TILE_STREAM)

    def body(*refs):
        o_ref = refs[11]
        for k in range(3):
            o_ref[:, k * BR:(k + 1) * BR] = (refs[k][...] + refs[3 + k][...] + refs[6 + k][...]).astype(BF16)
        o_ref[:, 3 * BR:4 * BR] = refs[9][...].astype(BF16)

    spec = pl.BlockSpec((TS, BR), lambda i: (i, 0))
    return _call(
        body, name="attn_dsum", grid=(S // TS,),
        in_specs=[spec] * 10 + [pl.BlockSpec(memory_space=pl.ANY)],
        out_specs=pl.BlockSpec((TS, 4 * BR), lambda i: (i, 1)),
        out_shape=jax.ShapeDtypeStruct(dproj.shape, BF16),
        input_output_aliases={10: 0},
        compiler_params=_params("arbitrary"),
    )(*[t for trip in ds for t in trip], dbg, dproj)


def _lru_bwd(proj, h, dcat, rows, wa, wx, dproj, BR):
    S = proj.shape[0]
    TS = _tile(S, TILE_RESIDENT)
    PAD = TS // 2
    nt = S // TS

    def body(cx_ref, hx_ref, cg_ref, h_ref, hh_ref, dy_ref, r_ref, wa_ref, wx_ref, _,
             dp_ref, sum_ref, dwa_ref, dwx_ref, ext, hext, aext, dext, sa, sb, carry):
        i = pl.program_id(0)
        ti = nt - 1 - i

        @pl.when(i == 0)
        def _():
            sa[TS:TS + PAD, :] = jnp.ones((PAD, BR), F32)
            sb[TS:TS + PAD, :] = jnp.zeros((PAD, BR), F32)
            carry[...] = jnp.zeros_like(carry)
            dext[TS:TS + 8, :] = jnp.zeros((8, BR), F32)
            sum_ref[...] = jnp.zeros_like(sum_ref)
            dwa_ref[...] = jnp.zeros_like(dwa_ref)
            dwx_ref[...] = jnp.zeros_like(dwx_ref)

        w0, w1, w2, w3 = r_ref[0:1, :], r_ref[1:2, :], r_ref[2:3, :], r_ref[3:4, :]
        sp = r_ref[7:8, :]
        cx = cx_ref[...]
        ext[0:8, :] = jnp.where(ti > 0, hx_ref[...], 0.0)
        ext[8:8 + TS, :] = cx
        x3, x2, x1 = ext[pl.ds(5, TS), :], ext[pl.ds(6, TS), :], ext[pl.ds(7, TS), :]
        xc = x3 * w0 + x2 * w1 + x1 * w2 + cx * w3 + r_ref[4:5, :]
        wa_, wx_ = wa_ref[...], wx_ref[...]
        r, ig, a, mult = _lru_gates(xc, wa_, wx_, r_ref[5:6, :], r_ref[6:7, :], sp)
        cg, dy, hv = cg_ref[...], dy_ref[...], h_ref[...]
        dp_ref[:, BR:2 * BR] = (dy * hv * _dsilu(cg)).astype(BF16)
        aext[0:TS, :] = a
        aext[TS:TS + 8, :] = jnp.broadcast_to(carry[0:1, :], (8, BR))
        sa[0:TS, :] = aext[pl.ds(1, TS), :]
        sb[0:TS, :] = dy * _silu(cg)
        d = 1
        while d < TS:
            A, B = sa[0:TS, :], sb[0:TS, :]
            As, Bs = sa[pl.ds(d, TS), :], sb[pl.ds(d, TS), :]
            sb[0:TS, :] = B + A * Bs
            sa[0:TS, :] = A * As
            d *= 2
        gh = sb[0:TS, :] + sa[0:TS, :] * carry[1:2, :]
        carry[0:1, :] = a[0:1, :]
        carry[1:2, :] = gh[0:1, :]
        hext[0:8, :] = jnp.where(ti > 0, hh_ref[...], 0.0)
        hext[8:8 + TS, :] = hv
        da = gh * hext[pl.ds(7, TS), :]
        dmult = gh * ig * xc
        dig = gh * mult * xc
        dxc = gh * mult * ig
        dla = da * a - dmult * a * a / mult
        dpr = dla * (-LRU_C) * sp * r * (1.0 - r)
        dpi = dig * ig * (1.0 - ig)
        xb, dprb, dpib = xc.astype(BF16), dpr.astype(BF16), dpi.astype(BF16)
        dwa_ref[...] += _dot_tn(xb, dprb)
        dwx_ref[...] += _dot_tn(xb, dpib)
        dxc = dxc + _dot_nt(dprb, wa_) + _dot_nt(dpib, wx_)
        sum_ref[0:1, :] += _colsum(dxc * x3)
        sum_ref[1:2, :] += _colsum(dxc * x2)
        sum_ref[2:3, :] += _colsum(dxc * x1)
        sum_ref[3:4, :] += _colsum(dxc * cx)
        sum_ref[4:5, :] += _colsum(dxc)
        sum_ref[5:6, :] += _colsum(dpr)
        sum_ref[6:7, :] += _colsum(dpi)
        sum_ref[7:8, :] += _colsum(dla * (-LRU_C) * r)
        dext[0:TS, :] = dxc
        dcx = w3 * dxc + w2 * dext[pl.ds(1, TS), :] + w1 * dext[pl.ds(2, TS), :] + w0 * dext[pl.ds(3, TS), :]
        dext[TS:TS + 8, :] = dxc[0:8, :]
        dp_ref[:, 0:BR] = dcx.astype(BF16)

    def rev(k):
        return pl.BlockSpec((TS, BR), lambda i: (nt - 1 - i, k))

    def rev_halo(k):
        return pl.BlockSpec((8, BR), lambda i: (jnp.maximum((nt - 1 - i) * (TS // 8) - 1, 0), k))

    full = pl.BlockSpec((BR, BR), lambda i: (0, 0))
    rspec = pl.BlockSpec((8, BR), lambda i: (0, 0))
    return _call(
        body, name="lru_bwd", grid=(nt,),
        in_specs=[rev(8), rev_halo(8), rev(9), rev(0), rev_halo(0), rev(2), rspec, full, full, pl.BlockSpec(memory_space=pl.ANY)],
        out_specs=[pl.BlockSpec((TS, 2 * BR), lambda i: (nt - 1 - i, 4)), rspec, full, full],
        input_output_aliases={9: 0},
        out_shape=[jax.ShapeDtypeStruct(dproj.shape, BF16), jax.ShapeDtypeStruct((8, BR), F32),
                   jax.ShapeDtypeStruct((BR, BR), F32), jax.ShapeDtypeStruct((BR, BR), F32)],
        scratch_shapes=[pltpu.VMEM((TS + 8, BR), F32)] * 4 + [pltpu.VMEM((TS + PAD, BR), F32)] * 2 + [pltpu.VMEM((8, BR), F32)],
        compiler_params=_params("arbitrary"),
    )(proj, proj, proj, h, h, dcat, rows, wa, wx, dproj)


def _s5_bwd(proj, dcat, xre_all, xim_all, a8, cl, bbre, bbim, cre, cim, rows, wg, dproj, BR):
    S = proj.shape[0]
    W = a8.shape[1]
    TS = _tile(S, TILE_STREAM)
    nt = S // TS
    CB, SB, J = S5_CB, S5_SB, S5_LAGS
    nblk = BR // CB

    def body(u_ref, dg_ref, dy_ref, xre_ref, xim_ref, hre_ref, him_ref, a8_ref, cl_ref, bbre_ref, bbim_ref, cre_ref, cim_ref,
             r_ref, wg_ref, _, dp_ref, sum_ref, dwg_ref, da_ref, dbbre_ref, dbbim_ref, dcre_ref, dcim_ref,
             sre, sim, ere, eim, ypre_s, dext, dylag, carry):
        i = pl.program_id(0)
        ti = nt - 1 - i

        @pl.when(i == 0)
        def _():
            dext[TS:TS + 8, :] = jnp.zeros((8, BR), F32)
            carry[...] = jnp.zeros_like(carry)
            for ref in (sum_ref, dwg_ref, da_ref, dbbre_ref, dbbim_ref, dcre_ref, dcim_ref):
                ref[...] = jnp.zeros_like(ref)

        u, dg, dy = u_ref[...], dg_ref[...], dy_ref[...]
        for m in range(nblk):
            cs, ws = slice(m * CB, (m + 1) * CB), slice(m * SB, (m + 1) * SB)
            ypre_s[:, cs] = (_dot(xre_ref[:, ws].astype(BF16), cre_ref[ws, :]) - _dot(xim_ref[:, ws].astype(BF16), cim_ref[ws, :]))
        ypre = ypre_s[...] + r_ref[0:1, :] * u
        yg = _gelu(ypre)
        wg_ = wg_ref[...]
        s = _sigmoid(_dot(yg.astype(BF16), wg_) + r_ref[1:2, :])
        dgl = dy * _silu(dg)
        dp_ref[:, BR:2 * BR] = (dy * yg * s * _dsilu(dg)).astype(BF16)
        dps = dgl * yg * s * (1.0 - s)
        dpsb = dps.astype(BF16)
        sum_ref[1:2, :] += _colsum(dps)
        dwg_ref[...] += _dot_tn(yg.astype(BF16), dpsb)
        dyp = (dgl * s + _dot_nt(dpsb, wg_)) * _dgelu(ypre)
        sum_ref[0:1, :] += _colsum(dyp * u)
        dext[0:TS, :] = dyp
        for m in range(nblk):
            cs, ws = slice(m * CB, (m + 1) * CB), slice(m * SB, (m + 1) * SB)
            dypb = dext[0:TS, cs].astype(BF16)
            xre, xim = xre_ref[:, ws], xim_ref[:, ws]
            dcre_ref[ws, :] += _dot_tn(xre.astype(BF16), dypb)
            dcim_ref[ws, :] -= _dot_tn(xim.astype(BF16), dypb)
            for j in range(J):
                dylag[:, j * CB:(j + 1) * CB] = dext[pl.ds(j, TS), cs].astype(BF16)
            w = _dot(dylag[...], cl_ref[m])
            sre[...] = w[:, 0:SB]
            sim[...] = w[:, SB:2 * SB]
            ar = jnp.broadcast_to(a8_ref[0:1, ws], (8, SB))
            ai = jnp.broadcast_to(a8_ref[1:2, ws], (8, SB))
            gr, gi = carry[0:8, ws], carry[8:16, ws]
            for g in reversed(range(TS // 8)):
                rg = slice(8 * g, 8 * g + 8)
                gr, gi = sre[rg, :] + ar * gr + ai * gi, sim[rg, :] + ar * gi - ai * gr
                sre[rg, :] = gr
                sim[rg, :] = gi
            carry[0:8, ws] = gr
            carry[8:16, ws] = gi
            gre, gim = sre[...], sim[...]
            ere[0:8, :] = jnp.where(ti > 0, hre_ref[:, ws], 0.0)
            eim[0:8, :] = jnp.where(ti > 0, him_ref[:, ws], 0.0)
            ere[8:8 + TS, :] = xre
            eim[8:8 + TS, :] = xim
            xpr, xpi = ere[pl.ds(7, TS), :], eim[pl.ds(7, TS), :]
            da_ref[0:1, ws] += _colsum(gre * xpr + gim * xpi)
            da_ref[1:2, ws] += _colsum(gim * xpr - gre * xpi)
            greb, gimb = gre.astype(BF16), gim.astype(BF16)
            ub = u_ref[:, cs].astype(BF16)
            dbbre_ref[cs, :] += _dot_tn(ub, greb)
            dbbim_ref[cs, :] += _dot_tn(ub, gimb)
            du = _dot_nt(greb, bbre_ref[cs, :]) + _dot_nt(gimb, bbim_ref[cs, :]) + r_ref[0:1, cs] * dext[0:TS, cs]
            dp_ref[:, cs] = du.astype(BF16)
        dext[TS:TS + 8, :] = dext[0:8, :]

    def rev(width, k):
        return pl.BlockSpec((TS, width), lambda i: (nt - 1 - i, k))

    def rev_halo(width):
        return pl.BlockSpec((8, width), lambda i: (jnp.maximum((nt - 1 - i) * (TS // 8) - 1, 0), 0))

    def const(shape):
        return pl.BlockSpec(shape, lambda i: (0,) * len(shape))

    return _call(
        body, name="s5_bwd", grid=(nt,),
        in_specs=[rev(BR, 10), rev(BR, 11), rev(BR, 3), rev(W, 0), rev(W, 0), rev_halo(W), rev_halo(W),
                  const((8, W)), const((nblk, J * CB, 2 * SB)), const((BR, SB)), const((BR, SB)), const((W, CB)), const((W, CB)),
                  const((8, BR)), const((BR, BR)), pl.BlockSpec(memory_space=pl.ANY)],
        input_output_aliases={15: 0},
        out_specs=[pl.BlockSpec((TS, 2 * BR), lambda i: (nt - 1 - i, 5)), const((8, BR)), const((BR, BR)), const((8, W)),
                   const((BR, SB)), const((BR, SB)), const((W, CB)), const((W, CB))],
        out_shape=[jax.ShapeDtypeStruct(dproj.shape, BF16), jax.ShapeDtypeStruct((8, BR), F32), jax.ShapeDtypeStruct((BR, BR), F32),
                   jax.ShapeDtypeStruct((8, W), F32), jax.ShapeDtypeStruct((BR, SB), F32), jax.ShapeDtypeStruct((BR, SB), F32),
                   jax.ShapeDtypeStruct((W, CB), F32), jax.ShapeDtypeStruct((W, CB), F32)],
        scratch_shapes=[pltpu.VMEM((TS, SB), F32)] * 2 + [pltpu.VMEM((TS + 8, SB), F32)] * 2
        + [pltpu.VMEM((TS, BR), F32), pltpu.VMEM((TS + 8, BR), F32), pltpu.VMEM((TS, J * CB), BF16), pltpu.VMEM((16, W), F32)],
        compiler_params=_params("arbitrary"),
    )(proj, proj, dcat, xre_all, xim_all, xre_all, xim_all, a8, cl, bbre, bbim, cre, cim, rows, wg, dproj)


_WEIGHTS = ['rel_bias', 'w_ada', 'b_ada', 'w_in', 'conv_a', 'conv_c', 'conv_c_b', 'lru_wa', 'lru_ba', 'lru_wx', 'lru_bx',
            'lru_lambda', 's5_lam_re', 's5_lam_im', 's5_log_dt', 's5_b_re', 's5_b_im', 's5_c_re', 's5_c_im', 's5_d',
            's5_w_glu', 's5_b_glu', 'w_out', 'ln_g', 'ln_b']
_LAYER_SMALL = ['conv_a', 'conv_c', 'conv_c_b', 'lru_wa', 'lru_ba', 'lru_wx', 'lru_bx', 'lru_lambda', 's5_lam_re', 's5_lam_im',
                's5_log_dt', 's5_b_re', 's5_b_im', 's5_c_re', 's5_c_im', 's5_d', 's5_b_glu']
_SMALL = ['rel_bias'] + _LAYER_SMALL + ['ln_g', 'ln_b']


def _t5_bucket(dist):
    max_exact = REL_BUCKETS // 2
    nf = jnp.maximum(dist, 1).astype(F32)
    large = max_exact + (jnp.log(nf / max_exact) / math.log(REL_MAX_DIST / max_exact)
                         * (REL_BUCKETS - max_exact)).astype(jnp.int32)
    large = jnp.minimum(large, REL_BUCKETS - 1)
    return jnp.where(dist < max_exact, dist, large)


def _bias_tables(rel_bias):
    i = jnp.arange(BLK)[:, None]
    j = jnp.arange(2 * BLK)[None, :]
    delta = i + BLK - j
    out = []
    for window, dil in DILATIONS:
        bucket = _t5_bucket(jnp.clip(delta, 0, window // dil) * dil)
        onehot = (bucket[:, :, None] == jnp.arange(REL_BUCKETS)[None, None, :]).astype(F32)
        out.append(jnp.einsum('ijb,bh->hij', onehot, rel_bias, precision=lax.Precision.HIGHEST))
    return jnp.stack(out)


def _prep_layer(q):
    eye8 = jnp.eye(8, dtype=F32)
    G = q['s5_lam_re'].shape[0]
    nblk = G // 8

    def block_diag(w):
        hd = w.shape[1]
        return (w[:, :, None, :] * eye8[:, None, :, None]).reshape(8 * hd, 8 * hd)

    def compact_b(bb):
        t = jnp.transpose(bb.reshape(nblk, 8, S5_STATE, S5_CH), (0, 1, 3, 2))
        return (t[:, :, :, None, :] * eye8[None, :, None, :, None]).reshape(G * S5_CH, 8 * S5_STATE)

    def compact_c(cc):
        t = jnp.transpose(cc.reshape(nblk, 8, S5_CH, S5_STATE), (0, 1, 3, 2))
        return (t[:, :, :, None, :] * eye8[None, :, None, :, None]).reshape(G * S5_STATE, 8 * S5_CH)

    lam_re, lam_im = q['s5_lam_re'], q['s5_lam_im']
    dt = jnp.exp(q['s5_log_dt'])[:, None]
    mag = jnp.exp(lam_re * dt)
    ab_re = mag * jnp.cos(lam_im * dt)
    ab_im = mag * jnp.sin(lam_im * dt)
    den = lam_re * lam_re + lam_im * lam_im
    f_re = ((ab_re - 1.0) * lam_re + ab_im * lam_im) / den
    f_im = (ab_im * lam_re - (ab_re - 1.0) * lam_im) / den
    b_re, b_im = q['s5_b_re'], q['s5_b_im']
    bb_re = f_re[..., None] * b_re - f_im[..., None] * b_im
    bb_im = f_re[..., None] * b_im + f_im[..., None] * b_re
    return dict(
        conv_a=q['conv_a'], conv_c=q['conv_c'], conv_c_b=q['conv_c_b'], lru_ba=q['lru_ba'], lru_bx=q['lru_bx'],
        sp=jax.nn.softplus(-q['lru_lambda']), wa=block_diag(q['lru_wa']), wx=block_diag(q['lru_wx']),
        ar=ab_re.reshape(-1), ai=ab_im.reshape(-1), bbre=compact_b(bb_re), bbim=compact_b(bb_im),
        cre=compact_c(q['s5_c_re']), cim=compact_c(q['s5_c_im']), s5_d=q['s5_d'], s5_b_glu=q['s5_b_glu'])


def _s5_lag_weights(kp):
    J, CB, SB = S5_LAGS, S5_CB, S5_SB
    ar, ai = kp['ar'], kp['ai']
    W = ar.shape[0]
    nblk = W // SB
    cr, ci = jnp.ones_like(ar), jnp.zeros_like(ai)
    pows = []
    for _ in range(J + 1):
        pows.append((cr, ci))
        cr, ci = cr * ar - ci * ai, cr * ai + ci * ar
    pwr = jnp.transpose(jnp.stack([p[0] for p in pows[:J]]).reshape(J, nblk, SB), (1, 0, 2))[:, :, None, :]
    pwi = jnp.transpose(jnp.stack([p[1] for p in pows[:J]]).reshape(J, nblk, SB), (1, 0, 2))[:, :, None, :]

    def lagged(re, im, sign):
        out = jnp.concatenate([re * pwr - sign * im * pwi, sign * re * pwi + im * pwr], axis=-1)
        return out.reshape(nblk, J * CB, 2 * SB).astype(BF16)

    bbl = lagged(kp['bbre'].reshape(nblk, 1, CB, SB), kp['bbim'].reshape(nblk, 1, CB, SB), 1.0)
    c0r = jnp.transpose(kp['cre'].reshape(nblk, SB, CB), (0, 2, 1))[:, None]
    c0i = -jnp.transpose(kp['cim'].reshape(nblk, SB, CB), (0, 2, 1))[:, None]
    cl = lagged(c0r, c0i, -1.0)
    return _rows8([pows[J][0], pows[J][1]], W), bbl, cl


def _rows8(vecs, width):
    rows = [v.reshape(1, width).astype(F32) for v in vecs]
    return jnp.concatenate(rows + [jnp.zeros((8 - len(rows), width), F32)], axis=0)


def kernel(x, c, rel_bias, w_ada, b_ada, w_in, conv_a, conv_c, conv_c_b, lru_wa, lru_ba, lru_wx, lru_bx, lru_lambda, s5_lam_re, s5_lam_im, s5_log_dt, s5_b_re, s5_b_im, s5_c_re, s5_c_im, s5_d, s5_w_glu, s5_b_glu, w_out, ln_g, ln_b, loss_target, m_rel_bias, m_w_ada, m_b_ada, m_w_in, m_conv_a, m_conv_c, m_conv_c_b, m_lru_wa, m_lru_ba, m_lru_wx, m_lru_bx, m_lru_lambda, m_s5_lam_re, m_s5_lam_im, m_s5_log_dt, m_s5_b_re, m_s5_b_im, m_s5_c_re, m_s5_c_im, m_s5_d, m_s5_w_glu, m_s5_b_glu, m_w_out, m_ln_g, m_ln_b, v_rel_bias, v_w_ada, v_b_ada, v_w_in, v_conv_a, v_conv_c, v_conv_c_b, v_lru_wa, v_lru_ba, v_lru_wx, v_lru_bx, v_lru_lambda, v_s5_lam_re, v_s5_lam_im, v_s5_log_dt, v_s5_b_re, v_s5_b_im, v_s5_c_re, v_s5_c_im, v_s5_d, v_s5_w_glu, v_s5_b_glu, v_w_out, v_ln_g, v_ln_b):
    a = dict(locals())
    me = 4 * lax.axis_index("x") + 2 * lax.axis_index("y") + lax.axis_index("c")
    x0, target = x[0], loss_target[0]
    S, D = x0.shape
    BR = D // 4
    NS = w_in.shape[2]
    SH = BR // NDEV
    assert S % (BLK * DILATIONS[-1][1]) == 0 and BR % (8 * S5_CH) == 0

    small = _all_gather(_pack([c, conv_a, conv_c]), "gather_small")
    per_dev = [_unpack(small[d], [c.shape, conv_a.shape, conv_c.shape]) for d in range(NDEV)]
    c_all = jnp.concatenate([p[0] for p in per_dev], axis=0)
    conv_a_full = jnp.concatenate([p[1] for p in per_dev], axis=2)
    conv_c_full = jnp.concatenate([p[2] for p in per_dev], axis=2)
    me1 = me.reshape(1).astype(jnp.int32)
    w_in_bf, w_out_bf, w_glu_bf = w_in.astype(BF16), w_out.astype(BF16), s5_w_glu.astype(BF16)

    b_cols = lax.dynamic_slice(b_ada, (0, me * NS), (DEPTH, NS)).reshape(DEPTH, 1, NS)
    ada_all = _all_gather(_ada_cols(c_all, w_ada, b_cols), "gather_ada")
    ada_me = lax.dynamic_index_in_dim(ada_all, me, axis=2, keepdims=False)
    ada_me = jnp.transpose(ada_me, (1, 0, 2)).reshape(DEPTH, 3 * D)
    shift, scale, gate = ada_me[:, :D], ada_me[:, D:2 * D], ada_me[:, 2 * D:]

    bias_tabs, bias_pull = jax.vjp(_bias_tables, rel_bias)
    HD = BR // 8
    head_ones = jnp.kron(jnp.eye(8, dtype=F32), jnp.ones((HD, HD), F32))

    saved = []
    xl = x0
    for l in range(DEPTH):
        q = {n: a[n][l] for n in _LAYER_SMALL}
        q['conv_a'], q['conv_c'] = conv_a_full[l], conv_c_full[l]
        kp, pull = jax.vjp(_prep_layer, q)
        a8, bbl, cl = _s5_lag_weights(kp)
        mod = _rows8([1.0 + scale[l], shift[l]], D)
        arows = _rows8(list(kp['conv_a']), BR)
        lrows = _rows8(list(kp['conv_c']) + [kp['conv_c_b'], kp['lru_ba'], kp['lru_bx'], kp['sp']], BR)
        srows = _rows8([kp['s5_d'], kp['s5_b_glu']], BR)
        orows = _rows8([1.0 + gate[l], ln_g[l], ln_b[l]], D)
        wa, wx = kp['wa'].astype(BF16), kp['wx'].astype(BF16)
        s5w = [kp[n].astype(BF16) for n in ('bbre', 'bbim', 'cre', 'cim')]

        hbf = _modulate(xl, mod)
        proj, w_in_l, w_out_l, w_glu_l = _proj_gather(hbf, w_in_bf[l], [w_out_bf[l], w_glu_bf[l]], me1)
        w_out_l, w_glu_l = w_out_l.reshape(D, D), w_glu_l.reshape(BR, BR)
        ya = _conv_a_fwd(proj, arows, BR)
        os_, ls_ = [], []
        for g, (_, dil) in enumerate(DILATIONS):
            o, lse = _attn_fwd(proj, bias_tabs[g], dil, BR)
            os_.append(o)
            ls_.append(lse)
        yb = _attn_mix_fwd(proj, os_, ls_, BR)
        yc, hs = _lru_fwd(proj, lrows, wa, wx, BR)
        yd, xre, xim = _s5_fwd(proj, a8, bbl, s5w[2], s5w[3], srows, w_glu_l, BR)
        if l < DEPTH - 1:
            xn, y, cat = _out_ln([ya, yb, yc, yd], w_out_l, xl, orows)
        else:
            xn, y, cat, loss_local = _out_ln([ya, yb, yc, yd], w_out_l, xl, orows, target)
        saved.append(dict(x=xl, proj=proj, hbf=hbf, os=os_, ls=ls_, hs=hs, xre=xre, xim=xim, y=y, cat=cat, pull=pull,
                          mod=mod, arows=arows, lrows=lrows, srows=srows, orows=orows, wa=wa, wx=wx, s5w=s5w,
                          qw=(a8, cl), w_in=w_in_l, w_out=w_out_l, w_glu=w_glu_l))
        xl = xn

    dout = xl

    dbias = jnp.zeros_like(bias_tabs)
    lgrads, dada, dw_in_parts, dw_out_parts, glu_parts = [None] * DEPTH, [None] * DEPTH, [None] * DEPTH, [None] * DEPTH, [None] * DEPTH
    for l in reversed(range(DEPTH)):
        sv = saved[l]
        proj = sv['proj']
        dxres, dyb, dcat, lnsum = _ln_bwd(dout, sv['x'], sv['y'], sv['orows'], sv['w_out'])
        dw_out_parts[l] = _dw_scatter(sv['cat'], dyb, D // NDEV, True, me1, "dw_out_scatter")[0]
        dproj, asum = _conv_a_bwd(proj, dcat, sv['arows'], BR)
        dos, es, dbg = _attn_mix_bwd(proj, dcat, sv['os'], sv['ls'], head_ones, BR)
        dqkv, dbs = [], []
        for g, (_, dil) in enumerate(DILATIONS):
            dq_, dk_, dv_, db_ = _attn_bwd(proj, dos[g], es[g], sv['ls'][g], bias_tabs[g], dil, BR)
            dqkv.append((dq_, dk_, dv_))
            dbs.append(db_)
        dbias = dbias + jnp.stack(dbs)
        dproj = _attn_dsum(dqkv, dbg, dproj, BR)
        dproj, lsum, dwa, dwx = _lru_bwd(proj, sv['hs'], dcat, sv['lrows'], sv['wa'], sv['wx'], dproj, BR)
        dproj, ssum, dwg, da_, dbbre, dbbim, dcre, dcim = _s5_bwd(
            proj, dcat, sv['xre'], sv['xim'], *sv['qw'], *sv['s5w'], sv['srows'], sv['w_glu'], dproj, BR)
        dkp = dict(conv_a=asum[0:3], conv_c=lsum[0:4], conv_c_b=lsum[4], lru_ba=lsum[5], lru_bx=lsum[6], sp=lsum[7],
                   wa=dwa, wx=dwx, ar=da_[0], ai=da_[1], bbre=dbbre, bbim=dbbim, cre=dcre, cim=dcim,
                   s5_d=ssum[0], s5_b_glu=ssum[1])
        lgrads[l] = dict(sv['pull'](dkp)[0], ln_g=lnsum[0], ln_b=lnsum[1])
        sides = [(dwg.reshape(NDEV, BR // NDEV, BR), False)]
        if l > 0:
            dout, msum, glu_parts[l] = _dh(dproj, sv['w_in'], dxres, sv['x'], sv['mod'], sides)
            dw_in_parts[l] = _dw_scatter(sv['hbf'], dproj, NS, False, me1, "dw_in_scatter")[0]
        else:
            local = {'rel_bias': bias_pull(dbias)[0], 'loss': loss_local.reshape(1)}
            for n in _LAYER_SMALL + ['ln_g', 'ln_b']:
                local[n] = jnp.stack([lgrads[k][n] for k in range(DEPTH)])
            packed = _SMALL + ['loss']
            sides.append((_pack([local[n] for n in packed]), True))
            dout, msum = _dh(dproj, sv['w_in'], dxres, sv['x'], sv['mod'], [])
            dw_in_parts[l], glu_parts[l], *gathered = _dw_scatter(sv['hbf'], dproj, NS, False, me1, "dw_in_scatter", sides)
        dada[l] = jnp.concatenate([msum[1], msum[0], lnsum[2]])
    grad_x = dout[None]

    out = {}

    def put(name, res):
        out[name] = res

    put('w_in', _adamw(dw_in_parts, w_in, m_w_in, v_w_in, "adamw_w_in"))
    put('w_out', _adamw(dw_out_parts, w_out, m_w_out, v_w_out, "adamw_w_out"))
    put('s5_w_glu', _adamw(glu_parts, s5_w_glu, m_s5_w_glu, v_s5_w_glu, "adamw_w_glu"))

    dada_all = _all_gather(jnp.stack(dada), "gather_dada")
    dada_cols = jnp.transpose(lax.dynamic_slice(dada_all, (0, 0, me * NS), (NDEV, DEPTH, NS)), (1, 0, 2))
    gw_ada, gb_ada = _ada_grads(c_all, dada_all, dada_cols)
    put('w_ada', _adamw([gw_ada[l] for l in range(DEPTH)], w_ada, m_w_ada, v_w_ada, "adamw_w_ada"))

    shapes = [local[n].shape for n in packed]
    gs = dict(zip(packed, _unpack(_sum8(gathered[0], "sum_small_grads"), shapes)))
    loss = gs['loss'][0]
    gs['conv_a'] = lax.dynamic_slice_in_dim(gs['conv_a'], me * SH, SH, axis=2)
    gs['conv_c'] = lax.dynamic_slice_in_dim(gs['conv_c'], me * SH, SH, axis=2)
    gs['b_ada'] = gb_ada
    names = _SMALL + ['b_ada']
    shapes = [a[n].shape for n in names]
    res = _adamw([_pack([gs[n] for n in names])], _pack([a[n] for n in names])[None], _pack([a['m_' + n] for n in names])[None],
                 _pack([a['v_' + n] for n in names])[None], "adamw_small")
    unpacked = [_unpack(r[0], shapes) for r in res]
    for k, n in enumerate(names):
        put(n, tuple(unpacked[j][k] for j in range(4)))

    return (loss, grad_x, *[out[n][0] for n in _WEIGHTS], *[out[n][1] for n in _WEIGHTS],
            *[out[n][2] for n in _WEIGHTS], *[out[n][3] for n in _WEIGHTS])
```

```python
import math

import jax
import jax.numpy as jnp
from jax import lax
from jax.experimental import pallas as pl
from jax.experimental.pallas import tpu as pltpu

F32 = jnp.float32
BF16 = jnp.bfloat16

NDEV = 8
DEPTH = 2
BLK = 128
DILATIONS = ((128, 1), (512, 4), (2048, 16))
REL_BUCKETS = 32
REL_MAX_DIST = 2048
LRU_C = 8.0
S5_CH = 16
S5_STATE = 64
ALPHA = (2 * DEPTH) ** 0.25
LN_EPS = 1e-5
ADAM_LR, ADAM_B1, ADAM_B2, ADAM_EPS, ADAM_WD, ADAM_STEP = 0.001, 0.9, 0.999, 1e-08, 0.01, 10
NEG = -1e30
VMEM_LIMIT = 56 * 1024 * 1024
TILE_STREAM = 512
TILE_RESIDENT = 256
TILE_PROJ = 2048


def _call(body, **kw):
    return pl.pallas_call(body, **kw)


def _params(*sem):
    return pltpu.CompilerParams(dimension_semantics=sem, vmem_limit_bytes=VMEM_LIMIT)


def _sigmoid(x):
    return 1.0 / (1.0 + jnp.exp(-x))


def _silu(x):
    return x * _sigmoid(x)


def _dsilu(x):
    s = _sigmoid(x)
    return s * (1.0 + x * (1.0 - s))


_GC = math.sqrt(2.0 / math.pi)


def _gelu(x):
    return 0.5 * x * (1.0 + jnp.tanh(_GC * (x + 0.044715 * x * x * x)))


def _dgelu(x):
    t = jnp.tanh(_GC * (x + 0.044715 * x * x * x))
    return 0.5 * (1.0 + t) + 0.5 * x * (1.0 - t * t) * _GC * (1.0 + 3.0 * 0.044715 * x * x)


def _expm1(x):
    series = x * (1.0 + x * (1.0 / 2) * (1.0 + x * (1.0 / 3) * (1.0 + x * (1.0 / 4) * (1.0 + x * (1.0 / 5) * (1.0 + x * (1.0 / 6))))))
    return jnp.where(jnp.abs(x) < 0.25, series, jnp.exp(x) - 1.0)


def _dot(a, b):
    return jnp.dot(a, b, preferred_element_type=F32)


def _dot_nt(a, b):
    return lax.dot_general(a, b, (((1,), (1,)), ((), ())), preferred_element_type=F32)


def _dot_tn(a, b):
    return lax.dot_general(a, b, (((0,), (0,)), ((), ())), preferred_element_type=F32)


def _colsum(v):
    return jnp.sum(v, axis=0, keepdims=True)


def _all_gather(x, name):
    def body(x_ref, o_ref, send_sems, recv_sems, local_sem):
        ix, iy, ic = lax.axis_index("x"), lax.axis_index("y"), lax.axis_index("c")
        me, sibling = (ix, iy, ic), (ix, iy, 1 - ic)
        chips = [(1 - ix, iy), (ix, 1 - iy), (1 - ix, 1 - iy)]

        def slot(px, py, pc):
            return o_ref.at[4 * px + 2 * py + pc]

        def copy(k, block, to, src=None):
            return pltpu.make_async_remote_copy(
                src_ref=slot(*block) if src is None else src, dst_ref=slot(*block), send_sem=send_sems.at[k],
                recv_sem=recv_sems.at[k], device_id=to, device_id_type=pl.DeviceIdType.MESH)

        mine = pltpu.make_async_copy(x_ref, slot(*me), local_sem)
        mine.start()
        first = [copy(0, me, sibling, src=x_ref)] + [copy(1 + j, me, (*chip, ic), src=x_ref) for j, chip in enumerate(chips)]
        for cp in first:
            cp.start()
        passed = [copy(4 + j, (*chip, ic), sibling) for j, chip in enumerate(chips)]
        for j, chip in enumerate(chips):
            copy(1 + j, (*chip, ic), me).wait_recv()
            passed[j].start()
        copy(0, sibling, me).wait_recv()
        for j, chip in enumerate(chips):
            copy(4 + j, (*chip, 1 - ic), me).wait_recv()
        for cp in first + passed:
            cp.wait_send()
        mine.wait()

    return _call(
        body, name=name,
        out_shape=jax.ShapeDtypeStruct((NDEV,) + x.shape, x.dtype),
        in_specs=[pl.BlockSpec(memory_space=pl.ANY)],
        out_specs=pl.BlockSpec(memory_space=pl.ANY),
        scratch_shapes=[pltpu.SemaphoreType.DMA((NDEV - 1,)), pltpu.SemaphoreType.DMA((NDEV - 1,)), pltpu.SemaphoreType.DMA],
    )(x)


def _pack(arrs):
    parts = []
    for a in arrs:
        n = math.prod(a.shape)
        parts.append(jnp.pad(a.reshape(-1).astype(F32), (0, _pack_rows(n) * 128 - n)).reshape(-1, 128))
    rows = sum(p.shape[0] for p in parts)
    total = rows if rows <= 512 else -(-rows // 512) * 512
    if total > rows:
        parts.append(jnp.zeros((total - rows, 128), F32))
    return jnp.concatenate(parts, axis=0)


def _pack_rows(n):
    return -(-n // 1024) * 8


def _unpack(packed, shapes):
    out, row = [], 0
    for s in shapes:
        n = math.prod(s)
        rows = _pack_rows(n)
        out.append(packed[row:row + rows].reshape(-1)[:n].reshape(s))
        row += rows
    return out


def _sum8(parts, name):
    _, R, C = parts.shape
    TR = R
    for cand in (512, 256, 128, 64, 32, 16, 8):
        if R % cand == 0:
            TR = cand
            break

    def body(p_ref, o_ref):
        acc = p_ref[0]
        for d in range(1, NDEV):
            acc = acc + p_ref[d]
        o_ref[...] = acc

    return _call(
        body, name=name, grid=(R // TR,),
        in_specs=[pl.BlockSpec((NDEV, TR, C), lambda i: (0, i, 0))],
        out_specs=pl.BlockSpec((TR, C), lambda i: (i, 0)),
        out_shape=jax.ShapeDtypeStruct((R, C), F32),
        compiler_params=_params("arbitrary"),
    )(parts)


def _adamw(g_layers, w, m, v, name):
    L, R, C = w.shape
    g0 = g_layers[0]
    nparts = g0.shape[0] if g0.ndim == 3 else 0
    TR = R
    for cand in (256, 128, 64, 32, 16):
        if R % cand == 0 and cand * C <= 128 * 1024:
            TR = cand
            break

    def body(*refs):
        g_refs = refs[:L]
        w_ref, m_ref, v_ref, go_ref, d_ref, mo_ref, vo_ref = refs[L:]
        for ll in range(L):
            @pl.when(pl.program_id(0) == ll)
            def _(g_ref=g_refs[ll]):
                if nparts:
                    g = g_ref[0].astype(F32)
                    for d in range(1, nparts):
                        g = g + g_ref[d].astype(F32)
                else:
                    g = g_ref[...]
                m2 = ADAM_B1 * m_ref[0] + (1.0 - ADAM_B1) * g
                v2 = ADAM_B2 * v_ref[0] + (1.0 - ADAM_B2) * (g * g)
                m_hat = m2 / (1.0 - ADAM_B1 ** ADAM_STEP)
                v_hat = v2 / (1.0 - ADAM_B2 ** ADAM_STEP)
                go_ref[0] = g
                d_ref[0] = -ADAM_LR * (m_hat / (jnp.sqrt(v_hat) + ADAM_EPS) + ADAM_WD * w_ref[0])
                mo_ref[0] = m2
                vo_ref[0] = v2

    def gspec(ll):
        if nparts:
            return pl.BlockSpec((nparts, TR, C), lambda l, i: (0, jnp.where(l == ll, i, 0), 0))
        return pl.BlockSpec((TR, C), lambda l, i: (jnp.where(l == ll, i, 0), 0))

    spec = pl.BlockSpec((1, TR, C), lambda l, i: (l, i, 0))
    return _call(
        body, name=name, grid=(L, R // TR),
        in_specs=[gspec(ll) for ll in range(L)] + [spec, spec, spec], out_specs=[spec] * 4,
        out_shape=[jax.ShapeDtypeStruct((L, R, C), F32)] * 4,
        compiler_params=_params("arbitrary", "arbitrary"),
    )(*g_layers, w, m, v)


def _ada_cols(c_all, w_ada, b_cols):
    L, D, NS = w_ada.shape

    def body(c_ref, w_ref, b_ref, o_ref):
        cond = _silu(c_ref[...]).astype(BF16)
        o_ref[0] = _dot(cond, w_ref[0].astype(BF16)) + b_ref[0]

    return _call(
        body, name="ada_cols", grid=(L,),
        in_specs=[pl.BlockSpec((NDEV, D), lambda l: (0, 0)), pl.BlockSpec((1, D, NS), lambda l: (l, 0, 0)),
                  pl.BlockSpec((1, 1, NS), lambda l: (l, 0, 0))],
        out_specs=pl.BlockSpec((1, NDEV, NS), lambda l: (l, 0, 0)),
        out_shape=jax.ShapeDtypeStruct((L, NDEV, NS), F32),
        compiler_params=_params("arbitrary"),
    )(c_all, w_ada, b_cols)


def _ada_grads(c_all, dada_all, dada_cols):
    _, D = c_all.shape
    L, _, NS = dada_cols.shape
    D3 = dada_all.shape[2]

    def body(c_ref, da_ref, dc_ref, gw_ref, gb_ref):
        cond = _silu(c_ref[...]).astype(BF16)
        gw_ref[0] = _dot_tn(cond, dc_ref[0].astype(BF16))
        acc = da_ref[0]
        for d in range(1, NDEV):
            acc = acc + da_ref[d]
        gb_ref[...] = acc

    return _call(
        body, name="ada_grads", grid=(L,),
        in_specs=[pl.BlockSpec((NDEV, D), lambda l: (0, 0)), pl.BlockSpec((NDEV, L, D3), lambda l: (0, 0, 0)),
                  pl.BlockSpec((1, NDEV, NS), lambda l: (l, 0, 0))],
        out_specs=[pl.BlockSpec((1, D, NS), lambda l: (l, 0, 0)), pl.BlockSpec((L, D3), lambda l: (0, 0))],
        out_shape=[jax.ShapeDtypeStruct((L, D, NS), F32), jax.ShapeDtypeStruct((L, D3), F32)],
        compiler_params=_params("arbitrary"),
    )(c_all, dada_all, dada_cols)


def _tile(S, want):
    return want if S % want == 0 else S


def _modulate(x, mod):
    S, D = x.shape
    TS = _tile(S, TILE_STREAM)

    def body(x_ref, mod_ref, h_ref):
        h_ref[...] = (x_ref[...] * mod_ref[0:1, :] + mod_ref[1:2, :]).astype(BF16)

    spec = pl.BlockSpec((TS, D), lambda i: (i, 0))
    return _call(
        body, name="modulate", grid=(S // TS,),
        in_specs=[spec, pl.BlockSpec((8, D), lambda i: (0, 0))], out_specs=spec,
        out_shape=jax.ShapeDtypeStruct((S, D), BF16),
        compiler_params=_params("arbitrary"),
    )(x, mod)


def _peer(k):
    ix, iy, ic = lax.axis_index("x"), lax.axis_index("y"), lax.axis_index("c")
    bx, by, bc = (k >> 2) & 1, (k >> 1) & 1, k & 1
    px, py, pc = ix + bx - 2 * ix * bx, iy + by - 2 * iy * by, ic + bc - 2 * ic * bc
    return (px, py, pc), 4 * px + 2 * py + pc


GATHER_ORDER = (0, 1, 2, 4, 6, 3, 5, 7)
GATHER_DIRECT = (1, 2, 4, 6)
SCATTER_ORDER = (3, 2, 5, 4, 7, 6, 1, 0)


def _offset(order, k):
    off = jnp.int32(order[-1])
    for step in reversed(range(len(order) - 1)):
        off = jnp.where(k == step, jnp.int32(order[step]), off)
    return off


def _proj_gather(h, w_shard, extras, me1):
    assert GATHER_ORDER[0] == 0
    S, D = h.shape
    NS = w_shard.shape[1]
    TS = _tile(S, TILE_PROJ)
    nt = S // TS
    ne = len(extras)

    def body(me_ref, h_ref, w_ref, *rest):
        x_refs, o_ref, wg_ref = rest[:ne], rest[ne], rest[ne + 1]
        xg_refs = rest[ne + 2:2 * ne + 2]
        wbuf, send_sems, recv_sems, local_sems, load_sems = rest[2 * ne + 2:]
        k, i = pl.program_id(0), pl.program_id(1)
        me = me_ref[0]
        off = _offset(GATHER_ORDER, k)

        def push(src, dst, which, kk):
            peer, _ = _peer(kk)
            return pltpu.make_async_remote_copy(
                src_ref=src, dst_ref=dst.at[me], send_sem=send_sems.at[which, kk], recv_sem=recv_sems.at[which, kk],
                device_id=peer, device_id_type=pl.DeviceIdType.MESH)

        def own(src, dst, which):
            return pltpu.make_async_copy(src, dst.at[me], local_sems.at[which])

        pairs = [(w_ref, wg_ref)] + list(zip(x_refs, xg_refs))

        @pl.when(jnp.logical_and(k == 0, i == 0))
        def _():
            for which, (src, dst) in enumerate(pairs):
                own(src, dst, which).start()
                for kk in (GATHER_DIRECT if which == 0 else GATHER_ORDER[1:]):
                    push(src, dst, which, kk).start()
            first = pltpu.make_async_copy(w_ref, wbuf.at[0], load_sems.at[0])
            first.start()
            first.wait()

        @pl.when(jnp.logical_and(k > 0, i == 0))
        def _():
            push(w_ref, wg_ref, 0, off).wait_recv()
            _, blk = _peer(off)
            for step, o in enumerate(GATHER_ORDER):
                if o in GATHER_DIRECT and o > 1:
                    @pl.when(k == step)
                    def _(o=o):
                        sibling, _ = _peer(1)
                        pltpu.make_async_remote_copy(
                            src_ref=wg_ref.at[blk], dst_ref=wg_ref.at[blk], send_sem=send_sems.at[0, o + 1],
                            recv_sem=recv_sems.at[0, o + 1], device_id=sibling, device_id_type=pl.DeviceIdType.MESH).start()
            load = pltpu.make_async_copy(wg_ref.at[blk], wbuf.at[k % 2], load_sems.at[k % 2])
            load.start()
            load.wait()

        o_ref[...] = _dot(h_ref[...], wbuf[k % 2])

        @pl.when(jnp.logical_and(k == NDEV - 1, i == nt - 1))
        def _():
            for which, (src, dst) in enumerate(pairs):
                for kk in range(1, NDEV):
                    cp = push(src, dst, which, kk)
                    cp.wait_send()
                    if which > 0:
                        cp.wait_recv()
                own(src, dst, which).wait()

    def col(k, i, me_ref):
        m, off = me_ref[0], _offset(GATHER_ORDER, k)
        return i, (m | off) - (m & off)

    anyspec = pl.BlockSpec(memory_space=pl.ANY)
    grid_spec = pltpu.PrefetchScalarGridSpec(
        num_scalar_prefetch=1, grid=(NDEV, nt),
        in_specs=[pl.BlockSpec((TS, D), lambda k, i, me_ref: (i, 0)), anyspec] + [anyspec] * ne,
        out_specs=[pl.BlockSpec((TS, NS), col), anyspec] + [anyspec] * ne,
        scratch_shapes=[pltpu.VMEM((2, D, NS), BF16), pltpu.SemaphoreType.DMA((1 + ne, NDEV)), pltpu.SemaphoreType.DMA((1 + ne, NDEV)),
                        pltpu.SemaphoreType.DMA((1 + ne,)), pltpu.SemaphoreType.DMA((2,))])
    return _call(
        body, name="proj_gather", grid_spec=grid_spec,
        out_shape=[jax.ShapeDtypeStruct((S, NDEV * NS), F32), jax.ShapeDtypeStruct((NDEV, D, NS), BF16)]
        + [jax.ShapeDtypeStruct((NDEV,) + e.shape, e.dtype) for e in extras],
        compiler_params=_params("arbitrary", "arbitrary"),
    )(me1, h, w_shard, *extras)


def _dw_scatter(a, b, blk, by_rows, me1, name):
    S = a.shape[0]
    shape = (blk, b.shape[1]) if by_rows else (a.shape[1], blk)
    fixed = shape[0] * shape[1] * (4 + 2 * 2 + 4 * 2)
    per_row = 2 * 2 * (shape[0] + shape[1])
    TK = next((t for t in (2048, 1024, 512, 256) if S % t == 0 and fixed + t * per_row <= 0.8 * VMEM_LIMIT), S)
    nk = S // TK
    last = NDEV - 1
    assert all(o & 1 for o in SCATTER_ORDER[0::2]) and not any(o & 1 for o in SCATTER_ORDER[1::2]) and SCATTER_ORDER[-1] == 0
    assert all(SCATTER_ORDER[s] == SCATTER_ORDER[s + 1] + 1 for s in range(0, NDEV, 2))

    def body(me_ref, a_ref, b_ref, recv_ref, acc, stage, sib, send_sems, recv_sems, local_sem):
        k, kk = pl.program_id(0), pl.program_id(1)

        def chip_of(step):
            return _offset(SCATTER_ORDER, step) >> 1

        def to_sibling(j, slot):
            sibling, _ = _peer(1)
            return pltpu.make_async_remote_copy(
                src_ref=stage.at[slot], dst_ref=sib.at[j], send_sem=send_sems.at[0, j], recv_sem=recv_sems.at[0, j],
                device_id=sibling, device_id_type=pl.DeviceIdType.MESH)

        def to_chip(j, slot):
            peer, _ = _peer(2 * j)
            return pltpu.make_async_remote_copy(
                src_ref=stage.at[slot], dst_ref=recv_ref.at[j], send_sem=send_sems.at[1, j], recv_sem=recv_sems.at[1, j],
                device_id=peer, device_id_type=pl.DeviceIdType.MESH)

        own = pltpu.make_async_copy(stage.at[last % 2], recv_ref.at[0], local_sem)

        @pl.when(kk == 0)
        def _():
            acc[...] = jnp.zeros_like(acc)
        acc[...] += _dot_tn(a_ref[...], b_ref[...])

        @pl.when(kk == nk - 1)
        def _():
            j = chip_of(k)
            even_step = k % 2 == 0

            @pl.when(jnp.logical_and(k >= 2, even_step))
            def _():
                to_sibling(chip_of(k - 2), k % 2).wait_send()

            @pl.when(jnp.logical_and(k >= 3, jnp.logical_not(even_step)))
            def _():
                to_chip(chip_of(k - 2), k % 2).wait_send()

            @pl.when(even_step)
            def _():
                stage[k % 2] = acc[...].astype(BF16)
                to_sibling(j, k % 2).start()

            @pl.when(jnp.logical_not(even_step))
            def _():
                to_sibling(j, 0).wait_recv()
                stage[k % 2] = (acc[...] + sib[j].astype(F32)).astype(BF16)

                @pl.when(k < last)
                def _():
                    to_chip(j, k % 2).start()

                @pl.when(k == last)
                def _():
                    own.start()
                    to_sibling(SCATTER_ORDER[last - 1] >> 1, (last - 1) % 2).wait_send()
                    own.wait()
                    for jj in range(1, NDEV // 2):
                        to_chip(jj, 0).wait_recv()

    def blk_of(k, me_ref):
        m, off = me_ref[0], _offset(SCATTER_ORDER, k)
        return (m | off) - (m & off)

    if by_rows:
        in_specs = [pl.BlockSpec((TK, blk), lambda k, kk, me_ref: (kk, blk_of(k, me_ref))),
                    pl.BlockSpec((TK, b.shape[1]), lambda k, kk, me_ref: (kk, 0))]
    else:
        in_specs = [pl.BlockSpec((TK, a.shape[1]), lambda k, kk, me_ref: (kk, 0)),
                    pl.BlockSpec((TK, blk), lambda k, kk, me_ref: (kk, blk_of(k, me_ref)))]
    grid_spec = pltpu.PrefetchScalarGridSpec(
        num_scalar_prefetch=1, grid=(NDEV, nk), in_specs=in_specs,
        out_specs=pl.BlockSpec(memory_space=pl.ANY),
        scratch_shapes=[pltpu.VMEM(shape, F32), pltpu.VMEM((2,) + shape, BF16), pltpu.VMEM((NDEV // 2,) + shape, BF16),
                        pltpu.SemaphoreType.DMA((2, NDEV // 2)), pltpu.SemaphoreType.DMA((2, NDEV // 2)), pltpu.SemaphoreType.DMA])
    return _call(
        body, name=name, grid_spec=grid_spec,
        out_shape=jax.ShapeDtypeStruct((NDEV // 2,) + shape, BF16),
        compiler_params=_params("arbitrary", "arbitrary"),
    )(me1, a, b)


def _chunk(TS, BR, k):
    return pl.BlockSpec((TS, BR), lambda i: (i, k))


def _halo_prev(TS, BR, k):
    return pl.BlockSpec((8, BR), lambda i: (jnp.maximum(i * (TS // 8) - 1, 0), k))


def _conv_a_fwd(proj, w8, BR):
    S = proj.shape[0]
    TS = _tile(S, TILE_STREAM)

    def body(ab_ref, ac_ref, ax_ref, ag_ref, hc_ref, hx_ref, w_ref, o_ref, ext):
        i = pl.program_id(0)
        u = ac_ref[...] * ax_ref[...]
        ext[0:8, :] = jnp.where(i > 0, hc_ref[...] * hx_ref[...], 0.0)
        ext[8:8 + TS, :] = u
        conv = ext[pl.ds(6, TS), :] * w_ref[0:1, :] + ext[pl.ds(7, TS), :] * w_ref[1:2, :] + u * w_ref[2:3, :]
        o_ref[...] = ab_ref[...] * conv * _silu(ag_ref[...])

    return _call(
        body, name="conv_a_fwd", grid=(S // TS,),
        in_specs=[_chunk(TS, BR, 0), _chunk(TS, BR, 1), _chunk(TS, BR, 2), _chunk(TS, BR, 3),
                  _halo_prev(TS, BR, 1), _halo_prev(TS, BR, 2), pl.BlockSpec((8, BR), lambda i: (0, 0))],
        out_specs=pl.BlockSpec((TS, BR), lambda i: (i, 0)),
        out_shape=jax.ShapeDtypeStruct((S, BR), F32),
        scratch_shapes=[pltpu.VMEM((TS + 8, BR), F32)],
        compiler_params=_params("arbitrary"),
    )(proj, proj, proj, proj, proj, proj, w8)


ATT_UNIT = 2048
ATT_LANES = 128


def _attn_mask():
    i = lax.broadcasted_iota(jnp.int32, (BLK, 2 * BLK), 0)
    j = lax.broadcasted_iota(jnp.int32, (BLK, 2 * BLK), 1)
    return jnp.logical_and(j >= i, j <= i + BLK), j >= BLK


def _attn_rows(n, r, dil):
    return pl.ds(n * BLK * dil + r, BLK, stride=dil) if dil > 1 else pl.ds(n * BLK, BLK)


def _attn_geometry(S, BR, dil):
    HD = BR // 8
    U = ATT_UNIT
    assert S % U == 0 and BR % ATT_LANES == 0 and ATT_LANES % HD == 0 and U % (BLK * dil) == 0
    return HD, U, S // U, U // (BLK * dil), ATT_LANES // HD, BR // ATT_LANES


def _attn_fwd(proj, bias, dil, BR):
    S = proj.shape[0]
    HD, U, NU, nbu, hpb, HBK = _attn_geometry(S, BR, dil)
    scale = HD ** -0.5
    combos = [(n, r) for n in range(nbu) for r in range(dil)]

    def body(q_ref, kc_ref, kp_ref, vc_ref, vp_ref, b_ref, o_ref, l_ref, sbuf, pbuf):
        m = pl.program_id(1)
        band, is_cur = _attn_mask()
        first = jnp.logical_and(band, jnp.logical_or(is_cur, m > 0))

        def keys(ref_c, ref_p, n, r):
            prev = ref_c[_attn_rows(n - 1, r, dil), :] if n > 0 else ref_p[_attn_rows(nbu - 1, r, dil), :]
            return jnp.concatenate([prev, ref_c[_attn_rows(n, r, dil), :]], axis=0).astype(BF16)

        for c, (n, r) in enumerate(combos):
            q = (q_ref[_attn_rows(n, r, dil), :] * scale).astype(BF16)
            kk = keys(kc_ref, kp_ref, n, r)
            for h in range(hpb):
                sl = slice(h * HD, (h + 1) * HD)
                sbuf[c * hpb + h] = _dot_nt(q[:, sl], kk[:, sl])
        for c, (n, r) in enumerate(combos):
            lses = []
            for h in range(hpb):
                s = jnp.where(band if n > 0 else first, sbuf[c * hpb + h] + b_ref[h], NEG)
                mx = jnp.max(s, axis=-1, keepdims=True)
                p = jnp.exp(s - mx)
                l = jnp.sum(p, axis=-1, keepdims=True)
                pbuf[c * hpb + h] = (p * (1.0 / l)).astype(BF16)
                lses.append(jnp.broadcast_to(mx + jnp.log(l), (BLK, HD)))
            l_ref[_attn_rows(n, r, dil), :] = jnp.concatenate(lses, axis=1)
        for c, (n, r) in enumerate(combos):
            vv = keys(vc_ref, vp_ref, n, r)
            o_ref[_attn_rows(n, r, dil), :] = jnp.concatenate(
                [_dot(pbuf[c * hpb + h], vv[:, h * HD:(h + 1) * HD]) for h in range(hpb)], axis=1)

    def cur(c):
        return pl.BlockSpec((U, ATT_LANES), lambda hb, m: (m, c * HBK + hb))

    def prev(c):
        return pl.BlockSpec((U, ATT_LANES), lambda hb, m: (jnp.maximum(m - 1, 0), c * HBK + hb))

    ospec = pl.BlockSpec((U, ATT_LANES), lambda hb, m: (m, hb))
    nhb = len(combos) * hpb
    return _call(
        body, name="attn_fwd_d%d" % dil, grid=(HBK, NU),
        in_specs=[cur(4), cur(5), prev(5), cur(6), prev(6), pl.BlockSpec((hpb, BLK, 2 * BLK), lambda hb, m: (hb, 0, 0))],
        out_specs=[ospec, ospec],
        out_shape=[jax.ShapeDtypeStruct((S, BR), F32)] * 2,
        scratch_shapes=[pltpu.VMEM((nhb, BLK, 2 * BLK), F32), pltpu.VMEM((nhb, BLK, 2 * BLK), BF16)],
        compiler_params=_params("arbitrary", "arbitrary"),
    )(proj, proj, proj, proj, proj, bias)


def _mix_weights(l1, l2, l3):
    m = jnp.maximum(jnp.maximum(l1, l2), l3)
    e1, e2, e3 = jnp.exp(l1 - m), jnp.exp(l2 - m), jnp.exp(l3 - m)
    inv = 1.0 / (e1 + e2 + e3)
    return e1 * inv, e2 * inv, e3 * inv


def _attn_mix_fwd(proj, os_, ls_, BR):
    S = proj.shape[0]
    TS = _tile(S, TILE_STREAM)

    def body(o1, o2, o3, l1, l2, l3, g_ref, y_ref):
        w1, w2, w3 = _mix_weights(l1[...], l2[...], l3[...])
        y_ref[...] = (w1 * o1[...] + w2 * o2[...] + w3 * o3[...]) * _silu(g_ref[...])

    spec = pl.BlockSpec((TS, BR), lambda i: (i, 0))
    return _call(
        body, name="attn_mix_fwd", grid=(S // TS,),
        in_specs=[spec] * 6 + [_chunk(TS, BR, 7)], out_specs=spec,
        out_shape=jax.ShapeDtypeStruct((S, BR), F32),
        compiler_params=_params("arbitrary"),
    )(*os_, *ls_, proj)


def _lru_gates(xc, wa, wx, ba, bx, sp):
    xb = xc.astype(BF16)
    r = _sigmoid(_dot(xb, wa) + ba)
    ig = _sigmoid(_dot(xb, wx) + bx)
    la = -LRU_C * r * sp
    a = jnp.exp(la)
    mult = jnp.sqrt(-_expm1(2.0 * la))
    return r, ig, a, mult


def _lru_fwd(proj, rows, wa, wx, BR):
    S = proj.shape[0]
    TS = _tile(S, TILE_RESIDENT)
    PAD = TS // 2

    def body(cx_ref, hx_ref, cg_ref, r_ref, wa_ref, wx_ref, y_ref, h_ref, ext, sa, sb, carry):
        i = pl.program_id(0)

        @pl.when(i == 0)
        def _():
            sa[0:PAD, :] = jnp.ones((PAD, BR), F32)
            sb[0:PAD, :] = jnp.zeros((PAD, BR), F32)
            carry[...] = jnp.zeros_like(carry)

        cx = cx_ref[...]
        ext[0:8, :] = jnp.where(i > 0, hx_ref[...], 0.0)
        ext[8:8 + TS, :] = cx
        xc = (ext[pl.ds(5, TS), :] * r_ref[0:1, :] + ext[pl.ds(6, TS), :] * r_ref[1:2, :]
              + ext[pl.ds(7, TS), :] * r_ref[2:3, :] + cx * r_ref[3:4, :] + r_ref[4:5, :])
        _, ig, a, mult = _lru_gates(xc, wa_ref[...], wx_ref[...], r_ref[5:6, :], r_ref[6:7, :], r_ref[7:8, :])
        sa[PAD:PAD + TS, :] = a
        sb[PAD:PAD + TS, :] = mult * ig * xc
        d = 1
        while d < TS:
            A, B = sa[PAD:PAD + TS, :], sb[PAD:PAD + TS, :]
            As, Bs = sa[pl.ds(PAD - d, TS), :], sb[pl.ds(PAD - d, TS), :]
            sb[PAD:PAD + TS, :] = A * Bs + B
            sa[PAD:PAD + TS, :] = A * As
            d *= 2
        h = sb[PAD:PAD + TS, :] + sa[PAD:PAD + TS, :] * carry[0:1, :]
        carry[0:1, :] = h[TS - 1:TS, :]
        h_ref[...] = h
        y_ref[...] = h * _silu(cg_ref[...])

    full = pl.BlockSpec((BR, BR), lambda i: (0, 0))
    spec = pl.BlockSpec((TS, BR), lambda i: (i, 0))
    return _call(
        body, name="lru_fwd", grid=(S // TS,),
        in_specs=[_chunk(TS, BR, 8), _halo_prev(TS, BR, 8), _chunk(TS, BR, 9), pl.BlockSpec((8, BR), lambda i: (0, 0)), full, full],
        out_specs=[spec, spec],
        out_shape=[jax.ShapeDtypeStruct((S, BR), F32)] * 2,
        scratch_shapes=[pltpu.VMEM((TS + 8, BR), F32), pltpu.VMEM((PAD + TS, BR), F32), pltpu.VMEM((PAD + TS, BR), F32),
                        pltpu.VMEM((8, BR), F32)],
        compiler_params=_params("arbitrary"),
    )(proj, proj, proj, rows, wa, wx)


S5_LAGS = 8
S5_CB = 8 * S5_CH
S5_SB = 8 * S5_STATE


def _s5_fwd(proj, a8, bbl, cre, cim, rows, wg, BR):
    S = proj.shape[0]
    W = a8.shape[1]
    TS = _tile(S, TILE_STREAM)
    CB, SB, J = S5_CB, S5_SB, S5_LAGS
    nblk = BR // CB

    def body(u_ref, uh_ref, dg_ref, a8_ref, bbl_ref, cre_ref, cim_ref, r_ref, wg_ref,
             y_ref, xre_ref, xim_ref, uext, ulag, sre, sim, ypre_s, carry):
        i = pl.program_id(0)

        @pl.when(i == 0)
        def _():
            carry[...] = jnp.zeros_like(carry)

        uext[0:8, :] = jnp.where(i > 0, uh_ref[...], 0.0)
        uext[8:8 + TS, :] = u_ref[...]
        for m in range(nblk):
            cs, ws = slice(m * CB, (m + 1) * CB), slice(m * SB, (m + 1) * SB)
            for j in range(J):
                ulag[:, j * CB:(j + 1) * CB] = uext[pl.ds(8 - j, TS), cs].astype(BF16)
            w = _dot(ulag[...], bbl_ref[m])
            sre[...] = w[:, 0:SB]
            sim[...] = w[:, SB:2 * SB]
            ar = jnp.broadcast_to(a8_ref[0:1, ws], (8, SB))
            ai = jnp.broadcast_to(a8_ref[1:2, ws], (8, SB))
            xr, xi = carry[0:8, ws], carry[8:16, ws]
            for g in range(TS // 8):
                rg = slice(8 * g, 8 * g + 8)
                xr, xi = sre[rg, :] + ar * xr - ai * xi, sim[rg, :] + ar * xi + ai * xr
                sre[rg, :] = xr
                sim[rg, :] = xi
            carry[0:8, ws] = xr
            carry[8:16, ws] = xi
            xre, xim = sre[...], sim[...]
            xre_ref[:, ws] = xre
            xim_ref[:, ws] = xim
            ypre_s[:, cs] = _dot(xre.astype(BF16), cre_ref[ws, :]) - _dot(xim.astype(BF16), cim_ref[ws, :])
        dg = dg_ref[...]
        yg = _gelu(ypre_s[...] + r_ref[0:1, :] * u_ref[...])
        s = _sigmoid(_dot(yg.astype(BF16), wg_ref[...]) + r_ref[1:2, :])
        y_ref[...] = yg * s * _silu(dg)

    def const(shape):
        return pl.BlockSpec(shape, lambda i: (0,) * len(shape))

    return _call(
        body, name="s5_fwd", grid=(S // TS,),
        in_specs=[_chunk(TS, BR, 10), _halo_prev(TS, BR, 10), _chunk(TS, BR, 11), const((8, W)), const((nblk, J * CB, 2 * SB)),
                  const((W, CB)), const((W, CB)), const((8, BR)), const((BR, BR))],
        out_specs=[pl.BlockSpec((TS, BR), lambda i: (i, 0)), pl.BlockSpec((TS, W), lambda i: (i, 0)), pl.BlockSpec((TS, W), lambda i: (i, 0))],
        out_shape=[jax.ShapeDtypeStruct((S, BR), F32), jax.ShapeDtypeStruct((S, W), F32), jax.ShapeDtypeStruct((S, W), F32)],
        scratch_shapes=[pltpu.VMEM((TS + 8, BR), F32), pltpu.VMEM((TS, J * CB), BF16), pltpu.VMEM((TS, SB), F32), pltpu.VMEM((TS, SB), F32),
                        pltpu.VMEM((TS, BR), F32), pltpu.VMEM((16, W), F32)],
        compiler_params=_params("arbitrary"),
    )(proj, proj, proj, a8, bbl, cre, cim, rows, wg)


def _out_ln(ys, w_out, x, rows, target=None):
    S, D = x.shape
    BR = D // 4
    TS = _tile(S, TILE_RESIDENT)
    last = target is not None

    def body(ya, yb, yc, yd, w_ref, x_ref, r_ref, *rest):
        t_ref = rest[0] if last else None
        o_ref, y_ref, cat_ref = rest[1:4] if last else rest[0:3]
        cat = jnp.concatenate([ya[...], yb[...], yc[...], yd[...]], axis=1).astype(BF16)
        y = _dot(cat, w_ref[...])
        z = ALPHA * x_ref[...] + r_ref[0:1, :] * y
        mu = jnp.mean(z, axis=-1, keepdims=True)
        zc = z - mu
        var = jnp.mean(zc * zc, axis=-1, keepdims=True)
        xn = zc * lax.rsqrt(var + LN_EPS) * r_ref[1:2, :] + r_ref[2:3, :]
        y_ref[...] = y.astype(BF16)
        cat_ref[...] = cat
        if last:
            l_ref = rest[4]

            @pl.when(pl.program_id(0) == 0)
            def _():
                l_ref[...] = jnp.zeros_like(l_ref)
            diff = xn - t_ref[...]
            o_ref[...] = diff * (1.0 / D)
            l_ref[...] += jnp.sum(_colsum(diff * diff), axis=1, keepdims=True) * (0.5 / D)
        else:
            o_ref[...] = xn
            rest[3][...] = (xn * r_ref[3:4, :] + r_ref[4:5, :]).astype(BF16)

    yspec = pl.BlockSpec((TS, BR), lambda i: (i, 0))
    spec = pl.BlockSpec((TS, D), lambda i: (i, 0))
    lspec = pl.BlockSpec((1, 1), lambda i: (0, 0))
    extra_spec, extra_shape = (lspec, jax.ShapeDtypeStruct((1, 1), F32)) if last else (spec, jax.ShapeDtypeStruct((S, D), BF16))
    return _call(
        body, name="out_ln_loss" if last else "out_ln", grid=(S // TS,),
        in_specs=[yspec] * 4 + [pl.BlockSpec((D, D), lambda i: (0, 0)), spec, pl.BlockSpec((8, D), lambda i: (0, 0))] + [spec] * last,
        out_specs=[spec, spec, spec, extra_spec],
        out_shape=[jax.ShapeDtypeStruct((S, D), F32), jax.ShapeDtypeStruct((S, D), BF16), jax.ShapeDtypeStruct((S, D), BF16), extra_shape],
        compiler_params=_params("arbitrary"),
    )(*ys, w_out, x, rows, *([target] if last else []))


def _ln_bwd(dout, x, y, rows, w_out):
    S, D = x.shape
    TS = _tile(S, TILE_RESIDENT)

    def body(do_ref, x_ref, y_ref, r_ref, w_ref, dxr_ref, dyb_ref, dcat_ref, acc_ref):
        @pl.when(pl.program_id(0) == 0)
        def _():
            acc_ref[...] = jnp.zeros_like(acc_ref)
        g1, lg = r_ref[0:1, :], r_ref[1:2, :]
        yv = y_ref[...].astype(F32)
        z = ALPHA * x_ref[...] + g1 * yv
        mu = jnp.mean(z, axis=-1, keepdims=True)
        zc = z - mu
        var = jnp.mean(zc * zc, axis=-1, keepdims=True)
        rstd = lax.rsqrt(var + LN_EPS)
        xhat = zc * rstd
        do = do_ref[...]
        dxh = do * lg
        dz = rstd * (dxh - jnp.mean(dxh, axis=-1, keepdims=True) - xhat * jnp.mean(dxh * xhat, axis=-1, keepdims=True))
        dxr_ref[...] = (ALPHA * dz).astype(BF16)
        dyb = (g1 * dz).astype(BF16)
        dyb_ref[...] = dyb
        dcat_ref[...] = _dot_nt(dyb, w_ref[...])
        acc_ref[0:1, :] += _colsum(do * xhat)
        acc_ref[1:2, :] += _colsum(do)
        acc_ref[2:3, :] += _colsum(dz * yv)

    spec = pl.BlockSpec((TS, D), lambda i: (i, 0))
    rspec = pl.BlockSpec((8, D), lambda i: (0, 0))
    return _call(
        body, name="ln_bwd", grid=(S // TS,),
        in_specs=[spec, spec, spec, rspec, pl.BlockSpec((D, D), lambda i: (0, 0))],
        out_specs=[spec, spec, spec, rspec],
        out_shape=[jax.ShapeDtypeStruct((S, D), BF16), jax.ShapeDtypeStruct((S, D), BF16), jax.ShapeDtypeStruct((S, D), F32),
                   jax.ShapeDtypeStruct((8, D), F32)],
        compiler_params=_params("arbitrary"),
    )(dout, x, y, rows, w_out)


def _dh(dproj, wg, dxres, x, mod, sides):
    S, D = x.shape
    NS = wg.shape[2]
    TS = _tile(S, TILE_RESIDENT)
    nt = S // TS
    ns = len(sides)

    def body(dp_ref, w_hbm, dxr_ref, x_ref, mod_ref, *rest):
        s_refs, dx_ref, sum_ref, so_refs = rest[:ns], rest[ns], rest[ns + 1], rest[ns + 2:2 * ns + 2]
        w_vmem, sem, send_sems, recv_sems, local_sems = rest[2 * ns + 2:]
        i = pl.program_id(0)
        ix, iy, ic = lax.axis_index("x"), lax.axis_index("y"), lax.axis_index("c")
        me = 4 * ix + 2 * iy + ic

        def exchange(which):
            (_, gather), x_side, o_side = sides[which], s_refs[which], so_refs[which]
            copies = [pltpu.make_async_copy(x_side if gather else x_side.at[me], o_side.at[me], local_sems.at[which])]
            for k in range(1, NDEV):
                px = 1 - ix if k & 4 else ix
                py = 1 - iy if k & 2 else iy
                pc = 1 - ic if k & 1 else ic
                copies.append(pltpu.make_async_remote_copy(
                    src_ref=x_side if gather else x_side.at[4 * px + 2 * py + pc], dst_ref=o_side.at[me],
                    send_sem=send_sems.at[which, k - 1], recv_sem=recv_sems.at[which, k - 1],
                    device_id=(px, py, pc), device_id_type=pl.DeviceIdType.MESH))
            return copies

        @pl.when(i == 0)
        def _():
            load = pltpu.make_async_copy(w_hbm, w_vmem, sem)
            load.start()
            for which in range(ns):
                for cp in exchange(which):
                    cp.start()
            sum_ref[...] = jnp.zeros_like(sum_ref)
            load.wait()

        dh = _dot_nt(dp_ref[:, 0:NS], w_vmem[0])
        for j in range(1, NDEV):
            dh = dh + _dot_nt(dp_ref[:, j * NS:(j + 1) * NS], w_vmem[j])
        dx_ref[...] = dxr_ref[...].astype(F32) + dh * mod_ref[0:1, :]
        sum_ref[0:1, :] += _colsum(dh * x_ref[...])
        sum_ref[1:2, :] += _colsum(dh)

        @pl.when(i == nt - 1)
        def _():
            for which in range(ns):
                for cp in exchange(which):
                    cp.wait()

    spec = pl.BlockSpec((TS, D), lambda i: (i, 0))
    rspec = pl.BlockSpec((8, D), lambda i: (0, 0))
    anyspec = pl.BlockSpec(memory_space=pl.ANY)
    side_shapes = [jax.ShapeDtypeStruct(((NDEV,) + a.shape) if g else a.shape, a.dtype) for a, g in sides]
    return _call(
        body, name="dh", grid=(nt,),
        in_specs=[pl.BlockSpec((TS, NDEV * NS), lambda i: (i, 0)), anyspec, spec, spec, rspec] + [anyspec] * ns,
        out_specs=[spec, rspec] + [anyspec] * ns,
        out_shape=[jax.ShapeDtypeStruct((S, D), F32), jax.ShapeDtypeStruct((8, D), F32)] + side_shapes,
        scratch_shapes=[pltpu.VMEM((NDEV, D, NS), BF16), pltpu.SemaphoreType.DMA, pltpu.SemaphoreType.DMA((max(ns, 1), NDEV - 1)),
                        pltpu.SemaphoreType.DMA((max(ns, 1), NDEV - 1)), pltpu.SemaphoreType.DMA((max(ns, 1),))],
        compiler_params=_params("arbitrary"),
    )(dproj, wg, dxres, x, mod, *[a for a, _ in sides])


def _halo_next(S, TS, BR, k):
    last = S // 8 - 1
    return pl.BlockSpec((8, BR), lambda i: (jnp.minimum((i + 1) * (TS // 8), last), k))


def _conv_a_bwd(proj, dcat, w8, BR):
    S = proj.shape[0]
    TS = _tile(S, TILE_STREAM)
    nt = S // TS

    def body(ab_ref, ac_ref, ax_ref, ag_ref, dy_ref, hc_ref, hx_ref, nab_ref, nag_ref, ndy_ref, w_ref, dp_ref, dw_ref, ext, dext):
        i = pl.program_id(0)

        @pl.when(i == 0)
        def _():
            dw_ref[...] = jnp.zeros_like(dw_ref)

        w0, w1, w2 = w_ref[0:1, :], w_ref[1:2, :], w_ref[2:3, :]
        ab, ac, ax, ag, dy = ab_ref[...], ac_ref[...], ax_ref[...], ag_ref[...], dy_ref[...]
        u = ac * ax
        ext[0:8, :] = jnp.where(i > 0, hc_ref[...] * hx_ref[...], 0.0)
        ext[8:8 + TS, :] = u
        u1, u2 = ext[pl.ds(7, TS), :], ext[pl.ds(6, TS), :]
        conv = u2 * w0 + u1 * w1 + u * w2
        sg = _silu(ag)
        dconv = dy * ab * sg
        dext[0:TS, :] = dconv
        dext[TS:TS + 8, :] = jnp.where(i < nt - 1, ndy_ref[...] * nab_ref[...] * _silu(nag_ref[...]), 0.0)
        du = w2 * dconv + w1 * dext[pl.ds(1, TS), :] + w0 * dext[pl.ds(2, TS), :]
        dp_ref[:, 0:BR] = (dy * conv * sg).astype(BF16)
        dp_ref[:, BR:2 * BR] = (du * ax).astype(BF16)
        dp_ref[:, 2 * BR:3 * BR] = (du * ac).astype(BF16)
        dp_ref[:, 3 * BR:4 * BR] = (dy * ab * conv * _dsilu(ag)).astype(BF16)
        dw_ref[0:1, :] += _colsum(dconv * u2)
        dw_ref[1:2, :] += _colsum(dconv * u1)
        dw_ref[2:3, :] += _colsum(dconv * u)

    rspec = pl.BlockSpec((8, BR), lambda i: (0, 0))
    return _call(
        body, name="conv_a_bwd", grid=(nt,),
        in_specs=[_chunk(TS, BR, 0), _chunk(TS, BR, 1), _chunk(TS, BR, 2), _chunk(TS, BR, 3), _chunk(TS, BR, 0),
                  _halo_prev(TS, BR, 1), _halo_prev(TS, BR, 2), _halo_next(S, TS, BR, 0), _halo_next(S, TS, BR, 3),
                  _halo_next(S, TS, BR, 0), rspec],
        out_specs=[pl.BlockSpec((TS, 4 * BR), lambda i: (i, 0)), rspec],
        out_shape=[jax.ShapeDtypeStruct((S, 12 * BR), BF16), jax.ShapeDtypeStruct((8, BR), F32)],
        scratch_shapes=[pltpu.VMEM((TS + 8, BR), F32), pltpu.VMEM((TS + 8, BR), F32)],
        compiler_params=_params("arbitrary"),
    )(proj, proj, proj, proj, dcat, proj, proj, proj, proj, dcat, w8)


def _attn_mix_bwd(proj, dcat, os_, ls_, head_ones, BR):
    S = proj.shape[0]
    TS = _tile(S, TILE_STREAM)

    def body(dy_ref, g_ref, o1, o2, o3, l1, l2, l3, ones_ref, d1, d2, d3, e1, e2, e3, dg_ref):
        w1, w2, w3 = _mix_weights(l1[...], l2[...], l3[...])
        mix = w1 * o1[...] + w2 * o2[...] + w3 * o3[...]
        g = g_ref[...]
        dy = dy_ref[...]
        dmix = dy * _silu(g)
        dg_ref[...] = dy * mix * _dsilu(g)
        t = jnp.dot(dmix * mix, ones_ref[...], preferred_element_type=F32, precision=lax.Precision.HIGHEST)
        d1[...] = w1 * dmix
        d2[...] = w2 * dmix
        d3[...] = w3 * dmix
        e1[...] = -w1 * t
        e2[...] = -w2 * t
        e3[...] = -w3 * t

    spec = pl.BlockSpec((TS, BR), lambda i: (i, 0))
    outs = _call(
        body, name="attn_mix_bwd", grid=(S // TS,),
        in_specs=[_chunk(TS, BR, 1), _chunk(TS, BR, 7)] + [spec] * 6 + [pl.BlockSpec((BR, BR), lambda i: (0, 0))],
        out_specs=[spec] * 7,
        out_shape=[jax.ShapeDtypeStruct((S, BR), F32)] * 7,
        compiler_params=_params("arbitrary"),
    )(dcat, proj, *os_, *ls_, head_ones)
    return outs[0:3], outs[3:6], outs[6]


def _attn_bwd(proj, do, e, lse, bias, dil, BR):
    S = proj.shape[0]
    HD, U, NU, nbu, hpb, HBK = _attn_geometry(S, BR, dil)
    scale = HD ** -0.5
    combos = [(n, r) for n in range(nbu) for r in range(dil)]

    def body(q_ref, kc_ref, kp_ref, vc_ref, vp_ref, do_ref, e_ref, l_ref, b_ref, dq_ref, dk_ref, dv_ref, db_ref,
             sbuf, dpbuf, pbuf, dsbuf, qs, dos, kks, dkacc, dvacc, nxk, nxv, cark, carv):
        j = pl.program_id(1)
        mu = NU - 1 - j
        band, is_cur = _attn_mask()
        first = jnp.logical_and(band, jnp.logical_or(is_cur, mu > 0))

        @pl.when(j == 0)
        def _():
            cark[...] = jnp.zeros_like(cark)
            carv[...] = jnp.zeros_like(carv)
            db_ref[...] = jnp.zeros_like(db_ref)

        if nbu > 1:
            nxk[...] = jnp.zeros_like(nxk)
            nxv[...] = jnp.zeros_like(nxv)

        def keys(ref_c, ref_p, n, r):
            prev = ref_c[_attn_rows(n - 1, r, dil), :] if n > 0 else ref_p[_attn_rows(nbu - 1, r, dil), :]
            return jnp.concatenate([prev, ref_c[_attn_rows(n, r, dil), :]], axis=0).astype(BF16)

        for c, (n, r) in enumerate(combos):
            rows = _attn_rows(n, r, dil)
            q = (q_ref[rows, :] * scale).astype(BF16)
            dob = do_ref[rows, :].astype(BF16)
            kk, vv = keys(kc_ref, kp_ref, n, r), keys(vc_ref, vp_ref, n, r)
            qs[c], dos[c], kks[c] = q, dob, kk
            for h in range(hpb):
                sl = slice(h * HD, (h + 1) * HD)
                sbuf[c * hpb + h] = _dot_nt(q[:, sl], kk[:, sl])
                dpbuf[c * hpb + h] = _dot_nt(dob[:, sl], vv[:, sl])
        dbs = [None] * hpb
        for c, (n, r) in enumerate(combos):
            rows = _attn_rows(n, r, dil)
            lse, ev = l_ref[rows, :], e_ref[rows, :]
            for h in range(hpb):
                one = slice(h * HD, h * HD + 1)
                s = jnp.where(band if n > 0 else first, sbuf[c * hpb + h] + b_ref[h], NEG)
                p = jnp.exp(s - lse[:, one])
                ds = p * (dpbuf[c * hpb + h] + ev[:, one])
                pbuf[c * hpb + h] = p.astype(BF16)
                dsbuf[c * hpb + h] = ds.astype(BF16)
                dbs[h] = ds if dbs[h] is None else dbs[h] + ds
        for h in range(hpb):
            db_ref[h] += dbs[h]
        for c, (n, r) in enumerate(combos):
            rows = _attn_rows(n, r, dil)
            q, dob, kk = qs[c], dos[c], kks[c]
            dq, dkk, dvv = [], [], []
            for h in range(hpb):
                sl = slice(h * HD, (h + 1) * HD)
                dsb = dsbuf[c * hpb + h]
                dq.append(_dot(dsb, kk[:, sl]) * scale)
                dkk.append(_dot_tn(dsb, q[:, sl]))
                dvv.append(_dot_tn(pbuf[c * hpb + h], dob[:, sl]))
            dq_ref[rows, :] = jnp.concatenate(dq, axis=1)
            dkk, dvv = jnp.concatenate(dkk, axis=1), jnp.concatenate(dvv, axis=1)
            dkacc[rows, :] = dkk[BLK:2 * BLK, :]
            dvacc[rows, :] = dvv[BLK:2 * BLK, :]
            if n > 0:
                prow = _attn_rows(n - 1, r, dil)
                dkacc[prow, :] += dkk[0:BLK, :]
                dvacc[prow, :] += dvv[0:BLK, :]
            else:
                prow = _attn_rows(nbu - 1, r, dil)
                nxk[prow, :] = dkk[0:BLK, :]
                nxv[prow, :] = dvv[0:BLK, :]
        dk_ref[...] = dkacc[...] + cark[...]
        dv_ref[...] = dvacc[...] + carv[...]
        cark[...] = nxk[...]
        carv[...] = nxv[...]

    def cur(c):
        return pl.BlockSpec((U, ATT_LANES), lambda hb, j: (NU - 1 - j, c * HBK + hb))

    def prev(c):
        return pl.BlockSpec((U, ATT_LANES), lambda hb, j: (jnp.maximum(NU - 2 - j, 0), c * HBK + hb))

    own = pl.BlockSpec((U, ATT_LANES), lambda hb, j: (NU - 1 - j, hb))
    bspec = pl.BlockSpec((hpb, BLK, 2 * BLK), lambda hb, j: (hb, 0, 0))
    nhb = len(combos) * hpb
    unit = pltpu.VMEM((U, ATT_LANES), F32)
    return _call(
        body, name="attn_bwd_d%d" % dil, grid=(HBK, NU),
        in_specs=[cur(4), cur(5), prev(5), cur(6), prev(6), own, own, own, bspec],
        out_specs=[own, own, own, bspec],
        out_shape=[jax.ShapeDtypeStruct((S, BR), F32)] * 3 + [jax.ShapeDtypeStruct((8, BLK, 2 * BLK), F32)],
        scratch_shapes=[pltpu.VMEM((nhb, BLK, 2 * BLK), F32)] * 2 + [pltpu.VMEM((nhb, BLK, 2 * BLK), BF16)] * 2
        + [pltpu.VMEM((len(combos), BLK, ATT_LANES), BF16)] * 2 + [pltpu.VMEM((len(combos), 2 * BLK, ATT_LANES), BF16)] + [unit] * 6,
        compiler_params=_params("arbitrary", "arbitrary"),
    )(proj, proj, proj, proj, proj, do, e, lse, bias)


def _attn_dsum(ds, dbg, dproj, BR):
    S = dbg.shape[0]
    TS = _tile(S, TILE_STREAM)

    def body(*refs):
        o_ref = refs[11]
        for k in range(3):
            o_ref[:, k * BR:(k + 1) * BR] = (refs[k][...] + refs[3 + k][...] + refs[6 + k][...]).astype(BF16)
        o_ref[:, 3 * BR:4 * BR] = refs[9][...].astype(BF16)

    spec = pl.BlockSpec((TS, BR), lambda i: (i, 0))
    return _call(
        body, name="attn_dsum", grid=(S // TS,),
        in_specs=[spec] * 10 + [pl.BlockSpec(memory_space=pl.ANY)],
        out_specs=pl.BlockSpec((TS, 4 * BR), lambda i: (i, 1)),
        out_shape=jax.ShapeDtypeStruct(dproj.shape, BF16),
        input_output_aliases={10: 0},
        compiler_params=_params("arbitrary"),
    )(*[t for trip in ds for t in trip], dbg, dproj)


def _lru_bwd(proj, h, dcat, rows, wa, wx, dproj, BR):
    S = proj.shape[0]
    TS = _tile(S, TILE_RESIDENT)
    PAD = TS // 2
    nt = S // TS

    def body(cx_ref, hx_ref, cg_ref, h_ref, hh_ref, dy_ref, r_ref, wa_ref, wx_ref, _,
             dp_ref, sum_ref, dwa_ref, dwx_ref, ext, hext, aext, dext, sa, sb, carry):
        i = pl.program_id(0)
        ti = nt - 1 - i

        @pl.when(i == 0)
        def _():
            sa[TS:TS + PAD, :] = jnp.ones((PAD, BR), F32)
            sb[TS:TS + PAD, :] = jnp.zeros((PAD, BR), F32)
            carry[...] = jnp.zeros_like(carry)
            dext[TS:TS + 8, :] = jnp.zeros((8, BR), F32)
            sum_ref[...] = jnp.zeros_like(sum_ref)
            dwa_ref[...] = jnp.zeros_like(dwa_ref)
            dwx_ref[...] = jnp.zeros_like(dwx_ref)

        w0, w1, w2, w3 = r_ref[0:1, :], r_ref[1:2, :], r_ref[2:3, :], r_ref[3:4, :]
        sp = r_ref[7:8, :]
        cx = cx_ref[...]
        ext[0:8, :] = jnp.where(ti > 0, hx_ref[...], 0.0)
        ext[8:8 + TS, :] = cx
        x3, x2, x1 = ext[pl.ds(5, TS), :], ext[pl.ds(6, TS), :], ext[pl.ds(7, TS), :]
        xc = x3 * w0 + x2 * w1 + x1 * w2 + cx * w3 + r_ref[4:5, :]
        wa_, wx_ = wa_ref[...], wx_ref[...]
        r, ig, a, mult = _lru_gates(xc, wa_, wx_, r_ref[5:6, :], r_ref[6:7, :], sp)
        cg, dy, hv = cg_ref[...], dy_ref[...], h_ref[...]
        dp_ref[:, BR:2 * BR] = (dy * hv * _dsilu(cg)).astype(BF16)
        aext[0:TS, :] = a
        aext[TS:TS + 8, :] = jnp.broadcast_to(carry[0:1, :], (8, BR))
        sa[0:TS, :] = aext[pl.ds(1, TS), :]
        sb[0:TS, :] = dy * _silu(cg)
        d = 1
        while d < TS:
            A, B = sa[0:TS, :], sb[0:TS, :]
            As, Bs = sa[pl.ds(d, TS), :], sb[pl.ds(d, TS), :]
            sb[0:TS, :] = B + A * Bs
            sa[0:TS, :] = A * As
            d *= 2
        gh = sb[0:TS, :] + sa[0:TS, :] * carry[1:2, :]
        carry[0:1, :] = a[0:1, :]
        carry[1:2, :] = gh[0:1, :]
        hext[0:8, :] = jnp.where(ti > 0, hh_ref[...], 0.0)
        hext[8:8 + TS, :] = hv
        da = gh * hext[pl.ds(7, TS), :]
        dmult = gh * ig * xc
        dig = gh * mult * xc
        dxc = gh * mult * ig
        dla = da * a - dmult * a * a / mult
        dpr = dla * (-LRU_C) * sp * r * (1.0 - r)
        dpi = dig * ig * (1.0 - ig)
        xb, dprb, dpib = xc.astype(BF16), dpr.astype(BF16), dpi.astype(BF16)
        dwa_ref[...] += _dot_tn(xb, dprb)
        dwx_ref[...] += _dot_tn(xb, dpib)
        dxc = dxc + _dot_nt(dprb, wa_) + _dot_nt(dpib, wx_)
        sum_ref[0:1, :] += _colsum(dxc * x3)
        sum_ref[1:2, :] += _colsum(dxc * x2)
        sum_ref[2:3, :] += _colsum(dxc * x1)
        sum_ref[3:4, :] += _colsum(dxc * cx)
        sum_ref[4:5, :] += _colsum(dxc)
        sum_ref[5:6, :] += _colsum(dpr)
        sum_ref[6:7, :] += _colsum(dpi)
        sum_ref[7:8, :] += _colsum(dla * (-LRU_C) * r)
        dext[0:TS, :] = dxc
        dcx = w3 * dxc + w2 * dext[pl.ds(1, TS), :] + w1 * dext[pl.ds(2, TS), :] + w0 * dext[pl.ds(3, TS), :]
        dext[TS:TS + 8, :] = dxc[0:8, :]
        dp_ref[:, 0:BR] = dcx.astype(BF16)

    def rev(k):
        return pl.BlockSpec((TS, BR), lambda i: (nt - 1 - i, k))

    def rev_halo(k):
        return pl.BlockSpec((8, BR), lambda i: (jnp.maximum((nt - 1 - i) * (TS // 8) - 1, 0), k))

    full = pl.BlockSpec((BR, BR), lambda i: (0, 0))
    rspec = pl.BlockSpec((8, BR), lambda i: (0, 0))
    return _call(
        body, name="lru_bwd", grid=(nt,),
        in_specs=[rev(8), rev_halo(8), rev(9), rev(0), rev_halo(0), rev(2), rspec, full, full, pl.BlockSpec(memory_space=pl.ANY)],
        out_specs=[pl.BlockSpec((TS, 2 * BR), lambda i: (nt - 1 - i, 4)), rspec, full, full],
        input_output_aliases={9: 0},
        out_shape=[jax.ShapeDtypeStruct(dproj.shape, BF16), jax.ShapeDtypeStruct((8, BR), F32),
                   jax.ShapeDtypeStruct((BR, BR), F32), jax.ShapeDtypeStruct((BR, BR), F32)],
        scratch_shapes=[pltpu.VMEM((TS + 8, BR), F32)] * 4 + [pltpu.VMEM((TS + PAD, BR), F32)] * 2 + [pltpu.VMEM((8, BR), F32)],
        compiler_params=_params("arbitrary"),
    )(proj, proj, proj, h, h, dcat, rows, wa, wx, dproj)


def _s5_bwd(proj, dcat, xre_all, xim_all, a8, cl, bbre, bbim, cre, cim, rows, wg, dproj, BR):
    S = proj.shape[0]
    W = a8.shape[1]
    TS = _tile(S, TILE_STREAM)
    nt = S // TS
    CB, SB, J = S5_CB, S5_SB, S5_LAGS
    nblk = BR // CB

    def body(u_ref, dg_ref, dy_ref, xre_ref, xim_ref, hre_ref, him_ref, a8_ref, cl_ref, bbre_ref, bbim_ref, cre_ref, cim_ref,
             r_ref, wg_ref, _, dp_ref, sum_ref, dwg_ref, da_ref, dbbre_ref, dbbim_ref, dcre_ref, dcim_ref,
             sre, sim, ere, eim, ypre_s, dext, dylag, carry):
        i = pl.program_id(0)
        ti = nt - 1 - i

        @pl.when(i == 0)
        def _():
            dext[TS:TS + 8, :] = jnp.zeros((8, BR), F32)
            carry[...] = jnp.zeros_like(carry)
            for ref in (sum_ref, dwg_ref, da_ref, dbbre_ref, dbbim_ref, dcre_ref, dcim_ref):
                ref[...] = jnp.zeros_like(ref)

        u, dg, dy = u_ref[...], dg_ref[...], dy_ref[...]
        for m in range(nblk):
            cs, ws = slice(m * CB, (m + 1) * CB), slice(m * SB, (m + 1) * SB)
            ypre_s[:, cs] = (_dot(xre_ref[:, ws].astype(BF16), cre_ref[ws, :]) - _dot(xim_ref[:, ws].astype(BF16), cim_ref[ws, :]))
        ypre = ypre_s[...] + r_ref[0:1, :] * u
        yg = _gelu(ypre)
        wg_ = wg_ref[...]
        s = _sigmoid(_dot(yg.astype(BF16), wg_) + r_ref[1:2, :])
        dgl = dy * _silu(dg)
        dp_ref[:, BR:2 * BR] = (dy * yg * s * _dsilu(dg)).astype(BF16)
        dps = dgl * yg * s * (1.0 - s)
        dpsb = dps.astype(BF16)
        sum_ref[1:2, :] += _colsum(dps)
        dwg_ref[...] += _dot_tn(yg.astype(BF16), dpsb)
        dyp = (dgl * s + _dot_nt(dpsb, wg_)) * _dgelu(ypre)
        sum_ref[0:1, :] += _colsum(dyp * u)
        dext[0:TS, :] = dyp
        for m in range(nblk):
            cs, ws = slice(m * CB, (m + 1) * CB), slice(m * SB, (m + 1) * SB)
            dypb = dext[0:TS, cs].astype(BF16)
            xre, xim = xre_ref[:, ws], xim_ref[:, ws]
            dcre_ref[ws, :] += _dot_tn(xre.astype(BF16), dypb)
            dcim_ref[ws, :] -= _dot_tn(xim.astype(BF16), dypb)
            for j in range(J):
                dylag[:, j * CB:(j + 1) * CB] = dext[pl.ds(j, TS), cs].astype(BF16)
            w = _dot(dylag[...], cl_ref[m])
            sre[...] = w[:, 0:SB]
            sim[...] = w[:, SB:2 * SB]
            ar = jnp.broadcast_to(a8_ref[0:1, ws], (8, SB))
            ai = jnp.broadcast_to(a8_ref[1:2, ws], (8, SB))
            gr, gi = carry[0:8, ws], carry[8:16, ws]
            for g in reversed(range(TS // 8)):
                rg = slice(8 * g, 8 * g + 8)
                gr, gi = sre[rg, :] + ar * gr + ai * gi, sim[rg, :] + ar * gi - ai * gr
                sre[rg, :] = gr
                sim[rg, :] = gi
            carry[0:8, ws] = gr
            carry[8:16, ws] = gi
            gre, gim = sre[...], sim[...]
            ere[0:8, :] = jnp.where(ti > 0, hre_ref[:, ws], 0.0)
            eim[0:8, :] = jnp.where(ti > 0, him_ref[:, ws], 0.0)
            ere[8:8 + TS, :] = xre
            eim[8:8 + TS, :] = xim
            xpr, xpi = ere[pl.ds(7, TS), :], eim[pl.ds(7, TS), :]
            da_ref[0:1, ws] += _colsum(gre * xpr + gim * xpi)
            da_ref[1:2, ws] += _colsum(gim * xpr - gre * xpi)
            greb, gimb = gre.astype(BF16), gim.astype(BF16)
            ub = u_ref[:, cs].astype(BF16)
            dbbre_ref[cs, :] += _dot_tn(ub, greb)
            dbbim_ref[cs, :] += _dot_tn(ub, gimb)
            du = _dot_nt(greb, bbre_ref[cs, :]) + _dot_nt(gimb, bbim_ref[cs, :]) + r_ref[0:1, cs] * dext[0:TS, cs]
            dp_ref[:, cs] = du.astype(BF16)
        dext[TS:TS + 8, :] = dext[0:8, :]

    def rev(width, k):
        return pl.BlockSpec((TS, width), lambda i: (nt - 1 - i, k))

    def rev_halo(width):
        return pl.BlockSpec((8, width), lambda i: (jnp.maximum((nt - 1 - i) * (TS // 8) - 1, 0), 0))

    def const(shape):
        return pl.BlockSpec(shape, lambda i: (0,) * len(shape))

    return _call(
        body, name="s5_bwd", grid=(nt,),
        in_specs=[rev(BR, 10), rev(BR, 11), rev(BR, 3), rev(W, 0), rev(W, 0), rev_halo(W), rev_halo(W),
                  const((8, W)), const((nblk, J * CB, 2 * SB)), const((BR, SB)), const((BR, SB)), const((W, CB)), const((W, CB)),
                  const((8, BR)), const((BR, BR)), pl.BlockSpec(memory_space=pl.ANY)],
        input_output_aliases={15: 0},
        out_specs=[pl.BlockSpec((TS, 2 * BR), lambda i: (nt - 1 - i, 5)), const((8, BR)), const((BR, BR)), const((8, W)),
                   const((BR, SB)), const((BR, SB)), const((W, CB)), const((W, CB))],
        out_shape=[jax.ShapeDtypeStruct(dproj.shape, BF16), jax.ShapeDtypeStruct((8, BR), F32), jax.ShapeDtypeStruct((BR, BR), F32),
                   jax.ShapeDtypeStruct((8, W), F32), jax.ShapeDtypeStruct((BR, SB), F32), jax.ShapeDtypeStruct((BR, SB), F32),
                   jax.ShapeDtypeStruct((W, CB), F32), jax.ShapeDtypeStruct((W, CB), F32)],
        scratch_shapes=[pltpu.VMEM((TS, SB), F32)] * 2 + [pltpu.VMEM((TS + 8, SB), F32)] * 2
        + [pltpu.VMEM((TS, BR), F32), pltpu.VMEM((TS + 8, BR), F32), pltpu.VMEM((TS, J * CB), BF16), pltpu.VMEM((16, W), F32)],
        compiler_params=_params("arbitrary"),
    )(proj, proj, dcat, xre_all, xim_all, xre_all, xim_all, a8, cl, bbre, bbim, cre, cim, rows, wg, dproj)


_WEIGHTS = ['rel_bias', 'w_ada', 'b_ada', 'w_in', 'conv_a', 'conv_c', 'conv_c_b', 'lru_wa', 'lru_ba', 'lru_wx', 'lru_bx',
            'lru_lambda', 's5_lam_re', 's5_lam_im', 's5_log_dt', 's5_b_re', 's5_b_im', 's5_c_re', 's5_c_im', 's5_d',
            's5_w_glu', 's5_b_glu', 'w_out', 'ln_g', 'ln_b']
_LAYER_SMALL = ['conv_a', 'conv_c', 'conv_c_b', 'lru_wa', 'lru_ba', 'lru_wx', 'lru_bx', 'lru_lambda', 's5_lam_re', 's5_lam_im',
                's5_log_dt', 's5_b_re', 's5_b_im', 's5_c_re', 's5_c_im', 's5_d', 's5_b_glu']
_SMALL = ['rel_bias'] + _LAYER_SMALL + ['ln_g', 'ln_b']


def _t5_bucket(dist):
    max_exact = REL_BUCKETS // 2
    nf = jnp.maximum(dist, 1).astype(F32)
    large = max_exact + (jnp.log(nf / max_exact) / math.log(REL_MAX_DIST / max_exact)
                         * (REL_BUCKETS - max_exact)).astype(jnp.int32)
    large = jnp.minimum(large, REL_BUCKETS - 1)
    return jnp.where(dist < max_exact, dist, large)


def _bias_tables(rel_bias):
    i = jnp.arange(BLK)[:, None]
    j = jnp.arange(2 * BLK)[None, :]
    delta = i + BLK - j
    out = []
    for window, dil in DILATIONS:
        bucket = _t5_bucket(jnp.clip(delta, 0, window // dil) * dil)
        onehot = (bucket[:, :, None] == jnp.arange(REL_BUCKETS)[None, None, :]).astype(F32)
        out.append(jnp.einsum('ijb,bh->hij', onehot, rel_bias, precision=lax.Precision.HIGHEST))
    return jnp.stack(out)


def _prep_layer(q):
    eye8 = jnp.eye(8, dtype=F32)
    G = q['s5_lam_re'].shape[0]
    nblk = G // 8

    def block_diag(w):
        hd = w.shape[1]
        return (w[:, :, None, :] * eye8[:, None, :, None]).reshape(8 * hd, 8 * hd)

    def compact_b(bb):
        t = jnp.transpose(bb.reshape(nblk, 8, S5_STATE, S5_CH), (0, 1, 3, 2))
        return (t[:, :, :, None, :] * eye8[None, :, None, :, None]).reshape(G * S5_CH, 8 * S5_STATE)

    def compact_c(cc):
        t = jnp.transpose(cc.reshape(nblk, 8, S5_CH, S5_STATE), (0, 1, 3, 2))
        return (t[:, :, :, None, :] * eye8[None, :, None, :, None]).reshape(G * S5_STATE, 8 * S5_CH)

    lam_re, lam_im = q['s5_lam_re'], q['s5_lam_im']
    dt = jnp.exp(q['s5_log_dt'])[:, None]
    mag = jnp.exp(lam_re * dt)
    ab_re = mag * jnp.cos(lam_im * dt)
    ab_im = mag * jnp.sin(lam_im * dt)
    den = lam_re * lam_re + lam_im * lam_im
    f_re = ((ab_re - 1.0) * lam_re + ab_im * lam_im) / den
    f_im = (ab_im * lam_re - (ab_re - 1.0) * lam_im) / den
    b_re, b_im = q['s5_b_re'], q['s5_b_im']
    bb_re = f_re[..., None] * b_re - f_im[..., None] * b_im
    bb_im = f_re[..., None] * b_im + f_im[..., None] * b_re
    return dict(
        conv_a=q['conv_a'], conv_c=q['conv_c'], conv_c_b=q['conv_c_b'], lru_ba=q['lru_ba'], lru_bx=q['lru_bx'],
        sp=jax.nn.softplus(-q['lru_lambda']), wa=block_diag(q['lru_wa']), wx=block_diag(q['lru_wx']),
        ar=ab_re.reshape(-1), ai=ab_im.reshape(-1), bbre=compact_b(bb_re), bbim=compact_b(bb_im),
        cre=compact_c(q['s5_c_re']), cim=compact_c(q['s5_c_im']), s5_d=q['s5_d'], s5_b_glu=q['s5_b_glu'])


def _s5_lag_weights(kp):
    J, CB, SB = S5_LAGS, S5_CB, S5_SB
    ar, ai = kp['ar'], kp['ai']
    W = ar.shape[0]
    nblk = W // SB
    cr, ci = jnp.ones_like(ar), jnp.zeros_like(ai)
    pows = []
    for _ in range(J + 1):
        pows.append((cr, ci))
        cr, ci = cr * ar - ci * ai, cr * ai + ci * ar
    pwr = jnp.transpose(jnp.stack([p[0] for p in pows[:J]]).reshape(J, nblk, SB), (1, 0, 2))[:, :, None, :]
    pwi = jnp.transpose(jnp.stack([p[1] for p in pows[:J]]).reshape(J, nblk, SB), (1, 0, 2))[:, :, None, :]

    def lagged(re, im, sign):
        out = jnp.concatenate([re * pwr - sign * im * pwi, sign * re * pwi + im * pwr], axis=-1)
        return out.reshape(nblk, J * CB, 2 * SB).astype(BF16)

    bbl = lagged(kp['bbre'].reshape(nblk, 1, CB, SB), kp['bbim'].reshape(nblk, 1, CB, SB), 1.0)
    c0r = jnp.transpose(kp['cre'].reshape(nblk, SB, CB), (0, 2, 1))[:, None]
    c0i = -jnp.transpose(kp['cim'].reshape(nblk, SB, CB), (0, 2, 1))[:, None]
    cl = lagged(c0r, c0i, -1.0)
    return _rows8([pows[J][0], pows[J][1]], W), bbl, cl


def _rows8(vecs, width):
    rows = [v.reshape(1, width).astype(F32) for v in vecs]
    return jnp.concatenate(rows + [jnp.zeros((8 - len(rows), width), F32)], axis=0)


def kernel(x, c, rel_bias, w_ada, b_ada, w_in, conv_a, conv_c, conv_c_b, lru_wa, lru_ba, lru_wx, lru_bx, lru_lambda, s5_lam_re, s5_lam_im, s5_log_dt, s5_b_re, s5_b_im, s5_c_re, s5_c_im, s5_d, s5_w_glu, s5_b_glu, w_out, ln_g, ln_b, loss_target, m_rel_bias, m_w_ada, m_b_ada, m_w_in, m_conv_a, m_conv_c, m_conv_c_b, m_lru_wa, m_lru_ba, m_lru_wx, m_lru_bx, m_lru_lambda, m_s5_lam_re, m_s5_lam_im, m_s5_log_dt, m_s5_b_re, m_s5_b_im, m_s5_c_re, m_s5_c_im, m_s5_d, m_s5_w_glu, m_s5_b_glu, m_w_out, m_ln_g, m_ln_b, v_rel_bias, v_w_ada, v_b_ada, v_w_in, v_conv_a, v_conv_c, v_conv_c_b, v_lru_wa, v_lru_ba, v_lru_wx, v_lru_bx, v_lru_lambda, v_s5_lam_re, v_s5_lam_im, v_s5_log_dt, v_s5_b_re, v_s5_b_im, v_s5_c_re, v_s5_c_im, v_s5_d, v_s5_w_glu, v_s5_b_glu, v_w_out, v_ln_g, v_ln_b):
    a = dict(locals())
    me = 4 * lax.axis_index("x") + 2 * lax.axis_index("y") + lax.axis_index("c")
    x0, target = x[0], loss_target[0]
    S, D = x0.shape
    BR = D // 4
    NS = w_in.shape[2]
    SH = BR // NDEV
    assert S % (BLK * DILATIONS[-1][1]) == 0 and BR % (8 * S5_CH) == 0

    small = _all_gather(_pack([c, conv_a, conv_c]), "gather_small")
    per_dev = [_unpack(small[d], [c.shape, conv_a.shape, conv_c.shape]) for d in range(NDEV)]
    c_all = jnp.concatenate([p[0] for p in per_dev], axis=0)
    conv_a_full = jnp.concatenate([p[1] for p in per_dev], axis=2)
    conv_c_full = jnp.concatenate([p[2] for p in per_dev], axis=2)
    me1 = me.reshape(1).astype(jnp.int32)
    w_in_bf, w_out_bf, w_glu_bf = w_in.astype(BF16), w_out.astype(BF16), s5_w_glu.astype(BF16)

    b_cols = lax.dynamic_slice(b_ada, (0, me * NS), (DEPTH, NS)).reshape(DEPTH, 1, NS)
    ada_all = _all_gather(_ada_cols(c_all, w_ada, b_cols), "gather_ada")
    ada_me = lax.dynamic_index_in_dim(ada_all, me, axis=2, keepdims=False)
    ada_me = jnp.transpose(ada_me, (1, 0, 2)).reshape(DEPTH, 3 * D)
    shift, scale, gate = ada_me[:, :D], ada_me[:, D:2 * D], ada_me[:, 2 * D:]

    bias_tabs, bias_pull = jax.vjp(_bias_tables, rel_bias)
    HD = BR // 8
    head_ones = jnp.kron(jnp.eye(8, dtype=F32), jnp.ones((HD, HD), F32))

    saved = []
    xl = x0
    for l in range(DEPTH):
        q = {n: a[n][l] for n in _LAYER_SMALL}
        q['conv_a'], q['conv_c'] = conv_a_full[l], conv_c_full[l]
        kp, pull = jax.vjp(_prep_layer, q)
        a8, bbl, cl = _s5_lag_weights(kp)
        mod = _rows8([1.0 + scale[l], shift[l]], D)
        arows = _rows8(list(kp['conv_a']), BR)
        lrows = _rows8(list(kp['conv_c']) + [kp['conv_c_b'], kp['lru_ba'], kp['lru_bx'], kp['sp']], BR)
        srows = _rows8([kp['s5_d'], kp['s5_b_glu']], BR)
        nxt = min(l + 1, DEPTH - 1)
        orows = _rows8([1.0 + gate[l], ln_g[l], ln_b[l], 1.0 + scale[nxt], shift[nxt]], D)
        wa, wx = kp['wa'].astype(BF16), kp['wx'].astype(BF16)
        s5w = [kp[n].astype(BF16) for n in ('bbre', 'bbim', 'cre', 'cim')]

        hbf = _modulate(xl, mod) if l == 0 else hnext
        proj, w_in_l, w_out_l, w_glu_l = _proj_gather(hbf, w_in_bf[l], [w_out_bf[l], w_glu_bf[l]], me1)
        w_out_l, w_glu_l = w_out_l.reshape(D, D), w_glu_l.reshape(BR, BR)
        ya = _conv_a_fwd(proj, arows, BR)
        os_, ls_ = [], []
        for g, (_, dil) in enumerate(DILATIONS):
            o, lse = _attn_fwd(proj, bias_tabs[g], dil, BR)
            os_.append(o)
            ls_.append(lse)
        yb = _attn_mix_fwd(proj, os_, ls_, BR)
        yc, hs = _lru_fwd(proj, lrows, wa, wx, BR)
        yd, xre, xim = _s5_fwd(proj, a8, bbl, s5w[2], s5w[3], srows, w_glu_l, BR)
        if l < DEPTH - 1:
            xn, y, cat, hnext = _out_ln([ya, yb, yc, yd], w_out_l, xl, orows)
        else:
            xn, y, cat, loss_local = _out_ln([ya, yb, yc, yd], w_out_l, xl, orows, target)
        saved.append(dict(x=xl, proj=proj, hbf=hbf, os=os_, ls=ls_, hs=hs, xre=xre, xim=xim, y=y, cat=cat, pull=pull,
                          mod=mod, arows=arows, lrows=lrows, srows=srows, orows=orows, wa=wa, wx=wx, s5w=s5w,
                          qw=(a8, cl), w_in=w_in_l, w_out=w_out_l, w_glu=w_glu_l))
        xl = xn

    dout = xl

    dbias = jnp.zeros_like(bias_tabs)
    lgrads, dada, dw_in_parts, dw_out_parts, glu_parts = [None] * DEPTH, [None] * DEPTH, [None] * DEPTH, [None] * DEPTH, [None] * DEPTH
    for l in reversed(range(DEPTH)):
        sv = saved[l]
        proj = sv['proj']
        dxres, dyb, dcat, lnsum = _ln_bwd(dout, sv['x'], sv['y'], sv['orows'], sv['w_out'])
        dw_out_parts[l] = _dw_scatter(sv['cat'], dyb, D // NDEV, True, me1, "dw_out_scatter")
        dproj, asum = _conv_a_bwd(proj, dcat, sv['arows'], BR)
        dos, es, dbg = _attn_mix_bwd(proj, dcat, sv['os'], sv['ls'], head_ones, BR)
        dqkv, dbs = [], []
        for g, (_, dil) in enumerate(DILATIONS):
            dq_, dk_, dv_, db_ = _attn_bwd(proj, dos[g], es[g], sv['ls'][g], bias_tabs[g], dil, BR)
            dqkv.append((dq_, dk_, dv_))
            dbs.append(db_)
        dbias = dbias + jnp.stack(dbs)
        dproj = _attn_dsum(dqkv, dbg, dproj, BR)
        dproj, lsum, dwa, dwx = _lru_bwd(proj, sv['hs'], dcat, sv['lrows'], sv['wa'], sv['wx'], dproj, BR)
        dproj, ssum, dwg, da_, dbbre, dbbim, dcre, dcim = _s5_bwd(
            proj, dcat, sv['xre'], sv['xim'], *sv['qw'], *sv['s5w'], sv['srows'], sv['w_glu'], dproj, BR)
        dkp = dict(conv_a=asum[0:3], conv_c=lsum[0:4], conv_c_b=lsum[4], lru_ba=lsum[5], lru_bx=lsum[6], sp=lsum[7],
                   wa=dwa, wx=dwx, ar=da_[0], ai=da_[1], bbre=dbbre, bbim=dbbim, cre=dcre, cim=dcim,
                   s5_d=ssum[0], s5_b_glu=ssum[1])
        lgrads[l] = dict(sv['pull'](dkp)[0], ln_g=lnsum[0], ln_b=lnsum[1])
        sides = [(dwg.reshape(NDEV, BR // NDEV, BR), False)]
        if l == 0:
            local = {'rel_bias': bias_pull(dbias)[0], 'loss': loss_local.reshape(1)}
            for n in _LAYER_SMALL + ['ln_g', 'ln_b']:
                local[n] = jnp.stack([lgrads[k][n] for k in range(DEPTH)])
            packed = _SMALL + ['loss']
            sides.append((_pack([local[n] for n in packed]), True))
        dout, msum, glu_parts[l], *gathered = _dh(dproj, sv['w_in'], dxres, sv['x'], sv['mod'], sides)
        dw_in_parts[l] = _dw_scatter(sv['hbf'], dproj, NS, False, me1, "dw_in_scatter")
        dada[l] = jnp.concatenate([msum[1], msum[0], lnsum[2]])
    grad_x = dout[None]

    out = {}

    def put(name, res):
        out[name] = res

    put('w_in', _adamw(dw_in_parts, w_in, m_w_in, v_w_in, "adamw_w_in"))
    put('w_out', _adamw(dw_out_parts, w_out, m_w_out, v_w_out, "adamw_w_out"))
    put('s5_w_glu', _adamw(glu_parts, s5_w_glu, m_s5_w_glu, v_s5_w_glu, "adamw_w_glu"))

    dada_all = _all_gather(jnp.stack(dada), "gather_dada")
    dada_cols = jnp.transpose(lax.dynamic_slice(dada_all, (0, 0, me * NS), (NDEV, DEPTH, NS)), (1, 0, 2))
    gw_ada, gb_ada = _ada_grads(c_all, dada_all, dada_cols)
    put('w_ada', _adamw([gw_ada[l] for l in range(DEPTH)], w_ada, m_w_ada, v_w_ada, "adamw_w_ada"))

    shapes = [local[n].shape for n in packed]
    gs = dict(zip(packed, _unpack(_sum8(gathered[0], "sum_small_grads"), shapes)))
    loss = gs['loss'][0]
    gs['conv_a'] = lax.dynamic_slice_in_dim(gs['conv_a'], me * SH, SH, axis=2)
    gs['conv_c'] = lax.dynamic_slice_in_dim(gs['conv_c'], me * SH, SH, axis=2)
    gs['b_ada'] = gb_ada
    names = _SMALL + ['b_ada']
    shapes = [a[n].shape for n in names]
    res = _adamw([_pack([gs[n] for n in names])], _pack([a[n] for n in names])[None], _pack([a['m_' + n] for n in names])[None],
                 _pack([a['v_' + n] for n in names])[None], "adamw_small")
    unpacked = [_unpack(r[0], shapes) for r in res]
    for k, n in enumerate(names):
        put(n, tuple(unpacked[j][k] for j in range(4)))

    return (loss, grad_x, *[out[n][0] for n in _WEIGHTS], *[out[n][1] for n in _WEIGHTS],
            *[out[n][2] for n in _WEIGHTS], *[out[n][3] for n in _WEIGHTS])
```
